```python
import jax
import jax.numpy as jnp
from jax import lax
import numpy as np

D_MODEL = 1024
BATCH = 8
SEQ = 8192
DEPTH = 4


HEAD_DIM = 64
N_MIX_HEADS = D_MODEL // HEAD_DIM
HEADS_A = (3 * N_MIX_HEADS) // 8
HEADS_B = (3 * N_MIX_HEADS) // 8
HEADS_C = N_MIX_HEADS - HEADS_A - HEADS_B
Q_LORA = D_MODEL // 4
KV_LORA = D_MODEL // 8
QK_NOPE = HEAD_DIM
QK_ROPE = HEAD_DIM // 2
V_DIM_A = HEAD_DIM
DILATED_PAIRS = ((128, 1), (512, 4), (2048, 16))
GRID_W = 64
NA_ROWS = 8
NA_COLS = 16
NA_QCOLS = 16
NA_KCOLS = 2 * NA_COLS
D_FF = 4 * D_MODEL
ROPE_THETA = 10000.0
Q_BLOCK = 128
NORM_EPS = 1e-6
NEG_INF = -1e30

COLS_A = Q_LORA + KV_LORA + QK_ROPE
COLS_B = 3 * HEADS_B * HEAD_DIM
COLS_C = 3 * HEADS_C * HEAD_DIM
IN_COLS = COLS_A + COLS_B + COLS_C
WIDTH_A = HEADS_A * V_DIM_A
WIDTH_B = HEADS_B * HEAD_DIM
WIDTH_C = HEADS_C * HEAD_DIM
MIX_WIDTH = WIDTH_A + WIDTH_B + WIDTH_C

kernel_name = "hybrid_mla_dilated_natten_encoder"


def rms_norm(x, g):
    xf = x.astype(jnp.float32)
    y = xf * lax.rsqrt(jnp.mean(xf * xf, axis=-1, keepdims=True) + NORM_EPS)
    return (y * g.astype(jnp.float32)).astype(x.dtype)


def rope(x, pos):
    half = x.shape[-1] // 2
    inv_freq = ROPE_THETA ** (-jnp.arange(half, dtype=jnp.float32) / half)
    ang = pos[:, None] * inv_freq[None, :]
    cos = jnp.cos(ang)[None, :, None, :]
    sin = jnp.sin(ang)[None, :, None, :]
    x1 = x[..., :half].astype(jnp.float32)
    x2 = x[..., half:].astype(jnp.float32)
    return jnp.concatenate([x1 * cos - x2 * sin, x1 * sin + x2 * cos], axis=-1).astype(x.dtype)


def dense_attention(q, k, v):
    b, s, h, dq = q.shape
    scale = dq ** -0.5
    qb = q.reshape(b, s // Q_BLOCK, Q_BLOCK, h, dq).transpose(1, 0, 2, 3, 4)

    def one_block(q_blk):
        sc = jnp.einsum('bqhd,bkhd->bhqk', q_blk, k, preferred_element_type=jnp.float32) * scale
        p = jax.nn.softmax(sc, axis=-1)
        return jnp.einsum('bhqk,bkhd->bqhd', p.astype(v.dtype), v)

    o = lax.map(one_block, qb)
    return o.transpose(1, 0, 2, 3, 4).reshape(b, s, h, v.shape[-1])


def banded_attention(q, k, v, half):
    bq, n, h, d = q.shape
    blk = half
    nb = -(-n // blk)
    n_pad = nb * blk
    qp = jnp.pad(q, ((0, 0), (0, n_pad - n), (0, 0), (0, 0))).reshape(bq, nb, blk, h, d)
    pad_kv = ((0, 0), (blk, n_pad - n + blk), (0, 0), (0, 0))
    kp = jnp.pad(k, pad_kv).reshape(bq, nb + 2, blk, h, d)
    vp = jnp.pad(v, pad_kv).reshape(bq, nb + 2, blk, h, d)
    kw = jnp.concatenate([kp[:, :-2], kp[:, 1:-1], kp[:, 2:]], axis=2)
    vw = jnp.concatenate([vp[:, :-2], vp[:, 1:-1], vp[:, 2:]], axis=2)
    q_idx = jnp.arange(n_pad).reshape(nb, blk)
    k_idx = jnp.arange(nb)[:, None] * blk - blk + jnp.arange(3 * blk)[None, :]
    mask = ((jnp.abs(q_idx[:, :, None] - k_idx[:, None, :]) <= half)
            & (k_idx[:, None, :] >= 0) & (k_idx[:, None, :] < n))
    sc = jnp.einsum('bnqhd,bnkhd->bnhqk', qp, kw, preferred_element_type=jnp.float32) * (d ** -0.5)
    sc = jnp.where(mask[None, :, None], sc, NEG_INF)
    m = jnp.max(sc, axis=-1, keepdims=True)
    p = jnp.exp(sc - m)
    den = jnp.sum(p, axis=-1)
    o = jnp.einsum('bnhqk,bnkhd->bnqhd', p.astype(v.dtype), vw).astype(jnp.float32)
    o = o / den.transpose(0, 1, 3, 2)[..., None]
    lse = (m[..., 0] + jnp.log(den)).transpose(0, 1, 3, 2)
    o = o.reshape(bq, n_pad, h, d)[:, :n]
    lse = lse.reshape(bq, n_pad, h)[:, :n]
    return o, lse


def dilated_sliding_attention(q, k, v):
    b, s, h, d = q.shape
    outs, lses = [], []
    for window, dil in DILATED_PAIRS:
        n = s // dil
        qc = q.reshape(b, n, dil, h, d).transpose(0, 2, 1, 3, 4).reshape(b * dil, n, h, d)
        kc = k.reshape(b, n, dil, h, d).transpose(0, 2, 1, 3, 4).reshape(b * dil, n, h, d)
        vc = v.reshape(b, n, dil, h, d).transpose(0, 2, 1, 3, 4).reshape(b * dil, n, h, d)
        o, lse = banded_attention(qc, kc, vc, window // (2 * dil))
        outs.append(o.reshape(b, dil, n, h, d).transpose(0, 2, 1, 3, 4).reshape(b, s, h, d))
        lses.append(lse.reshape(b, dil, n, h).transpose(0, 2, 1, 3).reshape(b, s, h))
    w = jax.nn.softmax(jnp.stack(lses, axis=-1), axis=-1)
    o = jnp.sum(jnp.stack(outs, axis=-1) * w[:, :, :, None, :], axis=-1)
    return o.astype(q.dtype)


def neighbourhood_attention(q, k, v, rpb):
    b, s, h, d = q.shape
    rows = s // GRID_W
    kr_win = min(NA_ROWS, rows)
    q_rows = kr_win
    k_rows = min(2 * kr_win, rows)
    nrb = -(-rows // q_rows)
    rows_pad = nrb * q_rows
    ncb = GRID_W // NA_QCOLS
    qg = jnp.pad(q.reshape(b, rows, GRID_W, h, d), ((0, 0), (0, rows_pad - rows), (0, 0), (0, 0), (0, 0)))
    qg = qg.reshape(b, nrb, q_rows, ncb, NA_QCOLS, h, d).transpose(0, 1, 3, 2, 4, 5, 6)
    r_q = jnp.arange(rows_pad).reshape(nrb, q_rows)
    r_start = jnp.clip(r_q - kr_win // 2, 0, rows - kr_win)
    c_q = jnp.arange(GRID_W).reshape(ncb, NA_QCOLS)
    c_start = jnp.clip(c_q - NA_COLS // 2, 0, GRID_W - NA_COLS)
    kr_idx = (jnp.clip(jnp.arange(nrb) * q_rows - kr_win // 2, 0, rows - k_rows)[:, None]
              + jnp.arange(k_rows)[None, :])
    kc_idx = (jnp.clip(jnp.arange(ncb) * NA_QCOLS - NA_COLS // 2, 0, GRID_W - NA_KCOLS)[:, None]
              + jnp.arange(NA_KCOLS)[None, :])
    kgrid = k.reshape(b, rows, GRID_W, h, d)
    vgrid = v.reshape(b, rows, GRID_W, h, d)
    ri = kr_idx[:, None, :, None]
    ci = kc_idx[None, :, None, :]
    kg = kgrid[:, ri, ci]
    vg = vgrid[:, ri, ci]
    row_ok = (kr_idx[:, None, :] >= r_start[:, :, None]) & (kr_idx[:, None, :] < r_start[:, :, None] + kr_win)
    col_ok = (kc_idx[:, None, :] >= c_start[:, :, None]) & (kc_idx[:, None, :] < c_start[:, :, None] + NA_COLS)
    dr = jnp.clip(kr_idx[:, None, :] - r_q[:, :, None], -(NA_ROWS - 1), NA_ROWS - 1) + (NA_ROWS - 1)
    dc = jnp.clip(kc_idx[:, None, :] - c_q[:, :, None], -(NA_COLS - 1), NA_COLS - 1) + (NA_COLS - 1)
    bias = rpb[:, dr[:, None, :, None, :, None], dc[None, :, None, :, None, :]]
    mask = row_ok[:, None, :, None, :, None] & col_ok[None, :, None, :, None, :]
    sc = jnp.einsum('bnmiphd,bnmjqhd->bnmhipjq', qg, kg, preferred_element_type=jnp.float32) * (d ** -0.5)
    sc = sc + bias.transpose(1, 2, 0, 3, 4, 5, 6).astype(jnp.float32)[None]
    sc = jnp.where(mask[:, :, None][None], sc, NEG_INF)
    shp = sc.shape
    p = jax.nn.softmax(sc.reshape(shp[:-2] + (k_rows * NA_KCOLS,)), axis=-1).reshape(shp)
    o = jnp.einsum('bnmhipjq,bnmjqhd->bnmiphd', p.astype(v.dtype), vg)
    o = o.transpose(0, 1, 3, 2, 4, 5, 6).reshape(b, rows_pad, GRID_W, h, d)[:, :rows]
    return o.reshape(b, s, h, d)


def _fwd_setup_inputs(seed: int = 0) -> dict:
    key = jax.random.key(seed)
    ks = jax.random.split(key, 16)

    def normal(k, shape, scale):
        return jax.random.normal(k, shape, dtype=jnp.float32) * scale

    def gain(k, shape):
        return 1.0 + 0.02 * jax.random.normal(k, shape, dtype=jnp.float32)

    return {
        "x": normal(ks[0], (BATCH, SEQ, D_MODEL), 1.0),
        "g_mix": gain(ks[1], (DEPTH, D_MODEL)),
        "w_in": normal(ks[2], (DEPTH, D_MODEL, IN_COLS), D_MODEL ** -0.5),
        "q_norm": gain(ks[3], (DEPTH, Q_LORA)),
        "w_uq": normal(ks[4], (DEPTH, Q_LORA, HEADS_A * (QK_NOPE + QK_ROPE)), Q_LORA ** -0.5),
        "kv_norm": gain(ks[5], (DEPTH, KV_LORA)),
        "w_ukv": normal(ks[6], (DEPTH, KV_LORA, HEADS_A * (QK_NOPE + V_DIM_A)), KV_LORA ** -0.5),
        "rpb": normal(ks[7], (DEPTH, HEADS_C, 2 * NA_ROWS - 1, 2 * NA_COLS - 1), 0.1),
        "out_norm_a": gain(ks[8], (DEPTH, WIDTH_A)),
        "out_norm_b": gain(ks[9], (DEPTH, WIDTH_B)),
        "out_norm_c": gain(ks[10], (DEPTH, WIDTH_C)),
        "w_out": normal(ks[11], (DEPTH, MIX_WIDTH, D_MODEL), MIX_WIDTH ** -0.5),
        "g_mlp": gain(ks[12], (DEPTH, D_MODEL)),
        "w_mlp_in": normal(ks[13], (DEPTH, D_MODEL, D_FF), D_MODEL ** -0.5),
        "w_mlp_out": normal(ks[14], (DEPTH, D_FF, D_MODEL), D_FF ** -0.5),
        "g_final": gain(ks[15], (D_MODEL,)),
    }


def _fwd_reference(x, g_mix, w_in, q_norm, w_uq, kv_norm, w_ukv, rpb, out_norm_a, out_norm_b, out_norm_c,
              w_out, g_mlp, w_mlp_in, w_mlp_out, g_final):
    b, s, _ = x.shape
    pos = jnp.arange(s, dtype=jnp.float32)
    for l in range(DEPTH):
        h = rms_norm(x, g_mix[l])
        proj = h @ w_in[l]
        p_a = proj[..., :COLS_A]
        p_b = proj[..., COLS_A:COLS_A + COLS_B]
        p_c = proj[..., COLS_A + COLS_B:]
        c_q = p_a[..., :Q_LORA]
        c_kv = p_a[..., Q_LORA:Q_LORA + KV_LORA]
        k_pe = p_a[..., Q_LORA + KV_LORA:]
        qa = (rms_norm(c_q, q_norm[l]) @ w_uq[l]).reshape(b, s, HEADS_A, QK_NOPE + QK_ROPE)
        kva = (rms_norm(c_kv, kv_norm[l]) @ w_ukv[l]).reshape(b, s, HEADS_A, QK_NOPE + V_DIM_A)
        k_pe = jnp.broadcast_to(rope(k_pe[:, :, None, :], pos), (b, s, HEADS_A, QK_ROPE))
        qa = jnp.concatenate([qa[..., :QK_NOPE], rope(qa[..., QK_NOPE:], pos)], axis=-1)
        ka = jnp.concatenate([kva[..., :QK_NOPE], k_pe], axis=-1)
        o_a = dense_attention(qa, ka, kva[..., QK_NOPE:])
        pb = p_b.reshape(b, s, 3, HEADS_B, HEAD_DIM)
        o_b = dilated_sliding_attention(rope(pb[:, :, 0], pos), rope(pb[:, :, 1], pos), pb[:, :, 2])
        pc = p_c.reshape(b, s, 3, HEADS_C, HEAD_DIM)
        o_c = neighbourhood_attention(pc[:, :, 0], pc[:, :, 1], pc[:, :, 2], rpb[l])
        mixed = jnp.concatenate([
            rms_norm(o_a.reshape(b, s, WIDTH_A), out_norm_a[l]),
            rms_norm(o_b.reshape(b, s, WIDTH_B), out_norm_b[l]),
            rms_norm(o_c.reshape(b, s, WIDTH_C), out_norm_c[l]),
        ], axis=-1)
        x = x + mixed @ w_out[l]
        h2 = rms_norm(x, g_mlp[l])
        x = x + jnp.square(jax.nn.relu(h2 @ w_mlp_in[l])) @ w_mlp_out[l]
    return rms_norm(x, g_final)


import jax as _jax
import jax.numpy as _jnp

TWIN_FORMAT = 'train_step'
FWD_PARAMS = ['x', 'g_mix', 'w_in', 'q_norm', 'w_uq', 'kv_norm', 'w_ukv', 'rpb', 'out_norm_a', 'out_norm_b', 'out_norm_c', 'w_out', 'g_mlp', 'w_mlp_in', 'w_mlp_out', 'g_final']
TWIN_WEIGHTS = ['g_mix', 'w_in', 'q_norm', 'w_uq', 'kv_norm', 'w_ukv', 'rpb', 'out_norm_a', 'out_norm_b', 'out_norm_c', 'w_out', 'g_mlp', 'w_mlp_in', 'w_mlp_out', 'g_final']
TWIN_DIFF_INPUT = 'x'
TWIN_INPUTS = ['x', 'g_mix', 'w_in', 'q_norm', 'w_uq', 'kv_norm', 'w_ukv', 'rpb', 'out_norm_a', 'out_norm_b', 'out_norm_c', 'w_out', 'g_mlp', 'w_mlp_in', 'w_mlp_out', 'g_final', 'loss_target', 'm_g_mix', 'm_w_in', 'm_q_norm', 'm_w_uq', 'm_kv_norm', 'm_w_ukv', 'm_rpb', 'm_out_norm_a', 'm_out_norm_b', 'm_out_norm_c', 'm_w_out', 'm_g_mlp', 'm_w_mlp_in', 'm_w_mlp_out', 'm_g_final', 'v_g_mix', 'v_w_in', 'v_q_norm', 'v_w_uq', 'v_kv_norm', 'v_w_ukv', 'v_rpb', 'v_out_norm_a', 'v_out_norm_b', 'v_out_norm_c', 'v_w_out', 'v_g_mlp', 'v_w_mlp_in', 'v_w_mlp_out', 'v_g_final']
TWIN_OUTPUTS = ['loss', 'grad_x', 'grad_g_mix', 'grad_w_in', 'grad_q_norm', 'grad_w_uq', 'grad_kv_norm', 'grad_w_ukv', 'grad_rpb', 'grad_out_norm_a', 'grad_out_norm_b', 'grad_out_norm_c', 'grad_w_out', 'grad_g_mlp', 'grad_w_mlp_in', 'grad_w_mlp_out', 'grad_g_final', 'delta_g_mix', 'delta_w_in', 'delta_q_norm', 'delta_w_uq', 'delta_kv_norm', 'delta_w_ukv', 'delta_rpb', 'delta_out_norm_a', 'delta_out_norm_b', 'delta_out_norm_c', 'delta_w_out', 'delta_g_mlp', 'delta_w_mlp_in', 'delta_w_mlp_out', 'delta_g_final', 'new_m_g_mix', 'new_m_w_in', 'new_m_q_norm', 'new_m_w_uq', 'new_m_kv_norm', 'new_m_w_ukv', 'new_m_rpb', 'new_m_out_norm_a', 'new_m_out_norm_b', 'new_m_out_norm_c', 'new_m_w_out', 'new_m_g_mlp', 'new_m_w_mlp_in', 'new_m_w_mlp_out', 'new_m_g_final', 'new_v_g_mix', 'new_v_w_in', 'new_v_q_norm', 'new_v_w_uq', 'new_v_kv_norm', 'new_v_w_ukv', 'new_v_rpb', 'new_v_out_norm_a', 'new_v_out_norm_b', 'new_v_out_norm_c', 'new_v_w_out', 'new_v_g_mlp', 'new_v_w_mlp_in', 'new_v_w_mlp_out', 'new_v_g_final']
TWIN_LEAF_KINDS = {'loss': 'loss', 'grad_x': 'grad_x', 'grad_g_mix': 'grad_w', 'grad_w_in': 'grad_w', 'grad_q_norm': 'grad_w', 'grad_w_uq': 'grad_w', 'grad_kv_norm': 'grad_w', 'grad_w_ukv': 'grad_w', 'grad_rpb': 'grad_w', 'grad_out_norm_a': 'grad_w', 'grad_out_norm_b': 'grad_w', 'grad_out_norm_c': 'grad_w', 'grad_w_out': 'grad_w', 'grad_g_mlp': 'grad_w', 'grad_w_mlp_in': 'grad_w', 'grad_w_mlp_out': 'grad_w', 'grad_g_final': 'grad_w', 'delta_g_mix': 'delta_w', 'delta_w_in': 'delta_w', 'delta_q_norm': 'delta_w', 'delta_w_uq': 'delta_w', 'delta_kv_norm': 'delta_w', 'delta_w_ukv': 'delta_w', 'delta_rpb': 'delta_w', 'delta_out_norm_a': 'delta_w', 'delta_out_norm_b': 'delta_w', 'delta_out_norm_c': 'delta_w', 'delta_w_out': 'delta_w', 'delta_g_mlp': 'delta_w', 'delta_w_mlp_in': 'delta_w', 'delta_w_mlp_out': 'delta_w', 'delta_g_final': 'delta_w', 'new_m_g_mix': 'new_m', 'new_m_w_in': 'new_m', 'new_m_q_norm': 'new_m', 'new_m_w_uq': 'new_m', 'new_m_kv_norm': 'new_m', 'new_m_w_ukv': 'new_m', 'new_m_rpb': 'new_m', 'new_m_out_norm_a': 'new_m', 'new_m_out_norm_b': 'new_m', 'new_m_out_norm_c': 'new_m', 'new_m_w_out': 'new_m', 'new_m_g_mlp': 'new_m', 'new_m_w_mlp_in': 'new_m', 'new_m_w_mlp_out': 'new_m', 'new_m_g_final': 'new_m', 'new_v_g_mix': 'new_v', 'new_v_w_in': 'new_v', 'new_v_q_norm': 'new_v', 'new_v_w_uq': 'new_v', 'new_v_kv_norm': 'new_v', 'new_v_w_ukv': 'new_v', 'new_v_rpb': 'new_v', 'new_v_out_norm_a': 'new_v', 'new_v_out_norm_b': 'new_v', 'new_v_out_norm_c': 'new_v', 'new_v_w_out': 'new_v', 'new_v_g_mlp': 'new_v', 'new_v_w_mlp_in': 'new_v', 'new_v_w_mlp_out': 'new_v', 'new_v_g_final': 'new_v'}


def _forward(args):
    return _fwd_reference(*[args[k] for k in FWD_PARAMS])


def _output_shape():
    def fwd():
        inp = _fwd_setup_inputs(0)
        return _fwd_reference(*[inp[k] for k in FWD_PARAMS])
    out = _jax.eval_shape(fwd)
    return out.shape, out.dtype

N_MICROBATCH = 1
ADAM_LR = 0.001
ADAM_B1 = 0.9
ADAM_B2 = 0.999
ADAM_EPS = 1e-08
ADAM_WD = 0.01
ADAM_STEP = 10
PER_EXAMPLE_BATCH_AXIS = {'x': 0, 'loss_target': 0}
SHARED_INPUTS = []
_WEIGHT_DTYPES = {'g_mix': _jnp.float32, 'w_in': _jnp.float32, 'q_norm': _jnp.float32, 'w_uq': _jnp.float32, 'kv_norm': _jnp.float32, 'w_ukv': _jnp.float32, 'rpb': _jnp.float32, 'out_norm_a': _jnp.float32, 'out_norm_b': _jnp.float32, 'out_norm_c': _jnp.float32, 'w_out': _jnp.float32, 'g_mlp': _jnp.float32, 'w_mlp_in': _jnp.float32, 'w_mlp_out': _jnp.float32, 'g_final': _jnp.float32}
MOMENT_SCALE = {'g_mix': 2.936896e-01, 'w_in': 1.985772e-01, 'q_norm': 1.083034e-01, 'w_uq': 7.179866e-02, 'kv_norm': 7.051105e-01, 'w_ukv': 2.313511e-01, 'rpb': 3.355803e-02, 'out_norm_a': 3.351840e-01, 'out_norm_b': 2.982278e-01, 'out_norm_c': 2.873802e-01, 'w_out': 2.978677e-01, 'g_mlp': 1.954255e-01, 'w_mlp_in': 9.449431e-02, 'w_mlp_out': 3.574104e-01, 'g_final': 6.854994e+01}


def _to_microbatches(a, axis):
    t = _jnp.moveaxis(a, axis, 0)
    t = t.reshape((N_MICROBATCH, t.shape[0] // N_MICROBATCH) + t.shape[1:])
    return _jnp.moveaxis(t, 1, axis + 1)


def setup_inputs(seed: int = 0) -> dict:
    inp = _fwd_setup_inputs(seed)
    key = _jax.random.fold_in(_jax.random.key(seed), 7919)
    shape, _ = _output_shape()
    out = dict(inp)
    out["loss_target"] = _jax.random.normal(_jax.random.fold_in(key, 0), shape, _jnp.float32)
    for i, name in enumerate(TWIN_WEIGHTS):
        w = inp[name].astype(_jnp.float32)
        if MOMENT_SCALE is None:
            s = _jnp.sqrt(_jnp.mean(_jnp.square(w)) + 1e-30)
        else:
            s = MOMENT_SCALE[name]
        km, kv = _jax.random.split(_jax.random.fold_in(key, i + 1))
        out[name] = w
        out["m_" + name] = s * _jax.random.normal(km, w.shape, _jnp.float32)
        out["v_" + name] = (s * s) * _jax.random.uniform(kv, w.shape, _jnp.float32, 0.5, 1.5)
    if N_MICROBATCH > 1:
        for name, axis in PER_EXAMPLE_BATCH_AXIS.items():
            out[name] = _to_microbatches(out[name], axis)
    return {'x': out['x'], 'g_mix': out['g_mix'], 'w_in': out['w_in'], 'q_norm': out['q_norm'], 'w_uq': out['w_uq'], 'kv_norm': out['kv_norm'], 'w_ukv': out['w_ukv'], 'rpb': out['rpb'], 'out_norm_a': out['out_norm_a'], 'out_norm_b': out['out_norm_b'], 'out_norm_c': out['out_norm_c'], 'w_out': out['w_out'], 'g_mlp': out['g_mlp'], 'w_mlp_in': out['w_mlp_in'], 'w_mlp_out': out['w_mlp_out'], 'g_final': out['g_final'], 'loss_target': out['loss_target'], 'm_g_mix': out['m_g_mix'], 'm_w_in': out['m_w_in'], 'm_q_norm': out['m_q_norm'], 'm_w_uq': out['m_w_uq'], 'm_kv_norm': out['m_kv_norm'], 'm_w_ukv': out['m_w_ukv'], 'm_rpb': out['m_rpb'], 'm_out_norm_a': out['m_out_norm_a'], 'm_out_norm_b': out['m_out_norm_b'], 'm_out_norm_c': out['m_out_norm_c'], 'm_w_out': out['m_w_out'], 'm_g_mlp': out['m_g_mlp'], 'm_w_mlp_in': out['m_w_mlp_in'], 'm_w_mlp_out': out['m_w_mlp_out'], 'm_g_final': out['m_g_final'], 'v_g_mix': out['v_g_mix'], 'v_w_in': out['v_w_in'], 'v_q_norm': out['v_q_norm'], 'v_w_uq': out['v_w_uq'], 'v_kv_norm': out['v_kv_norm'], 'v_w_ukv': out['v_w_ukv'], 'v_rpb': out['v_rpb'], 'v_out_norm_a': out['v_out_norm_a'], 'v_out_norm_b': out['v_out_norm_b'], 'v_out_norm_c': out['v_out_norm_c'], 'v_w_out': out['v_w_out'], 'v_g_mlp': out['v_g_mlp'], 'v_w_mlp_in': out['v_w_mlp_in'], 'v_w_mlp_out': out['v_w_mlp_out'], 'v_g_final': out['v_g_final']}


def _loss(weights, diff, rest, loss_target):
    with _jax.named_scope("forward"):
        args = {**rest, TWIN_DIFF_INPUT: diff, **{k: w.astype(_WEIGHT_DTYPES[k]) for k, w in weights.items()}}
        y = _forward(args)
    with _jax.named_scope("loss_head"):
        err = _jnp.square(y.astype(_jnp.float32) - loss_target)
        return 0.5 * _jnp.sum(_jnp.mean(err, axis=-1)) if err.ndim else 0.5 * err


def _adamw(w, g, m, v):
    m = ADAM_B1 * m + (1.0 - ADAM_B1) * g
    v = ADAM_B2 * v + (1.0 - ADAM_B2) * _jnp.square(g)
    m_hat = m / (1.0 - ADAM_B1 ** ADAM_STEP)
    v_hat = v / (1.0 - ADAM_B2 ** ADAM_STEP)
    delta = -ADAM_LR * (m_hat / (_jnp.sqrt(v_hat) + ADAM_EPS) + ADAM_WD * w)
    return delta, m, v


def reference(x, g_mix, w_in, q_norm, w_uq, kv_norm, w_ukv, rpb, out_norm_a, out_norm_b, out_norm_c, w_out, g_mlp, w_mlp_in, w_mlp_out, g_final, loss_target, m_g_mix, m_w_in, m_q_norm, m_w_uq, m_kv_norm, m_w_ukv, m_rpb, m_out_norm_a, m_out_norm_b, m_out_norm_c, m_w_out, m_g_mlp, m_w_mlp_in, m_w_mlp_out, m_g_final, v_g_mix, v_w_in, v_q_norm, v_w_uq, v_kv_norm, v_w_ukv, v_rpb, v_out_norm_a, v_out_norm_b, v_out_norm_c, v_w_out, v_g_mlp, v_w_mlp_in, v_w_mlp_out, v_g_final):
    given = dict(x=x, g_mix=g_mix, w_in=w_in, q_norm=q_norm, w_uq=w_uq, kv_norm=kv_norm, w_ukv=w_ukv, rpb=rpb, out_norm_a=out_norm_a, out_norm_b=out_norm_b, out_norm_c=out_norm_c, w_out=w_out, g_mlp=g_mlp, w_mlp_in=w_mlp_in, w_mlp_out=w_mlp_out, g_final=g_final, loss_target=loss_target, m_g_mix=m_g_mix, m_w_in=m_w_in, m_q_norm=m_q_norm, m_w_uq=m_w_uq, m_kv_norm=m_kv_norm, m_w_ukv=m_w_ukv, m_rpb=m_rpb, m_out_norm_a=m_out_norm_a, m_out_norm_b=m_out_norm_b, m_out_norm_c=m_out_norm_c, m_w_out=m_w_out, m_g_mlp=m_g_mlp, m_w_mlp_in=m_w_mlp_in, m_w_mlp_out=m_w_mlp_out, m_g_final=m_g_final, v_g_mix=v_g_mix, v_w_in=v_w_in, v_q_norm=v_q_norm, v_w_uq=v_w_uq, v_kv_norm=v_kv_norm, v_w_ukv=v_w_ukv, v_rpb=v_rpb, v_out_norm_a=v_out_norm_a, v_out_norm_b=v_out_norm_b, v_out_norm_c=v_out_norm_c, v_w_out=v_w_out, v_g_mlp=v_g_mlp, v_w_mlp_in=v_w_mlp_in, v_w_mlp_out=v_w_mlp_out, v_g_final=v_g_final)
    weights = {n: given[n] for n in TWIN_WEIGHTS}
    shared = {n: given[n] for n in SHARED_INPUTS}
    per_example = {n: given[n] for n in ['x']}
    grad_fn = _jax.value_and_grad(_loss, argnums=(0, 1))

    def one_microbatch(ex, loss_target):
        ex = dict(ex)
        diff = ex.pop(TWIN_DIFF_INPUT)
        return grad_fn(weights, diff, {**shared, **ex}, loss_target)

    if N_MICROBATCH == 1:
        loss, (grad_w, grad_x) = one_microbatch(per_example, given["loss_target"])
    else:
        def body(carry, xs):
            loss_sum, grad_sum = carry
            l_k, (gw_k, gx_k) = one_microbatch(xs[0], xs[1])
            with _jax.named_scope("update"):
                return (loss_sum + l_k, _jax.tree.map(_jnp.add, grad_sum, gw_k)), gx_k

        init = (_jnp.zeros((), _jnp.float32), _jax.tree.map(_jnp.zeros_like, weights))
        (loss, grad_w), grad_x = _jax.lax.scan(body, init, (per_example, given["loss_target"]))
    with _jax.named_scope("update"):
        delta_w, new_m, new_v = {}, {}, {}
        for n in TWIN_WEIGHTS:
            delta_w[n], new_m[n], new_v[n] = _adamw(weights[n], grad_w[n], given["m_" + n], given["v_" + n])
    return (loss, grad_x, *[grad_w[n] for n in TWIN_WEIGHTS], *[delta_w[n] for n in TWIN_WEIGHTS],
            *[new_m[n] for n in TWIN_WEIGHTS], *[new_v[n] for n in TWIN_WEIGHTS])
```

```python
import functools

import numpy as np
import jax
import jax.numpy as jnp
from jax import lax
from jax.experimental import pallas as pl
from jax.experimental.pallas import tpu as pltpu

F32 = jnp.float32
BF16 = jnp.bfloat16

D_MODEL = 1024
HEAD_DIM = 64
Q_LORA = 256
KV_LORA = 128
QK_ROPE = 32
HEADS_A = 6
HEADS_B = 6
HEADS_C = 4
DILATED_PAIRS = ((128, 1), (512, 4), (2048, 16))
BAND_HALF = 64
GRID_W = 64
NA_ROWS = 8
NA_COLS = 16
D_FF = 4096
ROPE_THETA = 10000.0
NORM_EPS = 1e-6
NEG_INF = -1e30
DEPTH = 4

LANE = 128
PROJ_W = 2432
COL_CKV = 256
COL_KPE = 384
COL_B = 512
COL_C = 1664
W_A2 = 768
WIDTH_AB = 384
WIDTH_C = 256
SCALE_A = (HEAD_DIM + QK_ROPE) ** -0.5
SCALE_BC = HEAD_DIM ** -0.5

ADAM_LR = 0.001
ADAM_B1 = 0.9
ADAM_B2 = 0.999
ADAM_EPS = 1e-08
ADAM_WD = 0.01
ADAM_STEP = 10

VMEM_LIMIT = 56 * 1024 * 1024
MESH_T = pl.DeviceIdType.MESH


def _cp(*sem):
    return pltpu.CompilerParams(dimension_semantics=sem or None, vmem_limit_bytes=VMEM_LIMIT)


def _tile(n, cands):
    for c in cands:
        if n % c == 0:
            return c
    return n


def _sds(shape, dtype):
    return jax.ShapeDtypeStruct(shape, dtype)


def _mm_nn(a, b, *, name, out_dtype=F32, res=None):
    m, k = a.shape
    n = b.shape[1]
    tm = _tile(m, (512, 256, 128))
    tn = _tile(n, (1024, 768, 512)) if n % LANE == 0 and n != PROJ_W else n

    def body(*refs):
        a_ref, b_ref = refs[0], refs[1]
        o_ref = refs[-1]
        acc = jnp.dot(a_ref[...], b_ref[...], preferred_element_type=F32)
        if res is not None:
            acc = refs[2][...] + acc
        o_ref[...] = acc.astype(o_ref.dtype)

    in_specs = [pl.BlockSpec((tm, k), lambda j, i: (i, 0)), pl.BlockSpec((k, tn), lambda j, i: (0, j))]
    args = [a, b]
    if res is not None:
        in_specs.append(pl.BlockSpec((tm, tn), lambda j, i: (i, j)))
        args.append(res)
    return pl.pallas_call(
        body, name=name, grid=(n // tn, m // tm), in_specs=in_specs,
        out_specs=pl.BlockSpec((tm, tn), lambda j, i: (i, j)), out_shape=_sds((m, n), out_dtype),
        compiler_params=_cp("parallel", "parallel"))(*args)


def _mm_mlp_in(h, w):
    m, k = h.shape
    n = w.shape[1]
    tm = _tile(m, (512, 256, 128))
    tn = _tile(n, (1024, 512))

    def body(a_ref, b_ref, u_ref, act_ref):
        u = jnp.dot(a_ref[...], b_ref[...], preferred_element_type=F32)
        u_ref[...] = u
        act_ref[...] = jnp.square(jnp.maximum(u, 0.0)).astype(BF16)

    spec_o = pl.BlockSpec((tm, tn), lambda j, i: (i, j))
    return pl.pallas_call(
        body, name="mlp_in", grid=(n // tn, m // tm),
        in_specs=[pl.BlockSpec((tm, k), lambda j, i: (i, 0)), pl.BlockSpec((k, tn), lambda j, i: (0, j))],
        out_specs=(spec_o, spec_o), out_shape=(_sds((m, n), F32), _sds((m, n), BF16)),
        compiler_params=_cp("parallel", "parallel"))(h, w)


def _mm_nt(a, b, *, name, out_dtype=F32, relu2_of=None):
    m, c = a.shape
    n = b.shape[0]
    tm = _tile(m, (512, 256, 128))
    tn = _tile(n, (1024, 512, 256, 128))

    def body(*refs):
        a_ref, b_ref = refs[0], refs[1]
        o_ref = refs[-1]
        acc = lax.dot_general(a_ref[...], b_ref[...], (((1,), (1,)), ((), ())), preferred_element_type=F32)
        if relu2_of is not None:
            acc = acc * (2.0 * jnp.maximum(refs[2][...], 0.0))
        o_ref[...] = acc.astype(o_ref.dtype)

    in_specs = [pl.BlockSpec((tm, c), lambda j, i: (i, 0)), pl.BlockSpec((tn, c), lambda j, i: (j, 0))]
    args = [a, b]
    if relu2_of is not None:
        in_specs.append(pl.BlockSpec((tm, tn), lambda j, i: (i, j)))
        args.append(relu2_of)
    return pl.pallas_call(
        body, name=name, grid=(n // tn, m // tm), in_specs=in_specs,
        out_specs=pl.BlockSpec((tm, tn), lambda j, i: (i, j)), out_shape=_sds((m, n), out_dtype),
        compiler_params=_cp("parallel", "parallel"))(*args)


def _mm_tn(a, b, *, name):
    m, ka = a.shape
    nb = b.shape[1]
    tka = _tile(ka, (512, 256, 128))
    tnb = _tile(nb, (1024, 768, 512)) if nb != PROJ_W else nb
    tc = _tile(m, (1024, 512, 256, 128))

    def body(a_ref, b_ref, o_ref):
        part = lax.dot_general(a_ref[...], b_ref[...], (((0,), (0,)), ((), ())), preferred_element_type=F32)

        @pl.when(pl.program_id(2) == 0)
        def _():
            o_ref[...] = part

        @pl.when(pl.program_id(2) != 0)
        def _():
            o_ref[...] += part

    return pl.pallas_call(
        body, name=name, grid=(ka // tka, nb // tnb, m // tc),
        in_specs=[pl.BlockSpec((tc, tka), lambda i, j, c: (c, i)), pl.BlockSpec((tc, tnb), lambda i, j, c: (c, j))],
        out_specs=pl.BlockSpec((tka, tnb), lambda i, j, c: (i, j)), out_shape=_sds((ka, nb), F32),
        compiler_params=_cp("parallel", "parallel", "arbitrary"))(a, b)


def _rstd(x):
    return lax.rsqrt(jnp.mean(x * x, axis=-1, keepdims=True) + NORM_EPS)


def _rms_bwd_rows(x, g, dy):
    r = _rstd(x)
    gy = dy * g
    c = jnp.sum(x * gy, axis=-1, keepdims=True) * (r * r * r) * (1.0 / x.shape[-1])
    return r * gy - x * c, jnp.sum(dy * x * r, axis=0, keepdims=True)


def _accum(ref, part, first):
    @pl.when(first)
    def _():
        ref[...] = part

    @pl.when(jnp.logical_not(first))
    def _():
        ref[...] += part


def _rope(x, c, s1, s2, sh):
    return x * c + pltpu.roll(x, LANE - sh, 1) * s1 + pltpu.roll(x, sh, 1) * s2


def _rope_t(g, c, s1, s2, sh):
    return g * c + pltpu.roll(g * s1, sh, 1) + pltpu.roll(g * s2, LANE - sh, 1)


def _rope_tables(s, half, reps):
    pos = jnp.arange(s, dtype=F32)
    inv_freq = ROPE_THETA ** (-jnp.arange(half, dtype=F32) / half)
    ang = pos[:, None] * inv_freq[None, :]
    cos, sin = jnp.cos(ang), jnp.sin(ang)
    zero = jnp.zeros_like(cos)
    pad = jnp.zeros((s, LANE - 2 * half * reps), F32)
    c = jnp.concatenate([cos, cos] * reps + [pad], axis=1)
    s1 = jnp.concatenate([-sin, zero] * reps + [pad], axis=1)
    s2 = jnp.concatenate([zero, sin] * reps + [pad], axis=1)
    return c, s1, s2


def _lane_lt64(shape):
    return lax.broadcasted_iota(jnp.int32, shape, len(shape) - 1) % LANE < HEAD_DIM


def _group_sum(x):
    outs = []
    for b in range(x.shape[1] // LANE):
        blk = x[:, b * LANE:(b + 1) * LANE]
        lo = _lane_lt64(blk.shape)
        s0 = jnp.sum(jnp.where(lo, blk, 0.0), axis=1, keepdims=True)
        s1 = jnp.sum(jnp.where(lo, 0.0, blk), axis=1, keepdims=True)
        outs.append(jnp.where(lo, s0, s1))
    return outs


def _row_spec(ts, w):
    return pl.BlockSpec((ts, w), lambda i: (i, 0))


def _fix_spec(w):
    return pl.BlockSpec((1, w), lambda i: (0, 0))


def _rms_fwd(x, g, *, name):
    s, d = x.shape
    ts = _tile(s, (512, 256, 128))

    def body(x_ref, g_ref, o_ref):
        xv = x_ref[...]
        o_ref[...] = (xv * _rstd(xv) * g_ref[...]).astype(BF16)

    return pl.pallas_call(
        body, name=name, grid=(s // ts,), in_specs=[_row_spec(ts, d), _fix_spec(d)], out_specs=_row_spec(ts, d),
        out_shape=_sds((s, d), BF16), compiler_params=_cp("parallel"))(x, g.reshape(1, d))


def _rms_bwd(x, g, dy, res, *, name):
    s, d = x.shape
    ts = _tile(s, (512, 256, 128))

    def body(x_ref, g_ref, dy_ref, res_ref, dx_ref, dg_ref):
        dx, dg = _rms_bwd_rows(x_ref[...], g_ref[...], dy_ref[...])
        dx_ref[...] = res_ref[...] + dx
        _accum(dg_ref, dg, pl.program_id(0) == 0)

    return pl.pallas_call(
        body, name=name, grid=(s // ts,),
        in_specs=[_row_spec(ts, d), _fix_spec(d), _row_spec(ts, d), _row_spec(ts, d)],
        out_specs=(_row_spec(ts, d), _fix_spec(d)), out_shape=(_sds((s, d), F32), _sds((1, d), F32)),
        compiler_params=_cp("arbitrary"))(x, g.reshape(1, d), dy, res)


def _loss_head(x, g, target):
    s, d = x.shape
    ts = _tile(s, (512, 256, 128))

    def body(x_ref, g_ref, t_ref, loss_ref, dx_ref, dg_ref):
        xv, gv = x_ref[...], g_ref[...]
        err = xv * _rstd(xv) * gv - t_ref[...]
        part = 0.5 * jnp.sum(jnp.sum(err * err, axis=-1, keepdims=True) * (1.0 / d), axis=0, keepdims=True)
        dx, dg = _rms_bwd_rows(xv, gv, err * (1.0 / d))
        dx_ref[...] = dx
        first = pl.program_id(0) == 0
        _accum(dg_ref, dg, first)
        _accum(loss_ref, jnp.broadcast_to(part, (1, LANE)), first)

    return pl.pallas_call(
        body, name="loss_head", grid=(s // ts,), in_specs=[_row_spec(ts, d), _fix_spec(d), _row_spec(ts, d)],
        out_specs=(_fix_spec(LANE), _row_spec(ts, d), _fix_spec(d)),
        out_shape=(_sds((1, LANE), F32), _sds((s, d), F32), _sds((1, d), F32)),
        compiler_params=_cp("arbitrary"))(x, g.reshape(1, d), target)


def _prep_fwd(proj, q_norm, kv_norm, t32, t64):
    s = proj.shape[0]
    ts = _tile(s, (256, 128))

    def body(p_ref, qn_ref, kn_ref, c32, a32, b32, c64, a64, b64, cqn_ref, ckvn_ref, kpe_ref, qkvb_ref, qkvc_ref):
        cq = p_ref[:, 0:Q_LORA]
        cqn_ref[...] = (cq * _rstd(cq) * qn_ref[...]).astype(BF16)
        ckv = p_ref[:, COL_CKV:COL_KPE]
        ckvn_ref[...] = (ckv * _rstd(ckv) * kn_ref[...]).astype(BF16)
        kp = p_ref[:, COL_KPE:COL_B]
        kp2 = kp + pltpu.roll(kp, QK_ROPE, 1)
        kpe_ref[...] = _rope(kp2, c32[...], a32[...], b32[...], QK_ROPE // 2).astype(BF16)
        for b in range(6):
            blk = _rope(p_ref[:, COL_B + b * LANE:COL_B + (b + 1) * LANE], c64[...], a64[...], b64[...], HEAD_DIM // 2)
            if b < 3:
                blk = blk * SCALE_BC
            qkvb_ref[:, b * LANE:(b + 1) * LANE] = blk.astype(BF16)
        qkvb_ref[:, 2 * WIDTH_AB:3 * WIDTH_AB] = p_ref[:, COL_B + 2 * WIDTH_AB:COL_C].astype(BF16)
        qkvc_ref[:, 0:WIDTH_C] = (p_ref[:, COL_C:COL_C + WIDTH_C] * SCALE_BC).astype(BF16)
        qkvc_ref[:, WIDTH_C:3 * WIDTH_C] = p_ref[:, COL_C + WIDTH_C:PROJ_W].astype(BF16)

    tab = [_row_spec(ts, LANE)] * 6
    return pl.pallas_call(
        body, name="prep_fwd", grid=(s // ts,),
        in_specs=[_row_spec(ts, PROJ_W), _fix_spec(Q_LORA), _fix_spec(KV_LORA)] + tab,
        out_specs=(_row_spec(ts, Q_LORA), _row_spec(ts, KV_LORA), _row_spec(ts, LANE), _row_spec(ts, 3 * WIDTH_AB),
                   _row_spec(ts, 3 * WIDTH_C)),
        out_shape=(_sds((s, Q_LORA), BF16), _sds((s, KV_LORA), BF16), _sds((s, LANE), BF16),
                   _sds((s, 3 * WIDTH_AB), BF16), _sds((s, 3 * WIDTH_C), BF16)),
        compiler_params=_cp("parallel"))(proj, q_norm.reshape(1, -1), kv_norm.reshape(1, -1), *t32, *t64)


def _prep_bwd(proj, q_norm, kv_norm, t32, t64, dcqn, dckvn, dkpe, db, dc):
    s = proj.shape[0]
    ts = _tile(s, (256, 128))

    def body(p_ref, qn_ref, kn_ref, c32, a32, b32, c64, a64, b64, dcqn_ref, dckvn_ref, dkpe_ref, *rest):
        db_refs, dc_refs = rest[0:9], rest[9:12]
        dp_ref, dqn_ref, dkn_ref = rest[12:15]
        first = pl.program_id(0) == 0
        dx, dg = _rms_bwd_rows(p_ref[:, 0:Q_LORA], qn_ref[...], dcqn_ref[...])
        dp_ref[:, 0:Q_LORA] = dx.astype(BF16)
        _accum(dqn_ref, dg, first)
        dx, dg = _rms_bwd_rows(p_ref[:, COL_CKV:COL_KPE], kn_ref[...], dckvn_ref[...])
        dp_ref[:, COL_CKV:COL_KPE] = dx.astype(BF16)
        _accum(dkn_ref, dg, first)
        g = _rope_t(dkpe_ref[...], c32[...], a32[...], b32[...], QK_ROPE // 2)
        g = g + pltpu.roll(g, LANE - QK_ROPE, 1)
        lane = lax.broadcasted_iota(jnp.int32, g.shape, 1)
        dp_ref[:, COL_KPE:COL_B] = jnp.where(lane < QK_ROPE, g, 0.0).astype(BF16)
        for which in range(3):
            for b in range(3):
                sl = slice(b * LANE, (b + 1) * LANE)
                g = db_refs[which][:, sl] + db_refs[3 + which][:, sl] + db_refs[6 + which][:, sl]
                if which < 2:
                    g = _rope_t(g, c64[...], a64[...], b64[...], HEAD_DIM // 2)
                if which == 0:
                    g = g * SCALE_BC
                col = COL_B + which * WIDTH_AB + b * LANE
                dp_ref[:, col:col + LANE] = g.astype(BF16)
        dp_ref[:, COL_C:COL_C + WIDTH_C] = (dc_refs[0][...] * SCALE_BC).astype(BF16)
        dp_ref[:, COL_C + WIDTH_C:COL_C + 2 * WIDTH_C] = dc_refs[1][...].astype(BF16)
        dp_ref[:, COL_C + 2 * WIDTH_C:PROJ_W] = dc_refs[2][...].astype(BF16)

    tab = [_row_spec(ts, LANE)] * 6
    in_specs = ([_row_spec(ts, PROJ_W), _fix_spec(Q_LORA), _fix_spec(KV_LORA)] + tab
                + [_row_spec(ts, Q_LORA), _row_spec(ts, KV_LORA), _row_spec(ts, LANE)]
                + [_row_spec(ts, WIDTH_AB)] * 9 + [_row_spec(ts, WIDTH_C)] * 3)
    return pl.pallas_call(
        body, name="prep_bwd", grid=(s // ts,), in_specs=in_specs,
        out_specs=(_row_spec(ts, PROJ_W), _fix_spec(Q_LORA), _fix_spec(KV_LORA)),
        out_shape=(_sds((s, PROJ_W), BF16), _sds((1, Q_LORA), F32), _sds((1, KV_LORA), F32)),
        compiler_params=_cp("arbitrary"))(proj, q_norm.reshape(1, -1), kv_norm.reshape(1, -1), *t32, *t64,
                                          dcqn, dckvn, dkpe, *db, *dc)


def _a_post_fwd(qa, kva, kpe, t32):
    s = qa.shape[0]
    ts = _tile(s, (512, 256, 128))

    def body(qa_ref, kva_ref, kpe_ref, c32, a32, b32, q_ref, k_ref, v_ref):
        for p in range(3):
            lo, hi = 2 * p * LANE, (2 * p + 1) * LANE
            q_ref[:, lo:hi] = qa_ref[:, lo:hi].astype(BF16)
            q_ref[:, hi:hi + LANE] = _rope(qa_ref[:, hi:hi + LANE], c32[...], a32[...], b32[...], QK_ROPE // 2).astype(BF16)
            k_ref[:, lo:hi] = kva_ref[:, p * LANE:(p + 1) * LANE].astype(BF16)
            k_ref[:, hi:hi + LANE] = kpe_ref[...]
        v_ref[...] = kva_ref[:, WIDTH_AB:2 * WIDTH_AB].astype(BF16)

    return pl.pallas_call(
        body, name="a_post_fwd", grid=(s // ts,),
        in_specs=[_row_spec(ts, W_A2), _row_spec(ts, W_A2), _row_spec(ts, LANE)] + [_row_spec(ts, LANE)] * 3,
        out_specs=(_row_spec(ts, W_A2), _row_spec(ts, W_A2), _row_spec(ts, WIDTH_AB)),
        out_shape=(_sds((s, W_A2), BF16), _sds((s, W_A2), BF16), _sds((s, WIDTH_AB), BF16)),
        compiler_params=_cp("parallel"))(qa, kva, kpe, *t32)


def _a_post_bwd(dqa2, dka2, dva, t32):
    s = dqa2.shape[0]
    ts = _tile(s, (512, 256, 128))

    def body(dq_ref, dk_ref, dv_ref, c32, a32, b32, dqa_ref, dkva_ref, dkpe_ref):
        acc = None
        for p in range(3):
            lo, hi = 2 * p * LANE, (2 * p + 1) * LANE
            dqa_ref[:, lo:hi] = dq_ref[:, lo:hi].astype(BF16)
            dqa_ref[:, hi:hi + LANE] = _rope_t(dq_ref[:, hi:hi + LANE], c32[...], a32[...], b32[...],
                                               QK_ROPE // 2).astype(BF16)
            dkva_ref[:, p * LANE:(p + 1) * LANE] = dk_ref[:, lo:hi].astype(BF16)
            part = dk_ref[:, hi:hi + LANE]
            acc = part if acc is None else acc + part
        dkva_ref[:, WIDTH_AB:2 * WIDTH_AB] = dv_ref[...].astype(BF16)
        dkpe_ref[...] = acc

    return pl.pallas_call(
        body, name="a_post_bwd", grid=(s // ts,),
        in_specs=[_row_spec(ts, W_A2), _row_spec(ts, W_A2), _row_spec(ts, WIDTH_AB)] + [_row_spec(ts, LANE)] * 3,
        out_specs=(_row_spec(ts, W_A2), _row_spec(ts, W_A2), _row_spec(ts, LANE)),
        out_shape=(_sds((s, W_A2), BF16), _sds((s, W_A2), BF16), _sds((s, LANE), F32)),
        compiler_params=_cp("parallel"))(dqa2, dka2, dva, *t32)


def _pair_masks(width):
    lane = lax.broadcasted_iota(jnp.int32, (1, width), 1)
    m0 = lane < HEAD_DIM
    m1 = (lane >= HEAD_DIM) & (lane < LANE)
    if width == 2 * LANE:
        m0 = m0 | ((lane >= LANE) & (lane < LANE + QK_ROPE))
        m1 = m1 | ((lane >= LANE + QK_ROPE) & (lane < LANE + 2 * QK_ROPE))
    return m0, m1


def _nt(a, b):
    return lax.dot_general(a, b, (((1,), (1,)), ((), ())), preferred_element_type=F32)


def _tn(a, b):
    return lax.dot_general(a, b, (((0,), (0,)), ((), ())), preferred_element_type=F32)


def _softmax_pair(q, kk, vv, bias_fn, scale):
    outs, lses = [], []
    for hh, msk in enumerate(_pair_masks(q.shape[1])):
        s = _nt(jnp.where(msk, q, jnp.zeros_like(q)), kk)
        if scale != 1.0:
            s = s * scale
        s = bias_fn(hh, s)
        m = jnp.max(s, axis=1, keepdims=True)
        p = jnp.exp(s - m)
        l = jnp.sum(p, axis=1, keepdims=True)
        outs.append(jnp.dot(p.astype(BF16), vv, preferred_element_type=F32) / l)
        lses.append(m + jnp.log(l))
    lo = _lane_lt64(outs[0].shape)
    return jnp.where(lo, outs[0], outs[1]), jnp.where(lo, lses[0], lses[1])


def _softmax_pair_bwd(q, kk, vv, do, lse, delta, bias_fn, scale):
    dq = dk = dv = None
    dss = []
    vm = _pair_masks(LANE)
    for hh, msk in enumerate(_pair_masks(q.shape[1])):
        qm = jnp.where(msk, q, jnp.zeros_like(q))
        s = _nt(qm, kk)
        if scale != 1.0:
            s = s * scale
        s = bias_fn(hh, s)
        col = hh * HEAD_DIM
        p = jnp.exp(s - lse[:, col:col + 1])
        dp = _nt(jnp.where(vm[hh], do, jnp.zeros_like(do)), vv)
        ds = p * (dp - delta[:, col:col + 1])
        dss.append(ds)
        dsb = (ds * scale if scale != 1.0 else ds).astype(BF16)
        dq_h = jnp.where(msk, jnp.dot(dsb, kk, preferred_element_type=F32), 0.0)
        dk_h = _tn(dsb, qm)
        dv_h = jnp.where(vm[hh], _tn(p.astype(BF16), do), 0.0)
        dq = dq_h if dq is None else dq + dq_h
        dk = dk_h if dk is None else dk + dk_h
        dv = dv_h if dv is None else dv + dv_h
    return dq, dk, dv, dss


def _no_bias(hh, s):
    return s


def _dense_fwd(qa, ka, va):
    s = qa.shape[0]
    tq = _tile(s, (512, 256, 128))
    tk = _tile(s, (512, 256, 128))
    nk = s // tk

    def body(q_ref, k_ref, v_ref, o_ref, lse_ref, m_sc, l_sc, acc_sc):
        j = pl.program_id(2)

        @pl.when(j == 0)
        def _():
            m_sc[...] = jnp.full(m_sc.shape, NEG_INF, F32)
            l_sc[...] = jnp.zeros(l_sc.shape, F32)
            acc_sc[...] = jnp.zeros(acc_sc.shape, F32)

        q, kk, vv = q_ref[...], k_ref[...], v_ref[...]
        for hh, msk in enumerate(_pair_masks(2 * LANE)):
            sc = _nt(jnp.where(msk, q, jnp.zeros_like(q)), kk) * SCALE_A
            m_prev = m_sc[hh]
            m_new = jnp.maximum(m_prev, jnp.max(sc, axis=1, keepdims=True))
            alpha = jnp.exp(m_prev - m_new)
            p = jnp.exp(sc - m_new)
            l_sc[hh] = alpha * l_sc[hh] + jnp.sum(p, axis=1, keepdims=True)
            acc_sc[hh] = alpha * acc_sc[hh] + jnp.dot(p.astype(BF16), vv, preferred_element_type=F32)
            m_sc[hh] = m_new

        @pl.when(j == nk - 1)
        def _():
            lo = _lane_lt64((tq, LANE))
            o_ref[...] = jnp.where(lo, acc_sc[0] / l_sc[0], acc_sc[1] / l_sc[1])
            lse_ref[...] = jnp.where(lo, m_sc[0] + jnp.log(l_sc[0]), m_sc[1] + jnp.log(l_sc[1]))

    o_spec = pl.BlockSpec((tq, LANE), lambda p, i, j: (i, p))
    return pl.pallas_call(
        body, name="dense_fwd", grid=(3, s // tq, nk),
        in_specs=[pl.BlockSpec((tq, 2 * LANE), lambda p, i, j: (i, p)), pl.BlockSpec((tk, 2 * LANE), lambda p, i, j: (j, p)),
                  pl.BlockSpec((tk, LANE), lambda p, i, j: (j, p))],
        out_specs=(o_spec, o_spec), out_shape=(_sds((s, WIDTH_AB), F32), _sds((s, WIDTH_AB), F32)),
        scratch_shapes=[pltpu.VMEM((2, tq, 1), F32), pltpu.VMEM((2, tq, 1), F32), pltpu.VMEM((2, tq, LANE), F32)],
        compiler_params=_cp("parallel", "parallel", "arbitrary"))(qa, ka, va)


def _dense_bwd(qa, ka, va, do, lse, delta):
    s = qa.shape[0]
    tq = _tile(s, (512, 256, 128))
    tk = _tile(s, (512, 256, 128))

    def body(q_ref, k_ref, v_ref, do_ref, lse_ref, dl_ref, dq_ref, dk_ref, dv_ref):
        j, i = pl.program_id(1), pl.program_id(2)

        @pl.when((j == 0) & (i == 0))
        def _():
            dq_ref[...] = jnp.zeros(dq_ref.shape, F32)

        dq, dk, dv, _ = _softmax_pair_bwd(q_ref[...], k_ref[...], v_ref[...], do_ref[...], lse_ref[...], dl_ref[...],
                                          _no_bias, SCALE_A)
        rows = pl.ds(pl.multiple_of(i * tq, tq), tq)
        dq_ref[rows, :] += dq
        _accum(dk_ref, dk, i == 0)
        _accum(dv_ref, dv, i == 0)

    q_spec = pl.BlockSpec((tq, LANE), lambda p, j, i: (i, p))
    return pl.pallas_call(
        body, name="dense_bwd", grid=(3, s // tk, s // tq),
        in_specs=[pl.BlockSpec((tq, 2 * LANE), lambda p, j, i: (i, p)), pl.BlockSpec((tk, 2 * LANE), lambda p, j, i: (j, p)),
                  pl.BlockSpec((tk, LANE), lambda p, j, i: (j, p)), q_spec, q_spec, q_spec],
        out_specs=(pl.BlockSpec((s, 2 * LANE), lambda p, j, i: (0, p)), pl.BlockSpec((tk, 2 * LANE), lambda p, j, i: (j, p)),
                   pl.BlockSpec((tk, LANE), lambda p, j, i: (j, p))),
        out_shape=(_sds((s, W_A2), F32), _sds((s, W_A2), F32), _sds((s, WIDTH_AB), F32)),
        compiler_params=_cp("parallel", "arbitrary", "arbitrary"))(qa, ka, va, do, lse, delta)


def _band_specs(t, n, dil, qoff, koff, voff, wblk):
    hpt = t // BAND_HALF
    last = n // BAND_HALF - 1

    def main(off):
        return pl.BlockSpec((t, LANE), lambda r, p, i: (i, r * wblk + off + p))

    def left(off):
        return pl.BlockSpec((BAND_HALF, LANE), lambda r, p, i: (jnp.maximum(i * hpt - 1, 0), r * wblk + off + p))

    def right(off):
        return pl.BlockSpec((BAND_HALF, LANE), lambda r, p, i: (jnp.minimum((i + 1) * hpt, last), r * wblk + off + p))

    return [main(qoff), left(koff), main(koff), right(koff), left(voff), main(voff), right(voff)]


def _band_bias(t, n):
    def fn(hh, s):
        i = pl.program_id(2)
        qpos = i * t + lax.broadcasted_iota(jnp.int32, s.shape, 0)
        kpos = i * t - BAND_HALF + lax.broadcasted_iota(jnp.int32, s.shape, 1)
        ok = (jnp.abs(qpos - kpos) <= BAND_HALF) & (kpos >= 0) & (kpos < n)
        return jnp.where(ok, s, NEG_INF)
    return fn


def _banded_fwd(qkvb, dil):
    s = qkvb.shape[0]
    n = s // dil
    t = _tile(n, (512, 256, 128, 64))
    view = qkvb.reshape(n, dil * 3 * WIDTH_AB)

    def body(q_ref, kl, km, kr, vl, vm, vr, o_ref, lse_ref):
        kk = jnp.concatenate([kl[...], km[...], kr[...]], axis=0)
        vv = jnp.concatenate([vl[...], vm[...], vr[...]], axis=0)
        o, lse = _softmax_pair(q_ref[...], kk, vv, _band_bias(t, n), 1.0)
        o_ref[...] = o
        lse_ref[...] = lse

    o_spec = pl.BlockSpec((t, LANE), lambda r, p, i: (i, r * 3 + p))
    o, lse = pl.pallas_call(
        body, name=f"banded_fwd_d{dil}", grid=(dil, 3, n // t), in_specs=_band_specs(t, n, dil, 0, 3, 6, 9),
        out_specs=(o_spec, o_spec), out_shape=(_sds((n, dil * WIDTH_AB), F32), _sds((n, dil * WIDTH_AB), F32)),
        compiler_params=_cp("parallel", "parallel", "parallel"))(*([view] * 7))
    return o.reshape(s, WIDTH_AB), lse.reshape(s, WIDTH_AB)


def _banded_bwd(qkvb, do, lse, delta, dil):
    s = qkvb.shape[0]
    n = s // dil
    t = _tile(n, (512, 256, 128, 64))
    view = qkvb.reshape(n, dil * 3 * WIDTH_AB)
    side = [a.reshape(n, dil * WIDTH_AB) for a in (do, lse, delta)]

    def body(q_ref, kl, km, kr, vl, vm, vr, do_ref, lse_ref, dl_ref, dq_ref, dk_ref, dv_ref):
        i = pl.program_id(2)

        @pl.when(i == 0)
        def _():
            dk_ref[...] = jnp.zeros(dk_ref.shape, F32)
            dv_ref[...] = jnp.zeros(dv_ref.shape, F32)

        kk = jnp.concatenate([kl[...], km[...], kr[...]], axis=0)
        vv = jnp.concatenate([vl[...], vm[...], vr[...]], axis=0)
        dq, dk, dv, _ = _softmax_pair_bwd(q_ref[...], kk, vv, do_ref[...], lse_ref[...], dl_ref[...], _band_bias(t, n), 1.0)
        dq_ref[...] = dq
        lrow = pl.multiple_of(jnp.maximum(i * t - BAND_HALF, 0), BAND_HALF)
        rrow = pl.multiple_of(jnp.minimum((i + 1) * t, n - BAND_HALF), BAND_HALF)
        mrow = pl.multiple_of(i * t, BAND_HALF)
        for ref, g in ((dk_ref, dk), (dv_ref, dv)):
            ref[pl.ds(lrow, BAND_HALF), :] += g[0:BAND_HALF]
            ref[pl.ds(mrow, t), :] += g[BAND_HALF:BAND_HALF + t]
            ref[pl.ds(rrow, BAND_HALF), :] += g[BAND_HALF + t:2 * BAND_HALF + t]

    q_spec = pl.BlockSpec((t, LANE), lambda r, p, i: (i, r * 3 + p))
    acc_spec = pl.BlockSpec((n, LANE), lambda r, p, i: (0, r * 3 + p))
    shp = _sds((n, dil * WIDTH_AB), F32)
    outs = pl.pallas_call(
        body, name=f"banded_bwd_d{dil}", grid=(dil, 3, n // t),
        in_specs=_band_specs(t, n, dil, 0, 3, 6, 9) + [q_spec, q_spec, q_spec],
        out_specs=(q_spec, acc_spec, acc_spec), out_shape=(shp, shp, shp),
        compiler_params=_cp("parallel", "parallel", "arbitrary"))(*([view] * 7), *side)
    return [a.reshape(s, WIDTH_AB) for a in outs]


def _merge_branches(outs, lses):
    s = outs[0].shape[0]
    ts = _tile(s, (512, 256, 128))

    def body(o1, o2, o3, l1, l2, l3, o_ref, lse_ref):
        a, b, c = l1[...], l2[...], l3[...]
        m = jnp.maximum(jnp.maximum(a, b), c)
        ea, eb, ec = jnp.exp(a - m), jnp.exp(b - m), jnp.exp(c - m)
        den = ea + eb + ec
        o_ref[...] = (o1[...] * ea + o2[...] * eb + o3[...] * ec) / den
        lse_ref[...] = m + jnp.log(den)

    sp = _row_spec(ts, WIDTH_AB)
    return pl.pallas_call(
        body, name="merge_branches", grid=(s // ts,), in_specs=[sp] * 6, out_specs=(sp, sp),
        out_shape=(_sds((s, WIDTH_AB), F32), _sds((s, WIDTH_AB), F32)), compiler_params=_cp("parallel"))(*outs, *lses)


def _na_geometry(s):
    rows = s // GRID_W
    assert rows >= 2 * NA_ROWS and rows % NA_ROWS == 0
    return rows, rows // NA_ROWS


def _na_row(n, i, rows):
    rq = n * NA_ROWS + i
    rs = jnp.clip(rq - NA_ROWS // 2, 0, rows - NA_ROWS)
    return pl.multiple_of(rs * GRID_W, GRID_W), rs - rq + NA_ROWS - 1


NA_KEYS = NA_ROWS * GRID_W


def _natten_fwd(qkvc, tfull):
    s = qkvc.shape[0]
    rows, nrb = _na_geometry(s)
    tq = NA_ROWS * GRID_W

    def body(q_ref, k_ref, v_ref, t_ref, o_ref, lse_ref):
        n = pl.program_id(1)
        for i in range(NA_ROWS):
            tok, base = _na_row(n, i, rows)
            kk, vv = k_ref[pl.ds(tok, NA_KEYS), :], v_ref[pl.ds(tok, NA_KEYS), :]
            sl = slice(i * GRID_W, (i + 1) * GRID_W)
            o, lse = _softmax_pair(q_ref[sl, :], kk, vv, lambda hh, sc: sc + t_ref[hh, base], 1.0)
            o_ref[sl, :] = o
            lse_ref[sl, :] = lse

    o_spec = pl.BlockSpec((tq, LANE), lambda p, n: (n, p))
    return pl.pallas_call(
        body, name="natten_fwd", grid=(2, nrb),
        in_specs=[pl.BlockSpec((tq, LANE), lambda p, n: (n, p)), pl.BlockSpec((s, LANE), lambda p, n: (0, 2 + p)),
                  pl.BlockSpec((s, LANE), lambda p, n: (0, 4 + p)),
                  pl.BlockSpec((2, NA_ROWS, GRID_W, NA_KEYS), lambda p, n: (p, 0, 0, 0))],
        out_specs=(o_spec, o_spec), out_shape=(_sds((s, WIDTH_C), F32), _sds((s, WIDTH_C), F32)),
        compiler_params=_cp("parallel", "parallel"))(qkvc, qkvc, qkvc, tfull)


def _natten_bwd(qkvc, tfull, do, lse, delta):
    s = qkvc.shape[0]
    rows, nrb = _na_geometry(s)
    tq = NA_ROWS * GRID_W

    def body(q_ref, k_ref, v_ref, t_ref, do_ref, lse_ref, dl_ref, dq_ref, dk_ref, dv_ref, dt_ref):
        n = pl.program_id(1)

        @pl.when(n == 0)
        def _():
            dk_ref[...] = jnp.zeros(dk_ref.shape, F32)
            dv_ref[...] = jnp.zeros(dv_ref.shape, F32)
            dt_ref[...] = jnp.zeros(dt_ref.shape, F32)

        for i in range(NA_ROWS):
            tok, base = _na_row(n, i, rows)
            win = pl.ds(tok, NA_KEYS)
            sl = slice(i * GRID_W, (i + 1) * GRID_W)
            dq, dk, dv, dss = _softmax_pair_bwd(q_ref[sl, :], k_ref[win, :], v_ref[win, :], do_ref[sl, :], lse_ref[sl, :],
                                                dl_ref[sl, :], lambda hh, sc: sc + t_ref[hh, base], 1.0)
            dq_ref[sl, :] = dq
            dk_ref[win, :] += dk
            dv_ref[win, :] += dv
            dt_ref[0, base] += dss[0]
            dt_ref[1, base] += dss[1]

    q_spec = pl.BlockSpec((tq, LANE), lambda p, n: (n, p))
    acc_spec = pl.BlockSpec((s, LANE), lambda p, n: (0, p))
    t_spec = pl.BlockSpec((2, NA_ROWS, GRID_W, NA_KEYS), lambda p, n: (p, 0, 0, 0))
    shp = _sds((s, WIDTH_C), F32)
    return pl.pallas_call(
        body, name="natten_bwd", grid=(2, nrb),
        in_specs=[q_spec, pl.BlockSpec((s, LANE), lambda p, n: (0, 2 + p)), pl.BlockSpec((s, LANE), lambda p, n: (0, 4 + p)),
                  t_spec, q_spec, q_spec, q_spec],
        out_specs=(q_spec, acc_spec, acc_spec, t_spec),
        out_shape=(shp, shp, shp, _sds((HEADS_C, NA_ROWS, GRID_W, NA_KEYS), F32)),
        compiler_params=_cp("parallel", "arbitrary"))(qkvc, qkvc, qkvc, tfull, do, lse, delta)


def _rpb_constants():
    p = np.arange(GRID_W)[:, None]
    qc = np.arange(GRID_W)[None, :]
    dc = np.clip(qc - p, -(NA_COLS - 1), NA_COLS - 1) + NA_COLS - 1
    onehot = (dc.reshape(1, -1) == np.arange(32)[:, None]).astype(np.float32)
    c_start = np.clip(p - NA_COLS // 2, 0, GRID_W - NA_COLS)
    col_ok = ((qc >= c_start) & (qc < c_start + NA_COLS)).reshape(1, -1).astype(np.float32)
    a = np.arange(16)[:, None]
    bj = np.arange(64)[None, :]
    row_sel = ((bj // 8 + bj % 8) == a).astype(np.float32)
    return jnp.asarray(onehot), jnp.asarray(col_ok), jnp.asarray(row_sel)


def _rpb_expand(rpb, onehot, col_ok):
    r2 = jnp.pad(rpb.reshape(HEADS_C * 15, 31), ((0, 4), (0, 1)))

    def body(r_ref, oh_ref, ok_ref, o_ref):
        t = jnp.dot(r_ref[...], oh_ref[...], preferred_element_type=F32, precision=lax.Precision.HIGHEST)
        o_ref[...] = jnp.where(ok_ref[...] > 0.5, t, NEG_INF)

    tm = pl.pallas_call(body, name="rpb_expand", out_shape=_sds((64, GRID_W * GRID_W), F32))(r2, onehot, col_ok)
    tm = tm[:HEADS_C * 15].reshape(HEADS_C, 15, GRID_W, GRID_W)
    tfull = jnp.stack([jnp.concatenate([tm[:, base + j] for j in range(NA_ROWS)], axis=-1) for base in range(NA_ROWS)], axis=1)
    return tfull


def _rpb_grad(dtfull, onehot, row_sel):
    g = dtfull.reshape(HEADS_C, NA_ROWS, GRID_W, NA_ROWS, GRID_W).transpose(0, 1, 3, 2, 4).reshape(HEADS_C, 64, GRID_W * GRID_W)

    def body(g_ref, oh_ref, sel_ref, o_ref):
        for h in range(HEADS_C):
            mid = lax.dot_general(g_ref[h], oh_ref[...], (((1,), (1,)), ((), ())), preferred_element_type=F32,
                                  precision=lax.Precision.HIGHEST)
            o_ref[h] = jnp.dot(sel_ref[...], mid, preferred_element_type=F32, precision=lax.Precision.HIGHEST)

    out = pl.pallas_call(body, name="rpb_grad", out_shape=_sds((HEADS_C, 16, 32), F32))(g, onehot, row_sel)
    return out[:, :15, :31]


def _outnorm_fwd(o_a, o_b, o_c, ga, gb, gc):
    s = o_a.shape[0]
    ts = _tile(s, (512, 256, 128))

    def body(a_ref, b_ref, c_ref, ga_ref, gb_ref, gc_ref, o_ref):
        col = 0
        for ref, g in ((a_ref, ga_ref), (b_ref, gb_ref), (c_ref, gc_ref)):
            x = ref[...]
            o_ref[:, col:col + x.shape[1]] = (x * _rstd(x) * g[...]).astype(BF16)
            col += x.shape[1]

    return pl.pallas_call(
        body, name="outnorm_fwd", grid=(s // ts,),
        in_specs=[_row_spec(ts, WIDTH_AB), _row_spec(ts, WIDTH_AB), _row_spec(ts, WIDTH_C), _fix_spec(WIDTH_AB),
                  _fix_spec(WIDTH_AB), _fix_spec(WIDTH_C)],
        out_specs=_row_spec(ts, D_MODEL), out_shape=_sds((s, D_MODEL), BF16),
        compiler_params=_cp("parallel"))(o_a, o_b, o_c, ga.reshape(1, -1), gb.reshape(1, -1), gc.reshape(1, -1))


def _outnorm_bwd(dmixed, o_a, o_b, o_c, ga, gb, gc):
    s = o_a.shape[0]
    ts = _tile(s, (512, 256, 128))

    def body(dm_ref, a_ref, b_ref, c_ref, ga_ref, gb_ref, gc_ref, *outs):
        first = pl.program_id(0) == 0
        col = 0
        for k, (ref, g) in enumerate(((a_ref, ga_ref), (b_ref, gb_ref), (c_ref, gc_ref))):
            x = ref[...]
            w = x.shape[1]
            dx, dg = _rms_bwd_rows(x, g[...], dm_ref[:, col:col + w])
            col += w
            outs[k][...] = dx.astype(BF16)
            for b, blk in enumerate(_group_sum(dx * x)):
                outs[3 + k][:, b * LANE:(b + 1) * LANE] = blk
            _accum(outs[6 + k], dg, first)

    widths = (WIDTH_AB, WIDTH_AB, WIDTH_C)
    return pl.pallas_call(
        body, name="outnorm_bwd", grid=(s // ts,),
        in_specs=[_row_spec(ts, D_MODEL)] + [_row_spec(ts, w) for w in widths] + [_fix_spec(w) for w in widths],
        out_specs=tuple([_row_spec(ts, w) for w in widths] * 2 + [_fix_spec(w) for w in widths]),
        out_shape=tuple([_sds((s, w), BF16) for w in widths] + [_sds((s, w), F32) for w in widths]
                        + [_sds((1, w), F32) for w in widths]),
        compiler_params=_cp("arbitrary"))(dmixed, o_a, o_b, o_c, ga.reshape(1, -1), gb.reshape(1, -1), gc.reshape(1, -1))


def _adamw(w, g, m, v, *, name):
    r, c = w.shape
    tr = _tile(r, (512, 256, 128, 64, 8))

    def body(w_ref, g_ref, m_ref, v_ref, d_ref, nm_ref, nv_ref):
        gv = g_ref[...]
        nm = ADAM_B1 * m_ref[...] + (1.0 - ADAM_B1) * gv
        nv = ADAM_B2 * v_ref[...] + (1.0 - ADAM_B2) * jnp.square(gv)
        m_hat = nm / (1.0 - ADAM_B1 ** ADAM_STEP)
        v_hat = nv / (1.0 - ADAM_B2 ** ADAM_STEP)
        d_ref[...] = -ADAM_LR * (m_hat / (jnp.sqrt(v_hat) + ADAM_EPS) + ADAM_WD * w_ref[...])
        nm_ref[...] = nm
        nv_ref[...] = nv

    sp = _row_spec(tr, c)
    return pl.pallas_call(
        body, name=name, grid=(r // tr,), in_specs=[sp] * 4, out_specs=(sp, sp, sp),
        out_shape=(_sds((r, c), F32),) * 3, compiler_params=_cp("parallel"))(w, g, m, v)


def _add_n(parts, *, name):
    r, c = parts[0].shape
    tr = _tile(r, (512, 256, 128, 64, 8))

    def body(*refs):
        acc = refs[0][...]
        for ref in refs[1:-1]:
            acc = acc + ref[...]
        refs[-1][...] = acc

    sp = _row_spec(tr, c)
    return pl.pallas_call(
        body, name=name, grid=(r // tr,), in_specs=[sp] * len(parts), out_specs=sp, out_shape=_sds((r, c), F32),
        compiler_params=_cp("parallel"))(*parts)


ANY = pl.BlockSpec(memory_space=pl.ANY)
CHIP_FLIPS = ((1, 0), (0, 1), (1, 1))


def _me():
    return lax.axis_index("x"), lax.axis_index("y"), lax.axis_index("c")


def _gather_chips(shard):
    def body(src, out, send_sems, recv_sems, local_sem):
        x, y, c = _me()
        mine = 2 * x + y
        own = pltpu.make_async_copy(src, out.at[mine], local_sem)
        own.start()
        sends = []
        for k, (fx, fy) in enumerate(CHIP_FLIPS):
            cp = pltpu.make_async_remote_copy(src_ref=src, dst_ref=out.at[mine], send_sem=send_sems.at[k],
                                              recv_sem=recv_sems.at[k], device_id=(x ^ fx, y ^ fy, c), device_id_type=MESH_T)
            cp.start()
            sends.append(cp)
        for k, (fx, fy) in enumerate(CHIP_FLIPS):
            theirs = 2 * (x ^ fx) + (y ^ fy)
            pltpu.make_async_remote_copy(src_ref=src, dst_ref=out.at[theirs], send_sem=send_sems.at[k],
                                         recv_sem=recv_sems.at[k], device_id=(x ^ fx, y ^ fy, c),
                                         device_id_type=MESH_T).wait_recv()
        for cp in sends:
            cp.wait_send()
        own.wait()

    return pl.pallas_call(
        body, name="gather_chips", in_specs=[ANY], out_specs=ANY, out_shape=_sds((4,) + shard.shape, shard.dtype),
        scratch_shapes=[pltpu.SemaphoreType.DMA((3,)), pltpu.SemaphoreType.DMA((3,)), pltpu.SemaphoreType.DMA(())],
    )(shard)


def _swap_sibling(block):
    def body(src, out, send_sem, recv_sem):
        x, y, c = _me()
        cp = pltpu.make_async_remote_copy(src_ref=src, dst_ref=out, send_sem=send_sem, recv_sem=recv_sem,
                                          device_id=(x, y, 1 - c), device_id_type=MESH_T)
        cp.start()
        cp.wait()

    return pl.pallas_call(
        body, name="swap_sibling", in_specs=[ANY], out_specs=ANY, out_shape=_sds(block.shape, block.dtype),
        scratch_shapes=[pltpu.SemaphoreType.DMA(()), pltpu.SemaphoreType.DMA(())])(block)


def _scatter_chips(parts):
    def body(src, out, send_sems, recv_sems, local_sem):
        x, y, c = _me()
        mine = 2 * x + y
        own = pltpu.make_async_copy(src.at[mine], out.at[mine], local_sem)
        own.start()
        sends = []
        for k, (fx, fy) in enumerate(CHIP_FLIPS):
            theirs = 2 * (x ^ fx) + (y ^ fy)
            cp = pltpu.make_async_remote_copy(src_ref=src.at[theirs], dst_ref=out.at[mine], send_sem=send_sems.at[k],
                                              recv_sem=recv_sems.at[k], device_id=(x ^ fx, y ^ fy, c), device_id_type=MESH_T)
            cp.start()
            sends.append(cp)
        for k, (fx, fy) in enumerate(CHIP_FLIPS):
            theirs = 2 * (x ^ fx) + (y ^ fy)
            pltpu.make_async_remote_copy(src_ref=src.at[theirs], dst_ref=out.at[theirs], send_sem=send_sems.at[k],
                                         recv_sem=recv_sems.at[k], device_id=(x ^ fx, y ^ fy, c),
                                         device_id_type=MESH_T).wait_recv()
        for cp in sends:
            cp.wait_send()
        own.wait()

    return pl.pallas_call(
        body, name="scatter_chips", in_specs=[ANY], out_specs=ANY, out_shape=_sds(parts.shape, parts.dtype),
        scratch_shapes=[pltpu.SemaphoreType.DMA((3,)), pltpu.SemaphoreType.DMA((3,)), pltpu.SemaphoreType.DMA(())],
    )(parts)


def _join_halves(half):
    def body(src, out, send_sem, recv_sem, local_sem):
        x, y, c = _me()
        own = pltpu.make_async_copy(src, out.at[c], local_sem)
        own.start()
        cp = pltpu.make_async_remote_copy(src_ref=src, dst_ref=out.at[c], send_sem=send_sem, recv_sem=recv_sem,
                                          device_id=(x, y, 1 - c), device_id_type=MESH_T)
        cp.start()
        pltpu.make_async_remote_copy(src_ref=src, dst_ref=out.at[1 - c], send_sem=send_sem, recv_sem=recv_sem,
                                     device_id=(x, y, 1 - c), device_id_type=MESH_T).wait_recv()
        cp.wait_send()
        own.wait()

    return pl.pallas_call(
        body, name="join_halves", in_specs=[ANY], out_specs=ANY, out_shape=_sds((2,) + half.shape, half.dtype),
        scratch_shapes=[pltpu.SemaphoreType.DMA(()), pltpu.SemaphoreType.DMA(()), pltpu.SemaphoreType.DMA(())])(half)


def _all_reduce_small(block):
    r, c = block.shape

    def body(src, out, slots, send_sems, recv_sems):
        x, y, cc = _me()
        mine = 4 * x + 2 * y + cc
        slots[mine] = src[...]
        sends = []
        for k in range(1, 8):
            fx, fy, fc = (k >> 2) & 1, (k >> 1) & 1, k & 1
            cp = pltpu.make_async_remote_copy(src_ref=src, dst_ref=slots.at[mine], send_sem=send_sems.at[k - 1],
                                              recv_sem=recv_sems.at[k - 1], device_id=(x ^ fx, y ^ fy, cc ^ fc),
                                              device_id_type=MESH_T)
            cp.start()
            sends.append(cp)
        for k in range(1, 8):
            fx, fy, fc = (k >> 2) & 1, (k >> 1) & 1, k & 1
            theirs = 4 * (x ^ fx) + 2 * (y ^ fy) + (cc ^ fc)
            pltpu.make_async_remote_copy(src_ref=src, dst_ref=slots.at[theirs], send_sem=send_sems.at[k - 1],
                                         recv_sem=recv_sems.at[k - 1], device_id=(x ^ fx, y ^ fy, cc ^ fc),
                                         device_id_type=MESH_T).wait_recv()
        for cp in sends:
            cp.wait_send()
        acc = slots[0]
        for d in range(1, 8):
            acc = acc + slots[d]
        out[...] = acc

    vm = pl.BlockSpec(memory_space=pltpu.VMEM)
    return pl.pallas_call(
        body, name="all_reduce_small", in_specs=[vm], out_specs=vm, out_shape=_sds((r, c), F32),
        scratch_shapes=[pltpu.VMEM((8, r, c), F32), pltpu.SemaphoreType.DMA((7,)), pltpu.SemaphoreType.DMA((7,))])(block)


BIG = ("w_in", "w_uq", "w_ukv", "w_out", "w_mlp_in", "w_mlp_out")
COL_SHARDED = {"w_in": True, "w_uq": True, "w_ukv": True, "w_out": False, "w_mlp_in": True, "w_mlp_out": False}
SMALL = ("g_mix", "q_norm", "kv_norm", "rpb", "out_norm_a", "out_norm_b", "out_norm_c", "g_mlp", "g_final")
PACK_C = 1024
ROW_ALIGN = 32


def _pack_rows(parts):
    flat = jnp.concatenate([p.reshape(-1, PACK_C) for p in parts], axis=0)
    return jnp.pad(flat, ((0, -flat.shape[0] % ROW_ALIGN), (0, 0)))


def _unpack_rows(flat, shapes):
    out, row = [], 0
    for shp in shapes:
        n = int(np.prod(shp)) // PACK_C
        out.append(flat[row:row + n].reshape(shp))
        row += n
    return out


def _full_from_shards(name, g):
    if COL_SHARDED[name]:
        return g.transpose(1, 2, 0, 3).reshape(g.shape[1], g.shape[2], 4 * g.shape[3])
    return g.transpose(1, 0, 2, 3).reshape(g.shape[1], 4 * g.shape[2], g.shape[3])


def _shards_from_full(name, w):
    l, k, n = w.shape
    if COL_SHARDED[name]:
        return w.reshape(l, k, 4, n // 4).transpose(2, 0, 1, 3)
    return w.reshape(l, 4, k // 4, n).transpose(1, 0, 2, 3)


def _arrange_w_in(w):
    z = jnp.zeros(w.shape[:-1] + (COL_B - COL_KPE - QK_ROPE,), w.dtype)
    return jnp.concatenate([w[..., :COL_KPE + QK_ROPE], z, w[..., COL_KPE + QK_ROPE:]], axis=-1)


def _unarrange_w_in(w):
    return jnp.concatenate([w[..., :COL_KPE + QK_ROPE], w[..., COL_B:]], axis=-1)


def _arrange_w_uq(w):
    per = HEAD_DIM + QK_ROPE
    z = jnp.zeros(w.shape[:-1] + (HEAD_DIM,), w.dtype)
    cols = []
    for p in range(3):
        a, b = 2 * p * per, (2 * p + 1) * per
        cols += [w[..., a:a + HEAD_DIM], w[..., b:b + HEAD_DIM], w[..., a + HEAD_DIM:a + per], w[..., b + HEAD_DIM:b + per], z]
    return jnp.concatenate(cols, axis=-1)


def _unarrange_w_uq(w):
    cols = []
    for h in range(HEADS_A):
        p, e = divmod(h, 2)
        base = 2 * p * LANE
        cols += [w[..., base + e * HEAD_DIM:base + (e + 1) * HEAD_DIM],
                 w[..., base + LANE + e * QK_ROPE:base + LANE + (e + 1) * QK_ROPE]]
    return jnp.concatenate(cols, axis=-1)


def _arrange_w_ukv(w):
    ks = [w[..., h * LANE:h * LANE + HEAD_DIM] for h in range(HEADS_A)]
    vs = [w[..., h * LANE + HEAD_DIM:(h + 1) * LANE] for h in range(HEADS_A)]
    return jnp.concatenate(ks + vs, axis=-1)


def _unarrange_w_ukv(w):
    cols = []
    for h in range(HEADS_A):
        cols += [w[..., h * HEAD_DIM:(h + 1) * HEAD_DIM], w[..., WIDTH_AB + h * HEAD_DIM:WIDTH_AB + (h + 1) * HEAD_DIM]]
    return jnp.concatenate(cols, axis=-1)


def _layer_fwd(x, w, sm, tabs, consts):
    t32, t64 = tabs
    onehot, col_ok, _ = consts
    h = _rms_fwd(x, sm["g_mix"], name="norm_mix")
    proj = _mm_nn(h, w["w_in"], name="in_proj")
    cqn, ckvn, kpe, qkvb, qkvc = _prep_fwd(proj, sm["q_norm"], sm["kv_norm"], t32, t64)
    qa = _mm_nn(cqn, w["w_uq"], name="q_up")
    kva = _mm_nn(ckvn, w["w_ukv"], name="kv_up")
    qa2, ka2, va = _a_post_fwd(qa, kva, kpe, t32)
    o_a, lse_a = _dense_fwd(qa2, ka2, va)
    branch = [_banded_fwd(qkvb, dil) for _, dil in DILATED_PAIRS]
    o_b, lse_b = _merge_branches([b[0] for b in branch], [b[1] for b in branch])
    tfull = _rpb_expand(sm["rpb"], onehot, col_ok)
    o_c, lse_c = _natten_fwd(qkvc, tfull)
    mixed = _outnorm_fwd(o_a, o_b, o_c, sm["out_norm_a"], sm["out_norm_b"], sm["out_norm_c"])
    x_mid = _mm_nn(mixed, w["w_out"], name="out_proj", res=x)
    h2 = _rms_fwd(x_mid, sm["g_mlp"], name="norm_mlp")
    u, act = _mm_mlp_in(h2, w["w_mlp_in"])
    x_out = _mm_nn(act, w["w_mlp_out"], name="mlp_out", res=x_mid)
    saved = dict(x=x, h=h, proj=proj, cqn=cqn, ckvn=ckvn, qkvb=qkvb, qkvc=qkvc, qa2=qa2, ka2=ka2, va=va, o_a=o_a,
                 lse_a=lse_a, o_b=o_b, lse_b=lse_b, o_c=o_c, lse_c=lse_c, tfull=tfull, mixed=mixed, x_mid=x_mid, h2=h2,
                 u=u, act=act)
    return x_out, saved


def _layer_bwd(dx, sv, w, sm, tabs, consts):
    t32, t64 = tabs
    onehot, _, row_sel = consts
    g = {}
    dxb = dx.astype(BF16)
    du = _mm_nt(dxb, w["w_mlp_out"], name="mlp_out_dx", out_dtype=BF16, relu2_of=sv["u"])
    g["w_mlp_out"] = _mm_tn(sv["act"], dxb, name="mlp_out_dw")
    dh2 = _mm_nt(du, w["w_mlp_in"], name="mlp_in_dx")
    g["w_mlp_in"] = _mm_tn(sv["h2"], du, name="mlp_in_dw")
    dx_mid, g["g_mlp"] = _rms_bwd(sv["x_mid"], sm["g_mlp"], dh2, dx, name="norm_mlp_bwd")
    dmb = dx_mid.astype(BF16)
    dmixed = _mm_nt(dmb, w["w_out"], name="out_proj_dx")
    g["w_out"] = _mm_tn(sv["mixed"], dmb, name="out_proj_dw")
    (do_a, do_b, do_c, dl_a, dl_b, dl_c, g["out_norm_a"], g["out_norm_b"], g["out_norm_c"]) = _outnorm_bwd(
        dmixed, sv["o_a"], sv["o_b"], sv["o_c"], sm["out_norm_a"], sm["out_norm_b"], sm["out_norm_c"])
    dqa2, dka2, dva = _dense_bwd(sv["qa2"], sv["ka2"], sv["va"], do_a, sv["lse_a"], dl_a)
    db = []
    for _, dil in DILATED_PAIRS:
        db += _banded_bwd(sv["qkvb"], do_b, sv["lse_b"], dl_b, dil)
    dq_c, dk_c, dv_c, dtfull = _natten_bwd(sv["qkvc"], sv["tfull"], do_c, sv["lse_c"], dl_c)
    g["rpb"] = _rpb_grad(dtfull, onehot, row_sel)
    dqa, dkva, dkpe = _a_post_bwd(dqa2, dka2, dva, t32)
    dcqn = _mm_nt(dqa, w["w_uq"], name="q_up_dx")
    g["w_uq"] = _unarrange_w_uq(_mm_tn(sv["cqn"], dqa, name="q_up_dw"))
    dckvn = _mm_nt(dkva, w["w_ukv"], name="kv_up_dx")
    g["w_ukv"] = _unarrange_w_ukv(_mm_tn(sv["ckvn"], dkva, name="kv_up_dw"))
    dproj, g["q_norm"], g["kv_norm"] = _prep_bwd(sv["proj"], sm["q_norm"], sm["kv_norm"], t32, t64, dcqn, dckvn, dkpe,
                                                  db, (dq_c, dk_c, dv_c))
    dh = _mm_nt(dproj, w["w_in"], name="in_proj_dx")
    g["w_in"] = _unarrange_w_in(_mm_tn(sv["h"], dproj, name="in_proj_dw"))
    dx_in, g["g_mix"] = _rms_bwd(sv["x"], sm["g_mix"], dh, dx_mid, name="norm_mix_bwd")
    return dx_in, g


def _local_step(x, target, wfull, small):
    s = x.shape[0]
    tabs = (_rope_tables(s, QK_ROPE // 2, 2), _rope_tables(s, HEAD_DIM // 2, 2))
    consts = _rpb_constants()
    saved = []
    for l in range(DEPTH):
        wl = {k: v[l] for k, v in wfull.items()}
        sl = {k: small[k][l] for k in SMALL if k != "g_final"}
        x, sv = _layer_fwd(x, wl, sl, tabs, consts)
        saved.append(sv)
    loss, dx, dg_final = _loss_head(x, small["g_final"], target)
    grads = [None] * DEPTH
    for l in reversed(range(DEPTH)):
        wl = {k: v[l] for k, v in wfull.items()}
        sl = {k: small[k][l] for k in SMALL if k != "g_final"}
        dx, grads[l] = _layer_bwd(dx, saved[l], wl, sl, tabs, consts)
    return loss, dx, grads, dg_final


ARRANGE = {"w_in": _arrange_w_in, "w_uq": _arrange_w_uq, "w_ukv": _arrange_w_ukv}


def kernel(x, g_mix, w_in, q_norm, w_uq, kv_norm, w_ukv, rpb, out_norm_a, out_norm_b, out_norm_c, w_out, g_mlp, w_mlp_in, w_mlp_out, g_final, loss_target, m_g_mix, m_w_in, m_q_norm, m_w_uq, m_kv_norm, m_w_ukv, m_rpb, m_out_norm_a, m_out_norm_b, m_out_norm_c, m_w_out, m_g_mlp, m_w_mlp_in, m_w_mlp_out, m_g_final, v_g_mix, v_w_in, v_q_norm, v_w_uq, v_kv_norm, v_w_ukv, v_rpb, v_out_norm_a, v_out_norm_b, v_out_norm_c, v_w_out, v_g_mlp, v_w_mlp_in, v_w_mlp_out, v_g_final):
    args = dict(locals())
    weights = {k: args[k] for k in BIG + SMALL}
    moms = {k: args["m_" + k] for k in BIG + SMALL}
    vels = {k: args["v_" + k] for k in BIG + SMALL}
    cc = lax.axis_index("c")

    shard_shapes = [weights[k].shape for k in BIG]
    gathered = _gather_chips(_pack_rows([weights[k].astype(BF16) for k in BIG]))
    per_chip = [_unpack_rows(gathered[j], shard_shapes) for j in range(4)]
    wfull = {}
    for idx, k in enumerate(BIG):
        full = _full_from_shards(k, jnp.stack([per_chip[j][idx] for j in range(4)]))
        wfull[k] = ARRANGE[k](full) if k in ARRANGE else full

    small = {k: weights[k] for k in SMALL}
    loss, dx, grads, dg_final = _local_step(x[0], loss_target[0], wfull, small)

    small_local = {k: jnp.stack([grads[l][k].reshape(weights[k].shape[1:]) for l in range(DEPTH)])
                   for k in SMALL if k != "g_final"}
    small_local["g_final"] = dg_final.reshape(-1)
    small_shapes = [weights[k].shape for k in SMALL]
    n_small = sum(int(np.prod(s)) for s in small_shapes)
    flat = jnp.concatenate([small_local[k].reshape(-1) for k in SMALL] + [loss[0, :1]])
    rows_small = -(-(n_small + 1) // PACK_C)
    rows_small += -rows_small % 8
    flat = jnp.pad(flat, (0, rows_small * PACK_C - n_small - 1)).reshape(rows_small, PACK_C)
    red = _all_reduce_small(flat).reshape(-1)
    loss_out = red[n_small]
    small_grads, off = {}, 0
    for k, shp in zip(SMALL, small_shapes):
        n = int(np.prod(shp))
        small_grads[k] = red[off:off + n].reshape(shp)
        off += n

    packed = jnp.stack([_pack_rows([_shards_from_full(k, jnp.stack([grads[l][k] for l in range(DEPTH)]))[j] for k in BIG])
                        for j in range(4)])
    rows = packed.shape[1]
    halves = packed.reshape(4, 2, rows // 2, PACK_C)
    mine = lax.dynamic_index_in_dim(halves, cc, axis=1, keepdims=False)
    other = lax.dynamic_index_in_dim(halves, 1 - cc, axis=1, keepdims=False)
    from_sibling = _swap_sibling(other)
    pair = _add_n([mine.reshape(-1, PACK_C), from_sibling.reshape(-1, PACK_C)], name="pair_sum").reshape(mine.shape)
    by_chip = _scatter_chips(pair)
    reduced = _add_n([by_chip[j] for j in range(4)], name="chip_sum")
    joined = _join_halves(reduced).reshape(rows, PACK_C)
    big_grads = dict(zip(BIG, _unpack_rows(joined, shard_shapes)))

    out_g, out_d, out_m, out_v = {}, {}, {}, {}
    for k in BIG:
        shp = weights[k].shape
        two_d = (shp[0] * shp[1], shp[2])
        d, nm, nv = _adamw(weights[k].reshape(two_d), big_grads[k].reshape(two_d), moms[k].reshape(two_d),
                           vels[k].reshape(two_d), name="adamw_" + k)
        out_g[k], out_d[k], out_m[k], out_v[k] = big_grads[k], d.reshape(shp), nm.reshape(shp), nv.reshape(shp)

    def pack_small(tree):
        f = jnp.concatenate([tree[k].reshape(-1) for k in SMALL])
        return jnp.pad(f, (0, rows_small * PACK_C - n_small)).reshape(rows_small, PACK_C)

    d, nm, nv = _adamw(pack_small(small), pack_small(small_grads), pack_small(moms), pack_small(vels), name="adamw_small")
    for tree, flat_out in ((out_d, d), (out_m, nm), (out_v, nv)):
        off = 0
        fo = flat_out.reshape(-1)
        for k, shp in zip(SMALL, small_shapes):
            n = int(np.prod(shp))
            tree[k] = fo[off:off + n].reshape(shp)
            off += n
    out_g.update(small_grads)

    order = ("g_mix", "w_in", "q_norm", "w_uq", "kv_norm", "w_ukv", "rpb", "out_norm_a", "out_norm_b", "out_norm_c", "w_out",
             "g_mlp", "w_mlp_in", "w_mlp_out", "g_final")
    return (loss_out, dx.reshape(x.shape), *[out_g[k] for k in order], *[out_d[k] for k in order],
            *[out_m[k] for k in order], *[out_v[k] for k in order])
```

```python
import math

import numpy as np
import jax
import jax.numpy as jnp
from jax import lax
from jax.experimental import pallas as pl
from jax.experimental.pallas import tpu as pltpu

F32 = jnp.float32
BF16 = jnp.bfloat16

D_MODEL = 1024
HEAD_DIM = 64
Q_LORA = 256
KV_LORA = 128
QK_ROPE = 32
HEADS_A = 6
HEADS_B = 6
HEADS_C = 4
DILATED_PAIRS = ((128, 1), (512, 4), (2048, 16))
BAND_HALF = 64
GRID_W = 64
NA_ROWS = 8
NA_COLS = 16
D_FF = 4096
ROPE_THETA = 10000.0
NORM_EPS = 1e-6
NEG_INF = -1e30
DEPTH = 4

LANE = 128
PROJ_W = 2432
COL_CKV = 256
COL_KPE = 384
COL_B = 512
COL_C = 1664
W_A2 = 768
WIDTH_AB = 384
WIDTH_C = 256
SCALE_A = (HEAD_DIM + QK_ROPE) ** -0.5
SCALE_BC = HEAD_DIM ** -0.5

ADAM_LR = 0.001
ADAM_B1 = 0.9
ADAM_B2 = 0.999
ADAM_EPS = 1e-08
ADAM_WD = 0.01
ADAM_STEP = 10

VMEM_LIMIT = 56 * 1024 * 1024
MESH_T = pl.DeviceIdType.MESH


def _cp(*sem):
    return pltpu.CompilerParams(dimension_semantics=sem or None, vmem_limit_bytes=VMEM_LIMIT)


def _tile(n, cands):
    for c in cands:
        if n % c == 0:
            return c
    return n


def _sds(shape, dtype):
    return jax.ShapeDtypeStruct(shape, dtype)


def _mm_nn(a, b, *, name, out_dtype=F32, res=None):
    m, k = a.shape
    n = b.shape[1]
    tm = _tile(m, (512, 256, 128))
    tn = _tile(n, (1024, 768, 512)) if n % LANE == 0 and n != PROJ_W else n

    def body(*refs):
        a_ref, b_ref = refs[0], refs[1]
        o_ref = refs[-1]
        acc = jnp.dot(a_ref[...], b_ref[...], preferred_element_type=F32)
        if res is not None:
            acc = refs[2][...] + acc
        o_ref[...] = acc.astype(o_ref.dtype)

    in_specs = [pl.BlockSpec((tm, k), lambda j, i: (i, 0)), pl.BlockSpec((k, tn), lambda j, i: (0, j))]
    args = [a, b]
    if res is not None:
        in_specs.append(pl.BlockSpec((tm, tn), lambda j, i: (i, j)))
        args.append(res)
    return pl.pallas_call(
        body, name=name, grid=(n // tn, m // tm), in_specs=in_specs,
        out_specs=pl.BlockSpec((tm, tn), lambda j, i: (i, j)), out_shape=_sds((m, n), out_dtype),
        compiler_params=_cp("parallel", "parallel"))(*args)


def _mm_mlp_in(h, w):
    m, k = h.shape
    n = w.shape[1]
    tm = _tile(m, (512, 256, 128))
    tn = _tile(n, (1024, 512))

    def body(a_ref, b_ref, u_ref, act_ref):
        u = jnp.dot(a_ref[...], b_ref[...], preferred_element_type=F32)
        u_ref[...] = u
        act_ref[...] = jnp.square(jnp.maximum(u, 0.0)).astype(BF16)

    spec_o = pl.BlockSpec((tm, tn), lambda j, i: (i, j))
    return pl.pallas_call(
        body, name="mlp_in", grid=(n // tn, m // tm),
        in_specs=[pl.BlockSpec((tm, k), lambda j, i: (i, 0)), pl.BlockSpec((k, tn), lambda j, i: (0, j))],
        out_specs=(spec_o, spec_o), out_shape=(_sds((m, n), F32), _sds((m, n), BF16)),
        compiler_params=_cp("parallel", "parallel"))(h, w)


def _mm_nt(a, b, *, name, out_dtype=F32, relu2_of=None):
    m, c = a.shape
    n = b.shape[0]
    tm = _tile(m, (512, 256, 128))
    tn = _tile(n, (1024, 512, 256, 128))

    def body(*refs):
        a_ref, b_ref = refs[0], refs[1]
        o_ref = refs[-1]
        acc = lax.dot_general(a_ref[...], b_ref[...], (((1,), (1,)), ((), ())), preferred_element_type=F32)
        if relu2_of is not None:
            acc = acc * (2.0 * jnp.maximum(refs[2][...], 0.0))
        o_ref[...] = acc.astype(o_ref.dtype)

    in_specs = [pl.BlockSpec((tm, c), lambda j, i: (i, 0)), pl.BlockSpec((tn, c), lambda j, i: (j, 0))]
    args = [a, b]
    if relu2_of is not None:
        in_specs.append(pl.BlockSpec((tm, tn), lambda j, i: (i, j)))
        args.append(relu2_of)
    return pl.pallas_call(
        body, name=name, grid=(n // tn, m // tm), in_specs=in_specs,
        out_specs=pl.BlockSpec((tm, tn), lambda j, i: (i, j)), out_shape=_sds((m, n), out_dtype),
        compiler_params=_cp("parallel", "parallel"))(*args)


def _mm_tn(a, b, *, name):
    m, ka = a.shape
    nb = b.shape[1]
    tka = _tile(ka, (512, 256, 128))
    tnb = _tile(nb, (1024, 768, 512)) if nb != PROJ_W else nb
    tc = _tile(m, (1024, 512, 256, 128))

    def body(a_ref, b_ref, o_ref):
        part = lax.dot_general(a_ref[...], b_ref[...], (((0,), (0,)), ((), ())), preferred_element_type=F32)

        @pl.when(pl.program_id(2) == 0)
        def _():
            o_ref[...] = part

        @pl.when(pl.program_id(2) != 0)
        def _():
            o_ref[...] += part

    return pl.pallas_call(
        body, name=name, grid=(ka // tka, nb // tnb, m // tc),
        in_specs=[pl.BlockSpec((tc, tka), lambda i, j, c: (c, i)), pl.BlockSpec((tc, tnb), lambda i, j, c: (c, j))],
        out_specs=pl.BlockSpec((tka, tnb), lambda i, j, c: (i, j)), out_shape=_sds((ka, nb), F32),
        compiler_params=_cp("parallel", "parallel", "arbitrary"))(a, b)


def _rstd(x):
    return lax.rsqrt(jnp.mean(x * x, axis=-1, keepdims=True) + NORM_EPS)


def _rms_bwd_rows(x, g, dy):
    r = _rstd(x)
    gy = dy * g
    c = jnp.sum(x * gy, axis=-1, keepdims=True) * (r * r * r) * (1.0 / x.shape[-1])
    return r * gy - x * c, jnp.sum(dy * x * r, axis=0, keepdims=True)


def _accum(ref, part, first):
    @pl.when(first)
    def _():
        ref[...] = part

    @pl.when(jnp.logical_not(first))
    def _():
        ref[...] += part


def _rope(x, c, s1, s2, sh):
    return x * c + pltpu.roll(x, LANE - sh, 1) * s1 + pltpu.roll(x, sh, 1) * s2


def _rope_t(g, c, s1, s2, sh):
    return g * c + pltpu.roll(g * s1, sh, 1) + pltpu.roll(g * s2, LANE - sh, 1)


def _rope_tables(s, half, reps):
    pos = jnp.arange(s, dtype=F32)
    inv_freq = ROPE_THETA ** (-jnp.arange(half, dtype=F32) / half)
    ang = pos[:, None] * inv_freq[None, :]
    cos, sin = jnp.cos(ang), jnp.sin(ang)
    zero = jnp.zeros_like(cos)
    pad = jnp.zeros((s, LANE - 2 * half * reps), F32)
    c = jnp.concatenate([cos, cos] * reps + [pad], axis=1)
    s1 = jnp.concatenate([-sin, zero] * reps + [pad], axis=1)
    s2 = jnp.concatenate([zero, sin] * reps + [pad], axis=1)
    return c, s1, s2


def _lane_lt64(shape):
    return lax.broadcasted_iota(jnp.int32, shape, len(shape) - 1) % LANE < HEAD_DIM


def _group_sum(x):
    outs = []
    for b in range(x.shape[1] // LANE):
        blk = x[:, b * LANE:(b + 1) * LANE]
        lo = _lane_lt64(blk.shape)
        s0 = jnp.sum(jnp.where(lo, blk, 0.0), axis=1, keepdims=True)
        s1 = jnp.sum(jnp.where(lo, 0.0, blk), axis=1, keepdims=True)
        outs.append(jnp.where(lo, s0, s1))
    return outs


def _row_spec(ts, w):
    return pl.BlockSpec((ts, w), lambda i: (i, 0))


def _fix_spec(w):
    return pl.BlockSpec((1, w), lambda i: (0, 0))


def _rms_fwd(x, g, *, name):
    s, d = x.shape
    ts = _tile(s, (512, 256, 128))

    def body(x_ref, g_ref, o_ref):
        xv = x_ref[...]
        o_ref[...] = (xv * _rstd(xv) * g_ref[...]).astype(BF16)

    return pl.pallas_call(
        body, name=name, grid=(s // ts,), in_specs=[_row_spec(ts, d), _fix_spec(d)], out_specs=_row_spec(ts, d),
        out_shape=_sds((s, d), BF16), compiler_params=_cp("parallel"))(x, g.reshape(1, d))


def _rms_bwd(x, g, dy, res, *, name):
    s, d = x.shape
    ts = _tile(s, (512, 256, 128))

    def body(x_ref, g_ref, dy_ref, res_ref, dx_ref, dg_ref):
        dx, dg = _rms_bwd_rows(x_ref[...], g_ref[...], dy_ref[...])
        dx_ref[...] = res_ref[...] + dx
        _accum(dg_ref, dg, pl.program_id(0) == 0)

    return pl.pallas_call(
        body, name=name, grid=(s // ts,),
        in_specs=[_row_spec(ts, d), _fix_spec(d), _row_spec(ts, d), _row_spec(ts, d)],
        out_specs=(_row_spec(ts, d), _fix_spec(d)), out_shape=(_sds((s, d), F32), _sds((1, d), F32)),
        compiler_params=_cp("arbitrary"))(x, g.reshape(1, d), dy, res)


def _loss_head(x, g, target):
    s, d = x.shape
    ts = _tile(s, (512, 256, 128))

    def body(x_ref, g_ref, t_ref, loss_ref, dx_ref, dg_ref):
        xv, gv = x_ref[...], g_ref[...]
        err = xv * _rstd(xv) * gv - t_ref[...]
        part = 0.5 * jnp.sum(jnp.sum(err * err, axis=-1, keepdims=True) * (1.0 / d), axis=0, keepdims=True)
        dx, dg = _rms_bwd_rows(xv, gv, err * (1.0 / d))
        dx_ref[...] = dx
        first = pl.program_id(0) == 0
        _accum(dg_ref, dg, first)
        _accum(loss_ref, jnp.broadcast_to(part, (1, LANE)), first)

    return pl.pallas_call(
        body, name="loss_head", grid=(s // ts,), in_specs=[_row_spec(ts, d), _fix_spec(d), _row_spec(ts, d)],
        out_specs=(_fix_spec(LANE), _row_spec(ts, d), _fix_spec(d)),
        out_shape=(_sds((1, LANE), F32), _sds((s, d), F32), _sds((1, d), F32)),
        compiler_params=_cp("arbitrary"))(x, g.reshape(1, d), target)


def _prep_fwd(proj, q_norm, kv_norm, t32, t64):
    s = proj.shape[0]
    ts = _tile(s, (256, 128))

    def body(p_ref, qn_ref, kn_ref, c32, a32, b32, c64, a64, b64, cqn_ref, ckvn_ref, kpe_ref, qkvb_ref, qkvc_ref):
        cq = p_ref[:, 0:Q_LORA]
        cqn_ref[...] = (cq * _rstd(cq) * qn_ref[...]).astype(BF16)
        ckv = p_ref[:, COL_CKV:COL_KPE]
        ckvn_ref[...] = (ckv * _rstd(ckv) * kn_ref[...]).astype(BF16)
        kp = p_ref[:, COL_KPE:COL_B]
        kp2 = kp + pltpu.roll(kp, QK_ROPE, 1)
        kpe_ref[...] = _rope(kp2, c32[...], a32[...], b32[...], QK_ROPE // 2).astype(BF16)
        for b in range(6):
            blk = _rope(p_ref[:, COL_B + b * LANE:COL_B + (b + 1) * LANE], c64[...], a64[...], b64[...], HEAD_DIM // 2)
            if b < 3:
                blk = blk * SCALE_BC
            qkvb_ref[:, b * LANE:(b + 1) * LANE] = blk.astype(BF16)
        qkvb_ref[:, 2 * WIDTH_AB:3 * WIDTH_AB] = p_ref[:, COL_B + 2 * WIDTH_AB:COL_C].astype(BF16)
        qkvc_ref[:, 0:WIDTH_C] = (p_ref[:, COL_C:COL_C + WIDTH_C] * SCALE_BC).astype(BF16)
        qkvc_ref[:, WIDTH_C:3 * WIDTH_C] = p_ref[:, COL_C + WIDTH_C:PROJ_W].astype(BF16)

    tab = [_row_spec(ts, LANE)] * 6
    return pl.pallas_call(
        body, name="prep_fwd", grid=(s // ts,),
        in_specs=[_row_spec(ts, PROJ_W), _fix_spec(Q_LORA), _fix_spec(KV_LORA)] + tab,
        out_specs=(_row_spec(ts, Q_LORA), _row_spec(ts, KV_LORA), _row_spec(ts, LANE), _row_spec(ts, 3 * WIDTH_AB),
                   _row_spec(ts, 3 * WIDTH_C)),
        out_shape=(_sds((s, Q_LORA), BF16), _sds((s, KV_LORA), BF16), _sds((s, LANE), BF16),
                   _sds((s, 3 * WIDTH_AB), BF16), _sds((s, 3 * WIDTH_C), BF16)),
        compiler_params=_cp("parallel"))(proj, q_norm.reshape(1, -1), kv_norm.reshape(1, -1), *t32, *t64)


def _prep_bwd(proj, q_norm, kv_norm, t32, t64, dcqn, dckvn, dkpe, db, dc):
    s = proj.shape[0]
    ts = _tile(s, (256, 128))

    def body(p_ref, qn_ref, kn_ref, c32, a32, b32, c64, a64, b64, dcqn_ref, dckvn_ref, dkpe_ref, *rest):
        db_refs, dc_refs = rest[0:9], rest[9:12]
        dp_ref, dqn_ref, dkn_ref = rest[12:15]
        first = pl.program_id(0) == 0
        dx, dg = _rms_bwd_rows(p_ref[:, 0:Q_LORA], qn_ref[...], dcqn_ref[...])
        dp_ref[:, 0:Q_LORA] = dx.astype(BF16)
        _accum(dqn_ref, dg, first)
        dx, dg = _rms_bwd_rows(p_ref[:, COL_CKV:COL_KPE], kn_ref[...], dckvn_ref[...])
        dp_ref[:, COL_CKV:COL_KPE] = dx.astype(BF16)
        _accum(dkn_ref, dg, first)
        g = _rope_t(dkpe_ref[...], c32[...], a32[...], b32[...], QK_ROPE // 2)
        g = g + pltpu.roll(g, LANE - QK_ROPE, 1)
        lane = lax.broadcasted_iota(jnp.int32, g.shape, 1)
        dp_ref[:, COL_KPE:COL_B] = jnp.where(lane < QK_ROPE, g, 0.0).astype(BF16)
        for which in range(3):
            for b in range(3):
                sl = slice(b * LANE, (b + 1) * LANE)
                g = db_refs[which][:, sl] + db_refs[3 + which][:, sl] + db_refs[6 + which][:, sl]
                if which < 2:
                    g = _rope_t(g, c64[...], a64[...], b64[...], HEAD_DIM // 2)
                if which == 0:
                    g = g * SCALE_BC
                col = COL_B + which * WIDTH_AB + b * LANE
                dp_ref[:, col:col + LANE] = g.astype(BF16)
        dp_ref[:, COL_C:COL_C + WIDTH_C] = (dc_refs[0][...] * SCALE_BC).astype(BF16)
        dp_ref[:, COL_C + WIDTH_C:COL_C + 2 * WIDTH_C] = dc_refs[1][...].astype(BF16)
        dp_ref[:, COL_C + 2 * WIDTH_C:PROJ_W] = dc_refs[2][...].astype(BF16)

    tab = [_row_spec(ts, LANE)] * 6
    in_specs = ([_row_spec(ts, PROJ_W), _fix_spec(Q_LORA), _fix_spec(KV_LORA)] + tab
                + [_row_spec(ts, Q_LORA), _row_spec(ts, KV_LORA), _row_spec(ts, LANE)]
                + [_row_spec(ts, WIDTH_AB)] * 9 + [_row_spec(ts, WIDTH_C)] * 3)
    return pl.pallas_call(
        body, name="prep_bwd", grid=(s // ts,), in_specs=in_specs,
        out_specs=(_row_spec(ts, PROJ_W), _fix_spec(Q_LORA), _fix_spec(KV_LORA)),
        out_shape=(_sds((s, PROJ_W), BF16), _sds((1, Q_LORA), F32), _sds((1, KV_LORA), F32)),
        compiler_params=_cp("arbitrary"))(proj, q_norm.reshape(1, -1), kv_norm.reshape(1, -1), *t32, *t64,
                                          dcqn, dckvn, dkpe, *db, *dc)


def _a_post_fwd(qa, kva, kpe, t32):
    s = qa.shape[0]
    ts = _tile(s, (512, 256, 128))

    def body(qa_ref, kva_ref, kpe_ref, c32, a32, b32, q_ref, k_ref, v_ref):
        for p in range(3):
            lo, hi = 2 * p * LANE, (2 * p + 1) * LANE
            q_ref[:, lo:hi] = qa_ref[:, lo:hi].astype(BF16)
            q_ref[:, hi:hi + LANE] = _rope(qa_ref[:, hi:hi + LANE], c32[...], a32[...], b32[...], QK_ROPE // 2).astype(BF16)
            k_ref[:, lo:hi] = kva_ref[:, p * LANE:(p + 1) * LANE].astype(BF16)
            k_ref[:, hi:hi + LANE] = kpe_ref[...]
            v_ref[:, lo:hi] = kva_ref[:, WIDTH_AB + p * LANE:WIDTH_AB + (p + 1) * LANE].astype(BF16)
            v_ref[:, hi:hi + LANE] = jnp.ones((ts, LANE), BF16)

    return pl.pallas_call(
        body, name="a_post_fwd", grid=(s // ts,),
        in_specs=[_row_spec(ts, W_A2), _row_spec(ts, W_A2), _row_spec(ts, LANE)] + [_row_spec(ts, LANE)] * 3,
        out_specs=(_row_spec(ts, W_A2), _row_spec(ts, W_A2), _row_spec(ts, W_A2)),
        out_shape=(_sds((s, W_A2), BF16), _sds((s, W_A2), BF16), _sds((s, W_A2), BF16)),
        compiler_params=_cp("parallel"))(qa, kva, kpe, *t32)


def _a_post_bwd(dqa2, dka2, dva, t32):
    s = dqa2.shape[0]
    ts = _tile(s, (512, 256, 128))

    def body(dq_ref, dk_ref, dv_ref, c32, a32, b32, dqa_ref, dkva_ref, dkpe_ref):
        acc = None
        for p in range(3):
            lo, hi = 2 * p * LANE, (2 * p + 1) * LANE
            dqa_ref[:, lo:hi] = (dq_ref[:, lo:hi] * SCALE_A).astype(BF16)
            dqa_ref[:, hi:hi + LANE] = _rope_t(dq_ref[:, hi:hi + LANE] * SCALE_A, c32[...], a32[...], b32[...],
                                               QK_ROPE // 2).astype(BF16)
            dkva_ref[:, p * LANE:(p + 1) * LANE] = (dk_ref[:, lo:hi] * SCALE_A).astype(BF16)
            part = dk_ref[:, hi:hi + LANE]
            acc = part if acc is None else acc + part
        dkva_ref[:, WIDTH_AB:2 * WIDTH_AB] = dv_ref[...].astype(BF16)
        dkpe_ref[...] = acc * SCALE_A

    return pl.pallas_call(
        body, name="a_post_bwd", grid=(s // ts,),
        in_specs=[_row_spec(ts, W_A2), _row_spec(ts, W_A2), _row_spec(ts, WIDTH_AB)] + [_row_spec(ts, LANE)] * 3,
        out_specs=(_row_spec(ts, W_A2), _row_spec(ts, W_A2), _row_spec(ts, LANE)),
        out_shape=(_sds((s, W_A2), BF16), _sds((s, W_A2), BF16), _sds((s, LANE), F32)),
        compiler_params=_cp("parallel"))(dqa2, dka2, dva, *t32)


def _pair_masks(width):
    lane = lax.broadcasted_iota(jnp.int32, (1, width), 1)
    m0 = lane < HEAD_DIM
    m1 = (lane >= HEAD_DIM) & (lane < LANE)
    if width == 2 * LANE:
        m0 = m0 | ((lane >= LANE) & (lane < LANE + QK_ROPE))
        m1 = m1 | ((lane >= LANE + QK_ROPE) & (lane < LANE + 2 * QK_ROPE))
    return m0, m1


def _nt(a, b):
    return lax.dot_general(a, b, (((1,), (1,)), ((), ())), preferred_element_type=F32)


def _tn(a, b):
    return lax.dot_general(a, b, (((0,), (0,)), ((), ())), preferred_element_type=F32)


def _softmax_pair(q, kk, vv, bias_fn, scale):
    outs, lses = [], []
    for hh, msk in enumerate(_pair_masks(q.shape[1])):
        s = _nt(jnp.where(msk, q, jnp.zeros_like(q)), kk)
        if scale != 1.0:
            s = s * scale
        s = bias_fn(hh, s)
        m = jnp.max(s, axis=1, keepdims=True)
        p = jnp.exp(s - m)
        l = jnp.sum(p, axis=1, keepdims=True)
        outs.append(jnp.dot(p.astype(BF16), vv, preferred_element_type=F32) / l)
        lses.append(m + jnp.log(l))
    lo = _lane_lt64(outs[0].shape)
    return jnp.where(lo, outs[0], outs[1]), jnp.where(lo, lses[0], lses[1])


def _softmax_pair_bwd(q, kk, vv, do, lse, delta, bias_fn, scale):
    dq = dk = dv = None
    dss = []
    vm = _pair_masks(LANE)
    for hh, msk in enumerate(_pair_masks(q.shape[1])):
        qm = jnp.where(msk, q, jnp.zeros_like(q))
        s = _nt(qm, kk)
        if scale != 1.0:
            s = s * scale
        s = bias_fn(hh, s)
        col = hh * HEAD_DIM
        p = jnp.exp(s - lse[:, col:col + 1])
        dp = _nt(jnp.where(vm[hh], do, jnp.zeros_like(do)), vv)
        ds = p * (dp - delta[:, col:col + 1])
        dss.append(ds)
        dsb = (ds * scale if scale != 1.0 else ds).astype(BF16)
        dq_h = jnp.where(msk, jnp.dot(dsb, kk, preferred_element_type=F32), 0.0)
        dk_h = _tn(dsb, qm)
        dv_h = jnp.where(vm[hh], _tn(p.astype(BF16), do), 0.0)
        dq = dq_h if dq is None else dq + dq_h
        dk = dk_h if dk is None else dk + dk_h
        dv = dv_h if dv is None else dv + dv_h
    return dq, dk, dv, dss


DENSE_FWD_TQ, DENSE_FWD_TK = 512, 8192
DENSE_BWD_TQ, DENSE_BWD_TK = 2048, 1024
LOG2E = math.log2(math.e)


def _dense_fwd(qa, ka, va1):
    s = qa.shape[0]
    tq, tk = min(DENSE_FWD_TQ, s), min(DENSE_FWD_TK, s)
    nk = s // tk
    c = SCALE_A * LOG2E

    def body(q_ref, k_ref, v_ref, o_ref, lse_ref, m_sc, acc_sc):
        j = pl.program_id(2)

        @pl.when(j == 0)
        def _():
            m_sc[...] = jnp.full(m_sc.shape, NEG_INF, F32)
            acc_sc[...] = jnp.zeros(acc_sc.shape, F32)

        q, kk, vv = q_ref[...], k_ref[...], v_ref[...]
        for hh, msk in enumerate(_pair_masks(2 * LANE)):
            sc = _nt(jnp.where(msk, q, jnp.zeros_like(q)), kk)
            m_prev = m_sc[hh]
            m_new = jnp.maximum(m_prev, jnp.max(sc, axis=1, keepdims=True))
            alpha = jnp.exp2((m_prev - m_new) * c)
            p = jnp.exp2((sc - m_new) * c)
            acc_sc[hh] = alpha * acc_sc[hh] + jnp.dot(p.astype(BF16), vv, preferred_element_type=F32)
            m_sc[hh] = m_new

        @pl.when(j == nk - 1)
        def _():
            lo = _lane_lt64((tq, LANE))
            a0, a1 = acc_sc[0], acc_sc[1]
            l0, l1 = a0[:, LANE:], a1[:, LANE:]
            o_ref[...] = jnp.where(lo, a0[:, :LANE] / l0, a1[:, :LANE] / l1)
            lse_ref[...] = jnp.where(lo, m_sc[0] * SCALE_A + jnp.log(l0), m_sc[1] * SCALE_A + jnp.log(l1))

    o_spec = pl.BlockSpec((tq, LANE), lambda p, i, j: (i, p))
    return pl.pallas_call(
        body, name="dense_fwd", grid=(3, s // tq, nk),
        in_specs=[pl.BlockSpec((tq, 2 * LANE), lambda p, i, j: (i, p)), pl.BlockSpec((tk, 2 * LANE), lambda p, i, j: (j, p)),
                  pl.BlockSpec((tk, 2 * LANE), lambda p, i, j: (j, p))],
        out_specs=(o_spec, o_spec), out_shape=(_sds((s, WIDTH_AB), F32), _sds((s, WIDTH_AB), F32)),
        scratch_shapes=[pltpu.VMEM((2, tq, 1), F32), pltpu.VMEM((2, tq, 2 * LANE), F32)],
        compiler_params=_cp("parallel", "parallel", "arbitrary"))(qa, ka, va1)


def _head_rows(lane_dense):
    s = lane_dense.shape[0]
    return lane_dense[:, ::HEAD_DIM].T.reshape(3, 2, s)


def _dense_bwd(qa, ka, va1, do, lse, delta):
    s = qa.shape[0]
    tq, tk = min(DENSE_BWD_TQ, s), min(DENSE_BWD_TK, s)
    c = SCALE_A * LOG2E
    stats = jnp.concatenate([_head_rows(lse) * LOG2E, _head_rows(delta), jnp.zeros((3, 4, s), F32)], axis=1)
    kat = ka.T

    def body(q_ref, k_ref, kt_ref, v_ref, do_ref, st_ref, dqt_ref, dk_ref, dv_ref):
        j, i = pl.program_id(1), pl.program_id(2)

        @pl.when((j == 0) & (i == 0))
        def _():
            dqt_ref[...] = jnp.zeros(dqt_ref.shape, F32)

        q, kk, kt, vv, do_, st = q_ref[...], k_ref[...], kt_ref[...], v_ref[...], do_ref[...], st_ref[0]
        row = lax.broadcasted_iota(jnp.int32, (2 * LANE, 1), 0)
        rmask = ((row < HEAD_DIM) | ((row >= LANE) & (row < LANE + QK_ROPE)),
                 ((row >= HEAD_DIM) & (row < LANE)) | ((row >= LANE + QK_ROPE) & (row < LANE + 2 * QK_ROPE)))
        vm = _pair_masks(LANE)
        dqt = dk = dv = None
        for hh, msk in enumerate(_pair_masks(2 * LANE)):
            qm = jnp.where(msk, q, jnp.zeros_like(q))
            dom = jnp.where(vm[hh], do_, jnp.zeros_like(do_))
            dpt = _nt(vv, dom)
            pt = jnp.exp2(_nt(kk, qm) * c - st[hh:hh + 1, :])
            dst = pt * (dpt - st[2 + hh:3 + hh, :])
            pb, dsb = pt.astype(BF16), dst.astype(BF16)
            dv_h = jnp.dot(pb, dom, preferred_element_type=F32)
            dk_h = jnp.dot(dsb, qm, preferred_element_type=F32)
            dqt_h = jnp.dot(jnp.where(rmask[hh], kt, jnp.zeros_like(kt)), dsb, preferred_element_type=F32)
            dqt = dqt_h if dqt is None else dqt + dqt_h
            dk = dk_h if dk is None else dk + dk_h
            dv = dv_h if dv is None else dv + dv_h
        cols = pl.ds(pl.multiple_of(i * tq, tq), tq)
        dqt_ref[:, cols] += dqt
        _accum(dk_ref, dk, i == 0)
        _accum(dv_ref, dv, i == 0)

    dqt, dk, dv = pl.pallas_call(
        body, name="dense_bwd", grid=(3, s // tk, s // tq),
        in_specs=[pl.BlockSpec((tq, 2 * LANE), lambda p, j, i: (i, p)), pl.BlockSpec((tk, 2 * LANE), lambda p, j, i: (j, p)),
                  pl.BlockSpec((2 * LANE, tk), lambda p, j, i: (p, j)), pl.BlockSpec((tk, LANE), lambda p, j, i: (j, 2 * p)),
                  pl.BlockSpec((tq, LANE), lambda p, j, i: (i, p)), pl.BlockSpec((1, 8, tq), lambda p, j, i: (p, 0, i))],
        out_specs=(pl.BlockSpec((2 * LANE, s), lambda p, j, i: (p, 0)), pl.BlockSpec((tk, 2 * LANE), lambda p, j, i: (j, p)),
                   pl.BlockSpec((tk, LANE), lambda p, j, i: (j, p))),
        out_shape=(_sds((W_A2, s), F32), _sds((s, W_A2), F32), _sds((s, WIDTH_AB), F32)),
        compiler_params=_cp("parallel", "arbitrary", "arbitrary"))(qa, ka, kat, va1, do, stats)
    return dqt.T, dk, dv


def _band_specs(t, n, dil, qoff, koff, voff, wblk):
    hpt = t // BAND_HALF
    last = n // BAND_HALF - 1

    def main(off):
        return pl.BlockSpec((t, LANE), lambda r, p, i: (i, r * wblk + off + p))

    def left(off):
        return pl.BlockSpec((BAND_HALF, LANE), lambda r, p, i: (jnp.maximum(i * hpt - 1, 0), r * wblk + off + p))

    def right(off):
        return pl.BlockSpec((BAND_HALF, LANE), lambda r, p, i: (jnp.minimum((i + 1) * hpt, last), r * wblk + off + p))

    return [main(qoff), left(koff), main(koff), right(koff), left(voff), main(voff), right(voff)]


def _band_bias(t, n):
    def fn(hh, s):
        i = pl.program_id(2)
        qpos = i * t + lax.broadcasted_iota(jnp.int32, s.shape, 0)
        kpos = i * t - BAND_HALF + lax.broadcasted_iota(jnp.int32, s.shape, 1)
        ok = (jnp.abs(qpos - kpos) <= BAND_HALF) & (kpos >= 0) & (kpos < n)
        return jnp.where(ok, s, NEG_INF)
    return fn


def _banded_fwd(qkvb, dil):
    s = qkvb.shape[0]
    n = s // dil
    t = _tile(n, (512, 256, 128, 64))
    view = qkvb.reshape(n, dil * 3 * WIDTH_AB)

    def body(q_ref, kl, km, kr, vl, vm, vr, o_ref, lse_ref):
        kk = jnp.concatenate([kl[...], km[...], kr[...]], axis=0)
        vv = jnp.concatenate([vl[...], vm[...], vr[...]], axis=0)
        o, lse = _softmax_pair(q_ref[...], kk, vv, _band_bias(t, n), 1.0)
        o_ref[...] = o
        lse_ref[...] = lse

    o_spec = pl.BlockSpec((t, LANE), lambda r, p, i: (i, r * 3 + p))
    o, lse = pl.pallas_call(
        body, name=f"banded_fwd_d{dil}", grid=(dil, 3, n // t), in_specs=_band_specs(t, n, dil, 0, 3, 6, 9),
        out_specs=(o_spec, o_spec), out_shape=(_sds((n, dil * WIDTH_AB), F32), _sds((n, dil * WIDTH_AB), F32)),
        compiler_params=_cp("parallel", "parallel", "parallel"))(*([view] * 7))
    return o.reshape(s, WIDTH_AB), lse.reshape(s, WIDTH_AB)


def _banded_bwd(qkvb, do, lse, delta, dil):
    s = qkvb.shape[0]
    n = s // dil
    t = _tile(n, (512, 256, 128, 64))
    view = qkvb.reshape(n, dil * 3 * WIDTH_AB)
    side = [a.reshape(n, dil * WIDTH_AB) for a in (do, lse, delta)]

    def body(q_ref, kl, km, kr, vl, vm, vr, do_ref, lse_ref, dl_ref, dq_ref, dk_ref, dv_ref):
        i = pl.program_id(2)

        @pl.when(i == 0)
        def _():
            dk_ref[...] = jnp.zeros(dk_ref.shape, F32)
            dv_ref[...] = jnp.zeros(dv_ref.shape, F32)

        kk = jnp.concatenate([kl[...], km[...], kr[...]], axis=0)
        vv = jnp.concatenate([vl[...], vm[...], vr[...]], axis=0)
        dq, dk, dv, _ = _softmax_pair_bwd(q_ref[...], kk, vv, do_ref[...], lse_ref[...], dl_ref[...], _band_bias(t, n), 1.0)
        dq_ref[...] = dq
        lrow = pl.multiple_of(jnp.maximum(i * t - BAND_HALF, 0), BAND_HALF)
        rrow = pl.multiple_of(jnp.minimum((i + 1) * t, n - BAND_HALF), BAND_HALF)
        mrow = pl.multiple_of(i * t, BAND_HALF)
        for ref, g in ((dk_ref, dk), (dv_ref, dv)):
            ref[pl.ds(lrow, BAND_HALF), :] += g[0:BAND_HALF]
            ref[pl.ds(mrow, t), :] += g[BAND_HALF:BAND_HALF + t]
            ref[pl.ds(rrow, BAND_HALF), :] += g[BAND_HALF + t:2 * BAND_HALF + t]

    q_spec = pl.BlockSpec((t, LANE), lambda r, p, i: (i, r * 3 + p))
    acc_spec = pl.BlockSpec((n, LANE), lambda r, p, i: (0, r * 3 + p))
    shp = _sds((n, dil * WIDTH_AB), F32)
    outs = pl.pallas_call(
        body, name=f"banded_bwd_d{dil}", grid=(dil, 3, n // t),
        in_specs=_band_specs(t, n, dil, 0, 3, 6, 9) + [q_spec, q_spec, q_spec],
        out_specs=(q_spec, acc_spec, acc_spec), out_shape=(shp, shp, shp),
        compiler_params=_cp("parallel", "parallel", "arbitrary"))(*([view] * 7), *side)
    return [a.reshape(s, WIDTH_AB) for a in outs]


def _merge_branches(outs, lses):
    s = outs[0].shape[0]
    ts = _tile(s, (512, 256, 128))

    def body(o1, o2, o3, l1, l2, l3, o_ref, lse_ref):
        a, b, c = l1[...], l2[...], l3[...]
        m = jnp.maximum(jnp.maximum(a, b), c)
        ea, eb, ec = jnp.exp(a - m), jnp.exp(b - m), jnp.exp(c - m)
        den = ea + eb + ec
        o_ref[...] = (o1[...] * ea + o2[...] * eb + o3[...] * ec) / den
        lse_ref[...] = m + jnp.log(den)

    sp = _row_spec(ts, WIDTH_AB)
    return pl.pallas_call(
        body, name="merge_branches", grid=(s // ts,), in_specs=[sp] * 6, out_specs=(sp, sp),
        out_shape=(_sds((s, WIDTH_AB), F32), _sds((s, WIDTH_AB), F32)), compiler_params=_cp("parallel"))(*outs, *lses)


def _na_geometry(s):
    rows = s // GRID_W
    assert rows >= 2 * NA_ROWS and rows % NA_ROWS == 0
    return rows, rows // NA_ROWS


def _na_row(n, i, rows):
    rq = n * NA_ROWS + i
    rs = jnp.clip(rq - NA_ROWS // 2, 0, rows - NA_ROWS)
    return pl.multiple_of(rs * GRID_W, GRID_W), rs - rq + NA_ROWS - 1


NA_KEYS = NA_ROWS * GRID_W


def _natten_fwd(qkvc, tfull):
    s = qkvc.shape[0]
    rows, nrb = _na_geometry(s)
    tq = NA_ROWS * GRID_W

    def body(q_ref, k_ref, v_ref, t_ref, o_ref, lse_ref):
        n = pl.program_id(1)
        for i in range(NA_ROWS):
            tok, base = _na_row(n, i, rows)
            kk, vv = k_ref[pl.ds(tok, NA_KEYS), :], v_ref[pl.ds(tok, NA_KEYS), :]
            sl = slice(i * GRID_W, (i + 1) * GRID_W)
            o, lse = _softmax_pair(q_ref[sl, :], kk, vv, lambda hh, sc: sc + t_ref[hh, base], 1.0)
            o_ref[sl, :] = o
            lse_ref[sl, :] = lse

    o_spec = pl.BlockSpec((tq, LANE), lambda p, n: (n, p))
    return pl.pallas_call(
        body, name="natten_fwd", grid=(2, nrb),
        in_specs=[pl.BlockSpec((tq, LANE), lambda p, n: (n, p)), pl.BlockSpec((s, LANE), lambda p, n: (0, 2 + p)),
                  pl.BlockSpec((s, LANE), lambda p, n: (0, 4 + p)),
                  pl.BlockSpec((2, NA_ROWS, GRID_W, NA_KEYS), lambda p, n: (p, 0, 0, 0))],
        out_specs=(o_spec, o_spec), out_shape=(_sds((s, WIDTH_C), F32), _sds((s, WIDTH_C), F32)),
        compiler_params=_cp("parallel", "parallel"))(qkvc, qkvc, qkvc, tfull)


def _natten_bwd(qkvc, tfull, do, lse, delta):
    s = qkvc.shape[0]
    rows, nrb = _na_geometry(s)
    tq = NA_ROWS * GRID_W

    def body(q_ref, k_ref, v_ref, t_ref, do_ref, lse_ref, dl_ref, dq_ref, dk_ref, dv_ref, dt_ref):
        n = pl.program_id(1)

        @pl.when(n == 0)
        def _():
            dk_ref[...] = jnp.zeros(dk_ref.shape, F32)
            dv_ref[...] = jnp.zeros(dv_ref.shape, F32)
            dt_ref[...] = jnp.zeros(dt_ref.shape, F32)

        for i in range(NA_ROWS):
            tok, base = _na_row(n, i, rows)
            win = pl.ds(tok, NA_KEYS)
            sl = slice(i * GRID_W, (i + 1) * GRID_W)
            dq, dk, dv, dss = _softmax_pair_bwd(q_ref[sl, :], k_ref[win, :], v_ref[win, :], do_ref[sl, :], lse_ref[sl, :],
                                                dl_ref[sl, :], lambda hh, sc: sc + t_ref[hh, base], 1.0)
            dq_ref[sl, :] = dq
            dk_ref[win, :] += dk
            dv_ref[win, :] += dv
            dt_ref[0, base] += dss[0]
            dt_ref[1, base] += dss[1]

    q_spec = pl.BlockSpec((tq, LANE), lambda p, n: (n, p))
    acc_spec = pl.BlockSpec((s, LANE), lambda p, n: (0, p))
    t_spec = pl.BlockSpec((2, NA_ROWS, GRID_W, NA_KEYS), lambda p, n: (p, 0, 0, 0))
    shp = _sds((s, WIDTH_C), F32)
    return pl.pallas_call(
        body, name="natten_bwd", grid=(2, nrb),
        in_specs=[q_spec, pl.BlockSpec((s, LANE), lambda p, n: (0, 2 + p)), pl.BlockSpec((s, LANE), lambda p, n: (0, 4 + p)),
                  t_spec, q_spec, q_spec, q_spec],
        out_specs=(q_spec, acc_spec, acc_spec, t_spec),
        out_shape=(shp, shp, shp, _sds((HEADS_C, NA_ROWS, GRID_W, NA_KEYS), F32)),
        compiler_params=_cp("parallel", "arbitrary"))(qkvc, qkvc, qkvc, tfull, do, lse, delta)


def _rpb_constants():
    p = np.arange(GRID_W)[:, None]
    qc = np.arange(GRID_W)[None, :]
    dc = np.clip(qc - p, -(NA_COLS - 1), NA_COLS - 1) + NA_COLS - 1
    onehot = (dc.reshape(1, -1) == np.arange(32)[:, None]).astype(np.float32)
    c_start = np.clip(p - NA_COLS // 2, 0, GRID_W - NA_COLS)
    col_ok = ((qc >= c_start) & (qc < c_start + NA_COLS)).reshape(1, -1).astype(np.float32)
    a = np.arange(16)[:, None]
    bj = np.arange(64)[None, :]
    row_sel = ((bj // 8 + bj % 8) == a).astype(np.float32)
    return jnp.asarray(onehot), jnp.asarray(col_ok), jnp.asarray(row_sel)


def _rpb_expand(rpb, onehot, col_ok):
    r2 = jnp.pad(rpb.reshape(HEADS_C * 15, 31), ((0, 4), (0, 1)))

    def body(r_ref, oh_ref, ok_ref, o_ref):
        t = jnp.dot(r_ref[...], oh_ref[...], preferred_element_type=F32, precision=lax.Precision.HIGHEST)
        o_ref[...] = jnp.where(ok_ref[...] > 0.5, t, NEG_INF)

    tm = pl.pallas_call(body, name="rpb_expand", out_shape=_sds((64, GRID_W * GRID_W), F32))(r2, onehot, col_ok)
    tm = tm[:HEADS_C * 15].reshape(HEADS_C, 15, GRID_W, GRID_W)
    tfull = jnp.stack([jnp.concatenate([tm[:, base + j] for j in range(NA_ROWS)], axis=-1) for base in range(NA_ROWS)], axis=1)
    return tfull


def _rpb_grad(dtfull, onehot, row_sel):
    g = dtfull.reshape(HEADS_C, NA_ROWS, GRID_W, NA_ROWS, GRID_W).transpose(0, 1, 3, 2, 4).reshape(HEADS_C, 64, GRID_W * GRID_W)

    def body(g_ref, oh_ref, sel_ref, o_ref):
        for h in range(HEADS_C):
            mid = lax.dot_general(g_ref[h], oh_ref[...], (((1,), (1,)), ((), ())), preferred_element_type=F32,
                                  precision=lax.Precision.HIGHEST)
            o_ref[h] = jnp.dot(sel_ref[...], mid, preferred_element_type=F32, precision=lax.Precision.HIGHEST)

    out = pl.pallas_call(body, name="rpb_grad", out_shape=_sds((HEADS_C, 16, 32), F32))(g, onehot, row_sel)
    return out[:, :15, :31]


def _outnorm_fwd(o_a, o_b, o_c, ga, gb, gc):
    s = o_a.shape[0]
    ts = _tile(s, (512, 256, 128))

    def body(a_ref, b_ref, c_ref, ga_ref, gb_ref, gc_ref, o_ref):
        col = 0
        for ref, g in ((a_ref, ga_ref), (b_ref, gb_ref), (c_ref, gc_ref)):
            x = ref[...]
            o_ref[:, col:col + x.shape[1]] = (x * _rstd(x) * g[...]).astype(BF16)
            col += x.shape[1]

    return pl.pallas_call(
        body, name="outnorm_fwd", grid=(s // ts,),
        in_specs=[_row_spec(ts, WIDTH_AB), _row_spec(ts, WIDTH_AB), _row_spec(ts, WIDTH_C), _fix_spec(WIDTH_AB),
                  _fix_spec(WIDTH_AB), _fix_spec(WIDTH_C)],
        out_specs=_row_spec(ts, D_MODEL), out_shape=_sds((s, D_MODEL), BF16),
        compiler_params=_cp("parallel"))(o_a, o_b, o_c, ga.reshape(1, -1), gb.reshape(1, -1), gc.reshape(1, -1))


def _outnorm_bwd(dmixed, o_a, o_b, o_c, ga, gb, gc):
    s = o_a.shape[0]
    ts = _tile(s, (512, 256, 128))

    def body(dm_ref, a_ref, b_ref, c_ref, ga_ref, gb_ref, gc_ref, *outs):
        first = pl.program_id(0) == 0
        col = 0
        for k, (ref, g) in enumerate(((a_ref, ga_ref), (b_ref, gb_ref), (c_ref, gc_ref))):
            x = ref[...]
            w = x.shape[1]
            dx, dg = _rms_bwd_rows(x, g[...], dm_ref[:, col:col + w])
            col += w
            outs[k][...] = dx.astype(BF16)
            for b, blk in enumerate(_group_sum(dx * x)):
                outs[3 + k][:, b * LANE:(b + 1) * LANE] = blk
            _accum(outs[6 + k], dg, first)

    widths = (WIDTH_AB, WIDTH_AB, WIDTH_C)
    return pl.pallas_call(
        body, name="outnorm_bwd", grid=(s // ts,),
        in_specs=[_row_spec(ts, D_MODEL)] + [_row_spec(ts, w) for w in widths] + [_fix_spec(w) for w in widths],
        out_specs=tuple([_row_spec(ts, w) for w in widths] * 2 + [_fix_spec(w) for w in widths]),
        out_shape=tuple([_sds((s, w), BF16) for w in widths] + [_sds((s, w), F32) for w in widths]
                        + [_sds((1, w), F32) for w in widths]),
        compiler_params=_cp("arbitrary"))(dmixed, o_a, o_b, o_c, ga.reshape(1, -1), gb.reshape(1, -1), gc.reshape(1, -1))


def _adamw(w, g, m, v, *, name):
    r, c = w.shape
    tr = _tile(r, (512, 256, 128, 64, 8))

    def body(w_ref, g_ref, m_ref, v_ref, d_ref, nm_ref, nv_ref):
        gv = g_ref[...]
        nm = ADAM_B1 * m_ref[...] + (1.0 - ADAM_B1) * gv
        nv = ADAM_B2 * v_ref[...] + (1.0 - ADAM_B2) * jnp.square(gv)
        m_hat = nm / (1.0 - ADAM_B1 ** ADAM_STEP)
        v_hat = nv / (1.0 - ADAM_B2 ** ADAM_STEP)
        d_ref[...] = -ADAM_LR * (m_hat / (jnp.sqrt(v_hat) + ADAM_EPS) + ADAM_WD * w_ref[...])
        nm_ref[...] = nm
        nv_ref[...] = nv

    sp = _row_spec(tr, c)
    return pl.pallas_call(
        body, name=name, grid=(r // tr,), in_specs=[sp] * 4, out_specs=(sp, sp, sp),
        out_shape=(_sds((r, c), F32),) * 3, compiler_params=_cp("parallel"))(w, g, m, v)


def _add_n(parts, *, name, out_dtype):
    r, c = parts[0].shape
    tr = max(t for t in range(16, 1025, 16) if r % t == 0)

    def body(*refs):
        acc = refs[0][...].astype(F32)
        for ref in refs[1:-1]:
            acc = acc + ref[...].astype(F32)
        refs[-1][...] = acc.astype(out_dtype)

    sp = _row_spec(tr, c)
    return pl.pallas_call(
        body, name=name, grid=(r // tr,), in_specs=[sp] * len(parts), out_specs=sp, out_shape=_sds((r, c), out_dtype),
        compiler_params=_cp("parallel"))(*parts)


ANY = pl.BlockSpec(memory_space=pl.ANY)
CHIP_FLIPS = ((1, 0), (0, 1), (1, 1))


def _me():
    return lax.axis_index("x"), lax.axis_index("y"), lax.axis_index("c")


def _gather_chips(half):
    def body(src, out, send_sems, recv_sems):
        x, y, c = _me()
        mine = 2 * x + y

        def copy(k, chip, half_idx, to, source=None):
            dst = out.at[chip, half_idx]
            return pltpu.make_async_remote_copy(src_ref=dst if source is None else source, dst_ref=dst,
                                                send_sem=send_sems.at[k], recv_sem=recv_sems.at[k], device_id=to,
                                                device_id_type=MESH_T)

        chips = [(x ^ fx, y ^ fy) for fx, fy in CHIP_FLIPS]
        first = [copy(k, mine, c, (cx, cy, c), source=src) for k, (cx, cy) in enumerate(chips)]
        for cp in first:
            cp.start()
        passed = []
        for k, (cx, cy) in enumerate(chips):
            theirs = 2 * cx + cy
            copy(k, theirs, c, (x, y, c)).wait_recv()
            cp = copy(3 + k, theirs, c, (x, y, 1 - c))
            cp.start()
            passed.append(cp)
        for k, (cx, cy) in enumerate(chips):
            copy(3 + k, 2 * cx + cy, 1 - c, (x, y, c)).wait_recv()
        for cp in first + passed:
            cp.wait_send()

    return pl.pallas_call(
        body, name="gather_chips", in_specs=[ANY], out_specs=ANY, out_shape=_sds((4, 2) + half.shape, half.dtype),
        scratch_shapes=[pltpu.SemaphoreType.DMA((6,)), pltpu.SemaphoreType.DMA((6,))])(half)


def _swap_sibling(block):
    def body(src, out, send_sem, recv_sem):
        x, y, c = _me()
        cp = pltpu.make_async_remote_copy(src_ref=src, dst_ref=out, send_sem=send_sem, recv_sem=recv_sem,
                                          device_id=(x, y, 1 - c), device_id_type=MESH_T)
        cp.start()
        cp.wait()

    return pl.pallas_call(
        body, name="swap_sibling", in_specs=[ANY], out_specs=ANY, out_shape=_sds(block.shape, block.dtype),
        scratch_shapes=[pltpu.SemaphoreType.DMA(()), pltpu.SemaphoreType.DMA(())])(block)


def _scatter_chips(parts):
    def body(src, out, send_sems, recv_sems):
        x, y, c = _me()
        mine = 2 * x + y
        sends = []
        for k, (fx, fy) in enumerate(CHIP_FLIPS):
            theirs = 2 * (x ^ fx) + (y ^ fy)
            cp = pltpu.make_async_remote_copy(src_ref=src.at[theirs], dst_ref=out.at[mine], send_sem=send_sems.at[k],
                                              recv_sem=recv_sems.at[k], device_id=(x ^ fx, y ^ fy, c), device_id_type=MESH_T)
            cp.start()
            sends.append(cp)
        for k, (fx, fy) in enumerate(CHIP_FLIPS):
            theirs = 2 * (x ^ fx) + (y ^ fy)
            pltpu.make_async_remote_copy(src_ref=src.at[theirs], dst_ref=out.at[theirs], send_sem=send_sems.at[k],
                                         recv_sem=recv_sems.at[k], device_id=(x ^ fx, y ^ fy, c),
                                         device_id_type=MESH_T).wait_recv()
        for cp in sends:
            cp.wait_send()

    return pl.pallas_call(
        body, name="scatter_chips", in_specs=[ANY], out_specs=ANY, out_shape=_sds(parts.shape, parts.dtype),
        scratch_shapes=[pltpu.SemaphoreType.DMA((3,)), pltpu.SemaphoreType.DMA((3,))])(parts)


def _all_reduce_small(block):
    r, c = block.shape

    def body(src, out, slots, send_sems, recv_sems):
        x, y, cc = _me()
        mine = 4 * x + 2 * y + cc
        slots[mine] = src[...]
        sends = []
        for k in range(1, 8):
            fx, fy, fc = (k >> 2) & 1, (k >> 1) & 1, k & 1
            cp = pltpu.make_async_remote_copy(src_ref=src, dst_ref=slots.at[mine], send_sem=send_sems.at[k - 1],
                                              recv_sem=recv_sems.at[k - 1], device_id=(x ^ fx, y ^ fy, cc ^ fc),
                                              device_id_type=MESH_T)
            cp.start()
            sends.append(cp)
        for k in range(1, 8):
            fx, fy, fc = (k >> 2) & 1, (k >> 1) & 1, k & 1
            theirs = 4 * (x ^ fx) + 2 * (y ^ fy) + (cc ^ fc)
            pltpu.make_async_remote_copy(src_ref=src, dst_ref=slots.at[theirs], send_sem=send_sems.at[k - 1],
                                         recv_sem=recv_sems.at[k - 1], device_id=(x ^ fx, y ^ fy, cc ^ fc),
                                         device_id_type=MESH_T).wait_recv()
        for cp in sends:
            cp.wait_send()
        acc = slots[0]
        for d in range(1, 8):
            acc = acc + slots[d]
        out[...] = acc

    vm = pl.BlockSpec(memory_space=pltpu.VMEM)
    return pl.pallas_call(
        body, name="all_reduce_small", in_specs=[vm], out_specs=vm, out_shape=_sds((r, c), F32),
        scratch_shapes=[pltpu.VMEM((8, r, c), F32), pltpu.SemaphoreType.DMA((7,)), pltpu.SemaphoreType.DMA((7,))])(block)


BIG = ("w_in", "w_uq", "w_ukv", "w_out", "w_mlp_in", "w_mlp_out")
COL_SHARDED = {"w_in": True, "w_uq": True, "w_ukv": True, "w_out": False, "w_mlp_in": True, "w_mlp_out": False}
SMALL = ("g_mix", "q_norm", "kv_norm", "rpb", "out_norm_a", "out_norm_b", "out_norm_c", "g_mlp", "g_final")
PACK_C = 1024
ROW_ALIGN = 32


def _pack_rows(parts):
    flat = jnp.concatenate([p.reshape(-1, PACK_C) for p in parts], axis=0)
    return jnp.pad(flat, ((0, -flat.shape[0] % ROW_ALIGN), (0, 0)))


def _unpack_rows(flat, shapes):
    out, row = [], 0
    for shp in shapes:
        n = int(np.prod(shp)) // PACK_C
        out.append(flat[row:row + n].reshape(shp))
        row += n
    return out


def _full_from_shards(name, g):
    if COL_SHARDED[name]:
        return g.transpose(1, 2, 0, 3).reshape(g.shape[1], g.shape[2], 4 * g.shape[3])
    return g.transpose(1, 0, 2, 3).reshape(g.shape[1], 4 * g.shape[2], g.shape[3])


def _shards_from_full(name, w):
    l, k, n = w.shape
    if COL_SHARDED[name]:
        return w.reshape(l, k, 4, n // 4).transpose(2, 0, 1, 3)
    return w.reshape(l, 4, k // 4, n).transpose(1, 0, 2, 3)


def _arrange_w_in(w):
    z = jnp.zeros(w.shape[:-1] + (COL_B - COL_KPE - QK_ROPE,), w.dtype)
    return jnp.concatenate([w[..., :COL_KPE + QK_ROPE], z, w[..., COL_KPE + QK_ROPE:]], axis=-1)


def _unarrange_w_in(w):
    return jnp.concatenate([w[..., :COL_KPE + QK_ROPE], w[..., COL_B:]], axis=-1)


def _arrange_w_uq(w):
    per = HEAD_DIM + QK_ROPE
    z = jnp.zeros(w.shape[:-1] + (HEAD_DIM,), w.dtype)
    cols = []
    for p in range(3):
        a, b = 2 * p * per, (2 * p + 1) * per
        cols += [w[..., a:a + HEAD_DIM], w[..., b:b + HEAD_DIM], w[..., a + HEAD_DIM:a + per], w[..., b + HEAD_DIM:b + per], z]
    return jnp.concatenate(cols, axis=-1)


def _unarrange_w_uq(w):
    cols = []
    for h in range(HEADS_A):
        p, e = divmod(h, 2)
        base = 2 * p * LANE
        cols += [w[..., base + e * HEAD_DIM:base + (e + 1) * HEAD_DIM],
                 w[..., base + LANE + e * QK_ROPE:base + LANE + (e + 1) * QK_ROPE]]
    return jnp.concatenate(cols, axis=-1)


def _arrange_w_ukv(w):
    ks = [w[..., h * LANE:h * LANE + HEAD_DIM] for h in range(HEADS_A)]
    vs = [w[..., h * LANE + HEAD_DIM:(h + 1) * LANE] for h in range(HEADS_A)]
    return jnp.concatenate(ks + vs, axis=-1)


def _unarrange_w_ukv(w):
    cols = []
    for h in range(HEADS_A):
        cols += [w[..., h * HEAD_DIM:(h + 1) * HEAD_DIM], w[..., WIDTH_AB + h * HEAD_DIM:WIDTH_AB + (h + 1) * HEAD_DIM]]
    return jnp.concatenate(cols, axis=-1)


def _layer_fwd(x, w, sm, tabs, consts):
    t32, t64 = tabs
    onehot, col_ok, _ = consts
    h = _rms_fwd(x, sm["g_mix"], name="norm_mix")
    proj = _mm_nn(h, w["w_in"], name="in_proj")
    cqn, ckvn, kpe, qkvb, qkvc = _prep_fwd(proj, sm["q_norm"], sm["kv_norm"], t32, t64)
    qa = _mm_nn(cqn, w["w_uq"], name="q_up")
    kva = _mm_nn(ckvn, w["w_ukv"], name="kv_up")
    qa2, ka2, va1 = _a_post_fwd(qa, kva, kpe, t32)
    o_a, lse_a = _dense_fwd(qa2, ka2, va1)
    branch = [_banded_fwd(qkvb, dil) for _, dil in DILATED_PAIRS]
    o_b, lse_b = _merge_branches([b[0] for b in branch], [b[1] for b in branch])
    tfull = _rpb_expand(sm["rpb"], onehot, col_ok)
    o_c, lse_c = _natten_fwd(qkvc, tfull)
    mixed = _outnorm_fwd(o_a, o_b, o_c, sm["out_norm_a"], sm["out_norm_b"], sm["out_norm_c"])
    x_mid = _mm_nn(mixed, w["w_out"], name="out_proj", res=x)
    h2 = _rms_fwd(x_mid, sm["g_mlp"], name="norm_mlp")
    u, act = _mm_mlp_in(h2, w["w_mlp_in"])
    x_out = _mm_nn(act, w["w_mlp_out"], name="mlp_out", res=x_mid)
    saved = dict(x=x, h=h, proj=proj, cqn=cqn, ckvn=ckvn, qkvb=qkvb, qkvc=qkvc, qa2=qa2, ka2=ka2, va1=va1, o_a=o_a,
                 lse_a=lse_a, o_b=o_b, lse_b=lse_b, o_c=o_c, lse_c=lse_c, tfull=tfull, mixed=mixed, x_mid=x_mid, h2=h2,
                 u=u, act=act)
    return x_out, saved


def _layer_bwd(dx, sv, w, sm, tabs, consts):
    t32, t64 = tabs
    onehot, _, row_sel = consts
    g = {}
    dxb = dx.astype(BF16)
    du = _mm_nt(dxb, w["w_mlp_out"], name="mlp_out_dx", out_dtype=BF16, relu2_of=sv["u"])
    g["w_mlp_out"] = _mm_tn(sv["act"], dxb, name="mlp_out_dw")
    dh2 = _mm_nt(du, w["w_mlp_in"], name="mlp_in_dx")
    g["w_mlp_in"] = _mm_tn(sv["h2"], du, name="mlp_in_dw")
    dx_mid, g["g_mlp"] = _rms_bwd(sv["x_mid"], sm["g_mlp"], dh2, dx, name="norm_mlp_bwd")
    dmb = dx_mid.astype(BF16)
    dmixed = _mm_nt(dmb, w["w_out"], name="out_proj_dx")
    g["w_out"] = _mm_tn(sv["mixed"], dmb, name="out_proj_dw")
    (do_a, do_b, do_c, dl_a, dl_b, dl_c, g["out_norm_a"], g["out_norm_b"], g["out_norm_c"]) = _outnorm_bwd(
        dmixed, sv["o_a"], sv["o_b"], sv["o_c"], sm["out_norm_a"], sm["out_norm_b"], sm["out_norm_c"])
    dqa2, dka2, dva = _dense_bwd(sv["qa2"], sv["ka2"], sv["va1"], do_a, sv["lse_a"], dl_a)
    db = []
    for _, dil in DILATED_PAIRS:
        db += _banded_bwd(sv["qkvb"], do_b, sv["lse_b"], dl_b, dil)
    dq_c, dk_c, dv_c, dtfull = _natten_bwd(sv["qkvc"], sv["tfull"], do_c, sv["lse_c"], dl_c)
    g["rpb"] = _rpb_grad(dtfull, onehot, row_sel)
    dqa, dkva, dkpe = _a_post_bwd(dqa2, dka2, dva, t32)
    dcqn = _mm_nt(dqa, w["w_uq"], name="q_up_dx")
    g["w_uq"] = _unarrange_w_uq(_mm_tn(sv["cqn"], dqa, name="q_up_dw"))
    dckvn = _mm_nt(dkva, w["w_ukv"], name="kv_up_dx")
    g["w_ukv"] = _unarrange_w_ukv(_mm_tn(sv["ckvn"], dkva, name="kv_up_dw"))
    dproj, g["q_norm"], g["kv_norm"] = _prep_bwd(sv["proj"], sm["q_norm"], sm["kv_norm"], t32, t64, dcqn, dckvn, dkpe,
                                                  db, (dq_c, dk_c, dv_c))
    dh = _mm_nt(dproj, w["w_in"], name="in_proj_dx")
    g["w_in"] = _unarrange_w_in(_mm_tn(sv["h"], dproj, name="in_proj_dw"))
    dx_in, g["g_mix"] = _rms_bwd(sv["x"], sm["g_mix"], dh, dx_mid, name="norm_mix_bwd")
    return dx_in, g


def _local_step(x, target, wfull, small):
    s = x.shape[0]
    tabs = (_rope_tables(s, QK_ROPE // 2, 2), _rope_tables(s, HEAD_DIM // 2, 2))
    consts = _rpb_constants()
    saved = []
    for l in range(DEPTH):
        wl = {k: v[l] for k, v in wfull.items()}
        sl = {k: small[k][l] for k in SMALL if k != "g_final"}
        x, sv = _layer_fwd(x, wl, sl, tabs, consts)
        saved.append(sv)
    loss, dx, dg_final = _loss_head(x, small["g_final"], target)
    grads = [None] * DEPTH
    for l in reversed(range(DEPTH)):
        wl = {k: v[l] for k, v in wfull.items()}
        sl = {k: small[k][l] for k in SMALL if k != "g_final"}
        dx, grads[l] = _layer_bwd(dx, saved[l], wl, sl, tabs, consts)
    return loss, dx, grads, dg_final


ARRANGE = {"w_in": _arrange_w_in, "w_uq": _arrange_w_uq, "w_ukv": _arrange_w_ukv}


def kernel(x, g_mix, w_in, q_norm, w_uq, kv_norm, w_ukv, rpb, out_norm_a, out_norm_b, out_norm_c, w_out, g_mlp, w_mlp_in, w_mlp_out, g_final, loss_target, m_g_mix, m_w_in, m_q_norm, m_w_uq, m_kv_norm, m_w_ukv, m_rpb, m_out_norm_a, m_out_norm_b, m_out_norm_c, m_w_out, m_g_mlp, m_w_mlp_in, m_w_mlp_out, m_g_final, v_g_mix, v_w_in, v_q_norm, v_w_uq, v_kv_norm, v_w_ukv, v_rpb, v_out_norm_a, v_out_norm_b, v_out_norm_c, v_w_out, v_g_mlp, v_w_mlp_in, v_w_mlp_out, v_g_final):
    args = dict(locals())
    weights = {k: args[k] for k in BIG + SMALL}
    moms = {k: args["m_" + k] for k in BIG + SMALL}
    vels = {k: args["v_" + k] for k in BIG + SMALL}
    cc = lax.axis_index("c")
    my_chip = 2 * lax.axis_index("x") + lax.axis_index("y")

    shard_shapes = [weights[k].shape for k in BIG]
    packed_w = _pack_rows([weights[k].astype(BF16) for k in BIG])
    rows = packed_w.shape[0]
    my_half = lax.dynamic_index_in_dim(packed_w.reshape(2, rows // 2, PACK_C), cc, axis=0, keepdims=False)
    gathered = _gather_chips(my_half).reshape(4, rows, PACK_C)
    per_chip = [_unpack_rows(jnp.where(my_chip == j, packed_w, gathered[j]), shard_shapes) for j in range(4)]
    wfull = {}
    for idx, k in enumerate(BIG):
        full = _full_from_shards(k, jnp.stack([per_chip[j][idx] for j in range(4)]))
        wfull[k] = ARRANGE[k](full) if k in ARRANGE else full

    small = {k: weights[k] for k in SMALL}
    loss, dx, grads, dg_final = _local_step(x[0], loss_target[0], wfull, small)

    small_local = {k: jnp.stack([grads[l][k].reshape(weights[k].shape[1:]) for l in range(DEPTH)])
                   for k in SMALL if k != "g_final"}
    small_local["g_final"] = dg_final.reshape(-1)
    small_shapes = [weights[k].shape for k in SMALL]
    n_small = sum(int(np.prod(s)) for s in small_shapes)
    flat = jnp.concatenate([small_local[k].reshape(-1) for k in SMALL] + [loss[0, :1]])
    rows_small = -(-(n_small + 1) // PACK_C)
    rows_small += -rows_small % 8
    flat = jnp.pad(flat, (0, rows_small * PACK_C - n_small - 1)).reshape(rows_small, PACK_C)
    red = _all_reduce_small(flat).reshape(-1)
    loss_out = red[n_small]
    small_grads, off = {}, 0
    for k, shp in zip(SMALL, small_shapes):
        n = int(np.prod(shp))
        small_grads[k] = red[off:off + n].reshape(shp)
        off += n

    by_shard = {k: _shards_from_full(k, jnp.stack([grads[l][k] for l in range(DEPTH)])) for k in BIG}
    packed = jnp.stack([_pack_rows([by_shard[k][j] for k in BIG]) for j in range(4)])
    halves = packed.reshape(4, 2, rows // 2, PACK_C)
    mine = lax.dynamic_index_in_dim(halves, cc, axis=1, keepdims=False)
    other = lax.dynamic_index_in_dim(halves, 1 - cc, axis=1, keepdims=False)
    from_sibling = _swap_sibling(other)
    pair = _add_n([mine.reshape(-1, PACK_C), from_sibling.reshape(-1, PACK_C)], name="pair_sum",
                  out_dtype=BF16).reshape(mine.shape)
    by_chip = _scatter_chips(pair)
    reduced = _add_n([jnp.where(my_chip == j, pair[j], by_chip[j]) for j in range(4)], name="chip_sum", out_dtype=F32)
    theirs = _swap_sibling(reduced)
    joined = jnp.where(cc == 0, jnp.concatenate([reduced, theirs]), jnp.concatenate([theirs, reduced]))
    big_grads = dict(zip(BIG, _unpack_rows(joined, shard_shapes)))

    out_g, out_d, out_m, out_v = {}, {}, {}, {}
    for k in BIG:
        shp = weights[k].shape
        two_d = (shp[0] * shp[1], shp[2])
        d, nm, nv = _adamw(weights[k].reshape(two_d), big_grads[k].reshape(two_d), moms[k].reshape(two_d),
                           vels[k].reshape(two_d), name="adamw_" + k)
        out_g[k], out_d[k], out_m[k], out_v[k] = big_grads[k], d.reshape(shp), nm.reshape(shp), nv.reshape(shp)

    def pack_small(tree):
        f = jnp.concatenate([tree[k].reshape(-1) for k in SMALL])
        return jnp.pad(f, (0, rows_small * PACK_C - n_small)).reshape(rows_small, PACK_C)

    d, nm, nv = _adamw(pack_small(small), pack_small(small_grads), pack_small(moms), pack_small(vels), name="adamw_small")
    for tree, flat_out in ((out_d, d), (out_m, nm), (out_v, nv)):
        off = 0
        fo = flat_out.reshape(-1)
        for k, shp in zip(SMALL, small_shapes):
            n = int(np.prod(shp))
            tree[k] = fo[off:off + n].reshape(shp)
            off += n
    out_g.update(small_grads)

    order = ("g_mix", "w_in", "q_norm", "w_uq", "kv_norm", "w_ukv", "rpb", "out_norm_a", "out_norm_b", "out_norm_c", "w_out",
             "g_mlp", "w_mlp_in", "w_mlp_out", "g_final")
    return (loss_out, dx.reshape(x.shape), *[out_g[k] for k in order], *[out_d[k] for k in order],
            *[out_m[k] for k in order], *[out_v[k] for k in order])
```

```python
import math

import numpy as np
import jax
import jax.numpy as jnp
from jax import lax
from jax.experimental import pallas as pl
from jax.experimental.pallas import tpu as pltpu

F32 = jnp.float32
BF16 = jnp.bfloat16

D_MODEL = 1024
HEAD_DIM = 64
Q_LORA = 256
KV_LORA = 128
QK_ROPE = 32
HEADS_A = 6
HEADS_B = 6
HEADS_C = 4
DILATED_PAIRS = ((128, 1), (512, 4), (2048, 16))
BAND_HALF = 64
GRID_W = 64
NA_ROWS = 8
NA_COLS = 16
D_FF = 4096
ROPE_THETA = 10000.0
NORM_EPS = 1e-6
NEG_INF = -1e30
DEPTH = 4

LANE = 128
PROJ_W = 2432
COL_CKV = 256
COL_KPE = 384
COL_B = 512
COL_C = 1664
W_A2 = 768
WIDTH_AB = 384
WIDTH_C = 256
SCALE_A = (HEAD_DIM + QK_ROPE) ** -0.5
SCALE_BC = HEAD_DIM ** -0.5

ADAM_LR = 0.001
ADAM_B1 = 0.9
ADAM_B2 = 0.999
ADAM_EPS = 1e-08
ADAM_WD = 0.01
ADAM_STEP = 10

VMEM_LIMIT = 56 * 1024 * 1024
MESH_T = pl.DeviceIdType.MESH


def _cp(*sem):
    return pltpu.CompilerParams(dimension_semantics=sem or None, vmem_limit_bytes=VMEM_LIMIT)


def _tile(n, cands):
    for c in cands:
        if n % c == 0:
            return c
    return n


def _sds(shape, dtype):
    return jax.ShapeDtypeStruct(shape, dtype)


def _mm_nn(a, b, *, name, out_dtype=F32, res=None):
    m, k = a.shape
    n = b.shape[1]
    tm = _tile(m, (512, 256, 128))
    tn = _tile(n, (1024, 768, 512)) if n % LANE == 0 and n != PROJ_W else n

    def body(*refs):
        a_ref, b_ref = refs[0], refs[1]
        o_ref = refs[-1]
        acc = jnp.dot(a_ref[...], b_ref[...], preferred_element_type=F32)
        if res is not None:
            acc = refs[2][...] + acc
        o_ref[...] = acc.astype(o_ref.dtype)

    in_specs = [pl.BlockSpec((tm, k), lambda j, i: (i, 0)), pl.BlockSpec((k, tn), lambda j, i: (0, j))]
    args = [a, b]
    if res is not None:
        in_specs.append(pl.BlockSpec((tm, tn), lambda j, i: (i, j)))
        args.append(res)
    return pl.pallas_call(
        body, name=name, grid=(n // tn, m // tm), in_specs=in_specs,
        out_specs=pl.BlockSpec((tm, tn), lambda j, i: (i, j)), out_shape=_sds((m, n), out_dtype),
        compiler_params=_cp("parallel", "parallel"))(*args)


def _mm_mlp_in(h, w):
    m, k = h.shape
    n = w.shape[1]
    tm = _tile(m, (512, 256, 128))
    tn = _tile(n, (1024, 512))

    def body(a_ref, b_ref, act_ref):
        u = jnp.dot(a_ref[...], b_ref[...], preferred_element_type=F32)
        act_ref[...] = jnp.square(jnp.maximum(u, 0.0)).astype(BF16)

    return pl.pallas_call(
        body, name="mlp_in", grid=(n // tn, m // tm),
        in_specs=[pl.BlockSpec((tm, k), lambda j, i: (i, 0)), pl.BlockSpec((k, tn), lambda j, i: (0, j))],
        out_specs=pl.BlockSpec((tm, tn), lambda j, i: (i, j)), out_shape=_sds((m, n), BF16),
        compiler_params=_cp("parallel", "parallel"))(h, w)


def _mm_nt(a, b, *, name, out_dtype=F32, relu2_act=None):
    m, c = a.shape
    n = b.shape[0]
    tm = _tile(m, (512, 256, 128))
    tn = _tile(n, (1024, 512, 256, 128))

    def body(*refs):
        a_ref, b_ref = refs[0], refs[1]
        o_ref = refs[-1]
        acc = lax.dot_general(a_ref[...], b_ref[...], (((1,), (1,)), ((), ())), preferred_element_type=F32)
        if relu2_act is not None:
            acc = acc * (2.0 * jnp.sqrt(refs[2][...].astype(F32)))
        o_ref[...] = acc.astype(o_ref.dtype)

    in_specs = [pl.BlockSpec((tm, c), lambda j, i: (i, 0)), pl.BlockSpec((tn, c), lambda j, i: (j, 0))]
    args = [a, b]
    if relu2_act is not None:
        in_specs.append(pl.BlockSpec((tm, tn), lambda j, i: (i, j)))
        args.append(relu2_act)
    return pl.pallas_call(
        body, name=name, grid=(n // tn, m // tm), in_specs=in_specs,
        out_specs=pl.BlockSpec((tm, tn), lambda j, i: (i, j)), out_shape=_sds((m, n), out_dtype),
        compiler_params=_cp("parallel", "parallel"))(*args)


def _mm_tn(a, b, *, name):
    m, ka = a.shape
    nb = b.shape[1]
    tka = _tile(ka, (512, 256, 128))
    tnb = _tile(nb, (1024, 768, 512)) if nb != PROJ_W else nb
    tc = _tile(m, (1024, 512, 256, 128))

    def body(a_ref, b_ref, o_ref):
        part = lax.dot_general(a_ref[...], b_ref[...], (((0,), (0,)), ((), ())), preferred_element_type=F32)

        @pl.when(pl.program_id(2) == 0)
        def _():
            o_ref[...] = part

        @pl.when(pl.program_id(2) != 0)
        def _():
            o_ref[...] += part

    return pl.pallas_call(
        body, name=name, grid=(ka // tka, nb // tnb, m // tc),
        in_specs=[pl.BlockSpec((tc, tka), lambda i, j, c: (c, i)), pl.BlockSpec((tc, tnb), lambda i, j, c: (c, j))],
        out_specs=pl.BlockSpec((tka, tnb), lambda i, j, c: (i, j)), out_shape=_sds((ka, nb), F32),
        compiler_params=_cp("parallel", "parallel", "arbitrary"))(a, b)


def _rstd(x):
    return lax.rsqrt(jnp.mean(x * x, axis=-1, keepdims=True) + NORM_EPS)


def _rms_bwd_rows(x, g, dy):
    r = _rstd(x)
    gy = dy * g
    c = jnp.sum(x * gy, axis=-1, keepdims=True) * (r * r * r) * (1.0 / x.shape[-1])
    return r * gy - x * c, jnp.sum(dy * x * r, axis=0, keepdims=True)


def _accum(ref, part, first):
    @pl.when(first)
    def _():
        ref[...] = part

    @pl.when(jnp.logical_not(first))
    def _():
        ref[...] += part


def _rope(x, c, s1, s2, sh):
    return x * c + pltpu.roll(x, LANE - sh, 1) * s1 + pltpu.roll(x, sh, 1) * s2


def _rope_t(g, c, s1, s2, sh):
    return g * c + pltpu.roll(g * s1, sh, 1) + pltpu.roll(g * s2, LANE - sh, 1)


def _rope_tables(s, half, reps):
    pos = jnp.arange(s, dtype=F32)
    inv_freq = ROPE_THETA ** (-jnp.arange(half, dtype=F32) / half)
    ang = pos[:, None] * inv_freq[None, :]
    cos, sin = jnp.cos(ang), jnp.sin(ang)
    zero = jnp.zeros_like(cos)
    pad = jnp.zeros((s, LANE - 2 * half * reps), F32)
    c = jnp.concatenate([cos, cos] * reps + [pad], axis=1)
    s1 = jnp.concatenate([-sin, zero] * reps + [pad], axis=1)
    s2 = jnp.concatenate([zero, sin] * reps + [pad], axis=1)
    return c, s1, s2


def _lane_lt64(shape):
    return lax.broadcasted_iota(jnp.int32, shape, len(shape) - 1) % LANE < HEAD_DIM


def _group_sum(x):
    outs = []
    for b in range(x.shape[1] // LANE):
        blk = x[:, b * LANE:(b + 1) * LANE]
        lo = _lane_lt64(blk.shape)
        s0 = jnp.sum(jnp.where(lo, blk, 0.0), axis=1, keepdims=True)
        s1 = jnp.sum(jnp.where(lo, 0.0, blk), axis=1, keepdims=True)
        outs.append(jnp.where(lo, s0, s1))
    return outs


def _row_spec(ts, w):
    return pl.BlockSpec((ts, w), lambda i: (i, 0))


def _fix_spec(w):
    return pl.BlockSpec((1, w), lambda i: (0, 0))


def _rms_fwd(x, g, *, name):
    s, d = x.shape
    ts = _tile(s, (512, 256, 128))

    def body(x_ref, g_ref, o_ref):
        xv = x_ref[...]
        o_ref[...] = (xv * _rstd(xv) * g_ref[...]).astype(BF16)

    return pl.pallas_call(
        body, name=name, grid=(s // ts,), in_specs=[_row_spec(ts, d), _fix_spec(d)], out_specs=_row_spec(ts, d),
        out_shape=_sds((s, d), BF16), compiler_params=_cp("parallel"))(x, g.reshape(1, d))


def _rms_bwd(x, g, dy, res, *, name):
    s, d = x.shape
    ts = _tile(s, (512, 256, 128))

    def body(x_ref, g_ref, dy_ref, res_ref, dx_ref, dg_ref):
        dx, dg = _rms_bwd_rows(x_ref[...], g_ref[...], dy_ref[...])
        dx_ref[...] = res_ref[...] + dx
        _accum(dg_ref, dg, pl.program_id(0) == 0)

    return pl.pallas_call(
        body, name=name, grid=(s // ts,),
        in_specs=[_row_spec(ts, d), _fix_spec(d), _row_spec(ts, d), _row_spec(ts, d)],
        out_specs=(_row_spec(ts, d), _fix_spec(d)), out_shape=(_sds((s, d), F32), _sds((1, d), F32)),
        compiler_params=_cp("arbitrary"))(x, g.reshape(1, d), dy, res)


def _loss_head(x, g, target):
    s, d = x.shape
    ts = _tile(s, (512, 256, 128))

    def body(x_ref, g_ref, t_ref, loss_ref, dx_ref, dg_ref):
        xv, gv = x_ref[...], g_ref[...]
        err = xv * _rstd(xv) * gv - t_ref[...]
        part = 0.5 * jnp.sum(jnp.sum(err * err, axis=-1, keepdims=True) * (1.0 / d), axis=0, keepdims=True)
        dx, dg = _rms_bwd_rows(xv, gv, err * (1.0 / d))
        dx_ref[...] = dx
        first = pl.program_id(0) == 0
        _accum(dg_ref, dg, first)
        _accum(loss_ref, jnp.broadcast_to(part, (1, LANE)), first)

    return pl.pallas_call(
        body, name="loss_head", grid=(s // ts,), in_specs=[_row_spec(ts, d), _fix_spec(d), _row_spec(ts, d)],
        out_specs=(_fix_spec(LANE), _row_spec(ts, d), _fix_spec(d)),
        out_shape=(_sds((1, LANE), F32), _sds((s, d), F32), _sds((1, d), F32)),
        compiler_params=_cp("arbitrary"))(x, g.reshape(1, d), target)


def _prep_fwd(proj, q_norm, kv_norm, t32, t64):
    s = proj.shape[0]
    ts = _tile(s, (256, 128))

    def body(p_ref, qn_ref, kn_ref, c32, a32, b32, c64, a64, b64, cqn_ref, ckvn_ref, kpe_ref, qkvb_ref, qkvc_ref):
        cq = p_ref[:, 0:Q_LORA]
        cqn_ref[...] = (cq * _rstd(cq) * qn_ref[...]).astype(BF16)
        ckv = p_ref[:, COL_CKV:COL_KPE]
        ckvn_ref[...] = (ckv * _rstd(ckv) * kn_ref[...]).astype(BF16)
        kp = p_ref[:, COL_KPE:COL_B]
        kp2 = kp + pltpu.roll(kp, QK_ROPE, 1)
        kpe_ref[...] = _rope(kp2, c32[...], a32[...], b32[...], QK_ROPE // 2).astype(BF16)
        for b in range(6):
            blk = _rope(p_ref[:, COL_B + b * LANE:COL_B + (b + 1) * LANE], c64[...], a64[...], b64[...], HEAD_DIM // 2)
            if b < 3:
                blk = blk * SCALE_BC
            qkvb_ref[:, b * LANE:(b + 1) * LANE] = blk.astype(BF16)
        qkvb_ref[:, 2 * WIDTH_AB:3 * WIDTH_AB] = p_ref[:, COL_B + 2 * WIDTH_AB:COL_C].astype(BF16)
        qkvc_ref[:, 0:WIDTH_C] = (p_ref[:, COL_C:COL_C + WIDTH_C] * SCALE_BC).astype(BF16)
        qkvc_ref[:, WIDTH_C:3 * WIDTH_C] = p_ref[:, COL_C + WIDTH_C:PROJ_W].astype(BF16)

    tab = [_row_spec(ts, LANE)] * 6
    return pl.pallas_call(
        body, name="prep_fwd", grid=(s // ts,),
        in_specs=[_row_spec(ts, PROJ_W), _fix_spec(Q_LORA), _fix_spec(KV_LORA)] + tab,
        out_specs=(_row_spec(ts, Q_LORA), _row_spec(ts, KV_LORA), _row_spec(ts, LANE), _row_spec(ts, 3 * WIDTH_AB),
                   _row_spec(ts, 3 * WIDTH_C)),
        out_shape=(_sds((s, Q_LORA), BF16), _sds((s, KV_LORA), BF16), _sds((s, LANE), BF16),
                   _sds((s, 3 * WIDTH_AB), BF16), _sds((s, 3 * WIDTH_C), BF16)),
        compiler_params=_cp("parallel"))(proj, q_norm.reshape(1, -1), kv_norm.reshape(1, -1), *t32, *t64)


def _prep_bwd(proj, q_norm, kv_norm, t32, t64, dcqn, dckvn, dkpe, db, dc):
    s = proj.shape[0]
    ts = _tile(s, (256, 128))

    def body(p_ref, qn_ref, kn_ref, c32, a32, b32, c64, a64, b64, dcqn_ref, dckvn_ref, dkpe_ref, *rest):
        db_refs, dc_refs = rest[0:9], rest[9:12]
        dp_ref, dqn_ref, dkn_ref = rest[12:15]
        first = pl.program_id(0) == 0
        dx, dg = _rms_bwd_rows(p_ref[:, 0:Q_LORA], qn_ref[...], dcqn_ref[...])
        dp_ref[:, 0:Q_LORA] = dx.astype(BF16)
        _accum(dqn_ref, dg, first)
        dx, dg = _rms_bwd_rows(p_ref[:, COL_CKV:COL_KPE], kn_ref[...], dckvn_ref[...])
        dp_ref[:, COL_CKV:COL_KPE] = dx.astype(BF16)
        _accum(dkn_ref, dg, first)
        g = _rope_t(dkpe_ref[...], c32[...], a32[...], b32[...], QK_ROPE // 2)
        g = g + pltpu.roll(g, LANE - QK_ROPE, 1)
        lane = lax.broadcasted_iota(jnp.int32, g.shape, 1)
        dp_ref[:, COL_KPE:COL_B] = jnp.where(lane < QK_ROPE, g, 0.0).astype(BF16)
        for which in range(3):
            for b in range(3):
                sl = slice(b * LANE, (b + 1) * LANE)
                g = db_refs[which][:, sl] + db_refs[3 + which][:, sl] + db_refs[6 + which][:, sl]
                if which < 2:
                    g = _rope_t(g, c64[...], a64[...], b64[...], HEAD_DIM // 2)
                if which == 0:
                    g = g * SCALE_BC
                col = COL_B + which * WIDTH_AB + b * LANE
                dp_ref[:, col:col + LANE] = g.astype(BF16)
        dp_ref[:, COL_C:COL_C + WIDTH_C] = (dc_refs[0][...] * SCALE_BC).astype(BF16)
        dp_ref[:, COL_C + WIDTH_C:COL_C + 2 * WIDTH_C] = dc_refs[1][...].astype(BF16)
        dp_ref[:, COL_C + 2 * WIDTH_C:PROJ_W] = dc_refs[2][...].astype(BF16)

    tab = [_row_spec(ts, LANE)] * 6
    in_specs = ([_row_spec(ts, PROJ_W), _fix_spec(Q_LORA), _fix_spec(KV_LORA)] + tab
                + [_row_spec(ts, Q_LORA), _row_spec(ts, KV_LORA), _row_spec(ts, LANE)]
                + [_row_spec(ts, WIDTH_AB)] * 9 + [_row_spec(ts, WIDTH_C)] * 3)
    return pl.pallas_call(
        body, name="prep_bwd", grid=(s // ts,), in_specs=in_specs,
        out_specs=(_row_spec(ts, PROJ_W), _fix_spec(Q_LORA), _fix_spec(KV_LORA)),
        out_shape=(_sds((s, PROJ_W), BF16), _sds((1, Q_LORA), F32), _sds((1, KV_LORA), F32)),
        compiler_params=_cp("arbitrary"))(proj, q_norm.reshape(1, -1), kv_norm.reshape(1, -1), *t32, *t64,
                                          dcqn, dckvn, dkpe, *db, *dc)


def _a_post_fwd(qa, kva, kpe, t32):
    s = qa.shape[0]
    ts = _tile(s, (512, 256, 128))

    def body(qa_ref, kva_ref, kpe_ref, c32, a32, b32, q_ref, k_ref, v_ref):
        for p in range(3):
            lo, hi = 2 * p * LANE, (2 * p + 1) * LANE
            q_ref[:, lo:hi] = qa_ref[:, lo:hi].astype(BF16)
            q_ref[:, hi:hi + LANE] = _rope(qa_ref[:, hi:hi + LANE], c32[...], a32[...], b32[...], QK_ROPE // 2).astype(BF16)
            k_ref[:, lo:hi] = kva_ref[:, p * LANE:(p + 1) * LANE].astype(BF16)
            k_ref[:, hi:hi + LANE] = kpe_ref[...]
            v_ref[:, lo:hi] = kva_ref[:, WIDTH_AB + p * LANE:WIDTH_AB + (p + 1) * LANE].astype(BF16)
            v_ref[:, hi:hi + LANE] = jnp.ones((ts, LANE), BF16)

    return pl.pallas_call(
        body, name="a_post_fwd", grid=(s // ts,),
        in_specs=[_row_spec(ts, W_A2), _row_spec(ts, W_A2), _row_spec(ts, LANE)] + [_row_spec(ts, LANE)] * 3,
        out_specs=(_row_spec(ts, W_A2), _row_spec(ts, W_A2), _row_spec(ts, W_A2)),
        out_shape=(_sds((s, W_A2), BF16), _sds((s, W_A2), BF16), _sds((s, W_A2), BF16)),
        compiler_params=_cp("parallel"))(qa, kva, kpe, *t32)


def _a_post_bwd(dqa2, dka2, dva, t32):
    s = dqa2.shape[0]
    ts = _tile(s, (512, 256, 128))

    def body(dq_ref, dk_ref, dv_ref, c32, a32, b32, dqa_ref, dkva_ref, dkpe_ref):
        acc = None
        for p in range(3):
            lo, hi = 2 * p * LANE, (2 * p + 1) * LANE
            dqa_ref[:, lo:hi] = (dq_ref[:, lo:hi] * SCALE_A).astype(BF16)
            dqa_ref[:, hi:hi + LANE] = _rope_t(dq_ref[:, hi:hi + LANE] * SCALE_A, c32[...], a32[...], b32[...],
                                               QK_ROPE // 2).astype(BF16)
            dkva_ref[:, p * LANE:(p + 1) * LANE] = (dk_ref[:, lo:hi] * SCALE_A).astype(BF16)
            part = dk_ref[:, hi:hi + LANE]
            acc = part if acc is None else acc + part
        dkva_ref[:, WIDTH_AB:2 * WIDTH_AB] = dv_ref[...].astype(BF16)
        dkpe_ref[...] = acc * SCALE_A

    return pl.pallas_call(
        body, name="a_post_bwd", grid=(s // ts,),
        in_specs=[_row_spec(ts, W_A2), _row_spec(ts, W_A2), _row_spec(ts, WIDTH_AB)] + [_row_spec(ts, LANE)] * 3,
        out_specs=(_row_spec(ts, W_A2), _row_spec(ts, W_A2), _row_spec(ts, LANE)),
        out_shape=(_sds((s, W_A2), BF16), _sds((s, W_A2), BF16), _sds((s, LANE), F32)),
        compiler_params=_cp("parallel"))(dqa2, dka2, dva, *t32)


def _pair_masks(width):
    lane = lax.broadcasted_iota(jnp.int32, (1, width), 1)
    m0 = lane < HEAD_DIM
    m1 = (lane >= HEAD_DIM) & (lane < LANE)
    if width == 2 * LANE:
        m0 = m0 | ((lane >= LANE) & (lane < LANE + QK_ROPE))
        m1 = m1 | ((lane >= LANE + QK_ROPE) & (lane < LANE + 2 * QK_ROPE))
    return m0, m1


def _nt(a, b):
    return lax.dot_general(a, b, (((1,), (1,)), ((), ())), preferred_element_type=F32)


def _tn(a, b):
    return lax.dot_general(a, b, (((0,), (0,)), ((), ())), preferred_element_type=F32)


def _stack_heads(x):
    m0, m1 = _pair_masks(LANE)
    zero = jnp.zeros_like(x)
    return jnp.concatenate([jnp.where(m0, x, zero), jnp.where(m1, x, zero)], axis=0)


def _stack_stat(x):
    return jnp.concatenate([x[:, 0:1], x[:, HEAD_DIM:HEAD_DIM + 1]], axis=0)


def _softmax_pair(q, kk, vv, bias2):
    t = q.shape[0]
    s = _nt(_stack_heads(q), kk) + bias2
    m = jnp.max(s, axis=1, keepdims=True)
    p = jnp.exp(s - m)
    l = jnp.sum(p, axis=1, keepdims=True)
    o2 = jnp.dot(p.astype(BF16), vv, preferred_element_type=F32) / l
    lse2 = m + jnp.log(l)
    lo = _lane_lt64((t, LANE))
    return jnp.where(lo, o2[:t], o2[t:]), jnp.where(lo, lse2[:t], lse2[t:])


def _softmax_pair_bwd(q, kk, vv, do, lse, delta, bias2):
    t = q.shape[0]
    q2, do2 = _stack_heads(q), _stack_heads(do)
    p = jnp.exp(_nt(q2, kk) + bias2 - _stack_stat(lse))
    ds = p * (_nt(do2, vv) - _stack_stat(delta))
    dsb = ds.astype(BF16)
    dq2 = jnp.dot(dsb, kk, preferred_element_type=F32)
    lo = _lane_lt64((t, LANE))
    return jnp.where(lo, dq2[:t], dq2[t:]), _tn(dsb, q2), _tn(p.astype(BF16), do2), ds


DENSE_FWD_TQ, DENSE_FWD_TK = 512, 8192
DENSE_BWD_TQ, DENSE_BWD_TK = 2048, 1024
LOG2E = math.log2(math.e)


def _dense_fwd(qa, ka, va1):
    s = qa.shape[0]
    tq, tk = min(DENSE_FWD_TQ, s), min(DENSE_FWD_TK, s)
    nk = s // tk
    c = SCALE_A * LOG2E

    def body(q_ref, k_ref, v_ref, o_ref, lse_ref, m_sc, acc_sc):
        j = pl.program_id(2)

        @pl.when(j == 0)
        def _():
            m_sc[...] = jnp.full(m_sc.shape, NEG_INF, F32)
            acc_sc[...] = jnp.zeros(acc_sc.shape, F32)

        q, kk, vv = q_ref[...], k_ref[...], v_ref[...]
        for hh, msk in enumerate(_pair_masks(2 * LANE)):
            sc = _nt(jnp.where(msk, q, jnp.zeros_like(q)), kk)
            m_prev = m_sc[hh]
            m_new = jnp.maximum(m_prev, jnp.max(sc, axis=1, keepdims=True))
            alpha = jnp.exp2((m_prev - m_new) * c)
            p = jnp.exp2((sc - m_new) * c)
            acc_sc[hh] = alpha * acc_sc[hh] + jnp.dot(p.astype(BF16), vv, preferred_element_type=F32)
            m_sc[hh] = m_new

        @pl.when(j == nk - 1)
        def _():
            lo = _lane_lt64((tq, LANE))
            a0, a1 = acc_sc[0], acc_sc[1]
            l0, l1 = a0[:, LANE:], a1[:, LANE:]
            o_ref[...] = jnp.where(lo, a0[:, :LANE] / l0, a1[:, :LANE] / l1)
            lse_ref[...] = jnp.where(lo, m_sc[0] * SCALE_A + jnp.log(l0), m_sc[1] * SCALE_A + jnp.log(l1))

    o_spec = pl.BlockSpec((tq, LANE), lambda p, i, j: (i, p))
    return pl.pallas_call(
        body, name="dense_fwd", grid=(3, s // tq, nk),
        in_specs=[pl.BlockSpec((tq, 2 * LANE), lambda p, i, j: (i, p)), pl.BlockSpec((tk, 2 * LANE), lambda p, i, j: (j, p)),
                  pl.BlockSpec((tk, 2 * LANE), lambda p, i, j: (j, p))],
        out_specs=(o_spec, o_spec), out_shape=(_sds((s, WIDTH_AB), F32), _sds((s, WIDTH_AB), F32)),
        scratch_shapes=[pltpu.VMEM((2, tq, 1), F32), pltpu.VMEM((2, tq, 2 * LANE), F32)],
        compiler_params=_cp("parallel", "parallel", "arbitrary"))(qa, ka, va1)


def _head_rows(lane_dense):
    s = lane_dense.shape[0]
    return lane_dense[:, ::HEAD_DIM].T.reshape(3, 2, s)


def _dense_bwd(qa, ka, va1, do, lse, delta):
    s = qa.shape[0]
    tq, tk = min(DENSE_BWD_TQ, s), min(DENSE_BWD_TK, s)
    c = SCALE_A * LOG2E
    stats = jnp.concatenate([_head_rows(lse) * LOG2E, _head_rows(delta), jnp.zeros((3, 4, s), F32)], axis=1)
    kat = ka.T

    def body(q_ref, k_ref, kt_ref, v_ref, do_ref, st_ref, dqt_ref, dk_ref, dv_ref):
        j, i = pl.program_id(1), pl.program_id(2)

        @pl.when((j == 0) & (i == 0))
        def _():
            dqt_ref[...] = jnp.zeros(dqt_ref.shape, F32)

        q, kk, kt, vv, do_, st = q_ref[...], k_ref[...], kt_ref[...], v_ref[...], do_ref[...], st_ref[0]
        row = lax.broadcasted_iota(jnp.int32, (2 * LANE, 1), 0)
        rmask = ((row < HEAD_DIM) | ((row >= LANE) & (row < LANE + QK_ROPE)),
                 ((row >= HEAD_DIM) & (row < LANE)) | ((row >= LANE + QK_ROPE) & (row < LANE + 2 * QK_ROPE)))
        vm = _pair_masks(LANE)
        dqt = dk = dv = None
        for hh, msk in enumerate(_pair_masks(2 * LANE)):
            qm = jnp.where(msk, q, jnp.zeros_like(q))
            dom = jnp.where(vm[hh], do_, jnp.zeros_like(do_))
            dpt = _nt(vv, dom)
            pt = jnp.exp2(_nt(kk, qm) * c - st[hh:hh + 1, :])
            dst = pt * (dpt - st[2 + hh:3 + hh, :])
            pb, dsb = pt.astype(BF16), dst.astype(BF16)
            dv_h = jnp.dot(pb, dom, preferred_element_type=F32)
            dk_h = jnp.dot(dsb, qm, preferred_element_type=F32)
            dqt_h = jnp.dot(jnp.where(rmask[hh], kt, jnp.zeros_like(kt)), dsb, preferred_element_type=F32)
            dqt = dqt_h if dqt is None else dqt + dqt_h
            dk = dk_h if dk is None else dk + dk_h
            dv = dv_h if dv is None else dv + dv_h
        cols = pl.ds(pl.multiple_of(i * tq, tq), tq)
        dqt_ref[:, cols] += dqt
        _accum(dk_ref, dk, i == 0)
        _accum(dv_ref, dv, i == 0)

    dqt, dk, dv = pl.pallas_call(
        body, name="dense_bwd", grid=(3, s // tk, s // tq),
        in_specs=[pl.BlockSpec((tq, 2 * LANE), lambda p, j, i: (i, p)), pl.BlockSpec((tk, 2 * LANE), lambda p, j, i: (j, p)),
                  pl.BlockSpec((2 * LANE, tk), lambda p, j, i: (p, j)), pl.BlockSpec((tk, LANE), lambda p, j, i: (j, 2 * p)),
                  pl.BlockSpec((tq, LANE), lambda p, j, i: (i, p)), pl.BlockSpec((1, 8, tq), lambda p, j, i: (p, 0, i))],
        out_specs=(pl.BlockSpec((2 * LANE, s), lambda p, j, i: (p, 0)), pl.BlockSpec((tk, 2 * LANE), lambda p, j, i: (j, p)),
                   pl.BlockSpec((tk, LANE), lambda p, j, i: (j, p))),
        out_shape=(_sds((W_A2, s), F32), _sds((s, W_A2), F32), _sds((s, WIDTH_AB), F32)),
        compiler_params=_cp("parallel", "arbitrary", "arbitrary"))(qa, ka, kat, va1, do, stats)
    return dqt.T, dk, dv


BAND_TILE = 1024
BAND_SUB = 128
QKV_W = 3 * WIDTH_AB


def _band_bias_table():
    row = np.arange(BAND_SUB)[:, None]
    col = np.arange(2 * BAND_SUB)[None, :]
    band = np.abs(row - col + BAND_HALF) <= BAND_HALF
    variants = []
    for idx in range(4):
        ok = band & ((col >= BAND_HALF) | ((idx & 1) == 0)) & ((col < 2 * BAND_SUB - BAND_HALF) | ((idx & 2) == 0))
        one = np.where(ok, 0.0, NEG_INF).astype(np.float32)
        variants.append(np.concatenate([one, one], axis=0))
    return jnp.asarray(np.stack(variants))


def _band_specs(t, n):
    hpt = t // BAND_HALF
    last = n // BAND_HALF - 1
    return [pl.BlockSpec((BAND_HALF, QKV_W), lambda r, i: (jnp.maximum(i * hpt - 1, 0), r)),
            pl.BlockSpec((t, QKV_W), lambda r, i: (i, r)),
            pl.BlockSpec((BAND_HALF, QKV_W), lambda r, i: (jnp.minimum((i + 1) * hpt, last), r)),
            pl.BlockSpec((4, 2 * BAND_SUB, 2 * BAND_SUB), lambda r, i: (0, 0, 0))]


def _band_bias(b_ref, a, nsub, i, nt):
    idx = 0
    if a == 0:
        idx = idx + (i == 0).astype(jnp.int32)
    if a == nsub - 1:
        idx = idx + 2 * (i == nt - 1).astype(jnp.int32)
    return b_ref[idx]


def _band_kv(left, main, right, p):
    kc = slice(WIDTH_AB + p * LANE, WIDTH_AB + (p + 1) * LANE)
    vc = slice(2 * WIDTH_AB + p * LANE, 2 * WIDTH_AB + (p + 1) * LANE)
    return (jnp.concatenate([left[:, kc], main[:, kc], right[:, kc]], axis=0),
            jnp.concatenate([left[:, vc], main[:, vc], right[:, vc]], axis=0))


def _banded_fwd(qkvb, dil, bias):
    s = qkvb.shape[0]
    n = s // dil
    t = min(n, BAND_TILE)
    nsub, nt = t // BAND_SUB, n // t
    view = qkvb.reshape(n, dil * QKV_W)

    def body(left, main, right, b_ref, o_ref, lse_ref):
        i = pl.program_id(1)
        for p in range(3):
            pc = slice(p * LANE, (p + 1) * LANE)
            kk, vv = _band_kv(left, main, right, p)
            for a in range(nsub):
                rows, win = slice(a * BAND_SUB, (a + 1) * BAND_SUB), slice(a * BAND_SUB, (a + 2) * BAND_SUB)
                o, lse = _softmax_pair(main[rows, pc], kk[win], vv[win], _band_bias(b_ref, a, nsub, i, nt))
                o_ref[rows, pc] = o
                lse_ref[rows, pc] = lse

    o_spec = pl.BlockSpec((t, WIDTH_AB), lambda r, i: (i, r))
    o, lse = pl.pallas_call(
        body, name=f"banded_fwd_d{dil}", grid=(dil, nt), in_specs=_band_specs(t, n), out_specs=(o_spec, o_spec),
        out_shape=(_sds((n, dil * WIDTH_AB), F32), _sds((n, dil * WIDTH_AB), F32)),
        compiler_params=_cp("parallel", "parallel"))(view, view, view, bias)
    return o.reshape(s, WIDTH_AB), lse.reshape(s, WIDTH_AB)


def _banded_bwd(qkvb, do, lse, delta, dil, bias):
    s = qkvb.shape[0]
    n = s // dil
    t = min(n, BAND_TILE)
    nsub, nt = t // BAND_SUB, n // t
    view = qkvb.reshape(n, dil * QKV_W)
    side = [a.reshape(n, dil * WIDTH_AB) for a in (do, lse, delta)]

    def body(left, main, right, b_ref, do_ref, lse_ref, dl_ref, dq_ref, dk_ref, dv_ref):
        i = pl.program_id(1)

        @pl.when(i == 0)
        def _():
            dk_ref[...] = jnp.zeros(dk_ref.shape, F32)
            dv_ref[...] = jnp.zeros(dv_ref.shape, F32)

        lrow = pl.multiple_of(jnp.maximum(i * t - BAND_HALF, 0), BAND_HALF)
        rrow = pl.multiple_of(jnp.minimum((i + 1) * t, n - BAND_HALF), BAND_HALF)
        mrow = pl.multiple_of(i * t, BAND_HALF)
        for p in range(3):
            pc = slice(p * LANE, (p + 1) * LANE)
            kk, vv = _band_kv(left, main, right, p)
            parts = []
            for a in range(nsub):
                rows, win = slice(a * BAND_SUB, (a + 1) * BAND_SUB), slice(a * BAND_SUB, (a + 2) * BAND_SUB)
                dq, dk, dv, _ = _softmax_pair_bwd(main[rows, pc], kk[win], vv[win], do_ref[rows, pc], lse_ref[rows, pc],
                                                  dl_ref[rows, pc], _band_bias(b_ref, a, nsub, i, nt))
                dq_ref[rows, pc] = dq
                parts.append((dk, dv))
            for which, ref in ((0, dk_ref), (1, dv_ref)):
                chunks = []
                for c in range(nsub + 1):
                    g = parts[c][which][:BAND_SUB] if c < nsub else None
                    if c >= 1:
                        h = parts[c - 1][which][BAND_SUB:]
                        g = h if g is None else g + h
                    chunks.append(g)
                mid = jnp.concatenate([chunks[0][BAND_HALF:]] + chunks[1:nsub] + [chunks[nsub][:BAND_HALF]], axis=0)
                ref[pl.ds(lrow, BAND_HALF), pc] += chunks[0][:BAND_HALF]
                ref[pl.ds(mrow, t), pc] += mid
                ref[pl.ds(rrow, BAND_HALF), pc] += chunks[nsub][BAND_HALF:]

    q_spec = pl.BlockSpec((t, WIDTH_AB), lambda r, i: (i, r))
    acc_spec = pl.BlockSpec((n, WIDTH_AB), lambda r, i: (0, r))
    shp = _sds((n, dil * WIDTH_AB), F32)
    outs = pl.pallas_call(
        body, name=f"banded_bwd_d{dil}", grid=(dil, nt), in_specs=_band_specs(t, n) + [q_spec, q_spec, q_spec],
        out_specs=(q_spec, acc_spec, acc_spec), out_shape=(shp, shp, shp),
        compiler_params=_cp("parallel", "arbitrary"))(view, view, view, bias, *side)
    return [a.reshape(s, WIDTH_AB) for a in outs]


def _merge_branches(outs, lses):
    s = outs[0].shape[0]
    ts = _tile(s, (512, 256, 128))

    def body(o1, o2, o3, l1, l2, l3, o_ref, lse_ref):
        a, b, c = l1[...], l2[...], l3[...]
        m = jnp.maximum(jnp.maximum(a, b), c)
        ea, eb, ec = jnp.exp(a - m), jnp.exp(b - m), jnp.exp(c - m)
        den = ea + eb + ec
        o_ref[...] = (o1[...] * ea + o2[...] * eb + o3[...] * ec) / den
        lse_ref[...] = m + jnp.log(den)

    sp = _row_spec(ts, WIDTH_AB)
    return pl.pallas_call(
        body, name="merge_branches", grid=(s // ts,), in_specs=[sp] * 6, out_specs=(sp, sp),
        out_shape=(_sds((s, WIDTH_AB), F32), _sds((s, WIDTH_AB), F32)), compiler_params=_cp("parallel"))(*outs, *lses)


def _na_geometry(s):
    rows = s // GRID_W
    assert rows >= 2 * NA_ROWS and rows % NA_ROWS == 0
    return rows, rows // NA_ROWS


def _na_row(n, i, rows):
    rq = n * NA_ROWS + i
    rs = jnp.clip(rq - NA_ROWS // 2, 0, rows - NA_ROWS)
    return pl.multiple_of(rs * GRID_W, GRID_W), rs - rq + NA_ROWS - 1


NA_KEYS = NA_ROWS * GRID_W


def _natten_fwd(qkvc, tfull):
    s = qkvc.shape[0]
    rows, nrb = _na_geometry(s)
    tq = NA_ROWS * GRID_W

    def body(q_ref, k_ref, v_ref, t_ref, o_ref, lse_ref):
        n = pl.program_id(1)
        for i in range(NA_ROWS):
            tok, base = _na_row(n, i, rows)
            kk, vv = k_ref[pl.ds(tok, NA_KEYS), :], v_ref[pl.ds(tok, NA_KEYS), :]
            sl = slice(i * GRID_W, (i + 1) * GRID_W)
            bias2 = jnp.concatenate([t_ref[0, base], t_ref[1, base]], axis=0)
            o, lse = _softmax_pair(q_ref[sl, :], kk, vv, bias2)
            o_ref[sl, :] = o
            lse_ref[sl, :] = lse

    o_spec = pl.BlockSpec((tq, LANE), lambda p, n: (n, p))
    return pl.pallas_call(
        body, name="natten_fwd", grid=(2, nrb),
        in_specs=[pl.BlockSpec((tq, LANE), lambda p, n: (n, p)), pl.BlockSpec((s, LANE), lambda p, n: (0, 2 + p)),
                  pl.BlockSpec((s, LANE), lambda p, n: (0, 4 + p)),
                  pl.BlockSpec((2, NA_ROWS, GRID_W, NA_KEYS), lambda p, n: (p, 0, 0, 0))],
        out_specs=(o_spec, o_spec), out_shape=(_sds((s, WIDTH_C), F32), _sds((s, WIDTH_C), F32)),
        compiler_params=_cp("parallel", "parallel"))(qkvc, qkvc, qkvc, tfull)


def _natten_bwd(qkvc, tfull, do, lse, delta):
    s = qkvc.shape[0]
    rows, nrb = _na_geometry(s)
    tq = NA_ROWS * GRID_W

    def body(q_ref, k_ref, v_ref, t_ref, do_ref, lse_ref, dl_ref, dq_ref, dk_ref, dv_ref, dt_ref):
        n = pl.program_id(1)

        @pl.when(n == 0)
        def _():
            dk_ref[...] = jnp.zeros(dk_ref.shape, F32)
            dv_ref[...] = jnp.zeros(dv_ref.shape, F32)
            dt_ref[...] = jnp.zeros(dt_ref.shape, F32)

        for i in range(NA_ROWS):
            tok, base = _na_row(n, i, rows)
            win = pl.ds(tok, NA_KEYS)
            sl = slice(i * GRID_W, (i + 1) * GRID_W)
            bias2 = jnp.concatenate([t_ref[0, base], t_ref[1, base]], axis=0)
            dq, dk, dv, ds = _softmax_pair_bwd(q_ref[sl, :], k_ref[win, :], v_ref[win, :], do_ref[sl, :], lse_ref[sl, :],
                                               dl_ref[sl, :], bias2)
            dq_ref[sl, :] = dq
            dk_ref[win, :] += dk
            dv_ref[win, :] += dv
            dt_ref[0, base] += ds[:GRID_W]
            dt_ref[1, base] += ds[GRID_W:]

    q_spec = pl.BlockSpec((tq, LANE), lambda p, n: (n, p))
    acc_spec = pl.BlockSpec((s, LANE), lambda p, n: (0, p))
    t_spec = pl.BlockSpec((2, NA_ROWS, GRID_W, NA_KEYS), lambda p, n: (p, 0, 0, 0))
    shp = _sds((s, WIDTH_C), F32)
    return pl.pallas_call(
        body, name="natten_bwd", grid=(2, nrb),
        in_specs=[q_spec, pl.BlockSpec((s, LANE), lambda p, n: (0, 2 + p)), pl.BlockSpec((s, LANE), lambda p, n: (0, 4 + p)),
                  t_spec, q_spec, q_spec, q_spec],
        out_specs=(q_spec, acc_spec, acc_spec, t_spec),
        out_shape=(shp, shp, shp, _sds((HEADS_C, NA_ROWS, GRID_W, NA_KEYS), F32)),
        compiler_params=_cp("parallel", "arbitrary"))(qkvc, qkvc, qkvc, tfull, do, lse, delta)


def _rpb_constants():
    p = np.arange(GRID_W)[:, None]
    qc = np.arange(GRID_W)[None, :]
    dc = np.clip(qc - p, -(NA_COLS - 1), NA_COLS - 1) + NA_COLS - 1
    onehot = (dc.reshape(1, -1) == np.arange(32)[:, None]).astype(np.float32)
    c_start = np.clip(p - NA_COLS // 2, 0, GRID_W - NA_COLS)
    col_ok = ((qc >= c_start) & (qc < c_start + NA_COLS)).reshape(1, -1).astype(np.float32)
    a = np.arange(16)[:, None]
    bj = np.arange(64)[None, :]
    row_sel = ((bj // 8 + bj % 8) == a).astype(np.float32)
    return jnp.asarray(onehot), jnp.asarray(col_ok), jnp.asarray(row_sel)


def _rpb_expand(rpb, onehot, col_ok):
    r2 = jnp.pad(rpb.reshape(HEADS_C * 15, 31), ((0, 4), (0, 1)))

    def body(r_ref, oh_ref, ok_ref, o_ref):
        t = jnp.dot(r_ref[...], oh_ref[...], preferred_element_type=F32, precision=lax.Precision.HIGHEST)
        o_ref[...] = jnp.where(ok_ref[...] > 0.5, t, NEG_INF)

    tm = pl.pallas_call(body, name="rpb_expand", out_shape=_sds((64, GRID_W * GRID_W), F32))(r2, onehot, col_ok)
    tm = tm[:HEADS_C * 15].reshape(HEADS_C, 15, GRID_W, GRID_W)
    tfull = jnp.stack([jnp.concatenate([tm[:, base + j] for j in range(NA_ROWS)], axis=-1) for base in range(NA_ROWS)], axis=1)
    return tfull


def _rpb_grad(dtfull, onehot, row_sel):
    g = dtfull.reshape(HEADS_C, NA_ROWS, GRID_W, NA_ROWS, GRID_W).transpose(0, 1, 3, 2, 4).reshape(HEADS_C, 64, GRID_W * GRID_W)

    def body(g_ref, oh_ref, sel_ref, o_ref):
        for h in range(HEADS_C):
            mid = lax.dot_general(g_ref[h], oh_ref[...], (((1,), (1,)), ((), ())), preferred_element_type=F32,
                                  precision=lax.Precision.HIGHEST)
            o_ref[h] = jnp.dot(sel_ref[...], mid, preferred_element_type=F32, precision=lax.Precision.HIGHEST)

    out = pl.pallas_call(body, name="rpb_grad", out_shape=_sds((HEADS_C, 16, 32), F32))(g, onehot, row_sel)
    return out[:, :15, :31]


def _outnorm_fwd(o_a, o_b, o_c, ga, gb, gc):
    s = o_a.shape[0]
    ts = _tile(s, (512, 256, 128))

    def body(a_ref, b_ref, c_ref, ga_ref, gb_ref, gc_ref, o_ref):
        col = 0
        for ref, g in ((a_ref, ga_ref), (b_ref, gb_ref), (c_ref, gc_ref)):
            x = ref[...]
            o_ref[:, col:col + x.shape[1]] = (x * _rstd(x) * g[...]).astype(BF16)
            col += x.shape[1]

    return pl.pallas_call(
        body, name="outnorm_fwd", grid=(s // ts,),
        in_specs=[_row_spec(ts, WIDTH_AB), _row_spec(ts, WIDTH_AB), _row_spec(ts, WIDTH_C), _fix_spec(WIDTH_AB),
                  _fix_spec(WIDTH_AB), _fix_spec(WIDTH_C)],
        out_specs=_row_spec(ts, D_MODEL), out_shape=_sds((s, D_MODEL), BF16),
        compiler_params=_cp("parallel"))(o_a, o_b, o_c, ga.reshape(1, -1), gb.reshape(1, -1), gc.reshape(1, -1))


def _outnorm_bwd(dmixed, o_a, o_b, o_c, ga, gb, gc):
    s = o_a.shape[0]
    ts = _tile(s, (512, 256, 128))

    def body(dm_ref, a_ref, b_ref, c_ref, ga_ref, gb_ref, gc_ref, *outs):
        first = pl.program_id(0) == 0
        col = 0
        for k, (ref, g) in enumerate(((a_ref, ga_ref), (b_ref, gb_ref), (c_ref, gc_ref))):
            x = ref[...]
            w = x.shape[1]
            dx, dg = _rms_bwd_rows(x, g[...], dm_ref[:, col:col + w])
            col += w
            outs[k][...] = dx.astype(BF16)
            for b, blk in enumerate(_group_sum(dx * x)):
                outs[3 + k][:, b * LANE:(b + 1) * LANE] = blk
            _accum(outs[6 + k], dg, first)

    widths = (WIDTH_AB, WIDTH_AB, WIDTH_C)
    return pl.pallas_call(
        body, name="outnorm_bwd", grid=(s // ts,),
        in_specs=[_row_spec(ts, D_MODEL)] + [_row_spec(ts, w) for w in widths] + [_fix_spec(w) for w in widths],
        out_specs=tuple([_row_spec(ts, w) for w in widths] * 2 + [_fix_spec(w) for w in widths]),
        out_shape=tuple([_sds((s, w), BF16) for w in widths] + [_sds((s, w), F32) for w in widths]
                        + [_sds((1, w), F32) for w in widths]),
        compiler_params=_cp("arbitrary"))(dmixed, o_a, o_b, o_c, ga.reshape(1, -1), gb.reshape(1, -1), gc.reshape(1, -1))


def _adamw(w, g, m, v, *, name):
    r, c = w.shape
    tr = _tile(r, (512, 256, 128, 64, 8))

    def body(w_ref, g_ref, m_ref, v_ref, d_ref, nm_ref, nv_ref):
        gv = g_ref[...]
        nm = ADAM_B1 * m_ref[...] + (1.0 - ADAM_B1) * gv
        nv = ADAM_B2 * v_ref[...] + (1.0 - ADAM_B2) * jnp.square(gv)
        m_hat = nm / (1.0 - ADAM_B1 ** ADAM_STEP)
        v_hat = nv / (1.0 - ADAM_B2 ** ADAM_STEP)
        d_ref[...] = -ADAM_LR * (m_hat / (jnp.sqrt(v_hat) + ADAM_EPS) + ADAM_WD * w_ref[...])
        nm_ref[...] = nm
        nv_ref[...] = nv

    sp = _row_spec(tr, c)
    return pl.pallas_call(
        body, name=name, grid=(r // tr,), in_specs=[sp] * 4, out_specs=(sp, sp, sp),
        out_shape=(_sds((r, c), F32),) * 3, compiler_params=_cp("parallel"))(w, g, m, v)


def _add_n(parts, *, name, out_dtype):
    r, c = parts[0].shape
    tr = max(t for t in range(16, 1025, 16) if r % t == 0)

    def body(*refs):
        acc = refs[0][...].astype(F32)
        for ref in refs[1:-1]:
            acc = acc + ref[...].astype(F32)
        refs[-1][...] = acc.astype(out_dtype)

    sp = _row_spec(tr, c)
    return pl.pallas_call(
        body, name=name, grid=(r // tr,), in_specs=[sp] * len(parts), out_specs=sp, out_shape=_sds((r, c), out_dtype),
        compiler_params=_cp("parallel"))(*parts)


ANY = pl.BlockSpec(memory_space=pl.ANY)
CHIP_FLIPS = ((1, 0), (0, 1), (1, 1))


def _me():
    return lax.axis_index("x"), lax.axis_index("y"), lax.axis_index("c")


def _gather_chips(half):
    def body(src, out, send_sems, recv_sems):
        x, y, c = _me()
        mine = 2 * x + y

        def copy(k, chip, half_idx, to, source=None):
            dst = out.at[chip, half_idx]
            return pltpu.make_async_remote_copy(src_ref=dst if source is None else source, dst_ref=dst,
                                                send_sem=send_sems.at[k], recv_sem=recv_sems.at[k], device_id=to,
                                                device_id_type=MESH_T)

        chips = [(x ^ fx, y ^ fy) for fx, fy in CHIP_FLIPS]
        first = [copy(k, mine, c, (cx, cy, c), source=src) for k, (cx, cy) in enumerate(chips)]
        for cp in first:
            cp.start()
        passed = []
        for k, (cx, cy) in enumerate(chips):
            theirs = 2 * cx + cy
            copy(k, theirs, c, (x, y, c)).wait_recv()
            cp = copy(3 + k, theirs, c, (x, y, 1 - c))
            cp.start()
            passed.append(cp)
        for k, (cx, cy) in enumerate(chips):
            copy(3 + k, 2 * cx + cy, 1 - c, (x, y, c)).wait_recv()
        for cp in first + passed:
            cp.wait_send()

    return pl.pallas_call(
        body, name="gather_chips", in_specs=[ANY], out_specs=ANY, out_shape=_sds((4, 2) + half.shape, half.dtype),
        scratch_shapes=[pltpu.SemaphoreType.DMA((6,)), pltpu.SemaphoreType.DMA((6,))])(half)


def _swap_sibling(block):
    def body(src, out, send_sem, recv_sem):
        x, y, c = _me()
        cp = pltpu.make_async_remote_copy(src_ref=src, dst_ref=out, send_sem=send_sem, recv_sem=recv_sem,
                                          device_id=(x, y, 1 - c), device_id_type=MESH_T)
        cp.start()
        cp.wait()

    return pl.pallas_call(
        body, name="swap_sibling", in_specs=[ANY], out_specs=ANY, out_shape=_sds(block.shape, block.dtype),
        scratch_shapes=[pltpu.SemaphoreType.DMA(()), pltpu.SemaphoreType.DMA(())])(block)


def _scatter_chips(parts):
    def body(src, out, send_sems, recv_sems):
        x, y, c = _me()
        mine = 2 * x + y
        sends = []
        for k, (fx, fy) in enumerate(CHIP_FLIPS):
            theirs = 2 * (x ^ fx) + (y ^ fy)
            cp = pltpu.make_async_remote_copy(src_ref=src.at[theirs], dst_ref=out.at[mine], send_sem=send_sems.at[k],
                                              recv_sem=recv_sems.at[k], device_id=(x ^ fx, y ^ fy, c), device_id_type=MESH_T)
            cp.start()
            sends.append(cp)
        for k, (fx, fy) in enumerate(CHIP_FLIPS):
            theirs = 2 * (x ^ fx) + (y ^ fy)
            pltpu.make_async_remote_copy(src_ref=src.at[theirs], dst_ref=out.at[theirs], send_sem=send_sems.at[k],
                                         recv_sem=recv_sems.at[k], device_id=(x ^ fx, y ^ fy, c),
                                         device_id_type=MESH_T).wait_recv()
        for cp in sends:
            cp.wait_send()

    return pl.pallas_call(
        body, name="scatter_chips", in_specs=[ANY], out_specs=ANY, out_shape=_sds(parts.shape, parts.dtype),
        scratch_shapes=[pltpu.SemaphoreType.DMA((3,)), pltpu.SemaphoreType.DMA((3,))])(parts)


def _all_reduce_small(block):
    r, c = block.shape

    def body(src, out, slots, send_sems, recv_sems):
        x, y, cc = _me()
        mine = 4 * x + 2 * y + cc
        slots[mine] = src[...]
        sends = []
        for k in range(1, 8):
            fx, fy, fc = (k >> 2) & 1, (k >> 1) & 1, k & 1
            cp = pltpu.make_async_remote_copy(src_ref=src, dst_ref=slots.at[mine], send_sem=send_sems.at[k - 1],
                                              recv_sem=recv_sems.at[k - 1], device_id=(x ^ fx, y ^ fy, cc ^ fc),
                                              device_id_type=MESH_T)
            cp.start()
            sends.append(cp)
        for k in range(1, 8):
            fx, fy, fc = (k >> 2) & 1, (k >> 1) & 1, k & 1
            theirs = 4 * (x ^ fx) + 2 * (y ^ fy) + (cc ^ fc)
            pltpu.make_async_remote_copy(src_ref=src, dst_ref=slots.at[theirs], send_sem=send_sems.at[k - 1],
                                         recv_sem=recv_sems.at[k - 1], device_id=(x ^ fx, y ^ fy, cc ^ fc),
                                         device_id_type=MESH_T).wait_recv()
        for cp in sends:
            cp.wait_send()
        acc = slots[0]
        for d in range(1, 8):
            acc = acc + slots[d]
        out[...] = acc

    vm = pl.BlockSpec(memory_space=pltpu.VMEM)
    return pl.pallas_call(
        body, name="all_reduce_small", in_specs=[vm], out_specs=vm, out_shape=_sds((r, c), F32),
        scratch_shapes=[pltpu.VMEM((8, r, c), F32), pltpu.SemaphoreType.DMA((7,)), pltpu.SemaphoreType.DMA((7,))])(block)


BIG = ("w_in", "w_uq", "w_ukv", "w_out", "w_mlp_in", "w_mlp_out")
COL_SHARDED = {"w_in": True, "w_uq": True, "w_ukv": True, "w_out": False, "w_mlp_in": True, "w_mlp_out": False}
SMALL = ("g_mix", "q_norm", "kv_norm", "rpb", "out_norm_a", "out_norm_b", "out_norm_c", "g_mlp", "g_final")
PACK_C = 1024
ROW_ALIGN = 32


def _pack_rows(parts):
    flat = jnp.concatenate([p.reshape(-1, PACK_C) for p in parts], axis=0)
    return jnp.pad(flat, ((0, -flat.shape[0] % ROW_ALIGN), (0, 0)))


def _unpack_rows(flat, shapes):
    out, row = [], 0
    for shp in shapes:
        n = int(np.prod(shp)) // PACK_C
        out.append(flat[row:row + n].reshape(shp))
        row += n
    return out


def _full_from_shards(name, g):
    if COL_SHARDED[name]:
        return g.transpose(1, 2, 0, 3).reshape(g.shape[1], g.shape[2], 4 * g.shape[3])
    return g.transpose(1, 0, 2, 3).reshape(g.shape[1], 4 * g.shape[2], g.shape[3])


def _shards_from_full(name, w):
    l, k, n = w.shape
    if COL_SHARDED[name]:
        return w.reshape(l, k, 4, n // 4).transpose(2, 0, 1, 3)
    return w.reshape(l, 4, k // 4, n).transpose(1, 0, 2, 3)


def _arrange_w_in(w):
    z = jnp.zeros(w.shape[:-1] + (COL_B - COL_KPE - QK_ROPE,), w.dtype)
    return jnp.concatenate([w[..., :COL_KPE + QK_ROPE], z, w[..., COL_KPE + QK_ROPE:]], axis=-1)


def _unarrange_w_in(w):
    return jnp.concatenate([w[..., :COL_KPE + QK_ROPE], w[..., COL_B:]], axis=-1)


def _arrange_w_uq(w):
    per = HEAD_DIM + QK_ROPE
    z = jnp.zeros(w.shape[:-1] + (HEAD_DIM,), w.dtype)
    cols = []
    for p in range(3):
        a, b = 2 * p * per, (2 * p + 1) * per
        cols += [w[..., a:a + HEAD_DIM], w[..., b:b + HEAD_DIM], w[..., a + HEAD_DIM:a + per], w[..., b + HEAD_DIM:b + per], z]
    return jnp.concatenate(cols, axis=-1)


def _unarrange_w_uq(w):
    cols = []
    for h in range(HEADS_A):
        p, e = divmod(h, 2)
        base = 2 * p * LANE
        cols += [w[..., base + e * HEAD_DIM:base + (e + 1) * HEAD_DIM],
                 w[..., base + LANE + e * QK_ROPE:base + LANE + (e + 1) * QK_ROPE]]
    return jnp.concatenate(cols, axis=-1)


def _arrange_w_ukv(w):
    ks = [w[..., h * LANE:h * LANE + HEAD_DIM] for h in range(HEADS_A)]
    vs = [w[..., h * LANE + HEAD_DIM:(h + 1) * LANE] for h in range(HEADS_A)]
    return jnp.concatenate(ks + vs, axis=-1)


def _unarrange_w_ukv(w):
    cols = []
    for h in range(HEADS_A):
        cols += [w[..., h * HEAD_DIM:(h + 1) * HEAD_DIM], w[..., WIDTH_AB + h * HEAD_DIM:WIDTH_AB + (h + 1) * HEAD_DIM]]
    return jnp.concatenate(cols, axis=-1)


def _layer_fwd(x, w, sm, tabs, consts):
    t32, t64 = tabs
    onehot, col_ok, _, band = consts
    h = _rms_fwd(x, sm["g_mix"], name="norm_mix")
    proj = _mm_nn(h, w["w_in"], name="in_proj")
    cqn, ckvn, kpe, qkvb, qkvc = _prep_fwd(proj, sm["q_norm"], sm["kv_norm"], t32, t64)
    qa = _mm_nn(cqn, w["w_uq"], name="q_up")
    kva = _mm_nn(ckvn, w["w_ukv"], name="kv_up")
    qa2, ka2, va1 = _a_post_fwd(qa, kva, kpe, t32)
    o_a, lse_a = _dense_fwd(qa2, ka2, va1)
    branch = [_banded_fwd(qkvb, dil, band) for _, dil in DILATED_PAIRS]
    o_b, lse_b = _merge_branches([b[0] for b in branch], [b[1] for b in branch])
    tfull = _rpb_expand(sm["rpb"], onehot, col_ok)
    o_c, lse_c = _natten_fwd(qkvc, tfull)
    mixed = _outnorm_fwd(o_a, o_b, o_c, sm["out_norm_a"], sm["out_norm_b"], sm["out_norm_c"])
    x_mid = _mm_nn(mixed, w["w_out"], name="out_proj", res=x)
    h2 = _rms_fwd(x_mid, sm["g_mlp"], name="norm_mlp")
    act = _mm_mlp_in(h2, w["w_mlp_in"])
    x_out = _mm_nn(act, w["w_mlp_out"], name="mlp_out", res=x_mid)
    saved = dict(x=x, h=h, proj=proj, cqn=cqn, ckvn=ckvn, qkvb=qkvb, qkvc=qkvc, qa2=qa2, ka2=ka2, va1=va1, o_a=o_a,
                 lse_a=lse_a, o_b=o_b, lse_b=lse_b, o_c=o_c, lse_c=lse_c, tfull=tfull, mixed=mixed, x_mid=x_mid, h2=h2,
                 act=act)
    return x_out, saved


def _layer_bwd(dx, sv, w, sm, tabs, consts):
    t32, t64 = tabs
    onehot, _, row_sel, band = consts
    g = {}
    dxb = dx.astype(BF16)
    du = _mm_nt(dxb, w["w_mlp_out"], name="mlp_out_dx", out_dtype=BF16, relu2_act=sv["act"])
    g["w_mlp_out"] = _mm_tn(sv["act"], dxb, name="mlp_out_dw")
    dh2 = _mm_nt(du, w["w_mlp_in"], name="mlp_in_dx")
    g["w_mlp_in"] = _mm_tn(sv["h2"], du, name="mlp_in_dw")
    dx_mid, g["g_mlp"] = _rms_bwd(sv["x_mid"], sm["g_mlp"], dh2, dx, name="norm_mlp_bwd")
    dmb = dx_mid.astype(BF16)
    dmixed = _mm_nt(dmb, w["w_out"], name="out_proj_dx")
    g["w_out"] = _mm_tn(sv["mixed"], dmb, name="out_proj_dw")
    (do_a, do_b, do_c, dl_a, dl_b, dl_c, g["out_norm_a"], g["out_norm_b"], g["out_norm_c"]) = _outnorm_bwd(
        dmixed, sv["o_a"], sv["o_b"], sv["o_c"], sm["out_norm_a"], sm["out_norm_b"], sm["out_norm_c"])
    dqa2, dka2, dva = _dense_bwd(sv["qa2"], sv["ka2"], sv["va1"], do_a, sv["lse_a"], dl_a)
    db = []
    for _, dil in DILATED_PAIRS:
        db += _banded_bwd(sv["qkvb"], do_b, sv["lse_b"], dl_b, dil, band)
    dq_c, dk_c, dv_c, dtfull = _natten_bwd(sv["qkvc"], sv["tfull"], do_c, sv["lse_c"], dl_c)
    g["rpb"] = _rpb_grad(dtfull, onehot, row_sel)
    dqa, dkva, dkpe = _a_post_bwd(dqa2, dka2, dva, t32)
    dcqn = _mm_nt(dqa, w["w_uq"], name="q_up_dx")
    g["w_uq"] = _unarrange_w_uq(_mm_tn(sv["cqn"], dqa, name="q_up_dw"))
    dckvn = _mm_nt(dkva, w["w_ukv"], name="kv_up_dx")
    g["w_ukv"] = _unarrange_w_ukv(_mm_tn(sv["ckvn"], dkva, name="kv_up_dw"))
    dproj, g["q_norm"], g["kv_norm"] = _prep_bwd(sv["proj"], sm["q_norm"], sm["kv_norm"], t32, t64, dcqn, dckvn, dkpe,
                                                  db, (dq_c, dk_c, dv_c))
    dh = _mm_nt(dproj, w["w_in"], name="in_proj_dx")
    g["w_in"] = _unarrange_w_in(_mm_tn(sv["h"], dproj, name="in_proj_dw"))
    dx_in, g["g_mix"] = _rms_bwd(sv["x"], sm["g_mix"], dh, dx_mid, name="norm_mix_bwd")
    return dx_in, g


def _local_step(x, target, wfull, small):
    s = x.shape[0]
    tabs = (_rope_tables(s, QK_ROPE // 2, 2), _rope_tables(s, HEAD_DIM // 2, 2))
    consts = _rpb_constants() + (_band_bias_table(),)
    saved = []
    for l in range(DEPTH):
        wl = {k: v[l] for k, v in wfull.items()}
        sl = {k: small[k][l] for k in SMALL if k != "g_final"}
        x, sv = _layer_fwd(x, wl, sl, tabs, consts)
        saved.append(sv)
    loss, dx, dg_final = _loss_head(x, small["g_final"], target)
    grads = [None] * DEPTH
    for l in reversed(range(DEPTH)):
        wl = {k: v[l] for k, v in wfull.items()}
        sl = {k: small[k][l] for k in SMALL if k != "g_final"}
        dx, grads[l] = _layer_bwd(dx, saved[l], wl, sl, tabs, consts)
    return loss, dx, grads, dg_final


ARRANGE = {"w_in": _arrange_w_in, "w_uq": _arrange_w_uq, "w_ukv": _arrange_w_ukv}


def kernel(x, g_mix, w_in, q_norm, w_uq, kv_norm, w_ukv, rpb, out_norm_a, out_norm_b, out_norm_c, w_out, g_mlp, w_mlp_in, w_mlp_out, g_final, loss_target, m_g_mix, m_w_in, m_q_norm, m_w_uq, m_kv_norm, m_w_ukv, m_rpb, m_out_norm_a, m_out_norm_b, m_out_norm_c, m_w_out, m_g_mlp, m_w_mlp_in, m_w_mlp_out, m_g_final, v_g_mix, v_w_in, v_q_norm, v_w_uq, v_kv_norm, v_w_ukv, v_rpb, v_out_norm_a, v_out_norm_b, v_out_norm_c, v_w_out, v_g_mlp, v_w_mlp_in, v_w_mlp_out, v_g_final):
    args = dict(locals())
    weights = {k: args[k] for k in BIG + SMALL}
    moms = {k: args["m_" + k] for k in BIG + SMALL}
    vels = {k: args["v_" + k] for k in BIG + SMALL}
    cc = lax.axis_index("c")
    my_chip = 2 * lax.axis_index("x") + lax.axis_index("y")

    shard_shapes = [weights[k].shape for k in BIG]
    packed_w = _pack_rows([weights[k].astype(BF16) for k in BIG])
    rows = packed_w.shape[0]
    my_half = lax.dynamic_index_in_dim(packed_w.reshape(2, rows // 2, PACK_C), cc, axis=0, keepdims=False)
    gathered = _gather_chips(my_half).reshape(4, rows, PACK_C)
    per_chip = [_unpack_rows(jnp.where(my_chip == j, packed_w, gathered[j]), shard_shapes) for j in range(4)]
    wfull = {}
    for idx, k in enumerate(BIG):
        full = _full_from_shards(k, jnp.stack([per_chip[j][idx] for j in range(4)]))
        wfull[k] = ARRANGE[k](full) if k in ARRANGE else full

    small = {k: weights[k] for k in SMALL}
    loss, dx, grads, dg_final = _local_step(x[0], loss_target[0], wfull, small)

    small_local = {k: jnp.stack([grads[l][k].reshape(weights[k].shape[1:]) for l in range(DEPTH)])
                   for k in SMALL if k != "g_final"}
    small_local["g_final"] = dg_final.reshape(-1)
    small_shapes = [weights[k].shape for k in SMALL]
    n_small = sum(int(np.prod(s)) for s in small_shapes)
    flat = jnp.concatenate([small_local[k].reshape(-1) for k in SMALL] + [loss[0, :1]])
    rows_small = -(-(n_small + 1) // PACK_C)
    rows_small += -rows_small % 8
    flat = jnp.pad(flat, (0, rows_small * PACK_C - n_small - 1)).reshape(rows_small, PACK_C)
    red = _all_reduce_small(flat).reshape(-1)
    loss_out = red[n_small]
    small_grads, off = {}, 0
    for k, shp in zip(SMALL, small_shapes):
        n = int(np.prod(shp))
        small_grads[k] = red[off:off + n].reshape(shp)
        off += n

    by_shard = {k: _shards_from_full(k, jnp.stack([grads[l][k] for l in range(DEPTH)])) for k in BIG}
    packed = jnp.stack([_pack_rows([by_shard[k][j] for k in BIG]) for j in range(4)])
    halves = packed.reshape(4, 2, rows // 2, PACK_C)
    mine = lax.dynamic_index_in_dim(halves, cc, axis=1, keepdims=False)
    other = lax.dynamic_index_in_dim(halves, 1 - cc, axis=1, keepdims=False)
    from_sibling = _swap_sibling(other)
    pair = _add_n([mine.reshape(-1, PACK_C), from_sibling.reshape(-1, PACK_C)], name="pair_sum",
                  out_dtype=BF16).reshape(mine.shape)
    by_chip = _scatter_chips(pair)
    reduced = _add_n([jnp.where(my_chip == j, pair[j], by_chip[j]) for j in range(4)], name="chip_sum", out_dtype=F32)
    theirs = _swap_sibling(reduced)
    joined = jnp.where(cc == 0, jnp.concatenate([reduced, theirs]), jnp.concatenate([theirs, reduced]))
    big_grads = dict(zip(BIG, _unpack_rows(joined, shard_shapes)))

    out_g, out_d, out_m, out_v = {}, {}, {}, {}
    for k in BIG:
        shp = weights[k].shape
        two_d = (shp[0] * shp[1], shp[2])
        d, nm, nv = _adamw(weights[k].reshape(two_d), big_grads[k].reshape(two_d), moms[k].reshape(two_d),
                           vels[k].reshape(two_d), name="adamw_" + k)
        out_g[k], out_d[k], out_m[k], out_v[k] = big_grads[k], d.reshape(shp), nm.reshape(shp), nv.reshape(shp)

    def pack_small(tree):
        f = jnp.concatenate([tree[k].reshape(-1) for k in SMALL])
        return jnp.pad(f, (0, rows_small * PACK_C - n_small)).reshape(rows_small, PACK_C)

    d, nm, nv = _adamw(pack_small(small), pack_small(small_grads), pack_small(moms), pack_small(vels), name="adamw_small")
    for tree, flat_out in ((out_d, d), (out_m, nm), (out_v, nv)):
        off = 0
        fo = flat_out.reshape(-1)
        for k, shp in zip(SMALL, small_shapes):
            n = int(np.prod(shp))
            tree[k] = fo[off:off + n].reshape(shp)
            off += n
    out_g.update(small_grads)

    order = ("g_mix", "w_in", "q_norm", "w_uq", "kv_norm", "w_ukv", "rpb", "out_norm_a", "out_norm_b", "out_norm_c", "w_out",
             "g_mlp", "w_mlp_in", "w_mlp_out", "g_final")
    return (loss_out, dx.reshape(x.shape), *[out_g[k] for k in order], *[out_d[k] for k in order],
            *[out_m[k] for k in order], *[out_v[k] for k in order])
```

```python
import math

import numpy as np
import jax
import jax.numpy as jnp
from jax import lax
from jax.experimental import pallas as pl
from jax.experimental.pallas import tpu as pltpu

F32 = jnp.float32
BF16 = jnp.bfloat16

D_MODEL = 1024
HEAD_DIM = 64
Q_LORA = 256
KV_LORA = 128
QK_ROPE = 32
HEADS_A = 6
HEADS_B = 6
HEADS_C = 4
DILATED_PAIRS = ((128, 1), (512, 4), (2048, 16))
BAND_HALF = 64
GRID_W = 64
NA_ROWS = 8
NA_COLS = 16
D_FF = 4096
ROPE_THETA = 10000.0
NORM_EPS = 1e-6
NEG_INF = -1e30
DEPTH = 4

LANE = 128
PROJ_W = 2432
COL_CKV = 256
COL_KPE = 384
COL_B = 512
COL_C = 1664
W_A2 = 768
WIDTH_AB = 384
WIDTH_C = 256
SCALE_A = (HEAD_DIM + QK_ROPE) ** -0.5
SCALE_BC = HEAD_DIM ** -0.5

ADAM_LR = 0.001
ADAM_B1 = 0.9
ADAM_B2 = 0.999
ADAM_EPS = 1e-08
ADAM_WD = 0.01
ADAM_STEP = 10

VMEM_LIMIT = 56 * 1024 * 1024
MESH_T = pl.DeviceIdType.MESH


def _cp(*sem):
    return pltpu.CompilerParams(dimension_semantics=sem or None, vmem_limit_bytes=VMEM_LIMIT)


def _tile(n, cands):
    for c in cands:
        if n % c == 0:
            return c
    return n


def _sds(shape, dtype):
    return jax.ShapeDtypeStruct(shape, dtype)


def _mm_nn(a, b, *, name, out_dtype=F32, res=None):
    m, k = a.shape
    n = b.shape[1]
    tm = _tile(m, (512, 256, 128))
    tn = _tile(n, (1024, 768, 512)) if n % LANE == 0 and n != PROJ_W else n

    def body(*refs):
        a_ref, b_ref = refs[0], refs[1]
        o_ref = refs[-1]
        acc = jnp.dot(a_ref[...], b_ref[...], preferred_element_type=F32)
        if res is not None:
            acc = refs[2][...] + acc
        o_ref[...] = acc.astype(o_ref.dtype)

    in_specs = [pl.BlockSpec((tm, k), lambda j, i: (i, 0)), pl.BlockSpec((k, tn), lambda j, i: (0, j))]
    args = [a, b]
    if res is not None:
        in_specs.append(pl.BlockSpec((tm, tn), lambda j, i: (i, j)))
        args.append(res)
    return pl.pallas_call(
        body, name=name, grid=(n // tn, m // tm), in_specs=in_specs,
        out_specs=pl.BlockSpec((tm, tn), lambda j, i: (i, j)), out_shape=_sds((m, n), out_dtype),
        compiler_params=_cp("parallel", "parallel"))(*args)


def _norm_mm(x, g, w, *, name, relu2):
    m, k = x.shape
    n = w.shape[1]
    tm = _tile(m, (512, 256, 128) if n <= PROJ_W else (256, 128))

    def body(x_ref, g_ref, w_ref, h_ref, o_ref):
        xv = x_ref[...]
        h = (xv * _rstd(xv) * g_ref[...]).astype(BF16)
        h_ref[...] = h
        acc = jnp.dot(h, w_ref[...], preferred_element_type=F32)
        if relu2:
            acc = jnp.square(jnp.maximum(acc, 0.0))
        o_ref[...] = acc.astype(o_ref.dtype)

    return pl.pallas_call(
        body, name=name, grid=(m // tm,),
        in_specs=[pl.BlockSpec((tm, k), lambda i: (i, 0)), pl.BlockSpec((1, k), lambda i: (0, 0)),
                  pl.BlockSpec((k, n), lambda i: (0, 0))],
        out_specs=(pl.BlockSpec((tm, k), lambda i: (i, 0)), pl.BlockSpec((tm, n), lambda i: (i, 0))),
        out_shape=(_sds((m, k), BF16), _sds((m, n), BF16 if relu2 else F32)),
        compiler_params=_cp("parallel"))(x, g.reshape(1, k), w)


def _mm_nt(a, b, *, name, out_dtype=F32, relu2_act=None):
    m, c = a.shape
    n = b.shape[0]
    tm = _tile(m, (512, 256, 128))
    tn = _tile(n, (1024, 512, 256, 128))

    def body(*refs):
        a_ref, b_ref = refs[0], refs[1]
        o_ref = refs[-1]
        acc = lax.dot_general(a_ref[...], b_ref[...], (((1,), (1,)), ((), ())), preferred_element_type=F32)
        if relu2_act is not None:
            acc = acc * (2.0 * jnp.sqrt(refs[2][...].astype(F32)))
        o_ref[...] = acc.astype(o_ref.dtype)

    in_specs = [pl.BlockSpec((tm, c), lambda j, i: (i, 0)), pl.BlockSpec((tn, c), lambda j, i: (j, 0))]
    args = [a, b]
    if relu2_act is not None:
        in_specs.append(pl.BlockSpec((tm, tn), lambda j, i: (i, j)))
        args.append(relu2_act)
    return pl.pallas_call(
        body, name=name, grid=(n // tn, m // tm), in_specs=in_specs,
        out_specs=pl.BlockSpec((tm, tn), lambda j, i: (i, j)), out_shape=_sds((m, n), out_dtype),
        compiler_params=_cp("parallel", "parallel"))(*args)


def _mm_tn(a, b, *, name):
    m, ka = a.shape
    nb = b.shape[1]
    tka = _tile(ka, (512, 256, 128))
    tnb = _tile(nb, (1024, 768, 512)) if nb != PROJ_W else nb
    tc = _tile(m, (1024, 512, 256, 128))

    def body(a_ref, b_ref, o_ref):
        part = lax.dot_general(a_ref[...], b_ref[...], (((0,), (0,)), ((), ())), preferred_element_type=F32)

        @pl.when(pl.program_id(2) == 0)
        def _():
            o_ref[...] = part

        @pl.when(pl.program_id(2) != 0)
        def _():
            o_ref[...] += part

    return pl.pallas_call(
        body, name=name, grid=(ka // tka, nb // tnb, m // tc),
        in_specs=[pl.BlockSpec((tc, tka), lambda i, j, c: (c, i)), pl.BlockSpec((tc, tnb), lambda i, j, c: (c, j))],
        out_specs=pl.BlockSpec((tka, tnb), lambda i, j, c: (i, j)), out_shape=_sds((ka, nb), F32),
        compiler_params=_cp("parallel", "parallel", "arbitrary"))(a, b)


def _rstd(x):
    return lax.rsqrt(jnp.mean(x * x, axis=-1, keepdims=True) + NORM_EPS)


def _rms_bwd_rows(x, g, dy):
    r = _rstd(x)
    gy = dy * g
    c = jnp.sum(x * gy, axis=-1, keepdims=True) * (r * r * r) * (1.0 / x.shape[-1])
    return r * gy - x * c, jnp.sum(dy * x * r, axis=0, keepdims=True)


def _accum(ref, part, first):
    @pl.when(first)
    def _():
        ref[...] = part

    @pl.when(jnp.logical_not(first))
    def _():
        ref[...] += part


def _rope(x, c, s1, s2, sh):
    return x * c + pltpu.roll(x, LANE - sh, 1) * s1 + pltpu.roll(x, sh, 1) * s2


def _rope_t(g, c, s1, s2, sh):
    return g * c + pltpu.roll(g * s1, sh, 1) + pltpu.roll(g * s2, LANE - sh, 1)


def _rope_tables(s, half, reps):
    pos = jnp.arange(s, dtype=F32)
    inv_freq = ROPE_THETA ** (-jnp.arange(half, dtype=F32) / half)
    ang = pos[:, None] * inv_freq[None, :]
    cos, sin = jnp.cos(ang), jnp.sin(ang)
    zero = jnp.zeros_like(cos)
    pad = jnp.zeros((s, LANE - 2 * half * reps), F32)
    c = jnp.concatenate([cos, cos] * reps + [pad], axis=1)
    s1 = jnp.concatenate([-sin, zero] * reps + [pad], axis=1)
    s2 = jnp.concatenate([zero, sin] * reps + [pad], axis=1)
    return c, s1, s2


def _lane_lt64(shape):
    return lax.broadcasted_iota(jnp.int32, shape, len(shape) - 1) % LANE < HEAD_DIM


def _group_sum(x):
    outs = []
    for b in range(x.shape[1] // LANE):
        blk = x[:, b * LANE:(b + 1) * LANE]
        lo = _lane_lt64(blk.shape)
        s0 = jnp.sum(jnp.where(lo, blk, 0.0), axis=1, keepdims=True)
        s1 = jnp.sum(jnp.where(lo, 0.0, blk), axis=1, keepdims=True)
        outs.append(jnp.where(lo, s0, s1))
    return outs


def _row_spec(ts, w):
    return pl.BlockSpec((ts, w), lambda i: (i, 0))


def _fix_spec(w):
    return pl.BlockSpec((1, w), lambda i: (0, 0))


def _rms_bwd(x, g, dy, res, *, name):
    s, d = x.shape
    ts = _tile(s, (512, 256, 128))

    def body(x_ref, g_ref, dy_ref, res_ref, dx_ref, dxb_ref, dg_ref):
        dx, dg = _rms_bwd_rows(x_ref[...], g_ref[...], dy_ref[...])
        dx = res_ref[...] + dx
        dx_ref[...] = dx
        dxb_ref[...] = dx.astype(BF16)
        _accum(dg_ref, dg, pl.program_id(0) == 0)

    return pl.pallas_call(
        body, name=name, grid=(s // ts,),
        in_specs=[_row_spec(ts, d), _fix_spec(d), _row_spec(ts, d), _row_spec(ts, d)],
        out_specs=(_row_spec(ts, d), _row_spec(ts, d), _fix_spec(d)),
        out_shape=(_sds((s, d), F32), _sds((s, d), BF16), _sds((1, d), F32)),
        compiler_params=_cp("arbitrary"))(x, g.reshape(1, d), dy, res)


def _loss_head(x, g, target):
    s, d = x.shape
    ts = _tile(s, (512, 256, 128))

    def body(x_ref, g_ref, t_ref, loss_ref, dx_ref, dxb_ref, dg_ref):
        xv, gv = x_ref[...], g_ref[...]
        err = xv * _rstd(xv) * gv - t_ref[...]
        part = 0.5 * jnp.sum(jnp.sum(err * err, axis=-1, keepdims=True) * (1.0 / d), axis=0, keepdims=True)
        dx, dg = _rms_bwd_rows(xv, gv, err * (1.0 / d))
        dx_ref[...] = dx
        dxb_ref[...] = dx.astype(BF16)
        first = pl.program_id(0) == 0
        _accum(dg_ref, dg, first)
        _accum(loss_ref, jnp.broadcast_to(part, (1, LANE)), first)

    return pl.pallas_call(
        body, name="loss_head", grid=(s // ts,), in_specs=[_row_spec(ts, d), _fix_spec(d), _row_spec(ts, d)],
        out_specs=(_fix_spec(LANE), _row_spec(ts, d), _row_spec(ts, d), _fix_spec(d)),
        out_shape=(_sds((1, LANE), F32), _sds((s, d), F32), _sds((s, d), BF16), _sds((1, d), F32)),
        compiler_params=_cp("arbitrary"))(x, g.reshape(1, d), target)


def _prep_fwd(proj, q_norm, kv_norm, t32, t64):
    s = proj.shape[0]
    ts = _tile(s, (256, 128))

    def body(p_ref, qn_ref, kn_ref, c32, a32, b32, c64, a64, b64, cqn_ref, ckvn_ref, kpe_ref, qkvb_ref, qkvc_ref):
        cq = p_ref[:, 0:Q_LORA]
        cqn_ref[...] = (cq * _rstd(cq) * qn_ref[...]).astype(BF16)
        ckv = p_ref[:, COL_CKV:COL_KPE]
        ckvn_ref[...] = (ckv * _rstd(ckv) * kn_ref[...]).astype(BF16)
        kp = p_ref[:, COL_KPE:COL_B]
        kp2 = kp + pltpu.roll(kp, QK_ROPE, 1)
        kpe_ref[...] = _rope(kp2, c32[...], a32[...], b32[...], QK_ROPE // 2).astype(BF16)
        for b in range(6):
            blk = _rope(p_ref[:, COL_B + b * LANE:COL_B + (b + 1) * LANE], c64[...], a64[...], b64[...], HEAD_DIM // 2)
            if b < 3:
                blk = blk * SCALE_BC
            qkvb_ref[:, b * LANE:(b + 1) * LANE] = blk.astype(BF16)
        qkvb_ref[:, 2 * WIDTH_AB:3 * WIDTH_AB] = p_ref[:, COL_B + 2 * WIDTH_AB:COL_C].astype(BF16)
        qkvc_ref[:, 0:WIDTH_C] = (p_ref[:, COL_C:COL_C + WIDTH_C] * SCALE_BC).astype(BF16)
        qkvc_ref[:, WIDTH_C:3 * WIDTH_C] = p_ref[:, COL_C + WIDTH_C:PROJ_W].astype(BF16)

    tab = [_row_spec(ts, LANE)] * 6
    return pl.pallas_call(
        body, name="prep_fwd", grid=(s // ts,),
        in_specs=[_row_spec(ts, PROJ_W), _fix_spec(Q_LORA), _fix_spec(KV_LORA)] + tab,
        out_specs=(_row_spec(ts, Q_LORA), _row_spec(ts, KV_LORA), _row_spec(ts, LANE), _row_spec(ts, 3 * WIDTH_AB),
                   _row_spec(ts, 3 * WIDTH_C)),
        out_shape=(_sds((s, Q_LORA), BF16), _sds((s, KV_LORA), BF16), _sds((s, LANE), BF16),
                   _sds((s, 3 * WIDTH_AB), BF16), _sds((s, 3 * WIDTH_C), BF16)),
        compiler_params=_cp("parallel"))(proj, q_norm.reshape(1, -1), kv_norm.reshape(1, -1), *t32, *t64)


def _prep_bwd(proj, q_norm, kv_norm, t32, t64, dcqn, dckvn, dkpe, db, dc):
    s = proj.shape[0]
    ts = _tile(s, (256, 128))

    def body(p_ref, qn_ref, kn_ref, c32, a32, b32, c64, a64, b64, dcqn_ref, dckvn_ref, dkpe_ref, *rest):
        db_refs, dc_refs = rest[0:9], rest[9:12]
        dp_ref, dqn_ref, dkn_ref = rest[12:15]
        first = pl.program_id(0) == 0
        dx, dg = _rms_bwd_rows(p_ref[:, 0:Q_LORA], qn_ref[...], dcqn_ref[...])
        dp_ref[:, 0:Q_LORA] = dx.astype(BF16)
        _accum(dqn_ref, dg, first)
        dx, dg = _rms_bwd_rows(p_ref[:, COL_CKV:COL_KPE], kn_ref[...], dckvn_ref[...])
        dp_ref[:, COL_CKV:COL_KPE] = dx.astype(BF16)
        _accum(dkn_ref, dg, first)
        g = _rope_t(dkpe_ref[...], c32[...], a32[...], b32[...], QK_ROPE // 2)
        g = g + pltpu.roll(g, LANE - QK_ROPE, 1)
        lane = lax.broadcasted_iota(jnp.int32, g.shape, 1)
        dp_ref[:, COL_KPE:COL_B] = jnp.where(lane < QK_ROPE, g, 0.0).astype(BF16)
        for which in range(3):
            for b in range(3):
                sl = slice(b * LANE, (b + 1) * LANE)
                g = db_refs[which][:, sl] + db_refs[3 + which][:, sl] + db_refs[6 + which][:, sl]
                if which < 2:
                    g = _rope_t(g, c64[...], a64[...], b64[...], HEAD_DIM // 2)
                if which == 0:
                    g = g * SCALE_BC
                col = COL_B + which * WIDTH_AB + b * LANE
                dp_ref[:, col:col + LANE] = g.astype(BF16)
        dp_ref[:, COL_C:COL_C + WIDTH_C] = (dc_refs[0][...] * SCALE_BC).astype(BF16)
        dp_ref[:, COL_C + WIDTH_C:COL_C + 2 * WIDTH_C] = dc_refs[1][...].astype(BF16)
        dp_ref[:, COL_C + 2 * WIDTH_C:PROJ_W] = dc_refs[2][...].astype(BF16)

    tab = [_row_spec(ts, LANE)] * 6
    in_specs = ([_row_spec(ts, PROJ_W), _fix_spec(Q_LORA), _fix_spec(KV_LORA)] + tab
                + [_row_spec(ts, Q_LORA), _row_spec(ts, KV_LORA), _row_spec(ts, LANE)]
                + [_row_spec(ts, WIDTH_AB)] * 9 + [_row_spec(ts, WIDTH_C)] * 3)
    return pl.pallas_call(
        body, name="prep_bwd", grid=(s // ts,), in_specs=in_specs,
        out_specs=(_row_spec(ts, PROJ_W), _fix_spec(Q_LORA), _fix_spec(KV_LORA)),
        out_shape=(_sds((s, PROJ_W), BF16), _sds((1, Q_LORA), F32), _sds((1, KV_LORA), F32)),
        compiler_params=_cp("arbitrary"))(proj, q_norm.reshape(1, -1), kv_norm.reshape(1, -1), *t32, *t64,
                                          dcqn, dckvn, dkpe, *db, *dc)


def _a_post_fwd(qa, kva, kpe, t32):
    s = qa.shape[0]
    ts = _tile(s, (512, 256, 128))

    def body(qa_ref, kva_ref, kpe_ref, c32, a32, b32, q_ref, k_ref, kt_ref, v_ref):
        kpe_t = kpe_ref[...].astype(F32).T.astype(BF16)
        for p in range(3):
            lo, hi = 2 * p * LANE, (2 * p + 1) * LANE
            q_ref[:, lo:hi] = qa_ref[:, lo:hi].astype(BF16)
            q_ref[:, hi:hi + LANE] = _rope(qa_ref[:, hi:hi + LANE], c32[...], a32[...], b32[...], QK_ROPE // 2).astype(BF16)
            knope = kva_ref[:, p * LANE:(p + 1) * LANE]
            k_ref[:, lo:hi] = knope.astype(BF16)
            k_ref[:, hi:hi + LANE] = kpe_ref[...]
            kt_ref[lo:hi, :] = knope.T.astype(BF16)
            kt_ref[hi:hi + LANE, :] = kpe_t
            v_ref[:, lo:hi] = kva_ref[:, WIDTH_AB + p * LANE:WIDTH_AB + (p + 1) * LANE].astype(BF16)
            v_ref[:, hi:hi + LANE] = jnp.ones((ts, LANE), BF16)

    return pl.pallas_call(
        body, name="a_post_fwd", grid=(s // ts,),
        in_specs=[_row_spec(ts, W_A2), _row_spec(ts, W_A2), _row_spec(ts, LANE)] + [_row_spec(ts, LANE)] * 3,
        out_specs=(_row_spec(ts, W_A2), _row_spec(ts, W_A2), pl.BlockSpec((W_A2, ts), lambda i: (0, i)), _row_spec(ts, W_A2)),
        out_shape=(_sds((s, W_A2), BF16), _sds((s, W_A2), BF16), _sds((W_A2, s), BF16), _sds((s, W_A2), BF16)),
        compiler_params=_cp("parallel"))(qa, kva, kpe, *t32)


def _a_post_bwd(dqa2_t, dka2, dva, t32):
    s = dka2.shape[0]
    ts = _tile(s, (512, 256, 128))

    def body(dqt_ref, dk_ref, dv_ref, c32, a32, b32, dqa_ref, dkva_ref, dkpe_ref):
        acc = None
        for p in range(3):
            lo, hi = 2 * p * LANE, (2 * p + 1) * LANE
            dqa_ref[:, lo:hi] = (dqt_ref[lo:hi, :].T * SCALE_A).astype(BF16)
            dqa_ref[:, hi:hi + LANE] = _rope_t(dqt_ref[hi:hi + LANE, :].T * SCALE_A, c32[...], a32[...], b32[...],
                                               QK_ROPE // 2).astype(BF16)
            dkva_ref[:, p * LANE:(p + 1) * LANE] = (dk_ref[:, lo:hi] * SCALE_A).astype(BF16)
            part = dk_ref[:, hi:hi + LANE]
            acc = part if acc is None else acc + part
        dkva_ref[:, WIDTH_AB:2 * WIDTH_AB] = dv_ref[...].astype(BF16)
        dkpe_ref[...] = acc * SCALE_A

    return pl.pallas_call(
        body, name="a_post_bwd", grid=(s // ts,),
        in_specs=[pl.BlockSpec((W_A2, ts), lambda i: (0, i)), _row_spec(ts, W_A2), _row_spec(ts, WIDTH_AB)]
        + [_row_spec(ts, LANE)] * 3,
        out_specs=(_row_spec(ts, W_A2), _row_spec(ts, W_A2), _row_spec(ts, LANE)),
        out_shape=(_sds((s, W_A2), BF16), _sds((s, W_A2), BF16), _sds((s, LANE), F32)),
        compiler_params=_cp("parallel"))(dqa2_t, dka2, dva, *t32)


def _pair_masks(width):
    lane = lax.broadcasted_iota(jnp.int32, (1, width), 1)
    m0 = lane < HEAD_DIM
    m1 = (lane >= HEAD_DIM) & (lane < LANE)
    if width == 2 * LANE:
        m0 = m0 | ((lane >= LANE) & (lane < LANE + QK_ROPE))
        m1 = m1 | ((lane >= LANE + QK_ROPE) & (lane < LANE + 2 * QK_ROPE))
    return m0, m1


def _nt(a, b):
    return lax.dot_general(a, b, (((1,), (1,)), ((), ())), preferred_element_type=F32)


def _tn(a, b):
    return lax.dot_general(a, b, (((0,), (0,)), ((), ())), preferred_element_type=F32)


def _stack_heads(x):
    m0, m1 = _pair_masks(LANE)
    zero = jnp.zeros_like(x)
    return jnp.concatenate([jnp.where(m0, x, zero), jnp.where(m1, x, zero)], axis=0)


def _stack_stat(x):
    return jnp.concatenate([x[:, 0:1], x[:, HEAD_DIM:HEAD_DIM + 1]], axis=0)


def _softmax_pair(q, kk, vv, bias2):
    t = q.shape[0]
    s = _nt(_stack_heads(q), kk) + bias2
    m = jnp.max(s, axis=1, keepdims=True)
    p = jnp.exp(s - m)
    l = jnp.sum(p, axis=1, keepdims=True)
    o2 = jnp.dot(p.astype(BF16), vv, preferred_element_type=F32) / l
    lse2 = m + jnp.log(l)
    lo = _lane_lt64((t, LANE))
    return jnp.where(lo, o2[:t], o2[t:]), jnp.where(lo, lse2[:t], lse2[t:])


def _softmax_pair_bwd(q, kk, vv, do, lse, delta, bias2):
    t = q.shape[0]
    q2, do2 = _stack_heads(q), _stack_heads(do)
    p = jnp.exp(_nt(q2, kk) + bias2 - _stack_stat(lse))
    ds = p * (_nt(do2, vv) - _stack_stat(delta))
    dsb = ds.astype(BF16)
    dq2 = jnp.dot(dsb, kk, preferred_element_type=F32)
    lo = _lane_lt64((t, LANE))
    return jnp.where(lo, dq2[:t], dq2[t:]), _tn(dsb, q2), _tn(p.astype(BF16), do2), ds


DENSE_FWD_TQ, DENSE_FWD_TK = 512, 8192
DENSE_BWD_TQ, DENSE_BWD_TK = 2048, 1024
LOG2E = math.log2(math.e)


def _dense_fwd(qa, ka, va1):
    s = qa.shape[0]
    tq, tk = min(DENSE_FWD_TQ, s), min(DENSE_FWD_TK, s)
    nk = s // tk
    c = SCALE_A * LOG2E

    def body(q_ref, k_ref, v_ref, o_ref, lse_ref, m_sc, acc_sc):
        j = pl.program_id(2)

        @pl.when(j == 0)
        def _():
            m_sc[...] = jnp.full(m_sc.shape, NEG_INF, F32)
            acc_sc[...] = jnp.zeros(acc_sc.shape, F32)

        q, kk, vv = q_ref[...], k_ref[...], v_ref[...]
        for hh, msk in enumerate(_pair_masks(2 * LANE)):
            sc = _nt(jnp.where(msk, q, jnp.zeros_like(q)), kk)
            m_prev = m_sc[hh]
            m_new = jnp.maximum(m_prev, jnp.max(sc, axis=1, keepdims=True))
            alpha = jnp.exp2((m_prev - m_new) * c)
            p = jnp.exp2((sc - m_new) * c)
            acc_sc[hh] = alpha * acc_sc[hh] + jnp.dot(p.astype(BF16), vv, preferred_element_type=F32)
            m_sc[hh] = m_new

        @pl.when(j == nk - 1)
        def _():
            lo = _lane_lt64((tq, LANE))
            a0, a1 = acc_sc[0], acc_sc[1]
            l0, l1 = a0[:, LANE:], a1[:, LANE:]
            o_ref[...] = jnp.where(lo, a0[:, :LANE] / l0, a1[:, :LANE] / l1)
            lse_ref[0] = _stat_rows(jnp.where(lo, m_sc[0] * SCALE_A + jnp.log(l0), m_sc[1] * SCALE_A + jnp.log(l1)))

    return pl.pallas_call(
        body, name="dense_fwd", grid=(3, s // tq, nk),
        in_specs=[pl.BlockSpec((tq, 2 * LANE), lambda p, i, j: (i, p)), pl.BlockSpec((tk, 2 * LANE), lambda p, i, j: (j, p)),
                  pl.BlockSpec((tk, 2 * LANE), lambda p, i, j: (j, p))],
        out_specs=(pl.BlockSpec((tq, LANE), lambda p, i, j: (i, p)), pl.BlockSpec((1, 8, tq), lambda p, i, j: (p, 0, i))),
        out_shape=(_sds((s, WIDTH_AB), F32), _sds((3, 8, s), F32)),
        scratch_shapes=[pltpu.VMEM((2, tq, 1), F32), pltpu.VMEM((2, tq, 2 * LANE), F32)],
        compiler_params=_cp("parallel", "parallel", "arbitrary"))(qa, ka, va1)


def _stat_rows(lane_dense):
    tr = lane_dense.T
    return jnp.concatenate([tr[0:1, :], tr[HEAD_DIM:HEAD_DIM + 1, :], jnp.zeros((6, tr.shape[1]), F32)], axis=0)


def _dense_bwd(qa, ka, kat, va1, do, lse_rows, delta_rows):
    s = qa.shape[0]
    tq, tk = min(DENSE_BWD_TQ, s), min(DENSE_BWD_TK, s)
    c = SCALE_A * LOG2E

    def body(q_ref, k_ref, kt_ref, v_ref, do_ref, lse_ref, dl_ref, dqt_ref, dk_ref, dv_ref):
        j, i = pl.program_id(1), pl.program_id(2)

        @pl.when((j == 0) & (i == 0))
        def _():
            dqt_ref[...] = jnp.zeros(dqt_ref.shape, F32)

        q, kk, kt, vv, do_ = q_ref[...], k_ref[...], kt_ref[...], v_ref[...], do_ref[...]
        lse_t, dl_t = lse_ref[0] * LOG2E, dl_ref[0]
        st = (lse_t[0:1, :], lse_t[1:2, :], dl_t[0:1, :], dl_t[1:2, :])
        row = lax.broadcasted_iota(jnp.int32, (2 * LANE, 1), 0)
        rmask = ((row < HEAD_DIM) | ((row >= LANE) & (row < LANE + QK_ROPE)),
                 ((row >= HEAD_DIM) & (row < LANE)) | ((row >= LANE + QK_ROPE) & (row < LANE + 2 * QK_ROPE)))
        vm = _pair_masks(LANE)
        dqt = dk = dv = None
        for hh, msk in enumerate(_pair_masks(2 * LANE)):
            qm = jnp.where(msk, q, jnp.zeros_like(q))
            dom = jnp.where(vm[hh], do_, jnp.zeros_like(do_))
            dpt = _nt(vv, dom)
            pt = jnp.exp2(_nt(kk, qm) * c - st[hh])
            dst = pt * (dpt - st[2 + hh])
            pb, dsb = pt.astype(BF16), dst.astype(BF16)
            dv_h = jnp.dot(pb, dom, preferred_element_type=F32)
            dk_h = jnp.dot(dsb, qm, preferred_element_type=F32)
            dqt_h = jnp.dot(jnp.where(rmask[hh], kt, jnp.zeros_like(kt)), dsb, preferred_element_type=F32)
            dqt = dqt_h if dqt is None else dqt + dqt_h
            dk = dk_h if dk is None else dk + dk_h
            dv = dv_h if dv is None else dv + dv_h
        cols = pl.ds(pl.multiple_of(i * tq, tq), tq)
        dqt_ref[:, cols] += dqt
        _accum(dk_ref, dk, i == 0)
        _accum(dv_ref, dv, i == 0)

    st_spec = pl.BlockSpec((1, 8, tq), lambda p, j, i: (p, 0, i))
    return pl.pallas_call(
        body, name="dense_bwd", grid=(3, s // tk, s // tq),
        in_specs=[pl.BlockSpec((tq, 2 * LANE), lambda p, j, i: (i, p)), pl.BlockSpec((tk, 2 * LANE), lambda p, j, i: (j, p)),
                  pl.BlockSpec((2 * LANE, tk), lambda p, j, i: (p, j)), pl.BlockSpec((tk, LANE), lambda p, j, i: (j, 2 * p)),
                  pl.BlockSpec((tq, LANE), lambda p, j, i: (i, p)), st_spec, st_spec],
        out_specs=(pl.BlockSpec((2 * LANE, s), lambda p, j, i: (p, 0)), pl.BlockSpec((tk, 2 * LANE), lambda p, j, i: (j, p)),
                   pl.BlockSpec((tk, LANE), lambda p, j, i: (j, p))),
        out_shape=(_sds((W_A2, s), F32), _sds((s, W_A2), F32), _sds((s, WIDTH_AB), F32)),
        compiler_params=_cp("parallel", "arbitrary", "arbitrary"))(qa, ka, kat, va1, do, lse_rows, delta_rows)


BAND_TILE = 1024
BAND_SUB = 128
QKV_W = 3 * WIDTH_AB


def _band_bias_table():
    row = np.arange(BAND_SUB)[:, None]
    col = np.arange(2 * BAND_SUB)[None, :]
    band = np.abs(row - col + BAND_HALF) <= BAND_HALF
    variants = []
    for idx in range(4):
        ok = band & ((col >= BAND_HALF) | ((idx & 1) == 0)) & ((col < 2 * BAND_SUB - BAND_HALF) | ((idx & 2) == 0))
        one = np.where(ok, 0.0, NEG_INF).astype(np.float32)
        variants.append(np.concatenate([one, one], axis=0))
    return jnp.asarray(np.stack(variants))


def _band_specs(t, n):
    hpt = t // BAND_HALF
    last = n // BAND_HALF - 1
    return [pl.BlockSpec((BAND_HALF, QKV_W), lambda r, i: (jnp.maximum(i * hpt - 1, 0), r)),
            pl.BlockSpec((t, QKV_W), lambda r, i: (i, r)),
            pl.BlockSpec((BAND_HALF, QKV_W), lambda r, i: (jnp.minimum((i + 1) * hpt, last), r)),
            pl.BlockSpec((4, 2 * BAND_SUB, 2 * BAND_SUB), lambda r, i: (0, 0, 0))]


def _band_bias(b_ref, a, nsub, i, nt):
    idx = 0
    if a == 0:
        idx = idx + (i == 0).astype(jnp.int32)
    if a == nsub - 1:
        idx = idx + 2 * (i == nt - 1).astype(jnp.int32)
    return b_ref[idx]


def _band_kv(left, main, right, p):
    kc = slice(WIDTH_AB + p * LANE, WIDTH_AB + (p + 1) * LANE)
    vc = slice(2 * WIDTH_AB + p * LANE, 2 * WIDTH_AB + (p + 1) * LANE)
    return (jnp.concatenate([left[:, kc], main[:, kc], right[:, kc]], axis=0),
            jnp.concatenate([left[:, vc], main[:, vc], right[:, vc]], axis=0))


def _banded_fwd(qkvb, dil, bias):
    s = qkvb.shape[0]
    n = s // dil
    t = min(n, BAND_TILE)
    nsub, nt = t // BAND_SUB, n // t
    view = qkvb.reshape(n, dil * QKV_W)

    def body(left, main, right, b_ref, o_ref, lse_ref):
        i = pl.program_id(1)
        for p in range(3):
            pc = slice(p * LANE, (p + 1) * LANE)
            kk, vv = _band_kv(left, main, right, p)
            for a in range(nsub):
                rows, win = slice(a * BAND_SUB, (a + 1) * BAND_SUB), slice(a * BAND_SUB, (a + 2) * BAND_SUB)
                o, lse = _softmax_pair(main[rows, pc], kk[win], vv[win], _band_bias(b_ref, a, nsub, i, nt))
                o_ref[rows, pc] = o
                lse_ref[rows, pc] = lse

    o_spec = pl.BlockSpec((t, WIDTH_AB), lambda r, i: (i, r))
    o, lse = pl.pallas_call(
        body, name=f"banded_fwd_d{dil}", grid=(dil, nt), in_specs=_band_specs(t, n), out_specs=(o_spec, o_spec),
        out_shape=(_sds((n, dil * WIDTH_AB), F32), _sds((n, dil * WIDTH_AB), F32)),
        compiler_params=_cp("parallel", "parallel"))(view, view, view, bias)
    return o.reshape(s, WIDTH_AB), lse.reshape(s, WIDTH_AB)


def _banded_bwd(qkvb, do, lse, delta, dil, bias):
    s = qkvb.shape[0]
    n = s // dil
    t = min(n, BAND_TILE)
    nsub, nt = t // BAND_SUB, n // t
    view = qkvb.reshape(n, dil * QKV_W)
    side = [a.reshape(n, dil * WIDTH_AB) for a in (do, lse, delta)]

    def body(left, main, right, b_ref, do_ref, lse_ref, dl_ref, dq_ref, dk_ref, dv_ref):
        i = pl.program_id(1)

        @pl.when(i == 0)
        def _():
            dk_ref[...] = jnp.zeros(dk_ref.shape, F32)
            dv_ref[...] = jnp.zeros(dv_ref.shape, F32)

        lrow = pl.multiple_of(jnp.maximum(i * t - BAND_HALF, 0), BAND_HALF)
        rrow = pl.multiple_of(jnp.minimum((i + 1) * t, n - BAND_HALF), BAND_HALF)
        mrow = pl.multiple_of(i * t, BAND_HALF)
        for p in range(3):
            pc = slice(p * LANE, (p + 1) * LANE)
            kk, vv = _band_kv(left, main, right, p)
            parts = []
            for a in range(nsub):
                rows, win = slice(a * BAND_SUB, (a + 1) * BAND_SUB), slice(a * BAND_SUB, (a + 2) * BAND_SUB)
                dq, dk, dv, _ = _softmax_pair_bwd(main[rows, pc], kk[win], vv[win], do_ref[rows, pc], lse_ref[rows, pc],
                                                  dl_ref[rows, pc], _band_bias(b_ref, a, nsub, i, nt))
                dq_ref[rows, pc] = dq
                parts.append((dk, dv))
            for which, ref in ((0, dk_ref), (1, dv_ref)):
                chunks = []
                for c in range(nsub + 1):
                    g = parts[c][which][:BAND_SUB] if c < nsub else None
                    if c >= 1:
                        h = parts[c - 1][which][BAND_SUB:]
                        g = h if g is None else g + h
                    chunks.append(g)
                mid = jnp.concatenate([chunks[0][BAND_HALF:]] + chunks[1:nsub] + [chunks[nsub][:BAND_HALF]], axis=0)
                ref[pl.ds(lrow, BAND_HALF), pc] += chunks[0][:BAND_HALF]
                ref[pl.ds(mrow, t), pc] += mid
                ref[pl.ds(rrow, BAND_HALF), pc] += chunks[nsub][BAND_HALF:]

    q_spec = pl.BlockSpec((t, WIDTH_AB), lambda r, i: (i, r))
    acc_spec = pl.BlockSpec((n, WIDTH_AB), lambda r, i: (0, r))
    shp = _sds((n, dil * WIDTH_AB), F32)
    outs = pl.pallas_call(
        body, name=f"banded_bwd_d{dil}", grid=(dil, nt), in_specs=_band_specs(t, n) + [q_spec, q_spec, q_spec],
        out_specs=(q_spec, acc_spec, acc_spec), out_shape=(shp, shp, shp),
        compiler_params=_cp("parallel", "arbitrary"))(view, view, view, bias, *side)
    return [a.reshape(s, WIDTH_AB) for a in outs]


def _merge_branches(outs, lses):
    s = outs[0].shape[0]
    ts = _tile(s, (512, 256, 128))

    def body(o1, o2, o3, l1, l2, l3, o_ref, lse_ref):
        a, b, c = l1[...], l2[...], l3[...]
        m = jnp.maximum(jnp.maximum(a, b), c)
        ea, eb, ec = jnp.exp(a - m), jnp.exp(b - m), jnp.exp(c - m)
        den = ea + eb + ec
        o_ref[...] = (o1[...] * ea + o2[...] * eb + o3[...] * ec) / den
        lse_ref[...] = m + jnp.log(den)

    sp = _row_spec(ts, WIDTH_AB)
    return pl.pallas_call(
        body, name="merge_branches", grid=(s // ts,), in_specs=[sp] * 6, out_specs=(sp, sp),
        out_shape=(_sds((s, WIDTH_AB), F32), _sds((s, WIDTH_AB), F32)), compiler_params=_cp("parallel"))(*outs, *lses)


def _na_geometry(s):
    rows = s // GRID_W
    assert rows >= 2 * NA_ROWS and rows % NA_ROWS == 0
    return rows, rows // NA_ROWS


def _na_row(n, i, rows):
    rq = n * NA_ROWS + i
    rs = jnp.clip(rq - NA_ROWS // 2, 0, rows - NA_ROWS)
    return pl.multiple_of(rs * GRID_W, GRID_W), rs - rq + NA_ROWS - 1


NA_KEYS = NA_ROWS * GRID_W


def _natten_fwd(qkvc, tfull):
    s = qkvc.shape[0]
    rows, nrb = _na_geometry(s)
    tq = NA_ROWS * GRID_W

    def body(q_ref, k_ref, v_ref, t_ref, o_ref, lse_ref):
        n = pl.program_id(1)
        for i in range(NA_ROWS):
            tok, base = _na_row(n, i, rows)
            kk, vv = k_ref[pl.ds(tok, NA_KEYS), :], v_ref[pl.ds(tok, NA_KEYS), :]
            sl = slice(i * GRID_W, (i + 1) * GRID_W)
            bias2 = jnp.concatenate([t_ref[0, base], t_ref[1, base]], axis=0)
            o, lse = _softmax_pair(q_ref[sl, :], kk, vv, bias2)
            o_ref[sl, :] = o
            lse_ref[sl, :] = lse

    o_spec = pl.BlockSpec((tq, LANE), lambda p, n: (n, p))
    return pl.pallas_call(
        body, name="natten_fwd", grid=(2, nrb),
        in_specs=[pl.BlockSpec((tq, LANE), lambda p, n: (n, p)), pl.BlockSpec((s, LANE), lambda p, n: (0, 2 + p)),
                  pl.BlockSpec((s, LANE), lambda p, n: (0, 4 + p)),
                  pl.BlockSpec((2, NA_ROWS, GRID_W, NA_KEYS), lambda p, n: (p, 0, 0, 0))],
        out_specs=(o_spec, o_spec), out_shape=(_sds((s, WIDTH_C), F32), _sds((s, WIDTH_C), F32)),
        compiler_params=_cp("parallel", "parallel"))(qkvc, qkvc, qkvc, tfull)


def _natten_bwd(qkvc, tfull, do, lse, delta):
    s = qkvc.shape[0]
    rows, nrb = _na_geometry(s)
    tq = NA_ROWS * GRID_W

    def body(q_ref, k_ref, v_ref, t_ref, do_ref, lse_ref, dl_ref, dq_ref, dk_ref, dv_ref, dt_ref):
        n = pl.program_id(1)

        @pl.when(n == 0)
        def _():
            dk_ref[...] = jnp.zeros(dk_ref.shape, F32)
            dv_ref[...] = jnp.zeros(dv_ref.shape, F32)
            dt_ref[...] = jnp.zeros(dt_ref.shape, F32)

        for i in range(NA_ROWS):
            tok, base = _na_row(n, i, rows)
            win = pl.ds(tok, NA_KEYS)
            sl = slice(i * GRID_W, (i + 1) * GRID_W)
            bias2 = jnp.concatenate([t_ref[0, base], t_ref[1, base]], axis=0)
            dq, dk, dv, ds = _softmax_pair_bwd(q_ref[sl, :], k_ref[win, :], v_ref[win, :], do_ref[sl, :], lse_ref[sl, :],
                                               dl_ref[sl, :], bias2)
            dq_ref[sl, :] = dq
            dk_ref[win, :] += dk
            dv_ref[win, :] += dv
            dt_ref[0, base] += ds[:GRID_W]
            dt_ref[1, base] += ds[GRID_W:]

    q_spec = pl.BlockSpec((tq, LANE), lambda p, n: (n, p))
    acc_spec = pl.BlockSpec((s, LANE), lambda p, n: (0, p))
    t_spec = pl.BlockSpec((2, NA_ROWS, GRID_W, NA_KEYS), lambda p, n: (p, 0, 0, 0))
    shp = _sds((s, WIDTH_C), F32)
    return pl.pallas_call(
        body, name="natten_bwd", grid=(2, nrb),
        in_specs=[q_spec, pl.BlockSpec((s, LANE), lambda p, n: (0, 2 + p)), pl.BlockSpec((s, LANE), lambda p, n: (0, 4 + p)),
                  t_spec, q_spec, q_spec, q_spec],
        out_specs=(q_spec, acc_spec, acc_spec, t_spec),
        out_shape=(shp, shp, shp, _sds((HEADS_C, NA_ROWS, GRID_W, NA_KEYS), F32)),
        compiler_params=_cp("parallel", "arbitrary"))(qkvc, qkvc, qkvc, tfull, do, lse, delta)


def _rpb_constants():
    p = np.arange(GRID_W)[:, None]
    qc = np.arange(GRID_W)[None, :]
    dc = np.clip(qc - p, -(NA_COLS - 1), NA_COLS - 1) + NA_COLS - 1
    onehot = (dc.reshape(1, -1) == np.arange(32)[:, None]).astype(np.float32)
    c_start = np.clip(p - NA_COLS // 2, 0, GRID_W - NA_COLS)
    col_ok = ((qc >= c_start) & (qc < c_start + NA_COLS)).reshape(1, -1).astype(np.float32)
    a = np.arange(16)[:, None]
    bj = np.arange(64)[None, :]
    row_sel = ((bj // 8 + bj % 8) == a).astype(np.float32)
    return jnp.asarray(onehot), jnp.asarray(col_ok), jnp.asarray(row_sel)


def _rpb_expand(rpb, onehot, col_ok):
    r2 = jnp.pad(rpb.reshape(HEADS_C * 15, 31), ((0, 4), (0, 1)))

    def body(r_ref, oh_ref, ok_ref, o_ref):
        t = jnp.dot(r_ref[...], oh_ref[...], preferred_element_type=F32, precision=lax.Precision.HIGHEST)
        o_ref[...] = jnp.where(ok_ref[...] > 0.5, t, NEG_INF)

    tm = pl.pallas_call(body, name="rpb_expand", out_shape=_sds((64, GRID_W * GRID_W), F32))(r2, onehot, col_ok)
    tm = tm[:HEADS_C * 15].reshape(HEADS_C, 15, GRID_W, GRID_W)
    tfull = jnp.stack([jnp.concatenate([tm[:, base + j] for j in range(NA_ROWS)], axis=-1) for base in range(NA_ROWS)], axis=1)
    return tfull


def _rpb_grad(dtfull, onehot, row_sel):
    g = dtfull.reshape(HEADS_C, NA_ROWS, GRID_W, NA_ROWS, GRID_W).transpose(0, 1, 3, 2, 4).reshape(HEADS_C, 64, GRID_W * GRID_W)

    def body(g_ref, oh_ref, sel_ref, o_ref):
        for h in range(HEADS_C):
            mid = lax.dot_general(g_ref[h], oh_ref[...], (((1,), (1,)), ((), ())), preferred_element_type=F32,
                                  precision=lax.Precision.HIGHEST)
            o_ref[h] = jnp.dot(sel_ref[...], mid, preferred_element_type=F32, precision=lax.Precision.HIGHEST)

    out = pl.pallas_call(body, name="rpb_grad", out_shape=_sds((HEADS_C, 16, 32), F32))(g, onehot, row_sel)
    return out[:, :15, :31]


def _outnorm_fwd(o_a, o_b, o_c, ga, gb, gc):
    s = o_a.shape[0]
    ts = _tile(s, (512, 256, 128))

    def body(a_ref, b_ref, c_ref, ga_ref, gb_ref, gc_ref, o_ref):
        col = 0
        for ref, g in ((a_ref, ga_ref), (b_ref, gb_ref), (c_ref, gc_ref)):
            x = ref[...]
            o_ref[:, col:col + x.shape[1]] = (x * _rstd(x) * g[...]).astype(BF16)
            col += x.shape[1]

    return pl.pallas_call(
        body, name="outnorm_fwd", grid=(s // ts,),
        in_specs=[_row_spec(ts, WIDTH_AB), _row_spec(ts, WIDTH_AB), _row_spec(ts, WIDTH_C), _fix_spec(WIDTH_AB),
                  _fix_spec(WIDTH_AB), _fix_spec(WIDTH_C)],
        out_specs=_row_spec(ts, D_MODEL), out_shape=_sds((s, D_MODEL), BF16),
        compiler_params=_cp("parallel"))(o_a, o_b, o_c, ga.reshape(1, -1), gb.reshape(1, -1), gc.reshape(1, -1))


def _outnorm_bwd(dmixed, o_a, o_b, o_c, ga, gb, gc):
    s = o_a.shape[0]
    ts = _tile(s, (512, 256, 128))

    def body(dm_ref, a_ref, b_ref, c_ref, ga_ref, gb_ref, gc_ref, *outs):
        first = pl.program_id(0) == 0
        col = 0
        for k, (ref, g) in enumerate(((a_ref, ga_ref), (b_ref, gb_ref), (c_ref, gc_ref))):
            x = ref[...]
            w = x.shape[1]
            dx, dg = _rms_bwd_rows(x, g[...], dm_ref[:, col:col + w])
            col += w
            outs[k][...] = dx.astype(BF16)
            for b, blk in enumerate(_group_sum(dx * x)):
                if k == 0:
                    outs[3][b] = _stat_rows(blk)
                else:
                    outs[3 + k][:, b * LANE:(b + 1) * LANE] = blk
            _accum(outs[6 + k], dg, first)

    widths = (WIDTH_AB, WIDTH_AB, WIDTH_C)
    return pl.pallas_call(
        body, name="outnorm_bwd", grid=(s // ts,),
        in_specs=[_row_spec(ts, D_MODEL)] + [_row_spec(ts, w) for w in widths] + [_fix_spec(w) for w in widths],
        out_specs=tuple([_row_spec(ts, w) for w in widths] + [pl.BlockSpec((3, 8, ts), lambda i: (0, 0, i))]
                        + [_row_spec(ts, w) for w in widths[1:]] + [_fix_spec(w) for w in widths]),
        out_shape=tuple([_sds((s, w), BF16) for w in widths] + [_sds((3, 8, s), F32)]
                        + [_sds((s, w), F32) for w in widths[1:]] + [_sds((1, w), F32) for w in widths]),
        compiler_params=_cp("arbitrary"))(dmixed, o_a, o_b, o_c, ga.reshape(1, -1), gb.reshape(1, -1), gc.reshape(1, -1))


def _adamw(w, g, m, v, *, name):
    r, c = w.shape
    tr = _tile(r, (512, 256, 128, 64, 8))

    def body(w_ref, g_ref, m_ref, v_ref, d_ref, nm_ref, nv_ref):
        gv = g_ref[...]
        nm = ADAM_B1 * m_ref[...] + (1.0 - ADAM_B1) * gv
        nv = ADAM_B2 * v_ref[...] + (1.0 - ADAM_B2) * jnp.square(gv)
        m_hat = nm / (1.0 - ADAM_B1 ** ADAM_STEP)
        v_hat = nv / (1.0 - ADAM_B2 ** ADAM_STEP)
        d_ref[...] = -ADAM_LR * (m_hat / (jnp.sqrt(v_hat) + ADAM_EPS) + ADAM_WD * w_ref[...])
        nm_ref[...] = nm
        nv_ref[...] = nv

    sp = _row_spec(tr, c)
    return pl.pallas_call(
        body, name=name, grid=(r // tr,), in_specs=[sp] * 4, out_specs=(sp, sp, sp),
        out_shape=(_sds((r, c), F32),) * 3, compiler_params=_cp("parallel"))(w, g, m, v)


def _add_n(parts, *, name, out_dtype):
    r, c = parts[0].shape
    tr = max(t for t in range(16, 1025, 16) if r % t == 0)

    def body(*refs):
        acc = refs[0][...].astype(F32)
        for ref in refs[1:-1]:
            acc = acc + ref[...].astype(F32)
        refs[-1][...] = acc.astype(out_dtype)

    sp = _row_spec(tr, c)
    return pl.pallas_call(
        body, name=name, grid=(r // tr,), in_specs=[sp] * len(parts), out_specs=sp, out_shape=_sds((r, c), out_dtype),
        compiler_params=_cp("parallel"))(*parts)


ANY = pl.BlockSpec(memory_space=pl.ANY)
CHIP_FLIPS = ((1, 0), (0, 1), (1, 1))


def _me():
    return lax.axis_index("x"), lax.axis_index("y"), lax.axis_index("c")


def _gather_chips(half):
    def body(src, out, send_sems, recv_sems):
        x, y, c = _me()
        mine = 2 * x + y

        def copy(k, chip, half_idx, to, source=None):
            dst = out.at[chip, half_idx]
            return pltpu.make_async_remote_copy(src_ref=dst if source is None else source, dst_ref=dst,
                                                send_sem=send_sems.at[k], recv_sem=recv_sems.at[k], device_id=to,
                                                device_id_type=MESH_T)

        chips = [(x ^ fx, y ^ fy) for fx, fy in CHIP_FLIPS]
        first = [copy(k, mine, c, (cx, cy, c), source=src) for k, (cx, cy) in enumerate(chips)]
        for cp in first:
            cp.start()
        passed = []
        for k, (cx, cy) in enumerate(chips):
            theirs = 2 * cx + cy
            copy(k, theirs, c, (x, y, c)).wait_recv()
            cp = copy(3 + k, theirs, c, (x, y, 1 - c))
            cp.start()
            passed.append(cp)
        for k, (cx, cy) in enumerate(chips):
            copy(3 + k, 2 * cx + cy, 1 - c, (x, y, c)).wait_recv()
        for cp in first + passed:
            cp.wait_send()

    return pl.pallas_call(
        body, name="gather_chips", in_specs=[ANY], out_specs=ANY, out_shape=_sds((4, 2) + half.shape, half.dtype),
        scratch_shapes=[pltpu.SemaphoreType.DMA((6,)), pltpu.SemaphoreType.DMA((6,))])(half)


def _swap_sibling(block):
    def body(src, out, send_sem, recv_sem):
        x, y, c = _me()
        cp = pltpu.make_async_remote_copy(src_ref=src, dst_ref=out, send_sem=send_sem, recv_sem=recv_sem,
                                          device_id=(x, y, 1 - c), device_id_type=MESH_T)
        cp.start()
        cp.wait()

    return pl.pallas_call(
        body, name="swap_sibling", in_specs=[ANY], out_specs=ANY, out_shape=_sds(block.shape, block.dtype),
        scratch_shapes=[pltpu.SemaphoreType.DMA(()), pltpu.SemaphoreType.DMA(())])(block)


def _scatter_chips(parts):
    def body(src, out, send_sems, recv_sems):
        x, y, c = _me()
        mine = 2 * x + y
        sends = []
        for k, (fx, fy) in enumerate(CHIP_FLIPS):
            theirs = 2 * (x ^ fx) + (y ^ fy)
            cp = pltpu.make_async_remote_copy(src_ref=src.at[theirs], dst_ref=out.at[mine], send_sem=send_sems.at[k],
                                              recv_sem=recv_sems.at[k], device_id=(x ^ fx, y ^ fy, c), device_id_type=MESH_T)
            cp.start()
            sends.append(cp)
        for k, (fx, fy) in enumerate(CHIP_FLIPS):
            theirs = 2 * (x ^ fx) + (y ^ fy)
            pltpu.make_async_remote_copy(src_ref=src.at[theirs], dst_ref=out.at[theirs], send_sem=send_sems.at[k],
                                         recv_sem=recv_sems.at[k], device_id=(x ^ fx, y ^ fy, c),
                                         device_id_type=MESH_T).wait_recv()
        for cp in sends:
            cp.wait_send()

    return pl.pallas_call(
        body, name="scatter_chips", in_specs=[ANY], out_specs=ANY, out_shape=_sds(parts.shape, parts.dtype),
        scratch_shapes=[pltpu.SemaphoreType.DMA((3,)), pltpu.SemaphoreType.DMA((3,))])(parts)


def _all_reduce_small(block):
    r, c = block.shape

    def body(src, out, slots, send_sems, recv_sems):
        x, y, cc = _me()
        mine = 4 * x + 2 * y + cc
        slots[mine] = src[...]
        sends = []
        for k in range(1, 8):
            fx, fy, fc = (k >> 2) & 1, (k >> 1) & 1, k & 1
            cp = pltpu.make_async_remote_copy(src_ref=src, dst_ref=slots.at[mine], send_sem=send_sems.at[k - 1],
                                              recv_sem=recv_sems.at[k - 1], device_id=(x ^ fx, y ^ fy, cc ^ fc),
                                              device_id_type=MESH_T)
            cp.start()
            sends.append(cp)
        for k in range(1, 8):
            fx, fy, fc = (k >> 2) & 1, (k >> 1) & 1, k & 1
            theirs = 4 * (x ^ fx) + 2 * (y ^ fy) + (cc ^ fc)
            pltpu.make_async_remote_copy(src_ref=src, dst_ref=slots.at[theirs], send_sem=send_sems.at[k - 1],
                                         recv_sem=recv_sems.at[k - 1], device_id=(x ^ fx, y ^ fy, cc ^ fc),
                                         device_id_type=MESH_T).wait_recv()
        for cp in sends:
            cp.wait_send()
        acc = slots[0]
        for d in range(1, 8):
            acc = acc + slots[d]
        out[...] = acc

    vm = pl.BlockSpec(memory_space=pltpu.VMEM)
    return pl.pallas_call(
        body, name="all_reduce_small", in_specs=[vm], out_specs=vm, out_shape=_sds((r, c), F32),
        scratch_shapes=[pltpu.VMEM((8, r, c), F32), pltpu.SemaphoreType.DMA((7,)), pltpu.SemaphoreType.DMA((7,))])(block)


BIG = ("w_in", "w_uq", "w_ukv", "w_out", "w_mlp_in", "w_mlp_out")
COL_SHARDED = {"w_in": True, "w_uq": True, "w_ukv": True, "w_out": False, "w_mlp_in": True, "w_mlp_out": False}
SMALL = ("g_mix", "q_norm", "kv_norm", "rpb", "out_norm_a", "out_norm_b", "out_norm_c", "g_mlp", "g_final")
PACK_C = 1024
ROW_ALIGN = 32


def _pack_rows(parts):
    flat = jnp.concatenate([p.reshape(-1, PACK_C) for p in parts], axis=0)
    return jnp.pad(flat, ((0, -flat.shape[0] % ROW_ALIGN), (0, 0)))


def _unpack_rows(flat, shapes):
    out, row = [], 0
    for shp in shapes:
        n = int(np.prod(shp)) // PACK_C
        out.append(flat[row:row + n].reshape(shp))
        row += n
    return out


def _full_from_shards(name, g):
    if COL_SHARDED[name]:
        return g.transpose(1, 2, 0, 3).reshape(g.shape[1], g.shape[2], 4 * g.shape[3])
    return g.transpose(1, 0, 2, 3).reshape(g.shape[1], 4 * g.shape[2], g.shape[3])


def _shards_from_full(name, w):
    l, k, n = w.shape
    if COL_SHARDED[name]:
        return w.reshape(l, k, 4, n // 4).transpose(2, 0, 1, 3)
    return w.reshape(l, 4, k // 4, n).transpose(1, 0, 2, 3)


def _arrange_w_in(w):
    z = jnp.zeros(w.shape[:-1] + (COL_B - COL_KPE - QK_ROPE,), w.dtype)
    return jnp.concatenate([w[..., :COL_KPE + QK_ROPE], z, w[..., COL_KPE + QK_ROPE:]], axis=-1)


def _unarrange_w_in(w):
    return jnp.concatenate([w[..., :COL_KPE + QK_ROPE], w[..., COL_B:]], axis=-1)


def _arrange_w_uq(w):
    per = HEAD_DIM + QK_ROPE
    z = jnp.zeros(w.shape[:-1] + (HEAD_DIM,), w.dtype)
    cols = []
    for p in range(3):
        a, b = 2 * p * per, (2 * p + 1) * per
        cols += [w[..., a:a + HEAD_DIM], w[..., b:b + HEAD_DIM], w[..., a + HEAD_DIM:a + per], w[..., b + HEAD_DIM:b + per], z]
    return jnp.concatenate(cols, axis=-1)


def _unarrange_w_uq(w):
    cols = []
    for h in range(HEADS_A):
        p, e = divmod(h, 2)
        base = 2 * p * LANE
        cols += [w[..., base + e * HEAD_DIM:base + (e + 1) * HEAD_DIM],
                 w[..., base + LANE + e * QK_ROPE:base + LANE + (e + 1) * QK_ROPE]]
    return jnp.concatenate(cols, axis=-1)


def _arrange_w_ukv(w):
    ks = [w[..., h * LANE:h * LANE + HEAD_DIM] for h in range(HEADS_A)]
    vs = [w[..., h * LANE + HEAD_DIM:(h + 1) * LANE] for h in range(HEADS_A)]
    return jnp.concatenate(ks + vs, axis=-1)


def _unarrange_w_ukv(w):
    cols = []
    for h in range(HEADS_A):
        cols += [w[..., h * HEAD_DIM:(h + 1) * HEAD_DIM], w[..., WIDTH_AB + h * HEAD_DIM:WIDTH_AB + (h + 1) * HEAD_DIM]]
    return jnp.concatenate(cols, axis=-1)


def _layer_fwd(x, w, sm, tabs, consts):
    t32, t64 = tabs
    onehot, col_ok, _, band = consts
    h, proj = _norm_mm(x, sm["g_mix"], w["w_in"], name="in_proj", relu2=False)
    cqn, ckvn, kpe, qkvb, qkvc = _prep_fwd(proj, sm["q_norm"], sm["kv_norm"], t32, t64)
    qa = _mm_nn(cqn, w["w_uq"], name="q_up")
    kva = _mm_nn(ckvn, w["w_ukv"], name="kv_up")
    qa2, ka2, kat, va1 = _a_post_fwd(qa, kva, kpe, t32)
    o_a, lse_a = _dense_fwd(qa2, ka2, va1)
    branch = [_banded_fwd(qkvb, dil, band) for _, dil in DILATED_PAIRS]
    o_b, lse_b = _merge_branches([b[0] for b in branch], [b[1] for b in branch])
    tfull = _rpb_expand(sm["rpb"], onehot, col_ok)
    o_c, lse_c = _natten_fwd(qkvc, tfull)
    mixed = _outnorm_fwd(o_a, o_b, o_c, sm["out_norm_a"], sm["out_norm_b"], sm["out_norm_c"])
    x_mid = _mm_nn(mixed, w["w_out"], name="out_proj", res=x)
    h2, act = _norm_mm(x_mid, sm["g_mlp"], w["w_mlp_in"], name="mlp_in", relu2=True)
    x_out = _mm_nn(act, w["w_mlp_out"], name="mlp_out", res=x_mid)
    saved = dict(x=x, h=h, proj=proj, cqn=cqn, ckvn=ckvn, qkvb=qkvb, qkvc=qkvc, qa2=qa2, ka2=ka2, kat=kat, va1=va1, o_a=o_a,
                 lse_a=lse_a, o_b=o_b, lse_b=lse_b, o_c=o_c, lse_c=lse_c, tfull=tfull, mixed=mixed, x_mid=x_mid, h2=h2,
                 act=act)
    return x_out, saved


def _layer_bwd(dx, dxb, sv, w, sm, tabs, consts):
    t32, t64 = tabs
    onehot, _, row_sel, band = consts
    g = {}
    du = _mm_nt(dxb, w["w_mlp_out"], name="mlp_out_dx", out_dtype=BF16, relu2_act=sv["act"])
    g["w_mlp_out"] = _mm_tn(sv["act"], dxb, name="mlp_out_dw")
    dh2 = _mm_nt(du, w["w_mlp_in"], name="mlp_in_dx")
    g["w_mlp_in"] = _mm_tn(sv["h2"], du, name="mlp_in_dw")
    dx_mid, dmb, g["g_mlp"] = _rms_bwd(sv["x_mid"], sm["g_mlp"], dh2, dx, name="norm_mlp_bwd")
    dmixed = _mm_nt(dmb, w["w_out"], name="out_proj_dx")
    g["w_out"] = _mm_tn(sv["mixed"], dmb, name="out_proj_dw")
    (do_a, do_b, do_c, dl_a, dl_b, dl_c, g["out_norm_a"], g["out_norm_b"], g["out_norm_c"]) = _outnorm_bwd(
        dmixed, sv["o_a"], sv["o_b"], sv["o_c"], sm["out_norm_a"], sm["out_norm_b"], sm["out_norm_c"])
    dqa2_t, dka2, dva = _dense_bwd(sv["qa2"], sv["ka2"], sv["kat"], sv["va1"], do_a, sv["lse_a"], dl_a)
    db = []
    for _, dil in DILATED_PAIRS:
        db += _banded_bwd(sv["qkvb"], do_b, sv["lse_b"], dl_b, dil, band)
    dq_c, dk_c, dv_c, dtfull = _natten_bwd(sv["qkvc"], sv["tfull"], do_c, sv["lse_c"], dl_c)
    g["rpb"] = _rpb_grad(dtfull, onehot, row_sel)
    dqa, dkva, dkpe = _a_post_bwd(dqa2_t, dka2, dva, t32)
    dcqn = _mm_nt(dqa, w["w_uq"], name="q_up_dx")
    g["w_uq"] = _unarrange_w_uq(_mm_tn(sv["cqn"], dqa, name="q_up_dw"))
    dckvn = _mm_nt(dkva, w["w_ukv"], name="kv_up_dx")
    g["w_ukv"] = _unarrange_w_ukv(_mm_tn(sv["ckvn"], dkva, name="kv_up_dw"))
    dproj, g["q_norm"], g["kv_norm"] = _prep_bwd(sv["proj"], sm["q_norm"], sm["kv_norm"], t32, t64, dcqn, dckvn, dkpe,
                                                  db, (dq_c, dk_c, dv_c))
    dh = _mm_nt(dproj, w["w_in"], name="in_proj_dx")
    g["w_in"] = _unarrange_w_in(_mm_tn(sv["h"], dproj, name="in_proj_dw"))
    dx_in, dxb_in, g["g_mix"] = _rms_bwd(sv["x"], sm["g_mix"], dh, dx_mid, name="norm_mix_bwd")
    return dx_in, dxb_in, g


def _local_step(x, target, wfull, small):
    s = x.shape[0]
    tabs = (_rope_tables(s, QK_ROPE // 2, 2), _rope_tables(s, HEAD_DIM // 2, 2))
    consts = _rpb_constants() + (_band_bias_table(),)
    saved = []
    for l in range(DEPTH):
        wl = {k: v[l] for k, v in wfull.items()}
        sl = {k: small[k][l] for k in SMALL if k != "g_final"}
        x, sv = _layer_fwd(x, wl, sl, tabs, consts)
        saved.append(sv)
    loss, dx, dxb, dg_final = _loss_head(x, small["g_final"], target)
    grads = [None] * DEPTH
    for l in reversed(range(DEPTH)):
        wl = {k: v[l] for k, v in wfull.items()}
        sl = {k: small[k][l] for k in SMALL if k != "g_final"}
        dx, dxb, grads[l] = _layer_bwd(dx, dxb, saved[l], wl, sl, tabs, consts)
    return loss, dx, grads, dg_final


ARRANGE = {"w_in": _arrange_w_in, "w_uq": _arrange_w_uq, "w_ukv": _arrange_w_ukv}


def kernel(x, g_mix, w_in, q_norm, w_uq, kv_norm, w_ukv, rpb, out_norm_a, out_norm_b, out_norm_c, w_out, g_mlp, w_mlp_in, w_mlp_out, g_final, loss_target, m_g_mix, m_w_in, m_q_norm, m_w_uq, m_kv_norm, m_w_ukv, m_rpb, m_out_norm_a, m_out_norm_b, m_out_norm_c, m_w_out, m_g_mlp, m_w_mlp_in, m_w_mlp_out, m_g_final, v_g_mix, v_w_in, v_q_norm, v_w_uq, v_kv_norm, v_w_ukv, v_rpb, v_out_norm_a, v_out_norm_b, v_out_norm_c, v_w_out, v_g_mlp, v_w_mlp_in, v_w_mlp_out, v_g_final):
    args = dict(locals())
    weights = {k: args[k] for k in BIG + SMALL}
    moms = {k: args["m_" + k] for k in BIG + SMALL}
    vels = {k: args["v_" + k] for k in BIG + SMALL}
    cc = lax.axis_index("c")
    my_chip = 2 * lax.axis_index("x") + lax.axis_index("y")

    shard_shapes = [weights[k].shape for k in BIG]
    packed_w = _pack_rows([weights[k].astype(BF16) for k in BIG])
    rows = packed_w.shape[0]
    my_half = lax.dynamic_index_in_dim(packed_w.reshape(2, rows // 2, PACK_C), cc, axis=0, keepdims=False)
    gathered = _gather_chips(my_half).reshape(4, rows, PACK_C)
    per_chip = [_unpack_rows(jnp.where(my_chip == j, packed_w, gathered[j]), shard_shapes) for j in range(4)]
    wfull = {}
    for idx, k in enumerate(BIG):
        full = _full_from_shards(k, jnp.stack([per_chip[j][idx] for j in range(4)]))
        wfull[k] = ARRANGE[k](full) if k in ARRANGE else full

    small = {k: weights[k] for k in SMALL}
    loss, dx, grads, dg_final = _local_step(x[0], loss_target[0], wfull, small)

    small_local = {k: jnp.stack([grads[l][k].reshape(weights[k].shape[1:]) for l in range(DEPTH)])
                   for k in SMALL if k != "g_final"}
    small_local["g_final"] = dg_final.reshape(-1)
    small_shapes = [weights[k].shape for k in SMALL]
    n_small = sum(int(np.prod(s)) for s in small_shapes)
    flat = jnp.concatenate([small_local[k].reshape(-1) for k in SMALL] + [loss[0, :1]])
    rows_small = -(-(n_small + 1) // PACK_C)
    rows_small += -rows_small % 8
    flat = jnp.pad(flat, (0, rows_small * PACK_C - n_small - 1)).reshape(rows_small, PACK_C)
    red = _all_reduce_small(flat).reshape(-1)
    loss_out = red[n_small]
    small_grads, off = {}, 0
    for k, shp in zip(SMALL, small_shapes):
        n = int(np.prod(shp))
        small_grads[k] = red[off:off + n].reshape(shp)
        off += n

    by_shard = {k: _shards_from_full(k, jnp.stack([grads[l][k] for l in range(DEPTH)])) for k in BIG}
    packed = jnp.stack([_pack_rows([by_shard[k][j] for k in BIG]) for j in range(4)])
    halves = packed.reshape(4, 2, rows // 2, PACK_C)
    mine = lax.dynamic_index_in_dim(halves, cc, axis=1, keepdims=False)
    other = lax.dynamic_index_in_dim(halves, 1 - cc, axis=1, keepdims=False)
    from_sibling = _swap_sibling(other)
    pair = _add_n([mine.reshape(-1, PACK_C), from_sibling.reshape(-1, PACK_C)], name="pair_sum",
                  out_dtype=BF16).reshape(mine.shape)
    by_chip = _scatter_chips(pair)
    reduced = _add_n([jnp.where(my_chip == j, pair[j], by_chip[j]) for j in range(4)], name="chip_sum", out_dtype=F32)
    theirs = _swap_sibling(reduced)
    joined = jnp.where(cc == 0, jnp.concatenate([reduced, theirs]), jnp.concatenate([theirs, reduced]))
    big_grads = dict(zip(BIG, _unpack_rows(joined, shard_shapes)))

    out_g, out_d, out_m, out_v = {}, {}, {}, {}
    for k in BIG:
        shp = weights[k].shape
        two_d = (shp[0] * shp[1], shp[2])
        d, nm, nv = _adamw(weights[k].reshape(two_d), big_grads[k].reshape(two_d), moms[k].reshape(two_d),
                           vels[k].reshape(two_d), name="adamw_" + k)
        out_g[k], out_d[k], out_m[k], out_v[k] = big_grads[k], d.reshape(shp), nm.reshape(shp), nv.reshape(shp)

    def pack_small(tree):
        f = jnp.concatenate([tree[k].reshape(-1) for k in SMALL])
        return jnp.pad(f, (0, rows_small * PACK_C - n_small)).reshape(rows_small, PACK_C)

    d, nm, nv = _adamw(pack_small(small), pack_small(small_grads), pack_small(moms), pack_small(vels), name="adamw_small")
    for tree, flat_out in ((out_d, d), (out_m, nm), (out_v, nv)):
        off = 0
        fo = flat_out.reshape(-1)
        for k, shp in zip(SMALL, small_shapes):
            n = int(np.prod(shp))
            tree[k] = fo[off:off + n].reshape(shp)
            off += n
    out_g.update(small_grads)

    order = ("g_mix", "w_in", "q_norm", "w_uq", "kv_norm", "w_ukv", "rpb", "out_norm_a", "out_norm_b", "out_norm_c", "w_out",
             "g_mlp", "w_mlp_in", "w_mlp_out", "g_final")
    return (loss_out, dx.reshape(x.shape), *[out_g[k] for k in order], *[out_d[k] for k in order],
            *[out_m[k] for k in order], *[out_v[k] for k in order])
```

```python
import math

import numpy as np
import jax
import jax.numpy as jnp
from jax import lax
from jax.experimental import pallas as pl
from jax.experimental.pallas import tpu as pltpu

F32 = jnp.float32
BF16 = jnp.bfloat16

D_MODEL = 1024
HEAD_DIM = 64
Q_LORA = 256
KV_LORA = 128
QK_ROPE = 32
HEADS_A = 6
HEADS_B = 6
HEADS_C = 4
DILATED_PAIRS = ((128, 1), (512, 4), (2048, 16))
BAND_HALF = 64
GRID_W = 64
NA_ROWS = 8
NA_COLS = 16
D_FF = 4096
ROPE_THETA = 10000.0
NORM_EPS = 1e-6
NEG_INF = -1e30
DEPTH = 4

LANE = 128
PROJ_W = 2432
COL_CKV = 256
COL_KPE = 384
COL_B = 512
COL_C = 1664
W_A2 = 768
WIDTH_AB = 384
WIDTH_C = 256
SCALE_A = (HEAD_DIM + QK_ROPE) ** -0.5
SCALE_BC = HEAD_DIM ** -0.5

ADAM_LR = 0.001
ADAM_B1 = 0.9
ADAM_B2 = 0.999
ADAM_EPS = 1e-08
ADAM_WD = 0.01
ADAM_STEP = 10

VMEM_LIMIT = 56 * 1024 * 1024
MESH_T = pl.DeviceIdType.MESH


def _cp(*sem):
    return pltpu.CompilerParams(dimension_semantics=sem or None, vmem_limit_bytes=VMEM_LIMIT)


def _tile(n, cands):
    for c in cands:
        if n % c == 0:
            return c
    return n


def _sds(shape, dtype):
    return jax.ShapeDtypeStruct(shape, dtype)


def _mm_nn(a, b, *, name, out_dtype=F32, res=None):
    m, k = a.shape
    n = b.shape[1]
    tm = _tile(m, (512, 256, 128))
    tn = _tile(n, (1024, 768, 512)) if n % LANE == 0 and n != PROJ_W else n

    def body(*refs):
        a_ref, b_ref = refs[0], refs[1]
        o_ref = refs[-1]
        acc = jnp.dot(a_ref[...], b_ref[...], preferred_element_type=F32)
        if res is not None:
            acc = refs[2][...] + acc
        o_ref[...] = acc.astype(o_ref.dtype)

    in_specs = [pl.BlockSpec((tm, k), lambda j, i: (i, 0)), pl.BlockSpec((k, tn), lambda j, i: (0, j))]
    args = [a, b]
    if res is not None:
        in_specs.append(pl.BlockSpec((tm, tn), lambda j, i: (i, j)))
        args.append(res)
    return pl.pallas_call(
        body, name=name, grid=(n // tn, m // tm), in_specs=in_specs,
        out_specs=pl.BlockSpec((tm, tn), lambda j, i: (i, j)), out_shape=_sds((m, n), out_dtype),
        compiler_params=_cp("parallel", "parallel"))(*args)


def _norm_mm(x, g, w, *, name, relu2):
    m, k = x.shape
    n = w.shape[1]
    tm = _tile(m, (512, 256, 128) if n <= PROJ_W else (256, 128))

    def body(x_ref, g_ref, w_ref, h_ref, o_ref):
        xv = x_ref[...]
        h = (xv * _rstd(xv) * g_ref[...]).astype(BF16)
        h_ref[...] = h
        acc = jnp.dot(h, w_ref[...], preferred_element_type=F32)
        if relu2:
            acc = jnp.square(jnp.maximum(acc, 0.0))
        o_ref[...] = acc.astype(o_ref.dtype)

    return pl.pallas_call(
        body, name=name, grid=(m // tm,),
        in_specs=[pl.BlockSpec((tm, k), lambda i: (i, 0)), pl.BlockSpec((1, k), lambda i: (0, 0)),
                  pl.BlockSpec((k, n), lambda i: (0, 0))],
        out_specs=(pl.BlockSpec((tm, k), lambda i: (i, 0)), pl.BlockSpec((tm, n), lambda i: (i, 0))),
        out_shape=(_sds((m, k), BF16), _sds((m, n), BF16 if relu2 else F32)),
        compiler_params=_cp("parallel"))(x, g.reshape(1, k), w)


def _mm_nt(a, b, *, name, out_dtype=F32, relu2_act=None):
    m, c = a.shape
    n = b.shape[0]
    tm = _tile(m, (512, 256, 128))
    tn = _tile(n, (1024, 512, 256, 128))

    def body(*refs):
        a_ref, b_ref = refs[0], refs[1]
        o_ref = refs[-1]
        acc = lax.dot_general(a_ref[...], b_ref[...], (((1,), (1,)), ((), ())), preferred_element_type=F32)
        if relu2_act is not None:
            acc = acc * (2.0 * jnp.sqrt(refs[2][...].astype(F32)))
        o_ref[...] = acc.astype(o_ref.dtype)

    in_specs = [pl.BlockSpec((tm, c), lambda j, i: (i, 0)), pl.BlockSpec((tn, c), lambda j, i: (j, 0))]
    args = [a, b]
    if relu2_act is not None:
        in_specs.append(pl.BlockSpec((tm, tn), lambda j, i: (i, j)))
        args.append(relu2_act)
    return pl.pallas_call(
        body, name=name, grid=(n // tn, m // tm), in_specs=in_specs,
        out_specs=pl.BlockSpec((tm, tn), lambda j, i: (i, j)), out_shape=_sds((m, n), out_dtype),
        compiler_params=_cp("parallel", "parallel"))(*args)


def _mm_tn(a, b, *, name, packed=None):
    m, ka = a.shape
    nb = b.shape[1]
    tka = _tile(ka, (512, 256, 128)) if packed is None else packed[1]
    tnb = _tile(nb, (1024, 768, 512)) if nb != PROJ_W else nb
    tc = _tile(m, (1024, 512, 256, 128))

    def body(*refs):
        a_ref, b_ref, o_ref = refs[0], refs[1], refs[-1]
        part = lax.dot_general(a_ref[...], b_ref[...], (((0,), (0,)), ((), ())), preferred_element_type=F32)

        @pl.when(pl.program_id(2) == 0)
        def _():
            o_ref[...] = part

        @pl.when(pl.program_id(2) != 0)
        def _():
            o_ref[...] += part

    in_specs = [pl.BlockSpec((tc, tka), lambda i, j, c: (c, i)), pl.BlockSpec((tc, tnb), lambda i, j, c: (c, j))]
    kwargs = dict(out_specs=pl.BlockSpec((tka, tnb), lambda i, j, c: (i, j)), out_shape=_sds((ka, nb), F32))
    args = [a, b]
    if packed is not None:
        buf, _, place = packed
        assert tnb == PACK_C
        kwargs = dict(out_specs=pl.BlockSpec((None, tka, tnb), lambda i, j, c: place(i, j) + (0,)))
        if isinstance(buf, jax.ShapeDtypeStruct):
            kwargs["out_shape"] = buf
        else:
            kwargs.update(out_shape=_sds(buf.shape, buf.dtype), input_output_aliases={2: 0})
            in_specs.append(pl.BlockSpec(memory_space=pl.ANY))
            args.append(buf)
    return pl.pallas_call(
        body, name=name, grid=(ka // tka, nb // tnb, m // tc), in_specs=in_specs,
        compiler_params=_cp("parallel", "parallel", "arbitrary"), **kwargs)(*args)


def _rstd(x):
    return lax.rsqrt(jnp.mean(x * x, axis=-1, keepdims=True) + NORM_EPS)


def _rms_bwd_rows(x, g, dy):
    r = _rstd(x)
    gy = dy * g
    c = jnp.sum(x * gy, axis=-1, keepdims=True) * (r * r * r) * (1.0 / x.shape[-1])
    return r * gy - x * c, jnp.sum(dy * x * r, axis=0, keepdims=True)


def _accum(ref, part, first):
    @pl.when(first)
    def _():
        ref[...] = part

    @pl.when(jnp.logical_not(first))
    def _():
        ref[...] += part


def _rope(x, c, s1, s2, sh):
    return x * c + pltpu.roll(x, LANE - sh, 1) * s1 + pltpu.roll(x, sh, 1) * s2


def _rope_t(g, c, s1, s2, sh):
    return g * c + pltpu.roll(g * s1, sh, 1) + pltpu.roll(g * s2, LANE - sh, 1)


def _rope_tables(s, half, reps):
    pos = jnp.arange(s, dtype=F32)
    inv_freq = ROPE_THETA ** (-jnp.arange(half, dtype=F32) / half)
    ang = pos[:, None] * inv_freq[None, :]
    cos, sin = jnp.cos(ang), jnp.sin(ang)
    zero = jnp.zeros_like(cos)
    pad = jnp.zeros((s, LANE - 2 * half * reps), F32)
    c = jnp.concatenate([cos, cos] * reps + [pad], axis=1)
    s1 = jnp.concatenate([-sin, zero] * reps + [pad], axis=1)
    s2 = jnp.concatenate([zero, sin] * reps + [pad], axis=1)
    return c, s1, s2


def _lane_lt64(shape):
    return lax.broadcasted_iota(jnp.int32, shape, len(shape) - 1) % LANE < HEAD_DIM


def _group_sum(x):
    outs = []
    for b in range(x.shape[1] // LANE):
        blk = x[:, b * LANE:(b + 1) * LANE]
        lo = _lane_lt64(blk.shape)
        s0 = jnp.sum(jnp.where(lo, blk, 0.0), axis=1, keepdims=True)
        s1 = jnp.sum(jnp.where(lo, 0.0, blk), axis=1, keepdims=True)
        outs.append(jnp.where(lo, s0, s1))
    return outs


def _row_spec(ts, w):
    return pl.BlockSpec((ts, w), lambda i: (i, 0))


def _fix_spec(w):
    return pl.BlockSpec((1, w), lambda i: (0, 0))


def _rms_bwd(x, g, dy, res, *, name):
    s, d = x.shape
    ts = _tile(s, (512, 256, 128))

    def body(x_ref, g_ref, dy_ref, res_ref, dx_ref, dxb_ref, dg_ref):
        dx, dg = _rms_bwd_rows(x_ref[...], g_ref[...], dy_ref[...])
        dx = res_ref[...] + dx
        dx_ref[...] = dx
        dxb_ref[...] = dx.astype(BF16)
        _accum(dg_ref, dg, pl.program_id(0) == 0)

    return pl.pallas_call(
        body, name=name, grid=(s // ts,),
        in_specs=[_row_spec(ts, d), _fix_spec(d), _row_spec(ts, d), _row_spec(ts, d)],
        out_specs=(_row_spec(ts, d), _row_spec(ts, d), _fix_spec(d)),
        out_shape=(_sds((s, d), F32), _sds((s, d), BF16), _sds((1, d), F32)),
        compiler_params=_cp("arbitrary"))(x, g.reshape(1, d), dy, res)


def _loss_head(x, g, target):
    s, d = x.shape
    ts = _tile(s, (512, 256, 128))

    def body(x_ref, g_ref, t_ref, loss_ref, dx_ref, dxb_ref, dg_ref):
        xv, gv = x_ref[...], g_ref[...]
        err = xv * _rstd(xv) * gv - t_ref[...]
        part = 0.5 * jnp.sum(jnp.sum(err * err, axis=-1, keepdims=True) * (1.0 / d), axis=0, keepdims=True)
        dx, dg = _rms_bwd_rows(xv, gv, err * (1.0 / d))
        dx_ref[...] = dx
        dxb_ref[...] = dx.astype(BF16)
        first = pl.program_id(0) == 0
        _accum(dg_ref, dg, first)
        _accum(loss_ref, jnp.broadcast_to(part, (1, LANE)), first)

    return pl.pallas_call(
        body, name="loss_head", grid=(s // ts,), in_specs=[_row_spec(ts, d), _fix_spec(d), _row_spec(ts, d)],
        out_specs=(_fix_spec(LANE), _row_spec(ts, d), _row_spec(ts, d), _fix_spec(d)),
        out_shape=(_sds((1, LANE), F32), _sds((s, d), F32), _sds((s, d), BF16), _sds((1, d), F32)),
        compiler_params=_cp("arbitrary"))(x, g.reshape(1, d), target)


def _prep_fwd(proj, q_norm, kv_norm, t32, t64):
    s = proj.shape[0]
    ts = _tile(s, (256, 128))

    def body(p_ref, qn_ref, kn_ref, c32, a32, b32, c64, a64, b64, cqn_ref, ckvn_ref, kpe_ref, qkvb_ref, qkvc_ref):
        cq = p_ref[:, 0:Q_LORA]
        cqn_ref[...] = (cq * _rstd(cq) * qn_ref[...]).astype(BF16)
        ckv = p_ref[:, COL_CKV:COL_KPE]
        ckvn_ref[...] = (ckv * _rstd(ckv) * kn_ref[...]).astype(BF16)
        kp = p_ref[:, COL_KPE:COL_B]
        kp2 = kp + pltpu.roll(kp, QK_ROPE, 1)
        kpe_ref[...] = _rope(kp2, c32[...], a32[...], b32[...], QK_ROPE // 2).astype(BF16)
        for b in range(6):
            blk = _rope(p_ref[:, COL_B + b * LANE:COL_B + (b + 1) * LANE], c64[...], a64[...], b64[...], HEAD_DIM // 2)
            if b < 3:
                blk = blk * SCALE_BC
            qkvb_ref[:, b * LANE:(b + 1) * LANE] = blk.astype(BF16)
        qkvb_ref[:, 2 * WIDTH_AB:3 * WIDTH_AB] = p_ref[:, COL_B + 2 * WIDTH_AB:COL_C].astype(BF16)
        qkvc_ref[:, 0:WIDTH_C] = (p_ref[:, COL_C:COL_C + WIDTH_C] * SCALE_BC).astype(BF16)
        qkvc_ref[:, WIDTH_C:3 * WIDTH_C] = p_ref[:, COL_C + WIDTH_C:PROJ_W].astype(BF16)

    tab = [_row_spec(ts, LANE)] * 6
    return pl.pallas_call(
        body, name="prep_fwd", grid=(s // ts,),
        in_specs=[_row_spec(ts, PROJ_W), _fix_spec(Q_LORA), _fix_spec(KV_LORA)] + tab,
        out_specs=(_row_spec(ts, Q_LORA), _row_spec(ts, KV_LORA), _row_spec(ts, LANE), _row_spec(ts, 3 * WIDTH_AB),
                   _row_spec(ts, 3 * WIDTH_C)),
        out_shape=(_sds((s, Q_LORA), BF16), _sds((s, KV_LORA), BF16), _sds((s, LANE), BF16),
                   _sds((s, 3 * WIDTH_AB), BF16), _sds((s, 3 * WIDTH_C), BF16)),
        compiler_params=_cp("parallel"))(proj, q_norm.reshape(1, -1), kv_norm.reshape(1, -1), *t32, *t64)


def _prep_bwd(proj, q_norm, kv_norm, t32, t64, dcqn, dckvn, dkpe, db, dc):
    s = proj.shape[0]
    ts = _tile(s, (256, 128))

    def body(p_ref, qn_ref, kn_ref, c32, a32, b32, c64, a64, b64, dcqn_ref, dckvn_ref, dkpe_ref, *rest):
        db_refs, dc_refs = rest[0:9], rest[9:12]
        dp_ref, dqn_ref, dkn_ref = rest[12:15]
        first = pl.program_id(0) == 0
        dx, dg = _rms_bwd_rows(p_ref[:, 0:Q_LORA], qn_ref[...], dcqn_ref[...])
        dp_ref[:, 0:Q_LORA] = dx.astype(BF16)
        _accum(dqn_ref, dg, first)
        dx, dg = _rms_bwd_rows(p_ref[:, COL_CKV:COL_KPE], kn_ref[...], dckvn_ref[...])
        dp_ref[:, COL_CKV:COL_KPE] = dx.astype(BF16)
        _accum(dkn_ref, dg, first)
        g = _rope_t(dkpe_ref[...], c32[...], a32[...], b32[...], QK_ROPE // 2)
        g = g + pltpu.roll(g, LANE - QK_ROPE, 1)
        lane = lax.broadcasted_iota(jnp.int32, g.shape, 1)
        dp_ref[:, COL_KPE:COL_B] = jnp.where(lane < QK_ROPE, g, 0.0).astype(BF16)
        for which in range(3):
            for b in range(3):
                sl = slice(b * LANE, (b + 1) * LANE)
                g = db_refs[which][:, sl] + db_refs[3 + which][:, sl] + db_refs[6 + which][:, sl]
                if which < 2:
                    g = _rope_t(g, c64[...], a64[...], b64[...], HEAD_DIM // 2)
                if which == 0:
                    g = g * SCALE_BC
                col = COL_B + which * WIDTH_AB + b * LANE
                dp_ref[:, col:col + LANE] = g.astype(BF16)
        dp_ref[:, COL_C:COL_C + WIDTH_C] = (dc_refs[0][...] * SCALE_BC).astype(BF16)
        dp_ref[:, COL_C + WIDTH_C:COL_C + 2 * WIDTH_C] = dc_refs[1][...].astype(BF16)
        dp_ref[:, COL_C + 2 * WIDTH_C:PROJ_W] = dc_refs[2][...].astype(BF16)

    tab = [_row_spec(ts, LANE)] * 6
    in_specs = ([_row_spec(ts, PROJ_W), _fix_spec(Q_LORA), _fix_spec(KV_LORA)] + tab
                + [_row_spec(ts, Q_LORA), _row_spec(ts, KV_LORA), _row_spec(ts, LANE)]
                + [_row_spec(ts, WIDTH_AB)] * 9 + [_row_spec(ts, WIDTH_C)] * 3)
    return pl.pallas_call(
        body, name="prep_bwd", grid=(s // ts,), in_specs=in_specs,
        out_specs=(_row_spec(ts, PROJ_W), _fix_spec(Q_LORA), _fix_spec(KV_LORA)),
        out_shape=(_sds((s, PROJ_W), BF16), _sds((1, Q_LORA), F32), _sds((1, KV_LORA), F32)),
        compiler_params=_cp("arbitrary"))(proj, q_norm.reshape(1, -1), kv_norm.reshape(1, -1), *t32, *t64,
                                          dcqn, dckvn, dkpe, *db, *dc)


def _a_post_fwd(qa, kva, kpe, t32):
    s = qa.shape[0]
    ts = _tile(s, (512, 256, 128))

    def body(qa_ref, kva_ref, kpe_ref, c32, a32, b32, q_ref, k_ref, kt_ref, v_ref):
        kpe_t = kpe_ref[...].astype(F32).T.astype(BF16)
        for p in range(3):
            lo, hi = 2 * p * LANE, (2 * p + 1) * LANE
            q_ref[:, lo:hi] = qa_ref[:, lo:hi].astype(BF16)
            q_ref[:, hi:hi + LANE] = _rope(qa_ref[:, hi:hi + LANE], c32[...], a32[...], b32[...], QK_ROPE // 2).astype(BF16)
            knope = kva_ref[:, p * LANE:(p + 1) * LANE]
            k_ref[:, lo:hi] = knope.astype(BF16)
            k_ref[:, hi:hi + LANE] = kpe_ref[...]
            kt_ref[lo:hi, :] = knope.T.astype(BF16)
            kt_ref[hi:hi + LANE, :] = kpe_t
            v_ref[:, lo:hi] = kva_ref[:, WIDTH_AB + p * LANE:WIDTH_AB + (p + 1) * LANE].astype(BF16)
            v_ref[:, hi:hi + LANE] = jnp.ones((ts, LANE), BF16)

    return pl.pallas_call(
        body, name="a_post_fwd", grid=(s // ts,),
        in_specs=[_row_spec(ts, W_A2), _row_spec(ts, W_A2), _row_spec(ts, LANE)] + [_row_spec(ts, LANE)] * 3,
        out_specs=(_row_spec(ts, W_A2), _row_spec(ts, W_A2), pl.BlockSpec((W_A2, ts), lambda i: (0, i)), _row_spec(ts, W_A2)),
        out_shape=(_sds((s, W_A2), BF16), _sds((s, W_A2), BF16), _sds((W_A2, s), BF16), _sds((s, W_A2), BF16)),
        compiler_params=_cp("parallel"))(qa, kva, kpe, *t32)


def _a_post_bwd(dqa2_t, dka2, dva, t32):
    s = dka2.shape[0]
    ts = _tile(s, (512, 256, 128))

    def body(dqt_ref, dk_ref, dv_ref, c32, a32, b32, dqa_ref, dkva_ref, dkpe_ref):
        acc = None
        for p in range(3):
            lo, hi = 2 * p * LANE, (2 * p + 1) * LANE
            dqa_ref[:, lo:hi] = (dqt_ref[lo:hi, :].T * SCALE_A).astype(BF16)
            dqa_ref[:, hi:hi + LANE] = _rope_t(dqt_ref[hi:hi + LANE, :].T * SCALE_A, c32[...], a32[...], b32[...],
                                               QK_ROPE // 2).astype(BF16)
            dkva_ref[:, p * LANE:(p + 1) * LANE] = (dk_ref[:, lo:hi] * SCALE_A).astype(BF16)
            part = dk_ref[:, hi:hi + LANE]
            acc = part if acc is None else acc + part
        dkva_ref[:, WIDTH_AB:2 * WIDTH_AB] = dv_ref[...].astype(BF16)
        dkpe_ref[...] = acc * SCALE_A

    return pl.pallas_call(
        body, name="a_post_bwd", grid=(s // ts,),
        in_specs=[pl.BlockSpec((W_A2, ts), lambda i: (0, i)), _row_spec(ts, W_A2), _row_spec(ts, WIDTH_AB)]
        + [_row_spec(ts, LANE)] * 3,
        out_specs=(_row_spec(ts, W_A2), _row_spec(ts, W_A2), _row_spec(ts, LANE)),
        out_shape=(_sds((s, W_A2), BF16), _sds((s, W_A2), BF16), _sds((s, LANE), F32)),
        compiler_params=_cp("parallel"))(dqa2_t, dka2, dva, *t32)


def _pair_masks(width):
    lane = lax.broadcasted_iota(jnp.int32, (1, width), 1)
    m0 = lane < HEAD_DIM
    m1 = (lane >= HEAD_DIM) & (lane < LANE)
    if width == 2 * LANE:
        m0 = m0 | ((lane >= LANE) & (lane < LANE + QK_ROPE))
        m1 = m1 | ((lane >= LANE + QK_ROPE) & (lane < LANE + 2 * QK_ROPE))
    return m0, m1


def _nt(a, b):
    return lax.dot_general(a, b, (((1,), (1,)), ((), ())), preferred_element_type=F32)


def _tn(a, b):
    return lax.dot_general(a, b, (((0,), (0,)), ((), ())), preferred_element_type=F32)


def _stack_heads(x):
    m0, m1 = _pair_masks(LANE)
    zero = jnp.zeros_like(x)
    return jnp.concatenate([jnp.where(m0, x, zero), jnp.where(m1, x, zero)], axis=0)


def _stack_stat(x):
    return jnp.concatenate([x[:, 0:1], x[:, HEAD_DIM:HEAD_DIM + 1]], axis=0)


def _softmax_pair(q, kk, vv, bias2):
    t = q.shape[0]
    s = _nt(_stack_heads(q), kk) + bias2
    m = jnp.max(s, axis=1, keepdims=True)
    p = jnp.exp(s - m)
    l = jnp.sum(p, axis=1, keepdims=True)
    o2 = jnp.dot(p.astype(BF16), vv, preferred_element_type=F32) / l
    lse2 = m + jnp.log(l)
    lo = _lane_lt64((t, LANE))
    return jnp.where(lo, o2[:t], o2[t:]), jnp.where(lo, lse2[:t], lse2[t:])


def _softmax_pair_bwd(q, kk, vv, do, lse, delta, bias2):
    t = q.shape[0]
    q2, do2 = _stack_heads(q), _stack_heads(do)
    p = jnp.exp(_nt(q2, kk) + bias2 - _stack_stat(lse))
    ds = p * (_nt(do2, vv) - _stack_stat(delta))
    dsb = ds.astype(BF16)
    dq2 = jnp.dot(dsb, kk, preferred_element_type=F32)
    lo = _lane_lt64((t, LANE))
    return jnp.where(lo, dq2[:t], dq2[t:]), _tn(dsb, q2), _tn(p.astype(BF16), do2), ds


DENSE_FWD_TQ, DENSE_FWD_TK = 512, 8192
DENSE_BWD_TQ, DENSE_BWD_TK = 2048, 1024
LOG2E = math.log2(math.e)


def _dense_fwd(qa, ka, va1):
    s = qa.shape[0]
    tq, tk = min(DENSE_FWD_TQ, s), min(DENSE_FWD_TK, s)
    nk = s // tk
    c = SCALE_A * LOG2E

    def body(q_ref, k_ref, v_ref, o_ref, lse_ref, m_sc, acc_sc):
        j = pl.program_id(2)

        @pl.when(j == 0)
        def _():
            m_sc[...] = jnp.full(m_sc.shape, NEG_INF, F32)
            acc_sc[...] = jnp.zeros(acc_sc.shape, F32)

        q, kk, vv = q_ref[...], k_ref[...], v_ref[...]
        for hh, msk in enumerate(_pair_masks(2 * LANE)):
            sc = _nt(jnp.where(msk, q, jnp.zeros_like(q)), kk)
            m_prev = m_sc[hh]
            m_new = jnp.maximum(m_prev, jnp.max(sc, axis=1, keepdims=True))
            alpha = jnp.exp2((m_prev - m_new) * c)
            p = jnp.exp2((sc - m_new) * c)
            acc_sc[hh] = alpha * acc_sc[hh] + jnp.dot(p.astype(BF16), vv, preferred_element_type=F32)
            m_sc[hh] = m_new

        @pl.when(j == nk - 1)
        def _():
            lo = _lane_lt64((tq, LANE))
            a0, a1 = acc_sc[0], acc_sc[1]
            l0, l1 = a0[:, LANE:], a1[:, LANE:]
            o_ref[...] = jnp.where(lo, a0[:, :LANE] / l0, a1[:, :LANE] / l1)
            lse_ref[0] = _stat_rows(jnp.where(lo, m_sc[0] * SCALE_A + jnp.log(l0), m_sc[1] * SCALE_A + jnp.log(l1)))

    return pl.pallas_call(
        body, name="dense_fwd", grid=(3, s // tq, nk),
        in_specs=[pl.BlockSpec((tq, 2 * LANE), lambda p, i, j: (i, p)), pl.BlockSpec((tk, 2 * LANE), lambda p, i, j: (j, p)),
                  pl.BlockSpec((tk, 2 * LANE), lambda p, i, j: (j, p))],
        out_specs=(pl.BlockSpec((tq, LANE), lambda p, i, j: (i, p)), pl.BlockSpec((1, 8, tq), lambda p, i, j: (p, 0, i))),
        out_shape=(_sds((s, WIDTH_AB), F32), _sds((3, 8, s), F32)),
        scratch_shapes=[pltpu.VMEM((2, tq, 1), F32), pltpu.VMEM((2, tq, 2 * LANE), F32)],
        compiler_params=_cp("parallel", "parallel", "arbitrary"))(qa, ka, va1)


def _stat_rows(lane_dense):
    tr = lane_dense.T
    return jnp.concatenate([tr[0:1, :], tr[HEAD_DIM:HEAD_DIM + 1, :], jnp.zeros((6, tr.shape[1]), F32)], axis=0)


def _dense_bwd(qa, ka, kat, va1, do, lse_rows, delta_rows):
    s = qa.shape[0]
    tq, tk = min(DENSE_BWD_TQ, s), min(DENSE_BWD_TK, s)
    c = SCALE_A * LOG2E

    def body(q_ref, k_ref, kt_ref, v_ref, do_ref, lse_ref, dl_ref, dqt_ref, dk_ref, dv_ref):
        j, i = pl.program_id(1), pl.program_id(2)

        @pl.when((j == 0) & (i == 0))
        def _():
            dqt_ref[...] = jnp.zeros(dqt_ref.shape, F32)

        q, kk, kt, vv, do_ = q_ref[...], k_ref[...], kt_ref[...], v_ref[...], do_ref[...]
        lse_t, dl_t = lse_ref[0] * LOG2E, dl_ref[0]
        st = (lse_t[0:1, :], lse_t[1:2, :], dl_t[0:1, :], dl_t[1:2, :])
        row = lax.broadcasted_iota(jnp.int32, (2 * LANE, 1), 0)
        rmask = ((row < HEAD_DIM) | ((row >= LANE) & (row < LANE + QK_ROPE)),
                 ((row >= HEAD_DIM) & (row < LANE)) | ((row >= LANE + QK_ROPE) & (row < LANE + 2 * QK_ROPE)))
        vm = _pair_masks(LANE)
        dqt = dk = dv = None
        for hh, msk in enumerate(_pair_masks(2 * LANE)):
            qm = jnp.where(msk, q, jnp.zeros_like(q))
            dom = jnp.where(vm[hh], do_, jnp.zeros_like(do_))
            dpt = _nt(vv, dom)
            pt = jnp.exp2(_nt(kk, qm) * c - st[hh])
            dst = pt * (dpt - st[2 + hh])
            pb, dsb = pt.astype(BF16), dst.astype(BF16)
            dv_h = jnp.dot(pb, dom, preferred_element_type=F32)
            dk_h = jnp.dot(dsb, qm, preferred_element_type=F32)
            dqt_h = jnp.dot(jnp.where(rmask[hh], kt, jnp.zeros_like(kt)), dsb, preferred_element_type=F32)
            dqt = dqt_h if dqt is None else dqt + dqt_h
            dk = dk_h if dk is None else dk + dk_h
            dv = dv_h if dv is None else dv + dv_h
        cols = pl.ds(pl.multiple_of(i * tq, tq), tq)
        dqt_ref[:, cols] += dqt
        _accum(dk_ref, dk, i == 0)
        _accum(dv_ref, dv, i == 0)

    st_spec = pl.BlockSpec((1, 8, tq), lambda p, j, i: (p, 0, i))
    return pl.pallas_call(
        body, name="dense_bwd", grid=(3, s // tk, s // tq),
        in_specs=[pl.BlockSpec((tq, 2 * LANE), lambda p, j, i: (i, p)), pl.BlockSpec((tk, 2 * LANE), lambda p, j, i: (j, p)),
                  pl.BlockSpec((2 * LANE, tk), lambda p, j, i: (p, j)), pl.BlockSpec((tk, LANE), lambda p, j, i: (j, 2 * p)),
                  pl.BlockSpec((tq, LANE), lambda p, j, i: (i, p)), st_spec, st_spec],
        out_specs=(pl.BlockSpec((2 * LANE, s), lambda p, j, i: (p, 0)), pl.BlockSpec((tk, 2 * LANE), lambda p, j, i: (j, p)),
                   pl.BlockSpec((tk, LANE), lambda p, j, i: (j, p))),
        out_shape=(_sds((W_A2, s), F32), _sds((s, W_A2), F32), _sds((s, WIDTH_AB), F32)),
        compiler_params=_cp("parallel", "arbitrary", "arbitrary"))(qa, ka, kat, va1, do, lse_rows, delta_rows)


BAND_TILE = 1024
BAND_SUB = 128
QKV_W = 3 * WIDTH_AB


def _band_bias_table():
    row = np.arange(BAND_SUB)[:, None]
    col = np.arange(2 * BAND_SUB)[None, :]
    band = np.abs(row - col + BAND_HALF) <= BAND_HALF
    variants = []
    for idx in range(4):
        ok = band & ((col >= BAND_HALF) | ((idx & 1) == 0)) & ((col < 2 * BAND_SUB - BAND_HALF) | ((idx & 2) == 0))
        one = np.where(ok, 0.0, NEG_INF).astype(np.float32)
        variants.append(np.concatenate([one, one], axis=0))
    return jnp.asarray(np.stack(variants))


def _band_specs(t, n):
    hpt = t // BAND_HALF
    last = n // BAND_HALF - 1
    return [pl.BlockSpec((BAND_HALF, QKV_W), lambda r, i: (jnp.maximum(i * hpt - 1, 0), r)),
            pl.BlockSpec((t, QKV_W), lambda r, i: (i, r)),
            pl.BlockSpec((BAND_HALF, QKV_W), lambda r, i: (jnp.minimum((i + 1) * hpt, last), r)),
            pl.BlockSpec((4, 2 * BAND_SUB, 2 * BAND_SUB), lambda r, i: (0, 0, 0))]


def _band_bias(b_ref, a, nsub, i, nt):
    idx = 0
    if a == 0:
        idx = idx + (i == 0).astype(jnp.int32)
    if a == nsub - 1:
        idx = idx + 2 * (i == nt - 1).astype(jnp.int32)
    return b_ref[idx]


def _band_kv(left, main, right, p):
    kc = slice(WIDTH_AB + p * LANE, WIDTH_AB + (p + 1) * LANE)
    vc = slice(2 * WIDTH_AB + p * LANE, 2 * WIDTH_AB + (p + 1) * LANE)
    return (jnp.concatenate([left[:, kc], main[:, kc], right[:, kc]], axis=0),
            jnp.concatenate([left[:, vc], main[:, vc], right[:, vc]], axis=0))


def _banded_fwd(qkvb, dil, bias):
    s = qkvb.shape[0]
    n = s // dil
    t = min(n, BAND_TILE)
    nsub, nt = t // BAND_SUB, n // t
    view = qkvb.reshape(n, dil * QKV_W)

    def body(left, main, right, b_ref, o_ref, lse_ref):
        i = pl.program_id(1)
        for p in range(3):
            pc = slice(p * LANE, (p + 1) * LANE)
            kk, vv = _band_kv(left, main, right, p)
            for a in range(nsub):
                rows, win = slice(a * BAND_SUB, (a + 1) * BAND_SUB), slice(a * BAND_SUB, (a + 2) * BAND_SUB)
                o, lse = _softmax_pair(main[rows, pc], kk[win], vv[win], _band_bias(b_ref, a, nsub, i, nt))
                o_ref[rows, pc] = o
                lse_ref[rows, pc] = lse

    o_spec = pl.BlockSpec((t, WIDTH_AB), lambda r, i: (i, r))
    o, lse = pl.pallas_call(
        body, name=f"banded_fwd_d{dil}", grid=(dil, nt), in_specs=_band_specs(t, n), out_specs=(o_spec, o_spec),
        out_shape=(_sds((n, dil * WIDTH_AB), F32), _sds((n, dil * WIDTH_AB), F32)),
        compiler_params=_cp("parallel", "parallel"))(view, view, view, bias)
    return o.reshape(s, WIDTH_AB), lse.reshape(s, WIDTH_AB)


def _banded_bwd(qkvb, do, lse, delta, dil, bias):
    s = qkvb.shape[0]
    n = s // dil
    t = min(n, BAND_TILE)
    nsub, nt = t // BAND_SUB, n // t
    view = qkvb.reshape(n, dil * QKV_W)
    side = [a.reshape(n, dil * WIDTH_AB) for a in (do, lse, delta)]

    def body(left, main, right, b_ref, do_ref, lse_ref, dl_ref, dq_ref, dk_ref, dv_ref):
        i = pl.program_id(1)

        @pl.when(i == 0)
        def _():
            dk_ref[...] = jnp.zeros(dk_ref.shape, F32)
            dv_ref[...] = jnp.zeros(dv_ref.shape, F32)

        lrow = pl.multiple_of(jnp.maximum(i * t - BAND_HALF, 0), BAND_HALF)
        rrow = pl.multiple_of(jnp.minimum((i + 1) * t, n - BAND_HALF), BAND_HALF)
        mrow = pl.multiple_of(i * t, BAND_HALF)
        for p in range(3):
            pc = slice(p * LANE, (p + 1) * LANE)
            kk, vv = _band_kv(left, main, right, p)
            parts = []
            for a in range(nsub):
                rows, win = slice(a * BAND_SUB, (a + 1) * BAND_SUB), slice(a * BAND_SUB, (a + 2) * BAND_SUB)
                dq, dk, dv, _ = _softmax_pair_bwd(main[rows, pc], kk[win], vv[win], do_ref[rows, pc], lse_ref[rows, pc],
                                                  dl_ref[rows, pc], _band_bias(b_ref, a, nsub, i, nt))
                dq_ref[rows, pc] = dq
                parts.append((dk, dv))
            for which, ref in ((0, dk_ref), (1, dv_ref)):
                chunks = []
                for c in range(nsub + 1):
                    g = parts[c][which][:BAND_SUB] if c < nsub else None
                    if c >= 1:
                        h = parts[c - 1][which][BAND_SUB:]
                        g = h if g is None else g + h
                    chunks.append(g)
                mid = jnp.concatenate([chunks[0][BAND_HALF:]] + chunks[1:nsub] + [chunks[nsub][:BAND_HALF]], axis=0)
                ref[pl.ds(lrow, BAND_HALF), pc] += chunks[0][:BAND_HALF]
                ref[pl.ds(mrow, t), pc] += mid
                ref[pl.ds(rrow, BAND_HALF), pc] += chunks[nsub][BAND_HALF:]

    q_spec = pl.BlockSpec((t, WIDTH_AB), lambda r, i: (i, r))
    acc_spec = pl.BlockSpec((n, WIDTH_AB), lambda r, i: (0, r))
    shp = _sds((n, dil * WIDTH_AB), F32)
    outs = pl.pallas_call(
        body, name=f"banded_bwd_d{dil}", grid=(dil, nt), in_specs=_band_specs(t, n) + [q_spec, q_spec, q_spec],
        out_specs=(q_spec, acc_spec, acc_spec), out_shape=(shp, shp, shp),
        compiler_params=_cp("parallel", "arbitrary"))(view, view, view, bias, *side)
    return [a.reshape(s, WIDTH_AB) for a in outs]


def _merge_branches(outs, lses):
    s = outs[0].shape[0]
    ts = _tile(s, (512, 256, 128))

    def body(o1, o2, o3, l1, l2, l3, o_ref, lse_ref):
        a, b, c = l1[...], l2[...], l3[...]
        m = jnp.maximum(jnp.maximum(a, b), c)
        ea, eb, ec = jnp.exp(a - m), jnp.exp(b - m), jnp.exp(c - m)
        den = ea + eb + ec
        o_ref[...] = (o1[...] * ea + o2[...] * eb + o3[...] * ec) / den
        lse_ref[...] = m + jnp.log(den)

    sp = _row_spec(ts, WIDTH_AB)
    return pl.pallas_call(
        body, name="merge_branches", grid=(s // ts,), in_specs=[sp] * 6, out_specs=(sp, sp),
        out_shape=(_sds((s, WIDTH_AB), F32), _sds((s, WIDTH_AB), F32)), compiler_params=_cp("parallel"))(*outs, *lses)


def _na_geometry(s):
    rows = s // GRID_W
    assert rows >= 2 * NA_ROWS and rows % NA_ROWS == 0
    return rows, rows // NA_ROWS


def _na_row(n, i, rows):
    rq = n * NA_ROWS + i
    rs = jnp.clip(rq - NA_ROWS // 2, 0, rows - NA_ROWS)
    return pl.multiple_of(rs * GRID_W, GRID_W), rs - rq + NA_ROWS - 1


NA_KEYS = NA_ROWS * GRID_W


def _natten_fwd(qkvc, tfull):
    s = qkvc.shape[0]
    rows, nrb = _na_geometry(s)
    tq = NA_ROWS * GRID_W

    def body(q_ref, k_ref, v_ref, t_ref, o_ref, lse_ref):
        n = pl.program_id(1)
        for i in range(NA_ROWS):
            tok, base = _na_row(n, i, rows)
            kk, vv = k_ref[pl.ds(tok, NA_KEYS), :], v_ref[pl.ds(tok, NA_KEYS), :]
            sl = slice(i * GRID_W, (i + 1) * GRID_W)
            bias2 = jnp.concatenate([t_ref[0, base], t_ref[1, base]], axis=0)
            o, lse = _softmax_pair(q_ref[sl, :], kk, vv, bias2)
            o_ref[sl, :] = o
            lse_ref[sl, :] = lse

    o_spec = pl.BlockSpec((tq, LANE), lambda p, n: (n, p))
    return pl.pallas_call(
        body, name="natten_fwd", grid=(2, nrb),
        in_specs=[pl.BlockSpec((tq, LANE), lambda p, n: (n, p)), pl.BlockSpec((s, LANE), lambda p, n: (0, 2 + p)),
                  pl.BlockSpec((s, LANE), lambda p, n: (0, 4 + p)),
                  pl.BlockSpec((2, NA_ROWS, GRID_W, NA_KEYS), lambda p, n: (p, 0, 0, 0))],
        out_specs=(o_spec, o_spec), out_shape=(_sds((s, WIDTH_C), F32), _sds((s, WIDTH_C), F32)),
        compiler_params=_cp("parallel", "parallel"))(qkvc, qkvc, qkvc, tfull)


def _natten_bwd(qkvc, tfull, do, lse, delta):
    s = qkvc.shape[0]
    rows, nrb = _na_geometry(s)
    tq = NA_ROWS * GRID_W

    def body(q_ref, k_ref, v_ref, t_ref, do_ref, lse_ref, dl_ref, dq_ref, dk_ref, dv_ref, dt_ref):
        n = pl.program_id(1)

        @pl.when(n == 0)
        def _():
            dk_ref[...] = jnp.zeros(dk_ref.shape, F32)
            dv_ref[...] = jnp.zeros(dv_ref.shape, F32)
            dt_ref[...] = jnp.zeros(dt_ref.shape, F32)

        for i in range(NA_ROWS):
            tok, base = _na_row(n, i, rows)
            win = pl.ds(tok, NA_KEYS)
            sl = slice(i * GRID_W, (i + 1) * GRID_W)
            bias2 = jnp.concatenate([t_ref[0, base], t_ref[1, base]], axis=0)
            dq, dk, dv, ds = _softmax_pair_bwd(q_ref[sl, :], k_ref[win, :], v_ref[win, :], do_ref[sl, :], lse_ref[sl, :],
                                               dl_ref[sl, :], bias2)
            dq_ref[sl, :] = dq
            dk_ref[win, :] += dk
            dv_ref[win, :] += dv
            dt_ref[0, base] += ds[:GRID_W]
            dt_ref[1, base] += ds[GRID_W:]

    q_spec = pl.BlockSpec((tq, LANE), lambda p, n: (n, p))
    acc_spec = pl.BlockSpec((s, LANE), lambda p, n: (0, p))
    t_spec = pl.BlockSpec((2, NA_ROWS, GRID_W, NA_KEYS), lambda p, n: (p, 0, 0, 0))
    shp = _sds((s, WIDTH_C), F32)
    return pl.pallas_call(
        body, name="natten_bwd", grid=(2, nrb),
        in_specs=[q_spec, pl.BlockSpec((s, LANE), lambda p, n: (0, 2 + p)), pl.BlockSpec((s, LANE), lambda p, n: (0, 4 + p)),
                  t_spec, q_spec, q_spec, q_spec],
        out_specs=(q_spec, acc_spec, acc_spec, t_spec),
        out_shape=(shp, shp, shp, _sds((HEADS_C, NA_ROWS, GRID_W, NA_KEYS), F32)),
        compiler_params=_cp("parallel", "arbitrary"))(qkvc, qkvc, qkvc, tfull, do, lse, delta)


def _rpb_constants():
    p = np.arange(GRID_W)[:, None]
    qc = np.arange(GRID_W)[None, :]
    dc = np.clip(qc - p, -(NA_COLS - 1), NA_COLS - 1) + NA_COLS - 1
    onehot = (dc.reshape(1, -1) == np.arange(32)[:, None]).astype(np.float32)
    c_start = np.clip(p - NA_COLS // 2, 0, GRID_W - NA_COLS)
    col_ok = ((qc >= c_start) & (qc < c_start + NA_COLS)).reshape(1, -1).astype(np.float32)
    a = np.arange(16)[:, None]
    bj = np.arange(64)[None, :]
    row_sel = ((bj // 8 + bj % 8) == a).astype(np.float32)
    return jnp.asarray(onehot), jnp.asarray(col_ok), jnp.asarray(row_sel)


def _rpb_expand(rpb, onehot, col_ok):
    r2 = jnp.pad(rpb.reshape(HEADS_C * 15, 31), ((0, 4), (0, 1)))

    def body(r_ref, oh_ref, ok_ref, o_ref):
        t = jnp.dot(r_ref[...], oh_ref[...], preferred_element_type=F32, precision=lax.Precision.HIGHEST)
        o_ref[...] = jnp.where(ok_ref[...] > 0.5, t, NEG_INF)

    tm = pl.pallas_call(body, name="rpb_expand", out_shape=_sds((64, GRID_W * GRID_W), F32))(r2, onehot, col_ok)
    tm = tm[:HEADS_C * 15].reshape(HEADS_C, 15, GRID_W, GRID_W)
    tfull = jnp.stack([jnp.concatenate([tm[:, base + j] for j in range(NA_ROWS)], axis=-1) for base in range(NA_ROWS)], axis=1)
    return tfull


def _rpb_grad(dtfull, onehot, row_sel):
    g = dtfull.reshape(HEADS_C, NA_ROWS, GRID_W, NA_ROWS, GRID_W).transpose(0, 1, 3, 2, 4).reshape(HEADS_C, 64, GRID_W * GRID_W)

    def body(g_ref, oh_ref, sel_ref, o_ref):
        for h in range(HEADS_C):
            mid = lax.dot_general(g_ref[h], oh_ref[...], (((1,), (1,)), ((), ())), preferred_element_type=F32,
                                  precision=lax.Precision.HIGHEST)
            o_ref[h] = jnp.dot(sel_ref[...], mid, preferred_element_type=F32, precision=lax.Precision.HIGHEST)

    out = pl.pallas_call(body, name="rpb_grad", out_shape=_sds((HEADS_C, 16, 32), F32))(g, onehot, row_sel)
    return out[:, :15, :31]


def _outnorm_fwd(o_a, o_b, o_c, ga, gb, gc):
    s = o_a.shape[0]
    ts = _tile(s, (512, 256, 128))

    def body(a_ref, b_ref, c_ref, ga_ref, gb_ref, gc_ref, o_ref):
        col = 0
        for ref, g in ((a_ref, ga_ref), (b_ref, gb_ref), (c_ref, gc_ref)):
            x = ref[...]
            o_ref[:, col:col + x.shape[1]] = (x * _rstd(x) * g[...]).astype(BF16)
            col += x.shape[1]

    return pl.pallas_call(
        body, name="outnorm_fwd", grid=(s // ts,),
        in_specs=[_row_spec(ts, WIDTH_AB), _row_spec(ts, WIDTH_AB), _row_spec(ts, WIDTH_C), _fix_spec(WIDTH_AB),
                  _fix_spec(WIDTH_AB), _fix_spec(WIDTH_C)],
        out_specs=_row_spec(ts, D_MODEL), out_shape=_sds((s, D_MODEL), BF16),
        compiler_params=_cp("parallel"))(o_a, o_b, o_c, ga.reshape(1, -1), gb.reshape(1, -1), gc.reshape(1, -1))


def _outnorm_bwd(dmixed, o_a, o_b, o_c, ga, gb, gc):
    s = o_a.shape[0]
    ts = _tile(s, (512, 256, 128))

    def body(dm_ref, a_ref, b_ref, c_ref, ga_ref, gb_ref, gc_ref, *outs):
        first = pl.program_id(0) == 0
        col = 0
        for k, (ref, g) in enumerate(((a_ref, ga_ref), (b_ref, gb_ref), (c_ref, gc_ref))):
            x = ref[...]
            w = x.shape[1]
            dx, dg = _rms_bwd_rows(x, g[...], dm_ref[:, col:col + w])
            col += w
            outs[k][...] = dx.astype(BF16)
            for b, blk in enumerate(_group_sum(dx * x)):
                if k == 0:
                    outs[3][b] = _stat_rows(blk)
                else:
                    outs[3 + k][:, b * LANE:(b + 1) * LANE] = blk
            _accum(outs[6 + k], dg, first)

    widths = (WIDTH_AB, WIDTH_AB, WIDTH_C)
    return pl.pallas_call(
        body, name="outnorm_bwd", grid=(s // ts,),
        in_specs=[_row_spec(ts, D_MODEL)] + [_row_spec(ts, w) for w in widths] + [_fix_spec(w) for w in widths],
        out_specs=tuple([_row_spec(ts, w) for w in widths] + [pl.BlockSpec((3, 8, ts), lambda i: (0, 0, i))]
                        + [_row_spec(ts, w) for w in widths[1:]] + [_fix_spec(w) for w in widths]),
        out_shape=tuple([_sds((s, w), BF16) for w in widths] + [_sds((3, 8, s), F32)]
                        + [_sds((s, w), F32) for w in widths[1:]] + [_sds((1, w), F32) for w in widths]),
        compiler_params=_cp("arbitrary"))(dmixed, o_a, o_b, o_c, ga.reshape(1, -1), gb.reshape(1, -1), gc.reshape(1, -1))


def _adamw(w, g, m, v, *, name):
    r, c = w.shape
    tr = _tile(r, (512, 256, 128, 64, 8))

    def body(w_ref, g_ref, m_ref, v_ref, d_ref, nm_ref, nv_ref):
        gv = g_ref[...]
        nm = ADAM_B1 * m_ref[...] + (1.0 - ADAM_B1) * gv
        nv = ADAM_B2 * v_ref[...] + (1.0 - ADAM_B2) * jnp.square(gv)
        m_hat = nm / (1.0 - ADAM_B1 ** ADAM_STEP)
        v_hat = nv / (1.0 - ADAM_B2 ** ADAM_STEP)
        d_ref[...] = -ADAM_LR * (m_hat / (jnp.sqrt(v_hat) + ADAM_EPS) + ADAM_WD * w_ref[...])
        nm_ref[...] = nm
        nv_ref[...] = nv

    sp = _row_spec(tr, c)
    return pl.pallas_call(
        body, name=name, grid=(r // tr,), in_specs=[sp] * 4, out_specs=(sp, sp, sp),
        out_shape=(_sds((r, c), F32),) * 3, compiler_params=_cp("parallel"))(w, g, m, v)


def _add_n(parts, *, name, out_dtype):
    r, c = parts[0].shape
    tr = max(t for t in range(16, 1025, 16) if r % t == 0)

    def body(*refs):
        acc = refs[0][...].astype(F32)
        for ref in refs[1:-1]:
            acc = acc + ref[...].astype(F32)
        refs[-1][...] = acc.astype(out_dtype)

    sp = _row_spec(tr, c)
    return pl.pallas_call(
        body, name=name, grid=(r // tr,), in_specs=[sp] * len(parts), out_specs=sp, out_shape=_sds((r, c), out_dtype),
        compiler_params=_cp("parallel"))(*parts)


ANY = pl.BlockSpec(memory_space=pl.ANY)
CHIP_FLIPS = ((1, 0), (0, 1), (1, 1))


def _me():
    return lax.axis_index("x"), lax.axis_index("y"), lax.axis_index("c")


def _gather_chips(half):
    def body(src, out, send_sems, recv_sems):
        x, y, c = _me()
        mine = 2 * x + y

        def copy(k, chip, half_idx, to, source=None):
            dst = out.at[chip, half_idx]
            return pltpu.make_async_remote_copy(src_ref=dst if source is None else source, dst_ref=dst,
                                                send_sem=send_sems.at[k], recv_sem=recv_sems.at[k], device_id=to,
                                                device_id_type=MESH_T)

        chips = [(x ^ fx, y ^ fy) for fx, fy in CHIP_FLIPS]
        first = [copy(k, mine, c, (cx, cy, c), source=src) for k, (cx, cy) in enumerate(chips)]
        for cp in first:
            cp.start()
        passed = []
        for k, (cx, cy) in enumerate(chips):
            theirs = 2 * cx + cy
            copy(k, theirs, c, (x, y, c)).wait_recv()
            cp = copy(3 + k, theirs, c, (x, y, 1 - c))
            cp.start()
            passed.append(cp)
        for k, (cx, cy) in enumerate(chips):
            copy(3 + k, 2 * cx + cy, 1 - c, (x, y, c)).wait_recv()
        for cp in first + passed:
            cp.wait_send()

    return pl.pallas_call(
        body, name="gather_chips", in_specs=[ANY], out_specs=ANY, out_shape=_sds((4, 2) + half.shape, half.dtype),
        scratch_shapes=[pltpu.SemaphoreType.DMA((6,)), pltpu.SemaphoreType.DMA((6,))])(half)


def _swap_sibling(block):
    def body(src, out, send_sem, recv_sem):
        x, y, c = _me()
        cp = pltpu.make_async_remote_copy(src_ref=src, dst_ref=out, send_sem=send_sem, recv_sem=recv_sem,
                                          device_id=(x, y, 1 - c), device_id_type=MESH_T)
        cp.start()
        cp.wait()

    return pl.pallas_call(
        body, name="swap_sibling", in_specs=[ANY], out_specs=ANY, out_shape=_sds(block.shape, block.dtype),
        scratch_shapes=[pltpu.SemaphoreType.DMA(()), pltpu.SemaphoreType.DMA(())])(block)


def _scatter_chips(parts):
    def body(src, out, send_sems, recv_sems):
        x, y, c = _me()
        mine = 2 * x + y
        sends = []
        for k, (fx, fy) in enumerate(CHIP_FLIPS):
            theirs = 2 * (x ^ fx) + (y ^ fy)
            cp = pltpu.make_async_remote_copy(src_ref=src.at[theirs], dst_ref=out.at[mine], send_sem=send_sems.at[k],
                                              recv_sem=recv_sems.at[k], device_id=(x ^ fx, y ^ fy, c), device_id_type=MESH_T)
            cp.start()
            sends.append(cp)
        for k, (fx, fy) in enumerate(CHIP_FLIPS):
            theirs = 2 * (x ^ fx) + (y ^ fy)
            pltpu.make_async_remote_copy(src_ref=src.at[theirs], dst_ref=out.at[theirs], send_sem=send_sems.at[k],
                                         recv_sem=recv_sems.at[k], device_id=(x ^ fx, y ^ fy, c),
                                         device_id_type=MESH_T).wait_recv()
        for cp in sends:
            cp.wait_send()

    return pl.pallas_call(
        body, name="scatter_chips", in_specs=[ANY], out_specs=ANY, out_shape=_sds(parts.shape, parts.dtype),
        scratch_shapes=[pltpu.SemaphoreType.DMA((3,)), pltpu.SemaphoreType.DMA((3,))])(parts)


def _all_reduce_small(block):
    r, c = block.shape

    def body(src, out, slots, send_sems, recv_sems):
        x, y, cc = _me()
        mine = 4 * x + 2 * y + cc
        slots[mine] = src[...]
        sends = []
        for k in range(1, 8):
            fx, fy, fc = (k >> 2) & 1, (k >> 1) & 1, k & 1
            cp = pltpu.make_async_remote_copy(src_ref=src, dst_ref=slots.at[mine], send_sem=send_sems.at[k - 1],
                                              recv_sem=recv_sems.at[k - 1], device_id=(x ^ fx, y ^ fy, cc ^ fc),
                                              device_id_type=MESH_T)
            cp.start()
            sends.append(cp)
        for k in range(1, 8):
            fx, fy, fc = (k >> 2) & 1, (k >> 1) & 1, k & 1
            theirs = 4 * (x ^ fx) + 2 * (y ^ fy) + (cc ^ fc)
            pltpu.make_async_remote_copy(src_ref=src, dst_ref=slots.at[theirs], send_sem=send_sems.at[k - 1],
                                         recv_sem=recv_sems.at[k - 1], device_id=(x ^ fx, y ^ fy, cc ^ fc),
                                         device_id_type=MESH_T).wait_recv()
        for cp in sends:
            cp.wait_send()
        acc = slots[0]
        for d in range(1, 8):
            acc = acc + slots[d]
        out[...] = acc

    vm = pl.BlockSpec(memory_space=pltpu.VMEM)
    return pl.pallas_call(
        body, name="all_reduce_small", in_specs=[vm], out_specs=vm, out_shape=_sds((r, c), F32),
        scratch_shapes=[pltpu.VMEM((8, r, c), F32), pltpu.SemaphoreType.DMA((7,)), pltpu.SemaphoreType.DMA((7,))])(block)


DIRECT = ("w_mlp_in", "w_mlp_out", "w_out")
BIG = DIRECT + ("w_in", "w_uq", "w_ukv")
COL_SHARDED = {"w_in": True, "w_uq": True, "w_ukv": True, "w_out": False, "w_mlp_in": True, "w_mlp_out": False}
SMALL = ("g_mix", "q_norm", "kv_norm", "rpb", "out_norm_a", "out_norm_b", "out_norm_c", "g_mlp", "g_final")
PACK_C = 1024
ROW_ALIGN = 32


def _pack_rows(parts):
    flat = jnp.concatenate([p.reshape(-1, PACK_C) for p in parts], axis=0)
    return jnp.pad(flat, ((0, -flat.shape[0] % ROW_ALIGN), (0, 0)))


def _unpack_rows(flat, shapes):
    out, row = [], 0
    for shp in shapes:
        n = int(np.prod(shp)) // PACK_C
        out.append(flat[row:row + n].reshape(shp))
        row += n
    return out


def _full_from_shards(name, g):
    if COL_SHARDED[name]:
        return g.transpose(1, 2, 0, 3).reshape(g.shape[1], g.shape[2], 4 * g.shape[3])
    return g.transpose(1, 0, 2, 3).reshape(g.shape[1], 4 * g.shape[2], g.shape[3])


def _shards_from_full(name, w):
    l, k, n = w.shape
    if COL_SHARDED[name]:
        return w.reshape(l, k, 4, n // 4).transpose(2, 0, 1, 3)
    return w.reshape(l, 4, k // 4, n).transpose(1, 0, 2, 3)


def _arrange_w_in(w):
    z = jnp.zeros(w.shape[:-1] + (COL_B - COL_KPE - QK_ROPE,), w.dtype)
    return jnp.concatenate([w[..., :COL_KPE + QK_ROPE], z, w[..., COL_KPE + QK_ROPE:]], axis=-1)


def _unarrange_w_in(w):
    return jnp.concatenate([w[..., :COL_KPE + QK_ROPE], w[..., COL_B:]], axis=-1)


def _arrange_w_uq(w):
    per = HEAD_DIM + QK_ROPE
    z = jnp.zeros(w.shape[:-1] + (HEAD_DIM,), w.dtype)
    cols = []
    for p in range(3):
        a, b = 2 * p * per, (2 * p + 1) * per
        cols += [w[..., a:a + HEAD_DIM], w[..., b:b + HEAD_DIM], w[..., a + HEAD_DIM:a + per], w[..., b + HEAD_DIM:b + per], z]
    return jnp.concatenate(cols, axis=-1)


def _unarrange_w_uq(w):
    cols = []
    for h in range(HEADS_A):
        p, e = divmod(h, 2)
        base = 2 * p * LANE
        cols += [w[..., base + e * HEAD_DIM:base + (e + 1) * HEAD_DIM],
                 w[..., base + LANE + e * QK_ROPE:base + LANE + (e + 1) * QK_ROPE]]
    return jnp.concatenate(cols, axis=-1)


def _arrange_w_ukv(w):
    ks = [w[..., h * LANE:h * LANE + HEAD_DIM] for h in range(HEADS_A)]
    vs = [w[..., h * LANE + HEAD_DIM:(h + 1) * LANE] for h in range(HEADS_A)]
    return jnp.concatenate(ks + vs, axis=-1)


def _unarrange_w_ukv(w):
    cols = []
    for h in range(HEADS_A):
        cols += [w[..., h * HEAD_DIM:(h + 1) * HEAD_DIM], w[..., WIDTH_AB + h * HEAD_DIM:WIDTH_AB + (h + 1) * HEAD_DIM]]
    return jnp.concatenate(cols, axis=-1)


def _layer_fwd(x, w, sm, tabs, consts):
    t32, t64 = tabs
    onehot, col_ok, _, band = consts
    h, proj = _norm_mm(x, sm["g_mix"], w["w_in"], name="in_proj", relu2=False)
    cqn, ckvn, kpe, qkvb, qkvc = _prep_fwd(proj, sm["q_norm"], sm["kv_norm"], t32, t64)
    qa = _mm_nn(cqn, w["w_uq"], name="q_up")
    kva = _mm_nn(ckvn, w["w_ukv"], name="kv_up")
    qa2, ka2, kat, va1 = _a_post_fwd(qa, kva, kpe, t32)
    o_a, lse_a = _dense_fwd(qa2, ka2, va1)
    branch = [_banded_fwd(qkvb, dil, band) for _, dil in DILATED_PAIRS]
    o_b, lse_b = _merge_branches([b[0] for b in branch], [b[1] for b in branch])
    tfull = _rpb_expand(sm["rpb"], onehot, col_ok)
    o_c, lse_c = _natten_fwd(qkvc, tfull)
    mixed = _outnorm_fwd(o_a, o_b, o_c, sm["out_norm_a"], sm["out_norm_b"], sm["out_norm_c"])
    x_mid = _mm_nn(mixed, w["w_out"], name="out_proj", res=x)
    h2, act = _norm_mm(x_mid, sm["g_mlp"], w["w_mlp_in"], name="mlp_in", relu2=True)
    x_out = _mm_nn(act, w["w_mlp_out"], name="mlp_out", res=x_mid)
    saved = dict(x=x, h=h, proj=proj, cqn=cqn, ckvn=ckvn, qkvb=qkvb, qkvc=qkvc, qa2=qa2, ka2=ka2, kat=kat, va1=va1, o_a=o_a,
                 lse_a=lse_a, o_b=o_b, lse_b=lse_b, o_c=o_c, lse_c=lse_c, tfull=tfull, mixed=mixed, x_mid=x_mid, h2=h2,
                 act=act)
    return x_out, saved


def _layer_bwd(dx, dxb, sv, w, sm, tabs, consts, packed, places):
    t32, t64 = tabs
    onehot, _, row_sel, band = consts
    g = {}
    du = _mm_nt(dxb, w["w_mlp_out"], name="mlp_out_dx", out_dtype=BF16, relu2_act=sv["act"])
    packed = _mm_tn(sv["act"], dxb, name="mlp_out_dw", packed=(packed,) + places["w_mlp_out"])
    dh2 = _mm_nt(du, w["w_mlp_in"], name="mlp_in_dx")
    packed = _mm_tn(sv["h2"], du, name="mlp_in_dw", packed=(packed,) + places["w_mlp_in"])
    dx_mid, dmb, g["g_mlp"] = _rms_bwd(sv["x_mid"], sm["g_mlp"], dh2, dx, name="norm_mlp_bwd")
    dmixed = _mm_nt(dmb, w["w_out"], name="out_proj_dx")
    packed = _mm_tn(sv["mixed"], dmb, name="out_proj_dw", packed=(packed,) + places["w_out"])
    (do_a, do_b, do_c, dl_a, dl_b, dl_c, g["out_norm_a"], g["out_norm_b"], g["out_norm_c"]) = _outnorm_bwd(
        dmixed, sv["o_a"], sv["o_b"], sv["o_c"], sm["out_norm_a"], sm["out_norm_b"], sm["out_norm_c"])
    dqa2_t, dka2, dva = _dense_bwd(sv["qa2"], sv["ka2"], sv["kat"], sv["va1"], do_a, sv["lse_a"], dl_a)
    db = []
    for _, dil in DILATED_PAIRS:
        db += _banded_bwd(sv["qkvb"], do_b, sv["lse_b"], dl_b, dil, band)
    dq_c, dk_c, dv_c, dtfull = _natten_bwd(sv["qkvc"], sv["tfull"], do_c, sv["lse_c"], dl_c)
    g["rpb"] = _rpb_grad(dtfull, onehot, row_sel)
    dqa, dkva, dkpe = _a_post_bwd(dqa2_t, dka2, dva, t32)
    dcqn = _mm_nt(dqa, w["w_uq"], name="q_up_dx")
    g["w_uq"] = _unarrange_w_uq(_mm_tn(sv["cqn"], dqa, name="q_up_dw"))
    dckvn = _mm_nt(dkva, w["w_ukv"], name="kv_up_dx")
    g["w_ukv"] = _unarrange_w_ukv(_mm_tn(sv["ckvn"], dkva, name="kv_up_dw"))
    dproj, g["q_norm"], g["kv_norm"] = _prep_bwd(sv["proj"], sm["q_norm"], sm["kv_norm"], t32, t64, dcqn, dckvn, dkpe,
                                                  db, (dq_c, dk_c, dv_c))
    dh = _mm_nt(dproj, w["w_in"], name="in_proj_dx")
    g["w_in"] = _unarrange_w_in(_mm_tn(sv["h"], dproj, name="in_proj_dw"))
    dx_in, dxb_in, g["g_mix"] = _rms_bwd(sv["x"], sm["g_mix"], dh, dx_mid, name="norm_mix_bwd")
    return dx_in, dxb_in, g, packed


def _packed_places(offs, l):
    d, r = D_MODEL, D_MODEL // 4
    return {"w_mlp_in": (512, lambda i, j: (j, (offs["w_mlp_in"] + l * d) // 512 + i)),
            "w_mlp_out": (512, lambda i, j: (i // 2, (offs["w_mlp_out"] + l * d) // 512 + i % 2)),
            "w_out": (r, lambda i, j: (i, (offs["w_out"] + l * r) // r))}


def _local_step(x, target, wfull, small, packed_shape, offs):
    s = x.shape[0]
    tabs = (_rope_tables(s, QK_ROPE // 2, 2), _rope_tables(s, HEAD_DIM // 2, 2))
    consts = _rpb_constants() + (_band_bias_table(),)
    saved = []
    for l in range(DEPTH):
        wl = {k: v[l] for k, v in wfull.items()}
        sl = {k: small[k][l] for k in SMALL if k != "g_final"}
        x, sv = _layer_fwd(x, wl, sl, tabs, consts)
        saved.append(sv)
    loss, dx, dxb, dg_final = _loss_head(x, small["g_final"], target)
    grads = [None] * DEPTH
    packed = packed_shape
    for l in reversed(range(DEPTH)):
        wl = {k: v[l] for k, v in wfull.items()}
        sl = {k: small[k][l] for k in SMALL if k != "g_final"}
        dx, dxb, grads[l], packed = _layer_bwd(dx, dxb, saved[l], wl, sl, tabs, consts, packed, _packed_places(offs, l))
    return loss, dx, grads, dg_final, packed


ARRANGE = {"w_in": _arrange_w_in, "w_uq": _arrange_w_uq, "w_ukv": _arrange_w_ukv}


def kernel(x, g_mix, w_in, q_norm, w_uq, kv_norm, w_ukv, rpb, out_norm_a, out_norm_b, out_norm_c, w_out, g_mlp, w_mlp_in, w_mlp_out, g_final, loss_target, m_g_mix, m_w_in, m_q_norm, m_w_uq, m_kv_norm, m_w_ukv, m_rpb, m_out_norm_a, m_out_norm_b, m_out_norm_c, m_w_out, m_g_mlp, m_w_mlp_in, m_w_mlp_out, m_g_final, v_g_mix, v_w_in, v_q_norm, v_w_uq, v_kv_norm, v_w_ukv, v_rpb, v_out_norm_a, v_out_norm_b, v_out_norm_c, v_w_out, v_g_mlp, v_w_mlp_in, v_w_mlp_out, v_g_final):
    args = dict(locals())
    weights = {k: args[k] for k in BIG + SMALL}
    moms = {k: args["m_" + k] for k in BIG + SMALL}
    vels = {k: args["v_" + k] for k in BIG + SMALL}
    cc = lax.axis_index("c")
    my_chip = 2 * lax.axis_index("x") + lax.axis_index("y")

    shard_shapes = [weights[k].shape for k in BIG]
    packed_w = _pack_rows([weights[k].astype(BF16) for k in BIG])
    rows = packed_w.shape[0]
    my_half = lax.dynamic_index_in_dim(packed_w.reshape(2, rows // 2, PACK_C), cc, axis=0, keepdims=False)
    gathered = _gather_chips(my_half).reshape(4, rows, PACK_C)
    per_chip = [_unpack_rows(jnp.where(my_chip == j, packed_w, gathered[j]), shard_shapes) for j in range(4)]
    wfull = {}
    for idx, k in enumerate(BIG):
        full = _full_from_shards(k, jnp.stack([per_chip[j][idx] for j in range(4)]))
        wfull[k] = ARRANGE[k](full) if k in ARRANGE else full

    small = {k: weights[k] for k in SMALL}
    offs, row = {}, 0
    for k, shp in zip(BIG, shard_shapes):
        offs[k] = row
        row += int(np.prod(shp)) // PACK_C
    loss, dx, grads, dg_final, packed = _local_step(x[0], loss_target[0], wfull, small, _sds((4, rows, PACK_C), F32), offs)

    small_local = {k: jnp.stack([grads[l][k].reshape(weights[k].shape[1:]) for l in range(DEPTH)])
                   for k in SMALL if k != "g_final"}
    small_local["g_final"] = dg_final.reshape(-1)
    small_shapes = [weights[k].shape for k in SMALL]
    n_small = sum(int(np.prod(s)) for s in small_shapes)
    flat = jnp.concatenate([small_local[k].reshape(-1) for k in SMALL] + [loss[0, :1]])
    rows_small = -(-(n_small + 1) // PACK_C)
    rows_small += -rows_small % 8
    flat = jnp.pad(flat, (0, rows_small * PACK_C - n_small - 1)).reshape(rows_small, PACK_C)
    red = _all_reduce_small(flat).reshape(-1)
    loss_out = red[n_small]
    small_grads, off = {}, 0
    for k, shp in zip(SMALL, small_shapes):
        n = int(np.prod(shp))
        small_grads[k] = red[off:off + n].reshape(shp)
        off += n

    rest = [k for k in BIG if k not in DIRECT]
    by_shard = {k: _shards_from_full(k, jnp.stack([grads[l][k] for l in range(DEPTH)])) for k in rest}
    tail = jnp.stack([_pack_rows([by_shard[k][j] for k in rest]) for j in range(4)])
    assert offs[rest[0]] + tail.shape[1] == rows
    packed = lax.dynamic_update_slice(packed, tail, (0, offs[rest[0]], 0))
    halves = packed.reshape(4, 2, rows // 2, PACK_C)
    mine = lax.dynamic_index_in_dim(halves, cc, axis=1, keepdims=False)
    other = lax.dynamic_index_in_dim(halves, 1 - cc, axis=1, keepdims=False)
    from_sibling = _swap_sibling(other)
    pair = _add_n([mine.reshape(-1, PACK_C), from_sibling.reshape(-1, PACK_C)], name="pair_sum",
                  out_dtype=BF16).reshape(mine.shape)
    by_chip = _scatter_chips(pair)
    reduced = _add_n([jnp.where(my_chip == j, pair[j], by_chip[j]) for j in range(4)], name="chip_sum", out_dtype=F32)
    theirs = _swap_sibling(reduced)
    joined = jnp.where(cc == 0, jnp.concatenate([reduced, theirs]), jnp.concatenate([theirs, reduced]))
    big_grads = dict(zip(BIG, _unpack_rows(joined, shard_shapes)))

    out_g, out_d, out_m, out_v = {}, {}, {}, {}
    for k in BIG:
        shp = weights[k].shape
        two_d = (shp[0] * shp[1], shp[2])
        d, nm, nv = _adamw(weights[k].reshape(two_d), big_grads[k].reshape(two_d), moms[k].reshape(two_d),
                           vels[k].reshape(two_d), name="adamw_" + k)
        out_g[k], out_d[k], out_m[k], out_v[k] = big_grads[k], d.reshape(shp), nm.reshape(shp), nv.reshape(shp)

    def pack_small(tree):
        f = jnp.concatenate([tree[k].reshape(-1) for k in SMALL])
        return jnp.pad(f, (0, rows_small * PACK_C - n_small)).reshape(rows_small, PACK_C)

    d, nm, nv = _adamw(pack_small(small), pack_small(small_grads), pack_small(moms), pack_small(vels), name="adamw_small")
    for tree, flat_out in ((out_d, d), (out_m, nm), (out_v, nv)):
        off = 0
        fo = flat_out.reshape(-1)
        for k, shp in zip(SMALL, small_shapes):
            n = int(np.prod(shp))
            tree[k] = fo[off:off + n].reshape(shp)
            off += n
    out_g.update(small_grads)

    order = ("g_mix", "w_in", "q_norm", "w_uq", "kv_norm", "w_ukv", "rpb", "out_norm_a", "out_norm_b", "out_norm_c", "w_out",
             "g_mlp", "w_mlp_in", "w_mlp_out", "g_final")
    return (loss_out, dx.reshape(x.shape), *[out_g[k] for k in order], *[out_d[k] for k in order],
            *[out_m[k] for k in order], *[out_v[k] for k in order])
```

```python
import math

import numpy as np
import jax
import jax.numpy as jnp
from jax import lax
from jax.experimental import pallas as pl
from jax.experimental.pallas import tpu as pltpu

F32 = jnp.float32
BF16 = jnp.bfloat16

D_MODEL = 1024
HEAD_DIM = 64
Q_LORA = 256
KV_LORA = 128
QK_ROPE = 32
HEADS_A = 6
HEADS_B = 6
HEADS_C = 4
DILATED_PAIRS = ((128, 1), (512, 4), (2048, 16))
BAND_HALF = 64
GRID_W = 64
NA_ROWS = 8
NA_COLS = 16
D_FF = 4096
ROPE_THETA = 10000.0
NORM_EPS = 1e-6
NEG_INF = -1e30
DEPTH = 4

LANE = 128
PROJ_W = 2432
COL_CKV = 256
COL_KPE = 384
COL_B = 512
COL_C = 1664
W_A2 = 768
W_KV = W_A2 + 384
WIDTH_AB = 384
WIDTH_C = 256
SCALE_A = (HEAD_DIM + QK_ROPE) ** -0.5
SCALE_BC = HEAD_DIM ** -0.5

ADAM_LR = 0.001
ADAM_B1 = 0.9
ADAM_B2 = 0.999
ADAM_EPS = 1e-08
ADAM_WD = 0.01
ADAM_STEP = 10

VMEM_LIMIT = 56 * 1024 * 1024
MESH_T = pl.DeviceIdType.MESH


def _cp(*sem):
    return pltpu.CompilerParams(dimension_semantics=sem or None, vmem_limit_bytes=VMEM_LIMIT)


def _tile(n, cands):
    for c in cands:
        if n % c == 0:
            return c
    return n


def _sds(shape, dtype):
    return jax.ShapeDtypeStruct(shape, dtype)


def _mm_nn(a, b, *, name, out_dtype=F32, res=None):
    m, k = a.shape
    n = b.shape[1]
    tm = _tile(m, (512, 256, 128))
    tn = _tile(n, (1024, 768, 512)) if n % LANE == 0 and n != PROJ_W else n

    def body(*refs):
        a_ref, b_ref = refs[0], refs[1]
        o_ref = refs[-1]
        acc = jnp.dot(a_ref[...], b_ref[...], preferred_element_type=F32)
        if res is not None:
            acc = refs[2][...] + acc
        o_ref[...] = acc.astype(o_ref.dtype)

    in_specs = [pl.BlockSpec((tm, k), lambda j, i: (i, 0)), pl.BlockSpec((k, tn), lambda j, i: (0, j))]
    args = [a, b]
    if res is not None:
        in_specs.append(pl.BlockSpec((tm, tn), lambda j, i: (i, j)))
        args.append(res)
    return pl.pallas_call(
        body, name=name, grid=(n // tn, m // tm), in_specs=in_specs,
        out_specs=pl.BlockSpec((tm, tn), lambda j, i: (i, j)), out_shape=_sds((m, n), out_dtype),
        compiler_params=_cp("parallel", "parallel"))(*args)


def _norm_mm(x, g, w, *, name, relu2):
    m, k = x.shape
    n = w.shape[1]
    tm = _tile(m, (512, 256, 128) if n <= PROJ_W else (256, 128))

    def body(x_ref, g_ref, w_ref, h_ref, o_ref):
        xv = x_ref[...]
        h = (xv * _rstd(xv) * g_ref[...]).astype(BF16)
        h_ref[...] = h
        acc = jnp.dot(h, w_ref[...], preferred_element_type=F32)
        if relu2:
            acc = jnp.square(jnp.maximum(acc, 0.0))
        o_ref[...] = acc.astype(o_ref.dtype)

    return pl.pallas_call(
        body, name=name, grid=(m // tm,),
        in_specs=[pl.BlockSpec((tm, k), lambda i: (i, 0)), pl.BlockSpec((1, k), lambda i: (0, 0)),
                  pl.BlockSpec((k, n), lambda i: (0, 0))],
        out_specs=(pl.BlockSpec((tm, k), lambda i: (i, 0)), pl.BlockSpec((tm, n), lambda i: (i, 0))),
        out_shape=(_sds((m, k), BF16), _sds((m, n), BF16 if relu2 else F32)),
        compiler_params=_cp("parallel"))(x, g.reshape(1, k), w)


def _mm_nt(a, b, *, name, out_dtype=F32, relu2_act=None):
    m, c = a.shape
    n = b.shape[0]
    tm = _tile(m, (512, 256, 128))
    tn = _tile(n, (1024, 512, 256, 128))

    def body(*refs):
        a_ref, b_ref = refs[0], refs[1]
        o_ref = refs[-1]
        acc = lax.dot_general(a_ref[...], b_ref[...], (((1,), (1,)), ((), ())), preferred_element_type=F32)
        if relu2_act is not None:
            acc = acc * (2.0 * jnp.sqrt(refs[2][...].astype(F32)))
        o_ref[...] = acc.astype(o_ref.dtype)

    in_specs = [pl.BlockSpec((tm, c), lambda j, i: (i, 0)), pl.BlockSpec((tn, c), lambda j, i: (j, 0))]
    args = [a, b]
    if relu2_act is not None:
        in_specs.append(pl.BlockSpec((tm, tn), lambda j, i: (i, j)))
        args.append(relu2_act)
    return pl.pallas_call(
        body, name=name, grid=(n // tn, m // tm), in_specs=in_specs,
        out_specs=pl.BlockSpec((tm, tn), lambda j, i: (i, j)), out_shape=_sds((m, n), out_dtype),
        compiler_params=_cp("parallel", "parallel"))(*args)


def _mm_tn(a, b, *, name, packed=None):
    m, ka = a.shape
    nb = b.shape[1]
    tka = _tile(ka, (512, 256, 128)) if packed is None else packed[1]
    tnb = _tile(nb, (1024, 768, 512)) if nb != PROJ_W else nb
    tc = _tile(m, (1024, 512, 256, 128))

    def body(*refs):
        a_ref, b_ref, o_ref = refs[0], refs[1], refs[-1]
        part = lax.dot_general(a_ref[...], b_ref[...], (((0,), (0,)), ((), ())), preferred_element_type=F32)

        @pl.when(pl.program_id(2) == 0)
        def _():
            o_ref[...] = part

        @pl.when(pl.program_id(2) != 0)
        def _():
            o_ref[...] += part

    in_specs = [pl.BlockSpec((tc, tka), lambda i, j, c: (c, i)), pl.BlockSpec((tc, tnb), lambda i, j, c: (c, j))]
    kwargs = dict(out_specs=pl.BlockSpec((tka, tnb), lambda i, j, c: (i, j)), out_shape=_sds((ka, nb), F32))
    args = [a, b]
    if packed is not None:
        buf, _, place = packed
        assert tnb == PACK_C
        kwargs = dict(out_specs=pl.BlockSpec((None, tka, tnb), lambda i, j, c: place(i, j) + (0,)))
        if isinstance(buf, jax.ShapeDtypeStruct):
            kwargs["out_shape"] = buf
        else:
            kwargs.update(out_shape=_sds(buf.shape, buf.dtype), input_output_aliases={2: 0})
            in_specs.append(pl.BlockSpec(memory_space=pl.ANY))
            args.append(buf)
    return pl.pallas_call(
        body, name=name, grid=(ka // tka, nb // tnb, m // tc), in_specs=in_specs,
        compiler_params=_cp("parallel", "parallel", "arbitrary"), **kwargs)(*args)


def _rstd(x):
    return lax.rsqrt(jnp.mean(x * x, axis=-1, keepdims=True) + NORM_EPS)


def _rms_bwd_rows(x, g, dy):
    r = _rstd(x)
    gy = dy * g
    c = jnp.sum(x * gy, axis=-1, keepdims=True) * (r * r * r) * (1.0 / x.shape[-1])
    return r * gy - x * c, jnp.sum(dy * x * r, axis=0, keepdims=True)


def _accum(ref, part, first):
    @pl.when(first)
    def _():
        ref[...] = part

    @pl.when(jnp.logical_not(first))
    def _():
        ref[...] += part


def _rope(x, c, s1, s2, sh):
    return x * c + pltpu.roll(x, LANE - sh, 1) * s1 + pltpu.roll(x, sh, 1) * s2


def _rope_t(g, c, s1, s2, sh):
    return g * c + pltpu.roll(g * s1, sh, 1) + pltpu.roll(g * s2, LANE - sh, 1)


def _rope_tables(s, half, reps, lead=0):
    pos = jnp.arange(s, dtype=F32)
    inv_freq = ROPE_THETA ** (-jnp.arange(half, dtype=F32) / half)
    ang = pos[:, None] * inv_freq[None, :]
    cos, sin = jnp.cos(ang), jnp.sin(ang)
    zero = jnp.zeros_like(cos)
    ones, lead0 = jnp.ones((s, lead), F32), jnp.zeros((s, lead), F32)
    pad = jnp.zeros((s, LANE - lead - 2 * half * reps), F32)
    c = jnp.concatenate([ones] + [cos, cos] * reps + [pad], axis=1)
    s1 = jnp.concatenate([lead0] + [-sin, zero] * reps + [pad], axis=1)
    s2 = jnp.concatenate([lead0] + [zero, sin] * reps + [pad], axis=1)
    return c, s1, s2


def _lane_lt64(shape):
    return lax.broadcasted_iota(jnp.int32, shape, len(shape) - 1) % LANE < HEAD_DIM


def _group_sum(x):
    outs = []
    for b in range(x.shape[1] // LANE):
        blk = x[:, b * LANE:(b + 1) * LANE]
        lo = _lane_lt64(blk.shape)
        s0 = jnp.sum(jnp.where(lo, blk, 0.0), axis=1, keepdims=True)
        s1 = jnp.sum(jnp.where(lo, 0.0, blk), axis=1, keepdims=True)
        outs.append(jnp.where(lo, s0, s1))
    return outs


def _row_spec(ts, w):
    return pl.BlockSpec((ts, w), lambda i: (i, 0))


def _fix_spec(w):
    return pl.BlockSpec((1, w), lambda i: (0, 0))


def _rms_bwd(x, g, dy, res, *, name):
    s, d = x.shape
    ts = _tile(s, (512, 256, 128))

    def body(x_ref, g_ref, dy_ref, res_ref, dx_ref, dxb_ref, dg_ref):
        dx, dg = _rms_bwd_rows(x_ref[...], g_ref[...], dy_ref[...])
        dx = res_ref[...] + dx
        dx_ref[...] = dx
        dxb_ref[...] = dx.astype(BF16)
        _accum(dg_ref, dg, pl.program_id(0) == 0)

    return pl.pallas_call(
        body, name=name, grid=(s // ts,),
        in_specs=[_row_spec(ts, d), _fix_spec(d), _row_spec(ts, d), _row_spec(ts, d)],
        out_specs=(_row_spec(ts, d), _row_spec(ts, d), _fix_spec(d)),
        out_shape=(_sds((s, d), F32), _sds((s, d), BF16), _sds((1, d), F32)),
        compiler_params=_cp("arbitrary"))(x, g.reshape(1, d), dy, res)


def _loss_head(x, g, target):
    s, d = x.shape
    ts = _tile(s, (512, 256, 128))

    def body(x_ref, g_ref, t_ref, loss_ref, dx_ref, dxb_ref, dg_ref):
        xv, gv = x_ref[...], g_ref[...]
        err = xv * _rstd(xv) * gv - t_ref[...]
        part = 0.5 * jnp.sum(jnp.sum(err * err, axis=-1, keepdims=True) * (1.0 / d), axis=0, keepdims=True)
        dx, dg = _rms_bwd_rows(xv, gv, err * (1.0 / d))
        dx_ref[...] = dx
        dxb_ref[...] = dx.astype(BF16)
        first = pl.program_id(0) == 0
        _accum(dg_ref, dg, first)
        _accum(loss_ref, jnp.broadcast_to(part, (1, LANE)), first)

    return pl.pallas_call(
        body, name="loss_head", grid=(s // ts,), in_specs=[_row_spec(ts, d), _fix_spec(d), _row_spec(ts, d)],
        out_specs=(_fix_spec(LANE), _row_spec(ts, d), _row_spec(ts, d), _fix_spec(d)),
        out_shape=(_sds((1, LANE), F32), _sds((s, d), F32), _sds((s, d), BF16), _sds((1, d), F32)),
        compiler_params=_cp("arbitrary"))(x, g.reshape(1, d), target)


def _prep_fwd(proj, q_norm, kv_norm, t32, t64):
    s = proj.shape[0]
    ts = _tile(s, (256, 128))

    def body(p_ref, qn_ref, kn_ref, c32, a32, b32, c64, a64, b64, cqn_ref, ckvn_ref, kpe_ref, qkvb_ref, qkvc_ref):
        cq = p_ref[:, 0:Q_LORA]
        cqn_ref[...] = (cq * _rstd(cq) * qn_ref[...]).astype(BF16)
        ckv = p_ref[:, COL_CKV:COL_KPE]
        ckvn_ref[...] = (ckv * _rstd(ckv) * kn_ref[...]).astype(BF16)
        kp = pltpu.roll(p_ref[:, COL_KPE:COL_B], HEAD_DIM, 1)
        kpe_ref[...] = _rope(kp, c32[...], a32[...], b32[...], QK_ROPE // 2).astype(BF16)
        for b in range(6):
            blk = _rope(p_ref[:, COL_B + b * LANE:COL_B + (b + 1) * LANE], c64[...], a64[...], b64[...], HEAD_DIM // 2)
            if b < 3:
                blk = blk * SCALE_BC
            qkvb_ref[:, b * LANE:(b + 1) * LANE] = blk.astype(BF16)
        qkvb_ref[:, 2 * WIDTH_AB:3 * WIDTH_AB] = p_ref[:, COL_B + 2 * WIDTH_AB:COL_C].astype(BF16)
        qkvc_ref[:, 0:WIDTH_C] = (p_ref[:, COL_C:COL_C + WIDTH_C] * SCALE_BC).astype(BF16)
        qkvc_ref[:, WIDTH_C:3 * WIDTH_C] = p_ref[:, COL_C + WIDTH_C:PROJ_W].astype(BF16)

    tab = [_row_spec(ts, LANE)] * 6
    return pl.pallas_call(
        body, name="prep_fwd", grid=(s // ts,),
        in_specs=[_row_spec(ts, PROJ_W), _fix_spec(Q_LORA), _fix_spec(KV_LORA)] + tab,
        out_specs=(_row_spec(ts, Q_LORA), _row_spec(ts, KV_LORA), _row_spec(ts, LANE), _row_spec(ts, 3 * WIDTH_AB),
                   _row_spec(ts, 3 * WIDTH_C)),
        out_shape=(_sds((s, Q_LORA), BF16), _sds((s, KV_LORA), BF16), _sds((s, LANE), BF16),
                   _sds((s, 3 * WIDTH_AB), BF16), _sds((s, 3 * WIDTH_C), BF16)),
        compiler_params=_cp("parallel"))(proj, q_norm.reshape(1, -1), kv_norm.reshape(1, -1), *t32, *t64)


def _prep_bwd(proj, q_norm, kv_norm, t32, t64, dcqn, dckvn, dkpe, db, dc):
    s = proj.shape[0]
    ts = _tile(s, (256, 128))

    def body(p_ref, qn_ref, kn_ref, c32, a32, b32, c64, a64, b64, dcqn_ref, dckvn_ref, dkpe_ref, *rest):
        db_refs, dc_refs = rest[0:9], rest[9:12]
        dp_ref, dqn_ref, dkn_ref = rest[12:15]
        first = pl.program_id(0) == 0
        dx, dg = _rms_bwd_rows(p_ref[:, 0:Q_LORA], qn_ref[...], dcqn_ref[...])
        dp_ref[:, 0:Q_LORA] = dx.astype(BF16)
        _accum(dqn_ref, dg, first)
        dx, dg = _rms_bwd_rows(p_ref[:, COL_CKV:COL_KPE], kn_ref[...], dckvn_ref[...])
        dp_ref[:, COL_CKV:COL_KPE] = dx.astype(BF16)
        _accum(dkn_ref, dg, first)
        g = pltpu.roll(_rope_t(dkpe_ref[...], c32[...], a32[...], b32[...], QK_ROPE // 2), LANE - HEAD_DIM, 1)
        lane = lax.broadcasted_iota(jnp.int32, g.shape, 1)
        dp_ref[:, COL_KPE:COL_B] = jnp.where(lane < QK_ROPE, g, 0.0).astype(BF16)
        for which in range(3):
            for b in range(3):
                sl = slice(b * LANE, (b + 1) * LANE)
                g = db_refs[which][:, sl] + db_refs[3 + which][:, sl] + db_refs[6 + which][:, sl]
                if which < 2:
                    g = _rope_t(g, c64[...], a64[...], b64[...], HEAD_DIM // 2)
                if which == 0:
                    g = g * SCALE_BC
                col = COL_B + which * WIDTH_AB + b * LANE
                dp_ref[:, col:col + LANE] = g.astype(BF16)
        dp_ref[:, COL_C:COL_C + WIDTH_C] = (dc_refs[0][...] * SCALE_BC).astype(BF16)
        dp_ref[:, COL_C + WIDTH_C:COL_C + 2 * WIDTH_C] = dc_refs[1][...].astype(BF16)
        dp_ref[:, COL_C + 2 * WIDTH_C:PROJ_W] = dc_refs[2][...].astype(BF16)

    tab = [_row_spec(ts, LANE)] * 6
    in_specs = ([_row_spec(ts, PROJ_W), _fix_spec(Q_LORA), _fix_spec(KV_LORA)] + tab
                + [_row_spec(ts, Q_LORA), _row_spec(ts, KV_LORA), _row_spec(ts, LANE)]
                + [_row_spec(ts, WIDTH_AB)] * 9 + [_row_spec(ts, WIDTH_C)] * 3)
    return pl.pallas_call(
        body, name="prep_bwd", grid=(s // ts,), in_specs=in_specs,
        out_specs=(_row_spec(ts, PROJ_W), _fix_spec(Q_LORA), _fix_spec(KV_LORA)),
        out_shape=(_sds((s, PROJ_W), BF16), _sds((1, Q_LORA), F32), _sds((1, KV_LORA), F32)),
        compiler_params=_cp("arbitrary"))(proj, q_norm.reshape(1, -1), kv_norm.reshape(1, -1), *t32, *t64,
                                          dcqn, dckvn, dkpe, *db, *dc)


def _a_post_fwd(qa, kva, kpe, t32):
    s = qa.shape[0]
    ts = _tile(s, (512, 256, 128))

    def body(qa_ref, kva_ref, kpe_ref, c32, a32, b32, q_ref, k_ref, kt_ref, v_ref):
        kpe = kpe_ref[...].astype(F32)
        for h in range(HEADS_A):
            hb = slice(h * LANE, (h + 1) * LANE)
            q_ref[:, hb] = _rope(qa_ref[:, hb], c32[...], a32[...], b32[...], QK_ROPE // 2).astype(BF16)
            kh = kva_ref[:, hb] + kpe
            k_ref[:, hb] = kh.astype(BF16)
            kt_ref[hb, :] = kh.T.astype(BF16)
        for p in range(3):
            lo, hi = 2 * p * LANE, (2 * p + 1) * LANE
            v_ref[:, lo:hi] = kva_ref[:, W_A2 + p * LANE:W_A2 + (p + 1) * LANE].astype(BF16)
            v_ref[:, hi:hi + LANE] = jnp.ones((ts, LANE), BF16)

    return pl.pallas_call(
        body, name="a_post_fwd", grid=(s // ts,),
        in_specs=[_row_spec(ts, W_A2), _row_spec(ts, W_KV), _row_spec(ts, LANE)] + [_row_spec(ts, LANE)] * 3,
        out_specs=(_row_spec(ts, W_A2), _row_spec(ts, W_A2), pl.BlockSpec((W_A2, ts), lambda i: (0, i)), _row_spec(ts, W_A2)),
        out_shape=(_sds((s, W_A2), BF16), _sds((s, W_A2), BF16), _sds((W_A2, s), BF16), _sds((s, W_A2), BF16)),
        compiler_params=_cp("parallel"))(qa, kva, kpe, *t32)


def _a_post_bwd(dqa2_t, dka2, dva, t32):
    s = dka2.shape[0]
    ts = _tile(s, (512, 256, 128))

    def body(dqt_ref, dk_ref, dv_ref, c32, a32, b32, dqa_ref, dkva_ref, dkpe_ref):
        acc = None
        for h in range(HEADS_A):
            hb = slice(h * LANE, (h + 1) * LANE)
            dqa_ref[:, hb] = _rope_t(dqt_ref[hb, :].T * SCALE_A, c32[...], a32[...], b32[...], QK_ROPE // 2).astype(BF16)
            part = dk_ref[:, hb] * SCALE_A
            dkva_ref[:, hb] = part.astype(BF16)
            acc = part if acc is None else acc + part
        dkva_ref[:, W_A2:W_KV] = dv_ref[...].astype(BF16)
        dkpe_ref[...] = acc

    return pl.pallas_call(
        body, name="a_post_bwd", grid=(s // ts,),
        in_specs=[pl.BlockSpec((W_A2, ts), lambda i: (0, i)), _row_spec(ts, W_A2), _row_spec(ts, WIDTH_AB)]
        + [_row_spec(ts, LANE)] * 3,
        out_specs=(_row_spec(ts, W_A2), _row_spec(ts, W_KV), _row_spec(ts, LANE)),
        out_shape=(_sds((s, W_A2), BF16), _sds((s, W_KV), BF16), _sds((s, LANE), F32)),
        compiler_params=_cp("parallel"))(dqa2_t, dka2, dva, *t32)


def _pair_masks():
    lane = lax.broadcasted_iota(jnp.int32, (1, LANE), 1)
    return lane < HEAD_DIM, lane >= HEAD_DIM


def _nt(a, b):
    return lax.dot_general(a, b, (((1,), (1,)), ((), ())), preferred_element_type=F32)


def _tn(a, b):
    return lax.dot_general(a, b, (((0,), (0,)), ((), ())), preferred_element_type=F32)


def _stack_heads(x):
    m0, m1 = _pair_masks()
    zero = jnp.zeros_like(x)
    return jnp.concatenate([jnp.where(m0, x, zero), jnp.where(m1, x, zero)], axis=0)


def _stack_stat(x):
    return jnp.concatenate([x[:, 0:1], x[:, HEAD_DIM:HEAD_DIM + 1]], axis=0)


def _softmax_pair(q, kk, vv, bias2):
    t = q.shape[0]
    s = _nt(_stack_heads(q), kk) + bias2
    m = jnp.max(s, axis=1, keepdims=True)
    p = jnp.exp(s - m)
    l = jnp.sum(p, axis=1, keepdims=True)
    o2 = jnp.dot(p.astype(BF16), vv, preferred_element_type=F32) / l
    lse2 = m + jnp.log(l)
    lo = _lane_lt64((t, LANE))
    return jnp.where(lo, o2[:t], o2[t:]), jnp.where(lo, lse2[:t], lse2[t:])


def _softmax_pair_bwd(q, kk, vv, do, lse, delta, bias2):
    t = q.shape[0]
    q2, do2 = _stack_heads(q), _stack_heads(do)
    p = jnp.exp(_nt(q2, kk) + bias2 - _stack_stat(lse))
    ds = p * (_nt(do2, vv) - _stack_stat(delta))
    dsb = ds.astype(BF16)
    dq2 = jnp.dot(dsb, kk, preferred_element_type=F32)
    lo = _lane_lt64((t, LANE))
    return jnp.where(lo, dq2[:t], dq2[t:]), _tn(dsb, q2), _tn(p.astype(BF16), do2), ds


DENSE_FWD_TQ, DENSE_FWD_TK = 512, 8192
DENSE_BWD_TQ, DENSE_BWD_TK = 2048, 1024
LOG2E = math.log2(math.e)


def _dense_fwd(qa, ka, va1):
    s = qa.shape[0]
    tq, tk = min(DENSE_FWD_TQ, s), min(DENSE_FWD_TK, s)
    nk = s // tk
    c = SCALE_A * LOG2E

    def body(q_ref, k_ref, v_ref, o_ref, lse_ref, m_sc, acc_sc):
        j = pl.program_id(2)

        @pl.when(j == 0)
        def _():
            m_sc[...] = jnp.full(m_sc.shape, NEG_INF, F32)
            acc_sc[...] = jnp.zeros(acc_sc.shape, F32)

        vv = v_ref[...]
        for hh in range(2):
            hs = slice(hh * LANE, (hh + 1) * LANE)
            sc = _nt(q_ref[:, hs], k_ref[:, hs])
            m_prev = m_sc[hh]
            m_new = jnp.maximum(m_prev, jnp.max(sc, axis=1, keepdims=True))
            alpha = jnp.exp2((m_prev - m_new) * c)
            p = jnp.exp2((sc - m_new) * c)
            acc_sc[hh] = alpha * acc_sc[hh] + jnp.dot(p.astype(BF16), vv, preferred_element_type=F32)
            m_sc[hh] = m_new

        @pl.when(j == nk - 1)
        def _():
            lo = _lane_lt64((tq, LANE))
            a0, a1 = acc_sc[0], acc_sc[1]
            l0, l1 = a0[:, LANE:], a1[:, LANE:]
            o_ref[...] = jnp.where(lo, a0[:, :LANE] / l0, a1[:, :LANE] / l1)
            lse_ref[0] = _stat_rows(jnp.where(lo, m_sc[0] * SCALE_A + jnp.log(l0), m_sc[1] * SCALE_A + jnp.log(l1)))

    return pl.pallas_call(
        body, name="dense_fwd", grid=(3, s // tq, nk),
        in_specs=[pl.BlockSpec((tq, 2 * LANE), lambda p, i, j: (i, p)), pl.BlockSpec((tk, 2 * LANE), lambda p, i, j: (j, p)),
                  pl.BlockSpec((tk, 2 * LANE), lambda p, i, j: (j, p))],
        out_specs=(pl.BlockSpec((tq, LANE), lambda p, i, j: (i, p)), pl.BlockSpec((1, 8, tq), lambda p, i, j: (p, 0, i))),
        out_shape=(_sds((s, WIDTH_AB), F32), _sds((3, 8, s), F32)),
        scratch_shapes=[pltpu.VMEM((2, tq, 1), F32), pltpu.VMEM((2, tq, 2 * LANE), F32)],
        compiler_params=_cp("parallel", "parallel", "arbitrary"))(qa, ka, va1)


def _stat_rows(lane_dense):
    tr = lane_dense.T
    return jnp.concatenate([tr[0:1, :], tr[HEAD_DIM:HEAD_DIM + 1, :], jnp.zeros((6, tr.shape[1]), F32)], axis=0)


def _dense_bwd(qa, ka, kat, va1, do, lse_rows, delta_rows):
    s = qa.shape[0]
    tq, tk = min(DENSE_BWD_TQ, s), min(DENSE_BWD_TK, s)
    c = SCALE_A * LOG2E

    def body(q_ref, k_ref, kt_ref, v_ref, do_ref, lse_ref, dl_ref, dqt_ref, dk_ref, dv_ref):
        j, i = pl.program_id(1), pl.program_id(2)

        @pl.when((j == 0) & (i == 0))
        def _():
            dqt_ref[...] = jnp.zeros(dqt_ref.shape, F32)

        vv, do_ = v_ref[...], do_ref[...]
        lse_t, dl_t = lse_ref[0] * LOG2E, dl_ref[0]
        vm = _pair_masks()
        cols = pl.ds(pl.multiple_of(i * tq, tq), tq)
        dv = None
        for hh in range(2):
            hs = slice(hh * LANE, (hh + 1) * LANE)
            qh = q_ref[:, hs]
            dom = jnp.where(vm[hh], do_, jnp.zeros_like(do_))
            pt = jnp.exp2(_nt(k_ref[:, hs], qh) * c - lse_t[hh:hh + 1, :])
            dst = pt * (_nt(vv, dom) - dl_t[hh:hh + 1, :])
            pb, dsb = pt.astype(BF16), dst.astype(BF16)
            dv_h = jnp.dot(pb, dom, preferred_element_type=F32)
            dv = dv_h if dv is None else dv + dv_h
            dqt_ref[hs, cols] += jnp.dot(kt_ref[hs, :], dsb, preferred_element_type=F32)
            dk_h = jnp.dot(dsb, qh, preferred_element_type=F32)

            @pl.when(i == 0)
            def _():
                dk_ref[:, hs] = dk_h

            @pl.when(i != 0)
            def _():
                dk_ref[:, hs] += dk_h
        _accum(dv_ref, dv, i == 0)

    st_spec = pl.BlockSpec((1, 8, tq), lambda p, j, i: (p, 0, i))
    return pl.pallas_call(
        body, name="dense_bwd", grid=(3, s // tk, s // tq),
        in_specs=[pl.BlockSpec((tq, 2 * LANE), lambda p, j, i: (i, p)), pl.BlockSpec((tk, 2 * LANE), lambda p, j, i: (j, p)),
                  pl.BlockSpec((2 * LANE, tk), lambda p, j, i: (p, j)), pl.BlockSpec((tk, LANE), lambda p, j, i: (j, 2 * p)),
                  pl.BlockSpec((tq, LANE), lambda p, j, i: (i, p)), st_spec, st_spec],
        out_specs=(pl.BlockSpec((2 * LANE, s), lambda p, j, i: (p, 0)), pl.BlockSpec((tk, 2 * LANE), lambda p, j, i: (j, p)),
                   pl.BlockSpec((tk, LANE), lambda p, j, i: (j, p))),
        out_shape=(_sds((W_A2, s), F32), _sds((s, W_A2), F32), _sds((s, WIDTH_AB), F32)),
        compiler_params=_cp("parallel", "arbitrary", "arbitrary"))(qa, ka, kat, va1, do, lse_rows, delta_rows)


BAND_TILE = 1024
BAND_SUB = 128
QKV_W = 3 * WIDTH_AB


def _band_bias_table():
    row = np.arange(BAND_SUB)[:, None]
    col = np.arange(2 * BAND_SUB)[None, :]
    band = np.abs(row - col + BAND_HALF) <= BAND_HALF
    variants = []
    for idx in range(4):
        ok = band & ((col >= BAND_HALF) | ((idx & 1) == 0)) & ((col < 2 * BAND_SUB - BAND_HALF) | ((idx & 2) == 0))
        one = np.where(ok, 0.0, NEG_INF).astype(np.float32)
        variants.append(np.concatenate([one, one], axis=0))
    return jnp.asarray(np.stack(variants))


def _band_specs(t, n):
    hpt = t // BAND_HALF
    last = n // BAND_HALF - 1
    return [pl.BlockSpec((BAND_HALF, QKV_W), lambda r, i: (jnp.maximum(i * hpt - 1, 0), r)),
            pl.BlockSpec((t, QKV_W), lambda r, i: (i, r)),
            pl.BlockSpec((BAND_HALF, QKV_W), lambda r, i: (jnp.minimum((i + 1) * hpt, last), r)),
            pl.BlockSpec((4, 2 * BAND_SUB, 2 * BAND_SUB), lambda r, i: (0, 0, 0))]


def _band_bias(b_ref, a, nsub, i, nt):
    idx = 0
    if a == 0:
        idx = idx + (i == 0).astype(jnp.int32)
    if a == nsub - 1:
        idx = idx + 2 * (i == nt - 1).astype(jnp.int32)
    return b_ref[idx]


def _band_kv(left, main, right, p):
    kc = slice(WIDTH_AB + p * LANE, WIDTH_AB + (p + 1) * LANE)
    vc = slice(2 * WIDTH_AB + p * LANE, 2 * WIDTH_AB + (p + 1) * LANE)
    return (jnp.concatenate([left[:, kc], main[:, kc], right[:, kc]], axis=0),
            jnp.concatenate([left[:, vc], main[:, vc], right[:, vc]], axis=0))


def _banded_fwd(qkvb, dil, bias):
    s = qkvb.shape[0]
    n = s // dil
    t = min(n, BAND_TILE)
    nsub, nt = t // BAND_SUB, n // t
    view = qkvb.reshape(n, dil * QKV_W)

    def body(left, main, right, b_ref, o_ref, lse_ref):
        i = pl.program_id(1)
        for p in range(3):
            pc = slice(p * LANE, (p + 1) * LANE)
            kk, vv = _band_kv(left, main, right, p)
            for a in range(nsub):
                rows, win = slice(a * BAND_SUB, (a + 1) * BAND_SUB), slice(a * BAND_SUB, (a + 2) * BAND_SUB)
                o, lse = _softmax_pair(main[rows, pc], kk[win], vv[win], _band_bias(b_ref, a, nsub, i, nt))
                o_ref[rows, pc] = o
                lse_ref[rows, pc] = lse

    o_spec = pl.BlockSpec((t, WIDTH_AB), lambda r, i: (i, r))
    o, lse = pl.pallas_call(
        body, name=f"banded_fwd_d{dil}", grid=(dil, nt), in_specs=_band_specs(t, n), out_specs=(o_spec, o_spec),
        out_shape=(_sds((n, dil * WIDTH_AB), F32), _sds((n, dil * WIDTH_AB), F32)),
        compiler_params=_cp("parallel", "parallel"))(view, view, view, bias)
    return o.reshape(s, WIDTH_AB), lse.reshape(s, WIDTH_AB)


def _banded_bwd(qkvb, do, lse, delta, dil, bias):
    s = qkvb.shape[0]
    n = s // dil
    t = min(n, BAND_TILE)
    nsub, nt = t // BAND_SUB, n // t
    view = qkvb.reshape(n, dil * QKV_W)
    side = [a.reshape(n, dil * WIDTH_AB) for a in (do, lse, delta)]

    def body(left, main, right, b_ref, do_ref, lse_ref, dl_ref, dq_ref, dk_ref, dv_ref):
        i = pl.program_id(1)

        @pl.when(i == 0)
        def _():
            dk_ref[...] = jnp.zeros(dk_ref.shape, F32)
            dv_ref[...] = jnp.zeros(dv_ref.shape, F32)

        lrow = pl.multiple_of(jnp.maximum(i * t - BAND_HALF, 0), BAND_HALF)
        rrow = pl.multiple_of(jnp.minimum((i + 1) * t, n - BAND_HALF), BAND_HALF)
        mrow = pl.multiple_of(i * t, BAND_HALF)
        for p in range(3):
            pc = slice(p * LANE, (p + 1) * LANE)
            kk, vv = _band_kv(left, main, right, p)
            parts = []
            for a in range(nsub):
                rows, win = slice(a * BAND_SUB, (a + 1) * BAND_SUB), slice(a * BAND_SUB, (a + 2) * BAND_SUB)
                dq, dk, dv, _ = _softmax_pair_bwd(main[rows, pc], kk[win], vv[win], do_ref[rows, pc], lse_ref[rows, pc],
                                                  dl_ref[rows, pc], _band_bias(b_ref, a, nsub, i, nt))
                dq_ref[rows, pc] = dq
                parts.append((dk, dv))
            for which, ref in ((0, dk_ref), (1, dv_ref)):
                chunks = []
                for c in range(nsub + 1):
                    g = parts[c][which][:BAND_SUB] if c < nsub else None
                    if c >= 1:
                        h = parts[c - 1][which][BAND_SUB:]
                        g = h if g is None else g + h
                    chunks.append(g)
                mid = jnp.concatenate([chunks[0][BAND_HALF:]] + chunks[1:nsub] + [chunks[nsub][:BAND_HALF]], axis=0)
                ref[pl.ds(lrow, BAND_HALF), pc] += chunks[0][:BAND_HALF]
                ref[pl.ds(mrow, t), pc] += mid
                ref[pl.ds(rrow, BAND_HALF), pc] += chunks[nsub][BAND_HALF:]

    q_spec = pl.BlockSpec((t, WIDTH_AB), lambda r, i: (i, r))
    acc_spec = pl.BlockSpec((n, WIDTH_AB), lambda r, i: (0, r))
    shp = _sds((n, dil * WIDTH_AB), F32)
    outs = pl.pallas_call(
        body, name=f"banded_bwd_d{dil}", grid=(dil, nt), in_specs=_band_specs(t, n) + [q_spec, q_spec, q_spec],
        out_specs=(q_spec, acc_spec, acc_spec), out_shape=(shp, shp, shp),
        compiler_params=_cp("parallel", "arbitrary"))(view, view, view, bias, *side)
    return [a.reshape(s, WIDTH_AB) for a in outs]


def _merge_branches(outs, lses):
    s = outs[0].shape[0]
    ts = _tile(s, (512, 256, 128))

    def body(o1, o2, o3, l1, l2, l3, o_ref, lse_ref):
        a, b, c = l1[...], l2[...], l3[...]
        m = jnp.maximum(jnp.maximum(a, b), c)
        ea, eb, ec = jnp.exp(a - m), jnp.exp(b - m), jnp.exp(c - m)
        den = ea + eb + ec
        o_ref[...] = (o1[...] * ea + o2[...] * eb + o3[...] * ec) / den
        lse_ref[...] = m + jnp.log(den)

    sp = _row_spec(ts, WIDTH_AB)
    return pl.pallas_call(
        body, name="merge_branches", grid=(s // ts,), in_specs=[sp] * 6, out_specs=(sp, sp),
        out_shape=(_sds((s, WIDTH_AB), F32), _sds((s, WIDTH_AB), F32)), compiler_params=_cp("parallel"))(*outs, *lses)


def _na_geometry(s):
    rows = s // GRID_W
    assert rows >= 2 * NA_ROWS and rows % NA_ROWS == 0
    return rows, rows // NA_ROWS


def _na_row(n, i, rows):
    rq = n * NA_ROWS + i
    rs = jnp.clip(rq - NA_ROWS // 2, 0, rows - NA_ROWS)
    return pl.multiple_of(rs * GRID_W, GRID_W), rs - rq + NA_ROWS - 1


NA_KEYS = NA_ROWS * GRID_W


def _natten_fwd(qkvc, tfull):
    s = qkvc.shape[0]
    rows, nrb = _na_geometry(s)
    tq = NA_ROWS * GRID_W

    def body(q_ref, k_ref, v_ref, t_ref, o_ref, lse_ref):
        n = pl.program_id(1)
        for i in range(NA_ROWS):
            tok, base = _na_row(n, i, rows)
            kk, vv = k_ref[pl.ds(tok, NA_KEYS), :], v_ref[pl.ds(tok, NA_KEYS), :]
            sl = slice(i * GRID_W, (i + 1) * GRID_W)
            bias2 = jnp.concatenate([t_ref[0, base], t_ref[1, base]], axis=0)
            o, lse = _softmax_pair(q_ref[sl, :], kk, vv, bias2)
            o_ref[sl, :] = o
            lse_ref[sl, :] = lse

    o_spec = pl.BlockSpec((tq, LANE), lambda p, n: (n, p))
    return pl.pallas_call(
        body, name="natten_fwd", grid=(2, nrb),
        in_specs=[pl.BlockSpec((tq, LANE), lambda p, n: (n, p)), pl.BlockSpec((s, LANE), lambda p, n: (0, 2 + p)),
                  pl.BlockSpec((s, LANE), lambda p, n: (0, 4 + p)),
                  pl.BlockSpec((2, NA_ROWS, GRID_W, NA_KEYS), lambda p, n: (p, 0, 0, 0))],
        out_specs=(o_spec, o_spec), out_shape=(_sds((s, WIDTH_C), F32), _sds((s, WIDTH_C), F32)),
        compiler_params=_cp("parallel", "parallel"))(qkvc, qkvc, qkvc, tfull)


def _natten_bwd(qkvc, tfull, do, lse, delta):
    s = qkvc.shape[0]
    rows, nrb = _na_geometry(s)
    tq = NA_ROWS * GRID_W

    def body(q_ref, k_ref, v_ref, t_ref, do_ref, lse_ref, dl_ref, dq_ref, dk_ref, dv_ref, dt_ref):
        n = pl.program_id(1)

        @pl.when(n == 0)
        def _():
            dk_ref[...] = jnp.zeros(dk_ref.shape, F32)
            dv_ref[...] = jnp.zeros(dv_ref.shape, F32)
            dt_ref[...] = jnp.zeros(dt_ref.shape, F32)

        for i in range(NA_ROWS):
            tok, base = _na_row(n, i, rows)
            win = pl.ds(tok, NA_KEYS)
            sl = slice(i * GRID_W, (i + 1) * GRID_W)
            bias2 = jnp.concatenate([t_ref[0, base], t_ref[1, base]], axis=0)
            dq, dk, dv, ds = _softmax_pair_bwd(q_ref[sl, :], k_ref[win, :], v_ref[win, :], do_ref[sl, :], lse_ref[sl, :],
                                               dl_ref[sl, :], bias2)
            dq_ref[sl, :] = dq
            dk_ref[win, :] += dk
            dv_ref[win, :] += dv
            dt_ref[0, base] += ds[:GRID_W]
            dt_ref[1, base] += ds[GRID_W:]

    q_spec = pl.BlockSpec((tq, LANE), lambda p, n: (n, p))
    acc_spec = pl.BlockSpec((s, LANE), lambda p, n: (0, p))
    t_spec = pl.BlockSpec((2, NA_ROWS, GRID_W, NA_KEYS), lambda p, n: (p, 0, 0, 0))
    shp = _sds((s, WIDTH_C), F32)
    return pl.pallas_call(
        body, name="natten_bwd", grid=(2, nrb),
        in_specs=[q_spec, pl.BlockSpec((s, LANE), lambda p, n: (0, 2 + p)), pl.BlockSpec((s, LANE), lambda p, n: (0, 4 + p)),
                  t_spec, q_spec, q_spec, q_spec],
        out_specs=(q_spec, acc_spec, acc_spec, t_spec),
        out_shape=(shp, shp, shp, _sds((HEADS_C, NA_ROWS, GRID_W, NA_KEYS), F32)),
        compiler_params=_cp("parallel", "arbitrary"))(qkvc, qkvc, qkvc, tfull, do, lse, delta)


def _rpb_constants():
    p = np.arange(GRID_W)[:, None]
    qc = np.arange(GRID_W)[None, :]
    dc = np.clip(qc - p, -(NA_COLS - 1), NA_COLS - 1) + NA_COLS - 1
    onehot = (dc.reshape(1, -1) == np.arange(32)[:, None]).astype(np.float32)
    c_start = np.clip(p - NA_COLS // 2, 0, GRID_W - NA_COLS)
    col_ok = ((qc >= c_start) & (qc < c_start + NA_COLS)).reshape(1, -1).astype(np.float32)
    a = np.arange(16)[:, None]
    bj = np.arange(64)[None, :]
    row_sel = ((bj // 8 + bj % 8) == a).astype(np.float32)
    return jnp.asarray(onehot), jnp.asarray(col_ok), jnp.asarray(row_sel)


def _rpb_expand(rpb, onehot, col_ok):
    r2 = jnp.pad(rpb.reshape(HEADS_C * 15, 31), ((0, 4), (0, 1)))

    def body(r_ref, oh_ref, ok_ref, o_ref):
        t = jnp.dot(r_ref[...], oh_ref[...], preferred_element_type=F32, precision=lax.Precision.HIGHEST)
        o_ref[...] = jnp.where(ok_ref[...] > 0.5, t, NEG_INF)

    tm = pl.pallas_call(body, name="rpb_expand", out_shape=_sds((64, GRID_W * GRID_W), F32))(r2, onehot, col_ok)
    tm = tm[:HEADS_C * 15].reshape(HEADS_C, 15, GRID_W, GRID_W)
    tfull = jnp.stack([jnp.concatenate([tm[:, base + j] for j in range(NA_ROWS)], axis=-1) for base in range(NA_ROWS)], axis=1)
    return tfull


def _rpb_grad(dtfull, onehot, row_sel):
    g = dtfull.reshape(HEADS_C, NA_ROWS, GRID_W, NA_ROWS, GRID_W).transpose(0, 1, 3, 2, 4).reshape(HEADS_C, 64, GRID_W * GRID_W)

    def body(g_ref, oh_ref, sel_ref, o_ref):
        for h in range(HEADS_C):
            mid = lax.dot_general(g_ref[h], oh_ref[...], (((1,), (1,)), ((), ())), preferred_element_type=F32,
                                  precision=lax.Precision.HIGHEST)
            o_ref[h] = jnp.dot(sel_ref[...], mid, preferred_element_type=F32, precision=lax.Precision.HIGHEST)

    out = pl.pallas_call(body, name="rpb_grad", out_shape=_sds((HEADS_C, 16, 32), F32))(g, onehot, row_sel)
    return out[:, :15, :31]


def _outnorm_fwd(o_a, o_b, o_c, ga, gb, gc):
    s = o_a.shape[0]
    ts = _tile(s, (512, 256, 128))

    def body(a_ref, b_ref, c_ref, ga_ref, gb_ref, gc_ref, o_ref):
        col = 0
        for ref, g in ((a_ref, ga_ref), (b_ref, gb_ref), (c_ref, gc_ref)):
            x = ref[...]
            o_ref[:, col:col + x.shape[1]] = (x * _rstd(x) * g[...]).astype(BF16)
            col += x.shape[1]

    return pl.pallas_call(
        body, name="outnorm_fwd", grid=(s // ts,),
        in_specs=[_row_spec(ts, WIDTH_AB), _row_spec(ts, WIDTH_AB), _row_spec(ts, WIDTH_C), _fix_spec(WIDTH_AB),
                  _fix_spec(WIDTH_AB), _fix_spec(WIDTH_C)],
        out_specs=_row_spec(ts, D_MODEL), out_shape=_sds((s, D_MODEL), BF16),
        compiler_params=_cp("parallel"))(o_a, o_b, o_c, ga.reshape(1, -1), gb.reshape(1, -1), gc.reshape(1, -1))


def _outnorm_bwd(dmixed, o_a, o_b, o_c, ga, gb, gc):
    s = o_a.shape[0]
    ts = _tile(s, (512, 256, 128))

    def body(dm_ref, a_ref, b_ref, c_ref, ga_ref, gb_ref, gc_ref, *outs):
        first = pl.program_id(0) == 0
        col = 0
        for k, (ref, g) in enumerate(((a_ref, ga_ref), (b_ref, gb_ref), (c_ref, gc_ref))):
            x = ref[...]
            w = x.shape[1]
            dx, dg = _rms_bwd_rows(x, g[...], dm_ref[:, col:col + w])
            col += w
            outs[k][...] = dx.astype(BF16)
            for b, blk in enumerate(_group_sum(dx * x)):
                if k == 0:
                    outs[3][b] = _stat_rows(blk)
                else:
                    outs[3 + k][:, b * LANE:(b + 1) * LANE] = blk
            _accum(outs[6 + k], dg, first)

    widths = (WIDTH_AB, WIDTH_AB, WIDTH_C)
    return pl.pallas_call(
        body, name="outnorm_bwd", grid=(s // ts,),
        in_specs=[_row_spec(ts, D_MODEL)] + [_row_spec(ts, w) for w in widths] + [_fix_spec(w) for w in widths],
        out_specs=tuple([_row_spec(ts, w) for w in widths] + [pl.BlockSpec((3, 8, ts), lambda i: (0, 0, i))]
                        + [_row_spec(ts, w) for w in widths[1:]] + [_fix_spec(w) for w in widths]),
        out_shape=tuple([_sds((s, w), BF16) for w in widths] + [_sds((3, 8, s), F32)]
                        + [_sds((s, w), F32) for w in widths[1:]] + [_sds((1, w), F32) for w in widths]),
        compiler_params=_cp("arbitrary"))(dmixed, o_a, o_b, o_c, ga.reshape(1, -1), gb.reshape(1, -1), gc.reshape(1, -1))


def _adamw(w, g, m, v, *, name):
    r, c = w.shape
    tr = _tile(r, (512, 256, 128, 64, 8))

    def body(w_ref, g_ref, m_ref, v_ref, d_ref, nm_ref, nv_ref):
        gv = g_ref[...]
        nm = ADAM_B1 * m_ref[...] + (1.0 - ADAM_B1) * gv
        nv = ADAM_B2 * v_ref[...] + (1.0 - ADAM_B2) * jnp.square(gv)
        m_hat = nm / (1.0 - ADAM_B1 ** ADAM_STEP)
        v_hat = nv / (1.0 - ADAM_B2 ** ADAM_STEP)
        d_ref[...] = -ADAM_LR * (m_hat / (jnp.sqrt(v_hat) + ADAM_EPS) + ADAM_WD * w_ref[...])
        nm_ref[...] = nm
        nv_ref[...] = nv

    sp = _row_spec(tr, c)
    return pl.pallas_call(
        body, name=name, grid=(r // tr,), in_specs=[sp] * 4, out_specs=(sp, sp, sp),
        out_shape=(_sds((r, c), F32),) * 3, compiler_params=_cp("parallel"))(w, g, m, v)


def _add_n(parts, *, name, out_dtype):
    r, c = parts[0].shape
    tr = max(t for t in range(16, 1025, 16) if r % t == 0)

    def body(*refs):
        acc = refs[0][...].astype(F32)
        for ref in refs[1:-1]:
            acc = acc + ref[...].astype(F32)
        refs[-1][...] = acc.astype(out_dtype)

    sp = _row_spec(tr, c)
    return pl.pallas_call(
        body, name=name, grid=(r // tr,), in_specs=[sp] * len(parts), out_specs=sp, out_shape=_sds((r, c), out_dtype),
        compiler_params=_cp("parallel"))(*parts)


ANY = pl.BlockSpec(memory_space=pl.ANY)
CHIP_FLIPS = ((1, 0), (0, 1), (1, 1))


def _me():
    return lax.axis_index("x"), lax.axis_index("y"), lax.axis_index("c")


def _gather_chips(half):
    def body(src, out, send_sems, recv_sems):
        x, y, c = _me()
        mine = 2 * x + y

        def copy(k, chip, half_idx, to, source=None):
            dst = out.at[chip, half_idx]
            return pltpu.make_async_remote_copy(src_ref=dst if source is None else source, dst_ref=dst,
                                                send_sem=send_sems.at[k], recv_sem=recv_sems.at[k], device_id=to,
                                                device_id_type=MESH_T)

        chips = [(x ^ fx, y ^ fy) for fx, fy in CHIP_FLIPS]
        first = [copy(k, mine, c, (cx, cy, c), source=src) for k, (cx, cy) in enumerate(chips)]
        for cp in first:
            cp.start()
        passed = []
        for k, (cx, cy) in enumerate(chips):
            theirs = 2 * cx + cy
            copy(k, theirs, c, (x, y, c)).wait_recv()
            cp = copy(3 + k, theirs, c, (x, y, 1 - c))
            cp.start()
            passed.append(cp)
        for k, (cx, cy) in enumerate(chips):
            copy(3 + k, 2 * cx + cy, 1 - c, (x, y, c)).wait_recv()
        for cp in first + passed:
            cp.wait_send()

    return pl.pallas_call(
        body, name="gather_chips", in_specs=[ANY], out_specs=ANY, out_shape=_sds((4, 2) + half.shape, half.dtype),
        scratch_shapes=[pltpu.SemaphoreType.DMA((6,)), pltpu.SemaphoreType.DMA((6,))])(half)


def _swap_sibling(block):
    def body(src, out, send_sem, recv_sem):
        x, y, c = _me()
        cp = pltpu.make_async_remote_copy(src_ref=src, dst_ref=out, send_sem=send_sem, recv_sem=recv_sem,
                                          device_id=(x, y, 1 - c), device_id_type=MESH_T)
        cp.start()
        cp.wait()

    return pl.pallas_call(
        body, name="swap_sibling", in_specs=[ANY], out_specs=ANY, out_shape=_sds(block.shape, block.dtype),
        scratch_shapes=[pltpu.SemaphoreType.DMA(()), pltpu.SemaphoreType.DMA(())])(block)


def _scatter_chips(parts):
    def body(src, out, send_sems, recv_sems):
        x, y, c = _me()
        mine = 2 * x + y
        sends = []
        for k, (fx, fy) in enumerate(CHIP_FLIPS):
            theirs = 2 * (x ^ fx) + (y ^ fy)
            cp = pltpu.make_async_remote_copy(src_ref=src.at[theirs], dst_ref=out.at[mine], send_sem=send_sems.at[k],
                                              recv_sem=recv_sems.at[k], device_id=(x ^ fx, y ^ fy, c), device_id_type=MESH_T)
            cp.start()
            sends.append(cp)
        for k, (fx, fy) in enumerate(CHIP_FLIPS):
            theirs = 2 * (x ^ fx) + (y ^ fy)
            pltpu.make_async_remote_copy(src_ref=src.at[theirs], dst_ref=out.at[theirs], send_sem=send_sems.at[k],
                                         recv_sem=recv_sems.at[k], device_id=(x ^ fx, y ^ fy, c),
                                         device_id_type=MESH_T).wait_recv()
        for cp in sends:
            cp.wait_send()

    return pl.pallas_call(
        body, name="scatter_chips", in_specs=[ANY], out_specs=ANY, out_shape=_sds(parts.shape, parts.dtype),
        scratch_shapes=[pltpu.SemaphoreType.DMA((3,)), pltpu.SemaphoreType.DMA((3,))])(parts)


def _all_reduce_small(block):
    r, c = block.shape

    def body(src, out, slots, send_sems, recv_sems):
        x, y, cc = _me()
        mine = 4 * x + 2 * y + cc
        slots[mine] = src[...]
        sends = []
        for k in range(1, 8):
            fx, fy, fc = (k >> 2) & 1, (k >> 1) & 1, k & 1
            cp = pltpu.make_async_remote_copy(src_ref=src, dst_ref=slots.at[mine], send_sem=send_sems.at[k - 1],
                                              recv_sem=recv_sems.at[k - 1], device_id=(x ^ fx, y ^ fy, cc ^ fc),
                                              device_id_type=MESH_T)
            cp.start()
            sends.append(cp)
        for k in range(1, 8):
            fx, fy, fc = (k >> 2) & 1, (k >> 1) & 1, k & 1
            theirs = 4 * (x ^ fx) + 2 * (y ^ fy) + (cc ^ fc)
            pltpu.make_async_remote_copy(src_ref=src, dst_ref=slots.at[theirs], send_sem=send_sems.at[k - 1],
                                         recv_sem=recv_sems.at[k - 1], device_id=(x ^ fx, y ^ fy, cc ^ fc),
                                         device_id_type=MESH_T).wait_recv()
        for cp in sends:
            cp.wait_send()
        acc = slots[0]
        for d in range(1, 8):
            acc = acc + slots[d]
        out[...] = acc

    vm = pl.BlockSpec(memory_space=pltpu.VMEM)
    return pl.pallas_call(
        body, name="all_reduce_small", in_specs=[vm], out_specs=vm, out_shape=_sds((r, c), F32),
        scratch_shapes=[pltpu.VMEM((8, r, c), F32), pltpu.SemaphoreType.DMA((7,)), pltpu.SemaphoreType.DMA((7,))])(block)


DIRECT = ("w_mlp_in", "w_mlp_out", "w_out")
BIG = DIRECT + ("w_in", "w_uq", "w_ukv")
COL_SHARDED = {"w_in": True, "w_uq": True, "w_ukv": True, "w_out": False, "w_mlp_in": True, "w_mlp_out": False}
SMALL = ("g_mix", "q_norm", "kv_norm", "rpb", "out_norm_a", "out_norm_b", "out_norm_c", "g_mlp", "g_final")
PACK_C = 1024
ROW_ALIGN = 32


def _pack_rows(parts):
    flat = jnp.concatenate([p.reshape(-1, PACK_C) for p in parts], axis=0)
    return jnp.pad(flat, ((0, -flat.shape[0] % ROW_ALIGN), (0, 0)))


def _unpack_rows(flat, shapes):
    out, row = [], 0
    for shp in shapes:
        n = int(np.prod(shp)) // PACK_C
        out.append(flat[row:row + n].reshape(shp))
        row += n
    return out


def _full_from_shards(name, g):
    if COL_SHARDED[name]:
        return g.transpose(1, 2, 0, 3).reshape(g.shape[1], g.shape[2], 4 * g.shape[3])
    return g.transpose(1, 0, 2, 3).reshape(g.shape[1], 4 * g.shape[2], g.shape[3])


def _shards_from_full(name, w):
    l, k, n = w.shape
    if COL_SHARDED[name]:
        return w.reshape(l, k, 4, n // 4).transpose(2, 0, 1, 3)
    return w.reshape(l, 4, k // 4, n).transpose(1, 0, 2, 3)


def _arrange_w_in(w):
    z = jnp.zeros(w.shape[:-1] + (COL_B - COL_KPE - QK_ROPE,), w.dtype)
    return jnp.concatenate([w[..., :COL_KPE + QK_ROPE], z, w[..., COL_KPE + QK_ROPE:]], axis=-1)


def _unarrange_w_in(w):
    return jnp.concatenate([w[..., :COL_KPE + QK_ROPE], w[..., COL_B:]], axis=-1)


def _arrange_w_uq(w):
    per = HEAD_DIM + QK_ROPE
    z = jnp.zeros(w.shape[:-1] + (LANE - per,), w.dtype)
    cols = []
    for h in range(HEADS_A):
        cols += [w[..., h * per:(h + 1) * per], z]
    return jnp.concatenate(cols, axis=-1)


def _unarrange_w_uq(w):
    per = HEAD_DIM + QK_ROPE
    return jnp.concatenate([w[..., h * LANE:h * LANE + per] for h in range(HEADS_A)], axis=-1)


def _arrange_w_ukv(w):
    z = jnp.zeros(w.shape[:-1] + (HEAD_DIM,), w.dtype)
    ks = []
    for h in range(HEADS_A):
        ks += [w[..., h * LANE:h * LANE + HEAD_DIM], z]
    vs = [w[..., h * LANE + HEAD_DIM:(h + 1) * LANE] for h in range(HEADS_A)]
    return jnp.concatenate(ks + vs, axis=-1)


def _unarrange_w_ukv(w):
    cols = []
    for h in range(HEADS_A):
        cols += [w[..., h * LANE:h * LANE + HEAD_DIM], w[..., W_A2 + h * HEAD_DIM:W_A2 + (h + 1) * HEAD_DIM]]
    return jnp.concatenate(cols, axis=-1)


def _layer_fwd(x, w, sm, tabs, consts):
    t32, t64 = tabs
    onehot, col_ok, _, band = consts
    h, proj = _norm_mm(x, sm["g_mix"], w["w_in"], name="in_proj", relu2=False)
    cqn, ckvn, kpe, qkvb, qkvc = _prep_fwd(proj, sm["q_norm"], sm["kv_norm"], t32, t64)
    qa = _mm_nn(cqn, w["w_uq"], name="q_up")
    kva = _mm_nn(ckvn, w["w_ukv"], name="kv_up")
    qa2, ka2, kat, va1 = _a_post_fwd(qa, kva, kpe, t32)
    o_a, lse_a = _dense_fwd(qa2, ka2, va1)
    branch = [_banded_fwd(qkvb, dil, band) for _, dil in DILATED_PAIRS]
    o_b, lse_b = _merge_branches([b[0] for b in branch], [b[1] for b in branch])
    tfull = _rpb_expand(sm["rpb"], onehot, col_ok)
    o_c, lse_c = _natten_fwd(qkvc, tfull)
    mixed = _outnorm_fwd(o_a, o_b, o_c, sm["out_norm_a"], sm["out_norm_b"], sm["out_norm_c"])
    x_mid = _mm_nn(mixed, w["w_out"], name="out_proj", res=x)
    h2, act = _norm_mm(x_mid, sm["g_mlp"], w["w_mlp_in"], name="mlp_in", relu2=True)
    x_out = _mm_nn(act, w["w_mlp_out"], name="mlp_out", res=x_mid)
    saved = dict(x=x, h=h, proj=proj, cqn=cqn, ckvn=ckvn, qkvb=qkvb, qkvc=qkvc, qa2=qa2, ka2=ka2, kat=kat, va1=va1, o_a=o_a,
                 lse_a=lse_a, o_b=o_b, lse_b=lse_b, o_c=o_c, lse_c=lse_c, tfull=tfull, mixed=mixed, x_mid=x_mid, h2=h2,
                 act=act)
    return x_out, saved


def _layer_bwd(dx, dxb, sv, w, sm, tabs, consts, packed, places):
    t32, t64 = tabs
    onehot, _, row_sel, band = consts
    g = {}
    du = _mm_nt(dxb, w["w_mlp_out"], name="mlp_out_dx", out_dtype=BF16, relu2_act=sv["act"])
    packed = _mm_tn(sv["act"], dxb, name="mlp_out_dw", packed=(packed,) + places["w_mlp_out"])
    dh2 = _mm_nt(du, w["w_mlp_in"], name="mlp_in_dx")
    packed = _mm_tn(sv["h2"], du, name="mlp_in_dw", packed=(packed,) + places["w_mlp_in"])
    dx_mid, dmb, g["g_mlp"] = _rms_bwd(sv["x_mid"], sm["g_mlp"], dh2, dx, name="norm_mlp_bwd")
    dmixed = _mm_nt(dmb, w["w_out"], name="out_proj_dx")
    packed = _mm_tn(sv["mixed"], dmb, name="out_proj_dw", packed=(packed,) + places["w_out"])
    (do_a, do_b, do_c, dl_a, dl_b, dl_c, g["out_norm_a"], g["out_norm_b"], g["out_norm_c"]) = _outnorm_bwd(
        dmixed, sv["o_a"], sv["o_b"], sv["o_c"], sm["out_norm_a"], sm["out_norm_b"], sm["out_norm_c"])
    dqa2_t, dka2, dva = _dense_bwd(sv["qa2"], sv["ka2"], sv["kat"], sv["va1"], do_a, sv["lse_a"], dl_a)
    db = []
    for _, dil in DILATED_PAIRS:
        db += _banded_bwd(sv["qkvb"], do_b, sv["lse_b"], dl_b, dil, band)
    dq_c, dk_c, dv_c, dtfull = _natten_bwd(sv["qkvc"], sv["tfull"], do_c, sv["lse_c"], dl_c)
    g["rpb"] = _rpb_grad(dtfull, onehot, row_sel)
    dqa, dkva, dkpe = _a_post_bwd(dqa2_t, dka2, dva, t32)
    dcqn = _mm_nt(dqa, w["w_uq"], name="q_up_dx")
    g["w_uq"] = _unarrange_w_uq(_mm_tn(sv["cqn"], dqa, name="q_up_dw"))
    dckvn = _mm_nt(dkva, w["w_ukv"], name="kv_up_dx")
    g["w_ukv"] = _unarrange_w_ukv(_mm_tn(sv["ckvn"], dkva, name="kv_up_dw"))
    dproj, g["q_norm"], g["kv_norm"] = _prep_bwd(sv["proj"], sm["q_norm"], sm["kv_norm"], t32, t64, dcqn, dckvn, dkpe,
                                                  db, (dq_c, dk_c, dv_c))
    dh = _mm_nt(dproj, w["w_in"], name="in_proj_dx")
    g["w_in"] = _unarrange_w_in(_mm_tn(sv["h"], dproj, name="in_proj_dw"))
    dx_in, dxb_in, g["g_mix"] = _rms_bwd(sv["x"], sm["g_mix"], dh, dx_mid, name="norm_mix_bwd")
    return dx_in, dxb_in, g, packed


def _packed_places(offs, l):
    d, r = D_MODEL, D_MODEL // 4
    return {"w_mlp_in": (512, lambda i, j: (j, (offs["w_mlp_in"] + l * d) // 512 + i)),
            "w_mlp_out": (512, lambda i, j: (i // 2, (offs["w_mlp_out"] + l * d) // 512 + i % 2)),
            "w_out": (r, lambda i, j: (i, (offs["w_out"] + l * r) // r))}


def _local_step(x, target, wfull, small, packed_shape, offs):
    s = x.shape[0]
    tabs = (_rope_tables(s, QK_ROPE // 2, 1, lead=HEAD_DIM), _rope_tables(s, HEAD_DIM // 2, 2))
    consts = _rpb_constants() + (_band_bias_table(),)
    saved = []
    for l in range(DEPTH):
        wl = {k: v[l] for k, v in wfull.items()}
        sl = {k: small[k][l] for k in SMALL if k != "g_final"}
        x, sv = _layer_fwd(x, wl, sl, tabs, consts)
        saved.append(sv)
    loss, dx, dxb, dg_final = _loss_head(x, small["g_final"], target)
    grads = [None] * DEPTH
    packed = packed_shape
    for l in reversed(range(DEPTH)):
        wl = {k: v[l] for k, v in wfull.items()}
        sl = {k: small[k][l] for k in SMALL if k != "g_final"}
        dx, dxb, grads[l], packed = _layer_bwd(dx, dxb, saved[l], wl, sl, tabs, consts, packed, _packed_places(offs, l))
    return loss, dx, grads, dg_final, packed


ARRANGE = {"w_in": _arrange_w_in, "w_uq": _arrange_w_uq, "w_ukv": _arrange_w_ukv}


def kernel(x, g_mix, w_in, q_norm, w_uq, kv_norm, w_ukv, rpb, out_norm_a, out_norm_b, out_norm_c, w_out, g_mlp, w_mlp_in, w_mlp_out, g_final, loss_target, m_g_mix, m_w_in, m_q_norm, m_w_uq, m_kv_norm, m_w_ukv, m_rpb, m_out_norm_a, m_out_norm_b, m_out_norm_c, m_w_out, m_g_mlp, m_w_mlp_in, m_w_mlp_out, m_g_final, v_g_mix, v_w_in, v_q_norm, v_w_uq, v_kv_norm, v_w_ukv, v_rpb, v_out_norm_a, v_out_norm_b, v_out_norm_c, v_w_out, v_g_mlp, v_w_mlp_in, v_w_mlp_out, v_g_final):
    args = dict(locals())
    weights = {k: args[k] for k in BIG + SMALL}
    moms = {k: args["m_" + k] for k in BIG + SMALL}
    vels = {k: args["v_" + k] for k in BIG + SMALL}
    cc = lax.axis_index("c")
    my_chip = 2 * lax.axis_index("x") + lax.axis_index("y")

    shard_shapes = [weights[k].shape for k in BIG]
    packed_w = _pack_rows([weights[k].astype(BF16) for k in BIG])
    rows = packed_w.shape[0]
    my_half = lax.dynamic_index_in_dim(packed_w.reshape(2, rows // 2, PACK_C), cc, axis=0, keepdims=False)
    gathered = _gather_chips(my_half).reshape(4, rows, PACK_C)
    per_chip = [_unpack_rows(jnp.where(my_chip == j, packed_w, gathered[j]), shard_shapes) for j in range(4)]
    wfull = {}
    for idx, k in enumerate(BIG):
        full = _full_from_shards(k, jnp.stack([per_chip[j][idx] for j in range(4)]))
        wfull[k] = ARRANGE[k](full) if k in ARRANGE else full

    small = {k: weights[k] for k in SMALL}
    offs, row = {}, 0
    for k, shp in zip(BIG, shard_shapes):
        offs[k] = row
        row += int(np.prod(shp)) // PACK_C
    loss, dx, grads, dg_final, packed = _local_step(x[0], loss_target[0], wfull, small, _sds((4, rows, PACK_C), F32), offs)

    small_local = {k: jnp.stack([grads[l][k].reshape(weights[k].shape[1:]) for l in range(DEPTH)])
                   for k in SMALL if k != "g_final"}
    small_local["g_final"] = dg_final.reshape(-1)
    small_shapes = [weights[k].shape for k in SMALL]
    n_small = sum(int(np.prod(s)) for s in small_shapes)
    flat = jnp.concatenate([small_local[k].reshape(-1) for k in SMALL] + [loss[0, :1]])
    rows_small = -(-(n_small + 1) // PACK_C)
    rows_small += -rows_small % 8
    flat = jnp.pad(flat, (0, rows_small * PACK_C - n_small - 1)).reshape(rows_small, PACK_C)
    red = _all_reduce_small(flat).reshape(-1)
    loss_out = red[n_small]
    small_grads, off = {}, 0
    for k, shp in zip(SMALL, small_shapes):
        n = int(np.prod(shp))
        small_grads[k] = red[off:off + n].reshape(shp)
        off += n

    rest = [k for k in BIG if k not in DIRECT]
    by_shard = {k: _shards_from_full(k, jnp.stack([grads[l][k] for l in range(DEPTH)])) for k in rest}
    tail = jnp.stack([_pack_rows([by_shard[k][j] for k in rest]) for j in range(4)])
    assert offs[rest[0]] + tail.shape[1] == rows
    packed = lax.dynamic_update_slice(packed, tail, (0, offs[rest[0]], 0))
    halves = packed.reshape(4, 2, rows // 2, PACK_C)
    mine = lax.dynamic_index_in_dim(halves, cc, axis=1, keepdims=False)
    other = lax.dynamic_index_in_dim(halves, 1 - cc, axis=1, keepdims=False)
    from_sibling = _swap_sibling(other)
    pair = _add_n([mine.reshape(-1, PACK_C), from_sibling.reshape(-1, PACK_C)], name="pair_sum",
                  out_dtype=BF16).reshape(mine.shape)
    by_chip = _scatter_chips(pair)
    reduced = _add_n([jnp.where(my_chip == j, pair[j], by_chip[j]) for j in range(4)], name="chip_sum", out_dtype=F32)
    theirs = _swap_sibling(reduced)
    joined = jnp.where(cc == 0, jnp.concatenate([reduced, theirs]), jnp.concatenate([theirs, reduced]))
    big_grads = dict(zip(BIG, _unpack_rows(joined, shard_shapes)))

    out_g, out_d, out_m, out_v = {}, {}, {}, {}
    for k in BIG:
        shp = weights[k].shape
        two_d = (shp[0] * shp[1], shp[2])
        d, nm, nv = _adamw(weights[k].reshape(two_d), big_grads[k].reshape(two_d), moms[k].reshape(two_d),
                           vels[k].reshape(two_d), name="adamw_" + k)
        out_g[k], out_d[k], out_m[k], out_v[k] = big_grads[k], d.reshape(shp), nm.reshape(shp), nv.reshape(shp)

    def pack_small(tree):
        f = jnp.concatenate([tree[k].reshape(-1) for k in SMALL])
        return jnp.pad(f, (0, rows_small * PACK_C - n_small)).reshape(rows_small, PACK_C)

    d, nm, nv = _adamw(pack_small(small), pack_small(small_grads), pack_small(moms), pack_small(vels), name="adamw_small")
    for tree, flat_out in ((out_d, d), (out_m, nm), (out_v, nv)):
        off = 0
        fo = flat_out.reshape(-1)
        for k, shp in zip(SMALL, small_shapes):
            n = int(np.prod(shp))
            tree[k] = fo[off:off + n].reshape(shp)
            off += n
    out_g.update(small_grads)

    order = ("g_mix", "w_in", "q_norm", "w_uq", "kv_norm", "w_ukv", "rpb", "out_norm_a", "out_norm_b", "out_norm_c", "w_out",
             "g_mlp", "w_mlp_in", "w_mlp_out", "g_final")
    return (loss_out, dx.reshape(x.shape), *[out_g[k] for k in order], *[out_d[k] for k in order],
            *[out_m[k] for k in order], *[out_v[k] for k in order])
```

```python
import math

import numpy as np
import jax
import jax.numpy as jnp
from jax import lax
from jax.experimental import pallas as pl
from jax.experimental.pallas import tpu as pltpu

F32 = jnp.float32
BF16 = jnp.bfloat16

D_MODEL = 1024
HEAD_DIM = 64
Q_LORA = 256
KV_LORA = 128
QK_ROPE = 32
HEADS_A = 6
HEADS_B = 6
HEADS_C = 4
DILATED_PAIRS = ((128, 1), (512, 4), (2048, 16))
BAND_HALF = 64
GRID_W = 64
NA_ROWS = 8
NA_COLS = 16
D_FF = 4096
ROPE_THETA = 10000.0
NORM_EPS = 1e-6
NEG_INF = -1e30
DEPTH = 4

LANE = 128
PROJ_W = 2432
COL_CKV = 256
COL_KPE = 384
COL_B = 512
COL_C = 1664
W_A2 = 768
W_KV = W_A2 + 384
WIDTH_AB = 384
WIDTH_C = 256
SCALE_A = (HEAD_DIM + QK_ROPE) ** -0.5
SCALE_BC = HEAD_DIM ** -0.5

ADAM_LR = 0.001
ADAM_B1 = 0.9
ADAM_B2 = 0.999
ADAM_EPS = 1e-08
ADAM_WD = 0.01
ADAM_STEP = 10

VMEM_LIMIT = 56 * 1024 * 1024
MESH_T = pl.DeviceIdType.MESH


def _cp(*sem):
    return pltpu.CompilerParams(dimension_semantics=sem or None, vmem_limit_bytes=VMEM_LIMIT)


def _tile(n, cands):
    for c in cands:
        if n % c == 0:
            return c
    return n


def _sds(shape, dtype):
    return jax.ShapeDtypeStruct(shape, dtype)


ROW_TILE_BYTES = 12 * 1024 * 1024


def _row_tiles(row_bytes):
    return tuple(t for t in (2048, 1024, 512, 256, 128) if t * row_bytes <= ROW_TILE_BYTES or t <= 512)


def _mm_nn(a, b, *, name, out_dtype=F32, res=None):
    m, k = a.shape
    n = b.shape[1]
    tn = _tile(n, (1024, 768, 512)) if n % LANE == 0 and n != PROJ_W else n
    tm = _tile(m, _row_tiles(2 * k + tn * (jnp.dtype(out_dtype).itemsize + (4 if res is not None else 0))))

    def body(*refs):
        a_ref, b_ref = refs[0], refs[1]
        o_ref = refs[-1]
        acc = jnp.dot(a_ref[...], b_ref[...], preferred_element_type=F32)
        if res is not None:
            acc = refs[2][...] + acc
        o_ref[...] = acc.astype(o_ref.dtype)

    in_specs = [pl.BlockSpec((tm, k), lambda j, i: (i, 0)), pl.BlockSpec((k, tn), lambda j, i: (0, j))]
    args = [a, b]
    if res is not None:
        in_specs.append(pl.BlockSpec((tm, tn), lambda j, i: (i, j)))
        args.append(res)
    return pl.pallas_call(
        body, name=name, grid=(n // tn, m // tm), in_specs=in_specs,
        out_specs=pl.BlockSpec((tm, tn), lambda j, i: (i, j)), out_shape=_sds((m, n), out_dtype),
        compiler_params=_cp("parallel", "parallel"))(*args)


def _norm_mm(x, g, w, *, name, relu2):
    m, k = x.shape
    n = w.shape[1]
    tm = _tile(m, (512, 256, 128) if n <= PROJ_W else (256, 128))

    def body(x_ref, g_ref, w_ref, h_ref, o_ref):
        xv = x_ref[...]
        h = (xv * _rstd(xv) * g_ref[...]).astype(BF16)
        h_ref[...] = h
        acc = jnp.dot(h, w_ref[...], preferred_element_type=F32)
        if relu2:
            acc = jnp.square(jnp.maximum(acc, 0.0))
        o_ref[...] = acc.astype(o_ref.dtype)

    return pl.pallas_call(
        body, name=name, grid=(m // tm,),
        in_specs=[pl.BlockSpec((tm, k), lambda i: (i, 0)), pl.BlockSpec((1, k), lambda i: (0, 0)),
                  pl.BlockSpec((k, n), lambda i: (0, 0))],
        out_specs=(pl.BlockSpec((tm, k), lambda i: (i, 0)), pl.BlockSpec((tm, n), lambda i: (i, 0))),
        out_shape=(_sds((m, k), BF16), _sds((m, n), BF16 if relu2 else F32)),
        compiler_params=_cp("parallel"))(x, g.reshape(1, k), w)


def _mm_nt(a, b, *, name, out_dtype=F32, relu2_act=None):
    m, c = a.shape
    n = b.shape[0]
    tn = _tile(n, (1024, 512, 256, 128))
    tm = _tile(m, _row_tiles(2 * c + tn * (jnp.dtype(out_dtype).itemsize + (2 if relu2_act is not None else 0))))

    def body(*refs):
        a_ref, b_ref = refs[0], refs[1]
        o_ref = refs[-1]
        acc = lax.dot_general(a_ref[...], b_ref[...], (((1,), (1,)), ((), ())), preferred_element_type=F32)
        if relu2_act is not None:
            acc = acc * (2.0 * jnp.sqrt(refs[2][...].astype(F32)))
        o_ref[...] = acc.astype(o_ref.dtype)

    in_specs = [pl.BlockSpec((tm, c), lambda j, i: (i, 0)), pl.BlockSpec((tn, c), lambda j, i: (j, 0))]
    args = [a, b]
    if relu2_act is not None:
        in_specs.append(pl.BlockSpec((tm, tn), lambda j, i: (i, j)))
        args.append(relu2_act)
    return pl.pallas_call(
        body, name=name, grid=(n // tn, m // tm), in_specs=in_specs,
        out_specs=pl.BlockSpec((tm, tn), lambda j, i: (i, j)), out_shape=_sds((m, n), out_dtype),
        compiler_params=_cp("parallel", "parallel"))(*args)


def _mm_tn(a, b, *, name, packed=None):
    m, ka = a.shape
    nb = b.shape[1]
    tka = _tile(ka, (512, 256, 128)) if packed is None else packed[1]
    tnb = _tile(nb, (1024, 768, 512)) if nb != PROJ_W else nb
    tc = _tile(m, (4096, 2048, 1024, 512, 256, 128) if tnb <= PACK_C else (2048, 1024, 512, 256, 128))

    def body(*refs):
        a_ref, b_ref, o_ref = refs[0], refs[1], refs[-1]
        part = lax.dot_general(a_ref[...], b_ref[...], (((0,), (0,)), ((), ())), preferred_element_type=F32)

        @pl.when(pl.program_id(2) == 0)
        def _():
            o_ref[...] = part

        @pl.when(pl.program_id(2) != 0)
        def _():
            o_ref[...] += part

    in_specs = [pl.BlockSpec((tc, tka), lambda i, j, c: (c, i)), pl.BlockSpec((tc, tnb), lambda i, j, c: (c, j))]
    kwargs = dict(out_specs=pl.BlockSpec((tka, tnb), lambda i, j, c: (i, j)), out_shape=_sds((ka, nb), F32))
    args = [a, b]
    if packed is not None:
        buf, _, place = packed
        assert tnb == PACK_C
        kwargs = dict(out_specs=pl.BlockSpec((None, tka, tnb), lambda i, j, c: place(i, j) + (0,)))
        if isinstance(buf, jax.ShapeDtypeStruct):
            kwargs["out_shape"] = buf
        else:
            kwargs.update(out_shape=_sds(buf.shape, buf.dtype), input_output_aliases={2: 0})
            in_specs.append(pl.BlockSpec(memory_space=pl.ANY))
            args.append(buf)
    return pl.pallas_call(
        body, name=name, grid=(ka // tka, nb // tnb, m // tc), in_specs=in_specs,
        compiler_params=_cp("parallel", "parallel", "arbitrary"), **kwargs)(*args)


def _rstd(x):
    return lax.rsqrt(jnp.mean(x * x, axis=-1, keepdims=True) + NORM_EPS)


def _rms_bwd_rows(x, g, dy):
    r = _rstd(x)
    gy = dy * g
    c = jnp.sum(x * gy, axis=-1, keepdims=True) * (r * r * r) * (1.0 / x.shape[-1])
    return r * gy - x * c, jnp.sum(dy * x * r, axis=0, keepdims=True)


def _accum(ref, part, first):
    @pl.when(first)
    def _():
        ref[...] = part

    @pl.when(jnp.logical_not(first))
    def _():
        ref[...] += part


def _rope(x, c, s1, s2, sh):
    return x * c + pltpu.roll(x, LANE - sh, 1) * s1 + pltpu.roll(x, sh, 1) * s2


def _rope_t(g, c, s1, s2, sh):
    return g * c + pltpu.roll(g * s1, sh, 1) + pltpu.roll(g * s2, LANE - sh, 1)


def _rope_tables(s, half, reps, lead=0):
    pos = jnp.arange(s, dtype=F32)
    inv_freq = ROPE_THETA ** (-jnp.arange(half, dtype=F32) / half)
    ang = pos[:, None] * inv_freq[None, :]
    cos, sin = jnp.cos(ang), jnp.sin(ang)
    zero = jnp.zeros_like(cos)
    ones, lead0 = jnp.ones((s, lead), F32), jnp.zeros((s, lead), F32)
    pad = jnp.zeros((s, LANE - lead - 2 * half * reps), F32)
    c = jnp.concatenate([ones] + [cos, cos] * reps + [pad], axis=1)
    s1 = jnp.concatenate([lead0] + [-sin, zero] * reps + [pad], axis=1)
    s2 = jnp.concatenate([lead0] + [zero, sin] * reps + [pad], axis=1)
    return c, s1, s2


def _lane_lt64(shape):
    return lax.broadcasted_iota(jnp.int32, shape, len(shape) - 1) % LANE < HEAD_DIM


def _group_sum(x):
    outs = []
    for b in range(x.shape[1] // LANE):
        blk = x[:, b * LANE:(b + 1) * LANE]
        lo = _lane_lt64(blk.shape)
        s0 = jnp.sum(jnp.where(lo, blk, 0.0), axis=1, keepdims=True)
        s1 = jnp.sum(jnp.where(lo, 0.0, blk), axis=1, keepdims=True)
        outs.append(jnp.where(lo, s0, s1))
    return outs


def _row_spec(ts, w):
    return pl.BlockSpec((ts, w), lambda i: (i, 0))


def _fix_spec(w):
    return pl.BlockSpec((1, w), lambda i: (0, 0))


def _rms_bwd(x, g, dy, res, *, name):
    s, d = x.shape
    ts = _tile(s, (512, 256, 128))

    def body(x_ref, g_ref, dy_ref, res_ref, dx_ref, dxb_ref, dg_ref):
        dx, dg = _rms_bwd_rows(x_ref[...], g_ref[...], dy_ref[...])
        dx = res_ref[...] + dx
        dx_ref[...] = dx
        dxb_ref[...] = dx.astype(BF16)
        _accum(dg_ref, dg, pl.program_id(0) == 0)

    return pl.pallas_call(
        body, name=name, grid=(s // ts,),
        in_specs=[_row_spec(ts, d), _fix_spec(d), _row_spec(ts, d), _row_spec(ts, d)],
        out_specs=(_row_spec(ts, d), _row_spec(ts, d), _fix_spec(d)),
        out_shape=(_sds((s, d), F32), _sds((s, d), BF16), _sds((1, d), F32)),
        compiler_params=_cp("arbitrary"))(x, g.reshape(1, d), dy, res)


def _loss_head(x, g, target):
    s, d = x.shape
    ts = _tile(s, (512, 256, 128))

    def body(x_ref, g_ref, t_ref, loss_ref, dx_ref, dxb_ref, dg_ref):
        xv, gv = x_ref[...], g_ref[...]
        err = xv * _rstd(xv) * gv - t_ref[...]
        part = 0.5 * jnp.sum(jnp.sum(err * err, axis=-1, keepdims=True) * (1.0 / d), axis=0, keepdims=True)
        dx, dg = _rms_bwd_rows(xv, gv, err * (1.0 / d))
        dx_ref[...] = dx
        dxb_ref[...] = dx.astype(BF16)
        first = pl.program_id(0) == 0
        _accum(dg_ref, dg, first)
        _accum(loss_ref, jnp.broadcast_to(part, (1, LANE)), first)

    return pl.pallas_call(
        body, name="loss_head", grid=(s // ts,), in_specs=[_row_spec(ts, d), _fix_spec(d), _row_spec(ts, d)],
        out_specs=(_fix_spec(LANE), _row_spec(ts, d), _row_spec(ts, d), _fix_spec(d)),
        out_shape=(_sds((1, LANE), F32), _sds((s, d), F32), _sds((s, d), BF16), _sds((1, d), F32)),
        compiler_params=_cp("arbitrary"))(x, g.reshape(1, d), target)


def _prep_fwd(proj, q_norm, kv_norm, t32, t64):
    s = proj.shape[0]
    ts = _tile(s, (256, 128))

    def body(p_ref, qn_ref, kn_ref, c32, a32, b32, c64, a64, b64, cqn_ref, ckvn_ref, kpe_ref, qkvb_ref, qkvc_ref):
        cq = p_ref[:, 0:Q_LORA]
        cqn_ref[...] = (cq * _rstd(cq) * qn_ref[...]).astype(BF16)
        ckv = p_ref[:, COL_CKV:COL_KPE]
        ckvn_ref[...] = (ckv * _rstd(ckv) * kn_ref[...]).astype(BF16)
        kp = pltpu.roll(p_ref[:, COL_KPE:COL_B], HEAD_DIM, 1)
        kpe_ref[...] = _rope(kp, c32[...], a32[...], b32[...], QK_ROPE // 2).astype(BF16)
        for b in range(6):
            blk = _rope(p_ref[:, COL_B + b * LANE:COL_B + (b + 1) * LANE], c64[...], a64[...], b64[...], HEAD_DIM // 2)
            if b < 3:
                blk = blk * SCALE_BC
            qkvb_ref[:, b * LANE:(b + 1) * LANE] = blk.astype(BF16)
        qkvb_ref[:, 2 * WIDTH_AB:3 * WIDTH_AB] = p_ref[:, COL_B + 2 * WIDTH_AB:COL_C].astype(BF16)
        qkvc_ref[:, 0:WIDTH_C] = (p_ref[:, COL_C:COL_C + WIDTH_C] * SCALE_BC).astype(BF16)
        qkvc_ref[:, WIDTH_C:3 * WIDTH_C] = p_ref[:, COL_C + WIDTH_C:PROJ_W].astype(BF16)

    tab = [_row_spec(ts, LANE)] * 6
    return pl.pallas_call(
        body, name="prep_fwd", grid=(s // ts,),
        in_specs=[_row_spec(ts, PROJ_W), _fix_spec(Q_LORA), _fix_spec(KV_LORA)] + tab,
        out_specs=(_row_spec(ts, Q_LORA), _row_spec(ts, KV_LORA), _row_spec(ts, LANE), _row_spec(ts, 3 * WIDTH_AB),
                   _row_spec(ts, 3 * WIDTH_C)),
        out_shape=(_sds((s, Q_LORA), BF16), _sds((s, KV_LORA), BF16), _sds((s, LANE), BF16),
                   _sds((s, 3 * WIDTH_AB), BF16), _sds((s, 3 * WIDTH_C), BF16)),
        compiler_params=_cp("parallel"))(proj, q_norm.reshape(1, -1), kv_norm.reshape(1, -1), *t32, *t64)


def _prep_bwd(proj, q_norm, kv_norm, t32, t64, dcqn, dckvn, dkpe, db, dc):
    s = proj.shape[0]
    ts = _tile(s, (256, 128))

    def body(p_ref, qn_ref, kn_ref, c32, a32, b32, c64, a64, b64, dcqn_ref, dckvn_ref, dkpe_ref, *rest):
        db_refs, dc_refs = rest[0:9], rest[9:12]
        dp_ref, dqn_ref, dkn_ref = rest[12:15]
        first = pl.program_id(0) == 0
        dx, dg = _rms_bwd_rows(p_ref[:, 0:Q_LORA], qn_ref[...], dcqn_ref[...])
        dp_ref[:, 0:Q_LORA] = dx.astype(BF16)
        _accum(dqn_ref, dg, first)
        dx, dg = _rms_bwd_rows(p_ref[:, COL_CKV:COL_KPE], kn_ref[...], dckvn_ref[...])
        dp_ref[:, COL_CKV:COL_KPE] = dx.astype(BF16)
        _accum(dkn_ref, dg, first)
        g = pltpu.roll(_rope_t(dkpe_ref[...], c32[...], a32[...], b32[...], QK_ROPE // 2), LANE - HEAD_DIM, 1)
        lane = lax.broadcasted_iota(jnp.int32, g.shape, 1)
        dp_ref[:, COL_KPE:COL_B] = jnp.where(lane < QK_ROPE, g, 0.0).astype(BF16)
        for which in range(3):
            for b in range(3):
                sl = slice(b * LANE, (b + 1) * LANE)
                g = db_refs[which][:, sl] + db_refs[3 + which][:, sl] + db_refs[6 + which][:, sl]
                if which < 2:
                    g = _rope_t(g, c64[...], a64[...], b64[...], HEAD_DIM // 2)
                if which == 0:
                    g = g * SCALE_BC
                col = COL_B + which * WIDTH_AB + b * LANE
                dp_ref[:, col:col + LANE] = g.astype(BF16)
        dp_ref[:, COL_C:COL_C + WIDTH_C] = (dc_refs[0][...] * SCALE_BC).astype(BF16)
        dp_ref[:, COL_C + WIDTH_C:COL_C + 2 * WIDTH_C] = dc_refs[1][...].astype(BF16)
        dp_ref[:, COL_C + 2 * WIDTH_C:PROJ_W] = dc_refs[2][...].astype(BF16)

    tab = [_row_spec(ts, LANE)] * 6
    in_specs = ([_row_spec(ts, PROJ_W), _fix_spec(Q_LORA), _fix_spec(KV_LORA)] + tab
                + [_row_spec(ts, Q_LORA), _row_spec(ts, KV_LORA), _row_spec(ts, LANE)]
                + [_row_spec(ts, WIDTH_AB)] * 9 + [_row_spec(ts, WIDTH_C)] * 3)
    return pl.pallas_call(
        body, name="prep_bwd", grid=(s // ts,), in_specs=in_specs,
        out_specs=(_row_spec(ts, PROJ_W), _fix_spec(Q_LORA), _fix_spec(KV_LORA)),
        out_shape=(_sds((s, PROJ_W), BF16), _sds((1, Q_LORA), F32), _sds((1, KV_LORA), F32)),
        compiler_params=_cp("arbitrary"))(proj, q_norm.reshape(1, -1), kv_norm.reshape(1, -1), *t32, *t64,
                                          dcqn, dckvn, dkpe, *db, *dc)


def _a_post_fwd(qa, kva, kpe, t32):
    s = qa.shape[0]
    ts = _tile(s, (512, 256, 128))

    def body(qa_ref, kva_ref, kpe_ref, c32, a32, b32, q_ref, k_ref, kt_ref, v_ref):
        kpe = kpe_ref[...].astype(F32)
        for h in range(HEADS_A):
            hb = slice(h * LANE, (h + 1) * LANE)
            q_ref[:, hb] = _rope(qa_ref[:, hb], c32[...], a32[...], b32[...], QK_ROPE // 2).astype(BF16)
            kh = kva_ref[:, hb] + kpe
            k_ref[:, hb] = kh.astype(BF16)
            kt_ref[hb, :] = kh.T.astype(BF16)
        for p in range(3):
            lo, hi = 2 * p * LANE, (2 * p + 1) * LANE
            v_ref[:, lo:hi] = kva_ref[:, W_A2 + p * LANE:W_A2 + (p + 1) * LANE].astype(BF16)
            v_ref[:, hi:hi + LANE] = jnp.ones((ts, LANE), BF16)

    return pl.pallas_call(
        body, name="a_post_fwd", grid=(s // ts,),
        in_specs=[_row_spec(ts, W_A2), _row_spec(ts, W_KV), _row_spec(ts, LANE)] + [_row_spec(ts, LANE)] * 3,
        out_specs=(_row_spec(ts, W_A2), _row_spec(ts, W_A2), pl.BlockSpec((W_A2, ts), lambda i: (0, i)), _row_spec(ts, W_A2)),
        out_shape=(_sds((s, W_A2), BF16), _sds((s, W_A2), BF16), _sds((W_A2, s), BF16), _sds((s, W_A2), BF16)),
        compiler_params=_cp("parallel"))(qa, kva, kpe, *t32)


def _a_post_bwd(dqa2_t, dka2, dva, t32):
    s = dka2.shape[0]
    ts = _tile(s, (512, 256, 128))

    def body(dqt_ref, dk_ref, dv_ref, c32, a32, b32, dqa_ref, dkva_ref, dkpe_ref):
        acc = None
        for h in range(HEADS_A):
            hb = slice(h * LANE, (h + 1) * LANE)
            dqa_ref[:, hb] = _rope_t(dqt_ref[hb, :].T * SCALE_A, c32[...], a32[...], b32[...], QK_ROPE // 2).astype(BF16)
            part = dk_ref[:, hb] * SCALE_A
            dkva_ref[:, hb] = part.astype(BF16)
            acc = part if acc is None else acc + part
        dkva_ref[:, W_A2:W_KV] = dv_ref[...].astype(BF16)
        dkpe_ref[...] = acc

    return pl.pallas_call(
        body, name="a_post_bwd", grid=(s // ts,),
        in_specs=[pl.BlockSpec((W_A2, ts), lambda i: (0, i)), _row_spec(ts, W_A2), _row_spec(ts, WIDTH_AB)]
        + [_row_spec(ts, LANE)] * 3,
        out_specs=(_row_spec(ts, W_A2), _row_spec(ts, W_KV), _row_spec(ts, LANE)),
        out_shape=(_sds((s, W_A2), BF16), _sds((s, W_KV), BF16), _sds((s, LANE), F32)),
        compiler_params=_cp("parallel"))(dqa2_t, dka2, dva, *t32)


def _pair_masks():
    lane = lax.broadcasted_iota(jnp.int32, (1, LANE), 1)
    return lane < HEAD_DIM, lane >= HEAD_DIM


def _nt(a, b):
    return lax.dot_general(a, b, (((1,), (1,)), ((), ())), preferred_element_type=F32)


def _tn(a, b):
    return lax.dot_general(a, b, (((0,), (0,)), ((), ())), preferred_element_type=F32)


def _stack_heads(x):
    m0, m1 = _pair_masks()
    zero = jnp.zeros_like(x)
    return jnp.concatenate([jnp.where(m0, x, zero), jnp.where(m1, x, zero)], axis=0)


def _stack_stat(x):
    return jnp.concatenate([x[:, 0:1], x[:, HEAD_DIM:HEAD_DIM + 1]], axis=0)


def _softmax_pair(q, kk, vv, bias2):
    t = q.shape[0]
    s = _nt(_stack_heads(q), kk) + bias2
    m = jnp.max(s, axis=1, keepdims=True)
    p = jnp.exp(s - m)
    l = jnp.sum(p, axis=1, keepdims=True)
    o2 = jnp.dot(p.astype(BF16), vv, preferred_element_type=F32) / l
    lse2 = m + jnp.log(l)
    lo = _lane_lt64((t, LANE))
    return jnp.where(lo, o2[:t], o2[t:]), jnp.where(lo, lse2[:t], lse2[t:])


def _softmax_pair_bwd(q, kk, vv, do, lse, delta, bias2):
    t = q.shape[0]
    q2, do2 = _stack_heads(q), _stack_heads(do)
    p = jnp.exp(_nt(q2, kk) + bias2 - _stack_stat(lse))
    ds = p * (_nt(do2, vv) - _stack_stat(delta))
    dsb = ds.astype(BF16)
    dq2 = jnp.dot(dsb, kk, preferred_element_type=F32)
    lo = _lane_lt64((t, LANE))
    return jnp.where(lo, dq2[:t], dq2[t:]), _tn(dsb, q2), _tn(p.astype(BF16), do2), ds


DENSE_FWD_TQ, DENSE_FWD_TK = 512, 8192
DENSE_BWD_TQ, DENSE_BWD_TK = 2048, 1024
LOG2E = math.log2(math.e)


def _dense_fwd(qa, ka, va1):
    s = qa.shape[0]
    tq, tk = min(DENSE_FWD_TQ, s), min(DENSE_FWD_TK, s)
    nk = s // tk
    c = SCALE_A * LOG2E

    def body(q_ref, k_ref, v_ref, o_ref, lse_ref, m_sc, acc_sc):
        j = pl.program_id(2)

        @pl.when(j == 0)
        def _():
            m_sc[...] = jnp.full(m_sc.shape, NEG_INF, F32)
            acc_sc[...] = jnp.zeros(acc_sc.shape, F32)

        vv = v_ref[...]
        for hh in range(2):
            hs = slice(hh * LANE, (hh + 1) * LANE)
            sc = _nt(q_ref[:, hs], k_ref[:, hs])
            m_prev = m_sc[hh]
            m_new = jnp.maximum(m_prev, jnp.max(sc, axis=1, keepdims=True))
            alpha = jnp.exp2((m_prev - m_new) * c)
            p = jnp.exp2((sc - m_new) * c)
            acc_sc[hh] = alpha * acc_sc[hh] + jnp.dot(p.astype(BF16), vv, preferred_element_type=F32)
            m_sc[hh] = m_new

        @pl.when(j == nk - 1)
        def _():
            lo = _lane_lt64((tq, LANE))
            a0, a1 = acc_sc[0], acc_sc[1]
            l0, l1 = a0[:, LANE:], a1[:, LANE:]
            o_ref[...] = jnp.where(lo, a0[:, :LANE] / l0, a1[:, :LANE] / l1)
            lse_ref[0] = _stat_rows(jnp.where(lo, m_sc[0] * SCALE_A + jnp.log(l0), m_sc[1] * SCALE_A + jnp.log(l1)))

    return pl.pallas_call(
        body, name="dense_fwd", grid=(3, s // tq, nk),
        in_specs=[pl.BlockSpec((tq, 2 * LANE), lambda p, i, j: (i, p)), pl.BlockSpec((tk, 2 * LANE), lambda p, i, j: (j, p)),
                  pl.BlockSpec((tk, 2 * LANE), lambda p, i, j: (j, p))],
        out_specs=(pl.BlockSpec((tq, LANE), lambda p, i, j: (i, p)), pl.BlockSpec((1, 8, tq), lambda p, i, j: (p, 0, i))),
        out_shape=(_sds((s, WIDTH_AB), F32), _sds((3, 8, s), F32)),
        scratch_shapes=[pltpu.VMEM((2, tq, 1), F32), pltpu.VMEM((2, tq, 2 * LANE), F32)],
        compiler_params=_cp("parallel", "parallel", "arbitrary"))(qa, ka, va1)


def _stat_rows(lane_dense):
    tr = lane_dense.T
    return jnp.concatenate([tr[0:1, :], tr[HEAD_DIM:HEAD_DIM + 1, :], jnp.zeros((6, tr.shape[1]), F32)], axis=0)


def _dense_bwd(qa, ka, kat, va1, do, lse_rows, delta_rows):
    s = qa.shape[0]
    tq, tk = min(DENSE_BWD_TQ, s), min(DENSE_BWD_TK, s)
    c = SCALE_A * LOG2E

    def body(q_ref, k_ref, kt_ref, v_ref, do_ref, lse_ref, dl_ref, dqt_ref, dk_ref, dv_ref):
        j, i = pl.program_id(1), pl.program_id(2)

        @pl.when((j == 0) & (i == 0))
        def _():
            dqt_ref[...] = jnp.zeros(dqt_ref.shape, F32)

        vv, do_ = v_ref[...], do_ref[...]
        lse_t, dl_t = lse_ref[0] * LOG2E, dl_ref[0]
        vm = _pair_masks()
        cols = pl.ds(pl.multiple_of(i * tq, tq), tq)
        dv = None
        for hh in range(2):
            hs = slice(hh * LANE, (hh + 1) * LANE)
            qh = q_ref[:, hs]
            dom = jnp.where(vm[hh], do_, jnp.zeros_like(do_))
            pt = jnp.exp2(_nt(k_ref[:, hs], qh) * c - lse_t[hh:hh + 1, :])
            dst = pt * (_nt(vv, dom) - dl_t[hh:hh + 1, :])
            pb, dsb = pt.astype(BF16), dst.astype(BF16)
            dv_h = jnp.dot(pb, dom, preferred_element_type=F32)
            dv = dv_h if dv is None else dv + dv_h
            dqt_ref[hs, cols] += jnp.dot(kt_ref[hs, :], dsb, preferred_element_type=F32)
            dk_h = jnp.dot(dsb, qh, preferred_element_type=F32)

            @pl.when(i == 0)
            def _():
                dk_ref[:, hs] = dk_h

            @pl.when(i != 0)
            def _():
                dk_ref[:, hs] += dk_h
        _accum(dv_ref, dv, i == 0)

    st_spec = pl.BlockSpec((1, 8, tq), lambda p, j, i: (p, 0, i))
    return pl.pallas_call(
        body, name="dense_bwd", grid=(3, s // tk, s // tq),
        in_specs=[pl.BlockSpec((tq, 2 * LANE), lambda p, j, i: (i, p)), pl.BlockSpec((tk, 2 * LANE), lambda p, j, i: (j, p)),
                  pl.BlockSpec((2 * LANE, tk), lambda p, j, i: (p, j)), pl.BlockSpec((tk, LANE), lambda p, j, i: (j, 2 * p)),
                  pl.BlockSpec((tq, LANE), lambda p, j, i: (i, p)), st_spec, st_spec],
        out_specs=(pl.BlockSpec((2 * LANE, s), lambda p, j, i: (p, 0)), pl.BlockSpec((tk, 2 * LANE), lambda p, j, i: (j, p)),
                   pl.BlockSpec((tk, LANE), lambda p, j, i: (j, p))),
        out_shape=(_sds((W_A2, s), F32), _sds((s, W_A2), F32), _sds((s, WIDTH_AB), F32)),
        compiler_params=_cp("parallel", "arbitrary", "arbitrary"))(qa, ka, kat, va1, do, lse_rows, delta_rows)


BAND_TILE = 1024
BAND_SUB = 128
QKV_W = 3 * WIDTH_AB


def _band_bias_table():
    row = np.arange(BAND_SUB)[:, None]
    col = np.arange(2 * BAND_SUB)[None, :]
    band = np.abs(row - col + BAND_HALF) <= BAND_HALF
    variants = []
    for idx in range(4):
        ok = band & ((col >= BAND_HALF) | ((idx & 1) == 0)) & ((col < 2 * BAND_SUB - BAND_HALF) | ((idx & 2) == 0))
        one = np.where(ok, 0.0, NEG_INF).astype(np.float32)
        variants.append(np.concatenate([one, one], axis=0))
    return jnp.asarray(np.stack(variants))


def _band_specs(t, n):
    hpt = t // BAND_HALF
    last = n // BAND_HALF - 1
    return [pl.BlockSpec((BAND_HALF, QKV_W), lambda r, i: (jnp.maximum(i * hpt - 1, 0), r)),
            pl.BlockSpec((t, QKV_W), lambda r, i: (i, r)),
            pl.BlockSpec((BAND_HALF, QKV_W), lambda r, i: (jnp.minimum((i + 1) * hpt, last), r)),
            pl.BlockSpec((4, 2 * BAND_SUB, 2 * BAND_SUB), lambda r, i: (0, 0, 0))]


def _band_bias(b_ref, a, nsub, i, nt):
    idx = 0
    if a == 0:
        idx = idx + (i == 0).astype(jnp.int32)
    if a == nsub - 1:
        idx = idx + 2 * (i == nt - 1).astype(jnp.int32)
    return b_ref[idx]


def _band_kv(left, main, right, p):
    kc = slice(WIDTH_AB + p * LANE, WIDTH_AB + (p + 1) * LANE)
    vc = slice(2 * WIDTH_AB + p * LANE, 2 * WIDTH_AB + (p + 1) * LANE)
    return (jnp.concatenate([left[:, kc], main[:, kc], right[:, kc]], axis=0),
            jnp.concatenate([left[:, vc], main[:, vc], right[:, vc]], axis=0))


def _banded_fwd(qkvb, dil, bias):
    s = qkvb.shape[0]
    n = s // dil
    t = min(n, BAND_TILE)
    nsub, nt = t // BAND_SUB, n // t
    view = qkvb.reshape(n, dil * QKV_W)

    def body(left, main, right, b_ref, o_ref, lse_ref):
        i = pl.program_id(1)
        for p in range(3):
            pc = slice(p * LANE, (p + 1) * LANE)
            kk, vv = _band_kv(left, main, right, p)
            for a in range(nsub):
                rows, win = slice(a * BAND_SUB, (a + 1) * BAND_SUB), slice(a * BAND_SUB, (a + 2) * BAND_SUB)
                o, lse = _softmax_pair(main[rows, pc], kk[win], vv[win], _band_bias(b_ref, a, nsub, i, nt))
                o_ref[rows, pc] = o
                lse_ref[rows, pc] = lse

    o_spec = pl.BlockSpec((t, WIDTH_AB), lambda r, i: (i, r))
    o, lse = pl.pallas_call(
        body, name=f"banded_fwd_d{dil}", grid=(dil, nt), in_specs=_band_specs(t, n), out_specs=(o_spec, o_spec),
        out_shape=(_sds((n, dil * WIDTH_AB), F32), _sds((n, dil * WIDTH_AB), F32)),
        compiler_params=_cp("parallel", "parallel"))(view, view, view, bias)
    return o.reshape(s, WIDTH_AB), lse.reshape(s, WIDTH_AB)


def _banded_bwd(qkvb, do, lse, delta, dil, bias):
    s = qkvb.shape[0]
    n = s // dil
    t = min(n, BAND_TILE)
    nsub, nt = t // BAND_SUB, n // t
    view = qkvb.reshape(n, dil * QKV_W)
    side = [a.reshape(n, dil * WIDTH_AB) for a in (do, lse, delta)]

    def body(left, main, right, b_ref, do_ref, lse_ref, dl_ref, dq_ref, dk_ref, dv_ref):
        i = pl.program_id(1)

        @pl.when(i == 0)
        def _():
            dk_ref[...] = jnp.zeros(dk_ref.shape, F32)
            dv_ref[...] = jnp.zeros(dv_ref.shape, F32)

        lrow = pl.multiple_of(jnp.maximum(i * t - BAND_HALF, 0), BAND_HALF)
        rrow = pl.multiple_of(jnp.minimum((i + 1) * t, n - BAND_HALF), BAND_HALF)
        mrow = pl.multiple_of(i * t, BAND_HALF)
        for p in range(3):
            pc = slice(p * LANE, (p + 1) * LANE)
            kk, vv = _band_kv(left, main, right, p)
            parts = []
            for a in range(nsub):
                rows, win = slice(a * BAND_SUB, (a + 1) * BAND_SUB), slice(a * BAND_SUB, (a + 2) * BAND_SUB)
                dq, dk, dv, _ = _softmax_pair_bwd(main[rows, pc], kk[win], vv[win], do_ref[rows, pc], lse_ref[rows, pc],
                                                  dl_ref[rows, pc], _band_bias(b_ref, a, nsub, i, nt))
                dq_ref[rows, pc] = dq
                parts.append((dk, dv))
            for which, ref in ((0, dk_ref), (1, dv_ref)):
                chunks = []
                for c in range(nsub + 1):
                    g = parts[c][which][:BAND_SUB] if c < nsub else None
                    if c >= 1:
                        h = parts[c - 1][which][BAND_SUB:]
                        g = h if g is None else g + h
                    chunks.append(g)
                mid = jnp.concatenate([chunks[0][BAND_HALF:]] + chunks[1:nsub] + [chunks[nsub][:BAND_HALF]], axis=0)
                ref[pl.ds(lrow, BAND_HALF), pc] += chunks[0][:BAND_HALF]
                ref[pl.ds(mrow, t), pc] += mid
                ref[pl.ds(rrow, BAND_HALF), pc] += chunks[nsub][BAND_HALF:]

    q_spec = pl.BlockSpec((t, WIDTH_AB), lambda r, i: (i, r))
    acc_spec = pl.BlockSpec((n, WIDTH_AB), lambda r, i: (0, r))
    shp = _sds((n, dil * WIDTH_AB), F32)
    outs = pl.pallas_call(
        body, name=f"banded_bwd_d{dil}", grid=(dil, nt), in_specs=_band_specs(t, n) + [q_spec, q_spec, q_spec],
        out_specs=(q_spec, acc_spec, acc_spec), out_shape=(shp, shp, shp),
        compiler_params=_cp("parallel", "arbitrary"))(view, view, view, bias, *side)
    return [a.reshape(s, WIDTH_AB) for a in outs]


def _merge_branches(outs, lses):
    s = outs[0].shape[0]
    ts = _tile(s, (512, 256, 128))

    def body(o1, o2, o3, l1, l2, l3, o_ref, lse_ref):
        a, b, c = l1[...], l2[...], l3[...]
        m = jnp.maximum(jnp.maximum(a, b), c)
        ea, eb, ec = jnp.exp(a - m), jnp.exp(b - m), jnp.exp(c - m)
        den = ea + eb + ec
        o_ref[...] = (o1[...] * ea + o2[...] * eb + o3[...] * ec) / den
        lse_ref[...] = m + jnp.log(den)

    sp = _row_spec(ts, WIDTH_AB)
    return pl.pallas_call(
        body, name="merge_branches", grid=(s // ts,), in_specs=[sp] * 6, out_specs=(sp, sp),
        out_shape=(_sds((s, WIDTH_AB), F32), _sds((s, WIDTH_AB), F32)), compiler_params=_cp("parallel"))(*outs, *lses)


def _na_geometry(s):
    rows = s // GRID_W
    assert rows >= 2 * NA_ROWS and rows % NA_ROWS == 0
    return rows, rows // NA_ROWS


def _na_row(n, i, rows):
    rq = n * NA_ROWS + i
    rs = jnp.clip(rq - NA_ROWS // 2, 0, rows - NA_ROWS)
    return pl.multiple_of(rs * GRID_W, GRID_W), rs - rq + NA_ROWS - 1


NA_KEYS = NA_ROWS * GRID_W


def _natten_fwd(qkvc, tfull):
    s = qkvc.shape[0]
    rows, nrb = _na_geometry(s)
    tq = NA_ROWS * GRID_W

    def body(q_ref, k_ref, v_ref, t_ref, o_ref, lse_ref):
        n = pl.program_id(1)
        for i in range(NA_ROWS):
            tok, base = _na_row(n, i, rows)
            kk, vv = k_ref[pl.ds(tok, NA_KEYS), :], v_ref[pl.ds(tok, NA_KEYS), :]
            sl = slice(i * GRID_W, (i + 1) * GRID_W)
            bias2 = jnp.concatenate([t_ref[0, base], t_ref[1, base]], axis=0)
            o, lse = _softmax_pair(q_ref[sl, :], kk, vv, bias2)
            o_ref[sl, :] = o
            lse_ref[sl, :] = lse

    o_spec = pl.BlockSpec((tq, LANE), lambda p, n: (n, p))
    return pl.pallas_call(
        body, name="natten_fwd", grid=(2, nrb),
        in_specs=[pl.BlockSpec((tq, LANE), lambda p, n: (n, p)), pl.BlockSpec((s, LANE), lambda p, n: (0, 2 + p)),
                  pl.BlockSpec((s, LANE), lambda p, n: (0, 4 + p)),
                  pl.BlockSpec((2, NA_ROWS, GRID_W, NA_KEYS), lambda p, n: (p, 0, 0, 0))],
        out_specs=(o_spec, o_spec), out_shape=(_sds((s, WIDTH_C), F32), _sds((s, WIDTH_C), F32)),
        compiler_params=_cp("parallel", "parallel"))(qkvc, qkvc, qkvc, tfull)


def _natten_bwd(qkvc, tfull, do, lse, delta):
    s = qkvc.shape[0]
    rows, nrb = _na_geometry(s)
    tq = NA_ROWS * GRID_W

    def body(q_ref, k_ref, v_ref, t_ref, do_ref, lse_ref, dl_ref, dq_ref, dk_ref, dv_ref, dt_ref):
        n = pl.program_id(1)

        @pl.when(n == 0)
        def _():
            dk_ref[...] = jnp.zeros(dk_ref.shape, F32)
            dv_ref[...] = jnp.zeros(dv_ref.shape, F32)
            dt_ref[...] = jnp.zeros(dt_ref.shape, F32)

        for i in range(NA_ROWS):
            tok, base = _na_row(n, i, rows)
            win = pl.ds(tok, NA_KEYS)
            sl = slice(i * GRID_W, (i + 1) * GRID_W)
            bias2 = jnp.concatenate([t_ref[0, base], t_ref[1, base]], axis=0)
            dq, dk, dv, ds = _softmax_pair_bwd(q_ref[sl, :], k_ref[win, :], v_ref[win, :], do_ref[sl, :], lse_ref[sl, :],
                                               dl_ref[sl, :], bias2)
            dq_ref[sl, :] = dq
            dk_ref[win, :] += dk
            dv_ref[win, :] += dv
            dt_ref[0, base] += ds[:GRID_W]
            dt_ref[1, base] += ds[GRID_W:]

    q_spec = pl.BlockSpec((tq, LANE), lambda p, n: (n, p))
    acc_spec = pl.BlockSpec((s, LANE), lambda p, n: (0, p))
    t_spec = pl.BlockSpec((2, NA_ROWS, GRID_W, NA_KEYS), lambda p, n: (p, 0, 0, 0))
    shp = _sds((s, WIDTH_C), F32)
    return pl.pallas_call(
        body, name="natten_bwd", grid=(2, nrb),
        in_specs=[q_spec, pl.BlockSpec((s, LANE), lambda p, n: (0, 2 + p)), pl.BlockSpec((s, LANE), lambda p, n: (0, 4 + p)),
                  t_spec, q_spec, q_spec, q_spec],
        out_specs=(q_spec, acc_spec, acc_spec, t_spec),
        out_shape=(shp, shp, shp, _sds((HEADS_C, NA_ROWS, GRID_W, NA_KEYS), F32)),
        compiler_params=_cp("parallel", "arbitrary"))(qkvc, qkvc, qkvc, tfull, do, lse, delta)


def _rpb_constants():
    p = np.arange(GRID_W)[:, None]
    qc = np.arange(GRID_W)[None, :]
    dc = np.clip(qc - p, -(NA_COLS - 1), NA_COLS - 1) + NA_COLS - 1
    onehot = (dc.reshape(1, -1) == np.arange(32)[:, None]).astype(np.float32)
    c_start = np.clip(p - NA_COLS // 2, 0, GRID_W - NA_COLS)
    col_ok = ((qc >= c_start) & (qc < c_start + NA_COLS)).reshape(1, -1).astype(np.float32)
    a = np.arange(16)[:, None]
    bj = np.arange(64)[None, :]
    row_sel = ((bj // 8 + bj % 8) == a).astype(np.float32)
    return jnp.asarray(onehot), jnp.asarray(col_ok), jnp.asarray(row_sel)


def _rpb_expand(rpb, onehot, col_ok):
    r2 = jnp.pad(rpb.reshape(HEADS_C * 15, 31), ((0, 4), (0, 1)))

    def body(r_ref, oh_ref, ok_ref, o_ref):
        t = jnp.dot(r_ref[...], oh_ref[...], preferred_element_type=F32, precision=lax.Precision.HIGHEST)
        o_ref[...] = jnp.where(ok_ref[...] > 0.5, t, NEG_INF)

    tm = pl.pallas_call(body, name="rpb_expand", out_shape=_sds((64, GRID_W * GRID_W), F32))(r2, onehot, col_ok)
    tm = tm[:HEADS_C * 15].reshape(HEADS_C, 15, GRID_W, GRID_W)
    tfull = jnp.stack([jnp.concatenate([tm[:, base + j] for j in range(NA_ROWS)], axis=-1) for base in range(NA_ROWS)], axis=1)
    return tfull


def _rpb_grad(dtfull, onehot, row_sel):
    g = dtfull.reshape(HEADS_C, NA_ROWS, GRID_W, NA_ROWS, GRID_W).transpose(0, 1, 3, 2, 4).reshape(HEADS_C, 64, GRID_W * GRID_W)

    def body(g_ref, oh_ref, sel_ref, o_ref):
        for h in range(HEADS_C):
            mid = lax.dot_general(g_ref[h], oh_ref[...], (((1,), (1,)), ((), ())), preferred_element_type=F32,
                                  precision=lax.Precision.HIGHEST)
            o_ref[h] = jnp.dot(sel_ref[...], mid, preferred_element_type=F32, precision=lax.Precision.HIGHEST)

    out = pl.pallas_call(body, name="rpb_grad", out_shape=_sds((HEADS_C, 16, 32), F32))(g, onehot, row_sel)
    return out[:, :15, :31]


def _outnorm_fwd(o_a, o_b, o_c, ga, gb, gc):
    s = o_a.shape[0]
    ts = _tile(s, (512, 256, 128))

    def body(a_ref, b_ref, c_ref, ga_ref, gb_ref, gc_ref, o_ref):
        col = 0
        for ref, g in ((a_ref, ga_ref), (b_ref, gb_ref), (c_ref, gc_ref)):
            x = ref[...]
            o_ref[:, col:col + x.shape[1]] = (x * _rstd(x) * g[...]).astype(BF16)
            col += x.shape[1]

    return pl.pallas_call(
        body, name="outnorm_fwd", grid=(s // ts,),
        in_specs=[_row_spec(ts, WIDTH_AB), _row_spec(ts, WIDTH_AB), _row_spec(ts, WIDTH_C), _fix_spec(WIDTH_AB),
                  _fix_spec(WIDTH_AB), _fix_spec(WIDTH_C)],
        out_specs=_row_spec(ts, D_MODEL), out_shape=_sds((s, D_MODEL), BF16),
        compiler_params=_cp("parallel"))(o_a, o_b, o_c, ga.reshape(1, -1), gb.reshape(1, -1), gc.reshape(1, -1))


def _outnorm_bwd(dmixed, o_a, o_b, o_c, ga, gb, gc):
    s = o_a.shape[0]
    ts = _tile(s, (512, 256, 128))

    def body(dm_ref, a_ref, b_ref, c_ref, ga_ref, gb_ref, gc_ref, *outs):
        first = pl.program_id(0) == 0
        col = 0
        for k, (ref, g) in enumerate(((a_ref, ga_ref), (b_ref, gb_ref), (c_ref, gc_ref))):
            x = ref[...]
            w = x.shape[1]
            dx, dg = _rms_bwd_rows(x, g[...], dm_ref[:, col:col + w])
            col += w
            outs[k][...] = dx.astype(BF16)
            for b, blk in enumerate(_group_sum(dx * x)):
                if k == 0:
                    outs[3][b] = _stat_rows(blk)
                else:
                    outs[3 + k][:, b * LANE:(b + 1) * LANE] = blk
            _accum(outs[6 + k], dg, first)

    widths = (WIDTH_AB, WIDTH_AB, WIDTH_C)
    return pl.pallas_call(
        body, name="outnorm_bwd", grid=(s // ts,),
        in_specs=[_row_spec(ts, D_MODEL)] + [_row_spec(ts, w) for w in widths] + [_fix_spec(w) for w in widths],
        out_specs=tuple([_row_spec(ts, w) for w in widths] + [pl.BlockSpec((3, 8, ts), lambda i: (0, 0, i))]
                        + [_row_spec(ts, w) for w in widths[1:]] + [_fix_spec(w) for w in widths]),
        out_shape=tuple([_sds((s, w), BF16) for w in widths] + [_sds((3, 8, s), F32)]
                        + [_sds((s, w), F32) for w in widths[1:]] + [_sds((1, w), F32) for w in widths]),
        compiler_params=_cp("arbitrary"))(dmixed, o_a, o_b, o_c, ga.reshape(1, -1), gb.reshape(1, -1), gc.reshape(1, -1))


def _adamw(w, g, m, v, *, name):
    r, c = w.shape
    tr = _tile(r, (512, 256, 128, 64, 8))

    def body(w_ref, g_ref, m_ref, v_ref, d_ref, nm_ref, nv_ref):
        gv = g_ref[...]
        nm = ADAM_B1 * m_ref[...] + (1.0 - ADAM_B1) * gv
        nv = ADAM_B2 * v_ref[...] + (1.0 - ADAM_B2) * jnp.square(gv)
        m_hat = nm / (1.0 - ADAM_B1 ** ADAM_STEP)
        v_hat = nv / (1.0 - ADAM_B2 ** ADAM_STEP)
        d_ref[...] = -ADAM_LR * (m_hat / (jnp.sqrt(v_hat) + ADAM_EPS) + ADAM_WD * w_ref[...])
        nm_ref[...] = nm
        nv_ref[...] = nv

    sp = _row_spec(tr, c)
    return pl.pallas_call(
        body, name=name, grid=(r // tr,), in_specs=[sp] * 4, out_specs=(sp, sp, sp),
        out_shape=(_sds((r, c), F32),) * 3, compiler_params=_cp("parallel"))(w, g, m, v)


def _add_n(parts, *, name, out_dtype):
    r, c = parts[0].shape
    tr = max(t for t in range(16, 1025, 16) if r % t == 0)

    def body(*refs):
        acc = refs[0][...].astype(F32)
        for ref in refs[1:-1]:
            acc = acc + ref[...].astype(F32)
        refs[-1][...] = acc.astype(out_dtype)

    sp = _row_spec(tr, c)
    return pl.pallas_call(
        body, name=name, grid=(r // tr,), in_specs=[sp] * len(parts), out_specs=sp, out_shape=_sds((r, c), out_dtype),
        compiler_params=_cp("parallel"))(*parts)


ANY = pl.BlockSpec(memory_space=pl.ANY)
CHIP_FLIPS = ((1, 0), (0, 1), (1, 1))


def _me():
    return lax.axis_index("x"), lax.axis_index("y"), lax.axis_index("c")


def _gather_chips(half):
    def body(src, out, send_sems, recv_sems):
        x, y, c = _me()
        mine = 2 * x + y

        def copy(k, chip, half_idx, to, source=None):
            dst = out.at[chip, half_idx]
            return pltpu.make_async_remote_copy(src_ref=dst if source is None else source, dst_ref=dst,
                                                send_sem=send_sems.at[k], recv_sem=recv_sems.at[k], device_id=to,
                                                device_id_type=MESH_T)

        chips = [(x ^ fx, y ^ fy) for fx, fy in CHIP_FLIPS]
        first = [copy(k, mine, c, (cx, cy, c), source=src) for k, (cx, cy) in enumerate(chips)]
        for cp in first:
            cp.start()
        passed = []
        for k, (cx, cy) in enumerate(chips):
            theirs = 2 * cx + cy
            copy(k, theirs, c, (x, y, c)).wait_recv()
            cp = copy(3 + k, theirs, c, (x, y, 1 - c))
            cp.start()
            passed.append(cp)
        for k, (cx, cy) in enumerate(chips):
            copy(3 + k, 2 * cx + cy, 1 - c, (x, y, c)).wait_recv()
        for cp in first + passed:
            cp.wait_send()

    return pl.pallas_call(
        body, name="gather_chips", in_specs=[ANY], out_specs=ANY, out_shape=_sds((4, 2) + half.shape, half.dtype),
        scratch_shapes=[pltpu.SemaphoreType.DMA((6,)), pltpu.SemaphoreType.DMA((6,))])(half)


def _swap_sibling(block):
    def body(src, out, send_sem, recv_sem):
        x, y, c = _me()
        cp = pltpu.make_async_remote_copy(src_ref=src, dst_ref=out, send_sem=send_sem, recv_sem=recv_sem,
                                          device_id=(x, y, 1 - c), device_id_type=MESH_T)
        cp.start()
        cp.wait()

    return pl.pallas_call(
        body, name="swap_sibling", in_specs=[ANY], out_specs=ANY, out_shape=_sds(block.shape, block.dtype),
        scratch_shapes=[pltpu.SemaphoreType.DMA(()), pltpu.SemaphoreType.DMA(())])(block)


def _swap_other_half(halves):
    def body(src, out, send_sem, recv_sem):
        x, y, c = _me()
        cp = pltpu.make_async_remote_copy(src_ref=src.at[:, 1 - c], dst_ref=out, send_sem=send_sem, recv_sem=recv_sem,
                                          device_id=(x, y, 1 - c), device_id_type=MESH_T)
        cp.start()
        cp.wait()

    shape = (halves.shape[0],) + halves.shape[2:]
    return pl.pallas_call(
        body, name="swap_other_half", in_specs=[ANY], out_specs=ANY, out_shape=_sds(shape, halves.dtype),
        scratch_shapes=[pltpu.SemaphoreType.DMA(()), pltpu.SemaphoreType.DMA(())])(halves)


def _pair_sum(halves, core, other):
    n, _, h, c = halves.shape
    tr = max(t for t in range(16, 1025, 16) if h % t == 0)

    def body(core_ref, a_ref, b_ref, o_ref):
        o_ref[...] = (a_ref[...] + b_ref[...]).astype(BF16)

    grid_spec = pltpu.PrefetchScalarGridSpec(
        num_scalar_prefetch=1, grid=(n, h // tr),
        in_specs=[pl.BlockSpec((None, None, tr, c), lambda j, i, core_ref: (j, core_ref[0], i, 0)),
                  pl.BlockSpec((None, tr, c), lambda j, i, core_ref: (j, i, 0))],
        out_specs=pl.BlockSpec((None, tr, c), lambda j, i, core_ref: (j, i, 0)))
    return pl.pallas_call(
        body, name="pair_sum", grid_spec=grid_spec, out_shape=_sds((n, h, c), BF16),
        compiler_params=_cp("parallel", "parallel"))(core.reshape(1).astype(jnp.int32), halves, other)


def _scatter_chips(parts):
    def body(src, out, send_sems, recv_sems):
        x, y, c = _me()
        mine = 2 * x + y
        sends = []
        for k, (fx, fy) in enumerate(CHIP_FLIPS):
            theirs = 2 * (x ^ fx) + (y ^ fy)
            cp = pltpu.make_async_remote_copy(src_ref=src.at[theirs], dst_ref=out.at[mine], send_sem=send_sems.at[k],
                                              recv_sem=recv_sems.at[k], device_id=(x ^ fx, y ^ fy, c), device_id_type=MESH_T)
            cp.start()
            sends.append(cp)
        for k, (fx, fy) in enumerate(CHIP_FLIPS):
            theirs = 2 * (x ^ fx) + (y ^ fy)
            pltpu.make_async_remote_copy(src_ref=src.at[theirs], dst_ref=out.at[theirs], send_sem=send_sems.at[k],
                                         recv_sem=recv_sems.at[k], device_id=(x ^ fx, y ^ fy, c),
                                         device_id_type=MESH_T).wait_recv()
        for cp in sends:
            cp.wait_send()

    return pl.pallas_call(
        body, name="scatter_chips", in_specs=[ANY], out_specs=ANY, out_shape=_sds(parts.shape, parts.dtype),
        scratch_shapes=[pltpu.SemaphoreType.DMA((3,)), pltpu.SemaphoreType.DMA((3,))])(parts)


def _all_reduce_small(block):
    r, c = block.shape

    def body(src, out, slots, send_sems, recv_sems):
        x, y, cc = _me()
        mine = 4 * x + 2 * y + cc
        slots[mine] = src[...]
        sends = []
        for k in range(1, 8):
            fx, fy, fc = (k >> 2) & 1, (k >> 1) & 1, k & 1
            cp = pltpu.make_async_remote_copy(src_ref=src, dst_ref=slots.at[mine], send_sem=send_sems.at[k - 1],
                                              recv_sem=recv_sems.at[k - 1], device_id=(x ^ fx, y ^ fy, cc ^ fc),
                                              device_id_type=MESH_T)
            cp.start()
            sends.append(cp)
        for k in range(1, 8):
            fx, fy, fc = (k >> 2) & 1, (k >> 1) & 1, k & 1
            theirs = 4 * (x ^ fx) + 2 * (y ^ fy) + (cc ^ fc)
            pltpu.make_async_remote_copy(src_ref=src, dst_ref=slots.at[theirs], send_sem=send_sems.at[k - 1],
                                         recv_sem=recv_sems.at[k - 1], device_id=(x ^ fx, y ^ fy, cc ^ fc),
                                         device_id_type=MESH_T).wait_recv()
        for cp in sends:
            cp.wait_send()
        acc = slots[0]
        for d in range(1, 8):
            acc = acc + slots[d]
        out[...] = acc

    vm = pl.BlockSpec(memory_space=pltpu.VMEM)
    return pl.pallas_call(
        body, name="all_reduce_small", in_specs=[vm], out_specs=vm, out_shape=_sds((r, c), F32),
        scratch_shapes=[pltpu.VMEM((8, r, c), F32), pltpu.SemaphoreType.DMA((7,)), pltpu.SemaphoreType.DMA((7,))])(block)


DIRECT = ("w_mlp_in", "w_mlp_out", "w_out")
BIG = DIRECT + ("w_in", "w_uq", "w_ukv")
COL_SHARDED = {"w_in": True, "w_uq": True, "w_ukv": True, "w_out": False, "w_mlp_in": True, "w_mlp_out": False}
SMALL = ("g_mix", "q_norm", "kv_norm", "rpb", "out_norm_a", "out_norm_b", "out_norm_c", "g_mlp", "g_final")
PACK_C = 1024
ROW_ALIGN = 32


def _pack_rows(parts):
    flat = jnp.concatenate([p.reshape(-1, PACK_C) for p in parts], axis=0)
    return jnp.pad(flat, ((0, -flat.shape[0] % ROW_ALIGN), (0, 0)))


def _unpack_rows(flat, shapes):
    out, row = [], 0
    for shp in shapes:
        n = int(np.prod(shp)) // PACK_C
        out.append(flat[row:row + n].reshape(shp))
        row += n
    return out


def _full_from_shards(name, g):
    if COL_SHARDED[name]:
        return g.transpose(1, 2, 0, 3).reshape(g.shape[1], g.shape[2], 4 * g.shape[3])
    return g.transpose(1, 0, 2, 3).reshape(g.shape[1], 4 * g.shape[2], g.shape[3])


def _shards_from_full(name, w):
    l, k, n = w.shape
    if COL_SHARDED[name]:
        return w.reshape(l, k, 4, n // 4).transpose(2, 0, 1, 3)
    return w.reshape(l, 4, k // 4, n).transpose(1, 0, 2, 3)


def _arrange_w_in(w):
    z = jnp.zeros(w.shape[:-1] + (COL_B - COL_KPE - QK_ROPE,), w.dtype)
    return jnp.concatenate([w[..., :COL_KPE + QK_ROPE], z, w[..., COL_KPE + QK_ROPE:]], axis=-1)


def _unarrange_w_in(w):
    return jnp.concatenate([w[..., :COL_KPE + QK_ROPE], w[..., COL_B:]], axis=-1)


def _arrange_w_uq(w):
    per = HEAD_DIM + QK_ROPE
    z = jnp.zeros(w.shape[:-1] + (LANE - per,), w.dtype)
    cols = []
    for h in range(HEADS_A):
        cols += [w[..., h * per:(h + 1) * per], z]
    return jnp.concatenate(cols, axis=-1)


def _unarrange_w_uq(w):
    per = HEAD_DIM + QK_ROPE
    return jnp.concatenate([w[..., h * LANE:h * LANE + per] for h in range(HEADS_A)], axis=-1)


def _arrange_w_ukv(w):
    z = jnp.zeros(w.shape[:-1] + (HEAD_DIM,), w.dtype)
    ks = []
    for h in range(HEADS_A):
        ks += [w[..., h * LANE:h * LANE + HEAD_DIM], z]
    vs = [w[..., h * LANE + HEAD_DIM:(h + 1) * LANE] for h in range(HEADS_A)]
    return jnp.concatenate(ks + vs, axis=-1)


def _unarrange_w_ukv(w):
    cols = []
    for h in range(HEADS_A):
        cols += [w[..., h * LANE:h * LANE + HEAD_DIM], w[..., W_A2 + h * HEAD_DIM:W_A2 + (h + 1) * HEAD_DIM]]
    return jnp.concatenate(cols, axis=-1)


def _layer_fwd(x, w, sm, tabs, consts):
    t32, t64 = tabs
    onehot, col_ok, _, band = consts
    h, proj = _norm_mm(x, sm["g_mix"], w["w_in"], name="in_proj", relu2=False)
    cqn, ckvn, kpe, qkvb, qkvc = _prep_fwd(proj, sm["q_norm"], sm["kv_norm"], t32, t64)
    qa = _mm_nn(cqn, w["w_uq"], name="q_up")
    kva = _mm_nn(ckvn, w["w_ukv"], name="kv_up")
    qa2, ka2, kat, va1 = _a_post_fwd(qa, kva, kpe, t32)
    o_a, lse_a = _dense_fwd(qa2, ka2, va1)
    branch = [_banded_fwd(qkvb, dil, band) for _, dil in DILATED_PAIRS]
    o_b, lse_b = _merge_branches([b[0] for b in branch], [b[1] for b in branch])
    tfull = _rpb_expand(sm["rpb"], onehot, col_ok)
    o_c, lse_c = _natten_fwd(qkvc, tfull)
    mixed = _outnorm_fwd(o_a, o_b, o_c, sm["out_norm_a"], sm["out_norm_b"], sm["out_norm_c"])
    x_mid = _mm_nn(mixed, w["w_out"], name="out_proj", res=x)
    h2, act = _norm_mm(x_mid, sm["g_mlp"], w["w_mlp_in"], name="mlp_in", relu2=True)
    x_out = _mm_nn(act, w["w_mlp_out"], name="mlp_out", res=x_mid)
    saved = dict(x=x, h=h, proj=proj, cqn=cqn, ckvn=ckvn, qkvb=qkvb, qkvc=qkvc, qa2=qa2, ka2=ka2, kat=kat, va1=va1, o_a=o_a,
                 lse_a=lse_a, o_b=o_b, lse_b=lse_b, o_c=o_c, lse_c=lse_c, tfull=tfull, mixed=mixed, x_mid=x_mid, h2=h2,
                 act=act)
    return x_out, saved


def _layer_bwd(dx, dxb, sv, w, sm, tabs, consts, packed, places):
    t32, t64 = tabs
    onehot, _, row_sel, band = consts
    g = {}
    du = _mm_nt(dxb, w["w_mlp_out"], name="mlp_out_dx", out_dtype=BF16, relu2_act=sv["act"])
    packed = _mm_tn(sv["act"], dxb, name="mlp_out_dw", packed=(packed,) + places["w_mlp_out"])
    dh2 = _mm_nt(du, w["w_mlp_in"], name="mlp_in_dx")
    packed = _mm_tn(sv["h2"], du, name="mlp_in_dw", packed=(packed,) + places["w_mlp_in"])
    dx_mid, dmb, g["g_mlp"] = _rms_bwd(sv["x_mid"], sm["g_mlp"], dh2, dx, name="norm_mlp_bwd")
    dmixed = _mm_nt(dmb, w["w_out"], name="out_proj_dx")
    packed = _mm_tn(sv["mixed"], dmb, name="out_proj_dw", packed=(packed,) + places["w_out"])
    (do_a, do_b, do_c, dl_a, dl_b, dl_c, g["out_norm_a"], g["out_norm_b"], g["out_norm_c"]) = _outnorm_bwd(
        dmixed, sv["o_a"], sv["o_b"], sv["o_c"], sm["out_norm_a"], sm["out_norm_b"], sm["out_norm_c"])
    dqa2_t, dka2, dva = _dense_bwd(sv["qa2"], sv["ka2"], sv["kat"], sv["va1"], do_a, sv["lse_a"], dl_a)
    db = []
    for _, dil in DILATED_PAIRS:
        db += _banded_bwd(sv["qkvb"], do_b, sv["lse_b"], dl_b, dil, band)
    dq_c, dk_c, dv_c, dtfull = _natten_bwd(sv["qkvc"], sv["tfull"], do_c, sv["lse_c"], dl_c)
    g["rpb"] = _rpb_grad(dtfull, onehot, row_sel)
    dqa, dkva, dkpe = _a_post_bwd(dqa2_t, dka2, dva, t32)
    dcqn = _mm_nt(dqa, w["w_uq"], name="q_up_dx")
    g["w_uq"] = _unarrange_w_uq(_mm_tn(sv["cqn"], dqa, name="q_up_dw"))
    dckvn = _mm_nt(dkva, w["w_ukv"], name="kv_up_dx")
    g["w_ukv"] = _unarrange_w_ukv(_mm_tn(sv["ckvn"], dkva, name="kv_up_dw"))
    dproj, g["q_norm"], g["kv_norm"] = _prep_bwd(sv["proj"], sm["q_norm"], sm["kv_norm"], t32, t64, dcqn, dckvn, dkpe,
                                                  db, (dq_c, dk_c, dv_c))
    dh = _mm_nt(dproj, w["w_in"], name="in_proj_dx")
    g["w_in"] = _unarrange_w_in(_mm_tn(sv["h"], dproj, name="in_proj_dw"))
    dx_in, dxb_in, g["g_mix"] = _rms_bwd(sv["x"], sm["g_mix"], dh, dx_mid, name="norm_mix_bwd")
    return dx_in, dxb_in, g, packed


def _packed_places(offs, l):
    d, r = D_MODEL, D_MODEL // 4
    return {"w_mlp_in": (512, lambda i, j: (j, (offs["w_mlp_in"] + l * d) // 512 + i)),
            "w_mlp_out": (512, lambda i, j: (i // 2, (offs["w_mlp_out"] + l * d) // 512 + i % 2)),
            "w_out": (r, lambda i, j: (i, (offs["w_out"] + l * r) // r))}


def _local_step(x, target, wfull, small, packed_shape, offs):
    s = x.shape[0]
    tabs = (_rope_tables(s, QK_ROPE // 2, 1, lead=HEAD_DIM), _rope_tables(s, HEAD_DIM // 2, 2))
    consts = _rpb_constants() + (_band_bias_table(),)
    saved = []
    for l in range(DEPTH):
        wl = {k: v[l] for k, v in wfull.items()}
        sl = {k: small[k][l] for k in SMALL if k != "g_final"}
        x, sv = _layer_fwd(x, wl, sl, tabs, consts)
        saved.append(sv)
    loss, dx, dxb, dg_final = _loss_head(x, small["g_final"], target)
    grads = [None] * DEPTH
    packed = packed_shape
    for l in reversed(range(DEPTH)):
        wl = {k: v[l] for k, v in wfull.items()}
        sl = {k: small[k][l] for k in SMALL if k != "g_final"}
        dx, dxb, grads[l], packed = _layer_bwd(dx, dxb, saved[l], wl, sl, tabs, consts, packed, _packed_places(offs, l))
    return loss, dx, grads, dg_final, packed


ARRANGE = {"w_in": _arrange_w_in, "w_uq": _arrange_w_uq, "w_ukv": _arrange_w_ukv}


def kernel(x, g_mix, w_in, q_norm, w_uq, kv_norm, w_ukv, rpb, out_norm_a, out_norm_b, out_norm_c, w_out, g_mlp, w_mlp_in, w_mlp_out, g_final, loss_target, m_g_mix, m_w_in, m_q_norm, m_w_uq, m_kv_norm, m_w_ukv, m_rpb, m_out_norm_a, m_out_norm_b, m_out_norm_c, m_w_out, m_g_mlp, m_w_mlp_in, m_w_mlp_out, m_g_final, v_g_mix, v_w_in, v_q_norm, v_w_uq, v_kv_norm, v_w_ukv, v_rpb, v_out_norm_a, v_out_norm_b, v_out_norm_c, v_w_out, v_g_mlp, v_w_mlp_in, v_w_mlp_out, v_g_final):
    args = dict(locals())
    weights = {k: args[k] for k in BIG + SMALL}
    moms = {k: args["m_" + k] for k in BIG + SMALL}
    vels = {k: args["v_" + k] for k in BIG + SMALL}
    cc = lax.axis_index("c")
    my_chip = 2 * lax.axis_index("x") + lax.axis_index("y")

    shard_shapes = [weights[k].shape for k in BIG]
    packed_w = _pack_rows([weights[k].astype(BF16) for k in BIG])
    rows = packed_w.shape[0]
    my_half = lax.dynamic_index_in_dim(packed_w.reshape(2, rows // 2, PACK_C), cc, axis=0, keepdims=False)
    gathered = _gather_chips(my_half).reshape(4, rows, PACK_C)
    per_chip = [_unpack_rows(jnp.where(my_chip == j, packed_w, gathered[j]), shard_shapes) for j in range(4)]
    wfull = {}
    for idx, k in enumerate(BIG):
        full = _full_from_shards(k, jnp.stack([per_chip[j][idx] for j in range(4)]))
        wfull[k] = ARRANGE[k](full) if k in ARRANGE else full

    small = {k: weights[k] for k in SMALL}
    offs, row = {}, 0
    for k, shp in zip(BIG, shard_shapes):
        offs[k] = row
        row += int(np.prod(shp)) // PACK_C
    loss, dx, grads, dg_final, packed = _local_step(x[0], loss_target[0], wfull, small, _sds((4, rows, PACK_C), F32), offs)

    small_local = {k: jnp.stack([grads[l][k].reshape(weights[k].shape[1:]) for l in range(DEPTH)])
                   for k in SMALL if k != "g_final"}
    small_local["g_final"] = dg_final.reshape(-1)
    small_shapes = [weights[k].shape for k in SMALL]
    n_small = sum(int(np.prod(s)) for s in small_shapes)
    flat = jnp.concatenate([small_local[k].reshape(-1) for k in SMALL] + [loss[0, :1]])
    rows_small = -(-(n_small + 1) // PACK_C)
    rows_small += -rows_small % 8
    flat = jnp.pad(flat, (0, rows_small * PACK_C - n_small - 1)).reshape(rows_small, PACK_C)
    red = _all_reduce_small(flat).reshape(-1)
    loss_out = red[n_small]
    small_grads, off = {}, 0
    for k, shp in zip(SMALL, small_shapes):
        n = int(np.prod(shp))
        small_grads[k] = red[off:off + n].reshape(shp)
        off += n

    rest = [k for k in BIG if k not in DIRECT]
    by_shard = {k: _shards_from_full(k, jnp.stack([grads[l][k] for l in range(DEPTH)])) for k in rest}
    tail = jnp.stack([_pack_rows([by_shard[k][j] for k in rest]) for j in range(4)])
    assert offs[rest[0]] + tail.shape[1] == rows
    packed = lax.dynamic_update_slice(packed, tail, (0, offs[rest[0]], 0))
    halves = packed.reshape(4, 2, rows // 2, PACK_C)
    pair = _pair_sum(halves, cc, _swap_other_half(halves))
    by_chip = _scatter_chips(pair)
    reduced = _add_n([jnp.where(my_chip == j, pair[j], by_chip[j]) for j in range(4)], name="chip_sum", out_dtype=F32)
    theirs = _swap_sibling(reduced)
    joined = jnp.where(cc == 0, jnp.concatenate([reduced, theirs]), jnp.concatenate([theirs, reduced]))
    big_grads = dict(zip(BIG, _unpack_rows(joined, shard_shapes)))

    out_g, out_d, out_m, out_v = {}, {}, {}, {}
    for k in BIG:
        shp = weights[k].shape
        two_d = (shp[0] * shp[1], shp[2])
        d, nm, nv = _adamw(weights[k].reshape(two_d), big_grads[k].reshape(two_d), moms[k].reshape(two_d),
                           vels[k].reshape(two_d), name="adamw_" + k)
        out_g[k], out_d[k], out_m[k], out_v[k] = big_grads[k], d.reshape(shp), nm.reshape(shp), nv.reshape(shp)

    def pack_small(tree):
        f = jnp.concatenate([tree[k].reshape(-1) for k in SMALL])
        return jnp.pad(f, (0, rows_small * PACK_C - n_small)).reshape(rows_small, PACK_C)

    d, nm, nv = _adamw(pack_small(small), pack_small(small_grads), pack_small(moms), pack_small(vels), name="adamw_small")
    for tree, flat_out in ((out_d, d), (out_m, nm), (out_v, nv)):
        off = 0
        fo = flat_out.reshape(-1)
        for k, shp in zip(SMALL, small_shapes):
            n = int(np.prod(shp))
            tree[k] = fo[off:off + n].reshape(shp)
            off += n
    out_g.update(small_grads)

    order = ("g_mix", "w_in", "q_norm", "w_uq", "kv_norm", "w_ukv", "rpb", "out_norm_a", "out_norm_b", "out_norm_c", "w_out",
             "g_mlp", "w_mlp_in", "w_mlp_out", "g_final")
    return (loss_out, dx.reshape(x.shape), *[out_g[k] for k in order], *[out_d[k] for k in order],
            *[out_m[k] for k in order], *[out_v[k] for k in order])
```

```python
import math

import numpy as np
import jax
import jax.numpy as jnp
from jax import lax
from jax.experimental import pallas as pl
from jax.experimental.pallas import tpu as pltpu

F32 = jnp.float32
BF16 = jnp.bfloat16

D_MODEL = 1024
HEAD_DIM = 64
Q_LORA = 256
KV_LORA = 128
QK_ROPE = 32
HEADS_A = 6
HEADS_B = 6
HEADS_C = 4
DILATED_PAIRS = ((128, 1), (512, 4), (2048, 16))
BAND_HALF = 64
GRID_W = 64
NA_ROWS = 8
NA_COLS = 16
D_FF = 4096
ROPE_THETA = 10000.0
NORM_EPS = 1e-6
NEG_INF = -1e30
DEPTH = 4

LANE = 128
PROJ_W = 2432
COL_CKV = 256
COL_KPE = 384
COL_B = 512
COL_C = 1664
W_A2 = 768
W_KV = W_A2 + 384
WIDTH_AB = 384
WIDTH_C = 256
SCALE_A = (HEAD_DIM + QK_ROPE) ** -0.5
SCALE_BC = HEAD_DIM ** -0.5

ADAM_LR = 0.001
ADAM_B1 = 0.9
ADAM_B2 = 0.999
ADAM_EPS = 1e-08
ADAM_WD = 0.01
ADAM_STEP = 10

VMEM_LIMIT = 56 * 1024 * 1024
MESH_T = pl.DeviceIdType.MESH


def _cp(*sem):
    return pltpu.CompilerParams(dimension_semantics=sem or None, vmem_limit_bytes=VMEM_LIMIT)


def _tile(n, cands):
    for c in cands:
        if n % c == 0:
            return c
    return n


def _sds(shape, dtype):
    return jax.ShapeDtypeStruct(shape, dtype)


ROW_TILE_BYTES = 12 * 1024 * 1024


def _row_tiles(row_bytes):
    return tuple(t for t in (2048, 1024, 512, 256, 128) if t * row_bytes <= ROW_TILE_BYTES or t <= 512)


def _mm_nn(a, b, *, name, out_dtype=F32, res=None):
    m, k = a.shape
    n = b.shape[1]
    tn = _tile(n, (1024, 768, 512)) if n % LANE == 0 and n != PROJ_W else n
    tm = _tile(m, _row_tiles(2 * k + tn * (jnp.dtype(out_dtype).itemsize + (4 if res is not None else 0))))

    def body(*refs):
        a_ref, b_ref = refs[0], refs[1]
        o_ref = refs[-1]
        acc = jnp.dot(a_ref[...], b_ref[...], preferred_element_type=F32)
        if res is not None:
            acc = refs[2][...] + acc
        o_ref[...] = acc.astype(o_ref.dtype)

    in_specs = [pl.BlockSpec((tm, k), lambda j, i: (i, 0)), pl.BlockSpec((k, tn), lambda j, i: (0, j))]
    args = [a, b]
    if res is not None:
        in_specs.append(pl.BlockSpec((tm, tn), lambda j, i: (i, j)))
        args.append(res)
    return pl.pallas_call(
        body, name=name, grid=(n // tn, m // tm), in_specs=in_specs,
        out_specs=pl.BlockSpec((tm, tn), lambda j, i: (i, j)), out_shape=_sds((m, n), out_dtype),
        compiler_params=_cp("parallel", "parallel"))(*args)


def _norm_mm(x, g, w, *, name, relu2):
    m, k = x.shape
    n = w.shape[1]
    tm = _tile(m, (512, 256, 128) if n <= PROJ_W else (256, 128))

    def body(x_ref, g_ref, w_ref, h_ref, o_ref):
        xv = x_ref[...]
        h = (xv * _rstd(xv) * g_ref[...]).astype(BF16)
        h_ref[...] = h
        acc = jnp.dot(h, w_ref[...], preferred_element_type=F32)
        if relu2:
            acc = jnp.square(jnp.maximum(acc, 0.0))
        o_ref[...] = acc.astype(o_ref.dtype)

    return pl.pallas_call(
        body, name=name, grid=(m // tm,),
        in_specs=[pl.BlockSpec((tm, k), lambda i: (i, 0)), pl.BlockSpec((1, k), lambda i: (0, 0)),
                  pl.BlockSpec((k, n), lambda i: (0, 0))],
        out_specs=(pl.BlockSpec((tm, k), lambda i: (i, 0)), pl.BlockSpec((tm, n), lambda i: (i, 0))),
        out_shape=(_sds((m, k), BF16), _sds((m, n), BF16 if relu2 else F32)),
        compiler_params=_cp("parallel"))(x, g.reshape(1, k), w)


def _mm_nt(a, b, *, name, out_dtype=F32, relu2_act=None):
    m, c = a.shape
    n = b.shape[0]
    tn = _tile(n, (1024, 512, 256, 128))
    tm = _tile(m, _row_tiles(2 * c + tn * (jnp.dtype(out_dtype).itemsize + (2 if relu2_act is not None else 0))))

    def body(*refs):
        a_ref, b_ref = refs[0], refs[1]
        o_ref = refs[-1]
        acc = lax.dot_general(a_ref[...], b_ref[...], (((1,), (1,)), ((), ())), preferred_element_type=F32)
        if relu2_act is not None:
            acc = acc * (2.0 * jnp.sqrt(refs[2][...].astype(F32)))
        o_ref[...] = acc.astype(o_ref.dtype)

    in_specs = [pl.BlockSpec((tm, c), lambda j, i: (i, 0)), pl.BlockSpec((tn, c), lambda j, i: (j, 0))]
    args = [a, b]
    if relu2_act is not None:
        in_specs.append(pl.BlockSpec((tm, tn), lambda j, i: (i, j)))
        args.append(relu2_act)
    return pl.pallas_call(
        body, name=name, grid=(n // tn, m // tm), in_specs=in_specs,
        out_specs=pl.BlockSpec((tm, tn), lambda j, i: (i, j)), out_shape=_sds((m, n), out_dtype),
        compiler_params=_cp("parallel", "parallel"))(*args)


def _mm_nt_norm_bwd(a, b, x, g, res, *, name):
    m, c = a.shape
    d = b.shape[0]
    tm = _tile(m, (512, 256, 128))

    def body(a_ref, b_ref, x_ref, g_ref, res_ref, dx_ref, dxb_ref, dg_ref):
        dy = lax.dot_general(a_ref[...], b_ref[...], (((1,), (1,)), ((), ())), preferred_element_type=F32)
        dx, dg = _rms_bwd_rows(x_ref[...], g_ref[...], dy)
        dx = res_ref[...] + dx
        dx_ref[...] = dx
        dxb_ref[...] = dx.astype(BF16)
        _accum(dg_ref, dg, pl.program_id(0) == 0)

    row = pl.BlockSpec((tm, d), lambda i: (i, 0))
    fix = pl.BlockSpec((1, d), lambda i: (0, 0))
    return pl.pallas_call(
        body, name=name, grid=(m // tm,),
        in_specs=[pl.BlockSpec((tm, c), lambda i: (i, 0)), pl.BlockSpec((d, c), lambda i: (0, 0)), row, fix, row],
        out_specs=(row, row, fix), out_shape=(_sds((m, d), F32), _sds((m, d), BF16), _sds((1, d), F32)),
        compiler_params=_cp("arbitrary"))(a, b, x, g.reshape(1, d), res)


def _mm_tn(a, b, *, name, packed=None):
    m, ka = a.shape
    nb = b.shape[1]
    tka = _tile(ka, (512, 256, 128)) if packed is None else packed[1]
    tnb = _tile(nb, (1024, 768, 512)) if nb != PROJ_W else nb
    tc = _tile(m, (4096, 2048, 1024, 512, 256, 128) if tnb <= PACK_C else (2048, 1024, 512, 256, 128))

    def body(*refs):
        a_ref, b_ref, o_ref = refs[0], refs[1], refs[-1]
        part = lax.dot_general(a_ref[...], b_ref[...], (((0,), (0,)), ((), ())), preferred_element_type=F32)

        @pl.when(pl.program_id(2) == 0)
        def _():
            o_ref[...] = part

        @pl.when(pl.program_id(2) != 0)
        def _():
            o_ref[...] += part

    in_specs = [pl.BlockSpec((tc, tka), lambda i, j, c: (c, i)), pl.BlockSpec((tc, tnb), lambda i, j, c: (c, j))]
    kwargs = dict(out_specs=pl.BlockSpec((tka, tnb), lambda i, j, c: (i, j)), out_shape=_sds((ka, nb), F32))
    args = [a, b]
    if packed is not None:
        buf, _, place = packed
        assert tnb == PACK_C
        kwargs = dict(out_specs=pl.BlockSpec((None, tka, tnb), lambda i, j, c: place(i, j) + (0,)))
        if isinstance(buf, jax.ShapeDtypeStruct):
            kwargs["out_shape"] = buf
        else:
            kwargs.update(out_shape=_sds(buf.shape, buf.dtype), input_output_aliases={2: 0})
            in_specs.append(pl.BlockSpec(memory_space=pl.ANY))
            args.append(buf)
    return pl.pallas_call(
        body, name=name, grid=(ka // tka, nb // tnb, m // tc), in_specs=in_specs,
        compiler_params=_cp("parallel", "parallel", "arbitrary"), **kwargs)(*args)


def _rstd(x):
    return lax.rsqrt(jnp.mean(x * x, axis=-1, keepdims=True) + NORM_EPS)


def _rms_bwd_rows(x, g, dy):
    r = _rstd(x)
    gy = dy * g
    c = jnp.sum(x * gy, axis=-1, keepdims=True) * (r * r * r) * (1.0 / x.shape[-1])
    return r * gy - x * c, jnp.sum(dy * x * r, axis=0, keepdims=True)


def _accum(ref, part, first):
    @pl.when(first)
    def _():
        ref[...] = part

    @pl.when(jnp.logical_not(first))
    def _():
        ref[...] += part


def _rope(x, c, s1, s2, sh):
    return x * c + pltpu.roll(x, LANE - sh, 1) * s1 + pltpu.roll(x, sh, 1) * s2


def _rope_t(g, c, s1, s2, sh):
    return g * c + pltpu.roll(g * s1, sh, 1) + pltpu.roll(g * s2, LANE - sh, 1)


def _rope_tables(s, half, reps, lead=0):
    pos = jnp.arange(s, dtype=F32)
    inv_freq = ROPE_THETA ** (-jnp.arange(half, dtype=F32) / half)
    ang = pos[:, None] * inv_freq[None, :]
    cos, sin = jnp.cos(ang), jnp.sin(ang)
    zero = jnp.zeros_like(cos)
    ones, lead0 = jnp.ones((s, lead), F32), jnp.zeros((s, lead), F32)
    pad = jnp.zeros((s, LANE - lead - 2 * half * reps), F32)
    c = jnp.concatenate([ones] + [cos, cos] * reps + [pad], axis=1)
    s1 = jnp.concatenate([lead0] + [-sin, zero] * reps + [pad], axis=1)
    s2 = jnp.concatenate([lead0] + [zero, sin] * reps + [pad], axis=1)
    return c, s1, s2


def _lane_lt64(shape):
    return lax.broadcasted_iota(jnp.int32, shape, len(shape) - 1) % LANE < HEAD_DIM


def _group_sum(x):
    outs = []
    for b in range(x.shape[1] // LANE):
        blk = x[:, b * LANE:(b + 1) * LANE]
        lo = _lane_lt64(blk.shape)
        s0 = jnp.sum(jnp.where(lo, blk, 0.0), axis=1, keepdims=True)
        s1 = jnp.sum(jnp.where(lo, 0.0, blk), axis=1, keepdims=True)
        outs.append(jnp.where(lo, s0, s1))
    return outs


def _row_spec(ts, w):
    return pl.BlockSpec((ts, w), lambda i: (i, 0))


def _fix_spec(w):
    return pl.BlockSpec((1, w), lambda i: (0, 0))


def _loss_head(x, g, target):
    s, d = x.shape
    ts = _tile(s, (512, 256, 128))

    def body(x_ref, g_ref, t_ref, loss_ref, dx_ref, dxb_ref, dg_ref):
        xv, gv = x_ref[...], g_ref[...]
        err = xv * _rstd(xv) * gv - t_ref[...]
        part = 0.5 * jnp.sum(jnp.sum(err * err, axis=-1, keepdims=True) * (1.0 / d), axis=0, keepdims=True)
        dx, dg = _rms_bwd_rows(xv, gv, err * (1.0 / d))
        dx_ref[...] = dx
        dxb_ref[...] = dx.astype(BF16)
        first = pl.program_id(0) == 0
        _accum(dg_ref, dg, first)
        _accum(loss_ref, jnp.broadcast_to(part, (1, LANE)), first)

    return pl.pallas_call(
        body, name="loss_head", grid=(s // ts,), in_specs=[_row_spec(ts, d), _fix_spec(d), _row_spec(ts, d)],
        out_specs=(_fix_spec(LANE), _row_spec(ts, d), _row_spec(ts, d), _fix_spec(d)),
        out_shape=(_sds((1, LANE), F32), _sds((s, d), F32), _sds((s, d), BF16), _sds((1, d), F32)),
        compiler_params=_cp("arbitrary"))(x, g.reshape(1, d), target)


def _prep_fwd(proj, q_norm, kv_norm, t32, t64):
    s = proj.shape[0]
    ts = _tile(s, (256, 128))

    def body(p_ref, qn_ref, kn_ref, c32, a32, b32, c64, a64, b64, cqn_ref, ckvn_ref, kpe_ref, qkvb_ref, qkvc_ref):
        cq = p_ref[:, 0:Q_LORA]
        cqn_ref[...] = (cq * _rstd(cq) * qn_ref[...]).astype(BF16)
        ckv = p_ref[:, COL_CKV:COL_KPE]
        ckvn_ref[...] = (ckv * _rstd(ckv) * kn_ref[...]).astype(BF16)
        kp = pltpu.roll(p_ref[:, COL_KPE:COL_B], HEAD_DIM, 1)
        kpe_ref[...] = _rope(kp, c32[...], a32[...], b32[...], QK_ROPE // 2).astype(BF16)
        for b in range(6):
            blk = _rope(p_ref[:, COL_B + b * LANE:COL_B + (b + 1) * LANE], c64[...], a64[...], b64[...], HEAD_DIM // 2)
            if b < 3:
                blk = blk * SCALE_BC
            qkvb_ref[:, b * LANE:(b + 1) * LANE] = blk.astype(BF16)
        qkvb_ref[:, 2 * WIDTH_AB:3 * WIDTH_AB] = p_ref[:, COL_B + 2 * WIDTH_AB:COL_C].astype(BF16)
        qkvc_ref[:, 0:WIDTH_C] = (p_ref[:, COL_C:COL_C + WIDTH_C] * SCALE_BC).astype(BF16)
        qkvc_ref[:, WIDTH_C:3 * WIDTH_C] = p_ref[:, COL_C + WIDTH_C:PROJ_W].astype(BF16)

    tab = [_row_spec(ts, LANE)] * 6
    return pl.pallas_call(
        body, name="prep_fwd", grid=(s // ts,),
        in_specs=[_row_spec(ts, PROJ_W), _fix_spec(Q_LORA), _fix_spec(KV_LORA)] + tab,
        out_specs=(_row_spec(ts, Q_LORA), _row_spec(ts, KV_LORA), _row_spec(ts, LANE), _row_spec(ts, 3 * WIDTH_AB),
                   _row_spec(ts, 3 * WIDTH_C)),
        out_shape=(_sds((s, Q_LORA), BF16), _sds((s, KV_LORA), BF16), _sds((s, LANE), BF16),
                   _sds((s, 3 * WIDTH_AB), BF16), _sds((s, 3 * WIDTH_C), BF16)),
        compiler_params=_cp("parallel"))(proj, q_norm.reshape(1, -1), kv_norm.reshape(1, -1), *t32, *t64)


def _prep_bwd(proj, q_norm, kv_norm, t32, t64, dcqn, dckvn, dkpe, db, dc):
    s = proj.shape[0]
    ts = _tile(s, (256, 128))

    def body(p_ref, qn_ref, kn_ref, c32, a32, b32, c64, a64, b64, dcqn_ref, dckvn_ref, dkpe_ref, *rest):
        db_refs, dc_refs = rest[0:9], rest[9:12]
        dp_ref, dqn_ref, dkn_ref = rest[12:15]
        first = pl.program_id(0) == 0
        dx, dg = _rms_bwd_rows(p_ref[:, 0:Q_LORA], qn_ref[...], dcqn_ref[...])
        dp_ref[:, 0:Q_LORA] = dx.astype(BF16)
        _accum(dqn_ref, dg, first)
        dx, dg = _rms_bwd_rows(p_ref[:, COL_CKV:COL_KPE], kn_ref[...], dckvn_ref[...])
        dp_ref[:, COL_CKV:COL_KPE] = dx.astype(BF16)
        _accum(dkn_ref, dg, first)
        g = pltpu.roll(_rope_t(dkpe_ref[...], c32[...], a32[...], b32[...], QK_ROPE // 2), LANE - HEAD_DIM, 1)
        lane = lax.broadcasted_iota(jnp.int32, g.shape, 1)
        dp_ref[:, COL_KPE:COL_B] = jnp.where(lane < QK_ROPE, g, 0.0).astype(BF16)
        for which in range(3):
            for b in range(3):
                sl = slice(b * LANE, (b + 1) * LANE)
                g = db_refs[which][:, sl] + db_refs[3 + which][:, sl] + db_refs[6 + which][:, sl]
                if which < 2:
                    g = _rope_t(g, c64[...], a64[...], b64[...], HEAD_DIM // 2)
                if which == 0:
                    g = g * SCALE_BC
                col = COL_B + which * WIDTH_AB + b * LANE
                dp_ref[:, col:col + LANE] = g.astype(BF16)
        dp_ref[:, COL_C:COL_C + WIDTH_C] = (dc_refs[0][...] * SCALE_BC).astype(BF16)
        dp_ref[:, COL_C + WIDTH_C:COL_C + 2 * WIDTH_C] = dc_refs[1][...].astype(BF16)
        dp_ref[:, COL_C + 2 * WIDTH_C:PROJ_W] = dc_refs[2][...].astype(BF16)

    tab = [_row_spec(ts, LANE)] * 6
    in_specs = ([_row_spec(ts, PROJ_W), _fix_spec(Q_LORA), _fix_spec(KV_LORA)] + tab
                + [_row_spec(ts, Q_LORA), _row_spec(ts, KV_LORA), _row_spec(ts, LANE)]
                + [_row_spec(ts, WIDTH_AB)] * 9 + [_row_spec(ts, WIDTH_C)] * 3)
    return pl.pallas_call(
        body, name="prep_bwd", grid=(s // ts,), in_specs=in_specs,
        out_specs=(_row_spec(ts, PROJ_W), _fix_spec(Q_LORA), _fix_spec(KV_LORA)),
        out_shape=(_sds((s, PROJ_W), BF16), _sds((1, Q_LORA), F32), _sds((1, KV_LORA), F32)),
        compiler_params=_cp("arbitrary"))(proj, q_norm.reshape(1, -1), kv_norm.reshape(1, -1), *t32, *t64,
                                          dcqn, dckvn, dkpe, *db, *dc)


def _a_post_fwd(qa, kva, kpe, t32):
    s = qa.shape[0]
    ts = _tile(s, (512, 256, 128))

    def body(qa_ref, kva_ref, kpe_ref, c32, a32, b32, q_ref, k_ref, kt_ref, v_ref):
        kpe = kpe_ref[...].astype(F32)
        for h in range(HEADS_A):
            hb = slice(h * LANE, (h + 1) * LANE)
            q_ref[:, hb] = _rope(qa_ref[:, hb], c32[...], a32[...], b32[...], QK_ROPE // 2).astype(BF16)
            kh = kva_ref[:, hb] + kpe
            k_ref[:, hb] = kh.astype(BF16)
            kt_ref[hb, :] = kh.T.astype(BF16)
        for p in range(3):
            lo, hi = 2 * p * LANE, (2 * p + 1) * LANE
            v_ref[:, lo:hi] = kva_ref[:, W_A2 + p * LANE:W_A2 + (p + 1) * LANE].astype(BF16)
            v_ref[:, hi:hi + LANE] = jnp.ones((ts, LANE), BF16)

    return pl.pallas_call(
        body, name="a_post_fwd", grid=(s // ts,),
        in_specs=[_row_spec(ts, W_A2), _row_spec(ts, W_KV), _row_spec(ts, LANE)] + [_row_spec(ts, LANE)] * 3,
        out_specs=(_row_spec(ts, W_A2), _row_spec(ts, W_A2), pl.BlockSpec((W_A2, ts), lambda i: (0, i)), _row_spec(ts, W_A2)),
        out_shape=(_sds((s, W_A2), BF16), _sds((s, W_A2), BF16), _sds((W_A2, s), BF16), _sds((s, W_A2), BF16)),
        compiler_params=_cp("parallel"))(qa, kva, kpe, *t32)


def _a_post_bwd(dqa2_t, dka2, dva, t32):
    s = dka2.shape[0]
    ts = _tile(s, (512, 256, 128))

    def body(dqt_ref, dk_ref, dv_ref, c32, a32, b32, dqa_ref, dkva_ref, dkpe_ref):
        acc = None
        for h in range(HEADS_A):
            hb = slice(h * LANE, (h + 1) * LANE)
            dqa_ref[:, hb] = _rope_t(dqt_ref[hb, :].T * SCALE_A, c32[...], a32[...], b32[...], QK_ROPE // 2).astype(BF16)
            part = dk_ref[:, hb] * SCALE_A
            dkva_ref[:, hb] = part.astype(BF16)
            acc = part if acc is None else acc + part
        dkva_ref[:, W_A2:W_KV] = dv_ref[...].astype(BF16)
        dkpe_ref[...] = acc

    return pl.pallas_call(
        body, name="a_post_bwd", grid=(s // ts,),
        in_specs=[pl.BlockSpec((W_A2, ts), lambda i: (0, i)), _row_spec(ts, W_A2), _row_spec(ts, WIDTH_AB)]
        + [_row_spec(ts, LANE)] * 3,
        out_specs=(_row_spec(ts, W_A2), _row_spec(ts, W_KV), _row_spec(ts, LANE)),
        out_shape=(_sds((s, W_A2), BF16), _sds((s, W_KV), BF16), _sds((s, LANE), F32)),
        compiler_params=_cp("parallel"))(dqa2_t, dka2, dva, *t32)


def _pair_masks():
    lane = lax.broadcasted_iota(jnp.int32, (1, LANE), 1)
    return lane < HEAD_DIM, lane >= HEAD_DIM


def _nt(a, b):
    return lax.dot_general(a, b, (((1,), (1,)), ((), ())), preferred_element_type=F32)


def _tn(a, b):
    return lax.dot_general(a, b, (((0,), (0,)), ((), ())), preferred_element_type=F32)


def _stack_heads(x):
    m0, m1 = _pair_masks()
    zero = jnp.zeros_like(x)
    return jnp.concatenate([jnp.where(m0, x, zero), jnp.where(m1, x, zero)], axis=0)


def _stack_stat(x):
    return jnp.concatenate([x[:, 0:1], x[:, HEAD_DIM:HEAD_DIM + 1]], axis=0)


def _softmax_pair(q, kk, vv, bias2):
    t = q.shape[0]
    s = _nt(_stack_heads(q), kk) + bias2
    m = jnp.max(s, axis=1, keepdims=True)
    p = jnp.exp(s - m)
    l = jnp.sum(p, axis=1, keepdims=True)
    o2 = jnp.dot(p.astype(BF16), vv, preferred_element_type=F32) / l
    lse2 = m + jnp.log(l)
    lo = _lane_lt64((t, LANE))
    return jnp.where(lo, o2[:t], o2[t:]), jnp.where(lo, lse2[:t], lse2[t:])


def _softmax_pair_bwd(q, kk, vv, do, lse, delta, bias2):
    t = q.shape[0]
    q2, do2 = _stack_heads(q), _stack_heads(do)
    p = jnp.exp(_nt(q2, kk) + bias2 - _stack_stat(lse))
    ds = p * (_nt(do2, vv) - _stack_stat(delta))
    dsb = ds.astype(BF16)
    dq2 = jnp.dot(dsb, kk, preferred_element_type=F32)
    lo = _lane_lt64((t, LANE))
    return jnp.where(lo, dq2[:t], dq2[t:]), _tn(dsb, q2), _tn(p.astype(BF16), do2), ds


DENSE_FWD_TQ, DENSE_FWD_TK = 512, 8192
DENSE_BWD_TQ, DENSE_BWD_TK = 2048, 1024
LOG2E = math.log2(math.e)


def _dense_fwd(qa, ka, va1):
    s = qa.shape[0]
    tq, tk = min(DENSE_FWD_TQ, s), min(DENSE_FWD_TK, s)
    nk = s // tk
    c = SCALE_A * LOG2E

    def body(q_ref, k_ref, v_ref, o_ref, lse_ref, m_sc, acc_sc):
        j = pl.program_id(2)

        @pl.when(j == 0)
        def _():
            m_sc[...] = jnp.full(m_sc.shape, NEG_INF, F32)
            acc_sc[...] = jnp.zeros(acc_sc.shape, F32)

        vv = v_ref[...]
        for hh in range(2):
            hs = slice(hh * LANE, (hh + 1) * LANE)
            sc = _nt(q_ref[:, hs], k_ref[:, hs])
            m_prev = m_sc[hh]
            m_new = jnp.maximum(m_prev, jnp.max(sc, axis=1, keepdims=True))
            alpha = jnp.exp2((m_prev - m_new) * c)
            p = jnp.exp2((sc - m_new) * c)
            acc_sc[hh] = alpha * acc_sc[hh] + jnp.dot(p.astype(BF16), vv, preferred_element_type=F32)
            m_sc[hh] = m_new

        @pl.when(j == nk - 1)
        def _():
            lo = _lane_lt64((tq, LANE))
            a0, a1 = acc_sc[0], acc_sc[1]
            l0, l1 = a0[:, LANE:], a1[:, LANE:]
            o_ref[...] = jnp.where(lo, a0[:, :LANE] / l0, a1[:, :LANE] / l1)
            lse_ref[0] = _stat_rows(jnp.where(lo, m_sc[0] * SCALE_A + jnp.log(l0), m_sc[1] * SCALE_A + jnp.log(l1)))

    return pl.pallas_call(
        body, name="dense_fwd", grid=(3, s // tq, nk),
        in_specs=[pl.BlockSpec((tq, 2 * LANE), lambda p, i, j: (i, p)), pl.BlockSpec((tk, 2 * LANE), lambda p, i, j: (j, p)),
                  pl.BlockSpec((tk, 2 * LANE), lambda p, i, j: (j, p))],
        out_specs=(pl.BlockSpec((tq, LANE), lambda p, i, j: (i, p)), pl.BlockSpec((1, 8, tq), lambda p, i, j: (p, 0, i))),
        out_shape=(_sds((s, WIDTH_AB), F32), _sds((3, 8, s), F32)),
        scratch_shapes=[pltpu.VMEM((2, tq, 1), F32), pltpu.VMEM((2, tq, 2 * LANE), F32)],
        compiler_params=_cp("parallel", "parallel", "arbitrary"))(qa, ka, va1)


def _stat_rows(lane_dense):
    tr = lane_dense.T
    return jnp.concatenate([tr[0:1, :], tr[HEAD_DIM:HEAD_DIM + 1, :], jnp.zeros((6, tr.shape[1]), F32)], axis=0)


def _dense_bwd(qa, ka, kat, va1, do, lse_rows, delta_rows):
    s = qa.shape[0]
    tq, tk = min(DENSE_BWD_TQ, s), min(DENSE_BWD_TK, s)
    c = SCALE_A * LOG2E

    def body(q_ref, k_ref, kt_ref, v_ref, do_ref, lse_ref, dl_ref, dqt_ref, dk_ref, dv_ref):
        j, i = pl.program_id(1), pl.program_id(2)

        @pl.when((j == 0) & (i == 0))
        def _():
            dqt_ref[...] = jnp.zeros(dqt_ref.shape, F32)

        vv, do_ = v_ref[...], do_ref[...]
        lse_t, dl_t = lse_ref[0] * LOG2E, dl_ref[0]
        vm = _pair_masks()
        cols = pl.ds(pl.multiple_of(i * tq, tq), tq)
        dv = None
        for hh in range(2):
            hs = slice(hh * LANE, (hh + 1) * LANE)
            qh = q_ref[:, hs]
            dom = jnp.where(vm[hh], do_, jnp.zeros_like(do_))
            pt = jnp.exp2(_nt(k_ref[:, hs], qh) * c - lse_t[hh:hh + 1, :])
            dst = pt * (_nt(vv, dom) - dl_t[hh:hh + 1, :])
            pb, dsb = pt.astype(BF16), dst.astype(BF16)
            dv_h = jnp.dot(pb, dom, preferred_element_type=F32)
            dv = dv_h if dv is None else dv + dv_h
            dqt_ref[hs, cols] += jnp.dot(kt_ref[hs, :], dsb, preferred_element_type=F32)
            dk_h = jnp.dot(dsb, qh, preferred_element_type=F32)

            @pl.when(i == 0)
            def _():
                dk_ref[:, hs] = dk_h

            @pl.when(i != 0)
            def _():
                dk_ref[:, hs] += dk_h
        _accum(dv_ref, dv, i == 0)

    st_spec = pl.BlockSpec((1, 8, tq), lambda p, j, i: (p, 0, i))
    return pl.pallas_call(
        body, name="dense_bwd", grid=(3, s // tk, s // tq),
        in_specs=[pl.BlockSpec((tq, 2 * LANE), lambda p, j, i: (i, p)), pl.BlockSpec((tk, 2 * LANE), lambda p, j, i: (j, p)),
                  pl.BlockSpec((2 * LANE, tk), lambda p, j, i: (p, j)), pl.BlockSpec((tk, LANE), lambda p, j, i: (j, 2 * p)),
                  pl.BlockSpec((tq, LANE), lambda p, j, i: (i, p)), st_spec, st_spec],
        out_specs=(pl.BlockSpec((2 * LANE, s), lambda p, j, i: (p, 0)), pl.BlockSpec((tk, 2 * LANE), lambda p, j, i: (j, p)),
                   pl.BlockSpec((tk, LANE), lambda p, j, i: (j, p))),
        out_shape=(_sds((W_A2, s), F32), _sds((s, W_A2), F32), _sds((s, WIDTH_AB), F32)),
        compiler_params=_cp("parallel", "arbitrary", "arbitrary"))(qa, ka, kat, va1, do, lse_rows, delta_rows)


BAND_TILE = 1024
BAND_SUB = 128
QKV_W = 3 * WIDTH_AB


def _band_bias_table():
    row = np.arange(BAND_SUB)[:, None]
    col = np.arange(2 * BAND_SUB)[None, :]
    band = np.abs(row - col + BAND_HALF) <= BAND_HALF
    variants = []
    for idx in range(4):
        ok = band & ((col >= BAND_HALF) | ((idx & 1) == 0)) & ((col < 2 * BAND_SUB - BAND_HALF) | ((idx & 2) == 0))
        one = np.where(ok, 0.0, NEG_INF).astype(np.float32)
        variants.append(np.concatenate([one, one], axis=0))
    return jnp.asarray(np.stack(variants))


def _band_specs(t, n):
    hpt = t // BAND_HALF
    last = n // BAND_HALF - 1
    return [pl.BlockSpec((BAND_HALF, QKV_W), lambda r, i: (jnp.maximum(i * hpt - 1, 0), r)),
            pl.BlockSpec((t, QKV_W), lambda r, i: (i, r)),
            pl.BlockSpec((BAND_HALF, QKV_W), lambda r, i: (jnp.minimum((i + 1) * hpt, last), r)),
            pl.BlockSpec((4, 2 * BAND_SUB, 2 * BAND_SUB), lambda r, i: (0, 0, 0))]


def _band_bias(b_ref, a, nsub, i, nt):
    idx = 0
    if a == 0:
        idx = idx + (i == 0).astype(jnp.int32)
    if a == nsub - 1:
        idx = idx + 2 * (i == nt - 1).astype(jnp.int32)
    return b_ref[idx]


def _band_kv(left, main, right, p):
    kc = slice(WIDTH_AB + p * LANE, WIDTH_AB + (p + 1) * LANE)
    vc = slice(2 * WIDTH_AB + p * LANE, 2 * WIDTH_AB + (p + 1) * LANE)
    return (jnp.concatenate([left[:, kc], main[:, kc], right[:, kc]], axis=0),
            jnp.concatenate([left[:, vc], main[:, vc], right[:, vc]], axis=0))


def _banded_fwd(qkvb, dil, bias):
    s = qkvb.shape[0]
    n = s // dil
    t = min(n, BAND_TILE)
    nsub, nt = t // BAND_SUB, n // t
    view = qkvb.reshape(n, dil * QKV_W)

    def body(left, main, right, b_ref, o_ref, lse_ref):
        i = pl.program_id(1)
        for p in range(3):
            pc = slice(p * LANE, (p + 1) * LANE)
            kk, vv = _band_kv(left, main, right, p)
            for a in range(nsub):
                rows, win = slice(a * BAND_SUB, (a + 1) * BAND_SUB), slice(a * BAND_SUB, (a + 2) * BAND_SUB)
                o, lse = _softmax_pair(main[rows, pc], kk[win], vv[win], _band_bias(b_ref, a, nsub, i, nt))
                o_ref[rows, pc] = o
                lse_ref[rows, pc] = lse

    o_spec = pl.BlockSpec((t, WIDTH_AB), lambda r, i: (i, r))
    o, lse = pl.pallas_call(
        body, name=f"banded_fwd_d{dil}", grid=(dil, nt), in_specs=_band_specs(t, n), out_specs=(o_spec, o_spec),
        out_shape=(_sds((n, dil * WIDTH_AB), F32), _sds((n, dil * WIDTH_AB), F32)),
        compiler_params=_cp("parallel", "parallel"))(view, view, view, bias)
    return o.reshape(s, WIDTH_AB), lse.reshape(s, WIDTH_AB)


def _banded_bwd(qkvb, do, lse, delta, dil, bias):
    s = qkvb.shape[0]
    n = s // dil
    t = min(n, BAND_TILE)
    nsub, nt = t // BAND_SUB, n // t
    view = qkvb.reshape(n, dil * QKV_W)
    side = [a.reshape(n, dil * WIDTH_AB) for a in (do, lse, delta)]

    def body(left, main, right, b_ref, do_ref, lse_ref, dl_ref, dq_ref, dk_ref, dv_ref):
        i = pl.program_id(1)

        @pl.when(i == 0)
        def _():
            dk_ref[...] = jnp.zeros(dk_ref.shape, F32)
            dv_ref[...] = jnp.zeros(dv_ref.shape, F32)

        lrow = pl.multiple_of(jnp.maximum(i * t - BAND_HALF, 0), BAND_HALF)
        rrow = pl.multiple_of(jnp.minimum((i + 1) * t, n - BAND_HALF), BAND_HALF)
        mrow = pl.multiple_of(i * t, BAND_HALF)
        for p in range(3):
            pc = slice(p * LANE, (p + 1) * LANE)
            kk, vv = _band_kv(left, main, right, p)
            parts = []
            for a in range(nsub):
                rows, win = slice(a * BAND_SUB, (a + 1) * BAND_SUB), slice(a * BAND_SUB, (a + 2) * BAND_SUB)
                dq, dk, dv, _ = _softmax_pair_bwd(main[rows, pc], kk[win], vv[win], do_ref[rows, pc], lse_ref[rows, pc],
                                                  dl_ref[rows, pc], _band_bias(b_ref, a, nsub, i, nt))
                dq_ref[rows, pc] = dq
                parts.append((dk, dv))
            for which, ref in ((0, dk_ref), (1, dv_ref)):
                chunks = []
                for c in range(nsub + 1):
                    g = parts[c][which][:BAND_SUB] if c < nsub else None
                    if c >= 1:
                        h = parts[c - 1][which][BAND_SUB:]
                        g = h if g is None else g + h
                    chunks.append(g)
                mid = jnp.concatenate([chunks[0][BAND_HALF:]] + chunks[1:nsub] + [chunks[nsub][:BAND_HALF]], axis=0)
                ref[pl.ds(lrow, BAND_HALF), pc] += chunks[0][:BAND_HALF]
                ref[pl.ds(mrow, t), pc] += mid
                ref[pl.ds(rrow, BAND_HALF), pc] += chunks[nsub][BAND_HALF:]

    q_spec = pl.BlockSpec((t, WIDTH_AB), lambda r, i: (i, r))
    acc_spec = pl.BlockSpec((n, WIDTH_AB), lambda r, i: (0, r))
    shp = _sds((n, dil * WIDTH_AB), F32)
    outs = pl.pallas_call(
        body, name=f"banded_bwd_d{dil}", grid=(dil, nt), in_specs=_band_specs(t, n) + [q_spec, q_spec, q_spec],
        out_specs=(q_spec, acc_spec, acc_spec), out_shape=(shp, shp, shp),
        compiler_params=_cp("parallel", "arbitrary"))(view, view, view, bias, *side)
    return [a.reshape(s, WIDTH_AB) for a in outs]


def _na_geometry(s):
    rows = s // GRID_W
    assert rows >= 2 * NA_ROWS and rows % NA_ROWS == 0
    return rows, rows // NA_ROWS


def _na_row(n, i, rows):
    rq = n * NA_ROWS + i
    rs = jnp.clip(rq - NA_ROWS // 2, 0, rows - NA_ROWS)
    return pl.multiple_of(rs * GRID_W, GRID_W), rs - rq + NA_ROWS - 1


NA_KEYS = NA_ROWS * GRID_W


def _natten_fwd(qkvc, tfull):
    s = qkvc.shape[0]
    rows, nrb = _na_geometry(s)
    tq = NA_ROWS * GRID_W

    def body(q_ref, k_ref, v_ref, t_ref, o_ref, lse_ref):
        n = pl.program_id(1)
        for i in range(NA_ROWS):
            tok, base = _na_row(n, i, rows)
            kk, vv = k_ref[pl.ds(tok, NA_KEYS), :], v_ref[pl.ds(tok, NA_KEYS), :]
            sl = slice(i * GRID_W, (i + 1) * GRID_W)
            bias2 = jnp.concatenate([t_ref[0, base], t_ref[1, base]], axis=0)
            o, lse = _softmax_pair(q_ref[sl, :], kk, vv, bias2)
            o_ref[sl, :] = o
            lse_ref[sl, :] = lse

    o_spec = pl.BlockSpec((tq, LANE), lambda p, n: (n, p))
    return pl.pallas_call(
        body, name="natten_fwd", grid=(2, nrb),
        in_specs=[pl.BlockSpec((tq, LANE), lambda p, n: (n, p)), pl.BlockSpec((s, LANE), lambda p, n: (0, 2 + p)),
                  pl.BlockSpec((s, LANE), lambda p, n: (0, 4 + p)),
                  pl.BlockSpec((2, NA_ROWS, GRID_W, NA_KEYS), lambda p, n: (p, 0, 0, 0))],
        out_specs=(o_spec, o_spec), out_shape=(_sds((s, WIDTH_C), F32), _sds((s, WIDTH_C), F32)),
        compiler_params=_cp("parallel", "parallel"))(qkvc, qkvc, qkvc, tfull)


def _natten_bwd(qkvc, tfull, do, lse, delta):
    s = qkvc.shape[0]
    rows, nrb = _na_geometry(s)
    tq = NA_ROWS * GRID_W

    def body(q_ref, k_ref, v_ref, t_ref, do_ref, lse_ref, dl_ref, dq_ref, dk_ref, dv_ref, dt_ref):
        n = pl.program_id(1)

        @pl.when(n == 0)
        def _():
            dk_ref[...] = jnp.zeros(dk_ref.shape, F32)
            dv_ref[...] = jnp.zeros(dv_ref.shape, F32)
            dt_ref[...] = jnp.zeros(dt_ref.shape, F32)

        for i in range(NA_ROWS):
            tok, base = _na_row(n, i, rows)
            win = pl.ds(tok, NA_KEYS)
            sl = slice(i * GRID_W, (i + 1) * GRID_W)
            bias2 = jnp.concatenate([t_ref[0, base], t_ref[1, base]], axis=0)
            dq, dk, dv, ds = _softmax_pair_bwd(q_ref[sl, :], k_ref[win, :], v_ref[win, :], do_ref[sl, :], lse_ref[sl, :],
                                               dl_ref[sl, :], bias2)
            dq_ref[sl, :] = dq
            dk_ref[win, :] += dk
            dv_ref[win, :] += dv
            dt_ref[0, base] += ds[:GRID_W]
            dt_ref[1, base] += ds[GRID_W:]

    q_spec = pl.BlockSpec((tq, LANE), lambda p, n: (n, p))
    acc_spec = pl.BlockSpec((s, LANE), lambda p, n: (0, p))
    t_spec = pl.BlockSpec((2, NA_ROWS, GRID_W, NA_KEYS), lambda p, n: (p, 0, 0, 0))
    shp = _sds((s, WIDTH_C), F32)
    return pl.pallas_call(
        body, name="natten_bwd", grid=(2, nrb),
        in_specs=[q_spec, pl.BlockSpec((s, LANE), lambda p, n: (0, 2 + p)), pl.BlockSpec((s, LANE), lambda p, n: (0, 4 + p)),
                  t_spec, q_spec, q_spec, q_spec],
        out_specs=(q_spec, acc_spec, acc_spec, t_spec),
        out_shape=(shp, shp, shp, _sds((HEADS_C, NA_ROWS, GRID_W, NA_KEYS), F32)),
        compiler_params=_cp("parallel", "arbitrary"))(qkvc, qkvc, qkvc, tfull, do, lse, delta)


def _rpb_constants():
    p = np.arange(GRID_W)[:, None]
    qc = np.arange(GRID_W)[None, :]
    dc = np.clip(qc - p, -(NA_COLS - 1), NA_COLS - 1) + NA_COLS - 1
    onehot = (dc.reshape(1, -1) == np.arange(32)[:, None]).astype(np.float32)
    c_start = np.clip(p - NA_COLS // 2, 0, GRID_W - NA_COLS)
    col_ok = ((qc >= c_start) & (qc < c_start + NA_COLS)).reshape(1, -1).astype(np.float32)
    a = np.arange(16)[:, None]
    bj = np.arange(64)[None, :]
    row_sel = ((bj // 8 + bj % 8) == a).astype(np.float32)
    return jnp.asarray(onehot), jnp.asarray(col_ok), jnp.asarray(row_sel)


def _rpb_expand(rpb, onehot, col_ok):
    r2 = jnp.pad(rpb.reshape(HEADS_C * 15, 31), ((0, 4), (0, 1)))

    def body(r_ref, oh_ref, ok_ref, o_ref):
        t = jnp.dot(r_ref[...], oh_ref[...], preferred_element_type=F32, precision=lax.Precision.HIGHEST)
        o_ref[...] = jnp.where(ok_ref[...] > 0.5, t, NEG_INF)

    tm = pl.pallas_call(body, name="rpb_expand", out_shape=_sds((64, GRID_W * GRID_W), F32))(r2, onehot, col_ok)
    tm = tm[:HEADS_C * 15].reshape(HEADS_C, 15, GRID_W, GRID_W)
    tfull = jnp.stack([jnp.concatenate([tm[:, base + j] for j in range(NA_ROWS)], axis=-1) for base in range(NA_ROWS)], axis=1)
    return tfull


def _rpb_grad(dtfull, onehot, row_sel):
    g = dtfull.reshape(HEADS_C, NA_ROWS, GRID_W, NA_ROWS, GRID_W).transpose(0, 1, 3, 2, 4).reshape(HEADS_C, 64, GRID_W * GRID_W)

    def body(g_ref, oh_ref, sel_ref, o_ref):
        for h in range(HEADS_C):
            mid = lax.dot_general(g_ref[h], oh_ref[...], (((1,), (1,)), ((), ())), preferred_element_type=F32,
                                  precision=lax.Precision.HIGHEST)
            o_ref[h] = jnp.dot(sel_ref[...], mid, preferred_element_type=F32, precision=lax.Precision.HIGHEST)

    out = pl.pallas_call(body, name="rpb_grad", out_shape=_sds((HEADS_C, 16, 32), F32))(g, onehot, row_sel)
    return out[:, :15, :31]


def _outnorm_fwd(o_a, branch_o, branch_lse, o_c, ga, gb, gc):
    s = o_a.shape[0]
    ts = _tile(s, (512, 256, 128))

    def body(a_ref, o1, o2, o3, l1, l2, l3, c_ref, ga_ref, gb_ref, gc_ref, ob_ref, lse_ref, o_ref):
        la, lb, lc = l1[...], l2[...], l3[...]
        m = jnp.maximum(jnp.maximum(la, lb), lc)
        ea, eb, ec = jnp.exp(la - m), jnp.exp(lb - m), jnp.exp(lc - m)
        den = ea + eb + ec
        o_b = (o1[...] * ea + o2[...] * eb + o3[...] * ec) / den
        ob_ref[...] = o_b
        lse_ref[...] = m + jnp.log(den)
        col = 0
        for x, g in ((a_ref[...], ga_ref), (o_b, gb_ref), (c_ref[...], gc_ref)):
            o_ref[:, col:col + x.shape[1]] = (x * _rstd(x) * g[...]).astype(BF16)
            col += x.shape[1]

    sp = _row_spec(ts, WIDTH_AB)
    return pl.pallas_call(
        body, name="outnorm_fwd", grid=(s // ts,),
        in_specs=[sp] * 7 + [_row_spec(ts, WIDTH_C), _fix_spec(WIDTH_AB), _fix_spec(WIDTH_AB), _fix_spec(WIDTH_C)],
        out_specs=(sp, sp, _row_spec(ts, D_MODEL)),
        out_shape=(_sds((s, WIDTH_AB), F32), _sds((s, WIDTH_AB), F32), _sds((s, D_MODEL), BF16)),
        compiler_params=_cp("parallel"))(o_a, *branch_o, *branch_lse, o_c, ga.reshape(1, -1), gb.reshape(1, -1),
                                         gc.reshape(1, -1))


def _outnorm_bwd(dmixed, o_a, o_b, o_c, ga, gb, gc):
    s = o_a.shape[0]
    ts = _tile(s, (512, 256, 128))

    def body(dm_ref, a_ref, b_ref, c_ref, ga_ref, gb_ref, gc_ref, *outs):
        first = pl.program_id(0) == 0
        col = 0
        for k, (ref, g) in enumerate(((a_ref, ga_ref), (b_ref, gb_ref), (c_ref, gc_ref))):
            x = ref[...]
            w = x.shape[1]
            dx, dg = _rms_bwd_rows(x, g[...], dm_ref[:, col:col + w])
            col += w
            outs[k][...] = dx.astype(BF16)
            for b, blk in enumerate(_group_sum(dx * x)):
                if k == 0:
                    outs[3][b] = _stat_rows(blk)
                else:
                    outs[3 + k][:, b * LANE:(b + 1) * LANE] = blk
            _accum(outs[6 + k], dg, first)

    widths = (WIDTH_AB, WIDTH_AB, WIDTH_C)
    return pl.pallas_call(
        body, name="outnorm_bwd", grid=(s // ts,),
        in_specs=[_row_spec(ts, D_MODEL)] + [_row_spec(ts, w) for w in widths] + [_fix_spec(w) for w in widths],
        out_specs=tuple([_row_spec(ts, w) for w in widths] + [pl.BlockSpec((3, 8, ts), lambda i: (0, 0, i))]
                        + [_row_spec(ts, w) for w in widths[1:]] + [_fix_spec(w) for w in widths]),
        out_shape=tuple([_sds((s, w), BF16) for w in widths] + [_sds((3, 8, s), F32)]
                        + [_sds((s, w), F32) for w in widths[1:]] + [_sds((1, w), F32) for w in widths]),
        compiler_params=_cp("arbitrary"))(dmixed, o_a, o_b, o_c, ga.reshape(1, -1), gb.reshape(1, -1), gc.reshape(1, -1))


def _adamw(w, g, m, v, *, name):
    r, c = w.shape
    tr = _tile(r, (512, 256, 128, 64, 8))

    def body(w_ref, g_ref, m_ref, v_ref, d_ref, nm_ref, nv_ref):
        gv = g_ref[...]
        nm = ADAM_B1 * m_ref[...] + (1.0 - ADAM_B1) * gv
        nv = ADAM_B2 * v_ref[...] + (1.0 - ADAM_B2) * jnp.square(gv)
        m_hat = nm / (1.0 - ADAM_B1 ** ADAM_STEP)
        v_hat = nv / (1.0 - ADAM_B2 ** ADAM_STEP)
        d_ref[...] = -ADAM_LR * (m_hat / (jnp.sqrt(v_hat) + ADAM_EPS) + ADAM_WD * w_ref[...])
        nm_ref[...] = nm
        nv_ref[...] = nv

    sp = _row_spec(tr, c)
    return pl.pallas_call(
        body, name=name, grid=(r // tr,), in_specs=[sp] * 4, out_specs=(sp, sp, sp),
        out_shape=(_sds((r, c), F32),) * 3, compiler_params=_cp("parallel"))(w, g, m, v)


def _add_n(parts, *, name, out_dtype):
    r, c = parts[0].shape
    tr = max(t for t in range(16, 1025, 16) if r % t == 0)

    def body(*refs):
        acc = refs[0][...].astype(F32)
        for ref in refs[1:-1]:
            acc = acc + ref[...].astype(F32)
        refs[-1][...] = acc.astype(out_dtype)

    sp = _row_spec(tr, c)
    return pl.pallas_call(
        body, name=name, grid=(r // tr,), in_specs=[sp] * len(parts), out_specs=sp, out_shape=_sds((r, c), out_dtype),
        compiler_params=_cp("parallel"))(*parts)


ANY = pl.BlockSpec(memory_space=pl.ANY)
CHIP_FLIPS = ((1, 0), (0, 1), (1, 1))


def _me():
    return lax.axis_index("x"), lax.axis_index("y"), lax.axis_index("c")


def _gather_chips(half):
    def body(src, out, send_sems, recv_sems):
        x, y, c = _me()
        mine = 2 * x + y

        def copy(k, chip, half_idx, to, source=None):
            dst = out.at[chip, half_idx]
            return pltpu.make_async_remote_copy(src_ref=dst if source is None else source, dst_ref=dst,
                                                send_sem=send_sems.at[k], recv_sem=recv_sems.at[k], device_id=to,
                                                device_id_type=MESH_T)

        chips = [(x ^ fx, y ^ fy) for fx, fy in CHIP_FLIPS]
        first = [copy(k, mine, c, (cx, cy, c), source=src) for k, (cx, cy) in enumerate(chips)]
        for cp in first:
            cp.start()
        passed = []
        for k, (cx, cy) in enumerate(chips):
            theirs = 2 * cx + cy
            copy(k, theirs, c, (x, y, c)).wait_recv()
            cp = copy(3 + k, theirs, c, (x, y, 1 - c))
            cp.start()
            passed.append(cp)
        for k, (cx, cy) in enumerate(chips):
            copy(3 + k, 2 * cx + cy, 1 - c, (x, y, c)).wait_recv()
        for cp in first + passed:
            cp.wait_send()

    return pl.pallas_call(
        body, name="gather_chips", in_specs=[ANY], out_specs=ANY, out_shape=_sds((4, 2) + half.shape, half.dtype),
        scratch_shapes=[pltpu.SemaphoreType.DMA((6,)), pltpu.SemaphoreType.DMA((6,))])(half)


def _swap_sibling(block):
    def body(src, out, send_sem, recv_sem):
        x, y, c = _me()
        cp = pltpu.make_async_remote_copy(src_ref=src, dst_ref=out, send_sem=send_sem, recv_sem=recv_sem,
                                          device_id=(x, y, 1 - c), device_id_type=MESH_T)
        cp.start()
        cp.wait()

    return pl.pallas_call(
        body, name="swap_sibling", in_specs=[ANY], out_specs=ANY, out_shape=_sds(block.shape, block.dtype),
        scratch_shapes=[pltpu.SemaphoreType.DMA(()), pltpu.SemaphoreType.DMA(())])(block)


def _swap_other_half(halves):
    def body(src, out, send_sem, recv_sem):
        x, y, c = _me()
        cp = pltpu.make_async_remote_copy(src_ref=src.at[:, 1 - c], dst_ref=out, send_sem=send_sem, recv_sem=recv_sem,
                                          device_id=(x, y, 1 - c), device_id_type=MESH_T)
        cp.start()
        cp.wait()

    shape = (halves.shape[0],) + halves.shape[2:]
    return pl.pallas_call(
        body, name="swap_other_half", in_specs=[ANY], out_specs=ANY, out_shape=_sds(shape, halves.dtype),
        scratch_shapes=[pltpu.SemaphoreType.DMA(()), pltpu.SemaphoreType.DMA(())])(halves)


def _pair_sum(halves, core, other):
    n, _, h, c = halves.shape
    tr = max(t for t in range(16, 1025, 16) if h % t == 0)

    def body(core_ref, a_ref, b_ref, o_ref):
        o_ref[...] = (a_ref[...] + b_ref[...]).astype(BF16)

    grid_spec = pltpu.PrefetchScalarGridSpec(
        num_scalar_prefetch=1, grid=(n, h // tr),
        in_specs=[pl.BlockSpec((None, None, tr, c), lambda j, i, core_ref: (j, core_ref[0], i, 0)),
                  pl.BlockSpec((None, tr, c), lambda j, i, core_ref: (j, i, 0))],
        out_specs=pl.BlockSpec((None, tr, c), lambda j, i, core_ref: (j, i, 0)))
    return pl.pallas_call(
        body, name="pair_sum", grid_spec=grid_spec, out_shape=_sds((n, h, c), BF16),
        compiler_params=_cp("parallel", "parallel"))(core.reshape(1).astype(jnp.int32), halves, other)


def _scatter_chips(parts):
    def body(src, out, send_sems, recv_sems):
        x, y, c = _me()
        mine = 2 * x + y
        sends = []
        for k, (fx, fy) in enumerate(CHIP_FLIPS):
            theirs = 2 * (x ^ fx) + (y ^ fy)
            cp = pltpu.make_async_remote_copy(src_ref=src.at[theirs], dst_ref=out.at[mine], send_sem=send_sems.at[k],
                                              recv_sem=recv_sems.at[k], device_id=(x ^ fx, y ^ fy, c), device_id_type=MESH_T)
            cp.start()
            sends.append(cp)
        for k, (fx, fy) in enumerate(CHIP_FLIPS):
            theirs = 2 * (x ^ fx) + (y ^ fy)
            pltpu.make_async_remote_copy(src_ref=src.at[theirs], dst_ref=out.at[theirs], send_sem=send_sems.at[k],
                                         recv_sem=recv_sems.at[k], device_id=(x ^ fx, y ^ fy, c),
                                         device_id_type=MESH_T).wait_recv()
        for cp in sends:
            cp.wait_send()

    return pl.pallas_call(
        body, name="scatter_chips", in_specs=[ANY], out_specs=ANY, out_shape=_sds(parts.shape, parts.dtype),
        scratch_shapes=[pltpu.SemaphoreType.DMA((3,)), pltpu.SemaphoreType.DMA((3,))])(parts)


def _all_reduce_small(block):
    r, c = block.shape

    def body(src, out, slots, send_sems, recv_sems):
        x, y, cc = _me()
        mine = 4 * x + 2 * y + cc
        slots[mine] = src[...]
        sends = []
        for k in range(1, 8):
            fx, fy, fc = (k >> 2) & 1, (k >> 1) & 1, k & 1
            cp = pltpu.make_async_remote_copy(src_ref=src, dst_ref=slots.at[mine], send_sem=send_sems.at[k - 1],
                                              recv_sem=recv_sems.at[k - 1], device_id=(x ^ fx, y ^ fy, cc ^ fc),
                                              device_id_type=MESH_T)
            cp.start()
            sends.append(cp)
        for k in range(1, 8):
            fx, fy, fc = (k >> 2) & 1, (k >> 1) & 1, k & 1
            theirs = 4 * (x ^ fx) + 2 * (y ^ fy) + (cc ^ fc)
            pltpu.make_async_remote_copy(src_ref=src, dst_ref=slots.at[theirs], send_sem=send_sems.at[k - 1],
                                         recv_sem=recv_sems.at[k - 1], device_id=(x ^ fx, y ^ fy, cc ^ fc),
                                         device_id_type=MESH_T).wait_recv()
        for cp in sends:
            cp.wait_send()
        acc = slots[0]
        for d in range(1, 8):
            acc = acc + slots[d]
        out[...] = acc

    vm = pl.BlockSpec(memory_space=pltpu.VMEM)
    return pl.pallas_call(
        body, name="all_reduce_small", in_specs=[vm], out_specs=vm, out_shape=_sds((r, c), F32),
        scratch_shapes=[pltpu.VMEM((8, r, c), F32), pltpu.SemaphoreType.DMA((7,)), pltpu.SemaphoreType.DMA((7,))])(block)


DIRECT = ("w_mlp_in", "w_mlp_out", "w_out")
BIG = DIRECT + ("w_in", "w_uq", "w_ukv")
COL_SHARDED = {"w_in": True, "w_uq": True, "w_ukv": True, "w_out": False, "w_mlp_in": True, "w_mlp_out": False}
SMALL = ("g_mix", "q_norm", "kv_norm", "rpb", "out_norm_a", "out_norm_b", "out_norm_c", "g_mlp", "g_final")
PACK_C = 1024
ROW_ALIGN = 32


def _pack_rows(parts):
    flat = jnp.concatenate([p.reshape(-1, PACK_C) for p in parts], axis=0)
    return jnp.pad(flat, ((0, -flat.shape[0] % ROW_ALIGN), (0, 0)))


def _unpack_rows(flat, shapes):
    out, row = [], 0
    for shp in shapes:
        n = int(np.prod(shp)) // PACK_C
        out.append(flat[row:row + n].reshape(shp))
        row += n
    return out


def _full_from_shards(name, g):
    if COL_SHARDED[name]:
        return g.transpose(1, 2, 0, 3).reshape(g.shape[1], g.shape[2], 4 * g.shape[3])
    return g.transpose(1, 0, 2, 3).reshape(g.shape[1], 4 * g.shape[2], g.shape[3])


def _shards_from_full(name, w):
    l, k, n = w.shape
    if COL_SHARDED[name]:
        return w.reshape(l, k, 4, n // 4).transpose(2, 0, 1, 3)
    return w.reshape(l, 4, k // 4, n).transpose(1, 0, 2, 3)


def _arrange_w_in(w):
    z = jnp.zeros(w.shape[:-1] + (COL_B - COL_KPE - QK_ROPE,), w.dtype)
    return jnp.concatenate([w[..., :COL_KPE + QK_ROPE], z, w[..., COL_KPE + QK_ROPE:]], axis=-1)


def _unarrange_w_in(w):
    return jnp.concatenate([w[..., :COL_KPE + QK_ROPE], w[..., COL_B:]], axis=-1)


def _arrange_w_uq(w):
    per = HEAD_DIM + QK_ROPE
    z = jnp.zeros(w.shape[:-1] + (LANE - per,), w.dtype)
    cols = []
    for h in range(HEADS_A):
        cols += [w[..., h * per:(h + 1) * per], z]
    return jnp.concatenate(cols, axis=-1)


def _unarrange_w_uq(w):
    per = HEAD_DIM + QK_ROPE
    return jnp.concatenate([w[..., h * LANE:h * LANE + per] for h in range(HEADS_A)], axis=-1)


def _arrange_w_ukv(w):
    z = jnp.zeros(w.shape[:-1] + (HEAD_DIM,), w.dtype)
    ks = []
    for h in range(HEADS_A):
        ks += [w[..., h * LANE:h * LANE + HEAD_DIM], z]
    vs = [w[..., h * LANE + HEAD_DIM:(h + 1) * LANE] for h in range(HEADS_A)]
    return jnp.concatenate(ks + vs, axis=-1)


def _unarrange_w_ukv(w):
    cols = []
    for h in range(HEADS_A):
        cols += [w[..., h * LANE:h * LANE + HEAD_DIM], w[..., W_A2 + h * HEAD_DIM:W_A2 + (h + 1) * HEAD_DIM]]
    return jnp.concatenate(cols, axis=-1)


def _layer_fwd(x, w, sm, tabs, consts):
    t32, t64 = tabs
    onehot, col_ok, _, band = consts
    h, proj = _norm_mm(x, sm["g_mix"], w["w_in"], name="in_proj", relu2=False)
    cqn, ckvn, kpe, qkvb, qkvc = _prep_fwd(proj, sm["q_norm"], sm["kv_norm"], t32, t64)
    qa = _mm_nn(cqn, w["w_uq"], name="q_up")
    kva = _mm_nn(ckvn, w["w_ukv"], name="kv_up")
    qa2, ka2, kat, va1 = _a_post_fwd(qa, kva, kpe, t32)
    o_a, lse_a = _dense_fwd(qa2, ka2, va1)
    branch = [_banded_fwd(qkvb, dil, band) for _, dil in DILATED_PAIRS]
    tfull = _rpb_expand(sm["rpb"], onehot, col_ok)
    o_c, lse_c = _natten_fwd(qkvc, tfull)
    o_b, lse_b, mixed = _outnorm_fwd(o_a, [b[0] for b in branch], [b[1] for b in branch], o_c, sm["out_norm_a"],
                                     sm["out_norm_b"], sm["out_norm_c"])
    x_mid = _mm_nn(mixed, w["w_out"], name="out_proj", res=x)
    h2, act = _norm_mm(x_mid, sm["g_mlp"], w["w_mlp_in"], name="mlp_in", relu2=True)
    x_out = _mm_nn(act, w["w_mlp_out"], name="mlp_out", res=x_mid)
    saved = dict(x=x, h=h, proj=proj, cqn=cqn, ckvn=ckvn, qkvb=qkvb, qkvc=qkvc, qa2=qa2, ka2=ka2, kat=kat, va1=va1, o_a=o_a,
                 lse_a=lse_a, o_b=o_b, lse_b=lse_b, o_c=o_c, lse_c=lse_c, tfull=tfull, mixed=mixed, x_mid=x_mid, h2=h2,
                 act=act)
    return x_out, saved


def _layer_bwd(dx, dxb, sv, w, sm, tabs, consts, packed, places):
    t32, t64 = tabs
    onehot, _, row_sel, band = consts
    g = {}
    du = _mm_nt(dxb, w["w_mlp_out"], name="mlp_out_dx", out_dtype=BF16, relu2_act=sv["act"])
    packed = _mm_tn(sv["act"], dxb, name="mlp_out_dw", packed=(packed,) + places["w_mlp_out"])
    dx_mid, dmb, g["g_mlp"] = _mm_nt_norm_bwd(du, w["w_mlp_in"], sv["x_mid"], sm["g_mlp"], dx, name="mlp_in_dx")
    packed = _mm_tn(sv["h2"], du, name="mlp_in_dw", packed=(packed,) + places["w_mlp_in"])
    dmixed = _mm_nt(dmb, w["w_out"], name="out_proj_dx")
    packed = _mm_tn(sv["mixed"], dmb, name="out_proj_dw", packed=(packed,) + places["w_out"])
    (do_a, do_b, do_c, dl_a, dl_b, dl_c, g["out_norm_a"], g["out_norm_b"], g["out_norm_c"]) = _outnorm_bwd(
        dmixed, sv["o_a"], sv["o_b"], sv["o_c"], sm["out_norm_a"], sm["out_norm_b"], sm["out_norm_c"])
    dqa2_t, dka2, dva = _dense_bwd(sv["qa2"], sv["ka2"], sv["kat"], sv["va1"], do_a, sv["lse_a"], dl_a)
    db = []
    for _, dil in DILATED_PAIRS:
        db += _banded_bwd(sv["qkvb"], do_b, sv["lse_b"], dl_b, dil, band)
    dq_c, dk_c, dv_c, dtfull = _natten_bwd(sv["qkvc"], sv["tfull"], do_c, sv["lse_c"], dl_c)
    g["rpb"] = _rpb_grad(dtfull, onehot, row_sel)
    dqa, dkva, dkpe = _a_post_bwd(dqa2_t, dka2, dva, t32)
    dcqn = _mm_nt(dqa, w["w_uq"], name="q_up_dx")
    g["w_uq"] = _unarrange_w_uq(_mm_tn(sv["cqn"], dqa, name="q_up_dw"))
    dckvn = _mm_nt(dkva, w["w_ukv"], name="kv_up_dx")
    g["w_ukv"] = _unarrange_w_ukv(_mm_tn(sv["ckvn"], dkva, name="kv_up_dw"))
    dproj, g["q_norm"], g["kv_norm"] = _prep_bwd(sv["proj"], sm["q_norm"], sm["kv_norm"], t32, t64, dcqn, dckvn, dkpe,
                                                  db, (dq_c, dk_c, dv_c))
    g["w_in"] = _unarrange_w_in(_mm_tn(sv["h"], dproj, name="in_proj_dw"))
    dx_in, dxb_in, g["g_mix"] = _mm_nt_norm_bwd(dproj, w["w_in"], sv["x"], sm["g_mix"], dx_mid, name="in_proj_dx")
    return dx_in, dxb_in, g, packed


def _packed_places(offs, l):
    d, r = D_MODEL, D_MODEL // 4
    return {"w_mlp_in": (512, lambda i, j: (j, (offs["w_mlp_in"] + l * d) // 512 + i)),
            "w_mlp_out": (512, lambda i, j: (i // 2, (offs["w_mlp_out"] + l * d) // 512 + i % 2)),
            "w_out": (r, lambda i, j: (i, (offs["w_out"] + l * r) // r))}


def _local_step(x, target, wfull, small, packed_shape, offs):
    s = x.shape[0]
    tabs = (_rope_tables(s, QK_ROPE // 2, 1, lead=HEAD_DIM), _rope_tables(s, HEAD_DIM // 2, 2))
    consts = _rpb_constants() + (_band_bias_table(),)
    saved = []
    for l in range(DEPTH):
        wl = {k: v[l] for k, v in wfull.items()}
        sl = {k: small[k][l] for k in SMALL if k != "g_final"}
        x, sv = _layer_fwd(x, wl, sl, tabs, consts)
        saved.append(sv)
    loss, dx, dxb, dg_final = _loss_head(x, small["g_final"], target)
    grads = [None] * DEPTH
    packed = packed_shape
    for l in reversed(range(DEPTH)):
        wl = {k: v[l] for k, v in wfull.items()}
        sl = {k: small[k][l] for k in SMALL if k != "g_final"}
        dx, dxb, grads[l], packed = _layer_bwd(dx, dxb, saved[l], wl, sl, tabs, consts, packed, _packed_places(offs, l))
    return loss, dx, grads, dg_final, packed


ARRANGE = {"w_in": _arrange_w_in, "w_uq": _arrange_w_uq, "w_ukv": _arrange_w_ukv}


def kernel(x, g_mix, w_in, q_norm, w_uq, kv_norm, w_ukv, rpb, out_norm_a, out_norm_b, out_norm_c, w_out, g_mlp, w_mlp_in, w_mlp_out, g_final, loss_target, m_g_mix, m_w_in, m_q_norm, m_w_uq, m_kv_norm, m_w_ukv, m_rpb, m_out_norm_a, m_out_norm_b, m_out_norm_c, m_w_out, m_g_mlp, m_w_mlp_in, m_w_mlp_out, m_g_final, v_g_mix, v_w_in, v_q_norm, v_w_uq, v_kv_norm, v_w_ukv, v_rpb, v_out_norm_a, v_out_norm_b, v_out_norm_c, v_w_out, v_g_mlp, v_w_mlp_in, v_w_mlp_out, v_g_final):
    args = dict(locals())
    weights = {k: args[k] for k in BIG + SMALL}
    moms = {k: args["m_" + k] for k in BIG + SMALL}
    vels = {k: args["v_" + k] for k in BIG + SMALL}
    cc = lax.axis_index("c")
    my_chip = 2 * lax.axis_index("x") + lax.axis_index("y")

    shard_shapes = [weights[k].shape for k in BIG]
    packed_w = _pack_rows([weights[k].astype(BF16) for k in BIG])
    rows = packed_w.shape[0]
    my_half = lax.dynamic_index_in_dim(packed_w.reshape(2, rows // 2, PACK_C), cc, axis=0, keepdims=False)
    gathered = _gather_chips(my_half).reshape(4, rows, PACK_C)
    per_chip = [_unpack_rows(jnp.where(my_chip == j, packed_w, gathered[j]), shard_shapes) for j in range(4)]
    wfull = {}
    for idx, k in enumerate(BIG):
        full = _full_from_shards(k, jnp.stack([per_chip[j][idx] for j in range(4)]))
        wfull[k] = ARRANGE[k](full) if k in ARRANGE else full

    small = {k: weights[k] for k in SMALL}
    offs, row = {}, 0
    for k, shp in zip(BIG, shard_shapes):
        offs[k] = row
        row += int(np.prod(shp)) // PACK_C
    loss, dx, grads, dg_final, packed = _local_step(x[0], loss_target[0], wfull, small, _sds((4, rows, PACK_C), F32), offs)

    small_local = {k: jnp.stack([grads[l][k].reshape(weights[k].shape[1:]) for l in range(DEPTH)])
                   for k in SMALL if k != "g_final"}
    small_local["g_final"] = dg_final.reshape(-1)
    small_shapes = [weights[k].shape for k in SMALL]
    n_small = sum(int(np.prod(s)) for s in small_shapes)
    flat = jnp.concatenate([small_local[k].reshape(-1) for k in SMALL] + [loss[0, :1]])
    rows_small = -(-(n_small + 1) // PACK_C)
    rows_small += -rows_small % 8
    flat = jnp.pad(flat, (0, rows_small * PACK_C - n_small - 1)).reshape(rows_small, PACK_C)
    red = _all_reduce_small(flat).reshape(-1)
    loss_out = red[n_small]
    small_grads, off = {}, 0
    for k, shp in zip(SMALL, small_shapes):
        n = int(np.prod(shp))
        small_grads[k] = red[off:off + n].reshape(shp)
        off += n

    rest = [k for k in BIG if k not in DIRECT]
    by_shard = {k: _shards_from_full(k, jnp.stack([grads[l][k] for l in range(DEPTH)])) for k in rest}
    tail = jnp.stack([_pack_rows([by_shard[k][j] for k in rest]) for j in range(4)])
    assert offs[rest[0]] + tail.shape[1] == rows
    packed = lax.dynamic_update_slice(packed, tail, (0, offs[rest[0]], 0))
    halves = packed.reshape(4, 2, rows // 2, PACK_C)
    pair = _pair_sum(halves, cc, _swap_other_half(halves))
    by_chip = _scatter_chips(pair)
    reduced = _add_n([jnp.where(my_chip == j, pair[j], by_chip[j]) for j in range(4)], name="chip_sum", out_dtype=F32)
    theirs = _swap_sibling(reduced)
    joined = jnp.where(cc == 0, jnp.concatenate([reduced, theirs]), jnp.concatenate([theirs, reduced]))
    big_grads = dict(zip(BIG, _unpack_rows(joined, shard_shapes)))

    out_g, out_d, out_m, out_v = {}, {}, {}, {}
    for k in BIG:
        shp = weights[k].shape
        two_d = (shp[0] * shp[1], shp[2])
        d, nm, nv = _adamw(weights[k].reshape(two_d), big_grads[k].reshape(two_d), moms[k].reshape(two_d),
                           vels[k].reshape(two_d), name="adamw_" + k)
        out_g[k], out_d[k], out_m[k], out_v[k] = big_grads[k], d.reshape(shp), nm.reshape(shp), nv.reshape(shp)

    def pack_small(tree):
        f = jnp.concatenate([tree[k].reshape(-1) for k in SMALL])
        return jnp.pad(f, (0, rows_small * PACK_C - n_small)).reshape(rows_small, PACK_C)

    d, nm, nv = _adamw(pack_small(small), pack_small(small_grads), pack_small(moms), pack_small(vels), name="adamw_small")
    for tree, flat_out in ((out_d, d), (out_m, nm), (out_v, nv)):
        off = 0
        fo = flat_out.reshape(-1)
        for k, shp in zip(SMALL, small_shapes):
            n = int(np.prod(shp))
            tree[k] = fo[off:off + n].reshape(shp)
            off += n
    out_g.update(small_grads)

    order = ("g_mix", "w_in", "q_norm", "w_uq", "kv_norm", "w_ukv", "rpb", "out_norm_a", "out_norm_b", "out_norm_c", "w_out",
             "g_mlp", "w_mlp_in", "w_mlp_out", "g_final")
    return (loss_out, dx.reshape(x.shape), *[out_g[k] for k in order], *[out_d[k] for k in order],
            *[out_m[k] for k in order], *[out_v[k] for k in order])
```

```python
import math

import numpy as np
import jax
import jax.numpy as jnp
from jax import lax
from jax.experimental import pallas as pl
from jax.experimental.pallas import tpu as pltpu

F32 = jnp.float32
BF16 = jnp.bfloat16

D_MODEL = 1024
HEAD_DIM = 64
Q_LORA = 256
KV_LORA = 128
QK_ROPE = 32
HEADS_A = 6
HEADS_B = 6
HEADS_C = 4
DILATED_PAIRS = ((128, 1), (512, 4), (2048, 16))
BAND_HALF = 64
GRID_W = 64
NA_ROWS = 8
NA_COLS = 16
D_FF = 4096
ROPE_THETA = 10000.0
NORM_EPS = 1e-6
NEG_INF = -1e30
DEPTH = 4

LANE = 128
PROJ_W = 2432
COL_CKV = 256
COL_KPE = 384
COL_B = 512
COL_C = 1664
W_A2 = 768
W_KV = W_A2 + 384
WIDTH_AB = 384
WIDTH_C = 256
SCALE_A = (HEAD_DIM + QK_ROPE) ** -0.5
SCALE_BC = HEAD_DIM ** -0.5

ADAM_LR = 0.001
ADAM_B1 = 0.9
ADAM_B2 = 0.999
ADAM_EPS = 1e-08
ADAM_WD = 0.01
ADAM_STEP = 10

VMEM_LIMIT = 56 * 1024 * 1024
MESH_T = pl.DeviceIdType.MESH


def _cp(*sem):
    return pltpu.CompilerParams(dimension_semantics=sem or None, vmem_limit_bytes=VMEM_LIMIT)


def _tile(n, cands):
    for c in cands:
        if n % c == 0:
            return c
    return n


def _sds(shape, dtype):
    return jax.ShapeDtypeStruct(shape, dtype)


ROW_TILE_BYTES = 12 * 1024 * 1024


def _row_tiles(row_bytes):
    return tuple(t for t in (2048, 1024, 512, 256, 128) if t * row_bytes <= ROW_TILE_BYTES or t <= 512)


def _mm_nn(a, b, *, name, out_dtype=F32, res=None):
    m, k = a.shape
    n = b.shape[1]
    tn = _tile(n, (1024, 768, 512)) if n % LANE == 0 and n != PROJ_W else n
    tm = _tile(m, _row_tiles(2 * k + tn * (jnp.dtype(out_dtype).itemsize + (4 if res is not None else 0))))

    def body(*refs):
        a_ref, b_ref = refs[0], refs[1]
        o_ref = refs[-1]
        acc = jnp.dot(a_ref[...], b_ref[...], preferred_element_type=F32)
        if res is not None:
            acc = refs[2][...] + acc
        o_ref[...] = acc.astype(o_ref.dtype)

    in_specs = [pl.BlockSpec((tm, k), lambda j, i: (i, 0)), pl.BlockSpec((k, tn), lambda j, i: (0, j))]
    args = [a, b]
    if res is not None:
        in_specs.append(pl.BlockSpec((tm, tn), lambda j, i: (i, j)))
        args.append(res)
    return pl.pallas_call(
        body, name=name, grid=(n // tn, m // tm), in_specs=in_specs,
        out_specs=pl.BlockSpec((tm, tn), lambda j, i: (i, j)), out_shape=_sds((m, n), out_dtype),
        compiler_params=_cp("parallel", "parallel"))(*args)


def _norm_mm(x, g, w, *, name, relu2):
    m, k = x.shape
    n = w.shape[1]
    tm = _tile(m, (512, 256, 128) if n <= PROJ_W else (256, 128))

    def body(x_ref, g_ref, w_ref, h_ref, o_ref):
        xv = x_ref[...]
        h = (xv * _rstd(xv) * g_ref[...]).astype(BF16)
        h_ref[...] = h
        acc = jnp.dot(h, w_ref[...], preferred_element_type=F32)
        if relu2:
            acc = jnp.square(jnp.maximum(acc, 0.0))
        o_ref[...] = acc.astype(o_ref.dtype)

    return pl.pallas_call(
        body, name=name, grid=(m // tm,),
        in_specs=[pl.BlockSpec((tm, k), lambda i: (i, 0)), pl.BlockSpec((1, k), lambda i: (0, 0)),
                  pl.BlockSpec((k, n), lambda i: (0, 0))],
        out_specs=(pl.BlockSpec((tm, k), lambda i: (i, 0)), pl.BlockSpec((tm, n), lambda i: (i, 0))),
        out_shape=(_sds((m, k), BF16), _sds((m, n), BF16 if relu2 else F32)),
        compiler_params=_cp("parallel"))(x, g.reshape(1, k), w)


def _mm_nt(a, b, *, name, out_dtype=F32, relu2_act=None):
    m, c = a.shape
    n = b.shape[0]
    tn = _tile(n, (1024, 512, 256, 128))
    tm = _tile(m, _row_tiles(2 * c + tn * (jnp.dtype(out_dtype).itemsize + (2 if relu2_act is not None else 0))))

    def body(*refs):
        a_ref, b_ref = refs[0], refs[1]
        o_ref = refs[-1]
        acc = lax.dot_general(a_ref[...], b_ref[...], (((1,), (1,)), ((), ())), preferred_element_type=F32)
        if relu2_act is not None:
            acc = acc * (2.0 * jnp.sqrt(refs[2][...].astype(F32)))
        o_ref[...] = acc.astype(o_ref.dtype)

    in_specs = [pl.BlockSpec((tm, c), lambda j, i: (i, 0)), pl.BlockSpec((tn, c), lambda j, i: (j, 0))]
    args = [a, b]
    if relu2_act is not None:
        in_specs.append(pl.BlockSpec((tm, tn), lambda j, i: (i, j)))
        args.append(relu2_act)
    return pl.pallas_call(
        body, name=name, grid=(n // tn, m // tm), in_specs=in_specs,
        out_specs=pl.BlockSpec((tm, tn), lambda j, i: (i, j)), out_shape=_sds((m, n), out_dtype),
        compiler_params=_cp("parallel", "parallel"))(*args)


def _mm_nt_norm_bwd(a, b, x, g, res, *, name):
    m, c = a.shape
    d = b.shape[0]
    tm = _tile(m, (512, 256, 128))

    def body(a_ref, b_ref, x_ref, g_ref, res_ref, dx_ref, dxb_ref, dg_ref):
        dy = lax.dot_general(a_ref[...], b_ref[...], (((1,), (1,)), ((), ())), preferred_element_type=F32)
        dx, dg = _rms_bwd_rows(x_ref[...], g_ref[...], dy)
        dx = res_ref[...] + dx
        dx_ref[...] = dx
        dxb_ref[...] = dx.astype(BF16)
        _accum(dg_ref, dg, pl.program_id(0) == 0)

    row = pl.BlockSpec((tm, d), lambda i: (i, 0))
    fix = pl.BlockSpec((1, d), lambda i: (0, 0))
    return pl.pallas_call(
        body, name=name, grid=(m // tm,),
        in_specs=[pl.BlockSpec((tm, c), lambda i: (i, 0)), pl.BlockSpec((d, c), lambda i: (0, 0)), row, fix, row],
        out_specs=(row, row, fix), out_shape=(_sds((m, d), F32), _sds((m, d), BF16), _sds((1, d), F32)),
        compiler_params=_cp("arbitrary"))(a, b, x, g.reshape(1, d), res)


def _mm_tn(a, b, *, name, packed=None):
    m, ka = a.shape
    nb = b.shape[1]
    tka = _tile(ka, (512, 256, 128)) if packed is None else packed[1]
    tnb = _tile(nb, (1024, 768, 512)) if nb != PROJ_W else nb
    tc = _tile(m, (4096, 2048, 1024, 512, 256, 128) if tnb <= PACK_C else (2048, 1024, 512, 256, 128))

    def body(*refs):
        a_ref, b_ref, o_ref = refs[0], refs[1], refs[-1]
        part = lax.dot_general(a_ref[...], b_ref[...], (((0,), (0,)), ((), ())), preferred_element_type=F32)

        @pl.when(pl.program_id(2) == 0)
        def _():
            o_ref[...] = part

        @pl.when(pl.program_id(2) != 0)
        def _():
            o_ref[...] += part

    in_specs = [pl.BlockSpec((tc, tka), lambda i, j, c: (c, i)), pl.BlockSpec((tc, tnb), lambda i, j, c: (c, j))]
    kwargs = dict(out_specs=pl.BlockSpec((tka, tnb), lambda i, j, c: (i, j)), out_shape=_sds((ka, nb), F32))
    args = [a, b]
    if packed is not None:
        buf, _, place = packed
        assert tnb == PACK_C
        kwargs = dict(out_specs=pl.BlockSpec((None, tka, tnb), lambda i, j, c: place(i, j) + (0,)))
        if isinstance(buf, jax.ShapeDtypeStruct):
            kwargs["out_shape"] = buf
        else:
            kwargs.update(out_shape=_sds(buf.shape, buf.dtype), input_output_aliases={2: 0})
            in_specs.append(pl.BlockSpec(memory_space=pl.ANY))
            args.append(buf)
    return pl.pallas_call(
        body, name=name, grid=(ka // tka, nb // tnb, m // tc), in_specs=in_specs,
        compiler_params=_cp("parallel", "parallel", "arbitrary"), **kwargs)(*args)


def _rstd(x):
    return lax.rsqrt(jnp.mean(x * x, axis=-1, keepdims=True) + NORM_EPS)


def _rms_bwd_rows(x, g, dy):
    r = _rstd(x)
    gy = dy * g
    c = jnp.sum(x * gy, axis=-1, keepdims=True) * (r * r * r) * (1.0 / x.shape[-1])
    return r * gy - x * c, jnp.sum(dy * x * r, axis=0, keepdims=True)


def _accum(ref, part, first):
    @pl.when(first)
    def _():
        ref[...] = part

    @pl.when(jnp.logical_not(first))
    def _():
        ref[...] += part


def _rope(x, c, s1, s2, sh):
    return x * c + pltpu.roll(x, LANE - sh, 1) * s1 + pltpu.roll(x, sh, 1) * s2


def _rope_t(g, c, s1, s2, sh):
    return g * c + pltpu.roll(g * s1, sh, 1) + pltpu.roll(g * s2, LANE - sh, 1)


def _rope_tables(s, half, reps, lead=0):
    pos = jnp.arange(s, dtype=F32)
    inv_freq = ROPE_THETA ** (-jnp.arange(half, dtype=F32) / half)
    ang = pos[:, None] * inv_freq[None, :]
    cos, sin = jnp.cos(ang), jnp.sin(ang)
    zero = jnp.zeros_like(cos)
    ones, lead0 = jnp.ones((s, lead), F32), jnp.zeros((s, lead), F32)
    pad = jnp.zeros((s, LANE - lead - 2 * half * reps), F32)
    c = jnp.concatenate([ones] + [cos, cos] * reps + [pad], axis=1)
    s1 = jnp.concatenate([lead0] + [-sin, zero] * reps + [pad], axis=1)
    s2 = jnp.concatenate([lead0] + [zero, sin] * reps + [pad], axis=1)
    return c, s1, s2


def _lane_lt64(shape):
    return lax.broadcasted_iota(jnp.int32, shape, len(shape) - 1) % LANE < HEAD_DIM


def _group_sum(x):
    outs = []
    for b in range(x.shape[1] // LANE):
        blk = x[:, b * LANE:(b + 1) * LANE]
        lo = _lane_lt64(blk.shape)
        s0 = jnp.sum(jnp.where(lo, blk, 0.0), axis=1, keepdims=True)
        s1 = jnp.sum(jnp.where(lo, 0.0, blk), axis=1, keepdims=True)
        outs.append(jnp.where(lo, s0, s1))
    return outs


def _row_spec(ts, w):
    return pl.BlockSpec((ts, w), lambda i: (i, 0))


def _fix_spec(w):
    return pl.BlockSpec((1, w), lambda i: (0, 0))


def _loss_head(x, g, target):
    s, d = x.shape
    ts = _tile(s, (512, 256, 128))

    def body(x_ref, g_ref, t_ref, loss_ref, dx_ref, dxb_ref, dg_ref):
        xv, gv = x_ref[...], g_ref[...]
        err = xv * _rstd(xv) * gv - t_ref[...]
        part = 0.5 * jnp.sum(jnp.sum(err * err, axis=-1, keepdims=True) * (1.0 / d), axis=0, keepdims=True)
        dx, dg = _rms_bwd_rows(xv, gv, err * (1.0 / d))
        dx_ref[...] = dx
        dxb_ref[...] = dx.astype(BF16)
        first = pl.program_id(0) == 0
        _accum(dg_ref, dg, first)
        _accum(loss_ref, jnp.broadcast_to(part, (1, LANE)), first)

    return pl.pallas_call(
        body, name="loss_head", grid=(s // ts,), in_specs=[_row_spec(ts, d), _fix_spec(d), _row_spec(ts, d)],
        out_specs=(_fix_spec(LANE), _row_spec(ts, d), _row_spec(ts, d), _fix_spec(d)),
        out_shape=(_sds((1, LANE), F32), _sds((s, d), F32), _sds((s, d), BF16), _sds((1, d), F32)),
        compiler_params=_cp("arbitrary"))(x, g.reshape(1, d), target)


def _prep_fwd(proj, q_norm, kv_norm, t32, t64):
    s = proj.shape[0]
    ts = _tile(s, (256, 128))

    def body(p_ref, qn_ref, kn_ref, c32, a32, b32, c64, a64, b64, cqn_ref, ckvn_ref, kpe_ref, qkvb_ref, qkvc_ref):
        cq = p_ref[:, 0:Q_LORA]
        cqn_ref[...] = (cq * _rstd(cq) * qn_ref[...]).astype(BF16)
        ckv = p_ref[:, COL_CKV:COL_KPE]
        ckvn_ref[...] = (ckv * _rstd(ckv) * kn_ref[...]).astype(BF16)
        kp = pltpu.roll(p_ref[:, COL_KPE:COL_B], HEAD_DIM, 1)
        kpe_ref[...] = _rope(kp, c32[...], a32[...], b32[...], QK_ROPE // 2).astype(BF16)
        for b in range(6):
            blk = _rope(p_ref[:, COL_B + b * LANE:COL_B + (b + 1) * LANE], c64[...], a64[...], b64[...], HEAD_DIM // 2)
            if b < 3:
                blk = blk * SCALE_BC
            qkvb_ref[:, b * LANE:(b + 1) * LANE] = blk.astype(BF16)
        qkvb_ref[:, 2 * WIDTH_AB:3 * WIDTH_AB] = p_ref[:, COL_B + 2 * WIDTH_AB:COL_C].astype(BF16)
        qkvc_ref[:, 0:WIDTH_C] = (p_ref[:, COL_C:COL_C + WIDTH_C] * SCALE_BC).astype(BF16)
        qkvc_ref[:, WIDTH_C:3 * WIDTH_C] = p_ref[:, COL_C + WIDTH_C:PROJ_W].astype(BF16)

    tab = [_row_spec(ts, LANE)] * 6
    return pl.pallas_call(
        body, name="prep_fwd", grid=(s // ts,),
        in_specs=[_row_spec(ts, PROJ_W), _fix_spec(Q_LORA), _fix_spec(KV_LORA)] + tab,
        out_specs=(_row_spec(ts, Q_LORA), _row_spec(ts, KV_LORA), _row_spec(ts, LANE), _row_spec(ts, 3 * WIDTH_AB),
                   _row_spec(ts, 3 * WIDTH_C)),
        out_shape=(_sds((s, Q_LORA), BF16), _sds((s, KV_LORA), BF16), _sds((s, LANE), BF16),
                   _sds((s, 3 * WIDTH_AB), BF16), _sds((s, 3 * WIDTH_C), BF16)),
        compiler_params=_cp("parallel"))(proj, q_norm.reshape(1, -1), kv_norm.reshape(1, -1), *t32, *t64)


def _prep_bwd(proj, q_norm, kv_norm, t32, t64, dcqn, dckvn, dkpe, db, dc):
    s = proj.shape[0]
    ts = _tile(s, (256, 128))

    def body(p_ref, qn_ref, kn_ref, c32, a32, b32, c64, a64, b64, dcqn_ref, dckvn_ref, dkpe_ref, *rest):
        db_refs, dc_refs = rest[0:9], rest[9:12]
        dp_ref, dqn_ref, dkn_ref = rest[12:15]
        first = pl.program_id(0) == 0
        dx, dg = _rms_bwd_rows(p_ref[:, 0:Q_LORA], qn_ref[...], dcqn_ref[...])
        dp_ref[:, 0:Q_LORA] = dx.astype(BF16)
        _accum(dqn_ref, dg, first)
        dx, dg = _rms_bwd_rows(p_ref[:, COL_CKV:COL_KPE], kn_ref[...], dckvn_ref[...])
        dp_ref[:, COL_CKV:COL_KPE] = dx.astype(BF16)
        _accum(dkn_ref, dg, first)
        g = pltpu.roll(_rope_t(dkpe_ref[...], c32[...], a32[...], b32[...], QK_ROPE // 2), LANE - HEAD_DIM, 1)
        lane = lax.broadcasted_iota(jnp.int32, g.shape, 1)
        dp_ref[:, COL_KPE:COL_B] = jnp.where(lane < QK_ROPE, g, 0.0).astype(BF16)
        for which in range(3):
            for b in range(3):
                sl = slice(b * LANE, (b + 1) * LANE)
                g = db_refs[which][:, sl] + db_refs[3 + which][:, sl] + db_refs[6 + which][:, sl]
                if which < 2:
                    g = _rope_t(g, c64[...], a64[...], b64[...], HEAD_DIM // 2)
                if which == 0:
                    g = g * SCALE_BC
                col = COL_B + which * WIDTH_AB + b * LANE
                dp_ref[:, col:col + LANE] = g.astype(BF16)
        dp_ref[:, COL_C:COL_C + WIDTH_C] = (dc_refs[0][...] * SCALE_BC).astype(BF16)
        dp_ref[:, COL_C + WIDTH_C:COL_C + 2 * WIDTH_C] = dc_refs[1][...].astype(BF16)
        dp_ref[:, COL_C + 2 * WIDTH_C:PROJ_W] = dc_refs[2][...].astype(BF16)

    tab = [_row_spec(ts, LANE)] * 6
    in_specs = ([_row_spec(ts, PROJ_W), _fix_spec(Q_LORA), _fix_spec(KV_LORA)] + tab
                + [_row_spec(ts, Q_LORA), _row_spec(ts, KV_LORA), _row_spec(ts, LANE)]
                + [_row_spec(ts, WIDTH_AB)] * 9 + [_row_spec(ts, WIDTH_C)] * 3)
    return pl.pallas_call(
        body, name="prep_bwd", grid=(s // ts,), in_specs=in_specs,
        out_specs=(_row_spec(ts, PROJ_W), _fix_spec(Q_LORA), _fix_spec(KV_LORA)),
        out_shape=(_sds((s, PROJ_W), BF16), _sds((1, Q_LORA), F32), _sds((1, KV_LORA), F32)),
        compiler_params=_cp("arbitrary"))(proj, q_norm.reshape(1, -1), kv_norm.reshape(1, -1), *t32, *t64,
                                          dcqn, dckvn, dkpe, *db, *dc)


def _a_post_fwd(cqn, ckvn, w_uq, w_ukv, kpe, t32):
    s = cqn.shape[0]
    ts = _tile(s, (512, 256, 128))

    def body(cq_ref, ckv_ref, wq_ref, wkv_ref, kpe_ref, c32, a32, b32, q_ref, k_ref, kt_ref, v_ref):
        qa = jnp.dot(cq_ref[...], wq_ref[...], preferred_element_type=F32)
        kva = jnp.dot(ckv_ref[...], wkv_ref[...], preferred_element_type=F32)
        kpe = kpe_ref[...].astype(F32)
        for h in range(HEADS_A):
            hb = slice(h * LANE, (h + 1) * LANE)
            q_ref[:, hb] = _rope(qa[:, hb], c32[...], a32[...], b32[...], QK_ROPE // 2).astype(BF16)
            kh = kva[:, hb] + kpe
            k_ref[:, hb] = kh.astype(BF16)
            kt_ref[hb, :] = kh.T.astype(BF16)
        for p in range(3):
            lo, hi = 2 * p * LANE, (2 * p + 1) * LANE
            v_ref[:, lo:hi] = kva[:, W_A2 + p * LANE:W_A2 + (p + 1) * LANE].astype(BF16)
            v_ref[:, hi:hi + LANE] = jnp.ones((ts, LANE), BF16)

    whole = lambda shape: pl.BlockSpec(shape, lambda i: (0, 0))
    return pl.pallas_call(
        body, name="a_post_fwd", grid=(s // ts,),
        in_specs=[_row_spec(ts, Q_LORA), _row_spec(ts, KV_LORA), whole(w_uq.shape), whole(w_ukv.shape), _row_spec(ts, LANE)]
        + [_row_spec(ts, LANE)] * 3,
        out_specs=(_row_spec(ts, W_A2), _row_spec(ts, W_A2), pl.BlockSpec((W_A2, ts), lambda i: (0, i)), _row_spec(ts, W_A2)),
        out_shape=(_sds((s, W_A2), BF16), _sds((s, W_A2), BF16), _sds((W_A2, s), BF16), _sds((s, W_A2), BF16)),
        compiler_params=_cp("parallel"))(cqn, ckvn, w_uq, w_ukv, kpe, *t32)


def _a_post_bwd(dqa2_t, dka2, dva, t32):
    s = dka2.shape[0]
    ts = _tile(s, (512, 256, 128))

    def body(dqt_ref, dk_ref, dv_ref, c32, a32, b32, dqa_ref, dkva_ref, dkpe_ref):
        acc = None
        for h in range(HEADS_A):
            hb = slice(h * LANE, (h + 1) * LANE)
            dqa_ref[:, hb] = _rope_t(dqt_ref[hb, :].T * SCALE_A, c32[...], a32[...], b32[...], QK_ROPE // 2).astype(BF16)
            part = dk_ref[:, hb] * SCALE_A
            dkva_ref[:, hb] = part.astype(BF16)
            acc = part if acc is None else acc + part
        dkva_ref[:, W_A2:W_KV] = dv_ref[...].astype(BF16)
        dkpe_ref[...] = acc

    return pl.pallas_call(
        body, name="a_post_bwd", grid=(s // ts,),
        in_specs=[pl.BlockSpec((W_A2, ts), lambda i: (0, i)), _row_spec(ts, W_A2), _row_spec(ts, WIDTH_AB)]
        + [_row_spec(ts, LANE)] * 3,
        out_specs=(_row_spec(ts, W_A2), _row_spec(ts, W_KV), _row_spec(ts, LANE)),
        out_shape=(_sds((s, W_A2), BF16), _sds((s, W_KV), BF16), _sds((s, LANE), F32)),
        compiler_params=_cp("parallel"))(dqa2_t, dka2, dva, *t32)


def _pair_masks():
    lane = lax.broadcasted_iota(jnp.int32, (1, LANE), 1)
    return lane < HEAD_DIM, lane >= HEAD_DIM


def _nt(a, b):
    return lax.dot_general(a, b, (((1,), (1,)), ((), ())), preferred_element_type=F32)


def _tn(a, b):
    return lax.dot_general(a, b, (((0,), (0,)), ((), ())), preferred_element_type=F32)


def _stack_heads(x):
    m0, m1 = _pair_masks()
    zero = jnp.zeros_like(x)
    return jnp.concatenate([jnp.where(m0, x, zero), jnp.where(m1, x, zero)], axis=0)


def _stack_stat(x):
    return jnp.concatenate([x[:, 0:1], x[:, HEAD_DIM:HEAD_DIM + 1]], axis=0)


def _softmax_pair(q, kk, vv, bias2):
    t = q.shape[0]
    s = _nt(_stack_heads(q), kk) + bias2
    m = jnp.max(s, axis=1, keepdims=True)
    p = jnp.exp(s - m)
    l = jnp.sum(p, axis=1, keepdims=True)
    o2 = jnp.dot(p.astype(BF16), vv, preferred_element_type=F32) / l
    lse2 = m + jnp.log(l)
    lo = _lane_lt64((t, LANE))
    return jnp.where(lo, o2[:t], o2[t:]), jnp.where(lo, lse2[:t], lse2[t:])


def _softmax_pair_bwd(q, kk, vv, do, lse, delta, bias2):
    t = q.shape[0]
    q2, do2 = _stack_heads(q), _stack_heads(do)
    p = jnp.exp(_nt(q2, kk) + bias2 - _stack_stat(lse))
    ds = p * (_nt(do2, vv) - _stack_stat(delta))
    dsb = ds.astype(BF16)
    dq2 = jnp.dot(dsb, kk, preferred_element_type=F32)
    lo = _lane_lt64((t, LANE))
    return jnp.where(lo, dq2[:t], dq2[t:]), _tn(dsb, q2), _tn(p.astype(BF16), do2), ds


DENSE_FWD_TQ, DENSE_FWD_TK = 512, 8192
DENSE_BWD_TQ, DENSE_BWD_TK = 2048, 1024
LOG2E = math.log2(math.e)


def _dense_fwd(qa, ka, va1):
    s = qa.shape[0]
    tq, tk = min(DENSE_FWD_TQ, s), min(DENSE_FWD_TK, s)
    nk = s // tk
    c = SCALE_A * LOG2E

    def body(q_ref, k_ref, v_ref, o_ref, lse_ref, m_sc, acc_sc):
        j = pl.program_id(2)

        @pl.when(j == 0)
        def _():
            m_sc[...] = jnp.full(m_sc.shape, NEG_INF, F32)
            acc_sc[...] = jnp.zeros(acc_sc.shape, F32)

        vv = v_ref[...]
        for hh in range(2):
            hs = slice(hh * LANE, (hh + 1) * LANE)
            sc = _nt(q_ref[:, hs], k_ref[:, hs])
            m_prev = m_sc[hh]
            m_new = jnp.maximum(m_prev, jnp.max(sc, axis=1, keepdims=True))
            alpha = jnp.exp2((m_prev - m_new) * c)
            p = jnp.exp2((sc - m_new) * c)
            acc_sc[hh] = alpha * acc_sc[hh] + jnp.dot(p.astype(BF16), vv, preferred_element_type=F32)
            m_sc[hh] = m_new

        @pl.when(j == nk - 1)
        def _():
            lo = _lane_lt64((tq, LANE))
            a0, a1 = acc_sc[0], acc_sc[1]
            l0, l1 = a0[:, LANE:], a1[:, LANE:]
            o_ref[...] = jnp.where(lo, a0[:, :LANE] / l0, a1[:, :LANE] / l1)
            lse_ref[0] = _stat_rows(jnp.where(lo, m_sc[0] * SCALE_A + jnp.log(l0), m_sc[1] * SCALE_A + jnp.log(l1)))

    return pl.pallas_call(
        body, name="dense_fwd", grid=(3, s // tq, nk),
        in_specs=[pl.BlockSpec((tq, 2 * LANE), lambda p, i, j: (i, p)), pl.BlockSpec((tk, 2 * LANE), lambda p, i, j: (j, p)),
                  pl.BlockSpec((tk, 2 * LANE), lambda p, i, j: (j, p))],
        out_specs=(pl.BlockSpec((tq, LANE), lambda p, i, j: (i, p)), pl.BlockSpec((1, 8, tq), lambda p, i, j: (p, 0, i))),
        out_shape=(_sds((s, WIDTH_AB), F32), _sds((3, 8, s), F32)),
        scratch_shapes=[pltpu.VMEM((2, tq, 1), F32), pltpu.VMEM((2, tq, 2 * LANE), F32)],
        compiler_params=_cp("parallel", "parallel", "arbitrary"))(qa, ka, va1)


def _stat_rows(lane_dense):
    tr = lane_dense.T
    return jnp.concatenate([tr[0:1, :], tr[HEAD_DIM:HEAD_DIM + 1, :], jnp.zeros((6, tr.shape[1]), F32)], axis=0)


def _dense_bwd(qa, ka, kat, va1, do, lse_rows, delta_rows):
    s = qa.shape[0]
    tq, tk = min(DENSE_BWD_TQ, s), min(DENSE_BWD_TK, s)
    c = SCALE_A * LOG2E

    def body(q_ref, k_ref, kt_ref, v_ref, do_ref, lse_ref, dl_ref, dqt_ref, dk_ref, dv_ref):
        j, i = pl.program_id(1), pl.program_id(2)

        @pl.when((j == 0) & (i == 0))
        def _():
            dqt_ref[...] = jnp.zeros(dqt_ref.shape, F32)

        vv, do_ = v_ref[...], do_ref[...]
        lse_t, dl_t = lse_ref[0] * LOG2E, dl_ref[0]
        vm = _pair_masks()
        cols = pl.ds(pl.multiple_of(i * tq, tq), tq)
        dv = None
        for hh in range(2):
            hs = slice(hh * LANE, (hh + 1) * LANE)
            qh = q_ref[:, hs]
            dom = jnp.where(vm[hh], do_, jnp.zeros_like(do_))
            pt = jnp.exp2(_nt(k_ref[:, hs], qh) * c - lse_t[hh:hh + 1, :])
            dst = pt * (_nt(vv, dom) - dl_t[hh:hh + 1, :])
            pb, dsb = pt.astype(BF16), dst.astype(BF16)
            dv_h = jnp.dot(pb, dom, preferred_element_type=F32)
            dv = dv_h if dv is None else dv + dv_h
            dqt_ref[hs, cols] += jnp.dot(kt_ref[hs, :], dsb, preferred_element_type=F32)
            dk_h = jnp.dot(dsb, qh, preferred_element_type=F32)

            @pl.when(i == 0)
            def _():
                dk_ref[:, hs] = dk_h

            @pl.when(i != 0)
            def _():
                dk_ref[:, hs] += dk_h
        _accum(dv_ref, dv, i == 0)

    st_spec = pl.BlockSpec((1, 8, tq), lambda p, j, i: (p, 0, i))
    return pl.pallas_call(
        body, name="dense_bwd", grid=(3, s // tk, s // tq),
        in_specs=[pl.BlockSpec((tq, 2 * LANE), lambda p, j, i: (i, p)), pl.BlockSpec((tk, 2 * LANE), lambda p, j, i: (j, p)),
                  pl.BlockSpec((2 * LANE, tk), lambda p, j, i: (p, j)), pl.BlockSpec((tk, LANE), lambda p, j, i: (j, 2 * p)),
                  pl.BlockSpec((tq, LANE), lambda p, j, i: (i, p)), st_spec, st_spec],
        out_specs=(pl.BlockSpec((2 * LANE, s), lambda p, j, i: (p, 0)), pl.BlockSpec((tk, 2 * LANE), lambda p, j, i: (j, p)),
                   pl.BlockSpec((tk, LANE), lambda p, j, i: (j, p))),
        out_shape=(_sds((W_A2, s), F32), _sds((s, W_A2), F32), _sds((s, WIDTH_AB), F32)),
        compiler_params=_cp("parallel", "arbitrary", "arbitrary"))(qa, ka, kat, va1, do, lse_rows, delta_rows)


BAND_TILE = 1024
BAND_SUB = 128
QKV_W = 3 * WIDTH_AB


def _band_bias_table():
    row = np.arange(BAND_SUB)[:, None]
    col = np.arange(2 * BAND_SUB)[None, :]
    band = np.abs(row - col + BAND_HALF) <= BAND_HALF
    variants = []
    for idx in range(4):
        ok = band & ((col >= BAND_HALF) | ((idx & 1) == 0)) & ((col < 2 * BAND_SUB - BAND_HALF) | ((idx & 2) == 0))
        one = np.where(ok, 0.0, NEG_INF).astype(np.float32)
        variants.append(np.concatenate([one, one], axis=0))
    return jnp.asarray(np.stack(variants))


def _band_specs(t, n):
    hpt = t // BAND_HALF
    last = n // BAND_HALF - 1
    return [pl.BlockSpec((BAND_HALF, QKV_W), lambda r, i: (jnp.maximum(i * hpt - 1, 0), r)),
            pl.BlockSpec((t, QKV_W), lambda r, i: (i, r)),
            pl.BlockSpec((BAND_HALF, QKV_W), lambda r, i: (jnp.minimum((i + 1) * hpt, last), r)),
            pl.BlockSpec((4, 2 * BAND_SUB, 2 * BAND_SUB), lambda r, i: (0, 0, 0))]


def _band_bias(b_ref, a, nsub, i, nt):
    idx = 0
    if a == 0:
        idx = idx + (i == 0).astype(jnp.int32)
    if a == nsub - 1:
        idx = idx + 2 * (i == nt - 1).astype(jnp.int32)
    return b_ref[idx]


def _band_kv(left, main, right, p):
    kc = slice(WIDTH_AB + p * LANE, WIDTH_AB + (p + 1) * LANE)
    vc = slice(2 * WIDTH_AB + p * LANE, 2 * WIDTH_AB + (p + 1) * LANE)
    return (jnp.concatenate([left[:, kc], main[:, kc], right[:, kc]], axis=0),
            jnp.concatenate([left[:, vc], main[:, vc], right[:, vc]], axis=0))


def _banded_fwd(qkvb, dil, bias):
    s = qkvb.shape[0]
    n = s // dil
    t = min(n, BAND_TILE)
    nsub, nt = t // BAND_SUB, n // t
    view = qkvb.reshape(n, dil * QKV_W)

    def body(left, main, right, b_ref, o_ref, lse_ref):
        i = pl.program_id(1)
        for p in range(3):
            pc = slice(p * LANE, (p + 1) * LANE)
            kk, vv = _band_kv(left, main, right, p)
            for a in range(nsub):
                rows, win = slice(a * BAND_SUB, (a + 1) * BAND_SUB), slice(a * BAND_SUB, (a + 2) * BAND_SUB)
                o, lse = _softmax_pair(main[rows, pc], kk[win], vv[win], _band_bias(b_ref, a, nsub, i, nt))
                o_ref[rows, pc] = o
                lse_ref[rows, pc] = lse

    o_spec = pl.BlockSpec((t, WIDTH_AB), lambda r, i: (i, r))
    o, lse = pl.pallas_call(
        body, name=f"banded_fwd_d{dil}", grid=(dil, nt), in_specs=_band_specs(t, n), out_specs=(o_spec, o_spec),
        out_shape=(_sds((n, dil * WIDTH_AB), F32), _sds((n, dil * WIDTH_AB), F32)),
        compiler_params=_cp("parallel", "parallel"))(view, view, view, bias)
    return o.reshape(s, WIDTH_AB), lse.reshape(s, WIDTH_AB)


def _banded_bwd(qkvb, do, lse, delta, dil, bias):
    s = qkvb.shape[0]
    n = s // dil
    t = min(n, BAND_TILE)
    nsub, nt = t // BAND_SUB, n // t
    view = qkvb.reshape(n, dil * QKV_W)
    side = [a.reshape(n, dil * WIDTH_AB) for a in (do, lse, delta)]

    def body(left, main, right, b_ref, do_ref, lse_ref, dl_ref, dq_ref, dk_ref, dv_ref):
        i = pl.program_id(1)

        @pl.when(i == 0)
        def _():
            dk_ref[...] = jnp.zeros(dk_ref.shape, F32)
            dv_ref[...] = jnp.zeros(dv_ref.shape, F32)

        lrow = pl.multiple_of(jnp.maximum(i * t - BAND_HALF, 0), BAND_HALF)
        rrow = pl.multiple_of(jnp.minimum((i + 1) * t, n - BAND_HALF), BAND_HALF)
        mrow = pl.multiple_of(i * t, BAND_HALF)
        for p in range(3):
            pc = slice(p * LANE, (p + 1) * LANE)
            kk, vv = _band_kv(left, main, right, p)
            parts = []
            for a in range(nsub):
                rows, win = slice(a * BAND_SUB, (a + 1) * BAND_SUB), slice(a * BAND_SUB, (a + 2) * BAND_SUB)
                dq, dk, dv, _ = _softmax_pair_bwd(main[rows, pc], kk[win], vv[win], do_ref[rows, pc], lse_ref[rows, pc],
                                                  dl_ref[rows, pc], _band_bias(b_ref, a, nsub, i, nt))
                dq_ref[rows, pc] = dq
                parts.append((dk, dv))
            for which, ref in ((0, dk_ref), (1, dv_ref)):
                chunks = []
                for c in range(nsub + 1):
                    g = parts[c][which][:BAND_SUB] if c < nsub else None
                    if c >= 1:
                        h = parts[c - 1][which][BAND_SUB:]
                        g = h if g is None else g + h
                    chunks.append(g)
                mid = jnp.concatenate([chunks[0][BAND_HALF:]] + chunks[1:nsub] + [chunks[nsub][:BAND_HALF]], axis=0)
                ref[pl.ds(lrow, BAND_HALF), pc] += chunks[0][:BAND_HALF]
                ref[pl.ds(mrow, t), pc] += mid
                ref[pl.ds(rrow, BAND_HALF), pc] += chunks[nsub][BAND_HALF:]

    q_spec = pl.BlockSpec((t, WIDTH_AB), lambda r, i: (i, r))
    acc_spec = pl.BlockSpec((n, WIDTH_AB), lambda r, i: (0, r))
    shp = _sds((n, dil * WIDTH_AB), F32)
    outs = pl.pallas_call(
        body, name=f"banded_bwd_d{dil}", grid=(dil, nt), in_specs=_band_specs(t, n) + [q_spec, q_spec, q_spec],
        out_specs=(q_spec, acc_spec, acc_spec), out_shape=(shp, shp, shp),
        compiler_params=_cp("parallel", "arbitrary"))(view, view, view, bias, *side)
    return [a.reshape(s, WIDTH_AB) for a in outs]


def _na_geometry(s):
    rows = s // GRID_W
    assert rows >= 2 * NA_ROWS and rows % NA_ROWS == 0
    return rows, rows // NA_ROWS


def _na_row(n, i, rows):
    rq = n * NA_ROWS + i
    rs = jnp.clip(rq - NA_ROWS // 2, 0, rows - NA_ROWS)
    return pl.multiple_of(rs * GRID_W, GRID_W), rs - rq + NA_ROWS - 1


NA_KEYS = NA_ROWS * GRID_W


def _natten_fwd(qkvc, tfull):
    s = qkvc.shape[0]
    rows, nrb = _na_geometry(s)
    tq = NA_ROWS * GRID_W

    def body(q_ref, k_ref, v_ref, t_ref, o_ref, lse_ref):
        n = pl.program_id(1)
        for i in range(NA_ROWS):
            tok, base = _na_row(n, i, rows)
            kk, vv = k_ref[pl.ds(tok, NA_KEYS), :], v_ref[pl.ds(tok, NA_KEYS), :]
            sl = slice(i * GRID_W, (i + 1) * GRID_W)
            bias2 = jnp.concatenate([t_ref[0, base], t_ref[1, base]], axis=0)
            o, lse = _softmax_pair(q_ref[sl, :], kk, vv, bias2)
            o_ref[sl, :] = o
            lse_ref[sl, :] = lse

    o_spec = pl.BlockSpec((tq, LANE), lambda p, n: (n, p))
    return pl.pallas_call(
        body, name="natten_fwd", grid=(2, nrb),
        in_specs=[pl.BlockSpec((tq, LANE), lambda p, n: (n, p)), pl.BlockSpec((s, LANE), lambda p, n: (0, 2 + p)),
                  pl.BlockSpec((s, LANE), lambda p, n: (0, 4 + p)),
                  pl.BlockSpec((2, NA_ROWS, GRID_W, NA_KEYS), lambda p, n: (p, 0, 0, 0))],
        out_specs=(o_spec, o_spec), out_shape=(_sds((s, WIDTH_C), F32), _sds((s, WIDTH_C), F32)),
        compiler_params=_cp("parallel", "parallel"))(qkvc, qkvc, qkvc, tfull)


def _natten_bwd(qkvc, tfull, do, lse, delta):
    s = qkvc.shape[0]
    rows, nrb = _na_geometry(s)
    tq = NA_ROWS * GRID_W

    def body(q_ref, k_ref, v_ref, t_ref, do_ref, lse_ref, dl_ref, dq_ref, dk_ref, dv_ref, dt_ref):
        n = pl.program_id(1)

        @pl.when(n == 0)
        def _():
            dk_ref[...] = jnp.zeros(dk_ref.shape, F32)
            dv_ref[...] = jnp.zeros(dv_ref.shape, F32)
            dt_ref[...] = jnp.zeros(dt_ref.shape, F32)

        for i in range(NA_ROWS):
            tok, base = _na_row(n, i, rows)
            win = pl.ds(tok, NA_KEYS)
            sl = slice(i * GRID_W, (i + 1) * GRID_W)
            bias2 = jnp.concatenate([t_ref[0, base], t_ref[1, base]], axis=0)
            dq, dk, dv, ds = _softmax_pair_bwd(q_ref[sl, :], k_ref[win, :], v_ref[win, :], do_ref[sl, :], lse_ref[sl, :],
                                               dl_ref[sl, :], bias2)
            dq_ref[sl, :] = dq
            dk_ref[win, :] += dk
            dv_ref[win, :] += dv
            dt_ref[0, base] += ds[:GRID_W]
            dt_ref[1, base] += ds[GRID_W:]

    q_spec = pl.BlockSpec((tq, LANE), lambda p, n: (n, p))
    acc_spec = pl.BlockSpec((s, LANE), lambda p, n: (0, p))
    t_spec = pl.BlockSpec((2, NA_ROWS, GRID_W, NA_KEYS), lambda p, n: (p, 0, 0, 0))
    shp = _sds((s, WIDTH_C), F32)
    return pl.pallas_call(
        body, name="natten_bwd", grid=(2, nrb),
        in_specs=[q_spec, pl.BlockSpec((s, LANE), lambda p, n: (0, 2 + p)), pl.BlockSpec((s, LANE), lambda p, n: (0, 4 + p)),
                  t_spec, q_spec, q_spec, q_spec],
        out_specs=(q_spec, acc_spec, acc_spec, t_spec),
        out_shape=(shp, shp, shp, _sds((HEADS_C, NA_ROWS, GRID_W, NA_KEYS), F32)),
        compiler_params=_cp("parallel", "arbitrary"))(qkvc, qkvc, qkvc, tfull, do, lse, delta)


def _rpb_constants():
    p = np.arange(GRID_W)[:, None]
    qc = np.arange(GRID_W)[None, :]
    dc = np.clip(qc - p, -(NA_COLS - 1), NA_COLS - 1) + NA_COLS - 1
    onehot = (dc.reshape(1, -1) == np.arange(32)[:, None]).astype(np.float32)
    c_start = np.clip(p - NA_COLS // 2, 0, GRID_W - NA_COLS)
    col_ok = ((qc >= c_start) & (qc < c_start + NA_COLS)).reshape(1, -1).astype(np.float32)
    a = np.arange(16)[:, None]
    bj = np.arange(64)[None, :]
    row_sel = ((bj // 8 + bj % 8) == a).astype(np.float32)
    return jnp.asarray(onehot), jnp.asarray(col_ok), jnp.asarray(row_sel)


def _rpb_expand(rpb, onehot, col_ok):
    r2 = jnp.pad(rpb.reshape(HEADS_C * 15, 31), ((0, 4), (0, 1)))

    def body(r_ref, oh_ref, ok_ref, o_ref):
        t = jnp.dot(r_ref[...], oh_ref[...], preferred_element_type=F32, precision=lax.Precision.HIGHEST)
        o_ref[...] = jnp.where(ok_ref[...] > 0.5, t, NEG_INF)

    tm = pl.pallas_call(body, name="rpb_expand", out_shape=_sds((64, GRID_W * GRID_W), F32))(r2, onehot, col_ok)
    tm = tm[:HEADS_C * 15].reshape(HEADS_C, 15, GRID_W, GRID_W)
    tfull = jnp.stack([jnp.concatenate([tm[:, base + j] for j in range(NA_ROWS)], axis=-1) for base in range(NA_ROWS)], axis=1)
    return tfull


def _rpb_grad(dtfull, onehot, row_sel):
    g = dtfull.reshape(HEADS_C, NA_ROWS, GRID_W, NA_ROWS, GRID_W).transpose(0, 1, 3, 2, 4).reshape(HEADS_C, 64, GRID_W * GRID_W)

    def body(g_ref, oh_ref, sel_ref, o_ref):
        for h in range(HEADS_C):
            mid = lax.dot_general(g_ref[h], oh_ref[...], (((1,), (1,)), ((), ())), preferred_element_type=F32,
                                  precision=lax.Precision.HIGHEST)
            o_ref[h] = jnp.dot(sel_ref[...], mid, preferred_element_type=F32, precision=lax.Precision.HIGHEST)

    out = pl.pallas_call(body, name="rpb_grad", out_shape=_sds((HEADS_C, 16, 32), F32))(g, onehot, row_sel)
    return out[:, :15, :31]


def _outnorm_fwd(o_a, branch_o, branch_lse, o_c, ga, gb, gc):
    s = o_a.shape[0]
    ts = _tile(s, (512, 256, 128))

    def body(a_ref, o1, o2, o3, l1, l2, l3, c_ref, ga_ref, gb_ref, gc_ref, ob_ref, lse_ref, o_ref):
        la, lb, lc = l1[...], l2[...], l3[...]
        m = jnp.maximum(jnp.maximum(la, lb), lc)
        ea, eb, ec = jnp.exp(la - m), jnp.exp(lb - m), jnp.exp(lc - m)
        den = ea + eb + ec
        o_b = (o1[...] * ea + o2[...] * eb + o3[...] * ec) / den
        ob_ref[...] = o_b
        lse_ref[...] = m + jnp.log(den)
        col = 0
        for x, g in ((a_ref[...], ga_ref), (o_b, gb_ref), (c_ref[...], gc_ref)):
            o_ref[:, col:col + x.shape[1]] = (x * _rstd(x) * g[...]).astype(BF16)
            col += x.shape[1]

    sp = _row_spec(ts, WIDTH_AB)
    return pl.pallas_call(
        body, name="outnorm_fwd", grid=(s // ts,),
        in_specs=[sp] * 7 + [_row_spec(ts, WIDTH_C), _fix_spec(WIDTH_AB), _fix_spec(WIDTH_AB), _fix_spec(WIDTH_C)],
        out_specs=(sp, sp, _row_spec(ts, D_MODEL)),
        out_shape=(_sds((s, WIDTH_AB), F32), _sds((s, WIDTH_AB), F32), _sds((s, D_MODEL), BF16)),
        compiler_params=_cp("parallel"))(o_a, *branch_o, *branch_lse, o_c, ga.reshape(1, -1), gb.reshape(1, -1),
                                         gc.reshape(1, -1))


def _outnorm_bwd(dxb, w_out, o_a, o_b, o_c, ga, gb, gc):
    s = o_a.shape[0]
    ts = _tile(s, (512, 256, 128))

    def body(dx_ref, w_ref, a_ref, b_ref, c_ref, ga_ref, gb_ref, gc_ref, *outs):
        first = pl.program_id(0) == 0
        dm = lax.dot_general(dx_ref[...], w_ref[...], (((1,), (1,)), ((), ())), preferred_element_type=F32)
        col = 0
        for k, (ref, g) in enumerate(((a_ref, ga_ref), (b_ref, gb_ref), (c_ref, gc_ref))):
            x = ref[...]
            w = x.shape[1]
            dx, dg = _rms_bwd_rows(x, g[...], dm[:, col:col + w])
            col += w
            outs[k][...] = dx.astype(BF16)
            for b, blk in enumerate(_group_sum(dx * x)):
                if k == 0:
                    outs[3][b] = _stat_rows(blk)
                else:
                    outs[3 + k][:, b * LANE:(b + 1) * LANE] = blk
            _accum(outs[6 + k], dg, first)

    widths = (WIDTH_AB, WIDTH_AB, WIDTH_C)
    return pl.pallas_call(
        body, name="outnorm_bwd", grid=(s // ts,),
        in_specs=[_row_spec(ts, D_MODEL), pl.BlockSpec(w_out.shape, lambda i: (0, 0))] + [_row_spec(ts, w) for w in widths]
        + [_fix_spec(w) for w in widths],
        out_specs=tuple([_row_spec(ts, w) for w in widths] + [pl.BlockSpec((3, 8, ts), lambda i: (0, 0, i))]
                        + [_row_spec(ts, w) for w in widths[1:]] + [_fix_spec(w) for w in widths]),
        out_shape=tuple([_sds((s, w), BF16) for w in widths] + [_sds((3, 8, s), F32)]
                        + [_sds((s, w), F32) for w in widths[1:]] + [_sds((1, w), F32) for w in widths]),
        compiler_params=_cp("arbitrary"))(dxb, w_out, o_a, o_b, o_c, ga.reshape(1, -1), gb.reshape(1, -1),
                                          gc.reshape(1, -1))


def _adamw(w, g, m, v, *, name):
    r, c = w.shape
    tr = _tile(r, (512, 256, 128, 64, 8))

    def body(w_ref, g_ref, m_ref, v_ref, d_ref, nm_ref, nv_ref):
        gv = g_ref[...]
        nm = ADAM_B1 * m_ref[...] + (1.0 - ADAM_B1) * gv
        nv = ADAM_B2 * v_ref[...] + (1.0 - ADAM_B2) * jnp.square(gv)
        m_hat = nm / (1.0 - ADAM_B1 ** ADAM_STEP)
        v_hat = nv / (1.0 - ADAM_B2 ** ADAM_STEP)
        d_ref[...] = -ADAM_LR * (m_hat / (jnp.sqrt(v_hat) + ADAM_EPS) + ADAM_WD * w_ref[...])
        nm_ref[...] = nm
        nv_ref[...] = nv

    sp = _row_spec(tr, c)
    return pl.pallas_call(
        body, name=name, grid=(r // tr,), in_specs=[sp] * 4, out_specs=(sp, sp, sp),
        out_shape=(_sds((r, c), F32),) * 3, compiler_params=_cp("parallel"))(w, g, m, v)


def _add_n(parts, *, name, out_dtype):
    r, c = parts[0].shape
    tr = max(t for t in range(16, 1025, 16) if r % t == 0)

    def body(*refs):
        acc = refs[0][...].astype(F32)
        for ref in refs[1:-1]:
            acc = acc + ref[...].astype(F32)
        refs[-1][...] = acc.astype(out_dtype)

    sp = _row_spec(tr, c)
    return pl.pallas_call(
        body, name=name, grid=(r // tr,), in_specs=[sp] * len(parts), out_specs=sp, out_shape=_sds((r, c), out_dtype),
        compiler_params=_cp("parallel"))(*parts)


ANY = pl.BlockSpec(memory_space=pl.ANY)
CHIP_FLIPS = ((1, 0), (0, 1), (1, 1))


def _me():
    return lax.axis_index("x"), lax.axis_index("y"), lax.axis_index("c")


def _gather_chips(half):
    def body(src, out, send_sems, recv_sems):
        x, y, c = _me()
        mine = 2 * x + y

        def copy(k, chip, half_idx, to, source=None):
            dst = out.at[chip, half_idx]
            return pltpu.make_async_remote_copy(src_ref=dst if source is None else source, dst_ref=dst,
                                                send_sem=send_sems.at[k], recv_sem=recv_sems.at[k], device_id=to,
                                                device_id_type=MESH_T)

        chips = [(x ^ fx, y ^ fy) for fx, fy in CHIP_FLIPS]
        first = [copy(k, mine, c, (cx, cy, c), source=src) for k, (cx, cy) in enumerate(chips)]
        for cp in first:
            cp.start()
        passed = []
        for k, (cx, cy) in enumerate(chips):
            theirs = 2 * cx + cy
            copy(k, theirs, c, (x, y, c)).wait_recv()
            cp = copy(3 + k, theirs, c, (x, y, 1 - c))
            cp.start()
            passed.append(cp)
        for k, (cx, cy) in enumerate(chips):
            copy(3 + k, 2 * cx + cy, 1 - c, (x, y, c)).wait_recv()
        for cp in first + passed:
            cp.wait_send()

    return pl.pallas_call(
        body, name="gather_chips", in_specs=[ANY], out_specs=ANY, out_shape=_sds((4, 2) + half.shape, half.dtype),
        scratch_shapes=[pltpu.SemaphoreType.DMA((6,)), pltpu.SemaphoreType.DMA((6,))])(half)


def _swap_sibling(block):
    def body(src, out, send_sem, recv_sem):
        x, y, c = _me()
        cp = pltpu.make_async_remote_copy(src_ref=src, dst_ref=out, send_sem=send_sem, recv_sem=recv_sem,
                                          device_id=(x, y, 1 - c), device_id_type=MESH_T)
        cp.start()
        cp.wait()

    return pl.pallas_call(
        body, name="swap_sibling", in_specs=[ANY], out_specs=ANY, out_shape=_sds(block.shape, block.dtype),
        scratch_shapes=[pltpu.SemaphoreType.DMA(()), pltpu.SemaphoreType.DMA(())])(block)


def _swap_other_half(halves):
    def body(src, out, send_sem, recv_sem):
        x, y, c = _me()
        cp = pltpu.make_async_remote_copy(src_ref=src.at[:, 1 - c], dst_ref=out, send_sem=send_sem, recv_sem=recv_sem,
                                          device_id=(x, y, 1 - c), device_id_type=MESH_T)
        cp.start()
        cp.wait()

    shape = (halves.shape[0],) + halves.shape[2:]
    return pl.pallas_call(
        body, name="swap_other_half", in_specs=[ANY], out_specs=ANY, out_shape=_sds(shape, halves.dtype),
        scratch_shapes=[pltpu.SemaphoreType.DMA(()), pltpu.SemaphoreType.DMA(())])(halves)


def _pair_sum(halves, core, other):
    n, _, h, c = halves.shape
    tr = max(t for t in range(16, 1025, 16) if h % t == 0)

    def body(core_ref, a_ref, b_ref, o_ref):
        o_ref[...] = (a_ref[...] + b_ref[...]).astype(BF16)

    grid_spec = pltpu.PrefetchScalarGridSpec(
        num_scalar_prefetch=1, grid=(n, h // tr),
        in_specs=[pl.BlockSpec((None, None, tr, c), lambda j, i, core_ref: (j, core_ref[0], i, 0)),
                  pl.BlockSpec((None, tr, c), lambda j, i, core_ref: (j, i, 0))],
        out_specs=pl.BlockSpec((None, tr, c), lambda j, i, core_ref: (j, i, 0)))
    return pl.pallas_call(
        body, name="pair_sum", grid_spec=grid_spec, out_shape=_sds((n, h, c), BF16),
        compiler_params=_cp("parallel", "parallel"))(core.reshape(1).astype(jnp.int32), halves, other)


def _scatter_chips(parts):
    def body(src, out, send_sems, recv_sems):
        x, y, c = _me()
        mine = 2 * x + y
        sends = []
        for k, (fx, fy) in enumerate(CHIP_FLIPS):
            theirs = 2 * (x ^ fx) + (y ^ fy)
            cp = pltpu.make_async_remote_copy(src_ref=src.at[theirs], dst_ref=out.at[mine], send_sem=send_sems.at[k],
                                              recv_sem=recv_sems.at[k], device_id=(x ^ fx, y ^ fy, c), device_id_type=MESH_T)
            cp.start()
            sends.append(cp)
        for k, (fx, fy) in enumerate(CHIP_FLIPS):
            theirs = 2 * (x ^ fx) + (y ^ fy)
            pltpu.make_async_remote_copy(src_ref=src.at[theirs], dst_ref=out.at[theirs], send_sem=send_sems.at[k],
                                         recv_sem=recv_sems.at[k], device_id=(x ^ fx, y ^ fy, c),
                                         device_id_type=MESH_T).wait_recv()
        for cp in sends:
            cp.wait_send()

    return pl.pallas_call(
        body, name="scatter_chips", in_specs=[ANY], out_specs=ANY, out_shape=_sds(parts.shape, parts.dtype),
        scratch_shapes=[pltpu.SemaphoreType.DMA((3,)), pltpu.SemaphoreType.DMA((3,))])(parts)


def _all_reduce_small(block):
    r, c = block.shape

    def body(src, out, slots, send_sems, recv_sems):
        x, y, cc = _me()
        mine = 4 * x + 2 * y + cc
        slots[mine] = src[...]
        sends = []
        for k in range(1, 8):
            fx, fy, fc = (k >> 2) & 1, (k >> 1) & 1, k & 1
            cp = pltpu.make_async_remote_copy(src_ref=src, dst_ref=slots.at[mine], send_sem=send_sems.at[k - 1],
                                              recv_sem=recv_sems.at[k - 1], device_id=(x ^ fx, y ^ fy, cc ^ fc),
                                              device_id_type=MESH_T)
            cp.start()
            sends.append(cp)
        for k in range(1, 8):
            fx, fy, fc = (k >> 2) & 1, (k >> 1) & 1, k & 1
            theirs = 4 * (x ^ fx) + 2 * (y ^ fy) + (cc ^ fc)
            pltpu.make_async_remote_copy(src_ref=src, dst_ref=slots.at[theirs], send_sem=send_sems.at[k - 1],
                                         recv_sem=recv_sems.at[k - 1], device_id=(x ^ fx, y ^ fy, cc ^ fc),
                                         device_id_type=MESH_T).wait_recv()
        for cp in sends:
            cp.wait_send()
        acc = slots[0]
        for d in range(1, 8):
            acc = acc + slots[d]
        out[...] = acc

    vm = pl.BlockSpec(memory_space=pltpu.VMEM)
    return pl.pallas_call(
        body, name="all_reduce_small", in_specs=[vm], out_specs=vm, out_shape=_sds((r, c), F32),
        scratch_shapes=[pltpu.VMEM((8, r, c), F32), pltpu.SemaphoreType.DMA((7,)), pltpu.SemaphoreType.DMA((7,))])(block)


DIRECT = ("w_mlp_in", "w_mlp_out", "w_out")
BIG = DIRECT + ("w_in", "w_uq", "w_ukv")
COL_SHARDED = {"w_in": True, "w_uq": True, "w_ukv": True, "w_out": False, "w_mlp_in": True, "w_mlp_out": False}
SMALL = ("g_mix", "q_norm", "kv_norm", "rpb", "out_norm_a", "out_norm_b", "out_norm_c", "g_mlp", "g_final")
PACK_C = 1024
ROW_ALIGN = 32


def _pack_rows(parts):
    flat = jnp.concatenate([p.reshape(-1, PACK_C) for p in parts], axis=0)
    return jnp.pad(flat, ((0, -flat.shape[0] % ROW_ALIGN), (0, 0)))


def _unpack_rows(flat, shapes):
    out, row = [], 0
    for shp in shapes:
        n = int(np.prod(shp)) // PACK_C
        out.append(flat[row:row + n].reshape(shp))
        row += n
    return out


def _full_from_shards(name, g):
    if COL_SHARDED[name]:
        return g.transpose(1, 2, 0, 3).reshape(g.shape[1], g.shape[2], 4 * g.shape[3])
    return g.transpose(1, 0, 2, 3).reshape(g.shape[1], 4 * g.shape[2], g.shape[3])


def _shards_from_full(name, w):
    l, k, n = w.shape
    if COL_SHARDED[name]:
        return w.reshape(l, k, 4, n // 4).transpose(2, 0, 1, 3)
    return w.reshape(l, 4, k // 4, n).transpose(1, 0, 2, 3)


def _arrange_w_in(w):
    z = jnp.zeros(w.shape[:-1] + (COL_B - COL_KPE - QK_ROPE,), w.dtype)
    return jnp.concatenate([w[..., :COL_KPE + QK_ROPE], z, w[..., COL_KPE + QK_ROPE:]], axis=-1)


def _unarrange_w_in(w):
    return jnp.concatenate([w[..., :COL_KPE + QK_ROPE], w[..., COL_B:]], axis=-1)


def _arrange_w_uq(w):
    per = HEAD_DIM + QK_ROPE
    z = jnp.zeros(w.shape[:-1] + (LANE - per,), w.dtype)
    cols = []
    for h in range(HEADS_A):
        cols += [w[..., h * per:(h + 1) * per], z]
    return jnp.concatenate(cols, axis=-1)


def _unarrange_w_uq(w):
    per = HEAD_DIM + QK_ROPE
    return jnp.concatenate([w[..., h * LANE:h * LANE + per] for h in range(HEADS_A)], axis=-1)


def _arrange_w_ukv(w):
    z = jnp.zeros(w.shape[:-1] + (HEAD_DIM,), w.dtype)
    ks = []
    for h in range(HEADS_A):
        ks += [w[..., h * LANE:h * LANE + HEAD_DIM], z]
    vs = [w[..., h * LANE + HEAD_DIM:(h + 1) * LANE] for h in range(HEADS_A)]
    return jnp.concatenate(ks + vs, axis=-1)


def _unarrange_w_ukv(w):
    cols = []
    for h in range(HEADS_A):
        cols += [w[..., h * LANE:h * LANE + HEAD_DIM], w[..., W_A2 + h * HEAD_DIM:W_A2 + (h + 1) * HEAD_DIM]]
    return jnp.concatenate(cols, axis=-1)


def _layer_fwd(x, w, sm, tabs, consts):
    t32, t64 = tabs
    onehot, col_ok, _, band = consts
    h, proj = _norm_mm(x, sm["g_mix"], w["w_in"], name="in_proj", relu2=False)
    cqn, ckvn, kpe, qkvb, qkvc = _prep_fwd(proj, sm["q_norm"], sm["kv_norm"], t32, t64)
    qa2, ka2, kat, va1 = _a_post_fwd(cqn, ckvn, w["w_uq"], w["w_ukv"], kpe, t32)
    o_a, lse_a = _dense_fwd(qa2, ka2, va1)
    branch = [_banded_fwd(qkvb, dil, band) for _, dil in DILATED_PAIRS]
    tfull = _rpb_expand(sm["rpb"], onehot, col_ok)
    o_c, lse_c = _natten_fwd(qkvc, tfull)
    o_b, lse_b, mixed = _outnorm_fwd(o_a, [b[0] for b in branch], [b[1] for b in branch], o_c, sm["out_norm_a"],
                                     sm["out_norm_b"], sm["out_norm_c"])
    x_mid = _mm_nn(mixed, w["w_out"], name="out_proj", res=x)
    h2, act = _norm_mm(x_mid, sm["g_mlp"], w["w_mlp_in"], name="mlp_in", relu2=True)
    x_out = _mm_nn(act, w["w_mlp_out"], name="mlp_out", res=x_mid)
    saved = dict(x=x, h=h, proj=proj, cqn=cqn, ckvn=ckvn, qkvb=qkvb, qkvc=qkvc, qa2=qa2, ka2=ka2, kat=kat, va1=va1, o_a=o_a,
                 lse_a=lse_a, o_b=o_b, lse_b=lse_b, o_c=o_c, lse_c=lse_c, tfull=tfull, mixed=mixed, x_mid=x_mid, h2=h2,
                 act=act)
    return x_out, saved


def _layer_bwd(dx, dxb, sv, w, sm, tabs, consts, packed, places):
    t32, t64 = tabs
    onehot, _, row_sel, band = consts
    g = {}
    du = _mm_nt(dxb, w["w_mlp_out"], name="mlp_out_dx", out_dtype=BF16, relu2_act=sv["act"])
    packed = _mm_tn(sv["act"], dxb, name="mlp_out_dw", packed=(packed,) + places["w_mlp_out"])
    dx_mid, dmb, g["g_mlp"] = _mm_nt_norm_bwd(du, w["w_mlp_in"], sv["x_mid"], sm["g_mlp"], dx, name="mlp_in_dx")
    packed = _mm_tn(sv["h2"], du, name="mlp_in_dw", packed=(packed,) + places["w_mlp_in"])
    packed = _mm_tn(sv["mixed"], dmb, name="out_proj_dw", packed=(packed,) + places["w_out"])
    (do_a, do_b, do_c, dl_a, dl_b, dl_c, g["out_norm_a"], g["out_norm_b"], g["out_norm_c"]) = _outnorm_bwd(
        dmb, w["w_out"], sv["o_a"], sv["o_b"], sv["o_c"], sm["out_norm_a"], sm["out_norm_b"], sm["out_norm_c"])
    dqa2_t, dka2, dva = _dense_bwd(sv["qa2"], sv["ka2"], sv["kat"], sv["va1"], do_a, sv["lse_a"], dl_a)
    db = []
    for _, dil in DILATED_PAIRS:
        db += _banded_bwd(sv["qkvb"], do_b, sv["lse_b"], dl_b, dil, band)
    dq_c, dk_c, dv_c, dtfull = _natten_bwd(sv["qkvc"], sv["tfull"], do_c, sv["lse_c"], dl_c)
    g["rpb"] = _rpb_grad(dtfull, onehot, row_sel)
    dqa, dkva, dkpe = _a_post_bwd(dqa2_t, dka2, dva, t32)
    dcqn = _mm_nt(dqa, w["w_uq"], name="q_up_dx")
    g["w_uq"] = _unarrange_w_uq(_mm_tn(sv["cqn"], dqa, name="q_up_dw"))
    dckvn = _mm_nt(dkva, w["w_ukv"], name="kv_up_dx")
    g["w_ukv"] = _unarrange_w_ukv(_mm_tn(sv["ckvn"], dkva, name="kv_up_dw"))
    dproj, g["q_norm"], g["kv_norm"] = _prep_bwd(sv["proj"], sm["q_norm"], sm["kv_norm"], t32, t64, dcqn, dckvn, dkpe,
                                                  db, (dq_c, dk_c, dv_c))
    g["w_in"] = _unarrange_w_in(_mm_tn(sv["h"], dproj, name="in_proj_dw"))
    dx_in, dxb_in, g["g_mix"] = _mm_nt_norm_bwd(dproj, w["w_in"], sv["x"], sm["g_mix"], dx_mid, name="in_proj_dx")
    return dx_in, dxb_in, g, packed


def _packed_places(offs, l):
    d, r = D_MODEL, D_MODEL // 4
    return {"w_mlp_in": (512, lambda i, j: (j, (offs["w_mlp_in"] + l * d) // 512 + i)),
            "w_mlp_out": (512, lambda i, j: (i // 2, (offs["w_mlp_out"] + l * d) // 512 + i % 2)),
            "w_out": (r, lambda i, j: (i, (offs["w_out"] + l * r) // r))}


def _local_step(x, target, wfull, small, packed_shape, offs):
    s = x.shape[0]
    tabs = (_rope_tables(s, QK_ROPE // 2, 1, lead=HEAD_DIM), _rope_tables(s, HEAD_DIM // 2, 2))
    consts = _rpb_constants() + (_band_bias_table(),)
    saved = []
    for l in range(DEPTH):
        wl = {k: v[l] for k, v in wfull.items()}
        sl = {k: small[k][l] for k in SMALL if k != "g_final"}
        x, sv = _layer_fwd(x, wl, sl, tabs, consts)
        saved.append(sv)
    loss, dx, dxb, dg_final = _loss_head(x, small["g_final"], target)
    grads = [None] * DEPTH
    packed = packed_shape
    for l in reversed(range(DEPTH)):
        wl = {k: v[l] for k, v in wfull.items()}
        sl = {k: small[k][l] for k in SMALL if k != "g_final"}
        dx, dxb, grads[l], packed = _layer_bwd(dx, dxb, saved[l], wl, sl, tabs, consts, packed, _packed_places(offs, l))
    return loss, dx, grads, dg_final, packed


ARRANGE = {"w_in": _arrange_w_in, "w_uq": _arrange_w_uq, "w_ukv": _arrange_w_ukv}


def kernel(x, g_mix, w_in, q_norm, w_uq, kv_norm, w_ukv, rpb, out_norm_a, out_norm_b, out_norm_c, w_out, g_mlp, w_mlp_in, w_mlp_out, g_final, loss_target, m_g_mix, m_w_in, m_q_norm, m_w_uq, m_kv_norm, m_w_ukv, m_rpb, m_out_norm_a, m_out_norm_b, m_out_norm_c, m_w_out, m_g_mlp, m_w_mlp_in, m_w_mlp_out, m_g_final, v_g_mix, v_w_in, v_q_norm, v_w_uq, v_kv_norm, v_w_ukv, v_rpb, v_out_norm_a, v_out_norm_b, v_out_norm_c, v_w_out, v_g_mlp, v_w_mlp_in, v_w_mlp_out, v_g_final):
    args = dict(locals())
    weights = {k: args[k] for k in BIG + SMALL}
    moms = {k: args["m_" + k] for k in BIG + SMALL}
    vels = {k: args["v_" + k] for k in BIG + SMALL}
    cc = lax.axis_index("c")
    my_chip = 2 * lax.axis_index("x") + lax.axis_index("y")

    shard_shapes = [weights[k].shape for k in BIG]
    packed_w = _pack_rows([weights[k].astype(BF16) for k in BIG])
    rows = packed_w.shape[0]
    my_half = lax.dynamic_index_in_dim(packed_w.reshape(2, rows // 2, PACK_C), cc, axis=0, keepdims=False)
    gathered = _gather_chips(my_half).reshape(4, rows, PACK_C)
    per_chip = [_unpack_rows(jnp.where(my_chip == j, packed_w, gathered[j]), shard_shapes) for j in range(4)]
    wfull = {}
    for idx, k in enumerate(BIG):
        full = _full_from_shards(k, jnp.stack([per_chip[j][idx] for j in range(4)]))
        wfull[k] = ARRANGE[k](full) if k in ARRANGE else full

    small = {k: weights[k] for k in SMALL}
    offs, row = {}, 0
    for k, shp in zip(BIG, shard_shapes):
        offs[k] = row
        row += int(np.prod(shp)) // PACK_C
    loss, dx, grads, dg_final, packed = _local_step(x[0], loss_target[0], wfull, small, _sds((4, rows, PACK_C), F32), offs)

    small_local = {k: jnp.stack([grads[l][k].reshape(weights[k].shape[1:]) for l in range(DEPTH)])
                   for k in SMALL if k != "g_final"}
    small_local["g_final"] = dg_final.reshape(-1)
    small_shapes = [weights[k].shape for k in SMALL]
    n_small = sum(int(np.prod(s)) for s in small_shapes)
    flat = jnp.concatenate([small_local[k].reshape(-1) for k in SMALL] + [loss[0, :1]])
    rows_small = -(-(n_small + 1) // PACK_C)
    rows_small += -rows_small % 8
    flat = jnp.pad(flat, (0, rows_small * PACK_C - n_small - 1)).reshape(rows_small, PACK_C)
    red = _all_reduce_small(flat).reshape(-1)
    loss_out = red[n_small]
    small_grads, off = {}, 0
    for k, shp in zip(SMALL, small_shapes):
        n = int(np.prod(shp))
        small_grads[k] = red[off:off + n].reshape(shp)
        off += n

    rest = [k for k in BIG if k not in DIRECT]
    by_shard = {k: _shards_from_full(k, jnp.stack([grads[l][k] for l in range(DEPTH)])) for k in rest}
    tail = jnp.stack([_pack_rows([by_shard[k][j] for k in rest]) for j in range(4)])
    assert offs[rest[0]] + tail.shape[1] == rows
    packed = lax.dynamic_update_slice(packed, tail, (0, offs[rest[0]], 0))
    halves = packed.reshape(4, 2, rows // 2, PACK_C)
    pair = _pair_sum(halves, cc, _swap_other_half(halves))
    by_chip = _scatter_chips(pair)
    reduced = _add_n([jnp.where(my_chip == j, pair[j], by_chip[j]) for j in range(4)], name="chip_sum", out_dtype=F32)
    theirs = _swap_sibling(reduced)
    joined = jnp.where(cc == 0, jnp.concatenate([reduced, theirs]), jnp.concatenate([theirs, reduced]))
    big_grads = dict(zip(BIG, _unpack_rows(joined, shard_shapes)))

    out_g, out_d, out_m, out_v = {}, {}, {}, {}
    for k in BIG:
        shp = weights[k].shape
        two_d = (shp[0] * shp[1], shp[2])
        d, nm, nv = _adamw(weights[k].reshape(two_d), big_grads[k].reshape(two_d), moms[k].reshape(two_d),
                           vels[k].reshape(two_d), name="adamw_" + k)
        out_g[k], out_d[k], out_m[k], out_v[k] = big_grads[k], d.reshape(shp), nm.reshape(shp), nv.reshape(shp)

    def pack_small(tree):
        f = jnp.concatenate([tree[k].reshape(-1) for k in SMALL])
        return jnp.pad(f, (0, rows_small * PACK_C - n_small)).reshape(rows_small, PACK_C)

    d, nm, nv = _adamw(pack_small(small), pack_small(small_grads), pack_small(moms), pack_small(vels), name="adamw_small")
    for tree, flat_out in ((out_d, d), (out_m, nm), (out_v, nv)):
        off = 0
        fo = flat_out.reshape(-1)
        for k, shp in zip(SMALL, small_shapes):
            n = int(np.prod(shp))
            tree[k] = fo[off:off + n].reshape(shp)
            off += n
    out_g.update(small_grads)

    order = ("g_mix", "w_in", "q_norm", "w_uq", "kv_norm", "w_ukv", "rpb", "out_norm_a", "out_norm_b", "out_norm_c", "w_out",
             "g_mlp", "w_mlp_in", "w_mlp_out", "g_final")
    return (loss_out, dx.reshape(x.shape), *[out_g[k] for k in order], *[out_d[k] for k in order],
            *[out_m[k] for k in order], *[out_v[k] for k in order])
```

```python
import math

import numpy as np
import jax
import jax.numpy as jnp
from jax import lax
from jax.experimental import pallas as pl
from jax.experimental.pallas import tpu as pltpu

F32 = jnp.float32
BF16 = jnp.bfloat16

D_MODEL = 1024
HEAD_DIM = 64
Q_LORA = 256
KV_LORA = 128
QK_ROPE = 32
HEADS_A = 6
HEADS_B = 6
HEADS_C = 4
DILATED_PAIRS = ((128, 1), (512, 4), (2048, 16))
BAND_HALF = 64
GRID_W = 64
NA_ROWS = 8
NA_COLS = 16
D_FF = 4096
ROPE_THETA = 10000.0
NORM_EPS = 1e-6
NEG_INF = -1e30
DEPTH = 4

LANE = 128
PROJ_W = 2432
COL_CKV = 256
COL_KPE = 384
COL_B = 512
COL_C = 1664
W_A2 = 768
W_KV = W_A2 + 384
WIDTH_AB = 384
WIDTH_C = 256
SCALE_A = (HEAD_DIM + QK_ROPE) ** -0.5
SCALE_BC = HEAD_DIM ** -0.5

ADAM_LR = 0.001
ADAM_B1 = 0.9
ADAM_B2 = 0.999
ADAM_EPS = 1e-08
ADAM_WD = 0.01
ADAM_STEP = 10

VMEM_LIMIT = 56 * 1024 * 1024
MESH_T = pl.DeviceIdType.MESH


def _cp(*sem):
    return pltpu.CompilerParams(dimension_semantics=sem or None, vmem_limit_bytes=VMEM_LIMIT)


def _tile(n, cands):
    for c in cands:
        if n % c == 0:
            return c
    return n


def _sds(shape, dtype):
    return jax.ShapeDtypeStruct(shape, dtype)


ROW_TILE_BYTES = 12 * 1024 * 1024


def _row_tiles(row_bytes):
    return tuple(t for t in (2048, 1024, 512, 256, 128) if t * row_bytes <= ROW_TILE_BYTES or t <= 512)


def _mm_nn(a, b, *, name, out_dtype=F32, res=None):
    m, k = a.shape
    n = b.shape[1]
    tn = _tile(n, (1024, 768, 512)) if n % LANE == 0 and n != PROJ_W else n
    tm = _tile(m, _row_tiles(2 * k + tn * (jnp.dtype(out_dtype).itemsize + (4 if res is not None else 0))))

    def body(*refs):
        a_ref, b_ref = refs[0], refs[1]
        o_ref = refs[-1]
        acc = jnp.dot(a_ref[...], b_ref[...], preferred_element_type=F32)
        if res is not None:
            acc = refs[2][...] + acc
        o_ref[...] = acc.astype(o_ref.dtype)

    in_specs = [pl.BlockSpec((tm, k), lambda j, i: (i, 0)), pl.BlockSpec((k, tn), lambda j, i: (0, j))]
    args = [a, b]
    if res is not None:
        in_specs.append(pl.BlockSpec((tm, tn), lambda j, i: (i, j)))
        args.append(res)
    return pl.pallas_call(
        body, name=name, grid=(n // tn, m // tm), in_specs=in_specs,
        out_specs=pl.BlockSpec((tm, tn), lambda j, i: (i, j)), out_shape=_sds((m, n), out_dtype),
        compiler_params=_cp("parallel", "parallel"))(*args)


def _norm_mm(x, g, w, *, name, relu2):
    m, k = x.shape
    n = w.shape[1]
    tm = _tile(m, (512, 256, 128) if n <= PROJ_W else (256, 128))

    def body(x_ref, g_ref, w_ref, h_ref, o_ref):
        xv = x_ref[...]
        h = (xv * _rstd(xv) * g_ref[...]).astype(BF16)
        h_ref[...] = h
        acc = jnp.dot(h, w_ref[...], preferred_element_type=F32)
        if relu2:
            acc = jnp.square(jnp.maximum(acc, 0.0))
        o_ref[...] = acc.astype(o_ref.dtype)

    return pl.pallas_call(
        body, name=name, grid=(m // tm,),
        in_specs=[pl.BlockSpec((tm, k), lambda i: (i, 0)), pl.BlockSpec((1, k), lambda i: (0, 0)),
                  pl.BlockSpec((k, n), lambda i: (0, 0))],
        out_specs=(pl.BlockSpec((tm, k), lambda i: (i, 0)), pl.BlockSpec((tm, n), lambda i: (i, 0))),
        out_shape=(_sds((m, k), BF16), _sds((m, n), BF16 if relu2 else F32)),
        compiler_params=_cp("parallel"))(x, g.reshape(1, k), w)


def _mm_nt(a, b, *, name, out_dtype=F32, relu2_act=None):
    m, c = a.shape
    n = b.shape[0]
    tn = _tile(n, (1024, 512, 256, 128))
    tm = _tile(m, _row_tiles(2 * c + tn * (jnp.dtype(out_dtype).itemsize + (2 if relu2_act is not None else 0))))

    def body(*refs):
        a_ref, b_ref = refs[0], refs[1]
        o_ref = refs[-1]
        acc = lax.dot_general(a_ref[...], b_ref[...], (((1,), (1,)), ((), ())), preferred_element_type=F32)
        if relu2_act is not None:
            acc = acc * (2.0 * jnp.sqrt(refs[2][...].astype(F32)))
        o_ref[...] = acc.astype(o_ref.dtype)

    in_specs = [pl.BlockSpec((tm, c), lambda j, i: (i, 0)), pl.BlockSpec((tn, c), lambda j, i: (j, 0))]
    args = [a, b]
    if relu2_act is not None:
        in_specs.append(pl.BlockSpec((tm, tn), lambda j, i: (i, j)))
        args.append(relu2_act)
    return pl.pallas_call(
        body, name=name, grid=(n // tn, m // tm), in_specs=in_specs,
        out_specs=pl.BlockSpec((tm, tn), lambda j, i: (i, j)), out_shape=_sds((m, n), out_dtype),
        compiler_params=_cp("parallel", "parallel"))(*args)


def _mm_nt_norm_bwd(a, b, x, g, res, *, name):
    m, c = a.shape
    d = b.shape[0]
    tm = _tile(m, (512, 256, 128))

    def body(a_ref, b_ref, x_ref, g_ref, res_ref, dx_ref, dxb_ref, dg_ref):
        dy = lax.dot_general(a_ref[...], b_ref[...], (((1,), (1,)), ((), ())), preferred_element_type=F32)
        dx, dg = _rms_bwd_rows(x_ref[...], g_ref[...], dy)
        dx = res_ref[...] + dx
        dx_ref[...] = dx
        dxb_ref[...] = dx.astype(BF16)
        _accum(dg_ref, dg, pl.program_id(0) == 0)

    row = pl.BlockSpec((tm, d), lambda i: (i, 0))
    fix = pl.BlockSpec((1, d), lambda i: (0, 0))
    return pl.pallas_call(
        body, name=name, grid=(m // tm,),
        in_specs=[pl.BlockSpec((tm, c), lambda i: (i, 0)), pl.BlockSpec((d, c), lambda i: (0, 0)), row, fix, row],
        out_specs=(row, row, fix), out_shape=(_sds((m, d), F32), _sds((m, d), BF16), _sds((1, d), F32)),
        compiler_params=_cp("arbitrary"))(a, b, x, g.reshape(1, d), res)


def _mm_tn(a, b, *, name, packed=None):
    m, ka = a.shape
    nb = b.shape[1]
    tka = _tile(ka, (512, 256, 128)) if packed is None else packed[1]
    tnb = _tile(nb, (1024, 768, 512)) if nb != PROJ_W else nb
    tc = _tile(m, (4096, 2048, 1024, 512, 256, 128) if tnb <= PACK_C else (2048, 1024, 512, 256, 128))

    def body(*refs):
        a_ref, b_ref, o_ref = refs[0], refs[1], refs[-1]
        part = lax.dot_general(a_ref[...], b_ref[...], (((0,), (0,)), ((), ())), preferred_element_type=F32)

        @pl.when(pl.program_id(2) == 0)
        def _():
            o_ref[...] = part

        @pl.when(pl.program_id(2) != 0)
        def _():
            o_ref[...] += part

    in_specs = [pl.BlockSpec((tc, tka), lambda i, j, c: (c, i)), pl.BlockSpec((tc, tnb), lambda i, j, c: (c, j))]
    kwargs = dict(out_specs=pl.BlockSpec((tka, tnb), lambda i, j, c: (i, j)), out_shape=_sds((ka, nb), F32))
    args = [a, b]
    if packed is not None:
        buf, _, place = packed
        assert tnb == PACK_C
        kwargs = dict(out_specs=pl.BlockSpec((None, tka, tnb), lambda i, j, c: place(i, j) + (0,)))
        if isinstance(buf, jax.ShapeDtypeStruct):
            kwargs["out_shape"] = buf
        else:
            kwargs.update(out_shape=_sds(buf.shape, buf.dtype), input_output_aliases={2: 0})
            in_specs.append(pl.BlockSpec(memory_space=pl.ANY))
            args.append(buf)
    return pl.pallas_call(
        body, name=name, grid=(ka // tka, nb // tnb, m // tc), in_specs=in_specs,
        compiler_params=_cp("parallel", "parallel", "arbitrary"), **kwargs)(*args)


def _rstd(x):
    return lax.rsqrt(jnp.mean(x * x, axis=-1, keepdims=True) + NORM_EPS)


def _rms_bwd_rows(x, g, dy):
    r = _rstd(x)
    gy = dy * g
    c = jnp.sum(x * gy, axis=-1, keepdims=True) * (r * r * r) * (1.0 / x.shape[-1])
    return r * gy - x * c, jnp.sum(dy * x * r, axis=0, keepdims=True)


def _accum(ref, part, first):
    @pl.when(first)
    def _():
        ref[...] = part

    @pl.when(jnp.logical_not(first))
    def _():
        ref[...] += part


def _rope(x, c, s1, s2, sh):
    return x * c + pltpu.roll(x, LANE - sh, 1) * s1 + pltpu.roll(x, sh, 1) * s2


def _rope_t(g, c, s1, s2, sh):
    return g * c + pltpu.roll(g * s1, sh, 1) + pltpu.roll(g * s2, LANE - sh, 1)


def _rope_tables(s, half, reps, lead=0):
    pos = jnp.arange(s, dtype=F32)
    inv_freq = ROPE_THETA ** (-jnp.arange(half, dtype=F32) / half)
    ang = pos[:, None] * inv_freq[None, :]
    cos, sin = jnp.cos(ang), jnp.sin(ang)
    zero = jnp.zeros_like(cos)
    ones, lead0 = jnp.ones((s, lead), F32), jnp.zeros((s, lead), F32)
    pad = jnp.zeros((s, LANE - lead - 2 * half * reps), F32)
    c = jnp.concatenate([ones] + [cos, cos] * reps + [pad], axis=1)
    s1 = jnp.concatenate([lead0] + [-sin, zero] * reps + [pad], axis=1)
    s2 = jnp.concatenate([lead0] + [zero, sin] * reps + [pad], axis=1)
    return c, s1, s2


def _lane_lt64(shape):
    return lax.broadcasted_iota(jnp.int32, shape, len(shape) - 1) % LANE < HEAD_DIM


def _group_sum(x):
    outs = []
    for b in range(x.shape[1] // LANE):
        blk = x[:, b * LANE:(b + 1) * LANE]
        lo = _lane_lt64(blk.shape)
        s0 = jnp.sum(jnp.where(lo, blk, 0.0), axis=1, keepdims=True)
        s1 = jnp.sum(jnp.where(lo, 0.0, blk), axis=1, keepdims=True)
        outs.append(jnp.where(lo, s0, s1))
    return outs


def _row_spec(ts, w):
    return pl.BlockSpec((ts, w), lambda i: (i, 0))


def _fix_spec(w):
    return pl.BlockSpec((1, w), lambda i: (0, 0))


def _loss_head(x, g, target):
    s, d = x.shape
    ts = _tile(s, (512, 256, 128))

    def body(x_ref, g_ref, t_ref, loss_ref, dx_ref, dxb_ref, dg_ref):
        xv, gv = x_ref[...], g_ref[...]
        err = xv * _rstd(xv) * gv - t_ref[...]
        part = 0.5 * jnp.sum(jnp.sum(err * err, axis=-1, keepdims=True) * (1.0 / d), axis=0, keepdims=True)
        dx, dg = _rms_bwd_rows(xv, gv, err * (1.0 / d))
        dx_ref[...] = dx
        dxb_ref[...] = dx.astype(BF16)
        first = pl.program_id(0) == 0
        _accum(dg_ref, dg, first)
        _accum(loss_ref, jnp.broadcast_to(part, (1, LANE)), first)

    return pl.pallas_call(
        body, name="loss_head", grid=(s // ts,), in_specs=[_row_spec(ts, d), _fix_spec(d), _row_spec(ts, d)],
        out_specs=(_fix_spec(LANE), _row_spec(ts, d), _row_spec(ts, d), _fix_spec(d)),
        out_shape=(_sds((1, LANE), F32), _sds((s, d), F32), _sds((s, d), BF16), _sds((1, d), F32)),
        compiler_params=_cp("arbitrary"))(x, g.reshape(1, d), target)


def _prep_fwd(proj, q_norm, kv_norm, t32, t64):
    s = proj.shape[0]
    ts = _tile(s, (256, 128))

    def body(p_ref, qn_ref, kn_ref, c32, a32, b32, c64, a64, b64, cqn_ref, ckvn_ref, kpe_ref, qkvb_ref, qkvc_ref):
        cq = p_ref[:, 0:Q_LORA]
        cqn_ref[...] = (cq * _rstd(cq) * qn_ref[...]).astype(BF16)
        ckv = p_ref[:, COL_CKV:COL_KPE]
        ckvn_ref[...] = (ckv * _rstd(ckv) * kn_ref[...]).astype(BF16)
        kp = pltpu.roll(p_ref[:, COL_KPE:COL_B], HEAD_DIM, 1)
        kpe_ref[...] = _rope(kp, c32[...], a32[...], b32[...], QK_ROPE // 2).astype(BF16)
        for b in range(6):
            blk = _rope(p_ref[:, COL_B + b * LANE:COL_B + (b + 1) * LANE], c64[...], a64[...], b64[...], HEAD_DIM // 2)
            if b < 3:
                blk = blk * SCALE_BC
            qkvb_ref[:, b * LANE:(b + 1) * LANE] = blk.astype(BF16)
        qkvb_ref[:, 2 * WIDTH_AB:3 * WIDTH_AB] = p_ref[:, COL_B + 2 * WIDTH_AB:COL_C].astype(BF16)
        qkvc_ref[:, 0:WIDTH_C] = (p_ref[:, COL_C:COL_C + WIDTH_C] * SCALE_BC).astype(BF16)
        qkvc_ref[:, WIDTH_C:3 * WIDTH_C] = p_ref[:, COL_C + WIDTH_C:PROJ_W].astype(BF16)

    tab = [_row_spec(ts, LANE)] * 6
    return pl.pallas_call(
        body, name="prep_fwd", grid=(s // ts,),
        in_specs=[_row_spec(ts, PROJ_W), _fix_spec(Q_LORA), _fix_spec(KV_LORA)] + tab,
        out_specs=(_row_spec(ts, Q_LORA), _row_spec(ts, KV_LORA), _row_spec(ts, LANE), _row_spec(ts, 3 * WIDTH_AB),
                   _row_spec(ts, 3 * WIDTH_C)),
        out_shape=(_sds((s, Q_LORA), BF16), _sds((s, KV_LORA), BF16), _sds((s, LANE), BF16),
                   _sds((s, 3 * WIDTH_AB), BF16), _sds((s, 3 * WIDTH_C), BF16)),
        compiler_params=_cp("parallel"))(proj, q_norm.reshape(1, -1), kv_norm.reshape(1, -1), *t32, *t64)


def _prep_bwd(proj, q_norm, kv_norm, t32, t64, dcqn, dckvn, dkpe, db, dc):
    s = proj.shape[0]
    ts = _tile(s, (256, 128))

    def body(p_ref, qn_ref, kn_ref, c32, a32, b32, c64, a64, b64, dcqn_ref, dckvn_ref, dkpe_ref, *rest):
        db_refs, dc_refs = rest[0:9], rest[9:12]
        dp_ref, dqn_ref, dkn_ref = rest[12:15]
        first = pl.program_id(0) == 0
        dx, dg = _rms_bwd_rows(p_ref[:, 0:Q_LORA], qn_ref[...], dcqn_ref[...])
        dp_ref[:, 0:Q_LORA] = dx.astype(BF16)
        _accum(dqn_ref, dg, first)
        dx, dg = _rms_bwd_rows(p_ref[:, COL_CKV:COL_KPE], kn_ref[...], dckvn_ref[...])
        dp_ref[:, COL_CKV:COL_KPE] = dx.astype(BF16)
        _accum(dkn_ref, dg, first)
        g = pltpu.roll(_rope_t(dkpe_ref[...], c32[...], a32[...], b32[...], QK_ROPE // 2), LANE - HEAD_DIM, 1)
        lane = lax.broadcasted_iota(jnp.int32, g.shape, 1)
        dp_ref[:, COL_KPE:COL_B] = jnp.where(lane < QK_ROPE, g, 0.0).astype(BF16)
        for which in range(3):
            for b in range(3):
                sl = slice(b * LANE, (b + 1) * LANE)
                g = db_refs[which][:, sl] + db_refs[3 + which][:, sl] + db_refs[6 + which][:, sl]
                if which < 2:
                    g = _rope_t(g, c64[...], a64[...], b64[...], HEAD_DIM // 2)
                if which == 0:
                    g = g * SCALE_BC
                col = COL_B + which * WIDTH_AB + b * LANE
                dp_ref[:, col:col + LANE] = g.astype(BF16)
        dp_ref[:, COL_C:COL_C + WIDTH_C] = (dc_refs[0][...] * SCALE_BC).astype(BF16)
        dp_ref[:, COL_C + WIDTH_C:COL_C + 2 * WIDTH_C] = dc_refs[1][...].astype(BF16)
        dp_ref[:, COL_C + 2 * WIDTH_C:PROJ_W] = dc_refs[2][...].astype(BF16)

    tab = [_row_spec(ts, LANE)] * 6
    in_specs = ([_row_spec(ts, PROJ_W), _fix_spec(Q_LORA), _fix_spec(KV_LORA)] + tab
                + [_row_spec(ts, Q_LORA), _row_spec(ts, KV_LORA), _row_spec(ts, LANE)]
                + [_row_spec(ts, WIDTH_AB)] * 9 + [_row_spec(ts, WIDTH_C)] * 3)
    return pl.pallas_call(
        body, name="prep_bwd", grid=(s // ts,), in_specs=in_specs,
        out_specs=(_row_spec(ts, PROJ_W), _fix_spec(Q_LORA), _fix_spec(KV_LORA)),
        out_shape=(_sds((s, PROJ_W), BF16), _sds((1, Q_LORA), F32), _sds((1, KV_LORA), F32)),
        compiler_params=_cp("arbitrary"))(proj, q_norm.reshape(1, -1), kv_norm.reshape(1, -1), *t32, *t64,
                                          dcqn, dckvn, dkpe, *db, *dc)


def _a_post_fwd(cqn, ckvn, w_uq, w_ukv, kpe, t32):
    s = cqn.shape[0]
    ts = _tile(s, (512, 256, 128))

    def body(cq_ref, ckv_ref, wq_ref, wkv_ref, kpe_ref, c32, a32, b32, q_ref, k_ref, kt_ref, v_ref):
        qa = jnp.dot(cq_ref[...], wq_ref[...], preferred_element_type=F32)
        kva = jnp.dot(ckv_ref[...], wkv_ref[...], preferred_element_type=F32)
        kpe = kpe_ref[...].astype(F32)
        for h in range(HEADS_A):
            hb = slice(h * LANE, (h + 1) * LANE)
            q_ref[:, hb] = _rope(qa[:, hb], c32[...], a32[...], b32[...], QK_ROPE // 2).astype(BF16)
            kh = kva[:, hb] + kpe
            k_ref[:, hb] = kh.astype(BF16)
            kt_ref[hb, :] = kh.T.astype(BF16)
        for p in range(3):
            lo, hi = 2 * p * LANE, (2 * p + 1) * LANE
            v_ref[:, lo:hi] = kva[:, W_A2 + p * LANE:W_A2 + (p + 1) * LANE].astype(BF16)
            v_ref[:, hi:hi + LANE] = jnp.ones((ts, LANE), BF16)

    whole = lambda shape: pl.BlockSpec(shape, lambda i: (0, 0))
    return pl.pallas_call(
        body, name="a_post_fwd", grid=(s // ts,),
        in_specs=[_row_spec(ts, Q_LORA), _row_spec(ts, KV_LORA), whole(w_uq.shape), whole(w_ukv.shape), _row_spec(ts, LANE)]
        + [_row_spec(ts, LANE)] * 3,
        out_specs=(_row_spec(ts, W_A2), _row_spec(ts, W_A2), pl.BlockSpec((W_A2, ts), lambda i: (0, i)), _row_spec(ts, W_A2)),
        out_shape=(_sds((s, W_A2), BF16), _sds((s, W_A2), BF16), _sds((W_A2, s), BF16), _sds((s, W_A2), BF16)),
        compiler_params=_cp("parallel"))(cqn, ckvn, w_uq, w_ukv, kpe, *t32)


def _a_post_bwd(dqa2_t, dka2, dva, t32):
    s = dka2.shape[0]
    ts = _tile(s, (512, 256, 128))

    def body(dqt_ref, dk_ref, dv_ref, c32, a32, b32, dqa_ref, dkva_ref, dkpe_ref):
        acc = None
        for h in range(HEADS_A):
            hb = slice(h * LANE, (h + 1) * LANE)
            dqa_ref[:, hb] = _rope_t(dqt_ref[hb, :].T * SCALE_A, c32[...], a32[...], b32[...], QK_ROPE // 2).astype(BF16)
            part = dk_ref[:, hb] * SCALE_A
            dkva_ref[:, hb] = part.astype(BF16)
            acc = part if acc is None else acc + part
        dkva_ref[:, W_A2:W_KV] = dv_ref[...].astype(BF16)
        dkpe_ref[...] = acc

    return pl.pallas_call(
        body, name="a_post_bwd", grid=(s // ts,),
        in_specs=[pl.BlockSpec((W_A2, ts), lambda i: (0, i)), _row_spec(ts, W_A2), _row_spec(ts, WIDTH_AB)]
        + [_row_spec(ts, LANE)] * 3,
        out_specs=(_row_spec(ts, W_A2), _row_spec(ts, W_KV), _row_spec(ts, LANE)),
        out_shape=(_sds((s, W_A2), BF16), _sds((s, W_KV), BF16), _sds((s, LANE), F32)),
        compiler_params=_cp("parallel"))(dqa2_t, dka2, dva, *t32)


def _pair_masks():
    lane = lax.broadcasted_iota(jnp.int32, (1, LANE), 1)
    return lane < HEAD_DIM, lane >= HEAD_DIM


def _nt(a, b):
    return lax.dot_general(a, b, (((1,), (1,)), ((), ())), preferred_element_type=F32)


def _tn(a, b):
    return lax.dot_general(a, b, (((0,), (0,)), ((), ())), preferred_element_type=F32)


def _stack_heads(x):
    m0, m1 = _pair_masks()
    zero = jnp.zeros_like(x)
    return jnp.concatenate([jnp.where(m0, x, zero), jnp.where(m1, x, zero)], axis=0)


def _stack_stat(x):
    return jnp.concatenate([x[:, 0:1], x[:, HEAD_DIM:HEAD_DIM + 1]], axis=0)


def _softmax_pair(q, kk, vv, bias2):
    t = q.shape[0]
    s = _nt(_stack_heads(q), kk) + bias2
    m = jnp.max(s, axis=1, keepdims=True)
    p = jnp.exp(s - m)
    l = jnp.sum(p, axis=1, keepdims=True)
    o2 = jnp.dot(p.astype(BF16), vv, preferred_element_type=F32) / l
    lse2 = m + jnp.log(l)
    lo = _lane_lt64((t, LANE))
    return jnp.where(lo, o2[:t], o2[t:]), jnp.where(lo, lse2[:t], lse2[t:])


def _softmax_pair_bwd(q, kk, vv, do, lse, delta, bias2):
    t = q.shape[0]
    q2, do2 = _stack_heads(q), _stack_heads(do)
    p = jnp.exp(_nt(q2, kk) + bias2 - _stack_stat(lse))
    ds = p * (_nt(do2, vv) - _stack_stat(delta))
    dsb = ds.astype(BF16)
    dq2 = jnp.dot(dsb, kk, preferred_element_type=F32)
    lo = _lane_lt64((t, LANE))
    return jnp.where(lo, dq2[:t], dq2[t:]), _tn(dsb, q2), _tn(p.astype(BF16), do2), ds


DENSE_FWD_TQ, DENSE_FWD_TK = 512, 8192
DENSE_BWD_TQ, DENSE_BWD_TK = 2048, 1024
LOG2E = math.log2(math.e)


def _dense_fwd(qa, ka, va1):
    s = qa.shape[0]
    tq, tk = min(DENSE_FWD_TQ, s), min(DENSE_FWD_TK, s)
    nk = s // tk
    c = SCALE_A * LOG2E

    def body(q_ref, k_ref, v_ref, o_ref, lse_ref, m_sc, acc_sc):
        j = pl.program_id(2)

        @pl.when(j == 0)
        def _():
            m_sc[...] = jnp.full(m_sc.shape, NEG_INF, F32)
            acc_sc[...] = jnp.zeros(acc_sc.shape, F32)

        vv = v_ref[...]
        for hh in range(2):
            hs = slice(hh * LANE, (hh + 1) * LANE)
            sc = _nt(q_ref[:, hs], k_ref[:, hs])
            m_prev = m_sc[hh]
            m_new = jnp.maximum(m_prev, jnp.max(sc, axis=1, keepdims=True))
            alpha = jnp.exp2((m_prev - m_new) * c)
            p = jnp.exp2((sc - m_new) * c)
            acc_sc[hh] = alpha * acc_sc[hh] + jnp.dot(p.astype(BF16), vv, preferred_element_type=F32)
            m_sc[hh] = m_new

        @pl.when(j == nk - 1)
        def _():
            lo = _lane_lt64((tq, LANE))
            a0, a1 = acc_sc[0], acc_sc[1]
            l0, l1 = a0[:, LANE:], a1[:, LANE:]
            o_ref[...] = jnp.where(lo, a0[:, :LANE] / l0, a1[:, :LANE] / l1)
            lse_ref[0] = _stat_rows(jnp.where(lo, m_sc[0] * SCALE_A + jnp.log(l0), m_sc[1] * SCALE_A + jnp.log(l1)))

    return pl.pallas_call(
        body, name="dense_fwd", grid=(3, s // tq, nk),
        in_specs=[pl.BlockSpec((tq, 2 * LANE), lambda p, i, j: (i, p)), pl.BlockSpec((tk, 2 * LANE), lambda p, i, j: (j, p)),
                  pl.BlockSpec((tk, 2 * LANE), lambda p, i, j: (j, p))],
        out_specs=(pl.BlockSpec((tq, LANE), lambda p, i, j: (i, p)), pl.BlockSpec((1, 8, tq), lambda p, i, j: (p, 0, i))),
        out_shape=(_sds((s, WIDTH_AB), F32), _sds((3, 8, s), F32)),
        scratch_shapes=[pltpu.VMEM((2, tq, 1), F32), pltpu.VMEM((2, tq, 2 * LANE), F32)],
        compiler_params=_cp("parallel", "parallel", "arbitrary"))(qa, ka, va1)


def _stat_rows(lane_dense):
    tr = lane_dense.T
    return jnp.concatenate([tr[0:1, :], tr[HEAD_DIM:HEAD_DIM + 1, :], jnp.zeros((6, tr.shape[1]), F32)], axis=0)


def _dense_bwd(qa, ka, kat, va1, do, lse_rows, delta_rows):
    s = qa.shape[0]
    tq, tk = min(DENSE_BWD_TQ, s), min(DENSE_BWD_TK, s)
    c = SCALE_A * LOG2E

    def body(q_ref, k_ref, kt_ref, v_ref, do_ref, lse_ref, dl_ref, dqt_ref, dk_ref, dv_ref):
        j, i = pl.program_id(1), pl.program_id(2)

        @pl.when((j == 0) & (i == 0))
        def _():
            dqt_ref[...] = jnp.zeros(dqt_ref.shape, F32)

        vv, do_ = v_ref[...], do_ref[...]
        lse_t, dl_t = lse_ref[0] * LOG2E, dl_ref[0]
        vm = _pair_masks()
        cols = pl.ds(pl.multiple_of(i * tq, tq), tq)
        dv = None
        for hh in range(2):
            hs = slice(hh * LANE, (hh + 1) * LANE)
            qh = q_ref[:, hs]
            dom = jnp.where(vm[hh], do_, jnp.zeros_like(do_))
            pt = jnp.exp2(_nt(k_ref[:, hs], qh) * c - lse_t[hh:hh + 1, :])
            dst = pt * (_nt(vv, dom) - dl_t[hh:hh + 1, :])
            pb, dsb = pt.astype(BF16), dst.astype(BF16)
            dv_h = jnp.dot(pb, dom, preferred_element_type=F32)
            dv = dv_h if dv is None else dv + dv_h
            dqt_ref[hs, cols] += jnp.dot(kt_ref[hs, :], dsb, preferred_element_type=F32)
            dk_h = jnp.dot(dsb, qh, preferred_element_type=F32)

            @pl.when(i == 0)
            def _():
                dk_ref[:, hs] = dk_h

            @pl.when(i != 0)
            def _():
                dk_ref[:, hs] += dk_h
        _accum(dv_ref, dv, i == 0)

    st_spec = pl.BlockSpec((1, 8, tq), lambda p, j, i: (p, 0, i))
    return pl.pallas_call(
        body, name="dense_bwd", grid=(3, s // tk, s // tq),
        in_specs=[pl.BlockSpec((tq, 2 * LANE), lambda p, j, i: (i, p)), pl.BlockSpec((tk, 2 * LANE), lambda p, j, i: (j, p)),
                  pl.BlockSpec((2 * LANE, tk), lambda p, j, i: (p, j)), pl.BlockSpec((tk, LANE), lambda p, j, i: (j, 2 * p)),
                  pl.BlockSpec((tq, LANE), lambda p, j, i: (i, p)), st_spec, st_spec],
        out_specs=(pl.BlockSpec((2 * LANE, s), lambda p, j, i: (p, 0)), pl.BlockSpec((tk, 2 * LANE), lambda p, j, i: (j, p)),
                   pl.BlockSpec((tk, LANE), lambda p, j, i: (j, p))),
        out_shape=(_sds((W_A2, s), F32), _sds((s, W_A2), F32), _sds((s, WIDTH_AB), F32)),
        compiler_params=_cp("parallel", "arbitrary", "arbitrary"))(qa, ka, kat, va1, do, lse_rows, delta_rows)


BAND_TILE = 1024
BAND_SUB = 128
QKV_W = 3 * WIDTH_AB


def _band_bias_table():
    row = np.arange(BAND_SUB)[:, None]
    col = np.arange(2 * BAND_SUB)[None, :]
    band = np.abs(row - col + BAND_HALF) <= BAND_HALF
    variants = []
    for idx in range(4):
        ok = band & ((col >= BAND_HALF) | ((idx & 1) == 0)) & ((col < 2 * BAND_SUB - BAND_HALF) | ((idx & 2) == 0))
        one = np.where(ok, 0.0, NEG_INF).astype(np.float32)
        variants.append(np.concatenate([one, one], axis=0))
    return jnp.asarray(np.stack(variants))


def _band_specs(t, n):
    hpt = t // BAND_HALF
    last = n // BAND_HALF - 1
    return [pl.BlockSpec((BAND_HALF, QKV_W), lambda r, i: (jnp.maximum(i * hpt - 1, 0), r)),
            pl.BlockSpec((t, QKV_W), lambda r, i: (i, r)),
            pl.BlockSpec((BAND_HALF, QKV_W), lambda r, i: (jnp.minimum((i + 1) * hpt, last), r)),
            pl.BlockSpec((4, 2 * BAND_SUB, 2 * BAND_SUB), lambda r, i: (0, 0, 0))]


def _band_bias(b_ref, a, nsub, i, nt):
    idx = 0
    if a == 0:
        idx = idx + (i == 0).astype(jnp.int32)
    if a == nsub - 1:
        idx = idx + 2 * (i == nt - 1).astype(jnp.int32)
    return b_ref[idx]


def _band_kv(left, main, right, p):
    kc = slice(WIDTH_AB + p * LANE, WIDTH_AB + (p + 1) * LANE)
    vc = slice(2 * WIDTH_AB + p * LANE, 2 * WIDTH_AB + (p + 1) * LANE)
    return (jnp.concatenate([left[:, kc], main[:, kc], right[:, kc]], axis=0),
            jnp.concatenate([left[:, vc], main[:, vc], right[:, vc]], axis=0))


def _banded_fwd(qkvb, dil, bias):
    s = qkvb.shape[0]
    n = s // dil
    t = min(n, BAND_TILE)
    nsub, nt = t // BAND_SUB, n // t
    view = qkvb.reshape(n, dil * QKV_W)

    def body(left, main, right, b_ref, o_ref, lse_ref):
        i = pl.program_id(1)
        for p in range(3):
            pc = slice(p * LANE, (p + 1) * LANE)
            kk, vv = _band_kv(left, main, right, p)
            for a in range(nsub):
                rows, win = slice(a * BAND_SUB, (a + 1) * BAND_SUB), slice(a * BAND_SUB, (a + 2) * BAND_SUB)
                o, lse = _softmax_pair(main[rows, pc], kk[win], vv[win], _band_bias(b_ref, a, nsub, i, nt))
                o_ref[rows, pc] = o
                lse_ref[rows, pc] = lse

    o_spec = pl.BlockSpec((t, WIDTH_AB), lambda r, i: (i, r))
    o, lse = pl.pallas_call(
        body, name=f"banded_fwd_d{dil}", grid=(dil, nt), in_specs=_band_specs(t, n), out_specs=(o_spec, o_spec),
        out_shape=(_sds((n, dil * WIDTH_AB), F32), _sds((n, dil * WIDTH_AB), F32)),
        compiler_params=_cp("parallel", "parallel"))(view, view, view, bias)
    return o.reshape(s, WIDTH_AB), lse.reshape(s, WIDTH_AB)


def _banded_bwd(qkvb, do, lse, delta, dil, bias):
    s = qkvb.shape[0]
    n = s // dil
    t = min(n, BAND_TILE)
    nsub, nt = t // BAND_SUB, n // t
    view = qkvb.reshape(n, dil * QKV_W)
    side = [a.reshape(n, dil * WIDTH_AB) for a in (do, lse, delta)]

    def body(left, main, right, b_ref, do_ref, lse_ref, dl_ref, dq_ref, dk_ref, dv_ref):
        i = pl.program_id(1)

        @pl.when(i == 0)
        def _():
            dk_ref[...] = jnp.zeros(dk_ref.shape, F32)
            dv_ref[...] = jnp.zeros(dv_ref.shape, F32)

        lrow = pl.multiple_of(jnp.maximum(i * t - BAND_HALF, 0), BAND_HALF)
        rrow = pl.multiple_of(jnp.minimum((i + 1) * t, n - BAND_HALF), BAND_HALF)
        mrow = pl.multiple_of(i * t, BAND_HALF)
        for p in range(3):
            pc = slice(p * LANE, (p + 1) * LANE)
            kk, vv = _band_kv(left, main, right, p)
            parts = []
            for a in range(nsub):
                rows, win = slice(a * BAND_SUB, (a + 1) * BAND_SUB), slice(a * BAND_SUB, (a + 2) * BAND_SUB)
                dq, dk, dv, _ = _softmax_pair_bwd(main[rows, pc], kk[win], vv[win], do_ref[rows, pc], lse_ref[rows, pc],
                                                  dl_ref[rows, pc], _band_bias(b_ref, a, nsub, i, nt))
                dq_ref[rows, pc] = dq
                parts.append((dk, dv))
            for which, ref in ((0, dk_ref), (1, dv_ref)):
                chunks = []
                for c in range(nsub + 1):
                    g = parts[c][which][:BAND_SUB] if c < nsub else None
                    if c >= 1:
                        h = parts[c - 1][which][BAND_SUB:]
                        g = h if g is None else g + h
                    chunks.append(g)
                mid = jnp.concatenate([chunks[0][BAND_HALF:]] + chunks[1:nsub] + [chunks[nsub][:BAND_HALF]], axis=0)
                ref[pl.ds(lrow, BAND_HALF), pc] += chunks[0][:BAND_HALF]
                ref[pl.ds(mrow, t), pc] += mid
                ref[pl.ds(rrow, BAND_HALF), pc] += chunks[nsub][BAND_HALF:]

    q_spec = pl.BlockSpec((t, WIDTH_AB), lambda r, i: (i, r))
    acc_spec = pl.BlockSpec((n, WIDTH_AB), lambda r, i: (0, r))
    shp = _sds((n, dil * WIDTH_AB), F32)
    outs = pl.pallas_call(
        body, name=f"banded_bwd_d{dil}", grid=(dil, nt), in_specs=_band_specs(t, n) + [q_spec, q_spec, q_spec],
        out_specs=(q_spec, acc_spec, acc_spec), out_shape=(shp, shp, shp),
        compiler_params=_cp("parallel", "arbitrary"))(view, view, view, bias, *side)
    return [a.reshape(s, WIDTH_AB) for a in outs]


def _na_geometry(s):
    rows = s // GRID_W
    assert rows >= 2 * NA_ROWS and rows % NA_ROWS == 0
    return rows, rows // NA_ROWS


def _na_row(n, i, rows):
    rq = n * NA_ROWS + i
    rs = jnp.clip(rq - NA_ROWS // 2, 0, rows - NA_ROWS)
    return pl.multiple_of(rs * GRID_W, GRID_W), rs - rq + NA_ROWS - 1


NA_KEYS = NA_ROWS * GRID_W


def _natten_fwd(qkvc, tfull):
    s = qkvc.shape[0]
    rows, nrb = _na_geometry(s)
    tq = NA_ROWS * GRID_W

    def body(q_ref, k_ref, v_ref, t_ref, o_ref, lse_ref):
        n = pl.program_id(1)
        for i in range(NA_ROWS):
            tok, base = _na_row(n, i, rows)
            kk, vv = k_ref[pl.ds(tok, NA_KEYS), :], v_ref[pl.ds(tok, NA_KEYS), :]
            sl = slice(i * GRID_W, (i + 1) * GRID_W)
            bias2 = jnp.concatenate([t_ref[0, base], t_ref[1, base]], axis=0)
            o, lse = _softmax_pair(q_ref[sl, :], kk, vv, bias2)
            o_ref[sl, :] = o
            lse_ref[sl, :] = lse

    o_spec = pl.BlockSpec((tq, LANE), lambda p, n: (n, p))
    return pl.pallas_call(
        body, name="natten_fwd", grid=(2, nrb),
        in_specs=[pl.BlockSpec((tq, LANE), lambda p, n: (n, p)), pl.BlockSpec((s, LANE), lambda p, n: (0, 2 + p)),
                  pl.BlockSpec((s, LANE), lambda p, n: (0, 4 + p)),
                  pl.BlockSpec((2, NA_ROWS, GRID_W, NA_KEYS), lambda p, n: (p, 0, 0, 0))],
        out_specs=(o_spec, o_spec), out_shape=(_sds((s, WIDTH_C), F32), _sds((s, WIDTH_C), F32)),
        compiler_params=_cp("parallel", "parallel"))(qkvc, qkvc, qkvc, tfull)


def _natten_bwd(qkvc, tfull, do, lse, delta):
    s = qkvc.shape[0]
    rows, nrb = _na_geometry(s)
    tq = NA_ROWS * GRID_W

    def body(q_ref, k_ref, v_ref, t_ref, do_ref, lse_ref, dl_ref, dq_ref, dk_ref, dv_ref, dt_ref):
        n = pl.program_id(1)

        @pl.when(n == 0)
        def _():
            dk_ref[...] = jnp.zeros(dk_ref.shape, F32)
            dv_ref[...] = jnp.zeros(dv_ref.shape, F32)
            dt_ref[...] = jnp.zeros(dt_ref.shape, F32)

        for i in range(NA_ROWS):
            tok, base = _na_row(n, i, rows)
            win = pl.ds(tok, NA_KEYS)
            sl = slice(i * GRID_W, (i + 1) * GRID_W)
            bias2 = jnp.concatenate([t_ref[0, base], t_ref[1, base]], axis=0)
            dq, dk, dv, ds = _softmax_pair_bwd(q_ref[sl, :], k_ref[win, :], v_ref[win, :], do_ref[sl, :], lse_ref[sl, :],
                                               dl_ref[sl, :], bias2)
            dq_ref[sl, :] = dq
            dk_ref[win, :] += dk
            dv_ref[win, :] += dv
            dt_ref[0, base] += ds[:GRID_W]
            dt_ref[1, base] += ds[GRID_W:]

    q_spec = pl.BlockSpec((tq, LANE), lambda p, n: (n, p))
    acc_spec = pl.BlockSpec((s, LANE), lambda p, n: (0, p))
    t_spec = pl.BlockSpec((2, NA_ROWS, GRID_W, NA_KEYS), lambda p, n: (p, 0, 0, 0))
    shp = _sds((s, WIDTH_C), F32)
    return pl.pallas_call(
        body, name="natten_bwd", grid=(2, nrb),
        in_specs=[q_spec, pl.BlockSpec((s, LANE), lambda p, n: (0, 2 + p)), pl.BlockSpec((s, LANE), lambda p, n: (0, 4 + p)),
                  t_spec, q_spec, q_spec, q_spec],
        out_specs=(q_spec, acc_spec, acc_spec, t_spec),
        out_shape=(shp, shp, shp, _sds((HEADS_C, NA_ROWS, GRID_W, NA_KEYS), F32)),
        compiler_params=_cp("parallel", "arbitrary"))(qkvc, qkvc, qkvc, tfull, do, lse, delta)


def _rpb_constants():
    p = np.arange(GRID_W)[:, None]
    qc = np.arange(GRID_W)[None, :]
    dc = np.clip(qc - p, -(NA_COLS - 1), NA_COLS - 1) + NA_COLS - 1
    onehot = (dc.reshape(1, -1) == np.arange(32)[:, None]).astype(np.float32)
    c_start = np.clip(p - NA_COLS // 2, 0, GRID_W - NA_COLS)
    col_ok = ((qc >= c_start) & (qc < c_start + NA_COLS)).reshape(1, -1).astype(np.float32)
    a = np.arange(16)[:, None]
    bj = np.arange(64)[None, :]
    row_sel = ((bj // 8 + bj % 8) == a).astype(np.float32)
    return jnp.asarray(onehot), jnp.asarray(col_ok), jnp.asarray(row_sel)


def _rpb_expand(rpb, onehot, col_ok):
    r2 = jnp.pad(rpb.reshape(HEADS_C * 15, 31), ((0, 4), (0, 1)))

    def body(r_ref, oh_ref, ok_ref, o_ref):
        t = jnp.dot(r_ref[...], oh_ref[...], preferred_element_type=F32, precision=lax.Precision.HIGHEST)
        o_ref[...] = jnp.where(ok_ref[...] > 0.5, t, NEG_INF)

    tm = pl.pallas_call(body, name="rpb_expand", out_shape=_sds((64, GRID_W * GRID_W), F32))(r2, onehot, col_ok)
    tm = tm[:HEADS_C * 15].reshape(HEADS_C, 15, GRID_W, GRID_W)
    tfull = jnp.stack([jnp.concatenate([tm[:, base + j] for j in range(NA_ROWS)], axis=-1) for base in range(NA_ROWS)], axis=1)
    return tfull


def _rpb_grad(dtfull, onehot, row_sel):
    g = dtfull.reshape(HEADS_C, NA_ROWS, GRID_W, NA_ROWS, GRID_W).transpose(0, 1, 3, 2, 4).reshape(HEADS_C, 64, GRID_W * GRID_W)

    def body(g_ref, oh_ref, sel_ref, o_ref):
        for h in range(HEADS_C):
            mid = lax.dot_general(g_ref[h], oh_ref[...], (((1,), (1,)), ((), ())), preferred_element_type=F32,
                                  precision=lax.Precision.HIGHEST)
            o_ref[h] = jnp.dot(sel_ref[...], mid, preferred_element_type=F32, precision=lax.Precision.HIGHEST)

    out = pl.pallas_call(body, name="rpb_grad", out_shape=_sds((HEADS_C, 16, 32), F32))(g, onehot, row_sel)
    return out[:, :15, :31]


def _outnorm_fwd(o_a, branch_o, branch_lse, o_c, ga, gb, gc):
    s = o_a.shape[0]
    ts = _tile(s, (512, 256, 128))

    def body(a_ref, o1, o2, o3, l1, l2, l3, c_ref, ga_ref, gb_ref, gc_ref, ob_ref, lse_ref, o_ref):
        la, lb, lc = l1[...], l2[...], l3[...]
        m = jnp.maximum(jnp.maximum(la, lb), lc)
        ea, eb, ec = jnp.exp(la - m), jnp.exp(lb - m), jnp.exp(lc - m)
        den = ea + eb + ec
        o_b = (o1[...] * ea + o2[...] * eb + o3[...] * ec) / den
        ob_ref[...] = o_b
        lse_ref[...] = m + jnp.log(den)
        col = 0
        for x, g in ((a_ref[...], ga_ref), (o_b, gb_ref), (c_ref[...], gc_ref)):
            o_ref[:, col:col + x.shape[1]] = (x * _rstd(x) * g[...]).astype(BF16)
            col += x.shape[1]

    sp = _row_spec(ts, WIDTH_AB)
    return pl.pallas_call(
        body, name="outnorm_fwd", grid=(s // ts,),
        in_specs=[sp] * 7 + [_row_spec(ts, WIDTH_C), _fix_spec(WIDTH_AB), _fix_spec(WIDTH_AB), _fix_spec(WIDTH_C)],
        out_specs=(sp, sp, _row_spec(ts, D_MODEL)),
        out_shape=(_sds((s, WIDTH_AB), F32), _sds((s, WIDTH_AB), F32), _sds((s, D_MODEL), BF16)),
        compiler_params=_cp("parallel"))(o_a, *branch_o, *branch_lse, o_c, ga.reshape(1, -1), gb.reshape(1, -1),
                                         gc.reshape(1, -1))


def _outnorm_bwd(dxb, w_out, o_a, o_b, o_c, ga, gb, gc):
    s = o_a.shape[0]
    ts = _tile(s, (512, 256, 128))

    def body(dx_ref, w_ref, a_ref, b_ref, c_ref, ga_ref, gb_ref, gc_ref, *outs):
        first = pl.program_id(0) == 0
        dm = lax.dot_general(dx_ref[...], w_ref[...], (((1,), (1,)), ((), ())), preferred_element_type=F32)
        col = 0
        for k, (ref, g) in enumerate(((a_ref, ga_ref), (b_ref, gb_ref), (c_ref, gc_ref))):
            x = ref[...]
            w = x.shape[1]
            dx, dg = _rms_bwd_rows(x, g[...], dm[:, col:col + w])
            col += w
            outs[k][...] = dx.astype(BF16)
            for b, blk in enumerate(_group_sum(dx * x)):
                if k == 0:
                    outs[3][b] = _stat_rows(blk)
                else:
                    outs[3 + k][:, b * LANE:(b + 1) * LANE] = blk
            _accum(outs[6 + k], dg, first)

    widths = (WIDTH_AB, WIDTH_AB, WIDTH_C)
    return pl.pallas_call(
        body, name="outnorm_bwd", grid=(s // ts,),
        in_specs=[_row_spec(ts, D_MODEL), pl.BlockSpec(w_out.shape, lambda i: (0, 0))] + [_row_spec(ts, w) for w in widths]
        + [_fix_spec(w) for w in widths],
        out_specs=tuple([_row_spec(ts, w) for w in widths] + [pl.BlockSpec((3, 8, ts), lambda i: (0, 0, i))]
                        + [_row_spec(ts, w) for w in widths[1:]] + [_fix_spec(w) for w in widths]),
        out_shape=tuple([_sds((s, w), BF16) for w in widths] + [_sds((3, 8, s), F32)]
                        + [_sds((s, w), F32) for w in widths[1:]] + [_sds((1, w), F32) for w in widths]),
        compiler_params=_cp("arbitrary"))(dxb, w_out, o_a, o_b, o_c, ga.reshape(1, -1), gb.reshape(1, -1),
                                          gc.reshape(1, -1))


def _adamw(w, g, m, v, *, name):
    r, c = w.shape
    tr = _tile(r, (512, 256, 128, 64, 8))

    def body(w_ref, g_ref, m_ref, v_ref, d_ref, nm_ref, nv_ref):
        gv = g_ref[...]
        nm = ADAM_B1 * m_ref[...] + (1.0 - ADAM_B1) * gv
        nv = ADAM_B2 * v_ref[...] + (1.0 - ADAM_B2) * jnp.square(gv)
        m_hat = nm / (1.0 - ADAM_B1 ** ADAM_STEP)
        v_hat = nv / (1.0 - ADAM_B2 ** ADAM_STEP)
        d_ref[...] = -ADAM_LR * (m_hat / (jnp.sqrt(v_hat) + ADAM_EPS) + ADAM_WD * w_ref[...])
        nm_ref[...] = nm
        nv_ref[...] = nv

    sp = _row_spec(tr, c)
    return pl.pallas_call(
        body, name=name, grid=(r // tr,), in_specs=[sp] * 4, out_specs=(sp, sp, sp),
        out_shape=(_sds((r, c), F32),) * 3, compiler_params=_cp("parallel"))(w, g, m, v)


def _add_n(parts, *, name, out_dtype):
    r, c = parts[0].shape
    tr = max(t for t in range(16, 1025, 16) if r % t == 0)

    def body(*refs):
        acc = refs[0][...].astype(F32)
        for ref in refs[1:-1]:
            acc = acc + ref[...].astype(F32)
        refs[-1][...] = acc.astype(out_dtype)

    sp = _row_spec(tr, c)
    return pl.pallas_call(
        body, name=name, grid=(r // tr,), in_specs=[sp] * len(parts), out_specs=sp, out_shape=_sds((r, c), out_dtype),
        compiler_params=_cp("parallel"))(*parts)


ANY = pl.BlockSpec(memory_space=pl.ANY)
CHIP_FLIPS = ((1, 0), (0, 1), (1, 1))


def _me():
    return lax.axis_index("x"), lax.axis_index("y"), lax.axis_index("c")


def _gather_chips(half):
    def body(src, out, send_sems, recv_sems):
        x, y, c = _me()
        me, mine = (x, y, c), 2 * x + y
        chip_x, chip_y, chip_d = 2 * (1 - x) + y, 2 * x + (1 - y), 2 * (1 - x) + (1 - y)
        passed_chip = 2 * (x ^ (1 - c)) + (y ^ c)
        pass_to = (x ^ c, y ^ (1 - c), c)

        def copy(k, chip, half_idx, to, source=None):
            dst = out.at[chip, half_idx]
            return pltpu.make_async_remote_copy(src_ref=dst if source is None else source, dst_ref=dst,
                                                send_sem=send_sems.at[k], recv_sem=recv_sems.at[k], device_id=to,
                                                device_id_type=MESH_T)

        sends = [copy(0, mine, c, (1 - x, y, c), source=src), copy(1, mine, c, (x, 1 - y, c), source=src)]
        for cp in sends:
            cp.start()
        copy(0, chip_x, c, me).wait_recv()
        copy(1, chip_y, c, me).wait_recv()
        sends += [copy(2, passed_chip, c, pass_to), copy(3, chip_x, c, (x, y, 1 - c)), copy(4, chip_y, c, (x, y, 1 - c))]
        for cp in sends[2:]:
            cp.start()
        copy(2, chip_d, c, me).wait_recv()
        sends.append(copy(5, chip_d, c, (x, y, 1 - c)))
        sends[-1].start()
        for k, chip in ((3, chip_x), (4, chip_y), (5, chip_d)):
            copy(k, chip, 1 - c, me).wait_recv()
        for cp in sends:
            cp.wait_send()

    return pl.pallas_call(
        body, name="gather_chips", in_specs=[ANY], out_specs=ANY, out_shape=_sds((4, 2) + half.shape, half.dtype),
        scratch_shapes=[pltpu.SemaphoreType.DMA((6,)), pltpu.SemaphoreType.DMA((6,))])(half)


def _swap_sibling(block):
    def body(src, out, send_sem, recv_sem):
        x, y, c = _me()
        cp = pltpu.make_async_remote_copy(src_ref=src, dst_ref=out, send_sem=send_sem, recv_sem=recv_sem,
                                          device_id=(x, y, 1 - c), device_id_type=MESH_T)
        cp.start()
        cp.wait()

    return pl.pallas_call(
        body, name="swap_sibling", in_specs=[ANY], out_specs=ANY, out_shape=_sds(block.shape, block.dtype),
        scratch_shapes=[pltpu.SemaphoreType.DMA(()), pltpu.SemaphoreType.DMA(())])(block)


def _swap_other_half(halves):
    def body(src, out, send_sem, recv_sem):
        x, y, c = _me()
        cp = pltpu.make_async_remote_copy(src_ref=src.at[:, 1 - c], dst_ref=out, send_sem=send_sem, recv_sem=recv_sem,
                                          device_id=(x, y, 1 - c), device_id_type=MESH_T)
        cp.start()
        cp.wait()

    shape = (halves.shape[0],) + halves.shape[2:]
    return pl.pallas_call(
        body, name="swap_other_half", in_specs=[ANY], out_specs=ANY, out_shape=_sds(shape, halves.dtype),
        scratch_shapes=[pltpu.SemaphoreType.DMA(()), pltpu.SemaphoreType.DMA(())])(halves)


def _pair_sum(halves, core, other):
    n, _, h, c = halves.shape
    tr = max(t for t in range(16, 1025, 16) if h % t == 0)

    def body(core_ref, a_ref, b_ref, o_ref):
        o_ref[...] = (a_ref[...] + b_ref[...]).astype(BF16)

    grid_spec = pltpu.PrefetchScalarGridSpec(
        num_scalar_prefetch=1, grid=(n, h // tr),
        in_specs=[pl.BlockSpec((None, None, tr, c), lambda j, i, core_ref: (j, core_ref[0], i, 0)),
                  pl.BlockSpec((None, tr, c), lambda j, i, core_ref: (j, i, 0))],
        out_specs=pl.BlockSpec((None, tr, c), lambda j, i, core_ref: (j, i, 0)))
    return pl.pallas_call(
        body, name="pair_sum", grid_spec=grid_spec, out_shape=_sds((n, h, c), BF16),
        compiler_params=_cp("parallel", "parallel"))(core.reshape(1).astype(jnp.int32), halves, other)


def _scatter_chips(parts):
    def body(src, out, stage, send_sems, recv_sems):
        x, y, c = _me()
        me, mine = (x, y, c), 2 * x + y
        chip_x, chip_y, chip_d = 2 * (1 - x) + y, 2 * x + (1 - y), 2 * (1 - x) + (1 - y)
        via = (x ^ (1 - c), y ^ c, c)
        via_chip = 2 * (x ^ (1 - c)) + (y ^ c)
        pass_to = (x ^ c, y ^ (1 - c), c)

        def copy(k, source, dst, to):
            return pltpu.make_async_remote_copy(src_ref=source, dst_ref=dst, send_sem=send_sems.at[k],
                                                recv_sem=recv_sems.at[k], device_id=to, device_id_type=MESH_T)

        sends = [copy(0, src.at[chip_x], out.at[mine], (1 - x, y, c)), copy(1, src.at[chip_y], out.at[mine], (x, 1 - y, c)),
                 copy(2, src.at[chip_d], stage, via)]
        for cp in sends:
            cp.start()
        copy(2, stage, stage, me).wait_recv()
        sends.append(copy(3, stage, out.at[via_chip], pass_to))
        sends[-1].start()
        copy(0, stage, out.at[chip_x], me).wait_recv()
        copy(1, stage, out.at[chip_y], me).wait_recv()
        copy(3, stage, out.at[chip_d], me).wait_recv()
        for cp in sends:
            cp.wait_send()

    out, _ = pl.pallas_call(
        body, name="scatter_chips", in_specs=[ANY], out_specs=(ANY, ANY),
        out_shape=(_sds(parts.shape, parts.dtype), _sds(parts.shape[1:], parts.dtype)),
        scratch_shapes=[pltpu.SemaphoreType.DMA((4,)), pltpu.SemaphoreType.DMA((4,))])(parts)
    return out


def _all_reduce_small(block):
    r, c = block.shape

    def body(src, out, slots, send_sems, recv_sems):
        x, y, cc = _me()
        mine = 4 * x + 2 * y + cc
        slots[mine] = src[...]
        sends = []
        for k in range(1, 8):
            fx, fy, fc = (k >> 2) & 1, (k >> 1) & 1, k & 1
            cp = pltpu.make_async_remote_copy(src_ref=src, dst_ref=slots.at[mine], send_sem=send_sems.at[k - 1],
                                              recv_sem=recv_sems.at[k - 1], device_id=(x ^ fx, y ^ fy, cc ^ fc),
                                              device_id_type=MESH_T)
            cp.start()
            sends.append(cp)
        for k in range(1, 8):
            fx, fy, fc = (k >> 2) & 1, (k >> 1) & 1, k & 1
            theirs = 4 * (x ^ fx) + 2 * (y ^ fy) + (cc ^ fc)
            pltpu.make_async_remote_copy(src_ref=src, dst_ref=slots.at[theirs], send_sem=send_sems.at[k - 1],
                                         recv_sem=recv_sems.at[k - 1], device_id=(x ^ fx, y ^ fy, cc ^ fc),
                                         device_id_type=MESH_T).wait_recv()
        for cp in sends:
            cp.wait_send()
        acc = slots[0]
        for d in range(1, 8):
            acc = acc + slots[d]
        out[...] = acc

    vm = pl.BlockSpec(memory_space=pltpu.VMEM)
    return pl.pallas_call(
        body, name="all_reduce_small", in_specs=[vm], out_specs=vm, out_shape=_sds((r, c), F32),
        scratch_shapes=[pltpu.VMEM((8, r, c), F32), pltpu.SemaphoreType.DMA((7,)), pltpu.SemaphoreType.DMA((7,))])(block)


DIRECT = ("w_mlp_in", "w_mlp_out", "w_out")
BIG = DIRECT + ("w_in", "w_uq", "w_ukv")
COL_SHARDED = {"w_in": True, "w_uq": True, "w_ukv": True, "w_out": False, "w_mlp_in": True, "w_mlp_out": False}
SMALL = ("g_mix", "q_norm", "kv_norm", "rpb", "out_norm_a", "out_norm_b", "out_norm_c", "g_mlp", "g_final")
PACK_C = 1024
ROW_ALIGN = 32


def _pack_rows(parts):
    flat = jnp.concatenate([p.reshape(-1, PACK_C) for p in parts], axis=0)
    return jnp.pad(flat, ((0, -flat.shape[0] % ROW_ALIGN), (0, 0)))


def _unpack_rows(flat, shapes):
    out, row = [], 0
    for shp in shapes:
        n = int(np.prod(shp)) // PACK_C
        out.append(flat[row:row + n].reshape(shp))
        row += n
    return out


def _full_from_shards(name, g):
    if COL_SHARDED[name]:
        return g.transpose(1, 2, 0, 3).reshape(g.shape[1], g.shape[2], 4 * g.shape[3])
    return g.transpose(1, 0, 2, 3).reshape(g.shape[1], 4 * g.shape[2], g.shape[3])


def _shards_from_full(name, w):
    l, k, n = w.shape
    if COL_SHARDED[name]:
        return w.reshape(l, k, 4, n // 4).transpose(2, 0, 1, 3)
    return w.reshape(l, 4, k // 4, n).transpose(1, 0, 2, 3)


def _arrange_w_in(w):
    z = jnp.zeros(w.shape[:-1] + (COL_B - COL_KPE - QK_ROPE,), w.dtype)
    return jnp.concatenate([w[..., :COL_KPE + QK_ROPE], z, w[..., COL_KPE + QK_ROPE:]], axis=-1)


def _unarrange_w_in(w):
    return jnp.concatenate([w[..., :COL_KPE + QK_ROPE], w[..., COL_B:]], axis=-1)


def _arrange_w_uq(w):
    per = HEAD_DIM + QK_ROPE
    z = jnp.zeros(w.shape[:-1] + (LANE - per,), w.dtype)
    cols = []
    for h in range(HEADS_A):
        cols += [w[..., h * per:(h + 1) * per], z]
    return jnp.concatenate(cols, axis=-1)


def _unarrange_w_uq(w):
    per = HEAD_DIM + QK_ROPE
    return jnp.concatenate([w[..., h * LANE:h * LANE + per] for h in range(HEADS_A)], axis=-1)


def _arrange_w_ukv(w):
    z = jnp.zeros(w.shape[:-1] + (HEAD_DIM,), w.dtype)
    ks = []
    for h in range(HEADS_A):
        ks += [w[..., h * LANE:h * LANE + HEAD_DIM], z]
    vs = [w[..., h * LANE + HEAD_DIM:(h + 1) * LANE] for h in range(HEADS_A)]
    return jnp.concatenate(ks + vs, axis=-1)


def _unarrange_w_ukv(w):
    cols = []
    for h in range(HEADS_A):
        cols += [w[..., h * LANE:h * LANE + HEAD_DIM], w[..., W_A2 + h * HEAD_DIM:W_A2 + (h + 1) * HEAD_DIM]]
    return jnp.concatenate(cols, axis=-1)


def _layer_fwd(x, w, sm, tabs, consts):
    t32, t64 = tabs
    onehot, col_ok, _, band = consts
    h, proj = _norm_mm(x, sm["g_mix"], w["w_in"], name="in_proj", relu2=False)
    cqn, ckvn, kpe, qkvb, qkvc = _prep_fwd(proj, sm["q_norm"], sm["kv_norm"], t32, t64)
    qa2, ka2, kat, va1 = _a_post_fwd(cqn, ckvn, w["w_uq"], w["w_ukv"], kpe, t32)
    o_a, lse_a = _dense_fwd(qa2, ka2, va1)
    branch = [_banded_fwd(qkvb, dil, band) for _, dil in DILATED_PAIRS]
    tfull = _rpb_expand(sm["rpb"], onehot, col_ok)
    o_c, lse_c = _natten_fwd(qkvc, tfull)
    o_b, lse_b, mixed = _outnorm_fwd(o_a, [b[0] for b in branch], [b[1] for b in branch], o_c, sm["out_norm_a"],
                                     sm["out_norm_b"], sm["out_norm_c"])
    x_mid = _mm_nn(mixed, w["w_out"], name="out_proj", res=x)
    h2, act = _norm_mm(x_mid, sm["g_mlp"], w["w_mlp_in"], name="mlp_in", relu2=True)
    x_out = _mm_nn(act, w["w_mlp_out"], name="mlp_out", res=x_mid)
    saved = dict(x=x, h=h, proj=proj, cqn=cqn, ckvn=ckvn, qkvb=qkvb, qkvc=qkvc, qa2=qa2, ka2=ka2, kat=kat, va1=va1, o_a=o_a,
                 lse_a=lse_a, o_b=o_b, lse_b=lse_b, o_c=o_c, lse_c=lse_c, tfull=tfull, mixed=mixed, x_mid=x_mid, h2=h2,
                 act=act)
    return x_out, saved


def _layer_bwd(dx, dxb, sv, w, sm, tabs, consts, packed, places):
    t32, t64 = tabs
    onehot, _, row_sel, band = consts
    g = {}
    du = _mm_nt(dxb, w["w_mlp_out"], name="mlp_out_dx", out_dtype=BF16, relu2_act=sv["act"])
    packed = _mm_tn(sv["act"], dxb, name="mlp_out_dw", packed=(packed,) + places["w_mlp_out"])
    dx_mid, dmb, g["g_mlp"] = _mm_nt_norm_bwd(du, w["w_mlp_in"], sv["x_mid"], sm["g_mlp"], dx, name="mlp_in_dx")
    packed = _mm_tn(sv["h2"], du, name="mlp_in_dw", packed=(packed,) + places["w_mlp_in"])
    packed = _mm_tn(sv["mixed"], dmb, name="out_proj_dw", packed=(packed,) + places["w_out"])
    (do_a, do_b, do_c, dl_a, dl_b, dl_c, g["out_norm_a"], g["out_norm_b"], g["out_norm_c"]) = _outnorm_bwd(
        dmb, w["w_out"], sv["o_a"], sv["o_b"], sv["o_c"], sm["out_norm_a"], sm["out_norm_b"], sm["out_norm_c"])
    dqa2_t, dka2, dva = _dense_bwd(sv["qa2"], sv["ka2"], sv["kat"], sv["va1"], do_a, sv["lse_a"], dl_a)
    db = []
    for _, dil in DILATED_PAIRS:
        db += _banded_bwd(sv["qkvb"], do_b, sv["lse_b"], dl_b, dil, band)
    dq_c, dk_c, dv_c, dtfull = _natten_bwd(sv["qkvc"], sv["tfull"], do_c, sv["lse_c"], dl_c)
    g["rpb"] = _rpb_grad(dtfull, onehot, row_sel)
    dqa, dkva, dkpe = _a_post_bwd(dqa2_t, dka2, dva, t32)
    dcqn = _mm_nt(dqa, w["w_uq"], name="q_up_dx")
    g["w_uq"] = _unarrange_w_uq(_mm_tn(sv["cqn"], dqa, name="q_up_dw"))
    dckvn = _mm_nt(dkva, w["w_ukv"], name="kv_up_dx")
    g["w_ukv"] = _unarrange_w_ukv(_mm_tn(sv["ckvn"], dkva, name="kv_up_dw"))
    dproj, g["q_norm"], g["kv_norm"] = _prep_bwd(sv["proj"], sm["q_norm"], sm["kv_norm"], t32, t64, dcqn, dckvn, dkpe,
                                                  db, (dq_c, dk_c, dv_c))
    g["w_in"] = _unarrange_w_in(_mm_tn(sv["h"], dproj, name="in_proj_dw"))
    dx_in, dxb_in, g["g_mix"] = _mm_nt_norm_bwd(dproj, w["w_in"], sv["x"], sm["g_mix"], dx_mid, name="in_proj_dx")
    return dx_in, dxb_in, g, packed


def _packed_places(offs, l):
    d, r = D_MODEL, D_MODEL // 4
    return {"w_mlp_in": (512, lambda i, j: (j, (offs["w_mlp_in"] + l * d) // 512 + i)),
            "w_mlp_out": (512, lambda i, j: (i // 2, (offs["w_mlp_out"] + l * d) // 512 + i % 2)),
            "w_out": (r, lambda i, j: (i, (offs["w_out"] + l * r) // r))}


def _local_step(x, target, wfull, small, packed_shape, offs):
    s = x.shape[0]
    tabs = (_rope_tables(s, QK_ROPE // 2, 1, lead=HEAD_DIM), _rope_tables(s, HEAD_DIM // 2, 2))
    consts = _rpb_constants() + (_band_bias_table(),)
    saved = []
    for l in range(DEPTH):
        wl = {k: v[l] for k, v in wfull.items()}
        sl = {k: small[k][l] for k in SMALL if k != "g_final"}
        x, sv = _layer_fwd(x, wl, sl, tabs, consts)
        saved.append(sv)
    loss, dx, dxb, dg_final = _loss_head(x, small["g_final"], target)
    grads = [None] * DEPTH
    packed = packed_shape
    for l in reversed(range(DEPTH)):
        wl = {k: v[l] for k, v in wfull.items()}
        sl = {k: small[k][l] for k in SMALL if k != "g_final"}
        dx, dxb, grads[l], packed = _layer_bwd(dx, dxb, saved[l], wl, sl, tabs, consts, packed, _packed_places(offs, l))
    return loss, dx, grads, dg_final, packed


ARRANGE = {"w_in": _arrange_w_in, "w_uq": _arrange_w_uq, "w_ukv": _arrange_w_ukv}


def kernel(x, g_mix, w_in, q_norm, w_uq, kv_norm, w_ukv, rpb, out_norm_a, out_norm_b, out_norm_c, w_out, g_mlp, w_mlp_in, w_mlp_out, g_final, loss_target, m_g_mix, m_w_in, m_q_norm, m_w_uq, m_kv_norm, m_w_ukv, m_rpb, m_out_norm_a, m_out_norm_b, m_out_norm_c, m_w_out, m_g_mlp, m_w_mlp_in, m_w_mlp_out, m_g_final, v_g_mix, v_w_in, v_q_norm, v_w_uq, v_kv_norm, v_w_ukv, v_rpb, v_out_norm_a, v_out_norm_b, v_out_norm_c, v_w_out, v_g_mlp, v_w_mlp_in, v_w_mlp_out, v_g_final):
    args = dict(locals())
    weights = {k: args[k] for k in BIG + SMALL}
    moms = {k: args["m_" + k] for k in BIG + SMALL}
    vels = {k: args["v_" + k] for k in BIG + SMALL}
    cc = lax.axis_index("c")
    my_chip = 2 * lax.axis_index("x") + lax.axis_index("y")

    shard_shapes = [weights[k].shape for k in BIG]
    packed_w = _pack_rows([weights[k].astype(BF16) for k in BIG])
    rows = packed_w.shape[0]
    my_half = lax.dynamic_index_in_dim(packed_w.reshape(2, rows // 2, PACK_C), cc, axis=0, keepdims=False)
    gathered = _gather_chips(my_half).reshape(4, rows, PACK_C)
    per_chip = [_unpack_rows(jnp.where(my_chip == j, packed_w, gathered[j]), shard_shapes) for j in range(4)]
    wfull = {}
    for idx, k in enumerate(BIG):
        full = _full_from_shards(k, jnp.stack([per_chip[j][idx] for j in range(4)]))
        wfull[k] = ARRANGE[k](full) if k in ARRANGE else full

    small = {k: weights[k] for k in SMALL}
    offs, row = {}, 0
    for k, shp in zip(BIG, shard_shapes):
        offs[k] = row
        row += int(np.prod(shp)) // PACK_C
    loss, dx, grads, dg_final, packed = _local_step(x[0], loss_target[0], wfull, small, _sds((4, rows, PACK_C), F32), offs)

    small_local = {k: jnp.stack([grads[l][k].reshape(weights[k].shape[1:]) for l in range(DEPTH)])
                   for k in SMALL if k != "g_final"}
    small_local["g_final"] = dg_final.reshape(-1)
    small_shapes = [weights[k].shape for k in SMALL]
    n_small = sum(int(np.prod(s)) for s in small_shapes)
    flat = jnp.concatenate([small_local[k].reshape(-1) for k in SMALL] + [loss[0, :1]])
    rows_small = -(-(n_small + 1) // PACK_C)
    rows_small += -rows_small % 8
    flat = jnp.pad(flat, (0, rows_small * PACK_C - n_small - 1)).reshape(rows_small, PACK_C)
    red = _all_reduce_small(flat).reshape(-1)
    loss_out = red[n_small]
    small_grads, off = {}, 0
    for k, shp in zip(SMALL, small_shapes):
        n = int(np.prod(shp))
        small_grads[k] = red[off:off + n].reshape(shp)
        off += n

    rest = [k for k in BIG if k not in DIRECT]
    by_shard = {k: _shards_from_full(k, jnp.stack([grads[l][k] for l in range(DEPTH)])) for k in rest}
    tail = jnp.stack([_pack_rows([by_shard[k][j] for k in rest]) for j in range(4)])
    assert offs[rest[0]] + tail.shape[1] == rows
    packed = lax.dynamic_update_slice(packed, tail, (0, offs[rest[0]], 0))
    halves = packed.reshape(4, 2, rows // 2, PACK_C)
    pair = _pair_sum(halves, cc, _swap_other_half(halves))
    by_chip = _scatter_chips(pair)
    reduced = _add_n([jnp.where(my_chip == j, pair[j], by_chip[j]) for j in range(4)], name="chip_sum", out_dtype=F32)
    theirs = _swap_sibling(reduced)
    joined = jnp.where(cc == 0, jnp.concatenate([reduced, theirs]), jnp.concatenate([theirs, reduced]))
    big_grads = dict(zip(BIG, _unpack_rows(joined, shard_shapes)))

    out_g, out_d, out_m, out_v = {}, {}, {}, {}
    for k in BIG:
        shp = weights[k].shape
        two_d = (shp[0] * shp[1], shp[2])
        d, nm, nv = _adamw(weights[k].reshape(two_d), big_grads[k].reshape(two_d), moms[k].reshape(two_d),
                           vels[k].reshape(two_d), name="adamw_" + k)
        out_g[k], out_d[k], out_m[k], out_v[k] = big_grads[k], d.reshape(shp), nm.reshape(shp), nv.reshape(shp)

    def pack_small(tree):
        f = jnp.concatenate([tree[k].reshape(-1) for k in SMALL])
        return jnp.pad(f, (0, rows_small * PACK_C - n_small)).reshape(rows_small, PACK_C)

    d, nm, nv = _adamw(pack_small(small), pack_small(small_grads), pack_small(moms), pack_small(vels), name="adamw_small")
    for tree, flat_out in ((out_d, d), (out_m, nm), (out_v, nv)):
        off = 0
        fo = flat_out.reshape(-1)
        for k, shp in zip(SMALL, small_shapes):
            n = int(np.prod(shp))
            tree[k] = fo[off:off + n].reshape(shp)
            off += n
    out_g.update(small_grads)

    order = ("g_mix", "w_in", "q_norm", "w_uq", "kv_norm", "w_ukv", "rpb", "out_norm_a", "out_norm_b", "out_norm_c", "w_out",
             "g_mlp", "w_mlp_in", "w_mlp_out", "g_final")
    return (loss_out, dx.reshape(x.shape), *[out_g[k] for k in order], *[out_d[k] for k in order],
            *[out_m[k] for k in order], *[out_v[k] for k in order])
```

```python
import math

import numpy as np
import jax
import jax.numpy as jnp
from jax import lax
from jax.experimental import pallas as pl
from jax.experimental.pallas import tpu as pltpu

F32 = jnp.float32
BF16 = jnp.bfloat16

D_MODEL = 1024
HEAD_DIM = 64
Q_LORA = 256
KV_LORA = 128
QK_ROPE = 32
HEADS_A = 6
HEADS_B = 6
HEADS_C = 4
DILATED_PAIRS = ((128, 1), (512, 4), (2048, 16))
BAND_HALF = 64
GRID_W = 64
NA_ROWS = 8
NA_COLS = 16
D_FF = 4096
ROPE_THETA = 10000.0
NORM_EPS = 1e-6
NEG_INF = -1e30
DEPTH = 4

LANE = 128
PROJ_W = 2432
COL_CKV = 256
COL_KPE = 384
COL_B = 512
COL_C = 1664
W_A2 = 768
W_KV = W_A2 + 384
WIDTH_AB = 384
WIDTH_C = 256
SCALE_A = (HEAD_DIM + QK_ROPE) ** -0.5
SCALE_BC = HEAD_DIM ** -0.5

ADAM_LR = 0.001
ADAM_B1 = 0.9
ADAM_B2 = 0.999
ADAM_EPS = 1e-08
ADAM_WD = 0.01
ADAM_STEP = 10

VMEM_LIMIT = 56 * 1024 * 1024
MESH_T = pl.DeviceIdType.MESH


def _cp(*sem):
    return pltpu.CompilerParams(dimension_semantics=sem or None, vmem_limit_bytes=VMEM_LIMIT)


def _tile(n, cands):
    for c in cands:
        if n % c == 0:
            return c
    return n


def _sds(shape, dtype):
    return jax.ShapeDtypeStruct(shape, dtype)


ROW_TILE_BYTES = 12 * 1024 * 1024


def _row_tiles(row_bytes):
    return tuple(t for t in (2048, 1024, 512, 256, 128) if t * row_bytes <= ROW_TILE_BYTES or t <= 512)


def _mm_nn(a, b, *, name, out_dtype=F32, res=None):
    m, k = a.shape
    n = b.shape[1]
    tn = _tile(n, (1024, 768, 512)) if n % LANE == 0 and n != PROJ_W else n
    tm = _tile(m, _row_tiles(2 * k + tn * (jnp.dtype(out_dtype).itemsize + (4 if res is not None else 0))))

    def body(*refs):
        a_ref, b_ref = refs[0], refs[1]
        o_ref = refs[-1]
        acc = jnp.dot(a_ref[...], b_ref[...], preferred_element_type=F32)
        if res is not None:
            acc = refs[2][...] + acc
        o_ref[...] = acc.astype(o_ref.dtype)

    in_specs = [pl.BlockSpec((tm, k), lambda j, i: (i, 0)), pl.BlockSpec((k, tn), lambda j, i: (0, j))]
    args = [a, b]
    if res is not None:
        in_specs.append(pl.BlockSpec((tm, tn), lambda j, i: (i, j)))
        args.append(res)
    return pl.pallas_call(
        body, name=name, grid=(n // tn, m // tm), in_specs=in_specs,
        out_specs=pl.BlockSpec((tm, tn), lambda j, i: (i, j)), out_shape=_sds((m, n), out_dtype),
        compiler_params=_cp("parallel", "parallel"))(*args)


def _mlp_in(x, g, w):
    m, k = x.shape
    n = w.shape[1]
    tm = _tile(m, (256, 128))

    def body(x_ref, g_ref, w_ref, h_ref, o_ref):
        xv = x_ref[...]
        h = (xv * _rstd(xv) * g_ref[...]).astype(BF16)
        h_ref[...] = h
        u = jnp.dot(h, w_ref[...], preferred_element_type=F32)
        o_ref[...] = jnp.square(jnp.maximum(u, 0.0)).astype(BF16)

    return pl.pallas_call(
        body, name="mlp_in", grid=(m // tm,),
        in_specs=[pl.BlockSpec((tm, k), lambda i: (i, 0)), pl.BlockSpec((1, k), lambda i: (0, 0)),
                  pl.BlockSpec((k, n), lambda i: (0, 0))],
        out_specs=(pl.BlockSpec((tm, k), lambda i: (i, 0)), pl.BlockSpec((tm, n), lambda i: (i, 0))),
        out_shape=(_sds((m, k), BF16), _sds((m, n), BF16)),
        compiler_params=_cp("parallel"))(x, g.reshape(1, k), w)


def _mm_nt(a, b, *, name, out_dtype=F32, relu2_act=None):
    m, c = a.shape
    n = b.shape[0]
    tn = _tile(n, (1024, 512, 256, 128))
    tm = _tile(m, _row_tiles(2 * c + tn * (jnp.dtype(out_dtype).itemsize + (2 if relu2_act is not None else 0))))

    def body(*refs):
        a_ref, b_ref = refs[0], refs[1]
        o_ref = refs[-1]
        acc = lax.dot_general(a_ref[...], b_ref[...], (((1,), (1,)), ((), ())), preferred_element_type=F32)
        if relu2_act is not None:
            acc = acc * (2.0 * jnp.sqrt(refs[2][...].astype(F32)))
        o_ref[...] = acc.astype(o_ref.dtype)

    in_specs = [pl.BlockSpec((tm, c), lambda j, i: (i, 0)), pl.BlockSpec((tn, c), lambda j, i: (j, 0))]
    args = [a, b]
    if relu2_act is not None:
        in_specs.append(pl.BlockSpec((tm, tn), lambda j, i: (i, j)))
        args.append(relu2_act)
    return pl.pallas_call(
        body, name=name, grid=(n // tn, m // tm), in_specs=in_specs,
        out_specs=pl.BlockSpec((tm, tn), lambda j, i: (i, j)), out_shape=_sds((m, n), out_dtype),
        compiler_params=_cp("parallel", "parallel"))(*args)


def _mm_nt_norm_bwd(a, b, x, g, res, *, name):
    m, c = a.shape
    d = b.shape[0]
    tm = _tile(m, (512, 256, 128))

    def body(a_ref, b_ref, x_ref, g_ref, res_ref, dx_ref, dxb_ref, dg_ref):
        dy = lax.dot_general(a_ref[...], b_ref[...], (((1,), (1,)), ((), ())), preferred_element_type=F32)
        dx, dg = _rms_bwd_rows(x_ref[...], g_ref[...], dy)
        dx = res_ref[...] + dx
        dx_ref[...] = dx
        dxb_ref[...] = dx.astype(BF16)
        _accum(dg_ref, dg, pl.program_id(0) == 0)

    row = pl.BlockSpec((tm, d), lambda i: (i, 0))
    fix = pl.BlockSpec((1, d), lambda i: (0, 0))
    return pl.pallas_call(
        body, name=name, grid=(m // tm,),
        in_specs=[pl.BlockSpec((tm, c), lambda i: (i, 0)), pl.BlockSpec((d, c), lambda i: (0, 0)), row, fix, row],
        out_specs=(row, row, fix), out_shape=(_sds((m, d), F32), _sds((m, d), BF16), _sds((1, d), F32)),
        compiler_params=_cp("arbitrary"))(a, b, x, g.reshape(1, d), res)


def _mm_tn(a, b, *, name, packed=None):
    m, ka = a.shape
    nb = b.shape[1]
    tka = _tile(ka, (512, 256, 128)) if packed is None else packed[1]
    tnb = _tile(nb, (1024, 768, 512)) if nb != PROJ_W else nb
    tc = _tile(m, (4096, 2048, 1024, 512, 256, 128) if tnb <= PACK_C else (2048, 1024, 512, 256, 128))

    def body(*refs):
        a_ref, b_ref, o_ref = refs[0], refs[1], refs[-1]
        part = lax.dot_general(a_ref[...], b_ref[...], (((0,), (0,)), ((), ())), preferred_element_type=F32)

        @pl.when(pl.program_id(2) == 0)
        def _():
            o_ref[...] = part

        @pl.when(pl.program_id(2) != 0)
        def _():
            o_ref[...] += part

    in_specs = [pl.BlockSpec((tc, tka), lambda i, j, c: (c, i)), pl.BlockSpec((tc, tnb), lambda i, j, c: (c, j))]
    kwargs = dict(out_specs=pl.BlockSpec((tka, tnb), lambda i, j, c: (i, j)), out_shape=_sds((ka, nb), F32))
    args = [a, b]
    if packed is not None:
        buf, _, place = packed
        assert tnb == PACK_C
        kwargs = dict(out_specs=pl.BlockSpec((None, tka, tnb), lambda i, j, c: place(i, j) + (0,)))
        if isinstance(buf, jax.ShapeDtypeStruct):
            kwargs["out_shape"] = buf
        else:
            kwargs.update(out_shape=_sds(buf.shape, buf.dtype), input_output_aliases={2: 0})
            in_specs.append(pl.BlockSpec(memory_space=pl.ANY))
            args.append(buf)
    return pl.pallas_call(
        body, name=name, grid=(ka // tka, nb // tnb, m // tc), in_specs=in_specs,
        compiler_params=_cp("parallel", "parallel", "arbitrary"), **kwargs)(*args)


def _rstd(x):
    return lax.rsqrt(jnp.mean(x * x, axis=-1, keepdims=True) + NORM_EPS)


def _rms_bwd_rows(x, g, dy):
    r = _rstd(x)
    gy = dy * g
    c = jnp.sum(x * gy, axis=-1, keepdims=True) * (r * r * r) * (1.0 / x.shape[-1])
    return r * gy - x * c, jnp.sum(dy * x * r, axis=0, keepdims=True)


def _accum(ref, part, first):
    @pl.when(first)
    def _():
        ref[...] = part

    @pl.when(jnp.logical_not(first))
    def _():
        ref[...] += part


def _rope(x, c, s1, s2, sh):
    return x * c + pltpu.roll(x, LANE - sh, 1) * s1 + pltpu.roll(x, sh, 1) * s2


def _rope_t(g, c, s1, s2, sh):
    return g * c + pltpu.roll(g * s1, sh, 1) + pltpu.roll(g * s2, LANE - sh, 1)


def _rope_tables(s, half, reps, lead=0):
    pos = jnp.arange(s, dtype=F32)
    inv_freq = ROPE_THETA ** (-jnp.arange(half, dtype=F32) / half)
    ang = pos[:, None] * inv_freq[None, :]
    cos, sin = jnp.cos(ang), jnp.sin(ang)
    zero = jnp.zeros_like(cos)
    ones, lead0 = jnp.ones((s, lead), F32), jnp.zeros((s, lead), F32)
    pad = jnp.zeros((s, LANE - lead - 2 * half * reps), F32)
    c = jnp.concatenate([ones] + [cos, cos] * reps + [pad], axis=1)
    s1 = jnp.concatenate([lead0] + [-sin, zero] * reps + [pad], axis=1)
    s2 = jnp.concatenate([lead0] + [zero, sin] * reps + [pad], axis=1)
    return c, s1, s2


def _lane_lt64(shape):
    return lax.broadcasted_iota(jnp.int32, shape, len(shape) - 1) % LANE < HEAD_DIM


def _group_sum(x):
    outs = []
    for b in range(x.shape[1] // LANE):
        blk = x[:, b * LANE:(b + 1) * LANE]
        lo = _lane_lt64(blk.shape)
        s0 = jnp.sum(jnp.where(lo, blk, 0.0), axis=1, keepdims=True)
        s1 = jnp.sum(jnp.where(lo, 0.0, blk), axis=1, keepdims=True)
        outs.append(jnp.where(lo, s0, s1))
    return outs


def _row_spec(ts, w):
    return pl.BlockSpec((ts, w), lambda i: (i, 0))


def _fix_spec(w):
    return pl.BlockSpec((1, w), lambda i: (0, 0))


def _loss_head(x, g, target):
    s, d = x.shape
    ts = _tile(s, (512, 256, 128))

    def body(x_ref, g_ref, t_ref, loss_ref, dx_ref, dxb_ref, dg_ref):
        xv, gv = x_ref[...], g_ref[...]
        err = xv * _rstd(xv) * gv - t_ref[...]
        part = 0.5 * jnp.sum(jnp.sum(err * err, axis=-1, keepdims=True) * (1.0 / d), axis=0, keepdims=True)
        dx, dg = _rms_bwd_rows(xv, gv, err * (1.0 / d))
        dx_ref[...] = dx
        dxb_ref[...] = dx.astype(BF16)
        first = pl.program_id(0) == 0
        _accum(dg_ref, dg, first)
        _accum(loss_ref, jnp.broadcast_to(part, (1, LANE)), first)

    return pl.pallas_call(
        body, name="loss_head", grid=(s // ts,), in_specs=[_row_spec(ts, d), _fix_spec(d), _row_spec(ts, d)],
        out_specs=(_fix_spec(LANE), _row_spec(ts, d), _row_spec(ts, d), _fix_spec(d)),
        out_shape=(_sds((1, LANE), F32), _sds((s, d), F32), _sds((s, d), BF16), _sds((1, d), F32)),
        compiler_params=_cp("arbitrary"))(x, g.reshape(1, d), target)


def _in_proj(x, g, w, q_norm, kv_norm, t32, t64):
    s, d = x.shape
    ts = _tile(s, (256, 128))

    def body(x_ref, g_ref, w_ref, qn_ref, kn_ref, c32, a32, b32, c64, a64, b64,
             h_ref, p_ref, cqn_ref, ckvn_ref, kpe_ref, qkvb_ref, qkvc_ref):
        xv = x_ref[...]
        h = (xv * _rstd(xv) * g_ref[...]).astype(BF16)
        h_ref[...] = h
        p = jnp.dot(h, w_ref[...], preferred_element_type=F32)
        p_ref[...] = p
        cq = p[:, 0:Q_LORA]
        cqn_ref[...] = (cq * _rstd(cq) * qn_ref[...]).astype(BF16)
        ckv = p[:, COL_CKV:COL_KPE]
        ckvn_ref[...] = (ckv * _rstd(ckv) * kn_ref[...]).astype(BF16)
        kp = pltpu.roll(p[:, COL_KPE:COL_B], HEAD_DIM, 1)
        kpe_ref[...] = _rope(kp, c32[...], a32[...], b32[...], QK_ROPE // 2).astype(BF16)
        for b in range(6):
            blk = _rope(p[:, COL_B + b * LANE:COL_B + (b + 1) * LANE], c64[...], a64[...], b64[...], HEAD_DIM // 2)
            if b < 3:
                blk = blk * SCALE_BC
            qkvb_ref[:, b * LANE:(b + 1) * LANE] = blk.astype(BF16)
        qkvb_ref[:, 2 * WIDTH_AB:3 * WIDTH_AB] = p[:, COL_B + 2 * WIDTH_AB:COL_C].astype(BF16)
        qkvc_ref[:, 0:WIDTH_C] = (p[:, COL_C:COL_C + WIDTH_C] * SCALE_BC).astype(BF16)
        qkvc_ref[:, WIDTH_C:3 * WIDTH_C] = p[:, COL_C + WIDTH_C:PROJ_W].astype(BF16)

    tab = [_row_spec(ts, LANE)] * 6
    widths = (d, PROJ_W, Q_LORA, KV_LORA, LANE, 3 * WIDTH_AB, 3 * WIDTH_C)
    dtypes = (BF16, F32, BF16, BF16, BF16, BF16, BF16)
    return pl.pallas_call(
        body, name="in_proj", grid=(s // ts,),
        in_specs=[_row_spec(ts, d), _fix_spec(d), pl.BlockSpec(w.shape, lambda i: (0, 0)), _fix_spec(Q_LORA),
                  _fix_spec(KV_LORA)] + tab,
        out_specs=tuple(_row_spec(ts, wd) for wd in widths),
        out_shape=tuple(_sds((s, wd), dt) for wd, dt in zip(widths, dtypes)),
        compiler_params=_cp("parallel"))(x, g.reshape(1, d), w, q_norm.reshape(1, -1), kv_norm.reshape(1, -1), *t32, *t64)


def _prep_bwd(proj, q_norm, kv_norm, t32, t64, dcqn, dckvn, dkpe, db, dc):
    s = proj.shape[0]
    ts = _tile(s, (256, 128))

    def body(p_ref, qn_ref, kn_ref, c32, a32, b32, c64, a64, b64, dcqn_ref, dckvn_ref, dkpe_ref, *rest):
        db_refs, dc_refs = rest[0:9], rest[9:12]
        dp_ref, dqn_ref, dkn_ref = rest[12:15]
        first = pl.program_id(0) == 0
        dx, dg = _rms_bwd_rows(p_ref[:, 0:Q_LORA], qn_ref[...], dcqn_ref[...])
        dp_ref[:, 0:Q_LORA] = dx.astype(BF16)
        _accum(dqn_ref, dg, first)
        dx, dg = _rms_bwd_rows(p_ref[:, COL_CKV:COL_KPE], kn_ref[...], dckvn_ref[...])
        dp_ref[:, COL_CKV:COL_KPE] = dx.astype(BF16)
        _accum(dkn_ref, dg, first)
        g = pltpu.roll(_rope_t(dkpe_ref[...], c32[...], a32[...], b32[...], QK_ROPE // 2), LANE - HEAD_DIM, 1)
        lane = lax.broadcasted_iota(jnp.int32, g.shape, 1)
        dp_ref[:, COL_KPE:COL_B] = jnp.where(lane < QK_ROPE, g, 0.0).astype(BF16)
        for which in range(3):
            for b in range(3):
                sl = slice(b * LANE, (b + 1) * LANE)
                g = db_refs[which][:, sl] + db_refs[3 + which][:, sl] + db_refs[6 + which][:, sl]
                if which < 2:
                    g = _rope_t(g, c64[...], a64[...], b64[...], HEAD_DIM // 2)
                if which == 0:
                    g = g * SCALE_BC
                col = COL_B + which * WIDTH_AB + b * LANE
                dp_ref[:, col:col + LANE] = g.astype(BF16)
        dp_ref[:, COL_C:COL_C + WIDTH_C] = (dc_refs[0][...] * SCALE_BC).astype(BF16)
        dp_ref[:, COL_C + WIDTH_C:COL_C + 2 * WIDTH_C] = dc_refs[1][...].astype(BF16)
        dp_ref[:, COL_C + 2 * WIDTH_C:PROJ_W] = dc_refs[2][...].astype(BF16)

    tab = [_row_spec(ts, LANE)] * 6
    in_specs = ([_row_spec(ts, PROJ_W), _fix_spec(Q_LORA), _fix_spec(KV_LORA)] + tab
                + [_row_spec(ts, Q_LORA), _row_spec(ts, KV_LORA), _row_spec(ts, LANE)]
                + [_row_spec(ts, WIDTH_AB)] * 9 + [_row_spec(ts, WIDTH_C)] * 3)
    return pl.pallas_call(
        body, name="prep_bwd", grid=(s // ts,), in_specs=in_specs,
        out_specs=(_row_spec(ts, PROJ_W), _fix_spec(Q_LORA), _fix_spec(KV_LORA)),
        out_shape=(_sds((s, PROJ_W), BF16), _sds((1, Q_LORA), F32), _sds((1, KV_LORA), F32)),
        compiler_params=_cp("arbitrary"))(proj, q_norm.reshape(1, -1), kv_norm.reshape(1, -1), *t32, *t64,
                                          dcqn, dckvn, dkpe, *db, *dc)


def _a_post_fwd(cqn, ckvn, w_uq, w_ukv, kpe, t32):
    s = cqn.shape[0]
    ts = _tile(s, (512, 256, 128))

    def body(cq_ref, ckv_ref, wq_ref, wkv_ref, kpe_ref, c32, a32, b32, q_ref, k_ref, kt_ref, v_ref):
        qa = jnp.dot(cq_ref[...], wq_ref[...], preferred_element_type=F32)
        kva = jnp.dot(ckv_ref[...], wkv_ref[...], preferred_element_type=F32)
        kpe = kpe_ref[...].astype(F32)
        for h in range(HEADS_A):
            hb = slice(h * LANE, (h + 1) * LANE)
            q_ref[:, hb] = _rope(qa[:, hb], c32[...], a32[...], b32[...], QK_ROPE // 2).astype(BF16)
            kh = kva[:, hb] + kpe
            k_ref[:, hb] = kh.astype(BF16)
            kt_ref[hb, :] = kh.T.astype(BF16)
        for p in range(3):
            lo, hi = 2 * p * LANE, (2 * p + 1) * LANE
            v_ref[:, lo:hi] = kva[:, W_A2 + p * LANE:W_A2 + (p + 1) * LANE].astype(BF16)
            v_ref[:, hi:hi + LANE] = jnp.ones((ts, LANE), BF16)

    whole = lambda shape: pl.BlockSpec(shape, lambda i: (0, 0))
    return pl.pallas_call(
        body, name="a_post_fwd", grid=(s // ts,),
        in_specs=[_row_spec(ts, Q_LORA), _row_spec(ts, KV_LORA), whole(w_uq.shape), whole(w_ukv.shape), _row_spec(ts, LANE)]
        + [_row_spec(ts, LANE)] * 3,
        out_specs=(_row_spec(ts, W_A2), _row_spec(ts, W_A2), pl.BlockSpec((W_A2, ts), lambda i: (0, i)), _row_spec(ts, W_A2)),
        out_shape=(_sds((s, W_A2), BF16), _sds((s, W_A2), BF16), _sds((W_A2, s), BF16), _sds((s, W_A2), BF16)),
        compiler_params=_cp("parallel"))(cqn, ckvn, w_uq, w_ukv, kpe, *t32)


def _a_post_bwd(dqa2_t, dka2, dva, t32):
    s = dka2.shape[0]
    ts = _tile(s, (512, 256, 128))

    def body(dqt_ref, dk_ref, dv_ref, c32, a32, b32, dqa_ref, dkva_ref, dkpe_ref):
        acc = None
        for h in range(HEADS_A):
            hb = slice(h * LANE, (h + 1) * LANE)
            dqa_ref[:, hb] = _rope_t(dqt_ref[hb, :].T * SCALE_A, c32[...], a32[...], b32[...], QK_ROPE // 2).astype(BF16)
            part = dk_ref[:, hb] * SCALE_A
            dkva_ref[:, hb] = part.astype(BF16)
            acc = part if acc is None else acc + part
        dkva_ref[:, W_A2:W_KV] = dv_ref[...].astype(BF16)
        dkpe_ref[...] = acc

    return pl.pallas_call(
        body, name="a_post_bwd", grid=(s // ts,),
        in_specs=[pl.BlockSpec((W_A2, ts), lambda i: (0, i)), _row_spec(ts, W_A2), _row_spec(ts, WIDTH_AB)]
        + [_row_spec(ts, LANE)] * 3,
        out_specs=(_row_spec(ts, W_A2), _row_spec(ts, W_KV), _row_spec(ts, LANE)),
        out_shape=(_sds((s, W_A2), BF16), _sds((s, W_KV), BF16), _sds((s, LANE), F32)),
        compiler_params=_cp("parallel"))(dqa2_t, dka2, dva, *t32)


def _pair_masks():
    lane = lax.broadcasted_iota(jnp.int32, (1, LANE), 1)
    return lane < HEAD_DIM, lane >= HEAD_DIM


def _nt(a, b):
    return lax.dot_general(a, b, (((1,), (1,)), ((), ())), preferred_element_type=F32)


def _tn(a, b):
    return lax.dot_general(a, b, (((0,), (0,)), ((), ())), preferred_element_type=F32)


def _stack_heads(x):
    m0, m1 = _pair_masks()
    zero = jnp.zeros_like(x)
    return jnp.concatenate([jnp.where(m0, x, zero), jnp.where(m1, x, zero)], axis=0)


def _stack_stat(x):
    return jnp.concatenate([x[:, 0:1], x[:, HEAD_DIM:HEAD_DIM + 1]], axis=0)


def _softmax_pair(q, kk, vv, bias2):
    t = q.shape[0]
    s = _nt(_stack_heads(q), kk) + bias2
    m = jnp.max(s, axis=1, keepdims=True)
    p = jnp.exp(s - m)
    l = jnp.sum(p, axis=1, keepdims=True)
    o2 = jnp.dot(p.astype(BF16), vv, preferred_element_type=F32) / l
    lse2 = m + jnp.log(l)
    lo = _lane_lt64((t, LANE))
    return jnp.where(lo, o2[:t], o2[t:]), jnp.where(lo, lse2[:t], lse2[t:])


def _softmax_pair_bwd(q, kk, vv, do, lse, delta, bias2):
    t = q.shape[0]
    q2, do2 = _stack_heads(q), _stack_heads(do)
    p = jnp.exp(_nt(q2, kk) + bias2 - _stack_stat(lse))
    ds = p * (_nt(do2, vv) - _stack_stat(delta))
    dsb = ds.astype(BF16)
    dq2 = jnp.dot(dsb, kk, preferred_element_type=F32)
    lo = _lane_lt64((t, LANE))
    return jnp.where(lo, dq2[:t], dq2[t:]), _tn(dsb, q2), _tn(p.astype(BF16), do2), ds


DENSE_FWD_TQ, DENSE_FWD_TK = 512, 8192
DENSE_BWD_TQ, DENSE_BWD_TK = 2048, 1024
LOG2E = math.log2(math.e)


def _dense_fwd(qa, ka, va1):
    s = qa.shape[0]
    tq, tk = min(DENSE_FWD_TQ, s), min(DENSE_FWD_TK, s)
    nk = s // tk
    c = SCALE_A * LOG2E

    def body(q_ref, k_ref, v_ref, o_ref, lse_ref, m_sc, acc_sc):
        j = pl.program_id(2)

        @pl.when(j == 0)
        def _():
            m_sc[...] = jnp.full(m_sc.shape, NEG_INF, F32)
            acc_sc[...] = jnp.zeros(acc_sc.shape, F32)

        vv = v_ref[...]
        for hh in range(2):
            hs = slice(hh * LANE, (hh + 1) * LANE)
            sc = _nt(q_ref[:, hs], k_ref[:, hs])
            m_prev = m_sc[hh]
            m_new = jnp.maximum(m_prev, jnp.max(sc, axis=1, keepdims=True))
            alpha = jnp.exp2((m_prev - m_new) * c)
            p = jnp.exp2((sc - m_new) * c)
            acc_sc[hh] = alpha * acc_sc[hh] + jnp.dot(p.astype(BF16), vv, preferred_element_type=F32)
            m_sc[hh] = m_new

        @pl.when(j == nk - 1)
        def _():
            lo = _lane_lt64((tq, LANE))
            a0, a1 = acc_sc[0], acc_sc[1]
            l0, l1 = a0[:, LANE:], a1[:, LANE:]
            o_ref[...] = jnp.where(lo, a0[:, :LANE] / l0, a1[:, :LANE] / l1)
            lse_ref[0] = _stat_rows(jnp.where(lo, m_sc[0] * SCALE_A + jnp.log(l0), m_sc[1] * SCALE_A + jnp.log(l1)))

    return pl.pallas_call(
        body, name="dense_fwd", grid=(3, s // tq, nk),
        in_specs=[pl.BlockSpec((tq, 2 * LANE), lambda p, i, j: (i, p)), pl.BlockSpec((tk, 2 * LANE), lambda p, i, j: (j, p)),
                  pl.BlockSpec((tk, 2 * LANE), lambda p, i, j: (j, p))],
        out_specs=(pl.BlockSpec((tq, LANE), lambda p, i, j: (i, p)), pl.BlockSpec((1, 8, tq), lambda p, i, j: (p, 0, i))),
        out_shape=(_sds((s, WIDTH_AB), F32), _sds((3, 8, s), F32)),
        scratch_shapes=[pltpu.VMEM((2, tq, 1), F32), pltpu.VMEM((2, tq, 2 * LANE), F32)],
        compiler_params=_cp("parallel", "parallel", "arbitrary"))(qa, ka, va1)


def _stat_rows(lane_dense):
    tr = lane_dense.T
    return jnp.concatenate([tr[0:1, :], tr[HEAD_DIM:HEAD_DIM + 1, :], jnp.zeros((6, tr.shape[1]), F32)], axis=0)


def _dense_bwd(qa, ka, kat, va1, do, lse_rows, delta_rows):
    s = qa.shape[0]
    tq, tk = min(DENSE_BWD_TQ, s), min(DENSE_BWD_TK, s)
    c = SCALE_A * LOG2E

    def body(q_ref, k_ref, kt_ref, v_ref, do_ref, lse_ref, dl_ref, dqt_ref, dk_ref, dv_ref):
        j, i = pl.program_id(1), pl.program_id(2)

        @pl.when((j == 0) & (i == 0))
        def _():
            dqt_ref[...] = jnp.zeros(dqt_ref.shape, F32)

        vv, do_ = v_ref[...], do_ref[...]
        lse_t, dl_t = lse_ref[0] * LOG2E, dl_ref[0]
        vm = _pair_masks()
        cols = pl.ds(pl.multiple_of(i * tq, tq), tq)
        dv = None
        for hh in range(2):
            hs = slice(hh * LANE, (hh + 1) * LANE)
            qh = q_ref[:, hs]
            dom = jnp.where(vm[hh], do_, jnp.zeros_like(do_))
            pt = jnp.exp2(_nt(k_ref[:, hs], qh) * c - lse_t[hh:hh + 1, :])
            dst = pt * (_nt(vv, dom) - dl_t[hh:hh + 1, :])
            pb, dsb = pt.astype(BF16), dst.astype(BF16)
            dv_h = jnp.dot(pb, dom, preferred_element_type=F32)
            dv = dv_h if dv is None else dv + dv_h
            dqt_ref[hs, cols] += jnp.dot(kt_ref[hs, :], dsb, preferred_element_type=F32)
            dk_h = jnp.dot(dsb, qh, preferred_element_type=F32)

            @pl.when(i == 0)
            def _():
                dk_ref[:, hs] = dk_h

            @pl.when(i != 0)
            def _():
                dk_ref[:, hs] += dk_h
        _accum(dv_ref, dv, i == 0)

    st_spec = pl.BlockSpec((1, 8, tq), lambda p, j, i: (p, 0, i))
    return pl.pallas_call(
        body, name="dense_bwd", grid=(3, s // tk, s // tq),
        in_specs=[pl.BlockSpec((tq, 2 * LANE), lambda p, j, i: (i, p)), pl.BlockSpec((tk, 2 * LANE), lambda p, j, i: (j, p)),
                  pl.BlockSpec((2 * LANE, tk), lambda p, j, i: (p, j)), pl.BlockSpec((tk, LANE), lambda p, j, i: (j, 2 * p)),
                  pl.BlockSpec((tq, LANE), lambda p, j, i: (i, p)), st_spec, st_spec],
        out_specs=(pl.BlockSpec((2 * LANE, s), lambda p, j, i: (p, 0)), pl.BlockSpec((tk, 2 * LANE), lambda p, j, i: (j, p)),
                   pl.BlockSpec((tk, LANE), lambda p, j, i: (j, p))),
        out_shape=(_sds((W_A2, s), F32), _sds((s, W_A2), F32), _sds((s, WIDTH_AB), F32)),
        compiler_params=_cp("parallel", "arbitrary", "arbitrary"))(qa, ka, kat, va1, do, lse_rows, delta_rows)


BAND_TILE = 1024
BAND_SUB = 128
QKV_W = 3 * WIDTH_AB


def _band_bias_table():
    row = np.arange(BAND_SUB)[:, None]
    col = np.arange(2 * BAND_SUB)[None, :]
    band = np.abs(row - col + BAND_HALF) <= BAND_HALF
    variants = []
    for idx in range(4):
        ok = band & ((col >= BAND_HALF) | ((idx & 1) == 0)) & ((col < 2 * BAND_SUB - BAND_HALF) | ((idx & 2) == 0))
        one = np.where(ok, 0.0, NEG_INF).astype(np.float32)
        variants.append(np.concatenate([one, one], axis=0))
    return jnp.asarray(np.stack(variants))


def _band_specs(t, n):
    hpt = t // BAND_HALF
    last = n // BAND_HALF - 1
    return [pl.BlockSpec((BAND_HALF, QKV_W), lambda r, i: (jnp.maximum(i * hpt - 1, 0), r)),
            pl.BlockSpec((t, QKV_W), lambda r, i: (i, r)),
            pl.BlockSpec((BAND_HALF, QKV_W), lambda r, i: (jnp.minimum((i + 1) * hpt, last), r)),
            pl.BlockSpec((4, 2 * BAND_SUB, 2 * BAND_SUB), lambda r, i: (0, 0, 0))]


def _band_bias(b_ref, a, nsub, i, nt):
    idx = 0
    if a == 0:
        idx = idx + (i == 0).astype(jnp.int32)
    if a == nsub - 1:
        idx = idx + 2 * (i == nt - 1).astype(jnp.int32)
    return b_ref[idx]


def _band_kv(left, main, right, p):
    kc = slice(WIDTH_AB + p * LANE, WIDTH_AB + (p + 1) * LANE)
    vc = slice(2 * WIDTH_AB + p * LANE, 2 * WIDTH_AB + (p + 1) * LANE)
    return (jnp.concatenate([left[:, kc], main[:, kc], right[:, kc]], axis=0),
            jnp.concatenate([left[:, vc], main[:, vc], right[:, vc]], axis=0))


def _banded_fwd(qkvb, dil, bias):
    s = qkvb.shape[0]
    n = s // dil
    t = min(n, BAND_TILE)
    nsub, nt = t // BAND_SUB, n // t
    view = qkvb.reshape(n, dil * QKV_W)

    def body(left, main, right, b_ref, o_ref, lse_ref):
        i = pl.program_id(1)
        for p in range(3):
            pc = slice(p * LANE, (p + 1) * LANE)
            kk, vv = _band_kv(left, main, right, p)
            for a in range(nsub):
                rows, win = slice(a * BAND_SUB, (a + 1) * BAND_SUB), slice(a * BAND_SUB, (a + 2) * BAND_SUB)
                o, lse = _softmax_pair(main[rows, pc], kk[win], vv[win], _band_bias(b_ref, a, nsub, i, nt))
                o_ref[rows, pc] = o
                lse_ref[rows, pc] = lse

    o_spec = pl.BlockSpec((t, WIDTH_AB), lambda r, i: (i, r))
    o, lse = pl.pallas_call(
        body, name=f"banded_fwd_d{dil}", grid=(dil, nt), in_specs=_band_specs(t, n), out_specs=(o_spec, o_spec),
        out_shape=(_sds((n, dil * WIDTH_AB), F32), _sds((n, dil * WIDTH_AB), F32)),
        compiler_params=_cp("parallel", "parallel"))(view, view, view, bias)
    return o.reshape(s, WIDTH_AB), lse.reshape(s, WIDTH_AB)


def _banded_bwd(qkvb, do, lse, delta, dil, bias):
    s = qkvb.shape[0]
    n = s // dil
    t = min(n, BAND_TILE)
    nsub, nt = t // BAND_SUB, n // t
    view = qkvb.reshape(n, dil * QKV_W)
    side = [a.reshape(n, dil * WIDTH_AB) for a in (do, lse, delta)]

    def body(left, main, right, b_ref, do_ref, lse_ref, dl_ref, dq_ref, dk_ref, dv_ref):
        i = pl.program_id(1)

        @pl.when(i == 0)
        def _():
            dk_ref[...] = jnp.zeros(dk_ref.shape, F32)
            dv_ref[...] = jnp.zeros(dv_ref.shape, F32)

        lrow = pl.multiple_of(jnp.maximum(i * t - BAND_HALF, 0), BAND_HALF)
        rrow = pl.multiple_of(jnp.minimum((i + 1) * t, n - BAND_HALF), BAND_HALF)
        mrow = pl.multiple_of(i * t, BAND_HALF)
        for p in range(3):
            pc = slice(p * LANE, (p + 1) * LANE)
            kk, vv = _band_kv(left, main, right, p)
            parts = []
            for a in range(nsub):
                rows, win = slice(a * BAND_SUB, (a + 1) * BAND_SUB), slice(a * BAND_SUB, (a + 2) * BAND_SUB)
                dq, dk, dv, _ = _softmax_pair_bwd(main[rows, pc], kk[win], vv[win], do_ref[rows, pc], lse_ref[rows, pc],
                                                  dl_ref[rows, pc], _band_bias(b_ref, a, nsub, i, nt))
                dq_ref[rows, pc] = dq
                parts.append((dk, dv))
            for which, ref in ((0, dk_ref), (1, dv_ref)):
                chunks = []
                for c in range(nsub + 1):
                    g = parts[c][which][:BAND_SUB] if c < nsub else None
                    if c >= 1:
                        h = parts[c - 1][which][BAND_SUB:]
                        g = h if g is None else g + h
                    chunks.append(g)
                mid = jnp.concatenate([chunks[0][BAND_HALF:]] + chunks[1:nsub] + [chunks[nsub][:BAND_HALF]], axis=0)
                ref[pl.ds(lrow, BAND_HALF), pc] += chunks[0][:BAND_HALF]
                ref[pl.ds(mrow, t), pc] += mid
                ref[pl.ds(rrow, BAND_HALF), pc] += chunks[nsub][BAND_HALF:]

    q_spec = pl.BlockSpec((t, WIDTH_AB), lambda r, i: (i, r))
    acc_spec = pl.BlockSpec((n, WIDTH_AB), lambda r, i: (0, r))
    shp = _sds((n, dil * WIDTH_AB), F32)
    outs = pl.pallas_call(
        body, name=f"banded_bwd_d{dil}", grid=(dil, nt), in_specs=_band_specs(t, n) + [q_spec, q_spec, q_spec],
        out_specs=(q_spec, acc_spec, acc_spec), out_shape=(shp, shp, shp),
        compiler_params=_cp("parallel", "arbitrary"))(view, view, view, bias, *side)
    return [a.reshape(s, WIDTH_AB) for a in outs]


def _na_geometry(s):
    rows = s // GRID_W
    assert rows >= 2 * NA_ROWS and rows % NA_ROWS == 0
    return rows, rows // NA_ROWS


def _na_row(n, i, rows):
    rq = n * NA_ROWS + i
    rs = jnp.clip(rq - NA_ROWS // 2, 0, rows - NA_ROWS)
    return pl.multiple_of(rs * GRID_W, GRID_W), rs - rq + NA_ROWS - 1


NA_KEYS = NA_ROWS * GRID_W


def _natten_fwd(qkvc, tfull):
    s = qkvc.shape[0]
    rows, nrb = _na_geometry(s)
    tq = NA_ROWS * GRID_W

    def body(q_ref, k_ref, v_ref, t_ref, o_ref, lse_ref):
        n = pl.program_id(1)
        for i in range(NA_ROWS):
            tok, base = _na_row(n, i, rows)
            kk, vv = k_ref[pl.ds(tok, NA_KEYS), :], v_ref[pl.ds(tok, NA_KEYS), :]
            sl = slice(i * GRID_W, (i + 1) * GRID_W)
            bias2 = jnp.concatenate([t_ref[0, base], t_ref[1, base]], axis=0)
            o, lse = _softmax_pair(q_ref[sl, :], kk, vv, bias2)
            o_ref[sl, :] = o
            lse_ref[sl, :] = lse

    o_spec = pl.BlockSpec((tq, LANE), lambda p, n: (n, p))
    return pl.pallas_call(
        body, name="natten_fwd", grid=(2, nrb),
        in_specs=[pl.BlockSpec((tq, LANE), lambda p, n: (n, p)), pl.BlockSpec((s, LANE), lambda p, n: (0, 2 + p)),
                  pl.BlockSpec((s, LANE), lambda p, n: (0, 4 + p)),
                  pl.BlockSpec((2, NA_ROWS, GRID_W, NA_KEYS), lambda p, n: (p, 0, 0, 0))],
        out_specs=(o_spec, o_spec), out_shape=(_sds((s, WIDTH_C), F32), _sds((s, WIDTH_C), F32)),
        compiler_params=_cp("parallel", "parallel"))(qkvc, qkvc, qkvc, tfull)


def _natten_bwd(qkvc, tfull, do, lse, delta):
    s = qkvc.shape[0]
    rows, nrb = _na_geometry(s)
    tq = NA_ROWS * GRID_W

    def body(q_ref, k_ref, v_ref, t_ref, do_ref, lse_ref, dl_ref, dq_ref, dk_ref, dv_ref, dt_ref):
        n = pl.program_id(1)

        @pl.when(n == 0)
        def _():
            dk_ref[...] = jnp.zeros(dk_ref.shape, F32)
            dv_ref[...] = jnp.zeros(dv_ref.shape, F32)
            dt_ref[...] = jnp.zeros(dt_ref.shape, F32)

        for i in range(NA_ROWS):
            tok, base = _na_row(n, i, rows)
            win = pl.ds(tok, NA_KEYS)
            sl = slice(i * GRID_W, (i + 1) * GRID_W)
            bias2 = jnp.concatenate([t_ref[0, base], t_ref[1, base]], axis=0)
            dq, dk, dv, ds = _softmax_pair_bwd(q_ref[sl, :], k_ref[win, :], v_ref[win, :], do_ref[sl, :], lse_ref[sl, :],
                                               dl_ref[sl, :], bias2)
            dq_ref[sl, :] = dq
            dk_ref[win, :] += dk
            dv_ref[win, :] += dv
            dt_ref[0, base] += ds[:GRID_W]
            dt_ref[1, base] += ds[GRID_W:]

    q_spec = pl.BlockSpec((tq, LANE), lambda p, n: (n, p))
    acc_spec = pl.BlockSpec((s, LANE), lambda p, n: (0, p))
    t_spec = pl.BlockSpec((2, NA_ROWS, GRID_W, NA_KEYS), lambda p, n: (p, 0, 0, 0))
    shp = _sds((s, WIDTH_C), F32)
    return pl.pallas_call(
        body, name="natten_bwd", grid=(2, nrb),
        in_specs=[q_spec, pl.BlockSpec((s, LANE), lambda p, n: (0, 2 + p)), pl.BlockSpec((s, LANE), lambda p, n: (0, 4 + p)),
                  t_spec, q_spec, q_spec, q_spec],
        out_specs=(q_spec, acc_spec, acc_spec, t_spec),
        out_shape=(shp, shp, shp, _sds((HEADS_C, NA_ROWS, GRID_W, NA_KEYS), F32)),
        compiler_params=_cp("parallel", "arbitrary"))(qkvc, qkvc, qkvc, tfull, do, lse, delta)


def _rpb_constants():
    p = np.arange(GRID_W)[:, None]
    qc = np.arange(GRID_W)[None, :]
    dc = np.clip(qc - p, -(NA_COLS - 1), NA_COLS - 1) + NA_COLS - 1
    onehot = (dc.reshape(1, -1) == np.arange(32)[:, None]).astype(np.float32)
    c_start = np.clip(p - NA_COLS // 2, 0, GRID_W - NA_COLS)
    col_ok = ((qc >= c_start) & (qc < c_start + NA_COLS)).reshape(1, -1).astype(np.float32)
    a = np.arange(16)[:, None]
    bj = np.arange(64)[None, :]
    row_sel = ((bj // 8 + bj % 8) == a).astype(np.float32)
    return jnp.asarray(onehot), jnp.asarray(col_ok), jnp.asarray(row_sel)


def _rpb_expand(rpb, onehot, col_ok):
    r2 = jnp.pad(rpb.reshape(HEADS_C * 15, 31), ((0, 4), (0, 1)))

    def body(r_ref, oh_ref, ok_ref, o_ref):
        t = jnp.dot(r_ref[...], oh_ref[...], preferred_element_type=F32, precision=lax.Precision.HIGHEST)
        o_ref[...] = jnp.where(ok_ref[...] > 0.5, t, NEG_INF)

    tm = pl.pallas_call(body, name="rpb_expand", out_shape=_sds((64, GRID_W * GRID_W), F32))(r2, onehot, col_ok)
    tm = tm[:HEADS_C * 15].reshape(HEADS_C, 15, GRID_W, GRID_W)
    tfull = jnp.stack([jnp.concatenate([tm[:, base + j] for j in range(NA_ROWS)], axis=-1) for base in range(NA_ROWS)], axis=1)
    return tfull


def _rpb_grad(dtfull, onehot, row_sel):
    g = dtfull.reshape(HEADS_C, NA_ROWS, GRID_W, NA_ROWS, GRID_W).transpose(0, 1, 3, 2, 4).reshape(HEADS_C, 64, GRID_W * GRID_W)

    def body(g_ref, oh_ref, sel_ref, o_ref):
        for h in range(HEADS_C):
            mid = lax.dot_general(g_ref[h], oh_ref[...], (((1,), (1,)), ((), ())), preferred_element_type=F32,
                                  precision=lax.Precision.HIGHEST)
            o_ref[h] = jnp.dot(sel_ref[...], mid, preferred_element_type=F32, precision=lax.Precision.HIGHEST)

    out = pl.pallas_call(body, name="rpb_grad", out_shape=_sds((HEADS_C, 16, 32), F32))(g, onehot, row_sel)
    return out[:, :15, :31]


def _outnorm_fwd(o_a, branch_o, branch_lse, o_c, ga, gb, gc):
    s = o_a.shape[0]
    ts = _tile(s, (512, 256, 128))

    def body(a_ref, o1, o2, o3, l1, l2, l3, c_ref, ga_ref, gb_ref, gc_ref, ob_ref, lse_ref, o_ref):
        la, lb, lc = l1[...], l2[...], l3[...]
        m = jnp.maximum(jnp.maximum(la, lb), lc)
        ea, eb, ec = jnp.exp(la - m), jnp.exp(lb - m), jnp.exp(lc - m)
        den = ea + eb + ec
        o_b = (o1[...] * ea + o2[...] * eb + o3[...] * ec) / den
        ob_ref[...] = o_b
        lse_ref[...] = m + jnp.log(den)
        col = 0
        for x, g in ((a_ref[...], ga_ref), (o_b, gb_ref), (c_ref[...], gc_ref)):
            o_ref[:, col:col + x.shape[1]] = (x * _rstd(x) * g[...]).astype(BF16)
            col += x.shape[1]

    sp = _row_spec(ts, WIDTH_AB)
    return pl.pallas_call(
        body, name="outnorm_fwd", grid=(s // ts,),
        in_specs=[sp] * 7 + [_row_spec(ts, WIDTH_C), _fix_spec(WIDTH_AB), _fix_spec(WIDTH_AB), _fix_spec(WIDTH_C)],
        out_specs=(sp, sp, _row_spec(ts, D_MODEL)),
        out_shape=(_sds((s, WIDTH_AB), F32), _sds((s, WIDTH_AB), F32), _sds((s, D_MODEL), BF16)),
        compiler_params=_cp("parallel"))(o_a, *branch_o, *branch_lse, o_c, ga.reshape(1, -1), gb.reshape(1, -1),
                                         gc.reshape(1, -1))


def _outnorm_bwd(dxb, w_out, o_a, o_b, o_c, ga, gb, gc):
    s = o_a.shape[0]
    ts = _tile(s, (512, 256, 128))

    def body(dx_ref, w_ref, a_ref, b_ref, c_ref, ga_ref, gb_ref, gc_ref, *outs):
        first = pl.program_id(0) == 0
        dm = lax.dot_general(dx_ref[...], w_ref[...], (((1,), (1,)), ((), ())), preferred_element_type=F32)
        col = 0
        for k, (ref, g) in enumerate(((a_ref, ga_ref), (b_ref, gb_ref), (c_ref, gc_ref))):
            x = ref[...]
            w = x.shape[1]
            dx, dg = _rms_bwd_rows(x, g[...], dm[:, col:col + w])
            col += w
            outs[k][...] = dx.astype(BF16)
            for b, blk in enumerate(_group_sum(dx * x)):
                if k == 0:
                    outs[3][b] = _stat_rows(blk)
                else:
                    outs[3 + k][:, b * LANE:(b + 1) * LANE] = blk
            _accum(outs[6 + k], dg, first)

    widths = (WIDTH_AB, WIDTH_AB, WIDTH_C)
    return pl.pallas_call(
        body, name="outnorm_bwd", grid=(s // ts,),
        in_specs=[_row_spec(ts, D_MODEL), pl.BlockSpec(w_out.shape, lambda i: (0, 0))] + [_row_spec(ts, w) for w in widths]
        + [_fix_spec(w) for w in widths],
        out_specs=tuple([_row_spec(ts, w) for w in widths] + [pl.BlockSpec((3, 8, ts), lambda i: (0, 0, i))]
                        + [_row_spec(ts, w) for w in widths[1:]] + [_fix_spec(w) for w in widths]),
        out_shape=tuple([_sds((s, w), BF16) for w in widths] + [_sds((3, 8, s), F32)]
                        + [_sds((s, w), F32) for w in widths[1:]] + [_sds((1, w), F32) for w in widths]),
        compiler_params=_cp("arbitrary"))(dxb, w_out, o_a, o_b, o_c, ga.reshape(1, -1), gb.reshape(1, -1),
                                          gc.reshape(1, -1))


def _adamw(w, g, m, v, *, name):
    r, c = w.shape
    tr = _tile(r, (512, 256, 128, 64, 8))

    def body(w_ref, g_ref, m_ref, v_ref, d_ref, nm_ref, nv_ref):
        gv = g_ref[...]
        nm = ADAM_B1 * m_ref[...] + (1.0 - ADAM_B1) * gv
        nv = ADAM_B2 * v_ref[...] + (1.0 - ADAM_B2) * jnp.square(gv)
        m_hat = nm / (1.0 - ADAM_B1 ** ADAM_STEP)
        v_hat = nv / (1.0 - ADAM_B2 ** ADAM_STEP)
        d_ref[...] = -ADAM_LR * (m_hat / (jnp.sqrt(v_hat) + ADAM_EPS) + ADAM_WD * w_ref[...])
        nm_ref[...] = nm
        nv_ref[...] = nv

    sp = _row_spec(tr, c)
    return pl.pallas_call(
        body, name=name, grid=(r // tr,), in_specs=[sp] * 4, out_specs=(sp, sp, sp),
        out_shape=(_sds((r, c), F32),) * 3, compiler_params=_cp("parallel"))(w, g, m, v)


def _add_n(parts, *, name, out_dtype):
    r, c = parts[0].shape
    tr = max(t for t in range(16, 1025, 16) if r % t == 0)

    def body(*refs):
        acc = refs[0][...].astype(F32)
        for ref in refs[1:-1]:
            acc = acc + ref[...].astype(F32)
        refs[-1][...] = acc.astype(out_dtype)

    sp = _row_spec(tr, c)
    return pl.pallas_call(
        body, name=name, grid=(r // tr,), in_specs=[sp] * len(parts), out_specs=sp, out_shape=_sds((r, c), out_dtype),
        compiler_params=_cp("parallel"))(*parts)


ANY = pl.BlockSpec(memory_space=pl.ANY)
CHIP_FLIPS = ((1, 0), (0, 1), (1, 1))


def _me():
    return lax.axis_index("x"), lax.axis_index("y"), lax.axis_index("c")


def _gather_chips(half):
    def body(src, out, send_sems, recv_sems):
        x, y, c = _me()
        me, mine = (x, y, c), 2 * x + y
        chip_x, chip_y, chip_d = 2 * (1 - x) + y, 2 * x + (1 - y), 2 * (1 - x) + (1 - y)
        passed_chip = 2 * (x ^ (1 - c)) + (y ^ c)
        pass_to = (x ^ c, y ^ (1 - c), c)

        def copy(k, chip, half_idx, to, source=None):
            dst = out.at[chip, half_idx]
            return pltpu.make_async_remote_copy(src_ref=dst if source is None else source, dst_ref=dst,
                                                send_sem=send_sems.at[k], recv_sem=recv_sems.at[k], device_id=to,
                                                device_id_type=MESH_T)

        sends = [copy(0, mine, c, (1 - x, y, c), source=src), copy(1, mine, c, (x, 1 - y, c), source=src)]
        for cp in sends:
            cp.start()
        copy(0, chip_x, c, me).wait_recv()
        copy(1, chip_y, c, me).wait_recv()
        sends += [copy(2, passed_chip, c, pass_to), copy(3, chip_x, c, (x, y, 1 - c)), copy(4, chip_y, c, (x, y, 1 - c))]
        for cp in sends[2:]:
            cp.start()
        copy(2, chip_d, c, me).wait_recv()
        sends.append(copy(5, chip_d, c, (x, y, 1 - c)))
        sends[-1].start()
        for k, chip in ((3, chip_x), (4, chip_y), (5, chip_d)):
            copy(k, chip, 1 - c, me).wait_recv()
        for cp in sends:
            cp.wait_send()

    return pl.pallas_call(
        body, name="gather_chips", in_specs=[ANY], out_specs=ANY, out_shape=_sds((4, 2) + half.shape, half.dtype),
        scratch_shapes=[pltpu.SemaphoreType.DMA((6,)), pltpu.SemaphoreType.DMA((6,))])(half)


def _swap_sibling(block):
    def body(src, out, send_sem, recv_sem):
        x, y, c = _me()
        cp = pltpu.make_async_remote_copy(src_ref=src, dst_ref=out, send_sem=send_sem, recv_sem=recv_sem,
                                          device_id=(x, y, 1 - c), device_id_type=MESH_T)
        cp.start()
        cp.wait()

    return pl.pallas_call(
        body, name="swap_sibling", in_specs=[ANY], out_specs=ANY, out_shape=_sds(block.shape, block.dtype),
        scratch_shapes=[pltpu.SemaphoreType.DMA(()), pltpu.SemaphoreType.DMA(())])(block)


def _swap_other_half(halves):
    def body(src, out, send_sem, recv_sem):
        x, y, c = _me()
        cp = pltpu.make_async_remote_copy(src_ref=src.at[:, 1 - c], dst_ref=out, send_sem=send_sem, recv_sem=recv_sem,
                                          device_id=(x, y, 1 - c), device_id_type=MESH_T)
        cp.start()
        cp.wait()

    shape = (halves.shape[0],) + halves.shape[2:]
    return pl.pallas_call(
        body, name="swap_other_half", in_specs=[ANY], out_specs=ANY, out_shape=_sds(shape, halves.dtype),
        scratch_shapes=[pltpu.SemaphoreType.DMA(()), pltpu.SemaphoreType.DMA(())])(halves)


def _pair_sum(halves, core, other):
    n, _, h, c = halves.shape
    tr = max(t for t in range(16, 1025, 16) if h % t == 0)

    def body(core_ref, a_ref, b_ref, o_ref):
        o_ref[...] = (a_ref[...] + b_ref[...]).astype(BF16)

    grid_spec = pltpu.PrefetchScalarGridSpec(
        num_scalar_prefetch=1, grid=(n, h // tr),
        in_specs=[pl.BlockSpec((None, None, tr, c), lambda j, i, core_ref: (j, core_ref[0], i, 0)),
                  pl.BlockSpec((None, tr, c), lambda j, i, core_ref: (j, i, 0))],
        out_specs=pl.BlockSpec((None, tr, c), lambda j, i, core_ref: (j, i, 0)))
    return pl.pallas_call(
        body, name="pair_sum", grid_spec=grid_spec, out_shape=_sds((n, h, c), BF16),
        compiler_params=_cp("parallel", "parallel"))(core.reshape(1).astype(jnp.int32), halves, other)


def _scatter_chips(parts):
    def body(src, out, send_sems, recv_sems):
        x, y, c = _me()
        mine = 2 * x + y
        sends = []
        for k, (fx, fy) in enumerate(CHIP_FLIPS):
            theirs = 2 * (x ^ fx) + (y ^ fy)
            cp = pltpu.make_async_remote_copy(src_ref=src.at[theirs], dst_ref=out.at[mine], send_sem=send_sems.at[k],
                                              recv_sem=recv_sems.at[k], device_id=(x ^ fx, y ^ fy, c), device_id_type=MESH_T)
            cp.start()
            sends.append(cp)
        for k, (fx, fy) in enumerate(CHIP_FLIPS):
            theirs = 2 * (x ^ fx) + (y ^ fy)
            pltpu.make_async_remote_copy(src_ref=src.at[theirs], dst_ref=out.at[theirs], send_sem=send_sems.at[k],
                                         recv_sem=recv_sems.at[k], device_id=(x ^ fx, y ^ fy, c),
                                         device_id_type=MESH_T).wait_recv()
        for cp in sends:
            cp.wait_send()

    return pl.pallas_call(
        body, name="scatter_chips", in_specs=[ANY], out_specs=ANY, out_shape=_sds(parts.shape, parts.dtype),
        scratch_shapes=[pltpu.SemaphoreType.DMA((3,)), pltpu.SemaphoreType.DMA((3,))])(parts)


def _all_reduce_small(block):
    r, c = block.shape

    def body(src, out, slots, send_sems, recv_sems):
        x, y, cc = _me()
        mine = 4 * x + 2 * y + cc
        slots[mine] = src[...]
        sends = []
        for k in range(1, 8):
            fx, fy, fc = (k >> 2) & 1, (k >> 1) & 1, k & 1
            cp = pltpu.make_async_remote_copy(src_ref=src, dst_ref=slots.at[mine], send_sem=send_sems.at[k - 1],
                                              recv_sem=recv_sems.at[k - 1], device_id=(x ^ fx, y ^ fy, cc ^ fc),
                                              device_id_type=MESH_T)
            cp.start()
            sends.append(cp)
        for k in range(1, 8):
            fx, fy, fc = (k >> 2) & 1, (k >> 1) & 1, k & 1
            theirs = 4 * (x ^ fx) + 2 * (y ^ fy) + (cc ^ fc)
            pltpu.make_async_remote_copy(src_ref=src, dst_ref=slots.at[theirs], send_sem=send_sems.at[k - 1],
                                         recv_sem=recv_sems.at[k - 1], device_id=(x ^ fx, y ^ fy, cc ^ fc),
                                         device_id_type=MESH_T).wait_recv()
        for cp in sends:
            cp.wait_send()
        acc = slots[0]
        for d in range(1, 8):
            acc = acc + slots[d]
        out[...] = acc

    vm = pl.BlockSpec(memory_space=pltpu.VMEM)
    return pl.pallas_call(
        body, name="all_reduce_small", in_specs=[vm], out_specs=vm, out_shape=_sds((r, c), F32),
        scratch_shapes=[pltpu.VMEM((8, r, c), F32), pltpu.SemaphoreType.DMA((7,)), pltpu.SemaphoreType.DMA((7,))])(block)


DIRECT = ("w_mlp_in", "w_mlp_out", "w_out")
BIG = DIRECT + ("w_in", "w_uq", "w_ukv")
COL_SHARDED = {"w_in": True, "w_uq": True, "w_ukv": True, "w_out": False, "w_mlp_in": True, "w_mlp_out": False}
SMALL = ("g_mix", "q_norm", "kv_norm", "rpb", "out_norm_a", "out_norm_b", "out_norm_c", "g_mlp", "g_final")
PACK_C = 1024
ROW_ALIGN = 32


def _pack_rows(parts):
    flat = jnp.concatenate([p.reshape(-1, PACK_C) for p in parts], axis=0)
    return jnp.pad(flat, ((0, -flat.shape[0] % ROW_ALIGN), (0, 0)))


def _unpack_rows(flat, shapes):
    out, row = [], 0
    for shp in shapes:
        n = int(np.prod(shp)) // PACK_C
        out.append(flat[row:row + n].reshape(shp))
        row += n
    return out


def _full_from_shards(name, g):
    if COL_SHARDED[name]:
        return g.transpose(1, 2, 0, 3).reshape(g.shape[1], g.shape[2], 4 * g.shape[3])
    return g.transpose(1, 0, 2, 3).reshape(g.shape[1], 4 * g.shape[2], g.shape[3])


def _shards_from_full(name, w):
    l, k, n = w.shape
    if COL_SHARDED[name]:
        return w.reshape(l, k, 4, n // 4).transpose(2, 0, 1, 3)
    return w.reshape(l, 4, k // 4, n).transpose(1, 0, 2, 3)


def _arrange_w_in(w):
    z = jnp.zeros(w.shape[:-1] + (COL_B - COL_KPE - QK_ROPE,), w.dtype)
    return jnp.concatenate([w[..., :COL_KPE + QK_ROPE], z, w[..., COL_KPE + QK_ROPE:]], axis=-1)


def _unarrange_w_in(w):
    return jnp.concatenate([w[..., :COL_KPE + QK_ROPE], w[..., COL_B:]], axis=-1)


def _arrange_w_uq(w):
    per = HEAD_DIM + QK_ROPE
    z = jnp.zeros(w.shape[:-1] + (LANE - per,), w.dtype)
    cols = []
    for h in range(HEADS_A):
        cols += [w[..., h * per:(h + 1) * per], z]
    return jnp.concatenate(cols, axis=-1)


def _unarrange_w_uq(w):
    per = HEAD_DIM + QK_ROPE
    return jnp.concatenate([w[..., h * LANE:h * LANE + per] for h in range(HEADS_A)], axis=-1)


def _arrange_w_ukv(w):
    z = jnp.zeros(w.shape[:-1] + (HEAD_DIM,), w.dtype)
    ks = []
    for h in range(HEADS_A):
        ks += [w[..., h * LANE:h * LANE + HEAD_DIM], z]
    vs = [w[..., h * LANE + HEAD_DIM:(h + 1) * LANE] for h in range(HEADS_A)]
    return jnp.concatenate(ks + vs, axis=-1)


def _unarrange_w_ukv(w):
    cols = []
    for h in range(HEADS_A):
        cols += [w[..., h * LANE:h * LANE + HEAD_DIM], w[..., W_A2 + h * HEAD_DIM:W_A2 + (h + 1) * HEAD_DIM]]
    return jnp.concatenate(cols, axis=-1)


def _layer_fwd(x, w, sm, tabs, consts):
    t32, t64 = tabs
    onehot, col_ok, _, band = consts
    h, proj, cqn, ckvn, kpe, qkvb, qkvc = _in_proj(x, sm["g_mix"], w["w_in"], sm["q_norm"], sm["kv_norm"], t32, t64)
    qa2, ka2, kat, va1 = _a_post_fwd(cqn, ckvn, w["w_uq"], w["w_ukv"], kpe, t32)
    o_a, lse_a = _dense_fwd(qa2, ka2, va1)
    branch = [_banded_fwd(qkvb, dil, band) for _, dil in DILATED_PAIRS]
    tfull = _rpb_expand(sm["rpb"], onehot, col_ok)
    o_c, lse_c = _natten_fwd(qkvc, tfull)
    o_b, lse_b, mixed = _outnorm_fwd(o_a, [b[0] for b in branch], [b[1] for b in branch], o_c, sm["out_norm_a"],
                                     sm["out_norm_b"], sm["out_norm_c"])
    x_mid = _mm_nn(mixed, w["w_out"], name="out_proj", res=x)
    h2, act = _mlp_in(x_mid, sm["g_mlp"], w["w_mlp_in"])
    x_out = _mm_nn(act, w["w_mlp_out"], name="mlp_out", res=x_mid)
    saved = dict(x=x, h=h, proj=proj, cqn=cqn, ckvn=ckvn, qkvb=qkvb, qkvc=qkvc, qa2=qa2, ka2=ka2, kat=kat, va1=va1, o_a=o_a,
                 lse_a=lse_a, o_b=o_b, lse_b=lse_b, o_c=o_c, lse_c=lse_c, tfull=tfull, mixed=mixed, x_mid=x_mid, h2=h2,
                 act=act)
    return x_out, saved


def _layer_bwd(dx, dxb, sv, w, sm, tabs, consts, packed, places):
    t32, t64 = tabs
    onehot, _, row_sel, band = consts
    g = {}
    du = _mm_nt(dxb, w["w_mlp_out"], name="mlp_out_dx", out_dtype=BF16, relu2_act=sv["act"])
    packed = _mm_tn(sv["act"], dxb, name="mlp_out_dw", packed=(packed,) + places["w_mlp_out"])
    dx_mid, dmb, g["g_mlp"] = _mm_nt_norm_bwd(du, w["w_mlp_in"], sv["x_mid"], sm["g_mlp"], dx, name="mlp_in_dx")
    packed = _mm_tn(sv["h2"], du, name="mlp_in_dw", packed=(packed,) + places["w_mlp_in"])
    packed = _mm_tn(sv["mixed"], dmb, name="out_proj_dw", packed=(packed,) + places["w_out"])
    (do_a, do_b, do_c, dl_a, dl_b, dl_c, g["out_norm_a"], g["out_norm_b"], g["out_norm_c"]) = _outnorm_bwd(
        dmb, w["w_out"], sv["o_a"], sv["o_b"], sv["o_c"], sm["out_norm_a"], sm["out_norm_b"], sm["out_norm_c"])
    dqa2_t, dka2, dva = _dense_bwd(sv["qa2"], sv["ka2"], sv["kat"], sv["va1"], do_a, sv["lse_a"], dl_a)
    db = []
    for _, dil in DILATED_PAIRS:
        db += _banded_bwd(sv["qkvb"], do_b, sv["lse_b"], dl_b, dil, band)
    dq_c, dk_c, dv_c, dtfull = _natten_bwd(sv["qkvc"], sv["tfull"], do_c, sv["lse_c"], dl_c)
    g["rpb"] = _rpb_grad(dtfull, onehot, row_sel)
    dqa, dkva, dkpe = _a_post_bwd(dqa2_t, dka2, dva, t32)
    dcqn = _mm_nt(dqa, w["w_uq"], name="q_up_dx")
    g["w_uq"] = _unarrange_w_uq(_mm_tn(sv["cqn"], dqa, name="q_up_dw"))
    dckvn = _mm_nt(dkva, w["w_ukv"], name="kv_up_dx")
    g["w_ukv"] = _unarrange_w_ukv(_mm_tn(sv["ckvn"], dkva, name="kv_up_dw"))
    dproj, g["q_norm"], g["kv_norm"] = _prep_bwd(sv["proj"], sm["q_norm"], sm["kv_norm"], t32, t64, dcqn, dckvn, dkpe,
                                                  db, (dq_c, dk_c, dv_c))
    g["w_in"] = _unarrange_w_in(_mm_tn(sv["h"], dproj, name="in_proj_dw"))
    dx_in, dxb_in, g["g_mix"] = _mm_nt_norm_bwd(dproj, w["w_in"], sv["x"], sm["g_mix"], dx_mid, name="in_proj_dx")
    return dx_in, dxb_in, g, packed


def _packed_places(offs, l):
    d, r = D_MODEL, D_MODEL // 4
    return {"w_mlp_in": (512, lambda i, j: (j, (offs["w_mlp_in"] + l * d) // 512 + i)),
            "w_mlp_out": (512, lambda i, j: (i // 2, (offs["w_mlp_out"] + l * d) // 512 + i % 2)),
            "w_out": (r, lambda i, j: (i, (offs["w_out"] + l * r) // r))}


def _local_step(x, target, wfull, small, packed_shape, offs):
    s = x.shape[0]
    tabs = (_rope_tables(s, QK_ROPE // 2, 1, lead=HEAD_DIM), _rope_tables(s, HEAD_DIM // 2, 2))
    consts = _rpb_constants() + (_band_bias_table(),)
    saved = []
    for l in range(DEPTH):
        wl = {k: v[l] for k, v in wfull.items()}
        sl = {k: small[k][l] for k in SMALL if k != "g_final"}
        x, sv = _layer_fwd(x, wl, sl, tabs, consts)
        saved.append(sv)
    loss, dx, dxb, dg_final = _loss_head(x, small["g_final"], target)
    grads = [None] * DEPTH
    packed = packed_shape
    for l in reversed(range(DEPTH)):
        wl = {k: v[l] for k, v in wfull.items()}
        sl = {k: small[k][l] for k in SMALL if k != "g_final"}
        dx, dxb, grads[l], packed = _layer_bwd(dx, dxb, saved[l], wl, sl, tabs, consts, packed, _packed_places(offs, l))
    return loss, dx, grads, dg_final, packed


ARRANGE = {"w_in": _arrange_w_in, "w_uq": _arrange_w_uq, "w_ukv": _arrange_w_ukv}


def kernel(x, g_mix, w_in, q_norm, w_uq, kv_norm, w_ukv, rpb, out_norm_a, out_norm_b, out_norm_c, w_out, g_mlp, w_mlp_in, w_mlp_out, g_final, loss_target, m_g_mix, m_w_in, m_q_norm, m_w_uq, m_kv_norm, m_w_ukv, m_rpb, m_out_norm_a, m_out_norm_b, m_out_norm_c, m_w_out, m_g_mlp, m_w_mlp_in, m_w_mlp_out, m_g_final, v_g_mix, v_w_in, v_q_norm, v_w_uq, v_kv_norm, v_w_ukv, v_rpb, v_out_norm_a, v_out_norm_b, v_out_norm_c, v_w_out, v_g_mlp, v_w_mlp_in, v_w_mlp_out, v_g_final):
    args = dict(locals())
    weights = {k: args[k] for k in BIG + SMALL}
    moms = {k: args["m_" + k] for k in BIG + SMALL}
    vels = {k: args["v_" + k] for k in BIG + SMALL}
    cc = lax.axis_index("c")
    my_chip = 2 * lax.axis_index("x") + lax.axis_index("y")

    shard_shapes = [weights[k].shape for k in BIG]
    packed_w = _pack_rows([weights[k].astype(BF16) for k in BIG])
    rows = packed_w.shape[0]
    my_half = lax.dynamic_index_in_dim(packed_w.reshape(2, rows // 2, PACK_C), cc, axis=0, keepdims=False)
    gathered = _gather_chips(my_half).reshape(4, rows, PACK_C)
    per_chip = [_unpack_rows(jnp.where(my_chip == j, packed_w, gathered[j]), shard_shapes) for j in range(4)]
    wfull = {}
    for idx, k in enumerate(BIG):
        full = _full_from_shards(k, jnp.stack([per_chip[j][idx] for j in range(4)]))
        wfull[k] = ARRANGE[k](full) if k in ARRANGE else full

    small = {k: weights[k] for k in SMALL}
    offs, row = {}, 0
    for k, shp in zip(BIG, shard_shapes):
        offs[k] = row
        row += int(np.prod(shp)) // PACK_C
    loss, dx, grads, dg_final, packed = _local_step(x[0], loss_target[0], wfull, small, _sds((4, rows, PACK_C), F32), offs)

    small_local = {k: jnp.stack([grads[l][k].reshape(weights[k].shape[1:]) for l in range(DEPTH)])
                   for k in SMALL if k != "g_final"}
    small_local["g_final"] = dg_final.reshape(-1)
    small_shapes = [weights[k].shape for k in SMALL]
    n_small = sum(int(np.prod(s)) for s in small_shapes)
    flat = jnp.concatenate([small_local[k].reshape(-1) for k in SMALL] + [loss[0, :1]])
    rows_small = -(-(n_small + 1) // PACK_C)
    rows_small += -rows_small % 8
    flat = jnp.pad(flat, (0, rows_small * PACK_C - n_small - 1)).reshape(rows_small, PACK_C)
    red = _all_reduce_small(flat).reshape(-1)
    loss_out = red[n_small]
    small_grads, off = {}, 0
    for k, shp in zip(SMALL, small_shapes):
        n = int(np.prod(shp))
        small_grads[k] = red[off:off + n].reshape(shp)
        off += n

    rest = [k for k in BIG if k not in DIRECT]
    by_shard = {k: _shards_from_full(k, jnp.stack([grads[l][k] for l in range(DEPTH)])) for k in rest}
    tail = jnp.stack([_pack_rows([by_shard[k][j] for k in rest]) for j in range(4)])
    assert offs[rest[0]] + tail.shape[1] == rows
    packed = lax.dynamic_update_slice(packed, tail, (0, offs[rest[0]], 0))
    halves = packed.reshape(4, 2, rows // 2, PACK_C)
    pair = _pair_sum(halves, cc, _swap_other_half(halves))
    by_chip = _scatter_chips(pair)
    reduced = _add_n([jnp.where(my_chip == j, pair[j], by_chip[j]) for j in range(4)], name="chip_sum", out_dtype=F32)
    theirs = _swap_sibling(reduced)
    joined = jnp.where(cc == 0, jnp.concatenate([reduced, theirs]), jnp.concatenate([theirs, reduced]))
    big_grads = dict(zip(BIG, _unpack_rows(joined, shard_shapes)))

    out_g, out_d, out_m, out_v = {}, {}, {}, {}
    for k in BIG:
        shp = weights[k].shape
        two_d = (shp[0] * shp[1], shp[2])
        d, nm, nv = _adamw(weights[k].reshape(two_d), big_grads[k].reshape(two_d), moms[k].reshape(two_d),
                           vels[k].reshape(two_d), name="adamw_" + k)
        out_g[k], out_d[k], out_m[k], out_v[k] = big_grads[k], d.reshape(shp), nm.reshape(shp), nv.reshape(shp)

    def pack_small(tree):
        f = jnp.concatenate([tree[k].reshape(-1) for k in SMALL])
        return jnp.pad(f, (0, rows_small * PACK_C - n_small)).reshape(rows_small, PACK_C)

    d, nm, nv = _adamw(pack_small(small), pack_small(small_grads), pack_small(moms), pack_small(vels), name="adamw_small")
    for tree, flat_out in ((out_d, d), (out_m, nm), (out_v, nv)):
        off = 0
        fo = flat_out.reshape(-1)
        for k, shp in zip(SMALL, small_shapes):
            n = int(np.prod(shp))
            tree[k] = fo[off:off + n].reshape(shp)
            off += n
    out_g.update(small_grads)

    order = ("g_mix", "w_in", "q_norm", "w_uq", "kv_norm", "w_ukv", "rpb", "out_norm_a", "out_norm_b", "out_norm_c", "w_out",
             "g_mlp", "w_mlp_in", "w_mlp_out", "g_final")
    return (loss_out, dx.reshape(x.shape), *[out_g[k] for k in order], *[out_d[k] for k in order],
            *[out_m[k] for k in order], *[out_v[k] for k in order])
```

```python
import math

import numpy as np
import jax
import jax.numpy as jnp
from jax import lax
from jax.experimental import pallas as pl
from jax.experimental.pallas import tpu as pltpu

F32 = jnp.float32
BF16 = jnp.bfloat16

D_MODEL = 1024
HEAD_DIM = 64
Q_LORA = 256
KV_LORA = 128
QK_ROPE = 32
HEADS_A = 6
HEADS_B = 6
HEADS_C = 4
DILATED_PAIRS = ((128, 1), (512, 4), (2048, 16))
BAND_HALF = 64
GRID_W = 64
NA_ROWS = 8
NA_COLS = 16
D_FF = 4096
ROPE_THETA = 10000.0
NORM_EPS = 1e-6
NEG_INF = -1e30
DEPTH = 4

LANE = 128
PROJ_W = 2432
COL_CKV = 256
COL_KPE = 384
COL_B = 512
COL_C = 1664
W_A2 = 768
W_KV = W_A2 + 384
WIDTH_AB = 384
WIDTH_C = 256
SCALE_A = (HEAD_DIM + QK_ROPE) ** -0.5
SCALE_BC = HEAD_DIM ** -0.5

ADAM_LR = 0.001
ADAM_B1 = 0.9
ADAM_B2 = 0.999
ADAM_EPS = 1e-08
ADAM_WD = 0.01
ADAM_STEP = 10

VMEM_LIMIT = 56 * 1024 * 1024
MESH_T = pl.DeviceIdType.MESH


def _cp(*sem):
    return pltpu.CompilerParams(dimension_semantics=sem or None, vmem_limit_bytes=VMEM_LIMIT)


def _tile(n, cands):
    for c in cands:
        if n % c == 0:
            return c
    return n


def _sds(shape, dtype):
    return jax.ShapeDtypeStruct(shape, dtype)


ROW_TILE_BYTES = 12 * 1024 * 1024


def _row_tiles(row_bytes):
    return tuple(t for t in (2048, 1024, 512, 256, 128) if t * row_bytes <= ROW_TILE_BYTES or t <= 512)


def _mm_nn(a, b, *, name, out_dtype=F32, res=None):
    m, k = a.shape
    n = b.shape[1]
    tn = _tile(n, (1024, 768, 512)) if n % LANE == 0 and n != PROJ_W else n
    tm = _tile(m, _row_tiles(2 * k + tn * (jnp.dtype(out_dtype).itemsize + (4 if res is not None else 0))))

    def body(*refs):
        a_ref, b_ref = refs[0], refs[1]
        o_ref = refs[-1]
        acc = jnp.dot(a_ref[...], b_ref[...], preferred_element_type=F32)
        if res is not None:
            acc = refs[2][...] + acc
        o_ref[...] = acc.astype(o_ref.dtype)

    in_specs = [pl.BlockSpec((tm, k), lambda j, i: (i, 0)), pl.BlockSpec((k, tn), lambda j, i: (0, j))]
    args = [a, b]
    if res is not None:
        in_specs.append(pl.BlockSpec((tm, tn), lambda j, i: (i, j)))
        args.append(res)
    return pl.pallas_call(
        body, name=name, grid=(n // tn, m // tm), in_specs=in_specs,
        out_specs=pl.BlockSpec((tm, tn), lambda j, i: (i, j)), out_shape=_sds((m, n), out_dtype),
        compiler_params=_cp("parallel", "parallel"))(*args)


def _mlp_in(x, g, w):
    m, k = x.shape
    n = w.shape[1]
    tm = _tile(m, (256, 128))

    def body(x_ref, g_ref, w_ref, h_ref, o_ref):
        xv = x_ref[...]
        h = (xv * _rstd(xv) * g_ref[...]).astype(BF16)
        h_ref[...] = h
        u = jnp.dot(h, w_ref[...], preferred_element_type=F32)
        o_ref[...] = jnp.square(jnp.maximum(u, 0.0)).astype(BF16)

    return pl.pallas_call(
        body, name="mlp_in", grid=(m // tm,),
        in_specs=[pl.BlockSpec((tm, k), lambda i: (i, 0)), pl.BlockSpec((1, k), lambda i: (0, 0)),
                  pl.BlockSpec((k, n), lambda i: (0, 0))],
        out_specs=(pl.BlockSpec((tm, k), lambda i: (i, 0)), pl.BlockSpec((tm, n), lambda i: (i, 0))),
        out_shape=(_sds((m, k), BF16), _sds((m, n), BF16)),
        compiler_params=_cp("parallel"))(x, g.reshape(1, k), w)


def _mm_nt(a, b, *, name, out_dtype=F32, relu2_act=None):
    m, c = a.shape
    n = b.shape[0]
    tn = _tile(n, (1024, 512, 256, 128))
    tm = _tile(m, _row_tiles(2 * c + tn * (jnp.dtype(out_dtype).itemsize + (2 if relu2_act is not None else 0))))

    def body(*refs):
        a_ref, b_ref = refs[0], refs[1]
        o_ref = refs[-1]
        acc = lax.dot_general(a_ref[...], b_ref[...], (((1,), (1,)), ((), ())), preferred_element_type=F32)
        if relu2_act is not None:
            acc = acc * (2.0 * jnp.sqrt(refs[2][...].astype(F32)))
        o_ref[...] = acc.astype(o_ref.dtype)

    in_specs = [pl.BlockSpec((tm, c), lambda j, i: (i, 0)), pl.BlockSpec((tn, c), lambda j, i: (j, 0))]
    args = [a, b]
    if relu2_act is not None:
        in_specs.append(pl.BlockSpec((tm, tn), lambda j, i: (i, j)))
        args.append(relu2_act)
    return pl.pallas_call(
        body, name=name, grid=(n // tn, m // tm), in_specs=in_specs,
        out_specs=pl.BlockSpec((tm, tn), lambda j, i: (i, j)), out_shape=_sds((m, n), out_dtype),
        compiler_params=_cp("parallel", "parallel"))(*args)


def _mm_nt_norm_bwd(a, b, x, g, res, *, name):
    m, c = a.shape
    d = b.shape[0]
    tm = _tile(m, (512, 256, 128))

    def body(a_ref, b_ref, x_ref, g_ref, res_ref, dx_ref, dxb_ref, dg_ref):
        dy = lax.dot_general(a_ref[...], b_ref[...], (((1,), (1,)), ((), ())), preferred_element_type=F32)
        dx, dg = _rms_bwd_rows(x_ref[...], g_ref[...], dy)
        dx = res_ref[...] + dx
        dx_ref[...] = dx
        dxb_ref[...] = dx.astype(BF16)
        _accum(dg_ref, dg, pl.program_id(0) == 0)

    row = pl.BlockSpec((tm, d), lambda i: (i, 0))
    fix = pl.BlockSpec((1, d), lambda i: (0, 0))
    return pl.pallas_call(
        body, name=name, grid=(m // tm,),
        in_specs=[pl.BlockSpec((tm, c), lambda i: (i, 0)), pl.BlockSpec((d, c), lambda i: (0, 0)), row, fix, row],
        out_specs=(row, row, fix), out_shape=(_sds((m, d), F32), _sds((m, d), BF16), _sds((1, d), F32)),
        compiler_params=_cp("arbitrary"))(a, b, x, g.reshape(1, d), res)


def _mm_tn(a, b, *, name, packed=None):
    m, ka = a.shape
    nb = b.shape[1]
    tka = _tile(ka, (512, 256, 128)) if packed is None else packed[1]
    tnb = _tile(nb, (1024, 768, 512)) if nb != PROJ_W else nb
    tc = _tile(m, (4096, 2048, 1024, 512, 256, 128) if tnb <= PACK_C else (2048, 1024, 512, 256, 128))

    def body(*refs):
        a_ref, b_ref, o_ref = refs[0], refs[1], refs[-1]
        part = lax.dot_general(a_ref[...], b_ref[...], (((0,), (0,)), ((), ())), preferred_element_type=F32)

        @pl.when(pl.program_id(2) == 0)
        def _():
            o_ref[...] = part

        @pl.when(pl.program_id(2) != 0)
        def _():
            o_ref[...] += part

    in_specs = [pl.BlockSpec((tc, tka), lambda i, j, c: (c, i)), pl.BlockSpec((tc, tnb), lambda i, j, c: (c, j))]
    kwargs = dict(out_specs=pl.BlockSpec((tka, tnb), lambda i, j, c: (i, j)), out_shape=_sds((ka, nb), F32))
    args = [a, b]
    if packed is not None:
        buf, _, place = packed
        assert tnb == PACK_C
        kwargs = dict(out_specs=pl.BlockSpec((None, tka, tnb), lambda i, j, c: place(i, j) + (0,)))
        if isinstance(buf, jax.ShapeDtypeStruct):
            kwargs["out_shape"] = buf
        else:
            kwargs.update(out_shape=_sds(buf.shape, buf.dtype), input_output_aliases={2: 0})
            in_specs.append(pl.BlockSpec(memory_space=pl.ANY))
            args.append(buf)
    return pl.pallas_call(
        body, name=name, grid=(ka // tka, nb // tnb, m // tc), in_specs=in_specs,
        compiler_params=_cp("parallel", "parallel", "arbitrary"), **kwargs)(*args)


def _rstd(x):
    return lax.rsqrt(jnp.mean(x * x, axis=-1, keepdims=True) + NORM_EPS)


def _rms_bwd_rows(x, g, dy):
    r = _rstd(x)
    gy = dy * g
    c = jnp.sum(x * gy, axis=-1, keepdims=True) * (r * r * r) * (1.0 / x.shape[-1])
    return r * gy - x * c, jnp.sum(dy * x * r, axis=0, keepdims=True)


def _accum(ref, part, first):
    @pl.when(first)
    def _():
        ref[...] = part

    @pl.when(jnp.logical_not(first))
    def _():
        ref[...] += part


def _rope(x, c, s1, s2, sh):
    return x * c + pltpu.roll(x, LANE - sh, 1) * s1 + pltpu.roll(x, sh, 1) * s2


def _rope_t(g, c, s1, s2, sh):
    return g * c + pltpu.roll(g * s1, sh, 1) + pltpu.roll(g * s2, LANE - sh, 1)


def _rope_tables(s, half, reps, lead=0):
    pos = jnp.arange(s, dtype=F32)
    inv_freq = ROPE_THETA ** (-jnp.arange(half, dtype=F32) / half)
    ang = pos[:, None] * inv_freq[None, :]
    cos, sin = jnp.cos(ang), jnp.sin(ang)
    zero = jnp.zeros_like(cos)
    ones, lead0 = jnp.ones((s, lead), F32), jnp.zeros((s, lead), F32)
    pad = jnp.zeros((s, LANE - lead - 2 * half * reps), F32)
    c = jnp.concatenate([ones] + [cos, cos] * reps + [pad], axis=1)
    s1 = jnp.concatenate([lead0] + [-sin, zero] * reps + [pad], axis=1)
    s2 = jnp.concatenate([lead0] + [zero, sin] * reps + [pad], axis=1)
    return c, s1, s2


def _lane_lt64(shape):
    return lax.broadcasted_iota(jnp.int32, shape, len(shape) - 1) % LANE < HEAD_DIM


def _group_sum(x):
    outs = []
    for b in range(x.shape[1] // LANE):
        blk = x[:, b * LANE:(b + 1) * LANE]
        lo = _lane_lt64(blk.shape)
        s0 = jnp.sum(jnp.where(lo, blk, 0.0), axis=1, keepdims=True)
        s1 = jnp.sum(jnp.where(lo, 0.0, blk), axis=1, keepdims=True)
        outs.append(jnp.where(lo, s0, s1))
    return outs


def _row_spec(ts, w):
    return pl.BlockSpec((ts, w), lambda i: (i, 0))


def _fix_spec(w):
    return pl.BlockSpec((1, w), lambda i: (0, 0))


def _loss_head(x, g, target):
    s, d = x.shape
    ts = _tile(s, (512, 256, 128))

    def body(x_ref, g_ref, t_ref, loss_ref, dx_ref, dxb_ref, dg_ref):
        xv, gv = x_ref[...], g_ref[...]
        err = xv * _rstd(xv) * gv - t_ref[...]
        part = 0.5 * jnp.sum(jnp.sum(err * err, axis=-1, keepdims=True) * (1.0 / d), axis=0, keepdims=True)
        dx, dg = _rms_bwd_rows(xv, gv, err * (1.0 / d))
        dx_ref[...] = dx
        dxb_ref[...] = dx.astype(BF16)
        first = pl.program_id(0) == 0
        _accum(dg_ref, dg, first)
        _accum(loss_ref, jnp.broadcast_to(part, (1, LANE)), first)

    return pl.pallas_call(
        body, name="loss_head", grid=(s // ts,), in_specs=[_row_spec(ts, d), _fix_spec(d), _row_spec(ts, d)],
        out_specs=(_fix_spec(LANE), _row_spec(ts, d), _row_spec(ts, d), _fix_spec(d)),
        out_shape=(_sds((1, LANE), F32), _sds((s, d), F32), _sds((s, d), BF16), _sds((1, d), F32)),
        compiler_params=_cp("arbitrary"))(x, g.reshape(1, d), target)


def _in_proj(x, g, w, q_norm, kv_norm, t32, t64):
    s, d = x.shape
    ts = _tile(s, (256, 128))

    def body(x_ref, g_ref, w_ref, qn_ref, kn_ref, c32, a32, b32, c64, a64, b64,
             h_ref, p_ref, cqn_ref, ckvn_ref, kpe_ref, qkvb_ref, qkvc_ref):
        xv = x_ref[...]
        h = (xv * _rstd(xv) * g_ref[...]).astype(BF16)
        h_ref[...] = h
        p = jnp.dot(h, w_ref[...], preferred_element_type=F32)
        p_ref[...] = p
        cq = p[:, 0:Q_LORA]
        cqn_ref[...] = (cq * _rstd(cq) * qn_ref[...]).astype(BF16)
        ckv = p[:, COL_CKV:COL_KPE]
        ckvn_ref[...] = (ckv * _rstd(ckv) * kn_ref[...]).astype(BF16)
        kp = pltpu.roll(p[:, COL_KPE:COL_B], HEAD_DIM, 1)
        kpe_ref[...] = _rope(kp, c32[...], a32[...], b32[...], QK_ROPE // 2).astype(BF16)
        for b in range(6):
            blk = _rope(p[:, COL_B + b * LANE:COL_B + (b + 1) * LANE], c64[...], a64[...], b64[...], HEAD_DIM // 2)
            if b < 3:
                blk = blk * SCALE_BC
            qkvb_ref[:, b * LANE:(b + 1) * LANE] = blk.astype(BF16)
        qkvb_ref[:, 2 * WIDTH_AB:3 * WIDTH_AB] = p[:, COL_B + 2 * WIDTH_AB:COL_C].astype(BF16)
        qkvc_ref[:, 0:WIDTH_C] = (p[:, COL_C:COL_C + WIDTH_C] * SCALE_BC).astype(BF16)
        qkvc_ref[:, WIDTH_C:3 * WIDTH_C] = p[:, COL_C + WIDTH_C:PROJ_W].astype(BF16)

    tab = [_row_spec(ts, LANE)] * 6
    widths = (d, PROJ_W, Q_LORA, KV_LORA, LANE, 3 * WIDTH_AB, 3 * WIDTH_C)
    dtypes = (BF16, F32, BF16, BF16, BF16, BF16, BF16)
    return pl.pallas_call(
        body, name="in_proj", grid=(s // ts,),
        in_specs=[_row_spec(ts, d), _fix_spec(d), pl.BlockSpec(w.shape, lambda i: (0, 0)), _fix_spec(Q_LORA),
                  _fix_spec(KV_LORA)] + tab,
        out_specs=tuple(_row_spec(ts, wd) for wd in widths),
        out_shape=tuple(_sds((s, wd), dt) for wd, dt in zip(widths, dtypes)),
        compiler_params=_cp("parallel"))(x, g.reshape(1, d), w, q_norm.reshape(1, -1), kv_norm.reshape(1, -1), *t32, *t64)


def _prep_bwd(proj, q_norm, kv_norm, t32, t64, dcqn, dckvn, dkpe, db, dc):
    s = proj.shape[0]
    ts = _tile(s, (256, 128))

    def body(p_ref, qn_ref, kn_ref, c32, a32, b32, c64, a64, b64, dcqn_ref, dckvn_ref, dkpe_ref, *rest):
        db_refs, dc_refs = rest[0:9], rest[9:12]
        dp_ref, dqn_ref, dkn_ref = rest[12:15]
        first = pl.program_id(0) == 0
        dx, dg = _rms_bwd_rows(p_ref[:, 0:Q_LORA], qn_ref[...], dcqn_ref[...])
        dp_ref[:, 0:Q_LORA] = dx.astype(BF16)
        _accum(dqn_ref, dg, first)
        dx, dg = _rms_bwd_rows(p_ref[:, COL_CKV:COL_KPE], kn_ref[...], dckvn_ref[...])
        dp_ref[:, COL_CKV:COL_KPE] = dx.astype(BF16)
        _accum(dkn_ref, dg, first)
        g = pltpu.roll(_rope_t(dkpe_ref[...], c32[...], a32[...], b32[...], QK_ROPE // 2), LANE - HEAD_DIM, 1)
        lane = lax.broadcasted_iota(jnp.int32, g.shape, 1)
        dp_ref[:, COL_KPE:COL_B] = jnp.where(lane < QK_ROPE, g, 0.0).astype(BF16)
        for which in range(3):
            for b in range(3):
                sl = slice(b * LANE, (b + 1) * LANE)
                g = db_refs[which][:, sl] + db_refs[3 + which][:, sl] + db_refs[6 + which][:, sl]
                if which < 2:
                    g = _rope_t(g, c64[...], a64[...], b64[...], HEAD_DIM // 2)
                if which == 0:
                    g = g * SCALE_BC
                col = COL_B + which * WIDTH_AB + b * LANE
                dp_ref[:, col:col + LANE] = g.astype(BF16)
        dp_ref[:, COL_C:COL_C + WIDTH_C] = (dc_refs[0][...] * SCALE_BC).astype(BF16)
        dp_ref[:, COL_C + WIDTH_C:COL_C + 2 * WIDTH_C] = dc_refs[1][...].astype(BF16)
        dp_ref[:, COL_C + 2 * WIDTH_C:PROJ_W] = dc_refs[2][...].astype(BF16)

    tab = [_row_spec(ts, LANE)] * 6
    in_specs = ([_row_spec(ts, PROJ_W), _fix_spec(Q_LORA), _fix_spec(KV_LORA)] + tab
                + [_row_spec(ts, Q_LORA), _row_spec(ts, KV_LORA), _row_spec(ts, LANE)]
                + [_row_spec(ts, WIDTH_AB)] * 9 + [_row_spec(ts, WIDTH_C)] * 3)
    return pl.pallas_call(
        body, name="prep_bwd", grid=(s // ts,), in_specs=in_specs,
        out_specs=(_row_spec(ts, PROJ_W), _fix_spec(Q_LORA), _fix_spec(KV_LORA)),
        out_shape=(_sds((s, PROJ_W), BF16), _sds((1, Q_LORA), F32), _sds((1, KV_LORA), F32)),
        compiler_params=_cp("arbitrary"))(proj, q_norm.reshape(1, -1), kv_norm.reshape(1, -1), *t32, *t64,
                                          dcqn, dckvn, dkpe, *db, *dc)


def _a_post_fwd(cqn, ckvn, w_uq, w_ukv, kpe, t32):
    s = cqn.shape[0]
    ts = _tile(s, (512, 256, 128))

    def body(cq_ref, ckv_ref, wq_ref, wkv_ref, kpe_ref, c32, a32, b32, q_ref, k_ref, kt_ref, v_ref):
        qa = jnp.dot(cq_ref[...], wq_ref[...], preferred_element_type=F32)
        kva = jnp.dot(ckv_ref[...], wkv_ref[...], preferred_element_type=F32)
        kpe = kpe_ref[...].astype(F32)
        for h in range(HEADS_A):
            hb = slice(h * LANE, (h + 1) * LANE)
            q_ref[:, hb] = _rope(qa[:, hb], c32[...], a32[...], b32[...], QK_ROPE // 2).astype(BF16)
            kh = kva[:, hb] + kpe
            k_ref[:, hb] = kh.astype(BF16)
            kt_ref[hb, :] = kh.T.astype(BF16)
        for p in range(3):
            lo, hi = 2 * p * LANE, (2 * p + 1) * LANE
            v_ref[:, lo:hi] = kva[:, W_A2 + p * LANE:W_A2 + (p + 1) * LANE].astype(BF16)
            v_ref[:, hi:hi + LANE] = jnp.ones((ts, LANE), BF16)

    whole = lambda shape: pl.BlockSpec(shape, lambda i: (0, 0))
    return pl.pallas_call(
        body, name="a_post_fwd", grid=(s // ts,),
        in_specs=[_row_spec(ts, Q_LORA), _row_spec(ts, KV_LORA), whole(w_uq.shape), whole(w_ukv.shape), _row_spec(ts, LANE)]
        + [_row_spec(ts, LANE)] * 3,
        out_specs=(_row_spec(ts, W_A2), _row_spec(ts, W_A2), pl.BlockSpec((W_A2, ts), lambda i: (0, i)), _row_spec(ts, W_A2)),
        out_shape=(_sds((s, W_A2), BF16), _sds((s, W_A2), BF16), _sds((W_A2, s), BF16), _sds((s, W_A2), BF16)),
        compiler_params=_cp("parallel"))(cqn, ckvn, w_uq, w_ukv, kpe, *t32)


def _a_post_bwd(dqa2_t, dka2, dva, t32):
    s = dka2.shape[0]
    ts = _tile(s, (512, 256, 128))

    def body(dqt_ref, dk_ref, dv_ref, c32, a32, b32, dqa_ref, dkva_ref, dkpe_ref):
        acc = None
        for h in range(HEADS_A):
            hb = slice(h * LANE, (h + 1) * LANE)
            dqa_ref[:, hb] = _rope_t(dqt_ref[hb, :].T * SCALE_A, c32[...], a32[...], b32[...], QK_ROPE // 2).astype(BF16)
            part = dk_ref[:, hb] * SCALE_A
            dkva_ref[:, hb] = part.astype(BF16)
            acc = part if acc is None else acc + part
        dkva_ref[:, W_A2:W_KV] = dv_ref[...].astype(BF16)
        dkpe_ref[...] = acc

    return pl.pallas_call(
        body, name="a_post_bwd", grid=(s // ts,),
        in_specs=[pl.BlockSpec((W_A2, ts), lambda i: (0, i)), _row_spec(ts, W_A2), _row_spec(ts, WIDTH_AB)]
        + [_row_spec(ts, LANE)] * 3,
        out_specs=(_row_spec(ts, W_A2), _row_spec(ts, W_KV), _row_spec(ts, LANE)),
        out_shape=(_sds((s, W_A2), BF16), _sds((s, W_KV), BF16), _sds((s, LANE), F32)),
        compiler_params=_cp("parallel"))(dqa2_t, dka2, dva, *t32)


def _pair_masks():
    lane = lax.broadcasted_iota(jnp.int32, (1, LANE), 1)
    return lane < HEAD_DIM, lane >= HEAD_DIM


def _nt(a, b):
    return lax.dot_general(a, b, (((1,), (1,)), ((), ())), preferred_element_type=F32)


def _tn(a, b):
    return lax.dot_general(a, b, (((0,), (0,)), ((), ())), preferred_element_type=F32)


def _stack_heads(x):
    m0, m1 = _pair_masks()
    zero = jnp.zeros_like(x)
    return jnp.concatenate([jnp.where(m0, x, zero), jnp.where(m1, x, zero)], axis=0)


def _stack_stat(x):
    return jnp.concatenate([x[:, 0:1], x[:, HEAD_DIM:HEAD_DIM + 1]], axis=0)


def _softmax_pair(q, kk, vv, bias2):
    t = q.shape[0]
    s = _nt(_stack_heads(q), kk) + bias2
    m = jnp.max(s, axis=1, keepdims=True)
    p = jnp.exp(s - m)
    l = jnp.sum(p, axis=1, keepdims=True)
    o2 = jnp.dot(p.astype(BF16), vv, preferred_element_type=F32) / l
    lse2 = m + jnp.log(l)
    lo = _lane_lt64((t, LANE))
    return jnp.where(lo, o2[:t], o2[t:]), jnp.where(lo, lse2[:t], lse2[t:])


def _softmax_pair_bwd(q, kk, vv, do, lse, delta, bias2):
    t = q.shape[0]
    q2, do2 = _stack_heads(q), _stack_heads(do)
    p = jnp.exp(_nt(q2, kk) + bias2 - _stack_stat(lse))
    ds = p * (_nt(do2, vv) - _stack_stat(delta))
    dsb = ds.astype(BF16)
    dq2 = jnp.dot(dsb, kk, preferred_element_type=F32)
    lo = _lane_lt64((t, LANE))
    return jnp.where(lo, dq2[:t], dq2[t:]), _tn(dsb, q2), _tn(p.astype(BF16), do2), ds


DENSE_FWD_TQ, DENSE_FWD_TK = 512, 8192
DENSE_BWD_TQ, DENSE_BWD_TK = 2048, 1024
LOG2E = math.log2(math.e)


def _dense_fwd(qa, ka, va1):
    s = qa.shape[0]
    tq, tk = min(DENSE_FWD_TQ, s), min(DENSE_FWD_TK, s)
    nk = s // tk
    c = SCALE_A * LOG2E

    def body(q_ref, k_ref, v_ref, o_ref, lse_ref, m_sc, acc_sc):
        j = pl.program_id(2)

        @pl.when(j == 0)
        def _():
            m_sc[...] = jnp.full(m_sc.shape, NEG_INF, F32)
            acc_sc[...] = jnp.zeros(acc_sc.shape, F32)

        vv = v_ref[...]
        for hh in range(2):
            hs = slice(hh * LANE, (hh + 1) * LANE)
            sc = _nt(q_ref[:, hs], k_ref[:, hs])
            m_prev = m_sc[hh]
            m_new = jnp.maximum(m_prev, jnp.max(sc, axis=1, keepdims=True))
            alpha = jnp.exp2((m_prev - m_new) * c)
            p = jnp.exp2((sc - m_new) * c)
            acc_sc[hh] = alpha * acc_sc[hh] + jnp.dot(p.astype(BF16), vv, preferred_element_type=F32)
            m_sc[hh] = m_new

        @pl.when(j == nk - 1)
        def _():
            lo = _lane_lt64((tq, LANE))
            a0, a1 = acc_sc[0], acc_sc[1]
            l0, l1 = a0[:, LANE:], a1[:, LANE:]
            o_ref[...] = jnp.where(lo, a0[:, :LANE] / l0, a1[:, :LANE] / l1)
            lse_ref[0] = _stat_rows(jnp.where(lo, m_sc[0] * SCALE_A + jnp.log(l0), m_sc[1] * SCALE_A + jnp.log(l1)))

    return pl.pallas_call(
        body, name="dense_fwd", grid=(3, s // tq, nk),
        in_specs=[pl.BlockSpec((tq, 2 * LANE), lambda p, i, j: (i, p)), pl.BlockSpec((tk, 2 * LANE), lambda p, i, j: (j, p)),
                  pl.BlockSpec((tk, 2 * LANE), lambda p, i, j: (j, p))],
        out_specs=(pl.BlockSpec((tq, LANE), lambda p, i, j: (i, p)), pl.BlockSpec((1, 8, tq), lambda p, i, j: (p, 0, i))),
        out_shape=(_sds((s, WIDTH_AB), F32), _sds((3, 8, s), F32)),
        scratch_shapes=[pltpu.VMEM((2, tq, 1), F32), pltpu.VMEM((2, tq, 2 * LANE), F32)],
        compiler_params=_cp("parallel", "parallel", "arbitrary"))(qa, ka, va1)


def _stat_rows(lane_dense):
    tr = lane_dense.T
    return jnp.concatenate([tr[0:1, :], tr[HEAD_DIM:HEAD_DIM + 1, :], jnp.zeros((6, tr.shape[1]), F32)], axis=0)


def _dense_bwd(qa, ka, kat, va1, do, lse_rows, delta_rows):
    s = qa.shape[0]
    tq, tk = min(DENSE_BWD_TQ, s), min(DENSE_BWD_TK, s)
    c = SCALE_A * LOG2E

    def body(q_ref, k_ref, kt_ref, v_ref, do_ref, lse_ref, dl_ref, dqt_ref, dk_ref, dv_ref):
        j, i = pl.program_id(1), pl.program_id(2)

        @pl.when((j == 0) & (i == 0))
        def _():
            dqt_ref[...] = jnp.zeros(dqt_ref.shape, F32)

        vv, do_ = v_ref[...], do_ref[...]
        lse_t, dl_t = lse_ref[0] * LOG2E, dl_ref[0]
        vm = _pair_masks()
        cols = pl.ds(pl.multiple_of(i * tq, tq), tq)
        dv = None
        for hh in range(2):
            hs = slice(hh * LANE, (hh + 1) * LANE)
            qh = q_ref[:, hs]
            dom = jnp.where(vm[hh], do_, jnp.zeros_like(do_))
            dpt = _nt(vv, dom)
            pt = jnp.exp2(_nt(k_ref[:, hs], qh) * c - lse_t[hh:hh + 1, :])
            dst = pt * (dpt - dl_t[hh:hh + 1, :])
            pb, dsb = pt.astype(BF16), dst.astype(BF16)
            dv_h = jnp.dot(pb, dom, preferred_element_type=F32)
            dv = dv_h if dv is None else dv + dv_h
            dk_h = jnp.dot(dsb, qh, preferred_element_type=F32)
            dqt_ref[hs, cols] += jnp.dot(kt_ref[hs, :], dsb, preferred_element_type=F32)

            @pl.when(i == 0)
            def _():
                dk_ref[:, hs] = dk_h

            @pl.when(i != 0)
            def _():
                dk_ref[:, hs] += dk_h
        _accum(dv_ref, dv, i == 0)

    st_spec = pl.BlockSpec((1, 8, tq), lambda p, j, i: (p, 0, i))
    return pl.pallas_call(
        body, name="dense_bwd", grid=(3, s // tk, s // tq),
        in_specs=[pl.BlockSpec((tq, 2 * LANE), lambda p, j, i: (i, p)), pl.BlockSpec((tk, 2 * LANE), lambda p, j, i: (j, p)),
                  pl.BlockSpec((2 * LANE, tk), lambda p, j, i: (p, j)), pl.BlockSpec((tk, LANE), lambda p, j, i: (j, 2 * p)),
                  pl.BlockSpec((tq, LANE), lambda p, j, i: (i, p)), st_spec, st_spec],
        out_specs=(pl.BlockSpec((2 * LANE, s), lambda p, j, i: (p, 0)), pl.BlockSpec((tk, 2 * LANE), lambda p, j, i: (j, p)),
                   pl.BlockSpec((tk, LANE), lambda p, j, i: (j, p))),
        out_shape=(_sds((W_A2, s), F32), _sds((s, W_A2), F32), _sds((s, WIDTH_AB), F32)),
        compiler_params=_cp("parallel", "arbitrary", "arbitrary"))(qa, ka, kat, va1, do, lse_rows, delta_rows)


BAND_TILE = 1024
BAND_SUB = 128
QKV_W = 3 * WIDTH_AB


def _band_bias_table():
    row = np.arange(BAND_SUB)[:, None]
    col = np.arange(2 * BAND_SUB)[None, :]
    band = np.abs(row - col + BAND_HALF) <= BAND_HALF
    variants = []
    for idx in range(4):
        ok = band & ((col >= BAND_HALF) | ((idx & 1) == 0)) & ((col < 2 * BAND_SUB - BAND_HALF) | ((idx & 2) == 0))
        one = np.where(ok, 0.0, NEG_INF).astype(np.float32)
        variants.append(np.concatenate([one, one], axis=0))
    return jnp.asarray(np.stack(variants))


def _band_specs(t, n):
    hpt = t // BAND_HALF
    last = n // BAND_HALF - 1
    return [pl.BlockSpec((BAND_HALF, QKV_W), lambda r, i: (jnp.maximum(i * hpt - 1, 0), r)),
            pl.BlockSpec((t, QKV_W), lambda r, i: (i, r)),
            pl.BlockSpec((BAND_HALF, QKV_W), lambda r, i: (jnp.minimum((i + 1) * hpt, last), r)),
            pl.BlockSpec((4, 2 * BAND_SUB, 2 * BAND_SUB), lambda r, i: (0, 0, 0))]


def _band_bias(b_ref, a, nsub, i, nt):
    idx = 0
    if a == 0:
        idx = idx + (i == 0).astype(jnp.int32)
    if a == nsub - 1:
        idx = idx + 2 * (i == nt - 1).astype(jnp.int32)
    return b_ref[idx]


def _band_kv(left, main, right, p):
    kc = slice(WIDTH_AB + p * LANE, WIDTH_AB + (p + 1) * LANE)
    vc = slice(2 * WIDTH_AB + p * LANE, 2 * WIDTH_AB + (p + 1) * LANE)
    return (jnp.concatenate([left[:, kc], main[:, kc], right[:, kc]], axis=0),
            jnp.concatenate([left[:, vc], main[:, vc], right[:, vc]], axis=0))


def _banded_fwd(qkvb, dil, bias):
    s = qkvb.shape[0]
    n = s // dil
    t = min(n, BAND_TILE)
    nsub, nt = t // BAND_SUB, n // t
    view = qkvb.reshape(n, dil * QKV_W)

    def body(left, main, right, b_ref, o_ref, lse_ref):
        i = pl.program_id(1)
        for p in range(3):
            pc = slice(p * LANE, (p + 1) * LANE)
            kk, vv = _band_kv(left, main, right, p)
            for a in range(nsub):
                rows, win = slice(a * BAND_SUB, (a + 1) * BAND_SUB), slice(a * BAND_SUB, (a + 2) * BAND_SUB)
                o, lse = _softmax_pair(main[rows, pc], kk[win], vv[win], _band_bias(b_ref, a, nsub, i, nt))
                o_ref[rows, pc] = o
                lse_ref[rows, pc] = lse

    o_spec = pl.BlockSpec((t, WIDTH_AB), lambda r, i: (i, r))
    o, lse = pl.pallas_call(
        body, name=f"banded_fwd_d{dil}", grid=(dil, nt), in_specs=_band_specs(t, n), out_specs=(o_spec, o_spec),
        out_shape=(_sds((n, dil * WIDTH_AB), F32), _sds((n, dil * WIDTH_AB), F32)),
        compiler_params=_cp("parallel", "parallel"))(view, view, view, bias)
    return o.reshape(s, WIDTH_AB), lse.reshape(s, WIDTH_AB)


def _banded_bwd(qkvb, do, lse, delta, dil, bias):
    s = qkvb.shape[0]
    n = s // dil
    t = min(n, BAND_TILE)
    nsub, nt = t // BAND_SUB, n // t
    view = qkvb.reshape(n, dil * QKV_W)
    side = [a.reshape(n, dil * WIDTH_AB) for a in (do, lse, delta)]

    def body(left, main, right, b_ref, do_ref, lse_ref, dl_ref, dq_ref, dk_ref, dv_ref):
        i = pl.program_id(1)

        @pl.when(i == 0)
        def _():
            dk_ref[...] = jnp.zeros(dk_ref.shape, F32)
            dv_ref[...] = jnp.zeros(dv_ref.shape, F32)

        lrow = pl.multiple_of(jnp.maximum(i * t - BAND_HALF, 0), BAND_HALF)
        rrow = pl.multiple_of(jnp.minimum((i + 1) * t, n - BAND_HALF), BAND_HALF)
        mrow = pl.multiple_of(i * t, BAND_HALF)
        for p in range(3):
            pc = slice(p * LANE, (p + 1) * LANE)
            kk, vv = _band_kv(left, main, right, p)
            parts = []
            for a in range(nsub):
                rows, win = slice(a * BAND_SUB, (a + 1) * BAND_SUB), slice(a * BAND_SUB, (a + 2) * BAND_SUB)
                dq, dk, dv, _ = _softmax_pair_bwd(main[rows, pc], kk[win], vv[win], do_ref[rows, pc], lse_ref[rows, pc],
                                                  dl_ref[rows, pc], _band_bias(b_ref, a, nsub, i, nt))
                dq_ref[rows, pc] = dq
                parts.append((dk, dv))
            for which, ref in ((0, dk_ref), (1, dv_ref)):
                chunks = []
                for c in range(nsub + 1):
                    g = parts[c][which][:BAND_SUB] if c < nsub else None
                    if c >= 1:
                        h = parts[c - 1][which][BAND_SUB:]
                        g = h if g is None else g + h
                    chunks.append(g)
                mid = jnp.concatenate([chunks[0][BAND_HALF:]] + chunks[1:nsub] + [chunks[nsub][:BAND_HALF]], axis=0)
                ref[pl.ds(lrow, BAND_HALF), pc] += chunks[0][:BAND_HALF]
                ref[pl.ds(mrow, t), pc] += mid
                ref[pl.ds(rrow, BAND_HALF), pc] += chunks[nsub][BAND_HALF:]

    q_spec = pl.BlockSpec((t, WIDTH_AB), lambda r, i: (i, r))
    acc_spec = pl.BlockSpec((n, WIDTH_AB), lambda r, i: (0, r))
    shp = _sds((n, dil * WIDTH_AB), F32)
    outs = pl.pallas_call(
        body, name=f"banded_bwd_d{dil}", grid=(dil, nt), in_specs=_band_specs(t, n) + [q_spec, q_spec, q_spec],
        out_specs=(q_spec, acc_spec, acc_spec), out_shape=(shp, shp, shp),
        compiler_params=_cp("parallel", "arbitrary"))(view, view, view, bias, *side)
    return [a.reshape(s, WIDTH_AB) for a in outs]


def _na_geometry(s):
    rows = s // GRID_W
    assert rows >= 2 * NA_ROWS and rows % NA_ROWS == 0
    return rows, rows // NA_ROWS


def _na_row(n, i, rows):
    rq = n * NA_ROWS + i
    rs = jnp.clip(rq - NA_ROWS // 2, 0, rows - NA_ROWS)
    return pl.multiple_of(rs * GRID_W, GRID_W), rs - rq + NA_ROWS - 1


NA_KEYS = NA_ROWS * GRID_W


def _natten_fwd(qkvc, tfull):
    s = qkvc.shape[0]
    rows, nrb = _na_geometry(s)
    tq = NA_ROWS * GRID_W

    def body(q_ref, k_ref, v_ref, t_ref, o_ref, lse_ref):
        n = pl.program_id(1)
        for i in range(NA_ROWS):
            tok, base = _na_row(n, i, rows)
            kk, vv = k_ref[pl.ds(tok, NA_KEYS), :], v_ref[pl.ds(tok, NA_KEYS), :]
            sl = slice(i * GRID_W, (i + 1) * GRID_W)
            bias2 = jnp.concatenate([t_ref[0, base], t_ref[1, base]], axis=0)
            o, lse = _softmax_pair(q_ref[sl, :], kk, vv, bias2)
            o_ref[sl, :] = o
            lse_ref[sl, :] = lse

    o_spec = pl.BlockSpec((tq, LANE), lambda p, n: (n, p))
    return pl.pallas_call(
        body, name="natten_fwd", grid=(2, nrb),
        in_specs=[pl.BlockSpec((tq, LANE), lambda p, n: (n, p)), pl.BlockSpec((s, LANE), lambda p, n: (0, 2 + p)),
                  pl.BlockSpec((s, LANE), lambda p, n: (0, 4 + p)),
                  pl.BlockSpec((2, NA_ROWS, GRID_W, NA_KEYS), lambda p, n: (p, 0, 0, 0))],
        out_specs=(o_spec, o_spec), out_shape=(_sds((s, WIDTH_C), F32), _sds((s, WIDTH_C), F32)),
        compiler_params=_cp("parallel", "parallel"))(qkvc, qkvc, qkvc, tfull)


def _natten_bwd(qkvc, tfull, do, lse, delta):
    s = qkvc.shape[0]
    rows, nrb = _na_geometry(s)
    tq = NA_ROWS * GRID_W

    def body(q_ref, k_ref, v_ref, t_ref, do_ref, lse_ref, dl_ref, dq_ref, dk_ref, dv_ref, dt_ref):
        n = pl.program_id(1)

        @pl.when(n == 0)
        def _():
            dk_ref[...] = jnp.zeros(dk_ref.shape, F32)
            dv_ref[...] = jnp.zeros(dv_ref.shape, F32)
            dt_ref[...] = jnp.zeros(dt_ref.shape, F32)

        for i in range(NA_ROWS):
            tok, base = _na_row(n, i, rows)
            win = pl.ds(tok, NA_KEYS)
            sl = slice(i * GRID_W, (i + 1) * GRID_W)
            bias2 = jnp.concatenate([t_ref[0, base], t_ref[1, base]], axis=0)
            dq, dk, dv, ds = _softmax_pair_bwd(q_ref[sl, :], k_ref[win, :], v_ref[win, :], do_ref[sl, :], lse_ref[sl, :],
                                               dl_ref[sl, :], bias2)
            dq_ref[sl, :] = dq
            dk_ref[win, :] += dk
            dv_ref[win, :] += dv
            dt_ref[0, base] += ds[:GRID_W]
            dt_ref[1, base] += ds[GRID_W:]

    q_spec = pl.BlockSpec((tq, LANE), lambda p, n: (n, p))
    acc_spec = pl.BlockSpec((s, LANE), lambda p, n: (0, p))
    t_spec = pl.BlockSpec((2, NA_ROWS, GRID_W, NA_KEYS), lambda p, n: (p, 0, 0, 0))
    shp = _sds((s, WIDTH_C), F32)
    return pl.pallas_call(
        body, name="natten_bwd", grid=(2, nrb),
        in_specs=[q_spec, pl.BlockSpec((s, LANE), lambda p, n: (0, 2 + p)), pl.BlockSpec((s, LANE), lambda p, n: (0, 4 + p)),
                  t_spec, q_spec, q_spec, q_spec],
        out_specs=(q_spec, acc_spec, acc_spec, t_spec),
        out_shape=(shp, shp, shp, _sds((HEADS_C, NA_ROWS, GRID_W, NA_KEYS), F32)),
        compiler_params=_cp("parallel", "arbitrary"))(qkvc, qkvc, qkvc, tfull, do, lse, delta)


def _rpb_constants():
    p = np.arange(GRID_W)[:, None]
    qc = np.arange(GRID_W)[None, :]
    dc = np.clip(qc - p, -(NA_COLS - 1), NA_COLS - 1) + NA_COLS - 1
    onehot = (dc.reshape(1, -1) == np.arange(32)[:, None]).astype(np.float32)
    c_start = np.clip(p - NA_COLS // 2, 0, GRID_W - NA_COLS)
    col_ok = ((qc >= c_start) & (qc < c_start + NA_COLS)).reshape(1, -1).astype(np.float32)
    a = np.arange(16)[:, None]
    bj = np.arange(64)[None, :]
    row_sel = ((bj // 8 + bj % 8) == a).astype(np.float32)
    return jnp.asarray(onehot), jnp.asarray(col_ok), jnp.asarray(row_sel)


def _rpb_expand(rpb, onehot, col_ok):
    r2 = jnp.pad(rpb.reshape(HEADS_C * 15, 31), ((0, 4), (0, 1)))

    def body(r_ref, oh_ref, ok_ref, o_ref):
        t = jnp.dot(r_ref[...], oh_ref[...], preferred_element_type=F32, precision=lax.Precision.HIGHEST)
        o_ref[...] = jnp.where(ok_ref[...] > 0.5, t, NEG_INF)

    tm = pl.pallas_call(body, name="rpb_expand", out_shape=_sds((64, GRID_W * GRID_W), F32))(r2, onehot, col_ok)
    tm = tm[:HEADS_C * 15].reshape(HEADS_C, 15, GRID_W, GRID_W)
    tfull = jnp.stack([jnp.concatenate([tm[:, base + j] for j in range(NA_ROWS)], axis=-1) for base in range(NA_ROWS)], axis=1)
    return tfull


def _rpb_grad(dtfull, onehot, row_sel):
    g = dtfull.reshape(HEADS_C, NA_ROWS, GRID_W, NA_ROWS, GRID_W).transpose(0, 1, 3, 2, 4).reshape(HEADS_C, 64, GRID_W * GRID_W)

    def body(g_ref, oh_ref, sel_ref, o_ref):
        for h in range(HEADS_C):
            mid = lax.dot_general(g_ref[h], oh_ref[...], (((1,), (1,)), ((), ())), preferred_element_type=F32,
                                  precision=lax.Precision.HIGHEST)
            o_ref[h] = jnp.dot(sel_ref[...], mid, preferred_element_type=F32, precision=lax.Precision.HIGHEST)

    out = pl.pallas_call(body, name="rpb_grad", out_shape=_sds((HEADS_C, 16, 32), F32))(g, onehot, row_sel)
    return out[:, :15, :31]


def _outnorm_fwd(o_a, branch_o, branch_lse, o_c, ga, gb, gc):
    s = o_a.shape[0]
    ts = _tile(s, (512, 256, 128))

    def body(a_ref, o1, o2, o3, l1, l2, l3, c_ref, ga_ref, gb_ref, gc_ref, ob_ref, lse_ref, o_ref):
        la, lb, lc = l1[...], l2[...], l3[...]
        m = jnp.maximum(jnp.maximum(la, lb), lc)
        ea, eb, ec = jnp.exp(la - m), jnp.exp(lb - m), jnp.exp(lc - m)
        den = ea + eb + ec
        o_b = (o1[...] * ea + o2[...] * eb + o3[...] * ec) / den
        ob_ref[...] = o_b
        lse_ref[...] = m + jnp.log(den)
        col = 0
        for x, g in ((a_ref[...], ga_ref), (o_b, gb_ref), (c_ref[...], gc_ref)):
            o_ref[:, col:col + x.shape[1]] = (x * _rstd(x) * g[...]).astype(BF16)
            col += x.shape[1]

    sp = _row_spec(ts, WIDTH_AB)
    return pl.pallas_call(
        body, name="outnorm_fwd", grid=(s // ts,),
        in_specs=[sp] * 7 + [_row_spec(ts, WIDTH_C), _fix_spec(WIDTH_AB), _fix_spec(WIDTH_AB), _fix_spec(WIDTH_C)],
        out_specs=(sp, sp, _row_spec(ts, D_MODEL)),
        out_shape=(_sds((s, WIDTH_AB), F32), _sds((s, WIDTH_AB), F32), _sds((s, D_MODEL), BF16)),
        compiler_params=_cp("parallel"))(o_a, *branch_o, *branch_lse, o_c, ga.reshape(1, -1), gb.reshape(1, -1),
                                         gc.reshape(1, -1))


def _outnorm_bwd(dxb, w_out, o_a, o_b, o_c, ga, gb, gc):
    s = o_a.shape[0]
    ts = _tile(s, (512, 256, 128))

    def body(dx_ref, w_ref, a_ref, b_ref, c_ref, ga_ref, gb_ref, gc_ref, *outs):
        first = pl.program_id(0) == 0
        dm = lax.dot_general(dx_ref[...], w_ref[...], (((1,), (1,)), ((), ())), preferred_element_type=F32)
        col = 0
        for k, (ref, g) in enumerate(((a_ref, ga_ref), (b_ref, gb_ref), (c_ref, gc_ref))):
            x = ref[...]
            w = x.shape[1]
            dx, dg = _rms_bwd_rows(x, g[...], dm[:, col:col + w])
            col += w
            outs[k][...] = dx.astype(BF16)
            for b, blk in enumerate(_group_sum(dx * x)):
                if k == 0:
                    outs[3][b] = _stat_rows(blk)
                else:
                    outs[3 + k][:, b * LANE:(b + 1) * LANE] = blk
            _accum(outs[6 + k], dg, first)

    widths = (WIDTH_AB, WIDTH_AB, WIDTH_C)
    return pl.pallas_call(
        body, name="outnorm_bwd", grid=(s // ts,),
        in_specs=[_row_spec(ts, D_MODEL), pl.BlockSpec(w_out.shape, lambda i: (0, 0))] + [_row_spec(ts, w) for w in widths]
        + [_fix_spec(w) for w in widths],
        out_specs=tuple([_row_spec(ts, w) for w in widths] + [pl.BlockSpec((3, 8, ts), lambda i: (0, 0, i))]
                        + [_row_spec(ts, w) for w in widths[1:]] + [_fix_spec(w) for w in widths]),
        out_shape=tuple([_sds((s, w), BF16) for w in widths] + [_sds((3, 8, s), F32)]
                        + [_sds((s, w), F32) for w in widths[1:]] + [_sds((1, w), F32) for w in widths]),
        compiler_params=_cp("arbitrary"))(dxb, w_out, o_a, o_b, o_c, ga.reshape(1, -1), gb.reshape(1, -1),
                                          gc.reshape(1, -1))


def _adamw(w, g, m, v, *, name):
    r, c = w.shape
    tr = _tile(r, (512, 256, 128, 64, 8))

    def body(w_ref, g_ref, m_ref, v_ref, d_ref, nm_ref, nv_ref):
        gv = g_ref[...]
        nm = ADAM_B1 * m_ref[...] + (1.0 - ADAM_B1) * gv
        nv = ADAM_B2 * v_ref[...] + (1.0 - ADAM_B2) * jnp.square(gv)
        m_hat = nm / (1.0 - ADAM_B1 ** ADAM_STEP)
        v_hat = nv / (1.0 - ADAM_B2 ** ADAM_STEP)
        d_ref[...] = -ADAM_LR * (m_hat / (jnp.sqrt(v_hat) + ADAM_EPS) + ADAM_WD * w_ref[...])
        nm_ref[...] = nm
        nv_ref[...] = nv

    sp = _row_spec(tr, c)
    return pl.pallas_call(
        body, name=name, grid=(r // tr,), in_specs=[sp] * 4, out_specs=(sp, sp, sp),
        out_shape=(_sds((r, c), F32),) * 3, compiler_params=_cp("parallel"))(w, g, m, v)


def _add_n(parts, *, name, out_dtype):
    r, c = parts[0].shape
    tr = max(t for t in range(16, 1025, 16) if r % t == 0)

    def body(*refs):
        acc = refs[0][...].astype(F32)
        for ref in refs[1:-1]:
            acc = acc + ref[...].astype(F32)
        refs[-1][...] = acc.astype(out_dtype)

    sp = _row_spec(tr, c)
    return pl.pallas_call(
        body, name=name, grid=(r // tr,), in_specs=[sp] * len(parts), out_specs=sp, out_shape=_sds((r, c), out_dtype),
        compiler_params=_cp("parallel"))(*parts)


ANY = pl.BlockSpec(memory_space=pl.ANY)
CHIP_FLIPS = ((1, 0), (0, 1), (1, 1))


def _me():
    return lax.axis_index("x"), lax.axis_index("y"), lax.axis_index("c")


def _gather_chips(half):
    def body(src, out, send_sems, recv_sems):
        x, y, c = _me()
        me, mine = (x, y, c), 2 * x + y
        chip_x, chip_y, chip_d = 2 * (1 - x) + y, 2 * x + (1 - y), 2 * (1 - x) + (1 - y)
        passed_chip = 2 * (x ^ (1 - c)) + (y ^ c)
        pass_to = (x ^ c, y ^ (1 - c), c)

        def copy(k, chip, half_idx, to, source=None):
            dst = out.at[chip, half_idx]
            return pltpu.make_async_remote_copy(src_ref=dst if source is None else source, dst_ref=dst,
                                                send_sem=send_sems.at[k], recv_sem=recv_sems.at[k], device_id=to,
                                                device_id_type=MESH_T)

        sends = [copy(0, mine, c, (1 - x, y, c), source=src), copy(1, mine, c, (x, 1 - y, c), source=src)]
        for cp in sends:
            cp.start()
        copy(0, chip_x, c, me).wait_recv()
        copy(1, chip_y, c, me).wait_recv()
        sends += [copy(2, passed_chip, c, pass_to), copy(3, chip_x, c, (x, y, 1 - c)), copy(4, chip_y, c, (x, y, 1 - c))]
        for cp in sends[2:]:
            cp.start()
        copy(2, chip_d, c, me).wait_recv()
        sends.append(copy(5, chip_d, c, (x, y, 1 - c)))
        sends[-1].start()
        for k, chip in ((3, chip_x), (4, chip_y), (5, chip_d)):
            copy(k, chip, 1 - c, me).wait_recv()
        for cp in sends:
            cp.wait_send()

    return pl.pallas_call(
        body, name="gather_chips", in_specs=[ANY], out_specs=ANY, out_shape=_sds((4, 2) + half.shape, half.dtype),
        scratch_shapes=[pltpu.SemaphoreType.DMA((6,)), pltpu.SemaphoreType.DMA((6,))])(half)


def _swap_sibling(block):
    def body(src, out, send_sem, recv_sem):
        x, y, c = _me()
        cp = pltpu.make_async_remote_copy(src_ref=src, dst_ref=out, send_sem=send_sem, recv_sem=recv_sem,
                                          device_id=(x, y, 1 - c), device_id_type=MESH_T)
        cp.start()
        cp.wait()

    return pl.pallas_call(
        body, name="swap_sibling", in_specs=[ANY], out_specs=ANY, out_shape=_sds(block.shape, block.dtype),
        scratch_shapes=[pltpu.SemaphoreType.DMA(()), pltpu.SemaphoreType.DMA(())])(block)


def _swap_other_half(halves):
    def body(src, out, send_sem, recv_sem):
        x, y, c = _me()
        cp = pltpu.make_async_remote_copy(src_ref=src.at[:, 1 - c], dst_ref=out, send_sem=send_sem, recv_sem=recv_sem,
                                          device_id=(x, y, 1 - c), device_id_type=MESH_T)
        cp.start()
        cp.wait()

    shape = (halves.shape[0],) + halves.shape[2:]
    return pl.pallas_call(
        body, name="swap_other_half", in_specs=[ANY], out_specs=ANY, out_shape=_sds(shape, halves.dtype),
        scratch_shapes=[pltpu.SemaphoreType.DMA(()), pltpu.SemaphoreType.DMA(())])(halves)


def _pair_sum(halves, core, other):
    n, _, h, c = halves.shape
    tr = max(t for t in range(16, 1025, 16) if h % t == 0)

    def body(core_ref, a_ref, b_ref, o_ref):
        o_ref[...] = (a_ref[...] + b_ref[...]).astype(BF16)

    grid_spec = pltpu.PrefetchScalarGridSpec(
        num_scalar_prefetch=1, grid=(n, h // tr),
        in_specs=[pl.BlockSpec((None, None, tr, c), lambda j, i, core_ref: (j, core_ref[0], i, 0)),
                  pl.BlockSpec((None, tr, c), lambda j, i, core_ref: (j, i, 0))],
        out_specs=pl.BlockSpec((None, tr, c), lambda j, i, core_ref: (j, i, 0)))
    return pl.pallas_call(
        body, name="pair_sum", grid_spec=grid_spec, out_shape=_sds((n, h, c), BF16),
        compiler_params=_cp("parallel", "parallel"))(core.reshape(1).astype(jnp.int32), halves, other)


def _scatter_chips(parts):
    def body(src, out, send_sems, recv_sems):
        x, y, c = _me()
        mine = 2 * x + y
        sends = []
        for k, (fx, fy) in enumerate(CHIP_FLIPS):
            theirs = 2 * (x ^ fx) + (y ^ fy)
            cp = pltpu.make_async_remote_copy(src_ref=src.at[theirs], dst_ref=out.at[mine], send_sem=send_sems.at[k],
                                              recv_sem=recv_sems.at[k], device_id=(x ^ fx, y ^ fy, c), device_id_type=MESH_T)
            cp.start()
            sends.append(cp)
        for k, (fx, fy) in enumerate(CHIP_FLIPS):
            theirs = 2 * (x ^ fx) + (y ^ fy)
            pltpu.make_async_remote_copy(src_ref=src.at[theirs], dst_ref=out.at[theirs], send_sem=send_sems.at[k],
                                         recv_sem=recv_sems.at[k], device_id=(x ^ fx, y ^ fy, c),
                                         device_id_type=MESH_T).wait_recv()
        for cp in sends:
            cp.wait_send()

    return pl.pallas_call(
        body, name="scatter_chips", in_specs=[ANY], out_specs=ANY, out_shape=_sds(parts.shape, parts.dtype),
        scratch_shapes=[pltpu.SemaphoreType.DMA((3,)), pltpu.SemaphoreType.DMA((3,))])(parts)


def _all_reduce_small(block):
    r, c = block.shape

    def body(src, out, slots, send_sems, recv_sems):
        x, y, cc = _me()
        mine = 4 * x + 2 * y + cc
        slots[mine] = src[...]
        sends = []
        for k in range(1, 8):
            fx, fy, fc = (k >> 2) & 1, (k >> 1) & 1, k & 1
            cp = pltpu.make_async_remote_copy(src_ref=src, dst_ref=slots.at[mine], send_sem=send_sems.at[k - 1],
                                              recv_sem=recv_sems.at[k - 1], device_id=(x ^ fx, y ^ fy, cc ^ fc),
                                              device_id_type=MESH_T)
            cp.start()
            sends.append(cp)
        for k in range(1, 8):
            fx, fy, fc = (k >> 2) & 1, (k >> 1) & 1, k & 1
            theirs = 4 * (x ^ fx) + 2 * (y ^ fy) + (cc ^ fc)
            pltpu.make_async_remote_copy(src_ref=src, dst_ref=slots.at[theirs], send_sem=send_sems.at[k - 1],
                                         recv_sem=recv_sems.at[k - 1], device_id=(x ^ fx, y ^ fy, cc ^ fc),
                                         device_id_type=MESH_T).wait_recv()
        for cp in sends:
            cp.wait_send()
        acc = slots[0]
        for d in range(1, 8):
            acc = acc + slots[d]
        out[...] = acc

    vm = pl.BlockSpec(memory_space=pltpu.VMEM)
    return pl.pallas_call(
        body, name="all_reduce_small", in_specs=[vm], out_specs=vm, out_shape=_sds((r, c), F32),
        scratch_shapes=[pltpu.VMEM((8, r, c), F32), pltpu.SemaphoreType.DMA((7,)), pltpu.SemaphoreType.DMA((7,))])(block)


DIRECT = ("w_mlp_in", "w_mlp_out", "w_out")
BIG = DIRECT + ("w_in", "w_uq", "w_ukv")
COL_SHARDED = {"w_in": True, "w_uq": True, "w_ukv": True, "w_out": False, "w_mlp_in": True, "w_mlp_out": False}
SMALL = ("g_mix", "q_norm", "kv_norm", "rpb", "out_norm_a", "out_norm_b", "out_norm_c", "g_mlp", "g_final")
PACK_C = 1024
ROW_ALIGN = 32


def _pack_rows(parts):
    flat = jnp.concatenate([p.reshape(-1, PACK_C) for p in parts], axis=0)
    return jnp.pad(flat, ((0, -flat.shape[0] % ROW_ALIGN), (0, 0)))


def _unpack_rows(flat, shapes):
    out, row = [], 0
    for shp in shapes:
        n = int(np.prod(shp)) // PACK_C
        out.append(flat[row:row + n].reshape(shp))
        row += n
    return out


def _full_from_shards(name, g):
    if COL_SHARDED[name]:
        return g.transpose(1, 2, 0, 3).reshape(g.shape[1], g.shape[2], 4 * g.shape[3])
    return g.transpose(1, 0, 2, 3).reshape(g.shape[1], 4 * g.shape[2], g.shape[3])


def _shards_from_full(name, w):
    l, k, n = w.shape
    if COL_SHARDED[name]:
        return w.reshape(l, k, 4, n // 4).transpose(2, 0, 1, 3)
    return w.reshape(l, 4, k // 4, n).transpose(1, 0, 2, 3)


def _arrange_w_in(w):
    z = jnp.zeros(w.shape[:-1] + (COL_B - COL_KPE - QK_ROPE,), w.dtype)
    return jnp.concatenate([w[..., :COL_KPE + QK_ROPE], z, w[..., COL_KPE + QK_ROPE:]], axis=-1)


def _unarrange_w_in(w):
    return jnp.concatenate([w[..., :COL_KPE + QK_ROPE], w[..., COL_B:]], axis=-1)


def _arrange_w_uq(w):
    per = HEAD_DIM + QK_ROPE
    z = jnp.zeros(w.shape[:-1] + (LANE - per,), w.dtype)
    cols = []
    for h in range(HEADS_A):
        cols += [w[..., h * per:(h + 1) * per], z]
    return jnp.concatenate(cols, axis=-1)


def _unarrange_w_uq(w):
    per = HEAD_DIM + QK_ROPE
    return jnp.concatenate([w[..., h * LANE:h * LANE + per] for h in range(HEADS_A)], axis=-1)


def _arrange_w_ukv(w):
    z = jnp.zeros(w.shape[:-1] + (HEAD_DIM,), w.dtype)
    ks = []
    for h in range(HEADS_A):
        ks += [w[..., h * LANE:h * LANE + HEAD_DIM], z]
    vs = [w[..., h * LANE + HEAD_DIM:(h + 1) * LANE] for h in range(HEADS_A)]
    return jnp.concatenate(ks + vs, axis=-1)


def _unarrange_w_ukv(w):
    cols = []
    for h in range(HEADS_A):
        cols += [w[..., h * LANE:h * LANE + HEAD_DIM], w[..., W_A2 + h * HEAD_DIM:W_A2 + (h + 1) * HEAD_DIM]]
    return jnp.concatenate(cols, axis=-1)


def _layer_fwd(x, w, sm, tabs, consts):
    t32, t64 = tabs
    onehot, col_ok, _, band = consts
    h, proj, cqn, ckvn, kpe, qkvb, qkvc = _in_proj(x, sm["g_mix"], w["w_in"], sm["q_norm"], sm["kv_norm"], t32, t64)
    qa2, ka2, kat, va1 = _a_post_fwd(cqn, ckvn, w["w_uq"], w["w_ukv"], kpe, t32)
    o_a, lse_a = _dense_fwd(qa2, ka2, va1)
    branch = [_banded_fwd(qkvb, dil, band) for _, dil in DILATED_PAIRS]
    tfull = _rpb_expand(sm["rpb"], onehot, col_ok)
    o_c, lse_c = _natten_fwd(qkvc, tfull)
    o_b, lse_b, mixed = _outnorm_fwd(o_a, [b[0] for b in branch], [b[1] for b in branch], o_c, sm["out_norm_a"],
                                     sm["out_norm_b"], sm["out_norm_c"])
    x_mid = _mm_nn(mixed, w["w_out"], name="out_proj", res=x)
    h2, act = _mlp_in(x_mid, sm["g_mlp"], w["w_mlp_in"])
    x_out = _mm_nn(act, w["w_mlp_out"], name="mlp_out", res=x_mid)
    saved = dict(x=x, h=h, proj=proj, cqn=cqn, ckvn=ckvn, qkvb=qkvb, qkvc=qkvc, qa2=qa2, ka2=ka2, kat=kat, va1=va1, o_a=o_a,
                 lse_a=lse_a, o_b=o_b, lse_b=lse_b, o_c=o_c, lse_c=lse_c, tfull=tfull, mixed=mixed, x_mid=x_mid, h2=h2,
                 act=act)
    return x_out, saved


def _layer_bwd(dx, dxb, sv, w, sm, tabs, consts, packed, places):
    t32, t64 = tabs
    onehot, _, row_sel, band = consts
    g = {}
    du = _mm_nt(dxb, w["w_mlp_out"], name="mlp_out_dx", out_dtype=BF16, relu2_act=sv["act"])
    packed = _mm_tn(sv["act"], dxb, name="mlp_out_dw", packed=(packed,) + places["w_mlp_out"])
    dx_mid, dmb, g["g_mlp"] = _mm_nt_norm_bwd(du, w["w_mlp_in"], sv["x_mid"], sm["g_mlp"], dx, name="mlp_in_dx")
    packed = _mm_tn(sv["h2"], du, name="mlp_in_dw", packed=(packed,) + places["w_mlp_in"])
    packed = _mm_tn(sv["mixed"], dmb, name="out_proj_dw", packed=(packed,) + places["w_out"])
    (do_a, do_b, do_c, dl_a, dl_b, dl_c, g["out_norm_a"], g["out_norm_b"], g["out_norm_c"]) = _outnorm_bwd(
        dmb, w["w_out"], sv["o_a"], sv["o_b"], sv["o_c"], sm["out_norm_a"], sm["out_norm_b"], sm["out_norm_c"])
    dqa2_t, dka2, dva = _dense_bwd(sv["qa2"], sv["ka2"], sv["kat"], sv["va1"], do_a, sv["lse_a"], dl_a)
    db = []
    for _, dil in DILATED_PAIRS:
        db += _banded_bwd(sv["qkvb"], do_b, sv["lse_b"], dl_b, dil, band)
    dq_c, dk_c, dv_c, dtfull = _natten_bwd(sv["qkvc"], sv["tfull"], do_c, sv["lse_c"], dl_c)
    g["rpb"] = _rpb_grad(dtfull, onehot, row_sel)
    dqa, dkva, dkpe = _a_post_bwd(dqa2_t, dka2, dva, t32)
    dcqn = _mm_nt(dqa, w["w_uq"], name="q_up_dx")
    g["w_uq"] = _unarrange_w_uq(_mm_tn(sv["cqn"], dqa, name="q_up_dw"))
    dckvn = _mm_nt(dkva, w["w_ukv"], name="kv_up_dx")
    g["w_ukv"] = _unarrange_w_ukv(_mm_tn(sv["ckvn"], dkva, name="kv_up_dw"))
    dproj, g["q_norm"], g["kv_norm"] = _prep_bwd(sv["proj"], sm["q_norm"], sm["kv_norm"], t32, t64, dcqn, dckvn, dkpe,
                                                  db, (dq_c, dk_c, dv_c))
    g["w_in"] = _unarrange_w_in(_mm_tn(sv["h"], dproj, name="in_proj_dw"))
    dx_in, dxb_in, g["g_mix"] = _mm_nt_norm_bwd(dproj, w["w_in"], sv["x"], sm["g_mix"], dx_mid, name="in_proj_dx")
    return dx_in, dxb_in, g, packed


def _packed_places(offs, l):
    d, r = D_MODEL, D_MODEL // 4
    return {"w_mlp_in": (512, lambda i, j: (j, (offs["w_mlp_in"] + l * d) // 512 + i)),
            "w_mlp_out": (512, lambda i, j: (i // 2, (offs["w_mlp_out"] + l * d) // 512 + i % 2)),
            "w_out": (r, lambda i, j: (i, (offs["w_out"] + l * r) // r))}


def _local_step(x, target, wfull, small, packed_shape, offs):
    s = x.shape[0]
    tabs = (_rope_tables(s, QK_ROPE // 2, 1, lead=HEAD_DIM), _rope_tables(s, HEAD_DIM // 2, 2))
    consts = _rpb_constants() + (_band_bias_table(),)
    saved = []
    for l in range(DEPTH):
        wl = {k: v[l] for k, v in wfull.items()}
        sl = {k: small[k][l] for k in SMALL if k != "g_final"}
        x, sv = _layer_fwd(x, wl, sl, tabs, consts)
        saved.append(sv)
    loss, dx, dxb, dg_final = _loss_head(x, small["g_final"], target)
    grads = [None] * DEPTH
    packed = packed_shape
    for l in reversed(range(DEPTH)):
        wl = {k: v[l] for k, v in wfull.items()}
        sl = {k: small[k][l] for k in SMALL if k != "g_final"}
        dx, dxb, grads[l], packed = _layer_bwd(dx, dxb, saved[l], wl, sl, tabs, consts, packed, _packed_places(offs, l))
    return loss, dx, grads, dg_final, packed


ARRANGE = {"w_in": _arrange_w_in, "w_uq": _arrange_w_uq, "w_ukv": _arrange_w_ukv}


def kernel(x, g_mix, w_in, q_norm, w_uq, kv_norm, w_ukv, rpb, out_norm_a, out_norm_b, out_norm_c, w_out, g_mlp, w_mlp_in, w_mlp_out, g_final, loss_target, m_g_mix, m_w_in, m_q_norm, m_w_uq, m_kv_norm, m_w_ukv, m_rpb, m_out_norm_a, m_out_norm_b, m_out_norm_c, m_w_out, m_g_mlp, m_w_mlp_in, m_w_mlp_out, m_g_final, v_g_mix, v_w_in, v_q_norm, v_w_uq, v_kv_norm, v_w_ukv, v_rpb, v_out_norm_a, v_out_norm_b, v_out_norm_c, v_w_out, v_g_mlp, v_w_mlp_in, v_w_mlp_out, v_g_final):
    args = dict(locals())
    weights = {k: args[k] for k in BIG + SMALL}
    moms = {k: args["m_" + k] for k in BIG + SMALL}
    vels = {k: args["v_" + k] for k in BIG + SMALL}
    cc = lax.axis_index("c")
    my_chip = 2 * lax.axis_index("x") + lax.axis_index("y")

    shard_shapes = [weights[k].shape for k in BIG]
    packed_w = _pack_rows([weights[k].astype(BF16) for k in BIG])
    rows = packed_w.shape[0]
    my_half = lax.dynamic_index_in_dim(packed_w.reshape(2, rows // 2, PACK_C), cc, axis=0, keepdims=False)
    gathered = _gather_chips(my_half).reshape(4, rows, PACK_C)
    per_chip = [_unpack_rows(jnp.where(my_chip == j, packed_w, gathered[j]), shard_shapes) for j in range(4)]
    wfull = {}
    for idx, k in enumerate(BIG):
        full = _full_from_shards(k, jnp.stack([per_chip[j][idx] for j in range(4)]))
        wfull[k] = ARRANGE[k](full) if k in ARRANGE else full

    small = {k: weights[k] for k in SMALL}
    offs, row = {}, 0
    for k, shp in zip(BIG, shard_shapes):
        offs[k] = row
        row += int(np.prod(shp)) // PACK_C
    loss, dx, grads, dg_final, packed = _local_step(x[0], loss_target[0], wfull, small, _sds((4, rows, PACK_C), F32), offs)

    small_local = {k: jnp.stack([grads[l][k].reshape(weights[k].shape[1:]) for l in range(DEPTH)])
                   for k in SMALL if k != "g_final"}
    small_local["g_final"] = dg_final.reshape(-1)
    small_shapes = [weights[k].shape for k in SMALL]
    n_small = sum(int(np.prod(s)) for s in small_shapes)
    flat = jnp.concatenate([small_local[k].reshape(-1) for k in SMALL] + [loss[0, :1]])
    rows_small = -(-(n_small + 1) // PACK_C)
    rows_small += -rows_small % 8
    flat = jnp.pad(flat, (0, rows_small * PACK_C - n_small - 1)).reshape(rows_small, PACK_C)
    red = _all_reduce_small(flat).reshape(-1)
    loss_out = red[n_small]
    small_grads, off = {}, 0
    for k, shp in zip(SMALL, small_shapes):
        n = int(np.prod(shp))
        small_grads[k] = red[off:off + n].reshape(shp)
        off += n

    rest = [k for k in BIG if k not in DIRECT]
    by_shard = {k: _shards_from_full(k, jnp.stack([grads[l][k] for l in range(DEPTH)])) for k in rest}
    tail = jnp.stack([_pack_rows([by_shard[k][j] for k in rest]) for j in range(4)])
    assert offs[rest[0]] + tail.shape[1] == rows
    packed = lax.dynamic_update_slice(packed, tail, (0, offs[rest[0]], 0))
    halves = packed.reshape(4, 2, rows // 2, PACK_C)
    pair = _pair_sum(halves, cc, _swap_other_half(halves))
    by_chip = _scatter_chips(pair)
    reduced = _add_n([jnp.where(my_chip == j, pair[j], by_chip[j]) for j in range(4)], name="chip_sum", out_dtype=F32)
    theirs = _swap_sibling(reduced)
    joined = jnp.where(cc == 0, jnp.concatenate([reduced, theirs]), jnp.concatenate([theirs, reduced]))
    big_grads = dict(zip(BIG, _unpack_rows(joined, shard_shapes)))

    out_g, out_d, out_m, out_v = {}, {}, {}, {}
    for k in BIG:
        shp = weights[k].shape
        two_d = (shp[0] * shp[1], shp[2])
        d, nm, nv = _adamw(weights[k].reshape(two_d), big_grads[k].reshape(two_d), moms[k].reshape(two_d),
                           vels[k].reshape(two_d), name="adamw_" + k)
        out_g[k], out_d[k], out_m[k], out_v[k] = big_grads[k], d.reshape(shp), nm.reshape(shp), nv.reshape(shp)

    def pack_small(tree):
        f = jnp.concatenate([tree[k].reshape(-1) for k in SMALL])
        return jnp.pad(f, (0, rows_small * PACK_C - n_small)).reshape(rows_small, PACK_C)

    d, nm, nv = _adamw(pack_small(small), pack_small(small_grads), pack_small(moms), pack_small(vels), name="adamw_small")
    for tree, flat_out in ((out_d, d), (out_m, nm), (out_v, nv)):
        off = 0
        fo = flat_out.reshape(-1)
        for k, shp in zip(SMALL, small_shapes):
            n = int(np.prod(shp))
            tree[k] = fo[off:off + n].reshape(shp)
            off += n
    out_g.update(small_grads)

    order = ("g_mix", "w_in", "q_norm", "w_uq", "kv_norm", "w_ukv", "rpb", "out_norm_a", "out_norm_b", "out_norm_c", "w_out",
             "g_mlp", "w_mlp_in", "w_mlp_out", "g_final")
    return (loss_out, dx.reshape(x.shape), *[out_g[k] for k in order], *[out_d[k] for k in order],
            *[out_m[k] for k in order], *[out_v[k] for k in order])
```

```python
import math

import numpy as np
import jax
import jax.numpy as jnp
from jax import lax
from jax.experimental import pallas as pl
from jax.experimental.pallas import tpu as pltpu

F32 = jnp.float32
BF16 = jnp.bfloat16

D_MODEL = 1024
HEAD_DIM = 64
Q_LORA = 256
KV_LORA = 128
QK_ROPE = 32
HEADS_A = 6
HEADS_B = 6
HEADS_C = 4
DILATED_PAIRS = ((128, 1), (512, 4), (2048, 16))
BAND_HALF = 64
GRID_W = 64
NA_ROWS = 8
NA_COLS = 16
D_FF = 4096
ROPE_THETA = 10000.0
NORM_EPS = 1e-6
NEG_INF = -1e30
DEPTH = 4

LANE = 128
PROJ_W = 2432
COL_CKV = 256
COL_KPE = 384
COL_B = 512
COL_C = 1664
W_A2 = 768
W_KV = W_A2 + 384
WIDTH_AB = 384
WIDTH_C = 256
SCALE_A = (HEAD_DIM + QK_ROPE) ** -0.5
SCALE_BC = HEAD_DIM ** -0.5

ADAM_LR = 0.001
ADAM_B1 = 0.9
ADAM_B2 = 0.999
ADAM_EPS = 1e-08
ADAM_WD = 0.01
ADAM_STEP = 10

VMEM_LIMIT = 56 * 1024 * 1024
MESH_T = pl.DeviceIdType.MESH


def _cp(*sem):
    return pltpu.CompilerParams(dimension_semantics=sem or None, vmem_limit_bytes=VMEM_LIMIT)


def _tile(n, cands):
    for c in cands:
        if n % c == 0:
            return c
    return n


def _sds(shape, dtype):
    return jax.ShapeDtypeStruct(shape, dtype)


ROW_TILE_BYTES = 12 * 1024 * 1024


def _row_tiles(row_bytes):
    return tuple(t for t in (2048, 1024, 512, 256, 128) if t * row_bytes <= ROW_TILE_BYTES or t <= 512)


def _mm_nn(a, b, *, name, out_dtype=F32, res=None):
    m, k = a.shape
    n = b.shape[1]
    tn = _tile(n, (1024, 768, 512)) if n % LANE == 0 and n != PROJ_W else n
    tm = _tile(m, _row_tiles(2 * k + tn * (jnp.dtype(out_dtype).itemsize + (4 if res is not None else 0))))

    def body(*refs):
        a_ref, b_ref = refs[0], refs[1]
        o_ref = refs[-1]
        acc = jnp.dot(a_ref[...], b_ref[...], preferred_element_type=F32)
        if res is not None:
            acc = refs[2][...] + acc
        o_ref[...] = acc.astype(o_ref.dtype)

    in_specs = [pl.BlockSpec((tm, k), lambda j, i: (i, 0)), pl.BlockSpec((k, tn), lambda j, i: (0, j))]
    args = [a, b]
    if res is not None:
        in_specs.append(pl.BlockSpec((tm, tn), lambda j, i: (i, j)))
        args.append(res)
    return pl.pallas_call(
        body, name=name, grid=(n // tn, m // tm), in_specs=in_specs,
        out_specs=pl.BlockSpec((tm, tn), lambda j, i: (i, j)), out_shape=_sds((m, n), out_dtype),
        compiler_params=_cp("parallel", "parallel"))(*args)


def _mlp_in(x, g, w):
    m, k = x.shape
    n = w.shape[1]
    tm = _tile(m, (256, 128))

    def body(x_ref, g_ref, w_ref, h_ref, o_ref):
        xv = x_ref[...]
        h = (xv * _rstd(xv) * g_ref[...]).astype(BF16)
        h_ref[...] = h
        u = jnp.dot(h, w_ref[...], preferred_element_type=F32)
        o_ref[...] = jnp.square(jnp.maximum(u, 0.0)).astype(BF16)

    return pl.pallas_call(
        body, name="mlp_in", grid=(m // tm,),
        in_specs=[pl.BlockSpec((tm, k), lambda i: (i, 0)), pl.BlockSpec((1, k), lambda i: (0, 0)),
                  pl.BlockSpec((k, n), lambda i: (0, 0))],
        out_specs=(pl.BlockSpec((tm, k), lambda i: (i, 0)), pl.BlockSpec((tm, n), lambda i: (i, 0))),
        out_shape=(_sds((m, k), BF16), _sds((m, n), BF16)),
        compiler_params=_cp("parallel"))(x, g.reshape(1, k), w)


def _mm_nt(a, b, *, name, out_dtype=F32, relu2_act=None):
    m, c = a.shape
    n = b.shape[0]
    tn = _tile(n, (1024, 512, 256, 128))
    tm = _tile(m, _row_tiles(2 * c + tn * (jnp.dtype(out_dtype).itemsize + (2 if relu2_act is not None else 0))))

    def body(*refs):
        a_ref, b_ref = refs[0], refs[1]
        o_ref = refs[-1]
        acc = lax.dot_general(a_ref[...], b_ref[...], (((1,), (1,)), ((), ())), preferred_element_type=F32)
        if relu2_act is not None:
            acc = acc * (2.0 * jnp.sqrt(refs[2][...].astype(F32)))
        o_ref[...] = acc.astype(o_ref.dtype)

    in_specs = [pl.BlockSpec((tm, c), lambda j, i: (i, 0)), pl.BlockSpec((tn, c), lambda j, i: (j, 0))]
    args = [a, b]
    if relu2_act is not None:
        in_specs.append(pl.BlockSpec((tm, tn), lambda j, i: (i, j)))
        args.append(relu2_act)
    return pl.pallas_call(
        body, name=name, grid=(n // tn, m // tm), in_specs=in_specs,
        out_specs=pl.BlockSpec((tm, tn), lambda j, i: (i, j)), out_shape=_sds((m, n), out_dtype),
        compiler_params=_cp("parallel", "parallel"))(*args)


def _mm_nt_norm_bwd(a, b, x, g, res, *, name):
    m, c = a.shape
    d = b.shape[0]
    tm = _tile(m, (512, 256, 128))

    def body(a_ref, b_ref, x_ref, g_ref, res_ref, dx_ref, dxb_ref, dg_ref):
        dy = lax.dot_general(a_ref[...], b_ref[...], (((1,), (1,)), ((), ())), preferred_element_type=F32)
        dx, dg = _rms_bwd_rows(x_ref[...], g_ref[...], dy)
        dx = res_ref[...] + dx
        dx_ref[...] = dx
        dxb_ref[...] = dx.astype(BF16)
        _accum(dg_ref, dg, pl.program_id(0) == 0)

    row = pl.BlockSpec((tm, d), lambda i: (i, 0))
    fix = pl.BlockSpec((1, d), lambda i: (0, 0))
    return pl.pallas_call(
        body, name=name, grid=(m // tm,),
        in_specs=[pl.BlockSpec((tm, c), lambda i: (i, 0)), pl.BlockSpec((d, c), lambda i: (0, 0)), row, fix, row],
        out_specs=(row, row, fix), out_shape=(_sds((m, d), F32), _sds((m, d), BF16), _sds((1, d), F32)),
        compiler_params=_cp("arbitrary"))(a, b, x, g.reshape(1, d), res)


def _mm_tn(a, b, *, name, packed=None):
    m, ka = a.shape
    nb = b.shape[1]
    tka = _tile(ka, (512, 256, 128)) if packed is None else packed[1]
    tnb = _tile(nb, (1024, 768, 512)) if nb != PROJ_W else nb
    tc = _tile(m, (4096, 2048, 1024, 512, 256, 128) if tnb <= PACK_C else (2048, 1024, 512, 256, 128))

    def body(*refs):
        a_ref, b_ref, o_ref = refs[0], refs[1], refs[-1]
        part = lax.dot_general(a_ref[...], b_ref[...], (((0,), (0,)), ((), ())), preferred_element_type=F32)

        @pl.when(pl.program_id(2) == 0)
        def _():
            o_ref[...] = part

        @pl.when(pl.program_id(2) != 0)
        def _():
            o_ref[...] += part

    in_specs = [pl.BlockSpec((tc, tka), lambda i, j, c: (c, i)), pl.BlockSpec((tc, tnb), lambda i, j, c: (c, j))]
    kwargs = dict(out_specs=pl.BlockSpec((tka, tnb), lambda i, j, c: (i, j)), out_shape=_sds((ka, nb), F32))
    args = [a, b]
    if packed is not None:
        buf, _, place = packed
        assert tnb == PACK_C
        kwargs = dict(out_specs=pl.BlockSpec((None, tka, tnb), lambda i, j, c: place(i, j) + (0,)))
        if isinstance(buf, jax.ShapeDtypeStruct):
            kwargs["out_shape"] = buf
        else:
            kwargs.update(out_shape=_sds(buf.shape, buf.dtype), input_output_aliases={2: 0})
            in_specs.append(pl.BlockSpec(memory_space=pl.ANY))
            args.append(buf)
    return pl.pallas_call(
        body, name=name, grid=(ka // tka, nb // tnb, m // tc), in_specs=in_specs,
        compiler_params=_cp("parallel", "parallel", "arbitrary"), **kwargs)(*args)


def _rstd(x):
    return lax.rsqrt(jnp.mean(x * x, axis=-1, keepdims=True) + NORM_EPS)


def _rms_bwd_rows(x, g, dy):
    r = _rstd(x)
    gy = dy * g
    c = jnp.sum(x * gy, axis=-1, keepdims=True) * (r * r * r) * (1.0 / x.shape[-1])
    return r * gy - x * c, jnp.sum(dy * x * r, axis=0, keepdims=True)


def _accum(ref, part, first):
    @pl.when(first)
    def _():
        ref[...] = part

    @pl.when(jnp.logical_not(first))
    def _():
        ref[...] += part


def _rope(x, c, s1, s2, sh):
    return x * c + pltpu.roll(x, LANE - sh, 1) * s1 + pltpu.roll(x, sh, 1) * s2


def _rope_t(g, c, s1, s2, sh):
    return g * c + pltpu.roll(g * s1, sh, 1) + pltpu.roll(g * s2, LANE - sh, 1)


def _rope_tables(s, half, reps, lead=0):
    pos = jnp.arange(s, dtype=F32)
    inv_freq = ROPE_THETA ** (-jnp.arange(half, dtype=F32) / half)
    ang = pos[:, None] * inv_freq[None, :]
    cos, sin = jnp.cos(ang), jnp.sin(ang)
    zero = jnp.zeros_like(cos)
    ones, lead0 = jnp.ones((s, lead), F32), jnp.zeros((s, lead), F32)
    pad = jnp.zeros((s, LANE - lead - 2 * half * reps), F32)
    c = jnp.concatenate([ones] + [cos, cos] * reps + [pad], axis=1)
    s1 = jnp.concatenate([lead0] + [-sin, zero] * reps + [pad], axis=1)
    s2 = jnp.concatenate([lead0] + [zero, sin] * reps + [pad], axis=1)
    return c, s1, s2


def _lane_lt64(shape):
    return lax.broadcasted_iota(jnp.int32, shape, len(shape) - 1) % LANE < HEAD_DIM


def _group_sum(x):
    outs = []
    for b in range(x.shape[1] // LANE):
        blk = x[:, b * LANE:(b + 1) * LANE]
        lo = _lane_lt64(blk.shape)
        s0 = jnp.sum(jnp.where(lo, blk, 0.0), axis=1, keepdims=True)
        s1 = jnp.sum(jnp.where(lo, 0.0, blk), axis=1, keepdims=True)
        outs.append(jnp.where(lo, s0, s1))
    return outs


def _row_spec(ts, w):
    return pl.BlockSpec((ts, w), lambda i: (i, 0))


def _fix_spec(w):
    return pl.BlockSpec((1, w), lambda i: (0, 0))


def _loss_head(x, g, target):
    s, d = x.shape
    ts = _tile(s, (512, 256, 128))

    def body(x_ref, g_ref, t_ref, loss_ref, dx_ref, dxb_ref, dg_ref):
        xv, gv = x_ref[...], g_ref[...]
        err = xv * _rstd(xv) * gv - t_ref[...]
        part = 0.5 * jnp.sum(jnp.sum(err * err, axis=-1, keepdims=True) * (1.0 / d), axis=0, keepdims=True)
        dx, dg = _rms_bwd_rows(xv, gv, err * (1.0 / d))
        dx_ref[...] = dx
        dxb_ref[...] = dx.astype(BF16)
        first = pl.program_id(0) == 0
        _accum(dg_ref, dg, first)
        _accum(loss_ref, jnp.broadcast_to(part, (1, LANE)), first)

    return pl.pallas_call(
        body, name="loss_head", grid=(s // ts,), in_specs=[_row_spec(ts, d), _fix_spec(d), _row_spec(ts, d)],
        out_specs=(_fix_spec(LANE), _row_spec(ts, d), _row_spec(ts, d), _fix_spec(d)),
        out_shape=(_sds((1, LANE), F32), _sds((s, d), F32), _sds((s, d), BF16), _sds((1, d), F32)),
        compiler_params=_cp("arbitrary"))(x, g.reshape(1, d), target)


def _in_proj(x, g, w, q_norm, kv_norm, t32, t64):
    s, d = x.shape
    ts = _tile(s, (256, 128))

    def body(x_ref, g_ref, w_ref, qn_ref, kn_ref, c32, a32, b32, c64, a64, b64,
             h_ref, p_ref, cqn_ref, ckvn_ref, kpe_ref, qkvb_ref, qkvc_ref):
        xv = x_ref[...]
        h = (xv * _rstd(xv) * g_ref[...]).astype(BF16)
        h_ref[...] = h
        p = jnp.dot(h, w_ref[...], preferred_element_type=F32)
        p_ref[...] = p
        cq = p[:, 0:Q_LORA]
        cqn_ref[...] = (cq * _rstd(cq) * qn_ref[...]).astype(BF16)
        ckv = p[:, COL_CKV:COL_KPE]
        ckvn_ref[...] = (ckv * _rstd(ckv) * kn_ref[...]).astype(BF16)
        kp = pltpu.roll(p[:, COL_KPE:COL_B], HEAD_DIM, 1)
        kpe_ref[...] = _rope(kp, c32[...], a32[...], b32[...], QK_ROPE // 2).astype(BF16)
        for b in range(6):
            blk = _rope(p[:, COL_B + b * LANE:COL_B + (b + 1) * LANE], c64[...], a64[...], b64[...], HEAD_DIM // 2)
            if b < 3:
                blk = blk * SCALE_BC
            qkvb_ref[:, b * LANE:(b + 1) * LANE] = blk.astype(BF16)
        qkvb_ref[:, 2 * WIDTH_AB:3 * WIDTH_AB] = p[:, COL_B + 2 * WIDTH_AB:COL_C].astype(BF16)
        qkvc_ref[:, 0:WIDTH_C] = (p[:, COL_C:COL_C + WIDTH_C] * SCALE_BC).astype(BF16)
        qkvc_ref[:, WIDTH_C:3 * WIDTH_C] = p[:, COL_C + WIDTH_C:PROJ_W].astype(BF16)

    tab = [_row_spec(ts, LANE)] * 6
    widths = (d, PROJ_W, Q_LORA, KV_LORA, LANE, 3 * WIDTH_AB, 3 * WIDTH_C)
    dtypes = (BF16, F32, BF16, BF16, BF16, BF16, BF16)
    return pl.pallas_call(
        body, name="in_proj", grid=(s // ts,),
        in_specs=[_row_spec(ts, d), _fix_spec(d), pl.BlockSpec(w.shape, lambda i: (0, 0)), _fix_spec(Q_LORA),
                  _fix_spec(KV_LORA)] + tab,
        out_specs=tuple(_row_spec(ts, wd) for wd in widths),
        out_shape=tuple(_sds((s, wd), dt) for wd, dt in zip(widths, dtypes)),
        compiler_params=_cp("parallel"))(x, g.reshape(1, d), w, q_norm.reshape(1, -1), kv_norm.reshape(1, -1), *t32, *t64)


def _prep_bwd(proj, q_norm, kv_norm, t32, t64, dcqn, dckvn, dkpe, db, dc):
    s = proj.shape[0]
    ts = _tile(s, (256, 128))

    def body(p_ref, qn_ref, kn_ref, c32, a32, b32, c64, a64, b64, dcqn_ref, dckvn_ref, dkpe_ref, *rest):
        db_refs, dc_refs = rest[0:9], rest[9:12]
        dp_ref, dqn_ref, dkn_ref = rest[12:15]
        first = pl.program_id(0) == 0
        dx, dg = _rms_bwd_rows(p_ref[:, 0:Q_LORA], qn_ref[...], dcqn_ref[...])
        dp_ref[:, 0:Q_LORA] = dx.astype(BF16)
        _accum(dqn_ref, dg, first)
        dx, dg = _rms_bwd_rows(p_ref[:, COL_CKV:COL_KPE], kn_ref[...], dckvn_ref[...])
        dp_ref[:, COL_CKV:COL_KPE] = dx.astype(BF16)
        _accum(dkn_ref, dg, first)
        g = pltpu.roll(_rope_t(dkpe_ref[...], c32[...], a32[...], b32[...], QK_ROPE // 2), LANE - HEAD_DIM, 1)
        lane = lax.broadcasted_iota(jnp.int32, g.shape, 1)
        dp_ref[:, COL_KPE:COL_B] = jnp.where(lane < QK_ROPE, g, 0.0).astype(BF16)
        for which in range(3):
            for b in range(3):
                sl = slice(b * LANE, (b + 1) * LANE)
                g = db_refs[which][:, sl] + db_refs[3 + which][:, sl] + db_refs[6 + which][:, sl]
                if which < 2:
                    g = _rope_t(g, c64[...], a64[...], b64[...], HEAD_DIM // 2)
                if which == 0:
                    g = g * SCALE_BC
                col = COL_B + which * WIDTH_AB + b * LANE
                dp_ref[:, col:col + LANE] = g.astype(BF16)
        dp_ref[:, COL_C:COL_C + WIDTH_C] = (dc_refs[0][...] * SCALE_BC).astype(BF16)
        dp_ref[:, COL_C + WIDTH_C:COL_C + 2 * WIDTH_C] = dc_refs[1][...].astype(BF16)
        dp_ref[:, COL_C + 2 * WIDTH_C:PROJ_W] = dc_refs[2][...].astype(BF16)

    tab = [_row_spec(ts, LANE)] * 6
    in_specs = ([_row_spec(ts, PROJ_W), _fix_spec(Q_LORA), _fix_spec(KV_LORA)] + tab
                + [_row_spec(ts, Q_LORA), _row_spec(ts, KV_LORA), _row_spec(ts, LANE)]
                + [_row_spec(ts, WIDTH_AB)] * 9 + [_row_spec(ts, WIDTH_C)] * 3)
    return pl.pallas_call(
        body, name="prep_bwd", grid=(s // ts,), in_specs=in_specs,
        out_specs=(_row_spec(ts, PROJ_W), _fix_spec(Q_LORA), _fix_spec(KV_LORA)),
        out_shape=(_sds((s, PROJ_W), BF16), _sds((1, Q_LORA), F32), _sds((1, KV_LORA), F32)),
        compiler_params=_cp("arbitrary"))(proj, q_norm.reshape(1, -1), kv_norm.reshape(1, -1), *t32, *t64,
                                          dcqn, dckvn, dkpe, *db, *dc)


def _a_post_fwd(cqn, ckvn, w_uq, w_ukv, kpe, t32):
    s = cqn.shape[0]
    ts = _tile(s, (512, 256, 128))

    def body(cq_ref, ckv_ref, wq_ref, wkv_ref, kpe_ref, c32, a32, b32, q_ref, k_ref, kt_ref, v_ref):
        qa = jnp.dot(cq_ref[...], wq_ref[...], preferred_element_type=F32)
        kva = jnp.dot(ckv_ref[...], wkv_ref[...], preferred_element_type=F32)
        kpe = kpe_ref[...].astype(F32)
        for h in range(HEADS_A):
            hb = slice(h * LANE, (h + 1) * LANE)
            q_ref[:, hb] = _rope(qa[:, hb], c32[...], a32[...], b32[...], QK_ROPE // 2).astype(BF16)
            kh = kva[:, hb] + kpe
            k_ref[:, hb] = kh.astype(BF16)
            kt_ref[hb, :] = kh.T.astype(BF16)
        for p in range(3):
            lo, hi = 2 * p * LANE, (2 * p + 1) * LANE
            v_ref[:, lo:hi] = kva[:, W_A2 + p * LANE:W_A2 + (p + 1) * LANE].astype(BF16)
            v_ref[:, hi:hi + LANE] = jnp.ones((ts, LANE), BF16)

    whole = lambda shape: pl.BlockSpec(shape, lambda i: (0, 0))
    return pl.pallas_call(
        body, name="a_post_fwd", grid=(s // ts,),
        in_specs=[_row_spec(ts, Q_LORA), _row_spec(ts, KV_LORA), whole(w_uq.shape), whole(w_ukv.shape), _row_spec(ts, LANE)]
        + [_row_spec(ts, LANE)] * 3,
        out_specs=(_row_spec(ts, W_A2), _row_spec(ts, W_A2), pl.BlockSpec((W_A2, ts), lambda i: (0, i)), _row_spec(ts, W_A2)),
        out_shape=(_sds((s, W_A2), BF16), _sds((s, W_A2), BF16), _sds((W_A2, s), BF16), _sds((s, W_A2), BF16)),
        compiler_params=_cp("parallel"))(cqn, ckvn, w_uq, w_ukv, kpe, *t32)


def _a_post_bwd(dqa2_t, dka2, dva, t32):
    s = dka2.shape[0]
    ts = _tile(s, (512, 256, 128))

    def body(dqt_ref, dk_ref, dv_ref, c32, a32, b32, dqa_ref, dkva_ref, dkpe_ref):
        acc = None
        for h in range(HEADS_A):
            hb = slice(h * LANE, (h + 1) * LANE)
            dqa_ref[:, hb] = _rope_t(dqt_ref[hb, :].T * SCALE_A, c32[...], a32[...], b32[...], QK_ROPE // 2).astype(BF16)
            part = dk_ref[:, hb] * SCALE_A
            dkva_ref[:, hb] = part.astype(BF16)
            acc = part if acc is None else acc + part
        dkva_ref[:, W_A2:W_KV] = dv_ref[...].astype(BF16)
        dkpe_ref[...] = acc

    return pl.pallas_call(
        body, name="a_post_bwd", grid=(s // ts,),
        in_specs=[pl.BlockSpec((W_A2, ts), lambda i: (0, i)), _row_spec(ts, W_A2), _row_spec(ts, WIDTH_AB)]
        + [_row_spec(ts, LANE)] * 3,
        out_specs=(_row_spec(ts, W_A2), _row_spec(ts, W_KV), _row_spec(ts, LANE)),
        out_shape=(_sds((s, W_A2), BF16), _sds((s, W_KV), BF16), _sds((s, LANE), F32)),
        compiler_params=_cp("parallel"))(dqa2_t, dka2, dva, *t32)


def _pair_masks():
    lane = lax.broadcasted_iota(jnp.int32, (1, LANE), 1)
    return lane < HEAD_DIM, lane >= HEAD_DIM


def _nt(a, b):
    return lax.dot_general(a, b, (((1,), (1,)), ((), ())), preferred_element_type=F32)


def _tn(a, b):
    return lax.dot_general(a, b, (((0,), (0,)), ((), ())), preferred_element_type=F32)


def _stack_heads(x):
    m0, m1 = _pair_masks()
    zero = jnp.zeros_like(x)
    return jnp.concatenate([jnp.where(m0, x, zero), jnp.where(m1, x, zero)], axis=0)


def _stack_stat(x):
    return jnp.concatenate([x[:, 0:1], x[:, HEAD_DIM:HEAD_DIM + 1]], axis=0)


def _softmax_pair(q, kk, vv, bias2):
    t = q.shape[0]
    s = _nt(_stack_heads(q), kk) + bias2
    m = jnp.max(s, axis=1, keepdims=True)
    p = jnp.exp(s - m)
    l = jnp.sum(p, axis=1, keepdims=True)
    o2 = jnp.dot(p.astype(BF16), vv, preferred_element_type=F32) / l
    lse2 = m + jnp.log(l)
    lo = _lane_lt64((t, LANE))
    return jnp.where(lo, o2[:t], o2[t:]), jnp.where(lo, lse2[:t], lse2[t:])


def _softmax_pair_bwd(q, kk, vv, do, lse, delta, bias2):
    t = q.shape[0]
    q2, do2 = _stack_heads(q), _stack_heads(do)
    p = jnp.exp(_nt(q2, kk) + bias2 - _stack_stat(lse))
    ds = p * (_nt(do2, vv) - _stack_stat(delta))
    dsb = ds.astype(BF16)
    dq2 = jnp.dot(dsb, kk, preferred_element_type=F32)
    lo = _lane_lt64((t, LANE))
    return jnp.where(lo, dq2[:t], dq2[t:]), _tn(dsb, q2), _tn(p.astype(BF16), do2), ds


DENSE_FWD_TQ = 1024
DENSE_BWD_TQ, DENSE_BWD_TK = 2048, 1024
LOG2E = math.log2(math.e)


def _dense_fwd(qa, ka, va1):
    s = qa.shape[0]
    tq = min(DENSE_FWD_TQ, s)
    c = SCALE_A * LOG2E

    def body(q_ref, k_ref, v_ref, o_ref, lse_ref):
        vv = v_ref[...]
        outs, lses = [], []
        for hh in range(2):
            hs = slice(hh * LANE, (hh + 1) * LANE)
            sc = _nt(q_ref[:, hs], k_ref[:, hs])
            m = jnp.max(sc, axis=1, keepdims=True)
            p = jnp.exp2((sc - m) * c)
            acc = jnp.dot(p.astype(BF16), vv, preferred_element_type=F32)
            l = acc[:, LANE:]
            outs.append(acc[:, :LANE] / l)
            lses.append(m * SCALE_A + jnp.log(l))
        lo = _lane_lt64((tq, LANE))
        o_ref[...] = jnp.where(lo, outs[0], outs[1])
        lse_ref[0] = _stat_rows(jnp.where(lo, lses[0], lses[1]))

    return pl.pallas_call(
        body, name="dense_fwd", grid=(3, s // tq),
        in_specs=[pl.BlockSpec((tq, 2 * LANE), lambda p, i: (i, p)), pl.BlockSpec((s, 2 * LANE), lambda p, i: (0, p)),
                  pl.BlockSpec((s, 2 * LANE), lambda p, i: (0, p))],
        out_specs=(pl.BlockSpec((tq, LANE), lambda p, i: (i, p)), pl.BlockSpec((1, 8, tq), lambda p, i: (p, 0, i))),
        out_shape=(_sds((s, WIDTH_AB), F32), _sds((3, 8, s), F32)),
        compiler_params=_cp("parallel", "parallel"))(qa, ka, va1)


def _stat_rows(lane_dense):
    tr = lane_dense.T
    return jnp.concatenate([tr[0:1, :], tr[HEAD_DIM:HEAD_DIM + 1, :], jnp.zeros((6, tr.shape[1]), F32)], axis=0)


def _dense_bwd(qa, ka, kat, va1, do, lse_rows, delta_rows):
    s = qa.shape[0]
    tq, tk = min(DENSE_BWD_TQ, s), min(DENSE_BWD_TK, s)
    c = SCALE_A * LOG2E

    def body(q_ref, k_ref, kt_ref, v_ref, do_ref, lse_ref, dl_ref, dqt_ref, dk_ref, dv_ref):
        j, i = pl.program_id(1), pl.program_id(2)

        @pl.when((j == 0) & (i == 0))
        def _():
            dqt_ref[...] = jnp.zeros(dqt_ref.shape, F32)

        vv, do_ = v_ref[...], do_ref[...]
        lse_t, dl_t = lse_ref[0] * LOG2E, dl_ref[0]
        vm = _pair_masks()
        cols = pl.ds(pl.multiple_of(i * tq, tq), tq)
        dv = None
        for hh in range(2):
            hs = slice(hh * LANE, (hh + 1) * LANE)
            qh = q_ref[:, hs]
            dom = jnp.where(vm[hh], do_, jnp.zeros_like(do_))
            dpt = _nt(vv, dom)
            pt = jnp.exp2(_nt(k_ref[:, hs], qh) * c - lse_t[hh:hh + 1, :])
            dst = pt * (dpt - dl_t[hh:hh + 1, :])
            pb, dsb = pt.astype(BF16), dst.astype(BF16)
            dv_h = jnp.dot(pb, dom, preferred_element_type=F32)
            dv = dv_h if dv is None else dv + dv_h
            dk_h = jnp.dot(dsb, qh, preferred_element_type=F32)
            dqt_ref[hs, cols] += jnp.dot(kt_ref[hs, :], dsb, preferred_element_type=F32)

            @pl.when(i == 0)
            def _():
                dk_ref[:, hs] = dk_h

            @pl.when(i != 0)
            def _():
                dk_ref[:, hs] += dk_h
        _accum(dv_ref, dv, i == 0)

    st_spec = pl.BlockSpec((1, 8, tq), lambda p, j, i: (p, 0, i))
    return pl.pallas_call(
        body, name="dense_bwd", grid=(3, s // tk, s // tq),
        in_specs=[pl.BlockSpec((tq, 2 * LANE), lambda p, j, i: (i, p)), pl.BlockSpec((tk, 2 * LANE), lambda p, j, i: (j, p)),
                  pl.BlockSpec((2 * LANE, tk), lambda p, j, i: (p, j)), pl.BlockSpec((tk, LANE), lambda p, j, i: (j, 2 * p)),
                  pl.BlockSpec((tq, LANE), lambda p, j, i: (i, p)), st_spec, st_spec],
        out_specs=(pl.BlockSpec((2 * LANE, s), lambda p, j, i: (p, 0)), pl.BlockSpec((tk, 2 * LANE), lambda p, j, i: (j, p)),
                   pl.BlockSpec((tk, LANE), lambda p, j, i: (j, p))),
        out_shape=(_sds((W_A2, s), F32), _sds((s, W_A2), F32), _sds((s, WIDTH_AB), F32)),
        compiler_params=_cp("parallel", "arbitrary", "arbitrary"))(qa, ka, kat, va1, do, lse_rows, delta_rows)


BAND_TILE = 1024
BAND_SUB = 128
QKV_W = 3 * WIDTH_AB


def _band_bias_table():
    row = np.arange(BAND_SUB)[:, None]
    col = np.arange(2 * BAND_SUB)[None, :]
    band = np.abs(row - col + BAND_HALF) <= BAND_HALF
    variants = []
    for idx in range(4):
        ok = band & ((col >= BAND_HALF) | ((idx & 1) == 0)) & ((col < 2 * BAND_SUB - BAND_HALF) | ((idx & 2) == 0))
        one = np.where(ok, 0.0, NEG_INF).astype(np.float32)
        variants.append(np.concatenate([one, one], axis=0))
    return jnp.asarray(np.stack(variants))


def _band_specs(t, n):
    hpt = t // BAND_HALF
    last = n // BAND_HALF - 1
    return [pl.BlockSpec((BAND_HALF, QKV_W), lambda r, i: (jnp.maximum(i * hpt - 1, 0), r)),
            pl.BlockSpec((t, QKV_W), lambda r, i: (i, r)),
            pl.BlockSpec((BAND_HALF, QKV_W), lambda r, i: (jnp.minimum((i + 1) * hpt, last), r)),
            pl.BlockSpec((4, 2 * BAND_SUB, 2 * BAND_SUB), lambda r, i: (0, 0, 0))]


def _band_bias(b_ref, a, nsub, i, nt):
    idx = 0
    if a == 0:
        idx = idx + (i == 0).astype(jnp.int32)
    if a == nsub - 1:
        idx = idx + 2 * (i == nt - 1).astype(jnp.int32)
    return b_ref[idx]


def _band_kv(left, main, right, p):
    kc = slice(WIDTH_AB + p * LANE, WIDTH_AB + (p + 1) * LANE)
    vc = slice(2 * WIDTH_AB + p * LANE, 2 * WIDTH_AB + (p + 1) * LANE)
    return (jnp.concatenate([left[:, kc], main[:, kc], right[:, kc]], axis=0),
            jnp.concatenate([left[:, vc], main[:, vc], right[:, vc]], axis=0))


def _banded_fwd(qkvb, dil, bias):
    s = qkvb.shape[0]
    n = s // dil
    t = min(n, BAND_TILE)
    nsub, nt = t // BAND_SUB, n // t
    view = qkvb.reshape(n, dil * QKV_W)

    def body(left, main, right, b_ref, o_ref, lse_ref):
        i = pl.program_id(1)
        for p in range(3):
            pc = slice(p * LANE, (p + 1) * LANE)
            kk, vv = _band_kv(left, main, right, p)
            for a in range(nsub):
                rows, win = slice(a * BAND_SUB, (a + 1) * BAND_SUB), slice(a * BAND_SUB, (a + 2) * BAND_SUB)
                o, lse = _softmax_pair(main[rows, pc], kk[win], vv[win], _band_bias(b_ref, a, nsub, i, nt))
                o_ref[rows, pc] = o
                lse_ref[rows, pc] = lse

    o_spec = pl.BlockSpec((t, WIDTH_AB), lambda r, i: (i, r))
    o, lse = pl.pallas_call(
        body, name=f"banded_fwd_d{dil}", grid=(dil, nt), in_specs=_band_specs(t, n), out_specs=(o_spec, o_spec),
        out_shape=(_sds((n, dil * WIDTH_AB), F32), _sds((n, dil * WIDTH_AB), F32)),
        compiler_params=_cp("parallel", "parallel"))(view, view, view, bias)
    return o.reshape(s, WIDTH_AB), lse.reshape(s, WIDTH_AB)


def _banded_bwd(qkvb, do, lse, delta, dil, bias):
    s = qkvb.shape[0]
    n = s // dil
    t = min(n, BAND_TILE)
    nsub, nt = t // BAND_SUB, n // t
    view = qkvb.reshape(n, dil * QKV_W)
    side = [a.reshape(n, dil * WIDTH_AB) for a in (do, lse, delta)]

    def body(left, main, right, b_ref, do_ref, lse_ref, dl_ref, dq_ref, dk_ref, dv_ref):
        i = pl.program_id(1)

        @pl.when(i == 0)
        def _():
            dk_ref[...] = jnp.zeros(dk_ref.shape, F32)
            dv_ref[...] = jnp.zeros(dv_ref.shape, F32)

        lrow = pl.multiple_of(jnp.maximum(i * t - BAND_HALF, 0), BAND_HALF)
        rrow = pl.multiple_of(jnp.minimum((i + 1) * t, n - BAND_HALF), BAND_HALF)
        mrow = pl.multiple_of(i * t, BAND_HALF)
        for p in range(3):
            pc = slice(p * LANE, (p + 1) * LANE)
            kk, vv = _band_kv(left, main, right, p)
            parts = []
            for a in range(nsub):
                rows, win = slice(a * BAND_SUB, (a + 1) * BAND_SUB), slice(a * BAND_SUB, (a + 2) * BAND_SUB)
                dq, dk, dv, _ = _softmax_pair_bwd(main[rows, pc], kk[win], vv[win], do_ref[rows, pc], lse_ref[rows, pc],
                                                  dl_ref[rows, pc], _band_bias(b_ref, a, nsub, i, nt))
                dq_ref[rows, pc] = dq
                parts.append((dk, dv))
            for which, ref in ((0, dk_ref), (1, dv_ref)):
                chunks = []
                for c in range(nsub + 1):
                    g = parts[c][which][:BAND_SUB] if c < nsub else None
                    if c >= 1:
                        h = parts[c - 1][which][BAND_SUB:]
                        g = h if g is None else g + h
                    chunks.append(g)
                mid = jnp.concatenate([chunks[0][BAND_HALF:]] + chunks[1:nsub] + [chunks[nsub][:BAND_HALF]], axis=0)
                ref[pl.ds(lrow, BAND_HALF), pc] += chunks[0][:BAND_HALF]
                ref[pl.ds(mrow, t), pc] += mid
                ref[pl.ds(rrow, BAND_HALF), pc] += chunks[nsub][BAND_HALF:]

    q_spec = pl.BlockSpec((t, WIDTH_AB), lambda r, i: (i, r))
    acc_spec = pl.BlockSpec((n, WIDTH_AB), lambda r, i: (0, r))
    shp = _sds((n, dil * WIDTH_AB), F32)
    outs = pl.pallas_call(
        body, name=f"banded_bwd_d{dil}", grid=(dil, nt), in_specs=_band_specs(t, n) + [q_spec, q_spec, q_spec],
        out_specs=(q_spec, acc_spec, acc_spec), out_shape=(shp, shp, shp),
        compiler_params=_cp("parallel", "arbitrary"))(view, view, view, bias, *side)
    return [a.reshape(s, WIDTH_AB) for a in outs]


def _na_geometry(s):
    rows = s // GRID_W
    assert rows >= 2 * NA_ROWS and rows % NA_ROWS == 0
    return rows, rows // NA_ROWS


def _na_row(n, i, rows):
    rq = n * NA_ROWS + i
    rs = jnp.clip(rq - NA_ROWS // 2, 0, rows - NA_ROWS)
    return pl.multiple_of(rs * GRID_W, GRID_W), rs - rq + NA_ROWS - 1


NA_KEYS = NA_ROWS * GRID_W


def _natten_fwd(qkvc, tfull):
    s = qkvc.shape[0]
    rows, nrb = _na_geometry(s)
    tq = NA_ROWS * GRID_W

    def body(q_ref, k_ref, v_ref, t_ref, o_ref, lse_ref):
        n = pl.program_id(1)
        for i in range(NA_ROWS):
            tok, base = _na_row(n, i, rows)
            kk, vv = k_ref[pl.ds(tok, NA_KEYS), :], v_ref[pl.ds(tok, NA_KEYS), :]
            sl = slice(i * GRID_W, (i + 1) * GRID_W)
            bias2 = jnp.concatenate([t_ref[0, base], t_ref[1, base]], axis=0)
            o, lse = _softmax_pair(q_ref[sl, :], kk, vv, bias2)
            o_ref[sl, :] = o
            lse_ref[sl, :] = lse

    o_spec = pl.BlockSpec((tq, LANE), lambda p, n: (n, p))
    return pl.pallas_call(
        body, name="natten_fwd", grid=(2, nrb),
        in_specs=[pl.BlockSpec((tq, LANE), lambda p, n: (n, p)), pl.BlockSpec((s, LANE), lambda p, n: (0, 2 + p)),
                  pl.BlockSpec((s, LANE), lambda p, n: (0, 4 + p)),
                  pl.BlockSpec((2, NA_ROWS, GRID_W, NA_KEYS), lambda p, n: (p, 0, 0, 0))],
        out_specs=(o_spec, o_spec), out_shape=(_sds((s, WIDTH_C), F32), _sds((s, WIDTH_C), F32)),
        compiler_params=_cp("parallel", "parallel"))(qkvc, qkvc, qkvc, tfull)


def _natten_bwd(qkvc, tfull, do, lse, delta):
    s = qkvc.shape[0]
    rows, nrb = _na_geometry(s)
    tq = NA_ROWS * GRID_W

    def body(q_ref, k_ref, v_ref, t_ref, do_ref, lse_ref, dl_ref, dq_ref, dk_ref, dv_ref, dt_ref):
        n = pl.program_id(1)

        @pl.when(n == 0)
        def _():
            dk_ref[...] = jnp.zeros(dk_ref.shape, F32)
            dv_ref[...] = jnp.zeros(dv_ref.shape, F32)
            dt_ref[...] = jnp.zeros(dt_ref.shape, F32)

        for i in range(NA_ROWS):
            tok, base = _na_row(n, i, rows)
            win = pl.ds(tok, NA_KEYS)
            sl = slice(i * GRID_W, (i + 1) * GRID_W)
            bias2 = jnp.concatenate([t_ref[0, base], t_ref[1, base]], axis=0)
            dq, dk, dv, ds = _softmax_pair_bwd(q_ref[sl, :], k_ref[win, :], v_ref[win, :], do_ref[sl, :], lse_ref[sl, :],
                                               dl_ref[sl, :], bias2)
            dq_ref[sl, :] = dq
            dk_ref[win, :] += dk
            dv_ref[win, :] += dv
            dt_ref[0, base] += ds[:GRID_W]
            dt_ref[1, base] += ds[GRID_W:]

    q_spec = pl.BlockSpec((tq, LANE), lambda p, n: (n, p))
    acc_spec = pl.BlockSpec((s, LANE), lambda p, n: (0, p))
    t_spec = pl.BlockSpec((2, NA_ROWS, GRID_W, NA_KEYS), lambda p, n: (p, 0, 0, 0))
    shp = _sds((s, WIDTH_C), F32)
    return pl.pallas_call(
        body, name="natten_bwd", grid=(2, nrb),
        in_specs=[q_spec, pl.BlockSpec((s, LANE), lambda p, n: (0, 2 + p)), pl.BlockSpec((s, LANE), lambda p, n: (0, 4 + p)),
                  t_spec, q_spec, q_spec, q_spec],
        out_specs=(q_spec, acc_spec, acc_spec, t_spec),
        out_shape=(shp, shp, shp, _sds((HEADS_C, NA_ROWS, GRID_W, NA_KEYS), F32)),
        compiler_params=_cp("parallel", "arbitrary"))(qkvc, qkvc, qkvc, tfull, do, lse, delta)


def _rpb_constants():
    p = np.arange(GRID_W)[:, None]
    qc = np.arange(GRID_W)[None, :]
    dc = np.clip(qc - p, -(NA_COLS - 1), NA_COLS - 1) + NA_COLS - 1
    onehot = (dc.reshape(1, -1) == np.arange(32)[:, None]).astype(np.float32)
    c_start = np.clip(p - NA_COLS // 2, 0, GRID_W - NA_COLS)
    col_ok = ((qc >= c_start) & (qc < c_start + NA_COLS)).reshape(1, -1).astype(np.float32)
    a = np.arange(16)[:, None]
    bj = np.arange(64)[None, :]
    row_sel = ((bj // 8 + bj % 8) == a).astype(np.float32)
    return jnp.asarray(onehot), jnp.asarray(col_ok), jnp.asarray(row_sel)


def _rpb_expand(rpb, onehot, col_ok):
    r2 = jnp.pad(rpb.reshape(HEADS_C * 15, 31), ((0, 4), (0, 1)))

    def body(r_ref, oh_ref, ok_ref, o_ref):
        t = jnp.dot(r_ref[...], oh_ref[...], preferred_element_type=F32, precision=lax.Precision.HIGHEST)
        o_ref[...] = jnp.where(ok_ref[...] > 0.5, t, NEG_INF)

    tm = pl.pallas_call(body, name="rpb_expand", out_shape=_sds((64, GRID_W * GRID_W), F32))(r2, onehot, col_ok)
    tm = tm[:HEADS_C * 15].reshape(HEADS_C, 15, GRID_W, GRID_W)
    tfull = jnp.stack([jnp.concatenate([tm[:, base + j] for j in range(NA_ROWS)], axis=-1) for base in range(NA_ROWS)], axis=1)
    return tfull


def _rpb_grad(dtfull, onehot, row_sel):
    g = dtfull.reshape(HEADS_C, NA_ROWS, GRID_W, NA_ROWS, GRID_W).transpose(0, 1, 3, 2, 4).reshape(HEADS_C, 64, GRID_W * GRID_W)

    def body(g_ref, oh_ref, sel_ref, o_ref):
        for h in range(HEADS_C):
            mid = lax.dot_general(g_ref[h], oh_ref[...], (((1,), (1,)), ((), ())), preferred_element_type=F32,
                                  precision=lax.Precision.HIGHEST)
            o_ref[h] = jnp.dot(sel_ref[...], mid, preferred_element_type=F32, precision=lax.Precision.HIGHEST)

    out = pl.pallas_call(body, name="rpb_grad", out_shape=_sds((HEADS_C, 16, 32), F32))(g, onehot, row_sel)
    return out[:, :15, :31]


def _outnorm_fwd(o_a, branch_o, branch_lse, o_c, ga, gb, gc):
    s = o_a.shape[0]
    ts = _tile(s, (512, 256, 128))

    def body(a_ref, o1, o2, o3, l1, l2, l3, c_ref, ga_ref, gb_ref, gc_ref, ob_ref, lse_ref, o_ref):
        la, lb, lc = l1[...], l2[...], l3[...]
        m = jnp.maximum(jnp.maximum(la, lb), lc)
        ea, eb, ec = jnp.exp(la - m), jnp.exp(lb - m), jnp.exp(lc - m)
        den = ea + eb + ec
        o_b = (o1[...] * ea + o2[...] * eb + o3[...] * ec) / den
        ob_ref[...] = o_b
        lse_ref[...] = m + jnp.log(den)
        col = 0
        for x, g in ((a_ref[...], ga_ref), (o_b, gb_ref), (c_ref[...], gc_ref)):
            o_ref[:, col:col + x.shape[1]] = (x * _rstd(x) * g[...]).astype(BF16)
            col += x.shape[1]

    sp = _row_spec(ts, WIDTH_AB)
    return pl.pallas_call(
        body, name="outnorm_fwd", grid=(s // ts,),
        in_specs=[sp] * 7 + [_row_spec(ts, WIDTH_C), _fix_spec(WIDTH_AB), _fix_spec(WIDTH_AB), _fix_spec(WIDTH_C)],
        out_specs=(sp, sp, _row_spec(ts, D_MODEL)),
        out_shape=(_sds((s, WIDTH_AB), F32), _sds((s, WIDTH_AB), F32), _sds((s, D_MODEL), BF16)),
        compiler_params=_cp("parallel"))(o_a, *branch_o, *branch_lse, o_c, ga.reshape(1, -1), gb.reshape(1, -1),
                                         gc.reshape(1, -1))


def _outnorm_bwd(dxb, w_out, o_a, o_b, o_c, ga, gb, gc):
    s = o_a.shape[0]
    ts = _tile(s, (512, 256, 128))

    def body(dx_ref, w_ref, a_ref, b_ref, c_ref, ga_ref, gb_ref, gc_ref, *outs):
        first = pl.program_id(0) == 0
        dm = lax.dot_general(dx_ref[...], w_ref[...], (((1,), (1,)), ((), ())), preferred_element_type=F32)
        col = 0
        for k, (ref, g) in enumerate(((a_ref, ga_ref), (b_ref, gb_ref), (c_ref, gc_ref))):
            x = ref[...]
            w = x.shape[1]
            dx, dg = _rms_bwd_rows(x, g[...], dm[:, col:col + w])
            col += w
            outs[k][...] = dx.astype(BF16)
            for b, blk in enumerate(_group_sum(dx * x)):
                if k == 0:
                    outs[3][b] = _stat_rows(blk)
                else:
                    outs[3 + k][:, b * LANE:(b + 1) * LANE] = blk
            _accum(outs[6 + k], dg, first)

    widths = (WIDTH_AB, WIDTH_AB, WIDTH_C)
    return pl.pallas_call(
        body, name="outnorm_bwd", grid=(s // ts,),
        in_specs=[_row_spec(ts, D_MODEL), pl.BlockSpec(w_out.shape, lambda i: (0, 0))] + [_row_spec(ts, w) for w in widths]
        + [_fix_spec(w) for w in widths],
        out_specs=tuple([_row_spec(ts, w) for w in widths] + [pl.BlockSpec((3, 8, ts), lambda i: (0, 0, i))]
                        + [_row_spec(ts, w) for w in widths[1:]] + [_fix_spec(w) for w in widths]),
        out_shape=tuple([_sds((s, w), BF16) for w in widths] + [_sds((3, 8, s), F32)]
                        + [_sds((s, w), F32) for w in widths[1:]] + [_sds((1, w), F32) for w in widths]),
        compiler_params=_cp("arbitrary"))(dxb, w_out, o_a, o_b, o_c, ga.reshape(1, -1), gb.reshape(1, -1),
                                          gc.reshape(1, -1))


def _adamw(w, g, m, v, *, name):
    r, c = w.shape
    tr = _tile(r, (512, 256, 128, 64, 8))

    def body(w_ref, g_ref, m_ref, v_ref, d_ref, nm_ref, nv_ref):
        gv = g_ref[...]
        nm = ADAM_B1 * m_ref[...] + (1.0 - ADAM_B1) * gv
        nv = ADAM_B2 * v_ref[...] + (1.0 - ADAM_B2) * jnp.square(gv)
        m_hat = nm / (1.0 - ADAM_B1 ** ADAM_STEP)
        v_hat = nv / (1.0 - ADAM_B2 ** ADAM_STEP)
        d_ref[...] = -ADAM_LR * (m_hat / (jnp.sqrt(v_hat) + ADAM_EPS) + ADAM_WD * w_ref[...])
        nm_ref[...] = nm
        nv_ref[...] = nv

    sp = _row_spec(tr, c)
    return pl.pallas_call(
        body, name=name, grid=(r // tr,), in_specs=[sp] * 4, out_specs=(sp, sp, sp),
        out_shape=(_sds((r, c), F32),) * 3, compiler_params=_cp("parallel"))(w, g, m, v)


def _add_n(parts, *, name, out_dtype):
    r, c = parts[0].shape
    tr = max(t for t in range(16, 1025, 16) if r % t == 0)

    def body(*refs):
        acc = refs[0][...].astype(F32)
        for ref in refs[1:-1]:
            acc = acc + ref[...].astype(F32)
        refs[-1][...] = acc.astype(out_dtype)

    sp = _row_spec(tr, c)
    return pl.pallas_call(
        body, name=name, grid=(r // tr,), in_specs=[sp] * len(parts), out_specs=sp, out_shape=_sds((r, c), out_dtype),
        compiler_params=_cp("parallel"))(*parts)


ANY = pl.BlockSpec(memory_space=pl.ANY)
CHIP_FLIPS = ((1, 0), (0, 1), (1, 1))


def _me():
    return lax.axis_index("x"), lax.axis_index("y"), lax.axis_index("c")


def _gather_chips(half):
    def body(src, out, send_sems, recv_sems):
        x, y, c = _me()
        me, mine = (x, y, c), 2 * x + y
        chip_x, chip_y, chip_d = 2 * (1 - x) + y, 2 * x + (1 - y), 2 * (1 - x) + (1 - y)
        passed_chip = 2 * (x ^ (1 - c)) + (y ^ c)
        pass_to = (x ^ c, y ^ (1 - c), c)

        def copy(k, chip, half_idx, to, source=None):
            dst = out.at[chip, half_idx]
            return pltpu.make_async_remote_copy(src_ref=dst if source is None else source, dst_ref=dst,
                                                send_sem=send_sems.at[k], recv_sem=recv_sems.at[k], device_id=to,
                                                device_id_type=MESH_T)

        sends = [copy(0, mine, c, (1 - x, y, c), source=src), copy(1, mine, c, (x, 1 - y, c), source=src)]
        for cp in sends:
            cp.start()
        copy(0, chip_x, c, me).wait_recv()
        copy(1, chip_y, c, me).wait_recv()
        sends += [copy(2, passed_chip, c, pass_to), copy(3, chip_x, c, (x, y, 1 - c)), copy(4, chip_y, c, (x, y, 1 - c))]
        for cp in sends[2:]:
            cp.start()
        copy(2, chip_d, c, me).wait_recv()
        sends.append(copy(5, chip_d, c, (x, y, 1 - c)))
        sends[-1].start()
        for k, chip in ((3, chip_x), (4, chip_y), (5, chip_d)):
            copy(k, chip, 1 - c, me).wait_recv()
        for cp in sends:
            cp.wait_send()

    return pl.pallas_call(
        body, name="gather_chips", in_specs=[ANY], out_specs=ANY, out_shape=_sds((4, 2) + half.shape, half.dtype),
        scratch_shapes=[pltpu.SemaphoreType.DMA((6,)), pltpu.SemaphoreType.DMA((6,))])(half)


def _swap_sibling(block):
    def body(src, out, send_sem, recv_sem):
        x, y, c = _me()
        cp = pltpu.make_async_remote_copy(src_ref=src, dst_ref=out, send_sem=send_sem, recv_sem=recv_sem,
                                          device_id=(x, y, 1 - c), device_id_type=MESH_T)
        cp.start()
        cp.wait()

    return pl.pallas_call(
        body, name="swap_sibling", in_specs=[ANY], out_specs=ANY, out_shape=_sds(block.shape, block.dtype),
        scratch_shapes=[pltpu.SemaphoreType.DMA(()), pltpu.SemaphoreType.DMA(())])(block)


def _swap_other_half(halves):
    def body(src, out, send_sem, recv_sem):
        x, y, c = _me()
        cp = pltpu.make_async_remote_copy(src_ref=src.at[:, 1 - c], dst_ref=out, send_sem=send_sem, recv_sem=recv_sem,
                                          device_id=(x, y, 1 - c), device_id_type=MESH_T)
        cp.start()
        cp.wait()

    shape = (halves.shape[0],) + halves.shape[2:]
    return pl.pallas_call(
        body, name="swap_other_half", in_specs=[ANY], out_specs=ANY, out_shape=_sds(shape, halves.dtype),
        scratch_shapes=[pltpu.SemaphoreType.DMA(()), pltpu.SemaphoreType.DMA(())])(halves)


def _pair_sum(halves, core, other):
    n, _, h, c = halves.shape
    tr = max(t for t in range(16, 1025, 16) if h % t == 0)

    def body(core_ref, a_ref, b_ref, o_ref):
        o_ref[...] = (a_ref[...] + b_ref[...]).astype(BF16)

    grid_spec = pltpu.PrefetchScalarGridSpec(
        num_scalar_prefetch=1, grid=(n, h // tr),
        in_specs=[pl.BlockSpec((None, None, tr, c), lambda j, i, core_ref: (j, core_ref[0], i, 0)),
                  pl.BlockSpec((None, tr, c), lambda j, i, core_ref: (j, i, 0))],
        out_specs=pl.BlockSpec((None, tr, c), lambda j, i, core_ref: (j, i, 0)))
    return pl.pallas_call(
        body, name="pair_sum", grid_spec=grid_spec, out_shape=_sds((n, h, c), BF16),
        compiler_params=_cp("parallel", "parallel"))(core.reshape(1).astype(jnp.int32), halves, other)


def _scatter_chips(parts):
    def body(src, out, send_sems, recv_sems):
        x, y, c = _me()
        mine = 2 * x + y
        sends = []
        for k, (fx, fy) in enumerate(CHIP_FLIPS):
            theirs = 2 * (x ^ fx) + (y ^ fy)
            cp = pltpu.make_async_remote_copy(src_ref=src.at[theirs], dst_ref=out.at[mine], send_sem=send_sems.at[k],
                                              recv_sem=recv_sems.at[k], device_id=(x ^ fx, y ^ fy, c), device_id_type=MESH_T)
            cp.start()
            sends.append(cp)
        for k, (fx, fy) in enumerate(CHIP_FLIPS):
            theirs = 2 * (x ^ fx) + (y ^ fy)
            pltpu.make_async_remote_copy(src_ref=src.at[theirs], dst_ref=out.at[theirs], send_sem=send_sems.at[k],
                                         recv_sem=recv_sems.at[k], device_id=(x ^ fx, y ^ fy, c),
                                         device_id_type=MESH_T).wait_recv()
        for cp in sends:
            cp.wait_send()

    return pl.pallas_call(
        body, name="scatter_chips", in_specs=[ANY], out_specs=ANY, out_shape=_sds(parts.shape, parts.dtype),
        scratch_shapes=[pltpu.SemaphoreType.DMA((3,)), pltpu.SemaphoreType.DMA((3,))])(parts)


def _all_reduce_small(block):
    r, c = block.shape

    def body(src, out, slots, send_sems, recv_sems):
        x, y, cc = _me()
        mine = 4 * x + 2 * y + cc
        slots[mine] = src[...]
        sends = []
        for k in range(1, 8):
            fx, fy, fc = (k >> 2) & 1, (k >> 1) & 1, k & 1
            cp = pltpu.make_async_remote_copy(src_ref=src, dst_ref=slots.at[mine], send_sem=send_sems.at[k - 1],
                                              recv_sem=recv_sems.at[k - 1], device_id=(x ^ fx, y ^ fy, cc ^ fc),
                                              device_id_type=MESH_T)
            cp.start()
            sends.append(cp)
        for k in range(1, 8):
            fx, fy, fc = (k >> 2) & 1, (k >> 1) & 1, k & 1
            theirs = 4 * (x ^ fx) + 2 * (y ^ fy) + (cc ^ fc)
            pltpu.make_async_remote_copy(src_ref=src, dst_ref=slots.at[theirs], send_sem=send_sems.at[k - 1],
                                         recv_sem=recv_sems.at[k - 1], device_id=(x ^ fx, y ^ fy, cc ^ fc),
                                         device_id_type=MESH_T).wait_recv()
        for cp in sends:
            cp.wait_send()
        acc = slots[0]
        for d in range(1, 8):
            acc = acc + slots[d]
        out[...] = acc

    vm = pl.BlockSpec(memory_space=pltpu.VMEM)
    return pl.pallas_call(
        body, name="all_reduce_small", in_specs=[vm], out_specs=vm, out_shape=_sds((r, c), F32),
        scratch_shapes=[pltpu.VMEM((8, r, c), F32), pltpu.SemaphoreType.DMA((7,)), pltpu.SemaphoreType.DMA((7,))])(block)


DIRECT = ("w_mlp_in", "w_mlp_out", "w_out")
BIG = DIRECT + ("w_in", "w_uq", "w_ukv")
COL_SHARDED = {"w_in": True, "w_uq": True, "w_ukv": True, "w_out": False, "w_mlp_in": True, "w_mlp_out": False}
SMALL = ("g_mix", "q_norm", "kv_norm", "rpb", "out_norm_a", "out_norm_b", "out_norm_c", "g_mlp", "g_final")
PACK_C = 1024
ROW_ALIGN = 32


def _pack_rows(parts):
    flat = jnp.concatenate([p.reshape(-1, PACK_C) for p in parts], axis=0)
    return jnp.pad(flat, ((0, -flat.shape[0] % ROW_ALIGN), (0, 0)))


def _unpack_rows(flat, shapes):
    out, row = [], 0
    for shp in shapes:
        n = int(np.prod(shp)) // PACK_C
        out.append(flat[row:row + n].reshape(shp))
        row += n
    return out


def _full_from_shards(name, g):
    if COL_SHARDED[name]:
        return g.transpose(1, 2, 0, 3).reshape(g.shape[1], g.shape[2], 4 * g.shape[3])
    return g.transpose(1, 0, 2, 3).reshape(g.shape[1], 4 * g.shape[2], g.shape[3])


def _shards_from_full(name, w):
    l, k, n = w.shape
    if COL_SHARDED[name]:
        return w.reshape(l, k, 4, n // 4).transpose(2, 0, 1, 3)
    return w.reshape(l, 4, k // 4, n).transpose(1, 0, 2, 3)


def _arrange_w_in(w):
    z = jnp.zeros(w.shape[:-1] + (COL_B - COL_KPE - QK_ROPE,), w.dtype)
    return jnp.concatenate([w[..., :COL_KPE + QK_ROPE], z, w[..., COL_KPE + QK_ROPE:]], axis=-1)


def _unarrange_w_in(w):
    return jnp.concatenate([w[..., :COL_KPE + QK_ROPE], w[..., COL_B:]], axis=-1)


def _arrange_w_uq(w):
    per = HEAD_DIM + QK_ROPE
    z = jnp.zeros(w.shape[:-1] + (LANE - per,), w.dtype)
    cols = []
    for h in range(HEADS_A):
        cols += [w[..., h * per:(h + 1) * per], z]
    return jnp.concatenate(cols, axis=-1)


def _unarrange_w_uq(w):
    per = HEAD_DIM + QK_ROPE
    return jnp.concatenate([w[..., h * LANE:h * LANE + per] for h in range(HEADS_A)], axis=-1)


def _arrange_w_ukv(w):
    z = jnp.zeros(w.shape[:-1] + (HEAD_DIM,), w.dtype)
    ks = []
    for h in range(HEADS_A):
        ks += [w[..., h * LANE:h * LANE + HEAD_DIM], z]
    vs = [w[..., h * LANE + HEAD_DIM:(h + 1) * LANE] for h in range(HEADS_A)]
    return jnp.concatenate(ks + vs, axis=-1)


def _unarrange_w_ukv(w):
    cols = []
    for h in range(HEADS_A):
        cols += [w[..., h * LANE:h * LANE + HEAD_DIM], w[..., W_A2 + h * HEAD_DIM:W_A2 + (h + 1) * HEAD_DIM]]
    return jnp.concatenate(cols, axis=-1)


def _layer_fwd(x, w, sm, tabs, consts):
    t32, t64 = tabs
    onehot, col_ok, _, band = consts
    h, proj, cqn, ckvn, kpe, qkvb, qkvc = _in_proj(x, sm["g_mix"], w["w_in"], sm["q_norm"], sm["kv_norm"], t32, t64)
    qa2, ka2, kat, va1 = _a_post_fwd(cqn, ckvn, w["w_uq"], w["w_ukv"], kpe, t32)
    o_a, lse_a = _dense_fwd(qa2, ka2, va1)
    branch = [_banded_fwd(qkvb, dil, band) for _, dil in DILATED_PAIRS]
    tfull = _rpb_expand(sm["rpb"], onehot, col_ok)
    o_c, lse_c = _natten_fwd(qkvc, tfull)
    o_b, lse_b, mixed = _outnorm_fwd(o_a, [b[0] for b in branch], [b[1] for b in branch], o_c, sm["out_norm_a"],
                                     sm["out_norm_b"], sm["out_norm_c"])
    x_mid = _mm_nn(mixed, w["w_out"], name="out_proj", res=x)
    h2, act = _mlp_in(x_mid, sm["g_mlp"], w["w_mlp_in"])
    x_out = _mm_nn(act, w["w_mlp_out"], name="mlp_out", res=x_mid)
    saved = dict(x=x, h=h, proj=proj, cqn=cqn, ckvn=ckvn, qkvb=qkvb, qkvc=qkvc, qa2=qa2, ka2=ka2, kat=kat, va1=va1, o_a=o_a,
                 lse_a=lse_a, o_b=o_b, lse_b=lse_b, o_c=o_c, lse_c=lse_c, tfull=tfull, mixed=mixed, x_mid=x_mid, h2=h2,
                 act=act)
    return x_out, saved


def _layer_bwd(dx, dxb, sv, w, sm, tabs, consts, packed, places):
    t32, t64 = tabs
    onehot, _, row_sel, band = consts
    g = {}
    du = _mm_nt(dxb, w["w_mlp_out"], name="mlp_out_dx", out_dtype=BF16, relu2_act=sv["act"])
    packed = _mm_tn(sv["act"], dxb, name="mlp_out_dw", packed=(packed,) + places["w_mlp_out"])
    dx_mid, dmb, g["g_mlp"] = _mm_nt_norm_bwd(du, w["w_mlp_in"], sv["x_mid"], sm["g_mlp"], dx, name="mlp_in_dx")
    packed = _mm_tn(sv["h2"], du, name="mlp_in_dw", packed=(packed,) + places["w_mlp_in"])
    packed = _mm_tn(sv["mixed"], dmb, name="out_proj_dw", packed=(packed,) + places["w_out"])
    (do_a, do_b, do_c, dl_a, dl_b, dl_c, g["out_norm_a"], g["out_norm_b"], g["out_norm_c"]) = _outnorm_bwd(
        dmb, w["w_out"], sv["o_a"], sv["o_b"], sv["o_c"], sm["out_norm_a"], sm["out_norm_b"], sm["out_norm_c"])
    dqa2_t, dka2, dva = _dense_bwd(sv["qa2"], sv["ka2"], sv["kat"], sv["va1"], do_a, sv["lse_a"], dl_a)
    db = []
    for _, dil in DILATED_PAIRS:
        db += _banded_bwd(sv["qkvb"], do_b, sv["lse_b"], dl_b, dil, band)
    dq_c, dk_c, dv_c, dtfull = _natten_bwd(sv["qkvc"], sv["tfull"], do_c, sv["lse_c"], dl_c)
    g["rpb"] = _rpb_grad(dtfull, onehot, row_sel)
    dqa, dkva, dkpe = _a_post_bwd(dqa2_t, dka2, dva, t32)
    dcqn = _mm_nt(dqa, w["w_uq"], name="q_up_dx")
    g["w_uq"] = _unarrange_w_uq(_mm_tn(sv["cqn"], dqa, name="q_up_dw"))
    dckvn = _mm_nt(dkva, w["w_ukv"], name="kv_up_dx")
    g["w_ukv"] = _unarrange_w_ukv(_mm_tn(sv["ckvn"], dkva, name="kv_up_dw"))
    dproj, g["q_norm"], g["kv_norm"] = _prep_bwd(sv["proj"], sm["q_norm"], sm["kv_norm"], t32, t64, dcqn, dckvn, dkpe,
                                                  db, (dq_c, dk_c, dv_c))
    g["w_in"] = _unarrange_w_in(_mm_tn(sv["h"], dproj, name="in_proj_dw"))
    dx_in, dxb_in, g["g_mix"] = _mm_nt_norm_bwd(dproj, w["w_in"], sv["x"], sm["g_mix"], dx_mid, name="in_proj_dx")
    return dx_in, dxb_in, g, packed


def _packed_places(offs, l):
    d, r = D_MODEL, D_MODEL // 4
    return {"w_mlp_in": (512, lambda i, j: (j, (offs["w_mlp_in"] + l * d) // 512 + i)),
            "w_mlp_out": (512, lambda i, j: (i // 2, (offs["w_mlp_out"] + l * d) // 512 + i % 2)),
            "w_out": (r, lambda i, j: (i, (offs["w_out"] + l * r) // r))}


def _local_step(x, target, wfull, small, packed_shape, offs):
    s = x.shape[0]
    tabs = (_rope_tables(s, QK_ROPE // 2, 1, lead=HEAD_DIM), _rope_tables(s, HEAD_DIM // 2, 2))
    consts = _rpb_constants() + (_band_bias_table(),)
    saved = []
    for l in range(DEPTH):
        wl = {k: v[l] for k, v in wfull.items()}
        sl = {k: small[k][l] for k in SMALL if k != "g_final"}
        x, sv = _layer_fwd(x, wl, sl, tabs, consts)
        saved.append(sv)
    loss, dx, dxb, dg_final = _loss_head(x, small["g_final"], target)
    grads = [None] * DEPTH
    packed = packed_shape
    for l in reversed(range(DEPTH)):
        wl = {k: v[l] for k, v in wfull.items()}
        sl = {k: small[k][l] for k in SMALL if k != "g_final"}
        dx, dxb, grads[l], packed = _layer_bwd(dx, dxb, saved[l], wl, sl, tabs, consts, packed, _packed_places(offs, l))
    return loss, dx, grads, dg_final, packed


ARRANGE = {"w_in": _arrange_w_in, "w_uq": _arrange_w_uq, "w_ukv": _arrange_w_ukv}


def kernel(x, g_mix, w_in, q_norm, w_uq, kv_norm, w_ukv, rpb, out_norm_a, out_norm_b, out_norm_c, w_out, g_mlp, w_mlp_in, w_mlp_out, g_final, loss_target, m_g_mix, m_w_in, m_q_norm, m_w_uq, m_kv_norm, m_w_ukv, m_rpb, m_out_norm_a, m_out_norm_b, m_out_norm_c, m_w_out, m_g_mlp, m_w_mlp_in, m_w_mlp_out, m_g_final, v_g_mix, v_w_in, v_q_norm, v_w_uq, v_kv_norm, v_w_ukv, v_rpb, v_out_norm_a, v_out_norm_b, v_out_norm_c, v_w_out, v_g_mlp, v_w_mlp_in, v_w_mlp_out, v_g_final):
    args = dict(locals())
    weights = {k: args[k] for k in BIG + SMALL}
    moms = {k: args["m_" + k] for k in BIG + SMALL}
    vels = {k: args["v_" + k] for k in BIG + SMALL}
    cc = lax.axis_index("c")
    my_chip = 2 * lax.axis_index("x") + lax.axis_index("y")

    shard_shapes = [weights[k].shape for k in BIG]
    packed_w = _pack_rows([weights[k].astype(BF16) for k in BIG])
    rows = packed_w.shape[0]
    my_half = lax.dynamic_index_in_dim(packed_w.reshape(2, rows // 2, PACK_C), cc, axis=0, keepdims=False)
    gathered = _gather_chips(my_half).reshape(4, rows, PACK_C)
    per_chip = [_unpack_rows(jnp.where(my_chip == j, packed_w, gathered[j]), shard_shapes) for j in range(4)]
    wfull = {}
    for idx, k in enumerate(BIG):
        full = _full_from_shards(k, jnp.stack([per_chip[j][idx] for j in range(4)]))
        wfull[k] = ARRANGE[k](full) if k in ARRANGE else full

    small = {k: weights[k] for k in SMALL}
    offs, row = {}, 0
    for k, shp in zip(BIG, shard_shapes):
        offs[k] = row
        row += int(np.prod(shp)) // PACK_C
    loss, dx, grads, dg_final, packed = _local_step(x[0], loss_target[0], wfull, small, _sds((4, rows, PACK_C), F32), offs)

    small_local = {k: jnp.stack([grads[l][k].reshape(weights[k].shape[1:]) for l in range(DEPTH)])
                   for k in SMALL if k != "g_final"}
    small_local["g_final"] = dg_final.reshape(-1)
    small_shapes = [weights[k].shape for k in SMALL]
    n_small = sum(int(np.prod(s)) for s in small_shapes)
    flat = jnp.concatenate([small_local[k].reshape(-1) for k in SMALL] + [loss[0, :1]])
    rows_small = -(-(n_small + 1) // PACK_C)
    rows_small += -rows_small % 8
    flat = jnp.pad(flat, (0, rows_small * PACK_C - n_small - 1)).reshape(rows_small, PACK_C)
    red = _all_reduce_small(flat).reshape(-1)
    loss_out = red[n_small]
    small_grads, off = {}, 0
    for k, shp in zip(SMALL, small_shapes):
        n = int(np.prod(shp))
        small_grads[k] = red[off:off + n].reshape(shp)
        off += n

    rest = [k for k in BIG if k not in DIRECT]
    by_shard = {k: _shards_from_full(k, jnp.stack([grads[l][k] for l in range(DEPTH)])) for k in rest}
    tail = jnp.stack([_pack_rows([by_shard[k][j] for k in rest]) for j in range(4)])
    assert offs[rest[0]] + tail.shape[1] == rows
    packed = lax.dynamic_update_slice(packed, tail, (0, offs[rest[0]], 0))
    halves = packed.reshape(4, 2, rows // 2, PACK_C)
    pair = _pair_sum(halves, cc, _swap_other_half(halves))
    by_chip = _scatter_chips(pair)
    reduced = _add_n([jnp.where(my_chip == j, pair[j], by_chip[j]) for j in range(4)], name="chip_sum", out_dtype=F32)
    theirs = _swap_sibling(reduced)
    joined = jnp.where(cc == 0, jnp.concatenate([reduced, theirs]), jnp.concatenate([theirs, reduced]))
    big_grads = dict(zip(BIG, _unpack_rows(joined, shard_shapes)))

    out_g, out_d, out_m, out_v = {}, {}, {}, {}
    for k in BIG:
        shp = weights[k].shape
        two_d = (shp[0] * shp[1], shp[2])
        d, nm, nv = _adamw(weights[k].reshape(two_d), big_grads[k].reshape(two_d), moms[k].reshape(two_d),
                           vels[k].reshape(two_d), name="adamw_" + k)
        out_g[k], out_d[k], out_m[k], out_v[k] = big_grads[k], d.reshape(shp), nm.reshape(shp), nv.reshape(shp)

    def pack_small(tree):
        f = jnp.concatenate([tree[k].reshape(-1) for k in SMALL])
        return jnp.pad(f, (0, rows_small * PACK_C - n_small)).reshape(rows_small, PACK_C)

    d, nm, nv = _adamw(pack_small(small), pack_small(small_grads), pack_small(moms), pack_small(vels), name="adamw_small")
    for tree, flat_out in ((out_d, d), (out_m, nm), (out_v, nv)):
        off = 0
        fo = flat_out.reshape(-1)
        for k, shp in zip(SMALL, small_shapes):
            n = int(np.prod(shp))
            tree[k] = fo[off:off + n].reshape(shp)
            off += n
    out_g.update(small_grads)

    order = ("g_mix", "w_in", "q_norm", "w_uq", "kv_norm", "w_ukv", "rpb", "out_norm_a", "out_norm_b", "out_norm_c", "w_out",
             "g_mlp", "w_mlp_in", "w_mlp_out", "g_final")
    return (loss_out, dx.reshape(x.shape), *[out_g[k] for k in order], *[out_d[k] for k in order],
            *[out_m[k] for k in order], *[out_v[k] for k in order])
```

```python
import math

import numpy as np
import jax
import jax.numpy as jnp
from jax import lax
from jax.experimental import pallas as pl
from jax.experimental.pallas import tpu as pltpu

F32 = jnp.float32
BF16 = jnp.bfloat16

D_MODEL = 1024
HEAD_DIM = 64
Q_LORA = 256
KV_LORA = 128
QK_ROPE = 32
HEADS_A = 6
HEADS_B = 6
HEADS_C = 4
DILATED_PAIRS = ((128, 1), (512, 4), (2048, 16))
BAND_HALF = 64
GRID_W = 64
NA_ROWS = 8
NA_COLS = 16
D_FF = 4096
ROPE_THETA = 10000.0
NORM_EPS = 1e-6
NEG_INF = -1e30
DEPTH = 4

LANE = 128
PROJ_W = 2432
COL_CKV = 256
COL_KPE = 384
COL_B = 512
COL_C = 1664
W_A2 = 768
W_KV = W_A2 + 384
WIDTH_AB = 384
WIDTH_C = 256
SCALE_A = (HEAD_DIM + QK_ROPE) ** -0.5
SCALE_BC = HEAD_DIM ** -0.5

ADAM_LR = 0.001
ADAM_B1 = 0.9
ADAM_B2 = 0.999
ADAM_EPS = 1e-08
ADAM_WD = 0.01
ADAM_STEP = 10

VMEM_LIMIT = 56 * 1024 * 1024
MESH_T = pl.DeviceIdType.MESH


def _cp(*sem):
    return pltpu.CompilerParams(dimension_semantics=sem or None, vmem_limit_bytes=VMEM_LIMIT)


def _tile(n, cands):
    for c in cands:
        if n % c == 0:
            return c
    return n


def _sds(shape, dtype):
    return jax.ShapeDtypeStruct(shape, dtype)


ROW_TILE_BYTES = 12 * 1024 * 1024


def _row_tiles(row_bytes):
    return tuple(t for t in (2048, 1024, 512, 256, 128) if t * row_bytes <= ROW_TILE_BYTES or t <= 512)


def _mm_nn(a, b, *, name, out_dtype=F32, res=None):
    m, k = a.shape
    n = b.shape[1]
    tn = _tile(n, (1024, 768, 512)) if n % LANE == 0 and n != PROJ_W else n
    tm = _tile(m, _row_tiles(2 * k + tn * (jnp.dtype(out_dtype).itemsize + (4 if res is not None else 0))))

    def body(*refs):
        a_ref, b_ref = refs[0], refs[1]
        o_ref = refs[-1]
        acc = jnp.dot(a_ref[...], b_ref[...], preferred_element_type=F32)
        if res is not None:
            acc = refs[2][...] + acc
        o_ref[...] = acc.astype(o_ref.dtype)

    in_specs = [pl.BlockSpec((tm, k), lambda j, i: (i, 0)), pl.BlockSpec((k, tn), lambda j, i: (0, j))]
    args = [a, b]
    if res is not None:
        in_specs.append(pl.BlockSpec((tm, tn), lambda j, i: (i, j)))
        args.append(res)
    return pl.pallas_call(
        body, name=name, grid=(n // tn, m // tm), in_specs=in_specs,
        out_specs=pl.BlockSpec((tm, tn), lambda j, i: (i, j)), out_shape=_sds((m, n), out_dtype),
        compiler_params=_cp("parallel", "parallel"))(*args)


def _mlp_in(x, g, w):
    m, k = x.shape
    n = w.shape[1]
    tm = _tile(m, (256, 128))

    def body(x_ref, g_ref, w_ref, h_ref, o_ref):
        xv = x_ref[...]
        h = (xv * _rstd(xv) * g_ref[...]).astype(BF16)
        h_ref[...] = h
        u = jnp.dot(h, w_ref[...], preferred_element_type=F32)
        o_ref[...] = jnp.square(jnp.maximum(u, 0.0)).astype(BF16)

    return pl.pallas_call(
        body, name="mlp_in", grid=(m // tm,),
        in_specs=[pl.BlockSpec((tm, k), lambda i: (i, 0)), pl.BlockSpec((1, k), lambda i: (0, 0)),
                  pl.BlockSpec((k, n), lambda i: (0, 0))],
        out_specs=(pl.BlockSpec((tm, k), lambda i: (i, 0)), pl.BlockSpec((tm, n), lambda i: (i, 0))),
        out_shape=(_sds((m, k), BF16), _sds((m, n), BF16)),
        compiler_params=_cp("parallel"))(x, g.reshape(1, k), w)


def _mm_nt(a, b, *, name, out_dtype=F32, relu2_act=None):
    m, c = a.shape
    n = b.shape[0]
    tn = _tile(n, (1024, 512, 256, 128))
    tm = _tile(m, _row_tiles(2 * c + tn * (jnp.dtype(out_dtype).itemsize + (2 if relu2_act is not None else 0))))

    def body(*refs):
        a_ref, b_ref = refs[0], refs[1]
        o_ref = refs[-1]
        acc = lax.dot_general(a_ref[...], b_ref[...], (((1,), (1,)), ((), ())), preferred_element_type=F32)
        if relu2_act is not None:
            acc = acc * (2.0 * jnp.sqrt(refs[2][...].astype(F32)))
        o_ref[...] = acc.astype(o_ref.dtype)

    in_specs = [pl.BlockSpec((tm, c), lambda j, i: (i, 0)), pl.BlockSpec((tn, c), lambda j, i: (j, 0))]
    args = [a, b]
    if relu2_act is not None:
        in_specs.append(pl.BlockSpec((tm, tn), lambda j, i: (i, j)))
        args.append(relu2_act)
    return pl.pallas_call(
        body, name=name, grid=(n // tn, m // tm), in_specs=in_specs,
        out_specs=pl.BlockSpec((tm, tn), lambda j, i: (i, j)), out_shape=_sds((m, n), out_dtype),
        compiler_params=_cp("parallel", "parallel"))(*args)


def _mm_nt_norm_bwd(a, b, x, g, res, *, name):
    m, c = a.shape
    d = b.shape[0]
    tm = _tile(m, (512, 256, 128))

    def body(a_ref, b_ref, x_ref, g_ref, res_ref, dx_ref, dxb_ref, dg_ref):
        dy = lax.dot_general(a_ref[...], b_ref[...], (((1,), (1,)), ((), ())), preferred_element_type=F32)
        dx, dg = _rms_bwd_rows(x_ref[...], g_ref[...], dy)
        dx = res_ref[...] + dx
        dx_ref[...] = dx
        dxb_ref[...] = dx.astype(BF16)
        _accum(dg_ref, dg, pl.program_id(0) == 0)

    row = pl.BlockSpec((tm, d), lambda i: (i, 0))
    fix = pl.BlockSpec((1, d), lambda i: (0, 0))
    return pl.pallas_call(
        body, name=name, grid=(m // tm,),
        in_specs=[pl.BlockSpec((tm, c), lambda i: (i, 0)), pl.BlockSpec((d, c), lambda i: (0, 0)), row, fix, row],
        out_specs=(row, row, fix), out_shape=(_sds((m, d), F32), _sds((m, d), BF16), _sds((1, d), F32)),
        compiler_params=_cp("arbitrary"))(a, b, x, g.reshape(1, d), res)


def _mm_tn(a, b, *, name, packed=None):
    m, ka = a.shape
    nb = b.shape[1]
    tka = _tile(ka, (512, 256, 128)) if packed is None else packed[1]
    tnb = _tile(nb, (1024, 768, 512)) if nb != PROJ_W else nb
    tc = _tile(m, (4096, 2048, 1024, 512, 256, 128) if tnb <= PACK_C else (2048, 1024, 512, 256, 128))

    def body(*refs):
        a_ref, b_ref, o_ref = refs[0], refs[1], refs[-1]
        part = lax.dot_general(a_ref[...], b_ref[...], (((0,), (0,)), ((), ())), preferred_element_type=F32)

        @pl.when(pl.program_id(2) == 0)
        def _():
            o_ref[...] = part

        @pl.when(pl.program_id(2) != 0)
        def _():
            o_ref[...] += part

    in_specs = [pl.BlockSpec((tc, tka), lambda i, j, c: (c, i)), pl.BlockSpec((tc, tnb), lambda i, j, c: (c, j))]
    kwargs = dict(out_specs=pl.BlockSpec((tka, tnb), lambda i, j, c: (i, j)), out_shape=_sds((ka, nb), F32))
    args = [a, b]
    if packed is not None:
        buf, _, place = packed
        assert tnb == PACK_C
        kwargs = dict(out_specs=pl.BlockSpec((None, tka, tnb), lambda i, j, c: place(i, j) + (0,)))
        if isinstance(buf, jax.ShapeDtypeStruct):
            kwargs["out_shape"] = buf
        else:
            kwargs.update(out_shape=_sds(buf.shape, buf.dtype), input_output_aliases={2: 0})
            in_specs.append(pl.BlockSpec(memory_space=pl.ANY))
            args.append(buf)
    return pl.pallas_call(
        body, name=name, grid=(ka // tka, nb // tnb, m // tc), in_specs=in_specs,
        compiler_params=_cp("parallel", "parallel", "arbitrary"), **kwargs)(*args)


def _rstd(x):
    return lax.rsqrt(jnp.mean(x * x, axis=-1, keepdims=True) + NORM_EPS)


def _rms_bwd_rows(x, g, dy):
    r = _rstd(x)
    gy = dy * g
    c = jnp.sum(x * gy, axis=-1, keepdims=True) * (r * r * r) * (1.0 / x.shape[-1])
    return r * gy - x * c, jnp.sum(dy * x * r, axis=0, keepdims=True)


def _accum(ref, part, first):
    @pl.when(first)
    def _():
        ref[...] = part

    @pl.when(jnp.logical_not(first))
    def _():
        ref[...] += part


def _rope(x, c, s1, s2, sh):
    return x * c + pltpu.roll(x, LANE - sh, 1) * s1 + pltpu.roll(x, sh, 1) * s2


def _rope_t(g, c, s1, s2, sh):
    return g * c + pltpu.roll(g * s1, sh, 1) + pltpu.roll(g * s2, LANE - sh, 1)


def _rope_tables(s, half, reps, lead=0):
    pos = jnp.arange(s, dtype=F32)
    inv_freq = ROPE_THETA ** (-jnp.arange(half, dtype=F32) / half)
    ang = pos[:, None] * inv_freq[None, :]
    cos, sin = jnp.cos(ang), jnp.sin(ang)
    zero = jnp.zeros_like(cos)
    ones, lead0 = jnp.ones((s, lead), F32), jnp.zeros((s, lead), F32)
    pad = jnp.zeros((s, LANE - lead - 2 * half * reps), F32)
    c = jnp.concatenate([ones] + [cos, cos] * reps + [pad], axis=1)
    s1 = jnp.concatenate([lead0] + [-sin, zero] * reps + [pad], axis=1)
    s2 = jnp.concatenate([lead0] + [zero, sin] * reps + [pad], axis=1)
    return c, s1, s2


def _lane_lt64(shape):
    return lax.broadcasted_iota(jnp.int32, shape, len(shape) - 1) % LANE < HEAD_DIM


def _group_sum(x):
    outs = []
    for b in range(x.shape[1] // LANE):
        blk = x[:, b * LANE:(b + 1) * LANE]
        lo = _lane_lt64(blk.shape)
        s0 = jnp.sum(jnp.where(lo, blk, 0.0), axis=1, keepdims=True)
        s1 = jnp.sum(jnp.where(lo, 0.0, blk), axis=1, keepdims=True)
        outs.append(jnp.where(lo, s0, s1))
    return outs


def _row_spec(ts, w):
    return pl.BlockSpec((ts, w), lambda i: (i, 0))


def _fix_spec(w):
    return pl.BlockSpec((1, w), lambda i: (0, 0))


def _loss_head(x, g, target):
    s, d = x.shape
    ts = _tile(s, (512, 256, 128))

    def body(x_ref, g_ref, t_ref, loss_ref, dx_ref, dxb_ref, dg_ref):
        xv, gv = x_ref[...], g_ref[...]
        err = xv * _rstd(xv) * gv - t_ref[...]
        part = 0.5 * jnp.sum(jnp.sum(err * err, axis=-1, keepdims=True) * (1.0 / d), axis=0, keepdims=True)
        dx, dg = _rms_bwd_rows(xv, gv, err * (1.0 / d))
        dx_ref[...] = dx
        dxb_ref[...] = dx.astype(BF16)
        first = pl.program_id(0) == 0
        _accum(dg_ref, dg, first)
        _accum(loss_ref, jnp.broadcast_to(part, (1, LANE)), first)

    return pl.pallas_call(
        body, name="loss_head", grid=(s // ts,), in_specs=[_row_spec(ts, d), _fix_spec(d), _row_spec(ts, d)],
        out_specs=(_fix_spec(LANE), _row_spec(ts, d), _row_spec(ts, d), _fix_spec(d)),
        out_shape=(_sds((1, LANE), F32), _sds((s, d), F32), _sds((s, d), BF16), _sds((1, d), F32)),
        compiler_params=_cp("arbitrary"))(x, g.reshape(1, d), target)


def _in_proj(x, g, w, q_norm, kv_norm, t32, t64):
    s, d = x.shape
    ts = _tile(s, (256, 128))

    def body(x_ref, g_ref, w_ref, qn_ref, kn_ref, c32, a32, b32, c64, a64, b64,
             h_ref, p_ref, cqn_ref, ckvn_ref, kpe_ref, qkvb_ref, qkvc_ref):
        xv = x_ref[...]
        h = (xv * _rstd(xv) * g_ref[...]).astype(BF16)
        h_ref[...] = h
        p = jnp.dot(h, w_ref[...], preferred_element_type=F32)
        p_ref[...] = p
        cq = p[:, 0:Q_LORA]
        cqn_ref[...] = (cq * _rstd(cq) * qn_ref[...]).astype(BF16)
        ckv = p[:, COL_CKV:COL_KPE]
        ckvn_ref[...] = (ckv * _rstd(ckv) * kn_ref[...]).astype(BF16)
        kp = pltpu.roll(p[:, COL_KPE:COL_B], HEAD_DIM, 1)
        kpe_ref[...] = _rope(kp, c32[...], a32[...], b32[...], QK_ROPE // 2).astype(BF16)
        for b in range(6):
            blk = _rope(p[:, COL_B + b * LANE:COL_B + (b + 1) * LANE], c64[...], a64[...], b64[...], HEAD_DIM // 2)
            if b < 3:
                blk = blk * SCALE_BC
            qkvb_ref[:, b * LANE:(b + 1) * LANE] = blk.astype(BF16)
        qkvb_ref[:, 2 * WIDTH_AB:3 * WIDTH_AB] = p[:, COL_B + 2 * WIDTH_AB:COL_C].astype(BF16)
        qkvc_ref[:, 0:WIDTH_C] = (p[:, COL_C:COL_C + WIDTH_C] * SCALE_BC).astype(BF16)
        qkvc_ref[:, WIDTH_C:3 * WIDTH_C] = p[:, COL_C + WIDTH_C:PROJ_W].astype(BF16)

    tab = [_row_spec(ts, LANE)] * 6
    widths = (d, PROJ_W, Q_LORA, KV_LORA, LANE, 3 * WIDTH_AB, 3 * WIDTH_C)
    dtypes = (BF16, F32, BF16, BF16, BF16, BF16, BF16)
    return pl.pallas_call(
        body, name="in_proj", grid=(s // ts,),
        in_specs=[_row_spec(ts, d), _fix_spec(d), pl.BlockSpec(w.shape, lambda i: (0, 0)), _fix_spec(Q_LORA),
                  _fix_spec(KV_LORA)] + tab,
        out_specs=tuple(_row_spec(ts, wd) for wd in widths),
        out_shape=tuple(_sds((s, wd), dt) for wd, dt in zip(widths, dtypes)),
        compiler_params=_cp("parallel"))(x, g.reshape(1, d), w, q_norm.reshape(1, -1), kv_norm.reshape(1, -1), *t32, *t64)


def _prep_bwd(proj, q_norm, kv_norm, t32, t64, dcqn, dckvn, dkpe, db, dc):
    s = proj.shape[0]
    ts = _tile(s, (256, 128))

    def body(p_ref, qn_ref, kn_ref, c32, a32, b32, c64, a64, b64, dcqn_ref, dckvn_ref, dkpe_ref, *rest):
        db_refs, dc_refs = rest[0:9], rest[9:12]
        dp_ref, dqn_ref, dkn_ref = rest[12:15]
        first = pl.program_id(0) == 0
        dx, dg = _rms_bwd_rows(p_ref[:, 0:Q_LORA], qn_ref[...], dcqn_ref[...])
        dp_ref[:, 0:Q_LORA] = dx.astype(BF16)
        _accum(dqn_ref, dg, first)
        dx, dg = _rms_bwd_rows(p_ref[:, COL_CKV:COL_KPE], kn_ref[...], dckvn_ref[...])
        dp_ref[:, COL_CKV:COL_KPE] = dx.astype(BF16)
        _accum(dkn_ref, dg, first)
        g = pltpu.roll(_rope_t(dkpe_ref[...], c32[...], a32[...], b32[...], QK_ROPE // 2), LANE - HEAD_DIM, 1)
        lane = lax.broadcasted_iota(jnp.int32, g.shape, 1)
        dp_ref[:, COL_KPE:COL_B] = jnp.where(lane < QK_ROPE, g, 0.0).astype(BF16)
        for which in range(3):
            for b in range(3):
                sl = slice(b * LANE, (b + 1) * LANE)
                g = db_refs[which][:, sl] + db_refs[3 + which][:, sl] + db_refs[6 + which][:, sl]
                if which < 2:
                    g = _rope_t(g, c64[...], a64[...], b64[...], HEAD_DIM // 2)
                if which == 0:
                    g = g * SCALE_BC
                col = COL_B + which * WIDTH_AB + b * LANE
                dp_ref[:, col:col + LANE] = g.astype(BF16)
        dp_ref[:, COL_C:COL_C + WIDTH_C] = (dc_refs[0][...] * SCALE_BC).astype(BF16)
        dp_ref[:, COL_C + WIDTH_C:COL_C + 2 * WIDTH_C] = dc_refs[1][...].astype(BF16)
        dp_ref[:, COL_C + 2 * WIDTH_C:PROJ_W] = dc_refs[2][...].astype(BF16)

    tab = [_row_spec(ts, LANE)] * 6
    in_specs = ([_row_spec(ts, PROJ_W), _fix_spec(Q_LORA), _fix_spec(KV_LORA)] + tab
                + [_row_spec(ts, Q_LORA), _row_spec(ts, KV_LORA), _row_spec(ts, LANE)]
                + [_row_spec(ts, WIDTH_AB)] * 9 + [_row_spec(ts, WIDTH_C)] * 3)
    return pl.pallas_call(
        body, name="prep_bwd", grid=(s // ts,), in_specs=in_specs,
        out_specs=(_row_spec(ts, PROJ_W), _fix_spec(Q_LORA), _fix_spec(KV_LORA)),
        out_shape=(_sds((s, PROJ_W), BF16), _sds((1, Q_LORA), F32), _sds((1, KV_LORA), F32)),
        compiler_params=_cp("arbitrary"))(proj, q_norm.reshape(1, -1), kv_norm.reshape(1, -1), *t32, *t64,
                                          dcqn, dckvn, dkpe, *db, *dc)


def _a_post_fwd(cqn, ckvn, w_uq, w_ukv, kpe, t32):
    s = cqn.shape[0]
    ts = _tile(s, (512, 256, 128))

    def body(cq_ref, ckv_ref, wq_ref, wkv_ref, kpe_ref, c32, a32, b32, q_ref, k_ref, kt_ref, v_ref):
        qa = jnp.dot(cq_ref[...], wq_ref[...], preferred_element_type=F32)
        kva = jnp.dot(ckv_ref[...], wkv_ref[...], preferred_element_type=F32)
        kpe = kpe_ref[...].astype(F32)
        for h in range(HEADS_A):
            hb = slice(h * LANE, (h + 1) * LANE)
            q_ref[:, hb] = _rope(qa[:, hb], c32[...], a32[...], b32[...], QK_ROPE // 2).astype(BF16)
            kh = kva[:, hb] + kpe
            k_ref[:, hb] = kh.astype(BF16)
            kt_ref[hb, :] = kh.T.astype(BF16)
        for p in range(3):
            lo, hi = 2 * p * LANE, (2 * p + 1) * LANE
            v_ref[:, lo:hi] = kva[:, W_A2 + p * LANE:W_A2 + (p + 1) * LANE].astype(BF16)
            v_ref[:, hi:hi + LANE] = jnp.ones((ts, LANE), BF16)

    whole = lambda shape: pl.BlockSpec(shape, lambda i: (0, 0))
    return pl.pallas_call(
        body, name="a_post_fwd", grid=(s // ts,),
        in_specs=[_row_spec(ts, Q_LORA), _row_spec(ts, KV_LORA), whole(w_uq.shape), whole(w_ukv.shape), _row_spec(ts, LANE)]
        + [_row_spec(ts, LANE)] * 3,
        out_specs=(_row_spec(ts, W_A2), _row_spec(ts, W_A2), pl.BlockSpec((W_A2, ts), lambda i: (0, i)), _row_spec(ts, W_A2)),
        out_shape=(_sds((s, W_A2), BF16), _sds((s, W_A2), BF16), _sds((W_A2, s), BF16), _sds((s, W_A2), BF16)),
        compiler_params=_cp("parallel"))(cqn, ckvn, w_uq, w_ukv, kpe, *t32)


def _a_post_bwd(dqa2_t, dka2, dva, w_uq, w_ukv, t32):
    s = dka2.shape[0]
    ts = _tile(s, (512, 256, 128))

    def body(dqt_ref, dk_ref, dv_ref, wq_ref, wkv_ref, c32, a32, b32, dqa_ref, dkva_ref, dkpe_ref, dcq_ref, dckv_ref):
        acc = None
        for h in range(HEADS_A):
            hb = slice(h * LANE, (h + 1) * LANE)
            dqa_ref[:, hb] = _rope_t(dqt_ref[hb, :].T * SCALE_A, c32[...], a32[...], b32[...], QK_ROPE // 2).astype(BF16)
            part = dk_ref[:, hb] * SCALE_A
            dkva_ref[:, hb] = part.astype(BF16)
            acc = part if acc is None else acc + part
        dkva_ref[:, W_A2:W_KV] = dv_ref[...].astype(BF16)
        dkpe_ref[...] = acc
        dcq_ref[...] = _nt(dqa_ref[...], wq_ref[...])
        dckv_ref[...] = _nt(dkva_ref[...], wkv_ref[...])

    whole = lambda shape: pl.BlockSpec(shape, lambda i: (0, 0))
    return pl.pallas_call(
        body, name="a_post_bwd", grid=(s // ts,),
        in_specs=[pl.BlockSpec((W_A2, ts), lambda i: (0, i)), _row_spec(ts, W_A2), _row_spec(ts, WIDTH_AB),
                  whole(w_uq.shape), whole(w_ukv.shape)] + [_row_spec(ts, LANE)] * 3,
        out_specs=(_row_spec(ts, W_A2), _row_spec(ts, W_KV), _row_spec(ts, LANE), _row_spec(ts, Q_LORA),
                   _row_spec(ts, KV_LORA)),
        out_shape=(_sds((s, W_A2), BF16), _sds((s, W_KV), BF16), _sds((s, LANE), F32), _sds((s, Q_LORA), F32),
                   _sds((s, KV_LORA), F32)),
        compiler_params=_cp("parallel"))(dqa2_t, dka2, dva, w_uq, w_ukv, *t32)


def _pair_masks():
    lane = lax.broadcasted_iota(jnp.int32, (1, LANE), 1)
    return lane < HEAD_DIM, lane >= HEAD_DIM


def _nt(a, b):
    return lax.dot_general(a, b, (((1,), (1,)), ((), ())), preferred_element_type=F32)


def _tn(a, b):
    return lax.dot_general(a, b, (((0,), (0,)), ((), ())), preferred_element_type=F32)


def _stack_heads(x):
    m0, m1 = _pair_masks()
    zero = jnp.zeros_like(x)
    return jnp.concatenate([jnp.where(m0, x, zero), jnp.where(m1, x, zero)], axis=0)


def _stack_stat(x):
    return jnp.concatenate([x[:, 0:1], x[:, HEAD_DIM:HEAD_DIM + 1]], axis=0)


def _softmax_pair(q, kk, vv, bias2):
    t = q.shape[0]
    s = _nt(_stack_heads(q), kk) + bias2
    m = jnp.max(s, axis=1, keepdims=True)
    p = jnp.exp(s - m)
    l = jnp.sum(p, axis=1, keepdims=True)
    o2 = jnp.dot(p.astype(BF16), vv, preferred_element_type=F32) / l
    lse2 = m + jnp.log(l)
    lo = _lane_lt64((t, LANE))
    return jnp.where(lo, o2[:t], o2[t:]), jnp.where(lo, lse2[:t], lse2[t:])


def _softmax_pair_bwd(q, kk, vv, do, lse, delta, bias2):
    t = q.shape[0]
    q2, do2 = _stack_heads(q), _stack_heads(do)
    p = jnp.exp(_nt(q2, kk) + bias2 - _stack_stat(lse))
    ds = p * (_nt(do2, vv) - _stack_stat(delta))
    dsb = ds.astype(BF16)
    dq2 = jnp.dot(dsb, kk, preferred_element_type=F32)
    lo = _lane_lt64((t, LANE))
    return jnp.where(lo, dq2[:t], dq2[t:]), _tn(dsb, q2), _tn(p.astype(BF16), do2), ds


DENSE_FWD_TQ = 1024
DENSE_BWD_TQ, DENSE_BWD_TK = 2048, 1024
LOG2E = math.log2(math.e)


def _dense_fwd(qa, ka, va1):
    s = qa.shape[0]
    tq = min(DENSE_FWD_TQ, s)
    c = SCALE_A * LOG2E

    def body(q_ref, k_ref, v_ref, o_ref, lse_ref):
        vv = v_ref[...]
        outs, lses = [], []
        for hh in range(2):
            hs = slice(hh * LANE, (hh + 1) * LANE)
            sc = _nt(q_ref[:, hs], k_ref[:, hs])
            m = jnp.max(sc, axis=1, keepdims=True)
            p = jnp.exp2((sc - m) * c)
            acc = jnp.dot(p.astype(BF16), vv, preferred_element_type=F32)
            l = acc[:, LANE:]
            outs.append(acc[:, :LANE] / l)
            lses.append(m * SCALE_A + jnp.log(l))
        lo = _lane_lt64((tq, LANE))
        o_ref[...] = jnp.where(lo, outs[0], outs[1])
        lse_ref[0] = _stat_rows(jnp.where(lo, lses[0], lses[1]))

    return pl.pallas_call(
        body, name="dense_fwd", grid=(3, s // tq),
        in_specs=[pl.BlockSpec((tq, 2 * LANE), lambda p, i: (i, p)), pl.BlockSpec((s, 2 * LANE), lambda p, i: (0, p)),
                  pl.BlockSpec((s, 2 * LANE), lambda p, i: (0, p))],
        out_specs=(pl.BlockSpec((tq, LANE), lambda p, i: (i, p)), pl.BlockSpec((1, 8, tq), lambda p, i: (p, 0, i))),
        out_shape=(_sds((s, WIDTH_AB), F32), _sds((3, 8, s), F32)),
        compiler_params=_cp("parallel", "parallel"))(qa, ka, va1)


def _stat_rows(lane_dense):
    tr = lane_dense.T
    return jnp.concatenate([tr[0:1, :], tr[HEAD_DIM:HEAD_DIM + 1, :], jnp.zeros((6, tr.shape[1]), F32)], axis=0)


def _dense_bwd(qa, ka, kat, va1, do, lse_rows, delta_rows):
    s = qa.shape[0]
    tq, tk = min(DENSE_BWD_TQ, s), min(DENSE_BWD_TK, s)
    c = SCALE_A * LOG2E

    def body(q_ref, k_ref, kt_ref, v_ref, do_ref, lse_ref, dl_ref, dqt_ref, dk_ref, dv_ref):
        j, i = pl.program_id(1), pl.program_id(2)

        @pl.when((j == 0) & (i == 0))
        def _():
            dqt_ref[...] = jnp.zeros(dqt_ref.shape, F32)

        vv, do_ = v_ref[...], do_ref[...]
        lse_t, dl_t = lse_ref[0] * LOG2E, dl_ref[0]
        vm = _pair_masks()
        cols = pl.ds(pl.multiple_of(i * tq, tq), tq)
        dv = None
        for hh in range(2):
            hs = slice(hh * LANE, (hh + 1) * LANE)
            qh = q_ref[:, hs]
            dom = jnp.where(vm[hh], do_, jnp.zeros_like(do_))
            dpt = _nt(vv, dom)
            pt = jnp.exp2(_nt(k_ref[:, hs], qh) * c - lse_t[hh:hh + 1, :])
            dst = pt * (dpt - dl_t[hh:hh + 1, :])
            pb, dsb = pt.astype(BF16), dst.astype(BF16)
            dv_h = jnp.dot(pb, dom, preferred_element_type=F32)
            dv = dv_h if dv is None else dv + dv_h
            dk_h = jnp.dot(dsb, qh, preferred_element_type=F32)
            dqt_ref[hs, cols] += jnp.dot(kt_ref[hs, :], dsb, preferred_element_type=F32)

            @pl.when(i == 0)
            def _():
                dk_ref[:, hs] = dk_h

            @pl.when(i != 0)
            def _():
                dk_ref[:, hs] += dk_h
        _accum(dv_ref, dv, i == 0)

    st_spec = pl.BlockSpec((1, 8, tq), lambda p, j, i: (p, 0, i))
    return pl.pallas_call(
        body, name="dense_bwd", grid=(3, s // tk, s // tq),
        in_specs=[pl.BlockSpec((tq, 2 * LANE), lambda p, j, i: (i, p)), pl.BlockSpec((tk, 2 * LANE), lambda p, j, i: (j, p)),
                  pl.BlockSpec((2 * LANE, tk), lambda p, j, i: (p, j)), pl.BlockSpec((tk, LANE), lambda p, j, i: (j, 2 * p)),
                  pl.BlockSpec((tq, LANE), lambda p, j, i: (i, p)), st_spec, st_spec],
        out_specs=(pl.BlockSpec((2 * LANE, s), lambda p, j, i: (p, 0)), pl.BlockSpec((tk, 2 * LANE), lambda p, j, i: (j, p)),
                   pl.BlockSpec((tk, LANE), lambda p, j, i: (j, p))),
        out_shape=(_sds((W_A2, s), F32), _sds((s, W_A2), F32), _sds((s, WIDTH_AB), F32)),
        compiler_params=_cp("parallel", "arbitrary", "arbitrary"))(qa, ka, kat, va1, do, lse_rows, delta_rows)


BAND_TILE = 1024
BAND_SUB = 128
QKV_W = 3 * WIDTH_AB


def _band_bias_table():
    row = np.arange(BAND_SUB)[:, None]
    col = np.arange(2 * BAND_SUB)[None, :]
    band = np.abs(row - col + BAND_HALF) <= BAND_HALF
    variants = []
    for idx in range(4):
        ok = band & ((col >= BAND_HALF) | ((idx & 1) == 0)) & ((col < 2 * BAND_SUB - BAND_HALF) | ((idx & 2) == 0))
        one = np.where(ok, 0.0, NEG_INF).astype(np.float32)
        variants.append(np.concatenate([one, one], axis=0))
    return jnp.asarray(np.stack(variants))


def _band_specs(t, n):
    hpt = t // BAND_HALF
    last = n // BAND_HALF - 1
    return [pl.BlockSpec((BAND_HALF, QKV_W), lambda r, i: (jnp.maximum(i * hpt - 1, 0), r)),
            pl.BlockSpec((t, QKV_W), lambda r, i: (i, r)),
            pl.BlockSpec((BAND_HALF, QKV_W), lambda r, i: (jnp.minimum((i + 1) * hpt, last), r)),
            pl.BlockSpec((4, 2 * BAND_SUB, 2 * BAND_SUB), lambda r, i: (0, 0, 0))]


def _band_bias(b_ref, a, nsub, i, nt):
    idx = 0
    if a == 0:
        idx = idx + (i == 0).astype(jnp.int32)
    if a == nsub - 1:
        idx = idx + 2 * (i == nt - 1).astype(jnp.int32)
    return b_ref[idx]


def _band_kv(left, main, right, p):
    kc = slice(WIDTH_AB + p * LANE, WIDTH_AB + (p + 1) * LANE)
    vc = slice(2 * WIDTH_AB + p * LANE, 2 * WIDTH_AB + (p + 1) * LANE)
    return (jnp.concatenate([left[:, kc], main[:, kc], right[:, kc]], axis=0),
            jnp.concatenate([left[:, vc], main[:, vc], right[:, vc]], axis=0))


def _banded_fwd(qkvb, dil, bias):
    s = qkvb.shape[0]
    n = s // dil
    t = min(n, BAND_TILE)
    nsub, nt = t // BAND_SUB, n // t
    view = qkvb.reshape(n, dil * QKV_W)

    def body(left, main, right, b_ref, o_ref, lse_ref):
        i = pl.program_id(1)
        for p in range(3):
            pc = slice(p * LANE, (p + 1) * LANE)
            kk, vv = _band_kv(left, main, right, p)
            for a in range(nsub):
                rows, win = slice(a * BAND_SUB, (a + 1) * BAND_SUB), slice(a * BAND_SUB, (a + 2) * BAND_SUB)
                o, lse = _softmax_pair(main[rows, pc], kk[win], vv[win], _band_bias(b_ref, a, nsub, i, nt))
                o_ref[rows, pc] = o
                lse_ref[rows, pc] = lse

    o_spec = pl.BlockSpec((t, WIDTH_AB), lambda r, i: (i, r))
    o, lse = pl.pallas_call(
        body, name=f"banded_fwd_d{dil}", grid=(dil, nt), in_specs=_band_specs(t, n), out_specs=(o_spec, o_spec),
        out_shape=(_sds((n, dil * WIDTH_AB), F32), _sds((n, dil * WIDTH_AB), F32)),
        compiler_params=_cp("parallel", "parallel"))(view, view, view, bias)
    return o.reshape(s, WIDTH_AB), lse.reshape(s, WIDTH_AB)


def _banded_bwd(qkvb, do, lse, delta, dil, bias):
    s = qkvb.shape[0]
    n = s // dil
    t = min(n, BAND_TILE)
    nsub, nt = t // BAND_SUB, n // t
    view = qkvb.reshape(n, dil * QKV_W)
    side = [a.reshape(n, dil * WIDTH_AB) for a in (do, lse, delta)]

    def body(left, main, right, b_ref, do_ref, lse_ref, dl_ref, dq_ref, dk_ref, dv_ref):
        i = pl.program_id(1)

        @pl.when(i == 0)
        def _():
            dk_ref[...] = jnp.zeros(dk_ref.shape, F32)
            dv_ref[...] = jnp.zeros(dv_ref.shape, F32)

        lrow = pl.multiple_of(jnp.maximum(i * t - BAND_HALF, 0), BAND_HALF)
        rrow = pl.multiple_of(jnp.minimum((i + 1) * t, n - BAND_HALF), BAND_HALF)
        mrow = pl.multiple_of(i * t, BAND_HALF)
        for p in range(3):
            pc = slice(p * LANE, (p + 1) * LANE)
            kk, vv = _band_kv(left, main, right, p)
            parts = []
            for a in range(nsub):
                rows, win = slice(a * BAND_SUB, (a + 1) * BAND_SUB), slice(a * BAND_SUB, (a + 2) * BAND_SUB)
                dq, dk, dv, _ = _softmax_pair_bwd(main[rows, pc], kk[win], vv[win], do_ref[rows, pc], lse_ref[rows, pc],
                                                  dl_ref[rows, pc], _band_bias(b_ref, a, nsub, i, nt))
                dq_ref[rows, pc] = dq
                parts.append((dk, dv))
            for which, ref in ((0, dk_ref), (1, dv_ref)):
                chunks = []
                for c in range(nsub + 1):
                    g = parts[c][which][:BAND_SUB] if c < nsub else None
                    if c >= 1:
                        h = parts[c - 1][which][BAND_SUB:]
                        g = h if g is None else g + h
                    chunks.append(g)
                mid = jnp.concatenate([chunks[0][BAND_HALF:]] + chunks[1:nsub] + [chunks[nsub][:BAND_HALF]], axis=0)
                ref[pl.ds(lrow, BAND_HALF), pc] += chunks[0][:BAND_HALF]
                ref[pl.ds(mrow, t), pc] += mid
                ref[pl.ds(rrow, BAND_HALF), pc] += chunks[nsub][BAND_HALF:]

    q_spec = pl.BlockSpec((t, WIDTH_AB), lambda r, i: (i, r))
    acc_spec = pl.BlockSpec((n, WIDTH_AB), lambda r, i: (0, r))
    shp = _sds((n, dil * WIDTH_AB), F32)
    outs = pl.pallas_call(
        body, name=f"banded_bwd_d{dil}", grid=(dil, nt), in_specs=_band_specs(t, n) + [q_spec, q_spec, q_spec],
        out_specs=(q_spec, acc_spec, acc_spec), out_shape=(shp, shp, shp),
        compiler_params=_cp("parallel", "arbitrary"))(view, view, view, bias, *side)
    return [a.reshape(s, WIDTH_AB) for a in outs]


def _na_geometry(s):
    rows = s // GRID_W
    assert rows >= 2 * NA_ROWS and rows % NA_ROWS == 0
    return rows, rows // NA_ROWS


def _na_row(n, i, rows):
    rq = n * NA_ROWS + i
    rs = jnp.clip(rq - NA_ROWS // 2, 0, rows - NA_ROWS)
    return pl.multiple_of(rs * GRID_W, GRID_W), rs - rq + NA_ROWS - 1


NA_KEYS = NA_ROWS * GRID_W


def _natten_fwd(qkvc, tfull):
    s = qkvc.shape[0]
    rows, nrb = _na_geometry(s)
    tq = NA_ROWS * GRID_W

    def body(q_ref, k_ref, v_ref, t_ref, o_ref, lse_ref):
        n = pl.program_id(1)
        for i in range(NA_ROWS):
            tok, base = _na_row(n, i, rows)
            kk, vv = k_ref[pl.ds(tok, NA_KEYS), :], v_ref[pl.ds(tok, NA_KEYS), :]
            sl = slice(i * GRID_W, (i + 1) * GRID_W)
            bias2 = jnp.concatenate([t_ref[0, base], t_ref[1, base]], axis=0)
            o, lse = _softmax_pair(q_ref[sl, :], kk, vv, bias2)
            o_ref[sl, :] = o
            lse_ref[sl, :] = lse

    o_spec = pl.BlockSpec((tq, LANE), lambda p, n: (n, p))
    return pl.pallas_call(
        body, name="natten_fwd", grid=(2, nrb),
        in_specs=[pl.BlockSpec((tq, LANE), lambda p, n: (n, p)), pl.BlockSpec((s, LANE), lambda p, n: (0, 2 + p)),
                  pl.BlockSpec((s, LANE), lambda p, n: (0, 4 + p)),
                  pl.BlockSpec((2, NA_ROWS, GRID_W, NA_KEYS), lambda p, n: (p, 0, 0, 0))],
        out_specs=(o_spec, o_spec), out_shape=(_sds((s, WIDTH_C), F32), _sds((s, WIDTH_C), F32)),
        compiler_params=_cp("parallel", "parallel"))(qkvc, qkvc, qkvc, tfull)


def _natten_bwd(qkvc, tfull, do, lse, delta):
    s = qkvc.shape[0]
    rows, nrb = _na_geometry(s)
    tq = NA_ROWS * GRID_W

    def body(q_ref, k_ref, v_ref, t_ref, do_ref, lse_ref, dl_ref, dq_ref, dk_ref, dv_ref, dt_ref):
        n = pl.program_id(1)

        @pl.when(n == 0)
        def _():
            dk_ref[...] = jnp.zeros(dk_ref.shape, F32)
            dv_ref[...] = jnp.zeros(dv_ref.shape, F32)
            dt_ref[...] = jnp.zeros(dt_ref.shape, F32)

        for i in range(NA_ROWS):
            tok, base = _na_row(n, i, rows)
            win = pl.ds(tok, NA_KEYS)
            sl = slice(i * GRID_W, (i + 1) * GRID_W)
            bias2 = jnp.concatenate([t_ref[0, base], t_ref[1, base]], axis=0)
            dq, dk, dv, ds = _softmax_pair_bwd(q_ref[sl, :], k_ref[win, :], v_ref[win, :], do_ref[sl, :], lse_ref[sl, :],
                                               dl_ref[sl, :], bias2)
            dq_ref[sl, :] = dq
            dk_ref[win, :] += dk
            dv_ref[win, :] += dv
            dt_ref[0, base] += ds[:GRID_W]
            dt_ref[1, base] += ds[GRID_W:]

    q_spec = pl.BlockSpec((tq, LANE), lambda p, n: (n, p))
    acc_spec = pl.BlockSpec((s, LANE), lambda p, n: (0, p))
    t_spec = pl.BlockSpec((2, NA_ROWS, GRID_W, NA_KEYS), lambda p, n: (p, 0, 0, 0))
    shp = _sds((s, WIDTH_C), F32)
    return pl.pallas_call(
        body, name="natten_bwd", grid=(2, nrb),
        in_specs=[q_spec, pl.BlockSpec((s, LANE), lambda p, n: (0, 2 + p)), pl.BlockSpec((s, LANE), lambda p, n: (0, 4 + p)),
                  t_spec, q_spec, q_spec, q_spec],
        out_specs=(q_spec, acc_spec, acc_spec, t_spec),
        out_shape=(shp, shp, shp, _sds((HEADS_C, NA_ROWS, GRID_W, NA_KEYS), F32)),
        compiler_params=_cp("parallel", "arbitrary"))(qkvc, qkvc, qkvc, tfull, do, lse, delta)


def _rpb_constants():
    p = np.arange(GRID_W)[:, None]
    qc = np.arange(GRID_W)[None, :]
    dc = np.clip(qc - p, -(NA_COLS - 1), NA_COLS - 1) + NA_COLS - 1
    onehot = (dc.reshape(1, -1) == np.arange(32)[:, None]).astype(np.float32)
    c_start = np.clip(p - NA_COLS // 2, 0, GRID_W - NA_COLS)
    col_ok = ((qc >= c_start) & (qc < c_start + NA_COLS)).reshape(1, -1).astype(np.float32)
    a = np.arange(16)[:, None]
    bj = np.arange(64)[None, :]
    row_sel = ((bj // 8 + bj % 8) == a).astype(np.float32)
    return jnp.asarray(onehot), jnp.asarray(col_ok), jnp.asarray(row_sel)


def _rpb_expand(rpb, onehot, col_ok):
    r2 = jnp.pad(rpb.reshape(HEADS_C * 15, 31), ((0, 4), (0, 1)))

    def body(r_ref, oh_ref, ok_ref, o_ref):
        t = jnp.dot(r_ref[...], oh_ref[...], preferred_element_type=F32, precision=lax.Precision.HIGHEST)
        o_ref[...] = jnp.where(ok_ref[...] > 0.5, t, NEG_INF)

    tm = pl.pallas_call(body, name="rpb_expand", out_shape=_sds((64, GRID_W * GRID_W), F32))(r2, onehot, col_ok)
    tm = tm[:HEADS_C * 15].reshape(HEADS_C, 15, GRID_W, GRID_W)
    tfull = jnp.stack([jnp.concatenate([tm[:, base + j] for j in range(NA_ROWS)], axis=-1) for base in range(NA_ROWS)], axis=1)
    return tfull


def _rpb_grad(dtfull, onehot, row_sel):
    g = dtfull.reshape(HEADS_C, NA_ROWS, GRID_W, NA_ROWS, GRID_W).transpose(0, 1, 3, 2, 4).reshape(HEADS_C, 64, GRID_W * GRID_W)

    def body(g_ref, oh_ref, sel_ref, o_ref):
        for h in range(HEADS_C):
            mid = lax.dot_general(g_ref[h], oh_ref[...], (((1,), (1,)), ((), ())), preferred_element_type=F32,
                                  precision=lax.Precision.HIGHEST)
            o_ref[h] = jnp.dot(sel_ref[...], mid, preferred_element_type=F32, precision=lax.Precision.HIGHEST)

    out = pl.pallas_call(body, name="rpb_grad", out_shape=_sds((HEADS_C, 16, 32), F32))(g, onehot, row_sel)
    return out[:, :15, :31]


def _outnorm_fwd(o_a, branch_o, branch_lse, o_c, ga, gb, gc):
    s = o_a.shape[0]
    ts = _tile(s, (512, 256, 128))

    def body(a_ref, o1, o2, o3, l1, l2, l3, c_ref, ga_ref, gb_ref, gc_ref, ob_ref, lse_ref, o_ref):
        la, lb, lc = l1[...], l2[...], l3[...]
        m = jnp.maximum(jnp.maximum(la, lb), lc)
        ea, eb, ec = jnp.exp(la - m), jnp.exp(lb - m), jnp.exp(lc - m)
        den = ea + eb + ec
        o_b = (o1[...] * ea + o2[...] * eb + o3[...] * ec) / den
        ob_ref[...] = o_b
        lse_ref[...] = m + jnp.log(den)
        col = 0
        for x, g in ((a_ref[...], ga_ref), (o_b, gb_ref), (c_ref[...], gc_ref)):
            o_ref[:, col:col + x.shape[1]] = (x * _rstd(x) * g[...]).astype(BF16)
            col += x.shape[1]

    sp = _row_spec(ts, WIDTH_AB)
    return pl.pallas_call(
        body, name="outnorm_fwd", grid=(s // ts,),
        in_specs=[sp] * 7 + [_row_spec(ts, WIDTH_C), _fix_spec(WIDTH_AB), _fix_spec(WIDTH_AB), _fix_spec(WIDTH_C)],
        out_specs=(sp, sp, _row_spec(ts, D_MODEL)),
        out_shape=(_sds((s, WIDTH_AB), F32), _sds((s, WIDTH_AB), F32), _sds((s, D_MODEL), BF16)),
        compiler_params=_cp("parallel"))(o_a, *branch_o, *branch_lse, o_c, ga.reshape(1, -1), gb.reshape(1, -1),
                                         gc.reshape(1, -1))


def _outnorm_bwd(dxb, w_out, o_a, o_b, o_c, ga, gb, gc):
    s = o_a.shape[0]
    ts = _tile(s, (512, 256, 128))

    def body(dx_ref, w_ref, a_ref, b_ref, c_ref, ga_ref, gb_ref, gc_ref, *outs):
        first = pl.program_id(0) == 0
        dm = lax.dot_general(dx_ref[...], w_ref[...], (((1,), (1,)), ((), ())), preferred_element_type=F32)
        col = 0
        for k, (ref, g) in enumerate(((a_ref, ga_ref), (b_ref, gb_ref), (c_ref, gc_ref))):
            x = ref[...]
            w = x.shape[1]
            dx, dg = _rms_bwd_rows(x, g[...], dm[:, col:col + w])
            col += w
            outs[k][...] = dx.astype(BF16)
            for b, blk in enumerate(_group_sum(dx * x)):
                if k == 0:
                    outs[3][b] = _stat_rows(blk)
                else:
                    outs[3 + k][:, b * LANE:(b + 1) * LANE] = blk
            _accum(outs[6 + k], dg, first)

    widths = (WIDTH_AB, WIDTH_AB, WIDTH_C)
    return pl.pallas_call(
        body, name="outnorm_bwd", grid=(s // ts,),
        in_specs=[_row_spec(ts, D_MODEL), pl.BlockSpec(w_out.shape, lambda i: (0, 0))] + [_row_spec(ts, w) for w in widths]
        + [_fix_spec(w) for w in widths],
        out_specs=tuple([_row_spec(ts, w) for w in widths] + [pl.BlockSpec((3, 8, ts), lambda i: (0, 0, i))]
                        + [_row_spec(ts, w) for w in widths[1:]] + [_fix_spec(w) for w in widths]),
        out_shape=tuple([_sds((s, w), BF16) for w in widths] + [_sds((3, 8, s), F32)]
                        + [_sds((s, w), F32) for w in widths[1:]] + [_sds((1, w), F32) for w in widths]),
        compiler_params=_cp("arbitrary"))(dxb, w_out, o_a, o_b, o_c, ga.reshape(1, -1), gb.reshape(1, -1),
                                          gc.reshape(1, -1))


def _adamw(w, g, m, v, *, name):
    r, c = w.shape
    tr = _tile(r, (512, 256, 128, 64, 8))

    def body(w_ref, g_ref, m_ref, v_ref, d_ref, nm_ref, nv_ref):
        gv = g_ref[...]
        nm = ADAM_B1 * m_ref[...] + (1.0 - ADAM_B1) * gv
        nv = ADAM_B2 * v_ref[...] + (1.0 - ADAM_B2) * jnp.square(gv)
        m_hat = nm / (1.0 - ADAM_B1 ** ADAM_STEP)
        v_hat = nv / (1.0 - ADAM_B2 ** ADAM_STEP)
        d_ref[...] = -ADAM_LR * (m_hat / (jnp.sqrt(v_hat) + ADAM_EPS) + ADAM_WD * w_ref[...])
        nm_ref[...] = nm
        nv_ref[...] = nv

    sp = _row_spec(tr, c)
    return pl.pallas_call(
        body, name=name, grid=(r // tr,), in_specs=[sp] * 4, out_specs=(sp, sp, sp),
        out_shape=(_sds((r, c), F32),) * 3, compiler_params=_cp("parallel"))(w, g, m, v)


def _add_n(parts, *, name, out_dtype):
    r, c = parts[0].shape
    tr = max(t for t in range(16, 1025, 16) if r % t == 0)

    def body(*refs):
        acc = refs[0][...].astype(F32)
        for ref in refs[1:-1]:
            acc = acc + ref[...].astype(F32)
        refs[-1][...] = acc.astype(out_dtype)

    sp = _row_spec(tr, c)
    return pl.pallas_call(
        body, name=name, grid=(r // tr,), in_specs=[sp] * len(parts), out_specs=sp, out_shape=_sds((r, c), out_dtype),
        compiler_params=_cp("parallel"))(*parts)


ANY = pl.BlockSpec(memory_space=pl.ANY)
CHIP_FLIPS = ((1, 0), (0, 1), (1, 1))


def _me():
    return lax.axis_index("x"), lax.axis_index("y"), lax.axis_index("c")


def _gather_chips(half):
    def body(src, out, send_sems, recv_sems):
        x, y, c = _me()
        me, mine = (x, y, c), 2 * x + y
        chip_x, chip_y, chip_d = 2 * (1 - x) + y, 2 * x + (1 - y), 2 * (1 - x) + (1 - y)
        passed_chip = 2 * (x ^ (1 - c)) + (y ^ c)
        pass_to = (x ^ c, y ^ (1 - c), c)

        def copy(k, chip, half_idx, to, source=None):
            dst = out.at[chip, half_idx]
            return pltpu.make_async_remote_copy(src_ref=dst if source is None else source, dst_ref=dst,
                                                send_sem=send_sems.at[k], recv_sem=recv_sems.at[k], device_id=to,
                                                device_id_type=MESH_T)

        sends = [copy(0, mine, c, (1 - x, y, c), source=src), copy(1, mine, c, (x, 1 - y, c), source=src)]
        for cp in sends:
            cp.start()
        copy(0, chip_x, c, me).wait_recv()
        copy(1, chip_y, c, me).wait_recv()
        sends += [copy(2, passed_chip, c, pass_to), copy(3, chip_x, c, (x, y, 1 - c)), copy(4, chip_y, c, (x, y, 1 - c))]
        for cp in sends[2:]:
            cp.start()
        copy(2, chip_d, c, me).wait_recv()
        sends.append(copy(5, chip_d, c, (x, y, 1 - c)))
        sends[-1].start()
        for k, chip in ((3, chip_x), (4, chip_y), (5, chip_d)):
            copy(k, chip, 1 - c, me).wait_recv()
        for cp in sends:
            cp.wait_send()

    return pl.pallas_call(
        body, name="gather_chips", in_specs=[ANY], out_specs=ANY, out_shape=_sds((4, 2) + half.shape, half.dtype),
        scratch_shapes=[pltpu.SemaphoreType.DMA((6,)), pltpu.SemaphoreType.DMA((6,))])(half)


def _swap_sibling(block):
    def body(src, out, send_sem, recv_sem):
        x, y, c = _me()
        cp = pltpu.make_async_remote_copy(src_ref=src, dst_ref=out, send_sem=send_sem, recv_sem=recv_sem,
                                          device_id=(x, y, 1 - c), device_id_type=MESH_T)
        cp.start()
        cp.wait()

    return pl.pallas_call(
        body, name="swap_sibling", in_specs=[ANY], out_specs=ANY, out_shape=_sds(block.shape, block.dtype),
        scratch_shapes=[pltpu.SemaphoreType.DMA(()), pltpu.SemaphoreType.DMA(())])(block)


def _swap_other_half(halves):
    def body(src, out, send_sem, recv_sem):
        x, y, c = _me()
        cp = pltpu.make_async_remote_copy(src_ref=src.at[:, 1 - c], dst_ref=out, send_sem=send_sem, recv_sem=recv_sem,
                                          device_id=(x, y, 1 - c), device_id_type=MESH_T)
        cp.start()
        cp.wait()

    shape = (halves.shape[0],) + halves.shape[2:]
    return pl.pallas_call(
        body, name="swap_other_half", in_specs=[ANY], out_specs=ANY, out_shape=_sds(shape, halves.dtype),
        scratch_shapes=[pltpu.SemaphoreType.DMA(()), pltpu.SemaphoreType.DMA(())])(halves)


def _pair_sum(halves, core, other):
    n, _, h, c = halves.shape
    tr = max(t for t in range(16, 1025, 16) if h % t == 0)

    def body(core_ref, a_ref, b_ref, o_ref):
        o_ref[...] = (a_ref[...] + b_ref[...]).astype(BF16)

    grid_spec = pltpu.PrefetchScalarGridSpec(
        num_scalar_prefetch=1, grid=(n, h // tr),
        in_specs=[pl.BlockSpec((None, None, tr, c), lambda j, i, core_ref: (j, core_ref[0], i, 0)),
                  pl.BlockSpec((None, tr, c), lambda j, i, core_ref: (j, i, 0))],
        out_specs=pl.BlockSpec((None, tr, c), lambda j, i, core_ref: (j, i, 0)))
    return pl.pallas_call(
        body, name="pair_sum", grid_spec=grid_spec, out_shape=_sds((n, h, c), BF16),
        compiler_params=_cp("parallel", "parallel"))(core.reshape(1).astype(jnp.int32), halves, other)


def _scatter_chips(parts):
    def body(src, out, send_sems, recv_sems):
        x, y, c = _me()
        mine = 2 * x + y
        sends = []
        for k, (fx, fy) in enumerate(CHIP_FLIPS):
            theirs = 2 * (x ^ fx) + (y ^ fy)
            cp = pltpu.make_async_remote_copy(src_ref=src.at[theirs], dst_ref=out.at[mine], send_sem=send_sems.at[k],
                                              recv_sem=recv_sems.at[k], device_id=(x ^ fx, y ^ fy, c), device_id_type=MESH_T)
            cp.start()
            sends.append(cp)
        for k, (fx, fy) in enumerate(CHIP_FLIPS):
            theirs = 2 * (x ^ fx) + (y ^ fy)
            pltpu.make_async_remote_copy(src_ref=src.at[theirs], dst_ref=out.at[theirs], send_sem=send_sems.at[k],
                                         recv_sem=recv_sems.at[k], device_id=(x ^ fx, y ^ fy, c),
                                         device_id_type=MESH_T).wait_recv()
        for cp in sends:
            cp.wait_send()

    return pl.pallas_call(
        body, name="scatter_chips", in_specs=[ANY], out_specs=ANY, out_shape=_sds(parts.shape, parts.dtype),
        scratch_shapes=[pltpu.SemaphoreType.DMA((3,)), pltpu.SemaphoreType.DMA((3,))])(parts)


def _all_reduce_small(block):
    r, c = block.shape

    def body(src, out, slots, send_sems, recv_sems):
        x, y, cc = _me()
        mine = 4 * x + 2 * y + cc
        slots[mine] = src[...]
        sends = []
        for k in range(1, 8):
            fx, fy, fc = (k >> 2) & 1, (k >> 1) & 1, k & 1
            cp = pltpu.make_async_remote_copy(src_ref=src, dst_ref=slots.at[mine], send_sem=send_sems.at[k - 1],
                                              recv_sem=recv_sems.at[k - 1], device_id=(x ^ fx, y ^ fy, cc ^ fc),
                                              device_id_type=MESH_T)
            cp.start()
            sends.append(cp)
        for k in range(1, 8):
            fx, fy, fc = (k >> 2) & 1, (k >> 1) & 1, k & 1
            theirs = 4 * (x ^ fx) + 2 * (y ^ fy) + (cc ^ fc)
            pltpu.make_async_remote_copy(src_ref=src, dst_ref=slots.at[theirs], send_sem=send_sems.at[k - 1],
                                         recv_sem=recv_sems.at[k - 1], device_id=(x ^ fx, y ^ fy, cc ^ fc),
                                         device_id_type=MESH_T).wait_recv()
        for cp in sends:
            cp.wait_send()
        acc = slots[0]
        for d in range(1, 8):
            acc = acc + slots[d]
        out[...] = acc

    vm = pl.BlockSpec(memory_space=pltpu.VMEM)
    return pl.pallas_call(
        body, name="all_reduce_small", in_specs=[vm], out_specs=vm, out_shape=_sds((r, c), F32),
        scratch_shapes=[pltpu.VMEM((8, r, c), F32), pltpu.SemaphoreType.DMA((7,)), pltpu.SemaphoreType.DMA((7,))])(block)


DIRECT = ("w_mlp_in", "w_mlp_out", "w_out")
BIG = DIRECT + ("w_in", "w_uq", "w_ukv")
COL_SHARDED = {"w_in": True, "w_uq": True, "w_ukv": True, "w_out": False, "w_mlp_in": True, "w_mlp_out": False}
SMALL = ("g_mix", "q_norm", "kv_norm", "rpb", "out_norm_a", "out_norm_b", "out_norm_c", "g_mlp", "g_final")
PACK_C = 1024
ROW_ALIGN = 32


def _pack_rows(parts):
    flat = jnp.concatenate([p.reshape(-1, PACK_C) for p in parts], axis=0)
    return jnp.pad(flat, ((0, -flat.shape[0] % ROW_ALIGN), (0, 0)))


def _unpack_rows(flat, shapes):
    out, row = [], 0
    for shp in shapes:
        n = int(np.prod(shp)) // PACK_C
        out.append(flat[row:row + n].reshape(shp))
        row += n
    return out


def _full_from_shards(name, g):
    if COL_SHARDED[name]:
        return g.transpose(1, 2, 0, 3).reshape(g.shape[1], g.shape[2], 4 * g.shape[3])
    return g.transpose(1, 0, 2, 3).reshape(g.shape[1], 4 * g.shape[2], g.shape[3])


def _shards_from_full(name, w):
    l, k, n = w.shape
    if COL_SHARDED[name]:
        return w.reshape(l, k, 4, n // 4).transpose(2, 0, 1, 3)
    return w.reshape(l, 4, k // 4, n).transpose(1, 0, 2, 3)


def _arrange_w_in(w):
    z = jnp.zeros(w.shape[:-1] + (COL_B - COL_KPE - QK_ROPE,), w.dtype)
    return jnp.concatenate([w[..., :COL_KPE + QK_ROPE], z, w[..., COL_KPE + QK_ROPE:]], axis=-1)


def _unarrange_w_in(w):
    return jnp.concatenate([w[..., :COL_KPE + QK_ROPE], w[..., COL_B:]], axis=-1)


def _arrange_w_uq(w):
    per = HEAD_DIM + QK_ROPE
    z = jnp.zeros(w.shape[:-1] + (LANE - per,), w.dtype)
    cols = []
    for h in range(HEADS_A):
        cols += [w[..., h * per:(h + 1) * per], z]
    return jnp.concatenate(cols, axis=-1)


def _unarrange_w_uq(w):
    per = HEAD_DIM + QK_ROPE
    return jnp.concatenate([w[..., h * LANE:h * LANE + per] for h in range(HEADS_A)], axis=-1)


def _arrange_w_ukv(w):
    z = jnp.zeros(w.shape[:-1] + (HEAD_DIM,), w.dtype)
    ks = []
    for h in range(HEADS_A):
        ks += [w[..., h * LANE:h * LANE + HEAD_DIM], z]
    vs = [w[..., h * LANE + HEAD_DIM:(h + 1) * LANE] for h in range(HEADS_A)]
    return jnp.concatenate(ks + vs, axis=-1)


def _unarrange_w_ukv(w):
    cols = []
    for h in range(HEADS_A):
        cols += [w[..., h * LANE:h * LANE + HEAD_DIM], w[..., W_A2 + h * HEAD_DIM:W_A2 + (h + 1) * HEAD_DIM]]
    return jnp.concatenate(cols, axis=-1)


def _layer_fwd(x, w, sm, tabs, consts):
    t32, t64 = tabs
    onehot, col_ok, _, band = consts
    h, proj, cqn, ckvn, kpe, qkvb, qkvc = _in_proj(x, sm["g_mix"], w["w_in"], sm["q_norm"], sm["kv_norm"], t32, t64)
    qa2, ka2, kat, va1 = _a_post_fwd(cqn, ckvn, w["w_uq"], w["w_ukv"], kpe, t32)
    o_a, lse_a = _dense_fwd(qa2, ka2, va1)
    branch = [_banded_fwd(qkvb, dil, band) for _, dil in DILATED_PAIRS]
    tfull = _rpb_expand(sm["rpb"], onehot, col_ok)
    o_c, lse_c = _natten_fwd(qkvc, tfull)
    o_b, lse_b, mixed = _outnorm_fwd(o_a, [b[0] for b in branch], [b[1] for b in branch], o_c, sm["out_norm_a"],
                                     sm["out_norm_b"], sm["out_norm_c"])
    x_mid = _mm_nn(mixed, w["w_out"], name="out_proj", res=x)
    h2, act = _mlp_in(x_mid, sm["g_mlp"], w["w_mlp_in"])
    x_out = _mm_nn(act, w["w_mlp_out"], name="mlp_out", res=x_mid)
    saved = dict(x=x, h=h, proj=proj, cqn=cqn, ckvn=ckvn, qkvb=qkvb, qkvc=qkvc, qa2=qa2, ka2=ka2, kat=kat, va1=va1, o_a=o_a,
                 lse_a=lse_a, o_b=o_b, lse_b=lse_b, o_c=o_c, lse_c=lse_c, tfull=tfull, mixed=mixed, x_mid=x_mid, h2=h2,
                 act=act)
    return x_out, saved


def _layer_bwd(dx, dxb, sv, w, sm, tabs, consts, packed, places):
    t32, t64 = tabs
    onehot, _, row_sel, band = consts
    g = {}
    du = _mm_nt(dxb, w["w_mlp_out"], name="mlp_out_dx", out_dtype=BF16, relu2_act=sv["act"])
    packed = _mm_tn(sv["act"], dxb, name="mlp_out_dw", packed=(packed,) + places["w_mlp_out"])
    dx_mid, dmb, g["g_mlp"] = _mm_nt_norm_bwd(du, w["w_mlp_in"], sv["x_mid"], sm["g_mlp"], dx, name="mlp_in_dx")
    packed = _mm_tn(sv["h2"], du, name="mlp_in_dw", packed=(packed,) + places["w_mlp_in"])
    packed = _mm_tn(sv["mixed"], dmb, name="out_proj_dw", packed=(packed,) + places["w_out"])
    (do_a, do_b, do_c, dl_a, dl_b, dl_c, g["out_norm_a"], g["out_norm_b"], g["out_norm_c"]) = _outnorm_bwd(
        dmb, w["w_out"], sv["o_a"], sv["o_b"], sv["o_c"], sm["out_norm_a"], sm["out_norm_b"], sm["out_norm_c"])
    dqa2_t, dka2, dva = _dense_bwd(sv["qa2"], sv["ka2"], sv["kat"], sv["va1"], do_a, sv["lse_a"], dl_a)
    db = []
    for _, dil in DILATED_PAIRS:
        db += _banded_bwd(sv["qkvb"], do_b, sv["lse_b"], dl_b, dil, band)
    dq_c, dk_c, dv_c, dtfull = _natten_bwd(sv["qkvc"], sv["tfull"], do_c, sv["lse_c"], dl_c)
    g["rpb"] = _rpb_grad(dtfull, onehot, row_sel)
    dqa, dkva, dkpe, dcqn, dckvn = _a_post_bwd(dqa2_t, dka2, dva, w["w_uq"], w["w_ukv"], t32)
    g["w_uq"] = _unarrange_w_uq(_mm_tn(sv["cqn"], dqa, name="q_up_dw"))
    g["w_ukv"] = _unarrange_w_ukv(_mm_tn(sv["ckvn"], dkva, name="kv_up_dw"))
    dproj, g["q_norm"], g["kv_norm"] = _prep_bwd(sv["proj"], sm["q_norm"], sm["kv_norm"], t32, t64, dcqn, dckvn, dkpe,
                                                  db, (dq_c, dk_c, dv_c))
    g["w_in"] = _unarrange_w_in(_mm_tn(sv["h"], dproj, name="in_proj_dw"))
    dx_in, dxb_in, g["g_mix"] = _mm_nt_norm_bwd(dproj, w["w_in"], sv["x"], sm["g_mix"], dx_mid, name="in_proj_dx")
    return dx_in, dxb_in, g, packed


def _packed_places(offs, l):
    d, r = D_MODEL, D_MODEL // 4
    return {"w_mlp_in": (512, lambda i, j: (j, (offs["w_mlp_in"] + l * d) // 512 + i)),
            "w_mlp_out": (512, lambda i, j: (i // 2, (offs["w_mlp_out"] + l * d) // 512 + i % 2)),
            "w_out": (r, lambda i, j: (i, (offs["w_out"] + l * r) // r))}


def _local_step(x, target, wfull, small, packed_shape, offs):
    s = x.shape[0]
    tabs = (_rope_tables(s, QK_ROPE // 2, 1, lead=HEAD_DIM), _rope_tables(s, HEAD_DIM // 2, 2))
    consts = _rpb_constants() + (_band_bias_table(),)
    saved = []
    for l in range(DEPTH):
        wl = {k: v[l] for k, v in wfull.items()}
        sl = {k: small[k][l] for k in SMALL if k != "g_final"}
        x, sv = _layer_fwd(x, wl, sl, tabs, consts)
        saved.append(sv)
    loss, dx, dxb, dg_final = _loss_head(x, small["g_final"], target)
    grads = [None] * DEPTH
    packed = packed_shape
    for l in reversed(range(DEPTH)):
        wl = {k: v[l] for k, v in wfull.items()}
        sl = {k: small[k][l] for k in SMALL if k != "g_final"}
        dx, dxb, grads[l], packed = _layer_bwd(dx, dxb, saved[l], wl, sl, tabs, consts, packed, _packed_places(offs, l))
    return loss, dx, grads, dg_final, packed


ARRANGE = {"w_in": _arrange_w_in, "w_uq": _arrange_w_uq, "w_ukv": _arrange_w_ukv}


def kernel(x, g_mix, w_in, q_norm, w_uq, kv_norm, w_ukv, rpb, out_norm_a, out_norm_b, out_norm_c, w_out, g_mlp, w_mlp_in, w_mlp_out, g_final, loss_target, m_g_mix, m_w_in, m_q_norm, m_w_uq, m_kv_norm, m_w_ukv, m_rpb, m_out_norm_a, m_out_norm_b, m_out_norm_c, m_w_out, m_g_mlp, m_w_mlp_in, m_w_mlp_out, m_g_final, v_g_mix, v_w_in, v_q_norm, v_w_uq, v_kv_norm, v_w_ukv, v_rpb, v_out_norm_a, v_out_norm_b, v_out_norm_c, v_w_out, v_g_mlp, v_w_mlp_in, v_w_mlp_out, v_g_final):
    args = dict(locals())
    weights = {k: args[k] for k in BIG + SMALL}
    moms = {k: args["m_" + k] for k in BIG + SMALL}
    vels = {k: args["v_" + k] for k in BIG + SMALL}
    cc = lax.axis_index("c")
    my_chip = 2 * lax.axis_index("x") + lax.axis_index("y")

    shard_shapes = [weights[k].shape for k in BIG]
    packed_w = _pack_rows([weights[k].astype(BF16) for k in BIG])
    rows = packed_w.shape[0]
    my_half = lax.dynamic_index_in_dim(packed_w.reshape(2, rows // 2, PACK_C), cc, axis=0, keepdims=False)
    gathered = _gather_chips(my_half).reshape(4, rows, PACK_C)
    per_chip = [_unpack_rows(jnp.where(my_chip == j, packed_w, gathered[j]), shard_shapes) for j in range(4)]
    wfull = {}
    for idx, k in enumerate(BIG):
        full = _full_from_shards(k, jnp.stack([per_chip[j][idx] for j in range(4)]))
        wfull[k] = ARRANGE[k](full) if k in ARRANGE else full

    small = {k: weights[k] for k in SMALL}
    offs, row = {}, 0
    for k, shp in zip(BIG, shard_shapes):
        offs[k] = row
        row += int(np.prod(shp)) // PACK_C
    loss, dx, grads, dg_final, packed = _local_step(x[0], loss_target[0], wfull, small, _sds((4, rows, PACK_C), F32), offs)

    small_local = {k: jnp.stack([grads[l][k].reshape(weights[k].shape[1:]) for l in range(DEPTH)])
                   for k in SMALL if k != "g_final"}
    small_local["g_final"] = dg_final.reshape(-1)
    small_shapes = [weights[k].shape for k in SMALL]
    n_small = sum(int(np.prod(s)) for s in small_shapes)
    flat = jnp.concatenate([small_local[k].reshape(-1) for k in SMALL] + [loss[0, :1]])
    rows_small = -(-(n_small + 1) // PACK_C)
    rows_small += -rows_small % 8
    flat = jnp.pad(flat, (0, rows_small * PACK_C - n_small - 1)).reshape(rows_small, PACK_C)
    red = _all_reduce_small(flat).reshape(-1)
    loss_out = red[n_small]
    small_grads, off = {}, 0
    for k, shp in zip(SMALL, small_shapes):
        n = int(np.prod(shp))
        small_grads[k] = red[off:off + n].reshape(shp)
        off += n

    rest = [k for k in BIG if k not in DIRECT]
    by_shard = {k: _shards_from_full(k, jnp.stack([grads[l][k] for l in range(DEPTH)])) for k in rest}
    tail = jnp.stack([_pack_rows([by_shard[k][j] for k in rest]) for j in range(4)])
    assert offs[rest[0]] + tail.shape[1] == rows
    packed = lax.dynamic_update_slice(packed, tail, (0, offs[rest[0]], 0))
    halves = packed.reshape(4, 2, rows // 2, PACK_C)
    pair = _pair_sum(halves, cc, _swap_other_half(halves))
    by_chip = _scatter_chips(pair)
    reduced = _add_n([jnp.where(my_chip == j, pair[j], by_chip[j]) for j in range(4)], name="chip_sum", out_dtype=F32)
    theirs = _swap_sibling(reduced)
    joined = jnp.where(cc == 0, jnp.concatenate([reduced, theirs]), jnp.concatenate([theirs, reduced]))
    big_grads = dict(zip(BIG, _unpack_rows(joined, shard_shapes)))

    out_g, out_d, out_m, out_v = {}, {}, {}, {}
    for k in BIG:
        shp = weights[k].shape
        two_d = (shp[0] * shp[1], shp[2])
        d, nm, nv = _adamw(weights[k].reshape(two_d), big_grads[k].reshape(two_d), moms[k].reshape(two_d),
                           vels[k].reshape(two_d), name="adamw_" + k)
        out_g[k], out_d[k], out_m[k], out_v[k] = big_grads[k], d.reshape(shp), nm.reshape(shp), nv.reshape(shp)

    def pack_small(tree):
        f = jnp.concatenate([tree[k].reshape(-1) for k in SMALL])
        return jnp.pad(f, (0, rows_small * PACK_C - n_small)).reshape(rows_small, PACK_C)

    d, nm, nv = _adamw(pack_small(small), pack_small(small_grads), pack_small(moms), pack_small(vels), name="adamw_small")
    for tree, flat_out in ((out_d, d), (out_m, nm), (out_v, nv)):
        off = 0
        fo = flat_out.reshape(-1)
        for k, shp in zip(SMALL, small_shapes):
            n = int(np.prod(shp))
            tree[k] = fo[off:off + n].reshape(shp)
            off += n
    out_g.update(small_grads)

    order = ("g_mix", "w_in", "q_norm", "w_uq", "kv_norm", "w_ukv", "rpb", "out_norm_a", "out_norm_b", "out_norm_c", "w_out",
             "g_mlp", "w_mlp_in", "w_mlp_out", "g_final")
    return (loss_out, dx.reshape(x.shape), *[out_g[k] for k in order], *[out_d[k] for k in order],
            *[out_m[k] for k in order], *[out_v[k] for k in order])
```
